```python
import jax, jax.numpy as jnp
from jax import lax
import numpy as np

D_MODEL = 2048
BATCH = 8
SEQ = 4096
DEPTH = 2

N_A_LAYERS = DEPTH // 2
N_B_LAYERS = DEPTH - N_A_LAYERS
CONV_WIDTH = 31
N_HEADS = 16
HEAD_DIM = D_MODEL // N_HEADS
D_FF = -(-(8 * D_MODEL) // (3 * 256)) * 256
BRANCHES = ((128, 1), (512, 4), (2048, 16))
BLOCK = 128
RMS_EPS = 1e-6
LN_EPS = 1e-5

kernel_name = "yoco_conformer_dilated_alibi"


def _rmsnorm(x, g):
    xf = x.astype(jnp.float32)
    y = xf * lax.rsqrt(jnp.mean(xf * xf, axis=-1, keepdims=True) + RMS_EPS)
    return (y * g.astype(jnp.float32)).astype(x.dtype)


def _layernorm(x, g, b):
    xf = x.astype(jnp.float32)
    mu = jnp.mean(xf, axis=-1, keepdims=True)
    var = jnp.mean(jnp.square(xf - mu), axis=-1, keepdims=True)
    y = (xf - mu) * lax.rsqrt(var + LN_EPS) * g.astype(jnp.float32) + b.astype(jnp.float32)
    return y.astype(x.dtype)


def _swiglu(h, w_gate, w_up, w_down):
    return (jax.nn.silu(h @ w_gate) * (h @ w_up)) @ w_down


def _conformer_conv(h, w1, b1, dw, dw_b, ln_g, ln_b, w2, b2):
    u = h @ w1 + b1
    a, gate = jnp.split(u, 2, axis=-1)
    u = a * jax.nn.sigmoid(gate)
    u = lax.conv_general_dilated(
        u, dw[:, None, :].astype(u.dtype), window_strides=(1,),
        padding=[(CONV_WIDTH - 1, 0)],
        dimension_numbers=("NWC", "WIO", "NWC"),
        feature_group_count=u.shape[-1]) + dw_b
    u = jax.nn.silu(_layernorm(u, ln_g, ln_b))
    return u @ w2 + b2


def _alibi_slopes():
    h = jnp.arange(1, N_HEADS + 1, dtype=jnp.float32)
    return jnp.exp2(-8.0 * h / N_HEADS)


def _residue_major(t, d):
    b, s, h, e = t.shape
    L = s // d
    return t.reshape(b, L, d, h, e).transpose(0, 2, 1, 3, 4), L


def _shared_branch_kv(k, v):
    out = []
    for window, d in BRANCHES:
        n_off = window // d
        kr, L = _residue_major(k, d)
        vr, _ = _residue_major(v, d)
        nb = -(-L // BLOCK)
        pad = ((0, 0), (0, 0), (n_off, nb * BLOCK - L), (0, 0), (0, 0))
        idx = jnp.arange(nb)[:, None] * BLOCK + jnp.arange(BLOCK + n_off)[None, :]
        out.append((jnp.pad(kr, pad)[:, :, idx], jnp.pad(vr, pad)[:, :, idx]))
    return out


def _dilated_attention(q, branch_kv):
    b, s, h, e = q.shape
    slopes = _alibi_slopes()
    scale = HEAD_DIM ** -0.5
    outs, lses = [], []
    for (window, d), (k_win, v_win) in zip(BRANCHES, branch_kv):
        n_off = window // d
        qr, L = _residue_major(q, d)
        nb = -(-L // BLOCK)
        qr = jnp.pad(qr, ((0, 0), (0, 0), (0, nb * BLOCK - L), (0, 0), (0, 0)))
        qr = qr.reshape(b, d, nb, BLOCK, h, e)
        scores = jnp.einsum("brnqhe,brnkhe->brnhqk", qr, k_win,
                            preferred_element_type=jnp.float32) * scale
        qi = jnp.arange(BLOCK)[:, None]
        kk = jnp.arange(BLOCK + n_off)[None, :]
        j = qi - kk + n_off
        key_idx = jnp.arange(nb)[:, None, None] * BLOCK + kk[None] - n_off
        valid = (j >= 0) & (j <= n_off) & (key_idx >= 0)
        bias = -slopes[:, None, None] * (d * j).astype(jnp.float32)
        logits = jnp.where(valid[None, None, :, None], scores + bias[None, None, None], -jnp.inf)
        m = jnp.max(logits, axis=-1, keepdims=True)
        p = jnp.exp(logits - m)
        den = jnp.sum(p, axis=-1, keepdims=True)
        o = jnp.einsum("brnhqk,brnkhe->brnqhe", p / den, v_win.astype(jnp.float32))
        lse = (m + jnp.log(den))[..., 0].transpose(0, 1, 2, 4, 3)
        o = o.reshape(b, d, nb * BLOCK, h, e)[:, :, :L].transpose(0, 2, 1, 3, 4).reshape(b, s, h, e)
        lse = lse.reshape(b, d, nb * BLOCK, h)[:, :, :L].transpose(0, 2, 1, 3).reshape(b, s, h)
        outs.append(o)
        lses.append(lse)
    w = jax.nn.softmax(jnp.stack(lses, axis=0), axis=0)
    return jnp.sum(w[..., None] * jnp.stack(outs, axis=0), axis=0)


def _fwd_setup_inputs(seed: int = 0) -> dict:
    key = jax.random.key(seed)
    ks = jax.random.split(key, 24)
    D, F, W = D_MODEL, D_FF, CONV_WIDTH

    def dense(k, shape, fan_in):
        return jax.random.normal(k, shape, jnp.float32) * (fan_in ** -0.5)

    def gain(k, shape):
        return 1.0 + 0.02 * jax.random.normal(k, shape, jnp.float32)

    def bias(k, shape):
        return 0.02 * jax.random.normal(k, shape, jnp.float32)

    return {
        "x": jax.random.normal(ks[0], (BATCH, SEQ, D), jnp.float32),
        "a_norm_g": gain(ks[1], (N_A_LAYERS, D)),
        "conv_w1": dense(ks[2], (N_A_LAYERS, D, 2 * D), D),
        "conv_b1": bias(ks[3], (N_A_LAYERS, 2 * D)),
        "conv_dw": dense(ks[4], (N_A_LAYERS, W, D), W),
        "conv_dw_b": bias(ks[5], (N_A_LAYERS, D)),
        "conv_ln_g": gain(ks[6], (N_A_LAYERS, D)),
        "conv_ln_b": bias(ks[7], (N_A_LAYERS, D)),
        "conv_w2": dense(ks[8], (N_A_LAYERS, D, D), D),
        "conv_b2": bias(ks[9], (N_A_LAYERS, D)),
        "kv_norm_g": gain(ks[10], (D,)),
        "w_k": dense(ks[11], (D, D), D),
        "w_v": dense(ks[12], (D, D), D),
        "b_norm_g": gain(ks[13], (N_B_LAYERS, D)),
        "w_q": dense(ks[14], (N_B_LAYERS, D, D), D),
        "w_o": dense(ks[15], (N_B_LAYERS, D, D), D),
        "ffn_norm_g": gain(ks[16], (DEPTH, D)),
        "ffn_w_gate": dense(ks[17], (DEPTH, D, F), D),
        "ffn_w_up": dense(ks[18], (DEPTH, D, F), D),
        "ffn_w_down": dense(ks[19], (DEPTH, F, D), F),
        "final_norm_g": gain(ks[20], (D,)),
    }


def _fwd_reference(x, a_norm_g, conv_w1, conv_b1, conv_dw, conv_dw_b, conv_ln_g, conv_ln_b,
              conv_w2, conv_b2, kv_norm_g, w_k, w_v, b_norm_g, w_q, w_o,
              ffn_norm_g, ffn_w_gate, ffn_w_up, ffn_w_down, final_norm_g):
    b, s, _ = x.shape
    h = x
    shared_kv = None
    for layer in range(DEPTH):
        if layer < N_A_LAYERS:
            a = layer
            h = h + _conformer_conv(_rmsnorm(h, a_norm_g[a]), conv_w1[a], conv_b1[a],
                                    conv_dw[a], conv_dw_b[a], conv_ln_g[a], conv_ln_b[a],
                                    conv_w2[a], conv_b2[a])
        else:
            if layer == N_A_LAYERS:
                kv_in = _rmsnorm(h, kv_norm_g)
                k = (kv_in @ w_k).reshape(b, s, N_HEADS, HEAD_DIM)
                v = (kv_in @ w_v).reshape(b, s, N_HEADS, HEAD_DIM)
                shared_kv = _shared_branch_kv(k, v)
            i = layer - N_A_LAYERS
            q = (_rmsnorm(h, b_norm_g[i]) @ w_q[i]).reshape(b, s, N_HEADS, HEAD_DIM)
            att = _dilated_attention(q, shared_kv).astype(h.dtype).reshape(b, s, D_MODEL)
            h = h + att @ w_o[i]
        h = h + _swiglu(_rmsnorm(h, ffn_norm_g[layer]), ffn_w_gate[layer],
                        ffn_w_up[layer], ffn_w_down[layer])
    return _rmsnorm(h, final_norm_g)


import jax as _jax
import jax.numpy as _jnp

TWIN_FORMAT = 'train_step'
FWD_PARAMS = ['x', 'a_norm_g', 'conv_w1', 'conv_b1', 'conv_dw', 'conv_dw_b', 'conv_ln_g', 'conv_ln_b', 'conv_w2', 'conv_b2', 'kv_norm_g', 'w_k', 'w_v', 'b_norm_g', 'w_q', 'w_o', 'ffn_norm_g', 'ffn_w_gate', 'ffn_w_up', 'ffn_w_down', 'final_norm_g']
TWIN_WEIGHTS = ['a_norm_g', 'conv_w1', 'conv_b1', 'conv_dw', 'conv_dw_b', 'conv_ln_g', 'conv_ln_b', 'conv_w2', 'conv_b2', 'kv_norm_g', 'w_k', 'w_v', 'b_norm_g', 'w_q', 'w_o', 'ffn_norm_g', 'ffn_w_gate', 'ffn_w_up', 'ffn_w_down', 'final_norm_g']
TWIN_DIFF_INPUT = 'x'
TWIN_INPUTS = ['x', 'a_norm_g', 'conv_w1', 'conv_b1', 'conv_dw', 'conv_dw_b', 'conv_ln_g', 'conv_ln_b', 'conv_w2', 'conv_b2', 'kv_norm_g', 'w_k', 'w_v', 'b_norm_g', 'w_q', 'w_o', 'ffn_norm_g', 'ffn_w_gate', 'ffn_w_up', 'ffn_w_down', 'final_norm_g', 'loss_target', 'm_a_norm_g', 'm_conv_w1', 'm_conv_b1', 'm_conv_dw', 'm_conv_dw_b', 'm_conv_ln_g', 'm_conv_ln_b', 'm_conv_w2', 'm_conv_b2', 'm_kv_norm_g', 'm_w_k', 'm_w_v', 'm_b_norm_g', 'm_w_q', 'm_w_o', 'm_ffn_norm_g', 'm_ffn_w_gate', 'm_ffn_w_up', 'm_ffn_w_down', 'm_final_norm_g', 'v_a_norm_g', 'v_conv_w1', 'v_conv_b1', 'v_conv_dw', 'v_conv_dw_b', 'v_conv_ln_g', 'v_conv_ln_b', 'v_conv_w2', 'v_conv_b2', 'v_kv_norm_g', 'v_w_k', 'v_w_v', 'v_b_norm_g', 'v_w_q', 'v_w_o', 'v_ffn_norm_g', 'v_ffn_w_gate', 'v_ffn_w_up', 'v_ffn_w_down', 'v_final_norm_g']
TWIN_OUTPUTS = ['loss', 'grad_x', 'grad_a_norm_g', 'grad_conv_w1', 'grad_conv_b1', 'grad_conv_dw', 'grad_conv_dw_b', 'grad_conv_ln_g', 'grad_conv_ln_b', 'grad_conv_w2', 'grad_conv_b2', 'grad_kv_norm_g', 'grad_w_k', 'grad_w_v', 'grad_b_norm_g', 'grad_w_q', 'grad_w_o', 'grad_ffn_norm_g', 'grad_ffn_w_gate', 'grad_ffn_w_up', 'grad_ffn_w_down', 'grad_final_norm_g', 'delta_a_norm_g', 'delta_conv_w1', 'delta_conv_b1', 'delta_conv_dw', 'delta_conv_dw_b', 'delta_conv_ln_g', 'delta_conv_ln_b', 'delta_conv_w2', 'delta_conv_b2', 'delta_kv_norm_g', 'delta_w_k', 'delta_w_v', 'delta_b_norm_g', 'delta_w_q', 'delta_w_o', 'delta_ffn_norm_g', 'delta_ffn_w_gate', 'delta_ffn_w_up', 'delta_ffn_w_down', 'delta_final_norm_g', 'new_m_a_norm_g', 'new_m_conv_w1', 'new_m_conv_b1', 'new_m_conv_dw', 'new_m_conv_dw_b', 'new_m_conv_ln_g', 'new_m_conv_ln_b', 'new_m_conv_w2', 'new_m_conv_b2', 'new_m_kv_norm_g', 'new_m_w_k', 'new_m_w_v', 'new_m_b_norm_g', 'new_m_w_q', 'new_m_w_o', 'new_m_ffn_norm_g', 'new_m_ffn_w_gate', 'new_m_ffn_w_up', 'new_m_ffn_w_down', 'new_m_final_norm_g', 'new_v_a_norm_g', 'new_v_conv_w1', 'new_v_conv_b1', 'new_v_conv_dw', 'new_v_conv_dw_b', 'new_v_conv_ln_g', 'new_v_conv_ln_b', 'new_v_conv_w2', 'new_v_conv_b2', 'new_v_kv_norm_g', 'new_v_w_k', 'new_v_w_v', 'new_v_b_norm_g', 'new_v_w_q', 'new_v_w_o', 'new_v_ffn_norm_g', 'new_v_ffn_w_gate', 'new_v_ffn_w_up', 'new_v_ffn_w_down', 'new_v_final_norm_g']
TWIN_LEAF_KINDS = {'loss': 'loss', 'grad_x': 'grad_x', 'grad_a_norm_g': 'grad_w', 'grad_conv_w1': 'grad_w', 'grad_conv_b1': 'grad_w', 'grad_conv_dw': 'grad_w', 'grad_conv_dw_b': 'grad_w', 'grad_conv_ln_g': 'grad_w', 'grad_conv_ln_b': 'grad_w', 'grad_conv_w2': 'grad_w', 'grad_conv_b2': 'grad_w', 'grad_kv_norm_g': 'grad_w', 'grad_w_k': 'grad_w', 'grad_w_v': 'grad_w', 'grad_b_norm_g': 'grad_w', 'grad_w_q': 'grad_w', 'grad_w_o': 'grad_w', 'grad_ffn_norm_g': 'grad_w', 'grad_ffn_w_gate': 'grad_w', 'grad_ffn_w_up': 'grad_w', 'grad_ffn_w_down': 'grad_w', 'grad_final_norm_g': 'grad_w', 'delta_a_norm_g': 'delta_w', 'delta_conv_w1': 'delta_w', 'delta_conv_b1': 'delta_w', 'delta_conv_dw': 'delta_w', 'delta_conv_dw_b': 'delta_w', 'delta_conv_ln_g': 'delta_w', 'delta_conv_ln_b': 'delta_w', 'delta_conv_w2': 'delta_w', 'delta_conv_b2': 'delta_w', 'delta_kv_norm_g': 'delta_w', 'delta_w_k': 'delta_w', 'delta_w_v': 'delta_w', 'delta_b_norm_g': 'delta_w', 'delta_w_q': 'delta_w', 'delta_w_o': 'delta_w', 'delta_ffn_norm_g': 'delta_w', 'delta_ffn_w_gate': 'delta_w', 'delta_ffn_w_up': 'delta_w', 'delta_ffn_w_down': 'delta_w', 'delta_final_norm_g': 'delta_w', 'new_m_a_norm_g': 'new_m', 'new_m_conv_w1': 'new_m', 'new_m_conv_b1': 'new_m', 'new_m_conv_dw': 'new_m', 'new_m_conv_dw_b': 'new_m', 'new_m_conv_ln_g': 'new_m', 'new_m_conv_ln_b': 'new_m', 'new_m_conv_w2': 'new_m', 'new_m_conv_b2': 'new_m', 'new_m_kv_norm_g': 'new_m', 'new_m_w_k': 'new_m', 'new_m_w_v': 'new_m', 'new_m_b_norm_g': 'new_m', 'new_m_w_q': 'new_m', 'new_m_w_o': 'new_m', 'new_m_ffn_norm_g': 'new_m', 'new_m_ffn_w_gate': 'new_m', 'new_m_ffn_w_up': 'new_m', 'new_m_ffn_w_down': 'new_m', 'new_m_final_norm_g': 'new_m', 'new_v_a_norm_g': 'new_v', 'new_v_conv_w1': 'new_v', 'new_v_conv_b1': 'new_v', 'new_v_conv_dw': 'new_v', 'new_v_conv_dw_b': 'new_v', 'new_v_conv_ln_g': 'new_v', 'new_v_conv_ln_b': 'new_v', 'new_v_conv_w2': 'new_v', 'new_v_conv_b2': 'new_v', 'new_v_kv_norm_g': 'new_v', 'new_v_w_k': 'new_v', 'new_v_w_v': 'new_v', 'new_v_b_norm_g': 'new_v', 'new_v_w_q': 'new_v', 'new_v_w_o': 'new_v', 'new_v_ffn_norm_g': 'new_v', 'new_v_ffn_w_gate': 'new_v', 'new_v_ffn_w_up': 'new_v', 'new_v_ffn_w_down': 'new_v', 'new_v_final_norm_g': 'new_v'}


def _forward(args):
    return _fwd_reference(*[args[k] for k in FWD_PARAMS])


def _output_shape():
    def fwd():
        inp = _fwd_setup_inputs(0)
        return _fwd_reference(*[inp[k] for k in FWD_PARAMS])
    out = _jax.eval_shape(fwd)
    return out.shape, out.dtype

N_MICROBATCH = 1
ADAM_LR = 0.001
ADAM_B1 = 0.9
ADAM_B2 = 0.999
ADAM_EPS = 1e-08
ADAM_WD = 0.01
ADAM_STEP = 10
PER_EXAMPLE_BATCH_AXIS = {'x': 0, 'loss_target': 0}
SHARED_INPUTS = []
_WEIGHT_DTYPES = {'a_norm_g': _jnp.float32, 'conv_w1': _jnp.float32, 'conv_b1': _jnp.float32, 'conv_dw': _jnp.float32, 'conv_dw_b': _jnp.float32, 'conv_ln_g': _jnp.float32, 'conv_ln_b': _jnp.float32, 'conv_w2': _jnp.float32, 'conv_b2': _jnp.float32, 'kv_norm_g': _jnp.float32, 'w_k': _jnp.float32, 'w_v': _jnp.float32, 'b_norm_g': _jnp.float32, 'w_q': _jnp.float32, 'w_o': _jnp.float32, 'ffn_norm_g': _jnp.float32, 'ffn_w_gate': _jnp.float32, 'ffn_w_up': _jnp.float32, 'ffn_w_down': _jnp.float32, 'final_norm_g': _jnp.float32}
MOMENT_SCALE = {'a_norm_g': 6.303025e-02, 'conv_w1': 4.401345e-02, 'conv_b1': 6.091560e-02, 'conv_dw': 5.801534e-02, 'conv_dw_b': 1.380744e-01, 'conv_ln_g': 7.766834e-02, 'conv_ln_b': 7.922752e-02, 'conv_w2': 5.922504e-02, 'conv_b2': 1.343476e-01, 'kv_norm_g': 3.479792e-02, 'w_k': 1.767219e-02, 'w_v': 3.023055e-02, 'b_norm_g': 1.785688e-02, 'w_q': 1.767900e-02, 'w_o': 3.017582e-02, 'ffn_norm_g': 5.797368e-02, 'ffn_w_gate': 2.512971e-02, 'ffn_w_up': 2.443916e-02, 'ffn_w_down': 4.056943e-02, 'final_norm_g': 1.602637e+01}


def _to_microbatches(a, axis):
    t = _jnp.moveaxis(a, axis, 0)
    t = t.reshape((N_MICROBATCH, t.shape[0] // N_MICROBATCH) + t.shape[1:])
    return _jnp.moveaxis(t, 1, axis + 1)


def setup_inputs(seed: int = 0) -> dict:
    inp = _fwd_setup_inputs(seed)
    key = _jax.random.fold_in(_jax.random.key(seed), 7919)
    shape, _ = _output_shape()
    out = dict(inp)
    out["loss_target"] = _jax.random.normal(_jax.random.fold_in(key, 0), shape, _jnp.float32)
    for i, name in enumerate(TWIN_WEIGHTS):
        w = inp[name].astype(_jnp.float32)
        if MOMENT_SCALE is None:
            s = _jnp.sqrt(_jnp.mean(_jnp.square(w)) + 1e-30)
        else:
            s = MOMENT_SCALE[name]
        km, kv = _jax.random.split(_jax.random.fold_in(key, i + 1))
        out[name] = w
        out["m_" + name] = s * _jax.random.normal(km, w.shape, _jnp.float32)
        out["v_" + name] = (s * s) * _jax.random.uniform(kv, w.shape, _jnp.float32, 0.5, 1.5)
    if N_MICROBATCH > 1:
        for name, axis in PER_EXAMPLE_BATCH_AXIS.items():
            out[name] = _to_microbatches(out[name], axis)
    return {'x': out['x'], 'a_norm_g': out['a_norm_g'], 'conv_w1': out['conv_w1'], 'conv_b1': out['conv_b1'], 'conv_dw': out['conv_dw'], 'conv_dw_b': out['conv_dw_b'], 'conv_ln_g': out['conv_ln_g'], 'conv_ln_b': out['conv_ln_b'], 'conv_w2': out['conv_w2'], 'conv_b2': out['conv_b2'], 'kv_norm_g': out['kv_norm_g'], 'w_k': out['w_k'], 'w_v': out['w_v'], 'b_norm_g': out['b_norm_g'], 'w_q': out['w_q'], 'w_o': out['w_o'], 'ffn_norm_g': out['ffn_norm_g'], 'ffn_w_gate': out['ffn_w_gate'], 'ffn_w_up': out['ffn_w_up'], 'ffn_w_down': out['ffn_w_down'], 'final_norm_g': out['final_norm_g'], 'loss_target': out['loss_target'], 'm_a_norm_g': out['m_a_norm_g'], 'm_conv_w1': out['m_conv_w1'], 'm_conv_b1': out['m_conv_b1'], 'm_conv_dw': out['m_conv_dw'], 'm_conv_dw_b': out['m_conv_dw_b'], 'm_conv_ln_g': out['m_conv_ln_g'], 'm_conv_ln_b': out['m_conv_ln_b'], 'm_conv_w2': out['m_conv_w2'], 'm_conv_b2': out['m_conv_b2'], 'm_kv_norm_g': out['m_kv_norm_g'], 'm_w_k': out['m_w_k'], 'm_w_v': out['m_w_v'], 'm_b_norm_g': out['m_b_norm_g'], 'm_w_q': out['m_w_q'], 'm_w_o': out['m_w_o'], 'm_ffn_norm_g': out['m_ffn_norm_g'], 'm_ffn_w_gate': out['m_ffn_w_gate'], 'm_ffn_w_up': out['m_ffn_w_up'], 'm_ffn_w_down': out['m_ffn_w_down'], 'm_final_norm_g': out['m_final_norm_g'], 'v_a_norm_g': out['v_a_norm_g'], 'v_conv_w1': out['v_conv_w1'], 'v_conv_b1': out['v_conv_b1'], 'v_conv_dw': out['v_conv_dw'], 'v_conv_dw_b': out['v_conv_dw_b'], 'v_conv_ln_g': out['v_conv_ln_g'], 'v_conv_ln_b': out['v_conv_ln_b'], 'v_conv_w2': out['v_conv_w2'], 'v_conv_b2': out['v_conv_b2'], 'v_kv_norm_g': out['v_kv_norm_g'], 'v_w_k': out['v_w_k'], 'v_w_v': out['v_w_v'], 'v_b_norm_g': out['v_b_norm_g'], 'v_w_q': out['v_w_q'], 'v_w_o': out['v_w_o'], 'v_ffn_norm_g': out['v_ffn_norm_g'], 'v_ffn_w_gate': out['v_ffn_w_gate'], 'v_ffn_w_up': out['v_ffn_w_up'], 'v_ffn_w_down': out['v_ffn_w_down'], 'v_final_norm_g': out['v_final_norm_g']}


def _loss(weights, diff, rest, loss_target):
    with _jax.named_scope("forward"):
        args = {**rest, TWIN_DIFF_INPUT: diff, **{k: w.astype(_WEIGHT_DTYPES[k]) for k, w in weights.items()}}
        y = _forward(args)
    with _jax.named_scope("loss_head"):
        err = _jnp.square(y.astype(_jnp.float32) - loss_target)
        return 0.5 * _jnp.sum(_jnp.mean(err, axis=-1)) if err.ndim else 0.5 * err


def _adamw(w, g, m, v):
    m = ADAM_B1 * m + (1.0 - ADAM_B1) * g
    v = ADAM_B2 * v + (1.0 - ADAM_B2) * _jnp.square(g)
    m_hat = m / (1.0 - ADAM_B1 ** ADAM_STEP)
    v_hat = v / (1.0 - ADAM_B2 ** ADAM_STEP)
    delta = -ADAM_LR * (m_hat / (_jnp.sqrt(v_hat) + ADAM_EPS) + ADAM_WD * w)
    return delta, m, v


def reference(x, a_norm_g, conv_w1, conv_b1, conv_dw, conv_dw_b, conv_ln_g, conv_ln_b, conv_w2, conv_b2, kv_norm_g, w_k, w_v, b_norm_g, w_q, w_o, ffn_norm_g, ffn_w_gate, ffn_w_up, ffn_w_down, final_norm_g, loss_target, m_a_norm_g, m_conv_w1, m_conv_b1, m_conv_dw, m_conv_dw_b, m_conv_ln_g, m_conv_ln_b, m_conv_w2, m_conv_b2, m_kv_norm_g, m_w_k, m_w_v, m_b_norm_g, m_w_q, m_w_o, m_ffn_norm_g, m_ffn_w_gate, m_ffn_w_up, m_ffn_w_down, m_final_norm_g, v_a_norm_g, v_conv_w1, v_conv_b1, v_conv_dw, v_conv_dw_b, v_conv_ln_g, v_conv_ln_b, v_conv_w2, v_conv_b2, v_kv_norm_g, v_w_k, v_w_v, v_b_norm_g, v_w_q, v_w_o, v_ffn_norm_g, v_ffn_w_gate, v_ffn_w_up, v_ffn_w_down, v_final_norm_g):
    given = dict(x=x, a_norm_g=a_norm_g, conv_w1=conv_w1, conv_b1=conv_b1, conv_dw=conv_dw, conv_dw_b=conv_dw_b, conv_ln_g=conv_ln_g, conv_ln_b=conv_ln_b, conv_w2=conv_w2, conv_b2=conv_b2, kv_norm_g=kv_norm_g, w_k=w_k, w_v=w_v, b_norm_g=b_norm_g, w_q=w_q, w_o=w_o, ffn_norm_g=ffn_norm_g, ffn_w_gate=ffn_w_gate, ffn_w_up=ffn_w_up, ffn_w_down=ffn_w_down, final_norm_g=final_norm_g, loss_target=loss_target, m_a_norm_g=m_a_norm_g, m_conv_w1=m_conv_w1, m_conv_b1=m_conv_b1, m_conv_dw=m_conv_dw, m_conv_dw_b=m_conv_dw_b, m_conv_ln_g=m_conv_ln_g, m_conv_ln_b=m_conv_ln_b, m_conv_w2=m_conv_w2, m_conv_b2=m_conv_b2, m_kv_norm_g=m_kv_norm_g, m_w_k=m_w_k, m_w_v=m_w_v, m_b_norm_g=m_b_norm_g, m_w_q=m_w_q, m_w_o=m_w_o, m_ffn_norm_g=m_ffn_norm_g, m_ffn_w_gate=m_ffn_w_gate, m_ffn_w_up=m_ffn_w_up, m_ffn_w_down=m_ffn_w_down, m_final_norm_g=m_final_norm_g, v_a_norm_g=v_a_norm_g, v_conv_w1=v_conv_w1, v_conv_b1=v_conv_b1, v_conv_dw=v_conv_dw, v_conv_dw_b=v_conv_dw_b, v_conv_ln_g=v_conv_ln_g, v_conv_ln_b=v_conv_ln_b, v_conv_w2=v_conv_w2, v_conv_b2=v_conv_b2, v_kv_norm_g=v_kv_norm_g, v_w_k=v_w_k, v_w_v=v_w_v, v_b_norm_g=v_b_norm_g, v_w_q=v_w_q, v_w_o=v_w_o, v_ffn_norm_g=v_ffn_norm_g, v_ffn_w_gate=v_ffn_w_gate, v_ffn_w_up=v_ffn_w_up, v_ffn_w_down=v_ffn_w_down, v_final_norm_g=v_final_norm_g)
    weights = {n: given[n] for n in TWIN_WEIGHTS}
    shared = {n: given[n] for n in SHARED_INPUTS}
    per_example = {n: given[n] for n in ['x']}
    grad_fn = _jax.value_and_grad(_loss, argnums=(0, 1))

    def one_microbatch(ex, loss_target):
        ex = dict(ex)
        diff = ex.pop(TWIN_DIFF_INPUT)
        return grad_fn(weights, diff, {**shared, **ex}, loss_target)

    if N_MICROBATCH == 1:
        loss, (grad_w, grad_x) = one_microbatch(per_example, given["loss_target"])
    else:
        def body(carry, xs):
            loss_sum, grad_sum = carry
            l_k, (gw_k, gx_k) = one_microbatch(xs[0], xs[1])
            with _jax.named_scope("update"):
                return (loss_sum + l_k, _jax.tree.map(_jnp.add, grad_sum, gw_k)), gx_k

        init = (_jnp.zeros((), _jnp.float32), _jax.tree.map(_jnp.zeros_like, weights))
        (loss, grad_w), grad_x = _jax.lax.scan(body, init, (per_example, given["loss_target"]))
    with _jax.named_scope("update"):
        delta_w, new_m, new_v = {}, {}, {}
        for n in TWIN_WEIGHTS:
            delta_w[n], new_m[n], new_v[n] = _adamw(weights[n], grad_w[n], given["m_" + n], given["v_" + n])
    return (loss, grad_x, *[grad_w[n] for n in TWIN_WEIGHTS], *[delta_w[n] for n in TWIN_WEIGHTS],
            *[new_m[n] for n in TWIN_WEIGHTS], *[new_v[n] for n in TWIN_WEIGHTS])
```

```python
import functools
import math

import jax
import jax.numpy as jnp
from jax import lax
from jax.experimental import pallas as pl
from jax.experimental.pallas import tpu as pltpu

F32 = jnp.float32
BF16 = jnp.bfloat16

HEAD_DIM = 128
BRANCHES = ((128, 1), (512, 4), (2048, 16))
CONV_WIDTH = 31
CONV_HALO = 32
RMS_EPS = 1e-6
LN_EPS = 1e-5
ADAM_LR = 0.001
ADAM_B1 = 0.9
ADAM_B2 = 0.999
ADAM_EPS = 1e-08
ADAM_WD = 0.01
ADAM_STEP = 10
N_CHIPS = 4
N_DEV = 8
MESH = pl.DeviceIdType.MESH


def _sigmoid(x):
    return 1.0 / (1.0 + jnp.exp(-x))


def _row_tile(rows, want):
    t = min(rows, want)
    assert rows % t == 0, (rows, want)
    return t


_DOT_DIMS = {"nn": ((1,), (0,)), "nt": ((1,), (1,)), "tn": ((0,), (0,))}


def _mm(name, mode, a, bs, epilogue, outs, *, m, n, k, extras=(), bm=512, bn=512, bk=None):
    bm, bn = min(bm, m), min(bn, n)
    bk = k if bk is None else min(bk, k)
    assert m % bm == 0 and n % bn == 0 and k % bk == 0, (name, m, n, k, bm, bn, bk)
    nk = k // bk
    nb, ne, no = len(bs), len(extras), len(outs)

    if mode == "tn":
        a_spec = pl.BlockSpec((bk, bm), lambda i, j, kk: (kk, i))
    else:
        a_spec = pl.BlockSpec((bm, bk), lambda i, j, kk: (i, kk))

    def b_spec(lead, off):
        if mode == "nt":
            blk, idx = (bn, bk), (lambda i, j, kk: (j + off, kk))
        else:
            blk, idx = (bk, bn), (lambda i, j, kk: (kk, j + off))
        if lead is None:
            return pl.BlockSpec(blk, idx)
        return pl.BlockSpec((None,) + blk, lambda i, j, kk: (lead,) + idx(i, j, kk))

    def e_spec(kind, off):
        if kind == "mn":
            return pl.BlockSpec((bm, bn), lambda i, j, kk: (i, j + off))
        return pl.BlockSpec((1, bn), lambda i, j, kk: (0, j + off))

    in_specs = [a_spec] + [b_spec(l, o) for _, l, o in bs] + [e_spec(kd, o) for _, kd, o in extras]
    out_specs = [pl.BlockSpec((bm, bn), lambda i, j, kk: (i, j)) for _ in outs]
    out_shape = [jax.ShapeDtypeStruct((m, n), dt) for (dt,) in outs]
    dims = (_DOT_DIMS[mode], ((), ()))

    def body(*refs):
        a_ref = refs[0]
        b_refs = refs[1:1 + nb]
        e_refs = refs[1 + nb:1 + nb + ne]
        o_refs = refs[1 + nb + ne:1 + nb + ne + no]
        acc_refs = refs[1 + nb + ne + no:]
        av = a_ref[...].astype(BF16)
        prods = [lax.dot_general(av, b_ref[...].astype(BF16), dims, preferred_element_type=F32) for b_ref in b_refs]

        def finish(accs):
            res = epilogue(accs, [e_ref[...] for e_ref in e_refs])
            for o_ref, r in zip(o_refs, res):
                o_ref[...] = r.astype(o_ref.dtype)

        if nk == 1:
            finish(prods)
        else:
            kk = pl.program_id(2)

            @pl.when(kk == 0)
            def _():
                for acc_ref, p in zip(acc_refs, prods):
                    acc_ref[...] = p

            @pl.when(kk > 0)
            def _():
                for acc_ref, p in zip(acc_refs, prods):
                    acc_ref[...] += p

            @pl.when(kk == nk - 1)
            def _():
                finish([acc_ref[...] for acc_ref in acc_refs])

    scratch = [] if nk == 1 else [pltpu.VMEM((bm, bn), F32) for _ in bs]
    res = pl.pallas_call(
        body,
        name=name,
        grid=(m // bm, n // bn, nk),
        in_specs=in_specs,
        out_specs=out_specs,
        out_shape=out_shape,
        scratch_shapes=scratch,
        compiler_params=pltpu.CompilerParams(dimension_semantics=("parallel", "parallel", "arbitrary")),
    )(a, *[b for b, _, _ in bs], *[e for e, _, _ in extras])
    return res


def _rms_fwd(name, x, gains):
    s, d = x.shape
    ng = gains.shape[0]
    t = _row_tile(s, 256)

    def body(x_ref, g_ref, *o_refs):
        xv = x_ref[...]
        r = lax.rsqrt(jnp.mean(xv * xv, axis=-1, keepdims=True) + RMS_EPS)
        xh = xv * r
        for gi, o_ref in enumerate(o_refs):
            o_ref[...] = (xh * g_ref[gi:gi + 1, :]).astype(o_ref.dtype)

    return pl.pallas_call(
        body,
        name=name,
        grid=(s // t,),
        in_specs=[pl.BlockSpec((t, d), lambda i: (i, 0)), pl.BlockSpec((ng, d), lambda i: (0, 0))],
        out_specs=[pl.BlockSpec((t, d), lambda i: (i, 0)) for _ in range(ng)],
        out_shape=[jax.ShapeDtypeStruct((s, d), BF16) for _ in range(ng)],
        compiler_params=pltpu.CompilerParams(dimension_semantics=("parallel",)),
    )(x, gains)


def _rms_bwd(name, x, gains, dns, dres):
    s, d = x.shape
    ng = gains.shape[0]
    t = _row_tile(s, 256)

    def body(x_ref, g_ref, dres_ref, *refs):
        dn_refs = refs[:ng]
        dx_ref, dxb_ref, dg_ref, cs_ref = refs[ng:]
        i = pl.program_id(0)
        xv = x_ref[...]
        r = lax.rsqrt(jnp.mean(xv * xv, axis=-1, keepdims=True) + RMS_EPS)
        xh = xv * r
        dx = dres_ref[...]
        dgs = []
        for gi in range(ng):
            dn = dn_refs[gi][...].astype(F32)
            dxh = dn * g_ref[gi:gi + 1, :]
            dgs.append(jnp.sum(dn * xh, axis=0, keepdims=True))
            dx = dx + r * (dxh - xh * jnp.mean(dxh * xh, axis=-1, keepdims=True))
        dx_ref[...] = dx
        dxb_ref[...] = dx.astype(BF16)
        dg = jnp.concatenate(dgs, axis=0) if ng > 1 else dgs[0]
        cs = jnp.sum(dx, axis=0, keepdims=True)

        @pl.when(i == 0)
        def _():
            dg_ref[...] = dg
            cs_ref[...] = cs

        @pl.when(i > 0)
        def _():
            dg_ref[...] += dg
            cs_ref[...] += cs

    row = pl.BlockSpec((t, d), lambda i: (i, 0))
    return pl.pallas_call(
        body,
        name=name,
        grid=(s // t,),
        in_specs=[row, pl.BlockSpec((ng, d), lambda i: (0, 0)), row] + [row] * ng,
        out_specs=[row, row, pl.BlockSpec((ng, d), lambda i: (0, 0)), pl.BlockSpec((1, d), lambda i: (0, 0))],
        out_shape=[
            jax.ShapeDtypeStruct((s, d), F32),
            jax.ShapeDtypeStruct((s, d), BF16),
            jax.ShapeDtypeStruct((ng, d), F32),
            jax.ShapeDtypeStruct((1, d), F32),
        ],
        compiler_params=pltpu.CompilerParams(dimension_semantics=("arbitrary",)),
    )(x, gains, dres, *dns)


def _final_loss(name, h, gain, target):
    s, d = h.shape
    t = _row_tile(s, 256)

    def body(h_ref, g_ref, t_ref, dh_ref, dhb_ref, dg_ref, ls_ref):
        i = pl.program_id(0)
        xv = h_ref[...]
        g = g_ref[...]
        r = lax.rsqrt(jnp.mean(xv * xv, axis=-1, keepdims=True) + RMS_EPS)
        xh = xv * r
        err = xh * g - t_ref[...]
        ls = jnp.sum(err * err, axis=0, keepdims=True) * (0.5 / d)
        dy = err * (1.0 / d)
        dxh = dy * g
        dg = jnp.sum(dy * xh, axis=0, keepdims=True)
        dx = r * (dxh - xh * jnp.mean(dxh * xh, axis=-1, keepdims=True))
        dh_ref[...] = dx
        dhb_ref[...] = dx.astype(BF16)

        @pl.when(i == 0)
        def _():
            dg_ref[...] = dg
            ls_ref[...] = ls

        @pl.when(i > 0)
        def _():
            dg_ref[...] += dg
            ls_ref[...] += ls

    row = pl.BlockSpec((t, d), lambda i: (i, 0))
    vec = pl.BlockSpec((1, d), lambda i: (0, 0))
    return pl.pallas_call(
        body,
        name=name,
        grid=(s // t,),
        in_specs=[row, vec, row],
        out_specs=[row, row, vec, vec],
        out_shape=[
            jax.ShapeDtypeStruct((s, d), F32),
            jax.ShapeDtypeStruct((s, d), BF16),
            jax.ShapeDtypeStruct((1, d), F32),
            jax.ShapeDtypeStruct((1, d), F32),
        ],
        compiler_params=pltpu.CompilerParams(dimension_semantics=("arbitrary",)),
    )(h, gain, target)


def _conv_tiles(s):
    t = _row_tile(s, 128)
    assert t % CONV_HALO == 0
    return t, t // CONV_HALO


def _dwconv_fwd(name, u, dw, dw_b, ln_g, ln_b):
    s, d = u.shape
    t, hb = _conv_tiles(s)
    w = dw.shape[0]
    lo = CONV_HALO - (w - 1)

    def body(cur_ref, prev_ref, dw_ref, dwb_ref, lg_ref, lb_ref, c_ref, sw_ref, cat_ref):
        i = pl.program_id(0)
        cat_ref[CONV_HALO:, :] = cur_ref[...]

        @pl.when(i == 0)
        def _():
            cat_ref[:CONV_HALO, :] = jnp.zeros((CONV_HALO, d), F32)

        @pl.when(i > 0)
        def _():
            cat_ref[:CONV_HALO, :] = prev_ref[...]

        acc = jnp.zeros((t, d), F32) + dwb_ref[...]
        for kk in range(w):
            acc = acc + dw_ref[kk:kk + 1, :] * cat_ref[lo + kk:lo + kk + t, :]
        c_ref[...] = acc
        mu = jnp.mean(acc, axis=-1, keepdims=True)
        cc = acc - mu
        var = jnp.mean(cc * cc, axis=-1, keepdims=True)
        ln = cc * lax.rsqrt(var + LN_EPS) * lg_ref[...] + lb_ref[...]
        sw_ref[...] = (ln * _sigmoid(ln)).astype(BF16)

    row = pl.BlockSpec((t, d), lambda i: (i, 0))
    prev = pl.BlockSpec((CONV_HALO, d), lambda i: (jnp.maximum(i * hb - 1, 0), 0))
    vec = pl.BlockSpec((1, d), lambda i: (0, 0))
    return pl.pallas_call(
        body,
        name=name,
        grid=(s // t,),
        in_specs=[row, prev, pl.BlockSpec((w, d), lambda i: (0, 0)), vec, vec, vec],
        out_specs=[row, row],
        out_shape=[jax.ShapeDtypeStruct((s, d), F32), jax.ShapeDtypeStruct((s, d), BF16)],
        scratch_shapes=[pltpu.VMEM((CONV_HALO + t, d), F32)],
        compiler_params=pltpu.CompilerParams(dimension_semantics=("parallel",)),
    )(u, u, dw, dw_b, ln_g, ln_b)


def _conv_ln_bwd(name, c, dsw, ln_g, ln_b):
    s, d = c.shape
    t = _row_tile(s, 256)

    def body(c_ref, dsw_ref, lg_ref, lb_ref, dc_ref, sums_ref):
        i = pl.program_id(0)
        cv = c_ref[...]
        g = lg_ref[...]
        mu = jnp.mean(cv, axis=-1, keepdims=True)
        cc = cv - mu
        rstd = lax.rsqrt(jnp.mean(cc * cc, axis=-1, keepdims=True) + LN_EPS)
        ch = cc * rstd
        ln = ch * g + lb_ref[...]
        sg = _sigmoid(ln)
        dln = dsw_ref[...] * (sg * (1.0 + ln * (1.0 - sg)))
        dch = dln * g
        dc = rstd * (dch - jnp.mean(dch, axis=-1, keepdims=True) - ch * jnp.mean(dch * ch, axis=-1, keepdims=True))
        dc_ref[...] = dc
        sums = jnp.concatenate(
            [
                jnp.sum(dln * ch, axis=0, keepdims=True),
                jnp.sum(dln, axis=0, keepdims=True),
                jnp.sum(dc, axis=0, keepdims=True),
                jnp.zeros((1, d), F32),
            ],
            axis=0,
        )

        @pl.when(i == 0)
        def _():
            sums_ref[...] = sums

        @pl.when(i > 0)
        def _():
            sums_ref[...] += sums

    row = pl.BlockSpec((t, d), lambda i: (i, 0))
    vec = pl.BlockSpec((1, d), lambda i: (0, 0))
    return pl.pallas_call(
        body,
        name=name,
        grid=(s // t,),
        in_specs=[row, row, vec, vec],
        out_specs=[row, pl.BlockSpec((4, d), lambda i: (0, 0))],
        out_shape=[jax.ShapeDtypeStruct((s, d), F32), jax.ShapeDtypeStruct((4, d), F32)],
        compiler_params=pltpu.CompilerParams(dimension_semantics=("arbitrary",)),
    )(c, dsw, ln_g, ln_b)


def _conv_dw_bwd(name, dc, u, a, gt, dw):
    s, d = dc.shape
    t, hb = _conv_tiles(s)
    w = dw.shape[0]
    lo = CONV_HALO - (w - 1)
    nt = s // t

    def body(dc_ref, dcn_ref, u_ref, up_ref, a_ref, gt_ref, dw_ref, dpre_ref, ddw_ref, db_ref, dcat_ref, ucat_ref):
        i = pl.program_id(0)
        dcat_ref[:t, :] = dc_ref[...]
        ucat_ref[CONV_HALO:, :] = u_ref[...]

        @pl.when(i == nt - 1)
        def _():
            dcat_ref[t:, :] = jnp.zeros((CONV_HALO, d), F32)

        @pl.when(i < nt - 1)
        def _():
            dcat_ref[t:, :] = dcn_ref[...]

        @pl.when(i == 0)
        def _():
            ucat_ref[:CONV_HALO, :] = jnp.zeros((CONV_HALO, d), F32)

        @pl.when(i > 0)
        def _():
            ucat_ref[:CONV_HALO, :] = up_ref[...]

        dcv = dc_ref[...]
        du = jnp.zeros((t, d), F32)
        rows = []
        for kk in range(w):
            sh = w - 1 - kk
            du = du + dw_ref[kk:kk + 1, :] * dcat_ref[sh:sh + t, :]
            rows.append(jnp.sum(dcv * ucat_ref[lo + kk:lo + kk + t, :], axis=0, keepdims=True))
        ddw = jnp.concatenate(rows, axis=0)
        av = a_ref[...].astype(F32)
        sg = _sigmoid(gt_ref[...].astype(F32))
        da = du * sg
        dgt = du * av * sg * (1.0 - sg)
        dpre_ref[:, :d] = da.astype(BF16)
        dpre_ref[:, d:] = dgt.astype(BF16)
        db = jnp.concatenate([jnp.sum(da, axis=0, keepdims=True), jnp.sum(dgt, axis=0, keepdims=True)], axis=1)

        @pl.when(i == 0)
        def _():
            ddw_ref[...] = ddw
            db_ref[...] = db

        @pl.when(i > 0)
        def _():
            ddw_ref[...] += ddw
            db_ref[...] += db

    row = pl.BlockSpec((t, d), lambda i: (i, 0))
    nxt = pl.BlockSpec((CONV_HALO, d), lambda i: (jnp.minimum((i + 1) * hb, s // CONV_HALO - 1), 0))
    prev = pl.BlockSpec((CONV_HALO, d), lambda i: (jnp.maximum(i * hb - 1, 0), 0))
    return pl.pallas_call(
        body,
        name=name,
        grid=(nt,),
        in_specs=[row, nxt, row, prev, row, row, pl.BlockSpec((w, d), lambda i: (0, 0))],
        out_specs=[
            pl.BlockSpec((t, 2 * d), lambda i: (i, 0)),
            pl.BlockSpec((w, d), lambda i: (0, 0)),
            pl.BlockSpec((1, 2 * d), lambda i: (0, 0)),
        ],
        out_shape=[
            jax.ShapeDtypeStruct((s, 2 * d), BF16),
            jax.ShapeDtypeStruct((w, d), F32),
            jax.ShapeDtypeStruct((1, 2 * d), F32),
        ],
        scratch_shapes=[pltpu.VMEM((t + CONV_HALO, d), F32), pltpu.VMEM((CONV_HALO + t, d), F32)],
        compiler_params=pltpu.CompilerParams(dimension_semantics=("arbitrary",)),
    )(dc, dc, u, u, a, gt, dw)


def _alibi_slopes(n_heads):
    h = jnp.arange(1, n_heads + 1, dtype=F32)
    return jnp.exp2(-8.0 * h / n_heads)


def _band_masks(bq):
    qi = lax.broadcasted_iota(jnp.int32, (bq, bq), 0)
    kj = lax.broadcasted_iota(jnp.int32, (bq, bq), 1)
    return qi - kj, qi - kj + bq


def _attn_fwd_branch(name, q, k, v, slopes, dil, bq, acc):
    s, dm = q.shape
    nh = dm // HEAD_DIM
    ll = s // dil
    assert ll % bq == 0
    nblk = ll // bq
    scale = HEAD_DIM ** -0.5
    view = lambda z: z.reshape(ll, dil * dm)
    first = acc is None

    def body(sl_ref, q_ref, kc_ref, kp_ref, vc_ref, vp_ref, *refs):
        if first:
            o_ref, l_ref = refs
        else:
            oa_ref, la_ref, o_ref, l_ref = refs
        nb = pl.program_id(1)
        jc, jp = _band_masks(bq)
        ok_c = jc >= 0
        ok_p = (jp <= bq) & (nb > 0)
        jcf = jc.astype(F32) * float(dil)
        jpf = jp.astype(F32) * float(dil)

        def head(h, carry):
            hs = pl.ds(pl.multiple_of(h * HEAD_DIM, HEAD_DIM), HEAD_DIM)
            slope = sl_ref[h]
            qh = q_ref[:, hs]
            dn = (((1,), (1,)), ((), ()))
            sc = lax.dot_general(qh, kc_ref[:, hs], dn, preferred_element_type=F32) * scale - slope * jcf
            sp = lax.dot_general(qh, kp_ref[:, hs], dn, preferred_element_type=F32) * scale - slope * jpf
            sc = jnp.where(ok_c, sc, -1e30)
            sp = jnp.where(ok_p, sp, -1e30)
            mx = jnp.maximum(jnp.max(sc, axis=-1, keepdims=True), jnp.max(sp, axis=-1, keepdims=True))
            pc = jnp.exp(sc - mx)
            pp = jnp.exp(sp - mx)
            den = jnp.sum(pc, axis=-1, keepdims=True) + jnp.sum(pp, axis=-1, keepdims=True)
            inv = 1.0 / den
            o = jnp.dot((pc * inv).astype(BF16), vc_ref[:, hs], preferred_element_type=F32)
            o = o + jnp.dot((pp * inv).astype(BF16), vp_ref[:, hs], preferred_element_type=F32)
            lse = jnp.broadcast_to(mx + jnp.log(den), (bq, HEAD_DIM))
            if not first:
                lo = la_ref[:, hs]
                mm = jnp.maximum(lo, lse)
                ea = jnp.exp(lo - mm)
                eb = jnp.exp(lse - mm)
                tot = ea + eb
                o = (ea * oa_ref[:, hs] + eb * o) / tot
                lse = mm + jnp.log(tot)
            o_ref[:, hs] = o
            l_ref[:, hs] = lse
            return carry

        lax.fori_loop(0, nh, head, 0)

    cur = pl.BlockSpec((bq, dm), lambda r, nb: (nb, r))
    prv = pl.BlockSpec((bq, dm), lambda r, nb: (jnp.maximum(nb - 1, 0), r))
    smem = pl.BlockSpec(memory_space=pltpu.SMEM)
    ins = [slopes, view(q), view(k), view(k), view(v), view(v)]
    in_specs = [smem, cur, cur, prv, cur, prv]
    if not first:
        ins += [view(acc[0]), view(acc[1])]
        in_specs += [cur, cur]
    o, lse = pl.pallas_call(
        body,
        name=name,
        grid=(dil, nblk),
        in_specs=in_specs,
        out_specs=[cur, cur],
        out_shape=[jax.ShapeDtypeStruct((ll, dil * dm), F32), jax.ShapeDtypeStruct((ll, dil * dm), F32)],
        compiler_params=pltpu.CompilerParams(dimension_semantics=("parallel", "parallel")),
    )(*ins)
    return o.reshape(s, dm), lse.reshape(s, dm)


def _attn_bwd_branch(name, q, k, v, o, lse, do, slopes, dil, bq, acc):
    s, dm = q.shape
    nh = dm // HEAD_DIM
    ll = s // dil
    nblk = ll // bq
    scale = HEAD_DIM ** -0.5
    view = lambda z: z.reshape(ll, dil * dm)
    first = acc is None

    def body(sl_ref, qc_ref, qn_ref, kc_ref, kp_ref, vc_ref, vp_ref, oc_ref, on_ref, lc_ref, ln_ref, dc_ref, dn_ref,
             *refs):
        if first:
            dq_ref, dk_ref, dv_ref = refs
        else:
            aq_ref, ak_ref, av_ref, dq_ref, dk_ref, dv_ref = refs
        nb = pl.program_id(1)
        jc, jp = _band_masks(bq)
        ok_c = jc >= 0
        ok_p = (jp <= bq) & (nb > 0)
        ok_n = (jp <= bq) & (nb < nblk - 1)
        jcf = jc.astype(F32) * float(dil)
        jpf = jp.astype(F32) * float(dil)
        nt_dims = (((1,), (1,)), ((), ()))
        tn_dims = (((0,), (0,)), ((), ()))

        def head(h, carry):
            hs = pl.ds(pl.multiple_of(h * HEAD_DIM, HEAD_DIM), HEAD_DIM)
            slope = sl_ref[h]
            qc, qn = qc_ref[:, hs], qn_ref[:, hs]
            kc, kp = kc_ref[:, hs], kp_ref[:, hs]
            vc, vp = vc_ref[:, hs], vp_ref[:, hs]
            doc, don = dc_ref[:, hs], dn_ref[:, hs]
            lc, ln = lc_ref[:, hs][:, :1], ln_ref[:, hs][:, :1]
            dl_c = jnp.sum(doc.astype(F32) * oc_ref[:, hs], axis=-1, keepdims=True)
            dl_n = jnp.sum(don.astype(F32) * on_ref[:, hs], axis=-1, keepdims=True)

            def block(qq, kk_, vv, dd, ll_, dl, bias, ok):
                sc = lax.dot_general(qq, kk_, nt_dims, preferred_element_type=F32) * scale - slope * bias
                p = jnp.where(ok, jnp.exp(jnp.where(ok, sc, -1e30) - ll_), 0.0)
                dp = lax.dot_general(dd, vv, nt_dims, preferred_element_type=F32)
                ds = (p * (dp - dl) * scale).astype(BF16)
                return p.astype(BF16), ds

            p_a, ds_a = block(qc, kc, vc, doc, lc, dl_c, jcf, ok_c)
            _, ds_b = block(qc, kp, vp, doc, lc, dl_c, jpf, ok_p)
            p_c, ds_c = block(qn, kc, vc, don, ln, dl_n, jpf, ok_n)
            dq = jnp.dot(ds_a, kc, preferred_element_type=F32) + jnp.dot(ds_b, kp, preferred_element_type=F32)
            dk = lax.dot_general(ds_a, qc, tn_dims, preferred_element_type=F32)
            dk = dk + lax.dot_general(ds_c, qn, tn_dims, preferred_element_type=F32)
            dv = lax.dot_general(p_a, doc, tn_dims, preferred_element_type=F32)
            dv = dv + lax.dot_general(p_c, don, tn_dims, preferred_element_type=F32)
            if not first:
                dq = dq + aq_ref[:, hs]
                dk = dk + ak_ref[:, hs]
                dv = dv + av_ref[:, hs]
            dq_ref[:, hs] = dq
            dk_ref[:, hs] = dk
            dv_ref[:, hs] = dv
            return carry

        lax.fori_loop(0, nh, head, 0)

    cur = pl.BlockSpec((bq, dm), lambda r, nb: (nb, r))
    prv = pl.BlockSpec((bq, dm), lambda r, nb: (jnp.maximum(nb - 1, 0), r))
    nxt = pl.BlockSpec((bq, dm), lambda r, nb: (jnp.minimum(nb + 1, nblk - 1), r))
    smem = pl.BlockSpec(memory_space=pltpu.SMEM)
    ins = [slopes, view(q), view(q), view(k), view(k), view(v), view(v), view(o), view(o), view(lse), view(lse),
           view(do), view(do)]
    in_specs = [smem, cur, nxt, cur, prv, cur, prv, cur, nxt, cur, nxt, cur, nxt]
    if not first:
        ins += [view(z) for z in acc]
        in_specs += [cur, cur, cur]
    res = pl.pallas_call(
        body,
        name=name,
        grid=(dil, nblk),
        in_specs=in_specs,
        out_specs=[cur, cur, cur],
        out_shape=[jax.ShapeDtypeStruct((ll, dil * dm), F32) for _ in range(3)],
        compiler_params=pltpu.CompilerParams(dimension_semantics=("parallel", "parallel")),
    )(*ins)
    return tuple(z.reshape(s, dm) for z in res)


def _ep_id(accs, ex):
    return [accs[0]]


def _ep_all(accs, ex):
    return list(accs)


def _ep_add(accs, ex):
    return [accs[0] + ex[0].astype(F32)]


def _ep_bias_res(accs, ex):
    return [accs[0] + ex[0] + ex[1]]


def _ep_glu(accs, ex):
    a = accs[0] + ex[0]
    gt = accs[1] + ex[1]
    return [a * _sigmoid(gt), a, gt]


def _ep_swiglu(accs, ex):
    g, u = accs
    return [g, u, g * _sigmoid(g) * u]


def _ep_swiglu_bwd(accs, ex):
    dact = accs[0]
    g = ex[0].astype(F32)
    u = ex[1].astype(F32)
    sg = _sigmoid(g)
    return [dact * u * (sg * (1.0 + g * (1.0 - sg))), dact * g * sg]


def _ffn_fwd(tag, h, gain, wg, wu, wd, layer):
    s, d = h.shape
    f = wg.shape[-1]
    (n,) = _rms_fwd(f"{tag}_norm", h, gain)
    g, u, act = _mm(f"{tag}_gate_up", "nn", n, [(wg, layer, 0), (wu, layer, 0)], _ep_swiglu,
                    [(BF16,), (BF16,), (BF16,)], m=s, n=f, k=d)
    (out,) = _mm(f"{tag}_down", "nn", act, [(wd, layer, 0)], _ep_add, [(F32,)], m=s, n=d, k=f,
                 extras=[(h, "mn", 0)], bk=f // 4)
    return out, (n, g, u, act)


def _ffn_bwd(tag, h_in, gain, wg, wu, wd, layer, saved, dh, dhb):
    s, d = h_in.shape
    f = wg.shape[-1]
    n, g, u, act = saved
    dg, du = _mm(f"{tag}_bwd_dact", "nt", dhb, [(wd, layer, 0)], _ep_swiglu_bwd, [(BF16,), (BF16,)],
                 m=s, n=f, k=d, extras=[(g, "mn", 0), (u, "mn", 0)])
    (dwd,) = _mm(f"{tag}_bwd_dwd", "tn", act, [(dhb, None, 0)], _ep_id, [(BF16,)], m=f, n=d, k=s)
    dwg, dwu = _mm(f"{tag}_bwd_dwgu", "tn", n, [(dg, None, 0), (du, None, 0)], _ep_all, [(BF16,), (BF16,)],
                   m=d, n=f, k=s)
    (dn,) = _mm(f"{tag}_bwd_dn_g", "nt", dg, [(wg, layer, 0)], _ep_id, [(F32,)], m=s, n=d, k=f, bk=f // 4)
    (dn,) = _mm(f"{tag}_bwd_dn_u", "nt", du, [(wu, layer, 0)], _ep_add, [(F32,)], m=s, n=d, k=f, bk=f // 4,
                extras=[(dn, "mn", 0)])
    dx, dxb, dgain, cs = _rms_bwd(f"{tag}_bwd_norm", h_in, gain, [dn], dh)
    return dx, dxb, dict(gain=dgain, wg=dwg, wu=dwu, wd=dwd), cs


def _local_step(x, target, w):
    s, d = x.shape
    nh = d // HEAD_DIM
    bq = BRANCHES[0][0] // BRANCHES[0][1]
    assert all(win // dil == bq for win, dil in BRANCHES)
    slopes = _alibi_slopes(nh)
    nd = d // 512 if d >= 512 else 1
    bn = d // nd

    (n1,) = _rms_fwd("a_norm", x, w["a_norm_g"])
    glu, a, gt = _mm("conv_pw1_glu", "nn", n1, [(w["conv_w1"], None, 0), (w["conv_w1"], None, nd)], _ep_glu,
                     [(F32,), (BF16,), (BF16,)], m=s, n=d, k=d, bn=bn,
                     extras=[(w["conv_b1"], "n", 0), (w["conv_b1"], "n", nd)])
    c, sw = _dwconv_fwd("conv_dw_ln", glu, w["conv_dw"], w["conv_dw_b"], w["conv_ln_g"], w["conv_ln_b"])
    (h1,) = _mm("conv_pw2", "nn", sw, [(w["conv_w2"], None, 0)], _ep_bias_res, [(F32,)], m=s, n=d, k=d,
                extras=[(w["conv_b2"], "n", 0), (x, "mn", 0)])
    h2, ffn0 = _ffn_fwd("ffn0", h1, w["ffn_norm_g"][0:1], w["ffn_w_gate"], w["ffn_w_up"], w["ffn_w_down"], 0)
    kvn, qn = _rms_fwd("kvq_norm", h2, jnp.concatenate([w["kv_norm_g"], w["b_norm_g"]], axis=0))
    k, v = _mm("kv_proj", "nn", kvn, [(w["w_k"], None, 0), (w["w_v"], None, 0)], _ep_all, [(BF16,), (BF16,)],
               m=s, n=d, k=d)
    (q,) = _mm("q_proj", "nn", qn, [(w["w_q"], None, 0)], _ep_id, [(BF16,)], m=s, n=d, k=d)
    acc = None
    for bi, (win, dil) in enumerate(BRANCHES):
        acc = _attn_fwd_branch(f"attn_fwd_{bi}", q, k, v, slopes, dil, bq, acc)
    att, lse = acc
    (h3,) = _mm("o_proj", "nn", att, [(w["w_o"], None, 0)], _ep_add, [(F32,)], m=s, n=d, k=d,
                extras=[(h2, "mn", 0)])
    h4, ffn1 = _ffn_fwd("ffn1", h3, w["ffn_norm_g"][1:2], w["ffn_w_gate"], w["ffn_w_up"], w["ffn_w_down"], 1)
    dh4, dh4b, d_final_g, loss_cols = _final_loss("final_loss", h4, w["final_norm_g"], target)

    g = {}
    dh3, dh3b, gf1, _ = _ffn_bwd("ffn1", h3, w["ffn_norm_g"][1:2], w["ffn_w_gate"], w["ffn_w_up"],
                                 w["ffn_w_down"], 1, ffn1, dh4, dh4b)
    (datt,) = _mm("o_proj_bwd_dx", "nt", dh3b, [(w["w_o"], None, 0)], _ep_id, [(BF16,)], m=s, n=d, k=d)
    (g["w_o"],) = _mm("o_proj_bwd_dw", "tn", att, [(dh3b, None, 0)], _ep_id, [(BF16,)], m=d, n=d, k=s)
    dacc = None
    for bi, (win, dil) in enumerate(BRANCHES):
        dacc = _attn_bwd_branch(f"attn_bwd_{bi}", q, k, v, att, lse, datt, slopes, dil, bq, dacc)
    dq, dk, dv = dacc
    (g["w_q"],) = _mm("q_proj_bwd_dw", "tn", qn, [(dq, None, 0)], _ep_id, [(BF16,)], m=d, n=d, k=s)
    g["w_k"], g["w_v"] = _mm("kv_proj_bwd_dw", "tn", kvn, [(dk, None, 0), (dv, None, 0)], _ep_all,
                             [(BF16,), (BF16,)], m=d, n=d, k=s)
    (dqn,) = _mm("q_proj_bwd_dx", "nt", dq, [(w["w_q"], None, 0)], _ep_id, [(F32,)], m=s, n=d, k=d)
    (dkvn,) = _mm("k_proj_bwd_dx", "nt", dk, [(w["w_k"], None, 0)], _ep_id, [(F32,)], m=s, n=d, k=d)
    (dkvn,) = _mm("v_proj_bwd_dx", "nt", dv, [(w["w_v"], None, 0)], _ep_add, [(F32,)], m=s, n=d, k=d,
                  extras=[(dkvn, "mn", 0)])
    dh2, dh2b, dg_kvq, _ = _rms_bwd("kvq_norm_bwd", h2, jnp.concatenate([w["kv_norm_g"], w["b_norm_g"]], axis=0),
                                    [dkvn, dqn], dh3)
    dh1, dh1b, gf0, cs_h1 = _ffn_bwd("ffn0", h1, w["ffn_norm_g"][0:1], w["ffn_w_gate"], w["ffn_w_up"],
                                     w["ffn_w_down"], 0, ffn0, dh2, dh2b)
    (dsw,) = _mm("conv_pw2_bwd_dx", "nt", dh1b, [(w["conv_w2"], None, 0)], _ep_id, [(F32,)], m=s, n=d, k=d)
    (g["conv_w2"],) = _mm("conv_pw2_bwd_dw", "tn", sw, [(dh1b, None, 0)], _ep_id, [(BF16,)], m=d, n=d, k=s)
    dc, ln_sums = _conv_ln_bwd("conv_ln_bwd", c, dsw, w["conv_ln_g"], w["conv_ln_b"])
    dpre, ddw, db1 = _conv_dw_bwd("conv_dw_bwd", dc, glu, a, gt, w["conv_dw"])
    (g["conv_w1"],) = _mm("conv_pw1_bwd_dw", "tn", n1, [(dpre, None, 0)], _ep_id, [(BF16,)], m=d, n=2 * d, k=s)
    (dn1,) = _mm("conv_pw1_bwd_dx", "nt", dpre, [(w["conv_w1"], None, 0)], _ep_id, [(F32,)], m=s, n=d, k=2 * d)
    dx, _, d_a_norm, _ = _rms_bwd("a_norm_bwd", x, w["a_norm_g"], [dn1], dh1)

    g.update(
        a_norm_g=d_a_norm, conv_b1=db1, conv_dw=ddw, conv_dw_b=ln_sums[2:3], conv_ln_g=ln_sums[0:1],
        conv_ln_b=ln_sums[1:2], conv_b2=cs_h1, kv_norm_g=dg_kvq[0:1], b_norm_g=dg_kvq[1:2],
        ffn_norm_g=jnp.concatenate([gf0["gain"], gf1["gain"]], axis=0), final_norm_g=d_final_g,
        ffn_w_gate=(gf0["wg"], gf1["wg"]), ffn_w_up=(gf0["wu"], gf1["wu"]), ffn_w_down=(gf0["wd"], gf1["wd"]),
    )
    return loss_cols, dx, g


HBM_SPEC = pl.BlockSpec(memory_space=pltpu.HBM)


def _mesh_place():
    x, y, c = lax.axis_index("x"), lax.axis_index("y"), lax.axis_index("c")
    chips = [(1 - x, y), (x, 1 - y), (1 - x, 1 - y)]
    return x, y, c, chips


def _shard_view(ref, kind, s, half=None):
    nl, rows, cols = ref.shape
    lead = pl.ds(0, nl)
    if kind == "col":
        cw = cols // N_CHIPS
        if half is None:
            return ref.at[lead, pl.ds(0, rows), pl.ds(s * cw, cw)]
        return ref.at[lead, pl.ds(half * (rows // 2), rows // 2), pl.ds(s * cw, cw)]
    r = rows // N_CHIPS
    if half is None:
        return ref.at[lead, pl.ds(s * r, r), pl.ds(0, cols)]
    return ref.at[lead, pl.ds(s * r + half * (r // 2), r // 2), pl.ds(0, cols)]


def _gather_weights(shards, kinds):
    nt = len(shards)
    fulls = []
    for sh, kind in zip(shards, kinds):
        nl, r, cw = sh.shape
        fulls.append(jax.ShapeDtypeStruct((nl, r, cw * N_CHIPS) if kind == "col" else (nl, r * N_CHIPS, cw), sh.dtype))

    def body(*refs):
        src = refs[:nt]
        dst = refs[nt:2 * nt]
        send, recv, fsend, frecv, local = refs[2 * nt:]
        x, y, c, chips = _mesh_place()
        s = 2 * x + y
        sib = (x, y, 1 - c)

        def half_of_shard(t):
            nl, r, cw = src[t].shape
            return src[t].at[pl.ds(0, nl), pl.ds(c * (r // 2), r // 2), pl.ds(0, cw)]

        locals_ = [pltpu.make_async_copy(src[t], _shard_view(dst[t], kinds[t], s), local.at[t]) for t in range(nt)]
        for cp in locals_:
            cp.start()
        sends = []
        for t in range(nt):
            for j, chip in enumerate(chips):
                cp = pltpu.make_async_remote_copy(
                    src_ref=half_of_shard(t), dst_ref=_shard_view(dst[t], kinds[t], s, c),
                    send_sem=send.at[t, j], recv_sem=recv.at[t, j], device_id=(*chip, c), device_id_type=MESH)
                cp.start()
                sends.append(cp)
        for t in range(nt):
            for j, (px, py) in enumerate(chips):
                landed = _shard_view(dst[t], kinds[t], 2 * px + py, c)
                pltpu.make_async_remote_copy(
                    src_ref=half_of_shard(t), dst_ref=landed, send_sem=send.at[t, j], recv_sem=recv.at[t, j],
                    device_id=(px, py, c), device_id_type=MESH).wait_recv()
                cp = pltpu.make_async_remote_copy(
                    src_ref=landed, dst_ref=landed, send_sem=fsend.at[t, j], recv_sem=frecv.at[t, j],
                    device_id=sib, device_id_type=MESH)
                cp.start()
                sends.append(cp)
        for t in range(nt):
            for j, (px, py) in enumerate(chips):
                other = _shard_view(dst[t], kinds[t], 2 * px + py, 1 - c)
                pltpu.make_async_remote_copy(
                    src_ref=other, dst_ref=other, send_sem=fsend.at[t, j], recv_sem=frecv.at[t, j],
                    device_id=sib, device_id_type=MESH).wait_recv()
        for cp in sends:
            cp.wait_send()
        for cp in locals_:
            cp.wait()

    return pl.pallas_call(
        body,
        name="gather_weights",
        in_specs=[HBM_SPEC] * nt,
        out_specs=[HBM_SPEC] * nt,
        out_shape=fulls,
        scratch_shapes=[pltpu.SemaphoreType.DMA((nt, 3))] * 4 + [pltpu.SemaphoreType.DMA((nt,))],
    )(*shards)


def _exchange_grads(items):
    n_items = len(items)
    outs = []
    out_of = []
    for grad, kind, n_layers, layer in items:
        rows, cols = grad.shape
        if kind == "col":
            r, cw = rows, cols // N_CHIPS
        elif kind == "row":
            r, cw = rows // N_CHIPS, cols
        else:
            r, cw = rows, cols
        if layer == 0:
            outs.append(jax.ShapeDtypeStruct((N_DEV, n_layers, r, cw), grad.dtype))
        out_of.append(len(outs) - 1)
    n_out = len(outs)

    def body(*refs):
        src = refs[:n_items]
        dst = refs[n_items:n_items + n_out]
        send, recv, local = refs[n_items + n_out:]
        x, y, c, chips = _mesh_place()
        s = 2 * x + y
        me = 4 * x + 2 * y + c
        sib_idx = 4 * x + 2 * y + (1 - c)
        sib = (x, y, 1 - c)

        def part(t, chip_s):
            rows, cols = src[t].shape
            kind = items[t][1]
            if kind == "col":
                cw = cols // N_CHIPS
                return src[t].at[pl.ds(0, rows), pl.ds(chip_s * cw, cw)]
            if kind == "row":
                r = rows // N_CHIPS
                return src[t].at[pl.ds(chip_s * r, r), pl.ds(0, cols)]
            return src[t]

        def slot(t, dev_idx):
            return dst[out_of[t]].at[dev_idx, items[t][3]]

        locals_ = [pltpu.make_async_copy(part(t, s), slot(t, me), local.at[t]) for t in range(n_items)]
        for cp in locals_:
            cp.start()
        sends = []
        for t in range(n_items):
            cp = pltpu.make_async_remote_copy(
                src_ref=part(t, s), dst_ref=slot(t, me), send_sem=send.at[t, 0], recv_sem=recv.at[t, 0],
                device_id=sib, device_id_type=MESH)
            cp.start()
            sends.append(cp)
            for j, (px, py) in enumerate(chips):
                cp = pltpu.make_async_remote_copy(
                    src_ref=part(t, 2 * px + py), dst_ref=slot(t, me), send_sem=send.at[t, 1 + j],
                    recv_sem=recv.at[t, 1 + j], device_id=(px, py, c), device_id_type=MESH)
                cp.start()
                sends.append(cp)
        for t in range(n_items):
            for j, (px, py) in enumerate(chips):
                landed = slot(t, 4 * px + 2 * py + c)
                pltpu.make_async_remote_copy(
                    src_ref=landed, dst_ref=landed, send_sem=send.at[t, 1 + j], recv_sem=recv.at[t, 1 + j],
                    device_id=(px, py, c), device_id_type=MESH).wait_recv()
                cp = pltpu.make_async_remote_copy(
                    src_ref=landed, dst_ref=landed, send_sem=send.at[t, 4 + j], recv_sem=recv.at[t, 4 + j],
                    device_id=sib, device_id_type=MESH)
                cp.start()
                sends.append(cp)
        for t in range(n_items):
            theirs = slot(t, sib_idx)
            pltpu.make_async_remote_copy(
                src_ref=theirs, dst_ref=theirs, send_sem=send.at[t, 0], recv_sem=recv.at[t, 0],
                device_id=sib, device_id_type=MESH).wait_recv()
            for j, (px, py) in enumerate(chips):
                other = slot(t, 4 * px + 2 * py + (1 - c))
                pltpu.make_async_remote_copy(
                    src_ref=other, dst_ref=other, send_sem=send.at[t, 4 + j], recv_sem=recv.at[t, 4 + j],
                    device_id=sib, device_id_type=MESH).wait_recv()
        for cp in sends:
            cp.wait_send()
        for cp in locals_:
            cp.wait()

    return pl.pallas_call(
        body,
        name="exchange_grads",
        in_specs=[HBM_SPEC] * n_items,
        out_specs=[HBM_SPEC] * n_out,
        out_shape=outs,
        scratch_shapes=[pltpu.SemaphoreType.DMA((n_items, 7))] * 2 + [pltpu.SemaphoreType.DMA((n_items,))],
    )(*[it[0] for it in items])


def _adamw_reduce(name, contrib, w, m, v):
    rows, cols = w.shape
    t = 128 if rows % 128 == 0 else rows
    c1 = 1.0 - ADAM_B1 ** ADAM_STEP
    c2 = 1.0 - ADAM_B2 ** ADAM_STEP

    def body(c_ref, w_ref, m_ref, v_ref, g_ref, d_ref, nm_ref, nv_ref):
        g = c_ref[0].astype(F32)
        for q in range(1, N_DEV):
            g = g + c_ref[q].astype(F32)
        nm = ADAM_B1 * m_ref[...] + (1.0 - ADAM_B1) * g
        nv = ADAM_B2 * v_ref[...] + (1.0 - ADAM_B2) * (g * g)
        g_ref[...] = g
        nm_ref[...] = nm
        nv_ref[...] = nv
        d_ref[...] = -ADAM_LR * ((nm / c1) / (jnp.sqrt(nv / c2) + ADAM_EPS) + ADAM_WD * w_ref[...])

    blk = pl.BlockSpec((t, cols), lambda i: (i, 0))
    return pl.pallas_call(
        body,
        name=name,
        grid=(rows // t,),
        in_specs=[pl.BlockSpec((N_DEV, t, cols), lambda i: (0, i, 0)), blk, blk, blk],
        out_specs=[blk] * 4,
        out_shape=[jax.ShapeDtypeStruct((rows, cols), F32)] * 4,
        compiler_params=pltpu.CompilerParams(dimension_semantics=("parallel",)),
    )(contrib, w, m, v)


WEIGHT_NAMES = ("a_norm_g", "conv_w1", "conv_b1", "conv_dw", "conv_dw_b", "conv_ln_g", "conv_ln_b", "conv_w2",
                "conv_b2", "kv_norm_g", "w_k", "w_v", "b_norm_g", "w_q", "w_o", "ffn_norm_g", "ffn_w_gate",
                "ffn_w_up", "ffn_w_down", "final_norm_g")
MATRICES = (("conv_w1", "col"), ("conv_w2", "row"), ("w_k", "row"), ("w_v", "row"), ("w_q", "row"), ("w_o", "row"),
            ("ffn_w_gate", "col"), ("ffn_w_up", "col"), ("ffn_w_down", "row"))
PACKED = (("a_norm_g", 0, 1), ("conv_b1", 8, 2), ("conv_dw", 16, CONV_WIDTH), ("conv_dw_b", 48, 1),
          ("conv_ln_g", 56, 1), ("conv_ln_b", 64, 1), ("conv_b2", 72, 1))
PACK_ROWS = 80
WHOLE = (("kv_norm_g", 0, 1), ("b_norm_g", 1, 1), ("ffn_norm_g", 2, 2), ("final_norm_g", 4, 1))
WHOLE_ROWS = 8


def _pack_rows(parts, total, width):
    out, at = [], 0
    for arr, first in parts:
        if first > at:
            out.append(jnp.zeros((first - at, width), F32))
        rows8 = -(-arr.shape[0] // 8) * 8
        out.append(jnp.pad(arr, ((0, rows8 - arr.shape[0]), (0, 0))))
        at = first + rows8
    if total > at:
        out.append(jnp.zeros((total - at, width), F32))
    return jnp.concatenate(out, axis=0)


def kernel(x, a_norm_g, conv_w1, conv_b1, conv_dw, conv_dw_b, conv_ln_g, conv_ln_b, conv_w2, conv_b2, kv_norm_g, w_k, w_v, b_norm_g, w_q, w_o, ffn_norm_g, ffn_w_gate, ffn_w_up, ffn_w_down, final_norm_g, loss_target, m_a_norm_g, m_conv_w1, m_conv_b1, m_conv_dw, m_conv_dw_b, m_conv_ln_g, m_conv_ln_b, m_conv_w2, m_conv_b2, m_kv_norm_g, m_w_k, m_w_v, m_b_norm_g, m_w_q, m_w_o, m_ffn_norm_g, m_ffn_w_gate, m_ffn_w_up, m_ffn_w_down, m_final_norm_g, v_a_norm_g, v_conv_w1, v_conv_b1, v_conv_dw, v_conv_dw_b, v_conv_ln_g, v_conv_ln_b, v_conv_w2, v_conv_b2, v_kv_norm_g, v_w_k, v_w_v, v_b_norm_g, v_w_q, v_w_o, v_ffn_norm_g, v_ffn_w_gate, v_ffn_w_up, v_ffn_w_down, v_final_norm_g):
    args = locals()
    wts = {n: args[n] for n in WEIGHT_NAMES}
    mom = {n: args["m_" + n] for n in WEIGHT_NAMES}
    vel = {n: args["v_" + n] for n in WEIGHT_NAMES}
    s, d = x.shape[-2:]
    dq = d // N_CHIPS
    x2 = x.reshape(s, d)
    tgt = loss_target.reshape(s, d)

    def as3(a):
        return a.reshape((-1,) + a.shape[-2:])

    def pack_shard(src):
        return _pack_rows([(src[n].reshape(-1, dq), first) for n, first, _ in PACKED], PACK_ROWS, dq)

    def pack_whole(src):
        return _pack_rows([(jnp.concatenate([src[n].reshape(-1, d) for n, _, _ in WHOLE], axis=0), 0)], WHOLE_ROWS, d)

    shards = [as3(wts[n]).astype(BF16) for n, _ in MATRICES] + [pack_shard(wts)[None]]
    kinds = [k for _, k in MATRICES] + ["row"]
    fulls = _gather_weights(shards, kinds)
    w = {}
    for (n, _), full in zip(MATRICES, fulls):
        w[n] = full if n.startswith("ffn_w") else full.reshape(full.shape[1:])
    packed = fulls[-1].reshape(N_CHIPS, PACK_ROWS, dq)
    for n, first, rows in PACKED:
        part = packed[:, first:first + rows, :]
        if n == "conv_dw":
            w[n] = part.transpose(1, 0, 2).reshape(rows, d)
        else:
            w[n] = part.reshape(1, N_CHIPS * rows * dq)
    for n, _, rows in WHOLE:
        w[n] = wts[n].reshape(rows, d)

    loss_cols, dx, g = _local_step(x2, tgt, w)
    loss = lax.psum(jnp.sum(loss_cols), ("x", "y", "c"))

    items = []
    for n, kind in MATRICES:
        layers = g[n] if isinstance(g[n], tuple) else (g[n],)
        for li, gl in enumerate(layers):
            items.append((gl, kind, len(layers), li))
    gp = []
    for n, first, rows in PACKED:
        if n == "conv_dw":
            part = g[n].reshape(rows, N_CHIPS, dq).transpose(1, 0, 2)
        else:
            part = g[n].reshape(N_CHIPS, rows, dq)
        gp.append((part, first))
    g_packed = jnp.concatenate(
        [_pack_rows([(p[ci], first) for p, first in gp], PACK_ROWS, dq) for ci in range(N_CHIPS)], axis=0)
    items.append((g_packed, "row", 1, 0))
    items.append((pack_whole(g), "all", 1, 0))
    contribs = _exchange_grads(items)

    res = {}
    for (n, _), contrib in zip(MATRICES, contribs):
        shape = wts[n].shape
        r2 = (-1, shape[-1])
        outs = _adamw_reduce("adamw_" + n, contrib.reshape((N_DEV,) + (contrib.shape[1] * contrib.shape[2], shape[-1])),
                             wts[n].reshape(r2), mom[n].reshape(r2), vel[n].reshape(r2))
        res[n] = [o.reshape(shape) for o in outs]
    outs = _adamw_reduce("adamw_packed", contribs[-2].reshape(N_DEV, PACK_ROWS, dq), pack_shard(wts), pack_shard(mom),
                         pack_shard(vel))
    for n, first, rows in PACKED:
        res[n] = [o[first:first + rows].reshape(wts[n].shape) for o in outs]
    outs = _adamw_reduce("adamw_whole", contribs[-1].reshape(N_DEV, WHOLE_ROWS, d), pack_whole(wts), pack_whole(mom),
                         pack_whole(vel))
    for n, first, rows in WHOLE:
        res[n] = [o[first:first + rows].reshape(wts[n].shape) for o in outs]

    out = [loss, dx.reshape(x.shape)]
    for which in range(4):
        out += [res[n][which] for n in WEIGHT_NAMES]
    return tuple(out)
```

```python
import functools
import math

import jax
import jax.numpy as jnp
from jax import lax
from jax.experimental import pallas as pl
from jax.experimental.pallas import tpu as pltpu

F32 = jnp.float32
BF16 = jnp.bfloat16

HEAD_DIM = 128
BRANCHES = ((128, 1), (512, 4), (2048, 16))
CONV_WIDTH = 31
CONV_HALO = 32
RMS_EPS = 1e-6
LN_EPS = 1e-5
ADAM_LR = 0.001
ADAM_B1 = 0.9
ADAM_B2 = 0.999
ADAM_EPS = 1e-08
ADAM_WD = 0.01
ADAM_STEP = 10
N_CHIPS = 4
N_DEV = 8
MESH = pl.DeviceIdType.MESH


def _sigmoid(x):
    return 1.0 / (1.0 + jnp.exp(-x))


def _row_tile(rows, want):
    t = min(rows, want)
    assert rows % t == 0, (rows, want)
    return t


_DOT_DIMS = {"nn": ((1,), (0,)), "nt": ((1,), (1,)), "tn": ((0,), (0,))}


ANY_SPEC = pl.BlockSpec(memory_space=pl.ANY)


def _mm(name, mode, a, bs, epilogue, outs, *, m, n, k, extras=(), bm=1024, bn=512, bk=None, after=()):
    bm, bn = min(bm, m), min(bn, n)
    bk = k if bk is None else min(bk, k)
    assert m % bm == 0 and n % bn == 0 and k % bk == 0, (name, m, n, k, bm, bn, bk)
    nk = k // bk
    a_list = list(a) if isinstance(a, (list, tuple)) else [a]
    na, nb, ne, no = len(a_list), len(bs), len(extras), len(outs)
    assert na in (1, nb)

    if mode == "tn":
        a_spec = pl.BlockSpec((bk, bm), lambda i, j, kk: (kk, i))
    else:
        a_spec = pl.BlockSpec((bm, bk), lambda i, j, kk: (i, kk))

    def b_spec(lead, off):
        if mode == "nt":
            blk, idx = (bn, bk), (lambda i, j, kk: (j + off, kk))
        else:
            blk, idx = (bk, bn), (lambda i, j, kk: (kk, j + off))
        if lead is None:
            return pl.BlockSpec(blk, idx)
        return pl.BlockSpec((None,) + blk, lambda i, j, kk: (lead,) + idx(i, j, kk))

    def e_spec(kind, off):
        if kind == "mn":
            return pl.BlockSpec((bm, bn), lambda i, j, kk: (i, j + off))
        return pl.BlockSpec((1, bn), lambda i, j, kk: (0, j + off))

    nf = len(after)
    in_specs = [a_spec] * na + [b_spec(l, o) for _, l, o in bs] + [e_spec(kd, o) for _, kd, o in extras]
    in_specs += [ANY_SPEC] * nf
    out_specs = [pl.BlockSpec((bm, bn), lambda i, j, kk: (i, j)) for _ in outs]
    out_shape = [jax.ShapeDtypeStruct((m, n), dt) for (dt,) in outs]
    dims = (_DOT_DIMS[mode], ((), ()))

    def body(*refs):
        a_refs = refs[:na]
        b_refs = refs[na:na + nb]
        e_refs = refs[na + nb:na + nb + ne]
        o_refs = refs[na + nb + ne + nf:na + nb + ne + nf + no]
        acc_refs = refs[na + nb + ne + nf + no:]
        avs = [a_ref[...].astype(BF16) for a_ref in a_refs]
        prods = [lax.dot_general(avs[bi % na], b_ref[...].astype(BF16), dims, preferred_element_type=F32)
                 for bi, b_ref in enumerate(b_refs)]

        def finish(accs):
            res = epilogue(accs, [e_ref[...] for e_ref in e_refs])
            for o_ref, r in zip(o_refs, res):
                o_ref[...] = r.astype(o_ref.dtype)

        if nk == 1:
            finish(prods)
        else:
            kk = pl.program_id(2)

            @pl.when(kk == 0)
            def _():
                for acc_ref, p in zip(acc_refs, prods):
                    acc_ref[...] = p

            @pl.when(kk > 0)
            def _():
                for acc_ref, p in zip(acc_refs, prods):
                    acc_ref[...] += p

            @pl.when(kk == nk - 1)
            def _():
                finish([acc_ref[...] for acc_ref in acc_refs])

    scratch = [] if nk == 1 else [pltpu.VMEM((bm, bn), F32) for _ in bs]
    res = pl.pallas_call(
        body,
        name=name,
        grid=(m // bm, n // bn, nk),
        in_specs=in_specs,
        out_specs=out_specs,
        out_shape=out_shape,
        scratch_shapes=scratch,
        compiler_params=pltpu.CompilerParams(dimension_semantics=("parallel", "parallel", "arbitrary")),
    )(*a_list, *[b for b, _, _ in bs], *[e for e, _, _ in extras], *after)
    return res


def _rms_fwd(name, x, gains, after=()):
    s, d = x.shape
    ng = gains.shape[0]
    t = _row_tile(s, 256)
    nf = len(after)

    def body(x_ref, g_ref, *refs):
        o_refs = refs[nf:]
        xv = x_ref[...]
        r = lax.rsqrt(jnp.mean(xv * xv, axis=-1, keepdims=True) + RMS_EPS)
        xh = xv * r
        for gi, o_ref in enumerate(o_refs):
            o_ref[...] = (xh * g_ref[gi:gi + 1, :]).astype(o_ref.dtype)

    return pl.pallas_call(
        body,
        name=name,
        grid=(s // t,),
        in_specs=[pl.BlockSpec((t, d), lambda i: (i, 0)), pl.BlockSpec((ng, d), lambda i: (0, 0))] + [ANY_SPEC] * nf,
        out_specs=[pl.BlockSpec((t, d), lambda i: (i, 0)) for _ in range(ng)],
        out_shape=[jax.ShapeDtypeStruct((s, d), BF16) for _ in range(ng)],
        compiler_params=pltpu.CompilerParams(dimension_semantics=("parallel",)),
    )(x, gains, *after)


def _rms_bwd(name, x, gains, dns, dres):
    s, d = x.shape
    ng = gains.shape[0]
    t = _row_tile(s, 256)

    def body(x_ref, g_ref, dres_ref, *refs):
        dn_refs = refs[:ng]
        dx_ref, dxb_ref, dg_ref, cs_ref = refs[ng:]
        i = pl.program_id(0)
        xv = x_ref[...]
        r = lax.rsqrt(jnp.mean(xv * xv, axis=-1, keepdims=True) + RMS_EPS)
        xh = xv * r
        dx = dres_ref[...]
        dgs = []
        for gi in range(ng):
            dn = dn_refs[gi][...].astype(F32)
            dxh = dn * g_ref[gi:gi + 1, :]
            dgs.append(jnp.sum(dn * xh, axis=0, keepdims=True))
            dx = dx + r * (dxh - xh * jnp.mean(dxh * xh, axis=-1, keepdims=True))
        dx_ref[...] = dx
        dxb_ref[...] = dx.astype(BF16)
        dg = jnp.concatenate(dgs, axis=0) if ng > 1 else dgs[0]
        cs = jnp.sum(dx, axis=0, keepdims=True)

        @pl.when(i == 0)
        def _():
            dg_ref[...] = dg
            cs_ref[...] = cs

        @pl.when(i > 0)
        def _():
            dg_ref[...] += dg
            cs_ref[...] += cs

    row = pl.BlockSpec((t, d), lambda i: (i, 0))
    return pl.pallas_call(
        body,
        name=name,
        grid=(s // t,),
        in_specs=[row, pl.BlockSpec((ng, d), lambda i: (0, 0)), row] + [row] * ng,
        out_specs=[row, row, pl.BlockSpec((ng, d), lambda i: (0, 0)), pl.BlockSpec((1, d), lambda i: (0, 0))],
        out_shape=[
            jax.ShapeDtypeStruct((s, d), F32),
            jax.ShapeDtypeStruct((s, d), BF16),
            jax.ShapeDtypeStruct((ng, d), F32),
            jax.ShapeDtypeStruct((1, d), F32),
        ],
        compiler_params=pltpu.CompilerParams(dimension_semantics=("arbitrary",)),
    )(x, gains, dres, *dns)


def _final_loss(name, h, gain, target):
    s, d = h.shape
    t = _row_tile(s, 256)

    def body(h_ref, g_ref, t_ref, dh_ref, dhb_ref, dg_ref, ls_ref):
        i = pl.program_id(0)
        xv = h_ref[...]
        g = g_ref[...]
        r = lax.rsqrt(jnp.mean(xv * xv, axis=-1, keepdims=True) + RMS_EPS)
        xh = xv * r
        err = xh * g - t_ref[...]
        ls = jnp.sum(err * err, axis=0, keepdims=True) * (0.5 / d)
        dy = err * (1.0 / d)
        dxh = dy * g
        dg = jnp.sum(dy * xh, axis=0, keepdims=True)
        dx = r * (dxh - xh * jnp.mean(dxh * xh, axis=-1, keepdims=True))
        dh_ref[...] = dx
        dhb_ref[...] = dx.astype(BF16)

        @pl.when(i == 0)
        def _():
            dg_ref[...] = dg
            ls_ref[...] = ls

        @pl.when(i > 0)
        def _():
            dg_ref[...] += dg
            ls_ref[...] += ls

    row = pl.BlockSpec((t, d), lambda i: (i, 0))
    vec = pl.BlockSpec((1, d), lambda i: (0, 0))
    return pl.pallas_call(
        body,
        name=name,
        grid=(s // t,),
        in_specs=[row, vec, row],
        out_specs=[row, row, vec, vec],
        out_shape=[
            jax.ShapeDtypeStruct((s, d), F32),
            jax.ShapeDtypeStruct((s, d), BF16),
            jax.ShapeDtypeStruct((1, d), F32),
            jax.ShapeDtypeStruct((1, d), F32),
        ],
        compiler_params=pltpu.CompilerParams(dimension_semantics=("arbitrary",)),
    )(h, gain, target)


def _conv_tiles(s):
    t = _row_tile(s, 128)
    assert t % CONV_HALO == 0
    return t, t // CONV_HALO


def _dwconv_fwd(name, u, dw, dw_b, ln_g, ln_b):
    s, d = u.shape
    t, hb = _conv_tiles(s)
    w = dw.shape[0]
    lo = CONV_HALO - (w - 1)

    def body(cur_ref, prev_ref, dw_ref, dwb_ref, lg_ref, lb_ref, c_ref, sw_ref, cat_ref):
        i = pl.program_id(0)
        cat_ref[CONV_HALO:, :] = cur_ref[...]

        @pl.when(i == 0)
        def _():
            cat_ref[:CONV_HALO, :] = jnp.zeros((CONV_HALO, d), F32)

        @pl.when(i > 0)
        def _():
            cat_ref[:CONV_HALO, :] = prev_ref[...]

        acc = jnp.zeros((t, d), F32) + dwb_ref[...]
        for kk in range(w):
            acc = acc + dw_ref[kk:kk + 1, :] * cat_ref[lo + kk:lo + kk + t, :]
        c_ref[...] = acc
        mu = jnp.mean(acc, axis=-1, keepdims=True)
        cc = acc - mu
        var = jnp.mean(cc * cc, axis=-1, keepdims=True)
        ln = cc * lax.rsqrt(var + LN_EPS) * lg_ref[...] + lb_ref[...]
        sw_ref[...] = (ln * _sigmoid(ln)).astype(BF16)

    row = pl.BlockSpec((t, d), lambda i: (i, 0))
    prev = pl.BlockSpec((CONV_HALO, d), lambda i: (jnp.maximum(i * hb - 1, 0), 0))
    vec = pl.BlockSpec((1, d), lambda i: (0, 0))
    return pl.pallas_call(
        body,
        name=name,
        grid=(s // t,),
        in_specs=[row, prev, pl.BlockSpec((w, d), lambda i: (0, 0)), vec, vec, vec],
        out_specs=[row, row],
        out_shape=[jax.ShapeDtypeStruct((s, d), F32), jax.ShapeDtypeStruct((s, d), BF16)],
        scratch_shapes=[pltpu.VMEM((CONV_HALO + t, d), F32)],
        compiler_params=pltpu.CompilerParams(dimension_semantics=("parallel",)),
    )(u, u, dw, dw_b, ln_g, ln_b)


def _conv_ln_bwd(name, c, dsw, ln_g, ln_b):
    s, d = c.shape
    t = _row_tile(s, 256)

    def body(c_ref, dsw_ref, lg_ref, lb_ref, dc_ref, sums_ref):
        i = pl.program_id(0)
        cv = c_ref[...]
        g = lg_ref[...]
        mu = jnp.mean(cv, axis=-1, keepdims=True)
        cc = cv - mu
        rstd = lax.rsqrt(jnp.mean(cc * cc, axis=-1, keepdims=True) + LN_EPS)
        ch = cc * rstd
        ln = ch * g + lb_ref[...]
        sg = _sigmoid(ln)
        dln = dsw_ref[...] * (sg * (1.0 + ln * (1.0 - sg)))
        dch = dln * g
        dc = rstd * (dch - jnp.mean(dch, axis=-1, keepdims=True) - ch * jnp.mean(dch * ch, axis=-1, keepdims=True))
        dc_ref[...] = dc
        sums = jnp.concatenate(
            [
                jnp.sum(dln * ch, axis=0, keepdims=True),
                jnp.sum(dln, axis=0, keepdims=True),
                jnp.sum(dc, axis=0, keepdims=True),
                jnp.zeros((1, d), F32),
            ],
            axis=0,
        )

        @pl.when(i == 0)
        def _():
            sums_ref[...] = sums

        @pl.when(i > 0)
        def _():
            sums_ref[...] += sums

    row = pl.BlockSpec((t, d), lambda i: (i, 0))
    vec = pl.BlockSpec((1, d), lambda i: (0, 0))
    return pl.pallas_call(
        body,
        name=name,
        grid=(s // t,),
        in_specs=[row, row, vec, vec],
        out_specs=[row, pl.BlockSpec((4, d), lambda i: (0, 0))],
        out_shape=[jax.ShapeDtypeStruct((s, d), F32), jax.ShapeDtypeStruct((4, d), F32)],
        compiler_params=pltpu.CompilerParams(dimension_semantics=("arbitrary",)),
    )(c, dsw, ln_g, ln_b)


def _conv_dw_bwd(name, dc, u, a, gt, dw):
    s, d = dc.shape
    t, hb = _conv_tiles(s)
    w = dw.shape[0]
    lo = CONV_HALO - (w - 1)
    nt = s // t

    def body(dc_ref, dcn_ref, u_ref, up_ref, a_ref, gt_ref, dw_ref, dpre_ref, ddw_ref, db_ref, dcat_ref, ucat_ref):
        i = pl.program_id(0)
        dcat_ref[:t, :] = dc_ref[...]
        ucat_ref[CONV_HALO:, :] = u_ref[...]

        @pl.when(i == nt - 1)
        def _():
            dcat_ref[t:, :] = jnp.zeros((CONV_HALO, d), F32)

        @pl.when(i < nt - 1)
        def _():
            dcat_ref[t:, :] = dcn_ref[...]

        @pl.when(i == 0)
        def _():
            ucat_ref[:CONV_HALO, :] = jnp.zeros((CONV_HALO, d), F32)

        @pl.when(i > 0)
        def _():
            ucat_ref[:CONV_HALO, :] = up_ref[...]

        dcv = dc_ref[...]
        du = jnp.zeros((t, d), F32)
        rows = []
        for kk in range(w):
            sh = w - 1 - kk
            du = du + dw_ref[kk:kk + 1, :] * dcat_ref[sh:sh + t, :]
            rows.append(jnp.sum(dcv * ucat_ref[lo + kk:lo + kk + t, :], axis=0, keepdims=True))
        ddw = jnp.concatenate(rows, axis=0)
        av = a_ref[...].astype(F32)
        sg = _sigmoid(gt_ref[...].astype(F32))
        da = du * sg
        dgt = du * av * sg * (1.0 - sg)
        dpre_ref[:, :d] = da.astype(BF16)
        dpre_ref[:, d:] = dgt.astype(BF16)
        db = jnp.concatenate([jnp.sum(da, axis=0, keepdims=True), jnp.sum(dgt, axis=0, keepdims=True)], axis=1)

        @pl.when(i == 0)
        def _():
            ddw_ref[...] = ddw
            db_ref[...] = db

        @pl.when(i > 0)
        def _():
            ddw_ref[...] += ddw
            db_ref[...] += db

    row = pl.BlockSpec((t, d), lambda i: (i, 0))
    nxt = pl.BlockSpec((CONV_HALO, d), lambda i: (jnp.minimum((i + 1) * hb, s // CONV_HALO - 1), 0))
    prev = pl.BlockSpec((CONV_HALO, d), lambda i: (jnp.maximum(i * hb - 1, 0), 0))
    return pl.pallas_call(
        body,
        name=name,
        grid=(nt,),
        in_specs=[row, nxt, row, prev, row, row, pl.BlockSpec((w, d), lambda i: (0, 0))],
        out_specs=[
            pl.BlockSpec((t, 2 * d), lambda i: (i, 0)),
            pl.BlockSpec((w, d), lambda i: (0, 0)),
            pl.BlockSpec((1, 2 * d), lambda i: (0, 0)),
        ],
        out_shape=[
            jax.ShapeDtypeStruct((s, 2 * d), BF16),
            jax.ShapeDtypeStruct((w, d), F32),
            jax.ShapeDtypeStruct((1, 2 * d), F32),
        ],
        scratch_shapes=[pltpu.VMEM((t + CONV_HALO, d), F32), pltpu.VMEM((CONV_HALO + t, d), F32)],
        compiler_params=pltpu.CompilerParams(dimension_semantics=("arbitrary",)),
    )(dc, dc, u, u, a, gt, dw)


def _alibi_slopes(n_heads):
    h = jnp.arange(1, n_heads + 1, dtype=F32)
    return jnp.exp2(-8.0 * h / n_heads)


def _band_masks(bq):
    qi = lax.broadcasted_iota(jnp.int32, (bq, bq), 0)
    kj = lax.broadcasted_iota(jnp.int32, (bq, bq), 1)
    return qi - kj, qi - kj + bq


def _attn_fwd_branch(name, q, k, v, slopes, dil, bq, acc):
    s, dm = q.shape
    nh = dm // HEAD_DIM
    ll = s // dil
    assert ll % bq == 0
    nblk = ll // bq
    scale = HEAD_DIM ** -0.5
    view = lambda z: z.reshape(ll, dil * dm)
    first = acc is None

    def body(sl_ref, q_ref, kc_ref, kp_ref, vc_ref, vp_ref, *refs):
        if first:
            o_ref, l_ref = refs
        else:
            oa_ref, la_ref, o_ref, l_ref = refs
        nb = pl.program_id(1)
        jc, jp = _band_masks(bq)
        ok_c = jc >= 0
        ok_p = (jp <= bq) & (nb > 0)
        jcf = jc.astype(F32) * float(dil)
        jpf = jp.astype(F32) * float(dil)

        def head(h, carry):
            hs = pl.ds(pl.multiple_of(h * HEAD_DIM, HEAD_DIM), HEAD_DIM)
            slope = sl_ref[h]
            qh = q_ref[:, hs]
            dn = (((1,), (1,)), ((), ()))
            sc = lax.dot_general(qh, kc_ref[:, hs], dn, preferred_element_type=F32) * scale - slope * jcf
            sp = lax.dot_general(qh, kp_ref[:, hs], dn, preferred_element_type=F32) * scale - slope * jpf
            sc = jnp.where(ok_c, sc, -1e30)
            sp = jnp.where(ok_p, sp, -1e30)
            mx = jnp.maximum(jnp.max(sc, axis=-1, keepdims=True), jnp.max(sp, axis=-1, keepdims=True))
            pc = jnp.exp(sc - mx)
            pp = jnp.exp(sp - mx)
            den = jnp.sum(pc, axis=-1, keepdims=True) + jnp.sum(pp, axis=-1, keepdims=True)
            inv = 1.0 / den
            o = jnp.dot((pc * inv).astype(BF16), vc_ref[:, hs], preferred_element_type=F32)
            o = o + jnp.dot((pp * inv).astype(BF16), vp_ref[:, hs], preferred_element_type=F32)
            lse = jnp.broadcast_to(mx + jnp.log(den), (bq, HEAD_DIM))
            if not first:
                lo = la_ref[:, hs]
                mm = jnp.maximum(lo, lse)
                ea = jnp.exp(lo - mm)
                eb = jnp.exp(lse - mm)
                tot = ea + eb
                o = (ea * oa_ref[:, hs] + eb * o) / tot
                lse = mm + jnp.log(tot)
            o_ref[:, hs] = o
            l_ref[:, hs] = lse
            return carry

        lax.fori_loop(0, nh, head, 0)

    cur = pl.BlockSpec((bq, dm), lambda r, nb: (nb, r))
    prv = pl.BlockSpec((bq, dm), lambda r, nb: (jnp.maximum(nb - 1, 0), r))
    smem = pl.BlockSpec(memory_space=pltpu.SMEM)
    ins = [slopes, view(q), view(k), view(k), view(v), view(v)]
    in_specs = [smem, cur, cur, prv, cur, prv]
    if not first:
        ins += [view(acc[0]), view(acc[1])]
        in_specs += [cur, cur]
    o, lse = pl.pallas_call(
        body,
        name=name,
        grid=(dil, nblk),
        in_specs=in_specs,
        out_specs=[cur, cur],
        out_shape=[jax.ShapeDtypeStruct((ll, dil * dm), F32), jax.ShapeDtypeStruct((ll, dil * dm), F32)],
        compiler_params=pltpu.CompilerParams(dimension_semantics=("parallel", "parallel")),
    )(*ins)
    return o.reshape(s, dm), lse.reshape(s, dm)


def _attn_bwd_branch(name, q, k, v, o, lse, do, slopes, dil, bq, acc):
    s, dm = q.shape
    nh = dm // HEAD_DIM
    ll = s // dil
    nblk = ll // bq
    scale = HEAD_DIM ** -0.5
    view = lambda z: z.reshape(ll, dil * dm)
    first = acc is None

    def body(sl_ref, qc_ref, qn_ref, kc_ref, kp_ref, vc_ref, vp_ref, oc_ref, on_ref, lc_ref, ln_ref, dc_ref, dn_ref,
             *refs):
        if first:
            dq_ref, dk_ref, dv_ref = refs
        else:
            aq_ref, ak_ref, av_ref, dq_ref, dk_ref, dv_ref = refs
        nb = pl.program_id(1)
        jc, jp = _band_masks(bq)
        ok_c = jc >= 0
        ok_p = (jp <= bq) & (nb > 0)
        ok_n = (jp <= bq) & (nb < nblk - 1)
        jcf = jc.astype(F32) * float(dil)
        jpf = jp.astype(F32) * float(dil)
        nt_dims = (((1,), (1,)), ((), ()))
        tn_dims = (((0,), (0,)), ((), ()))

        def head(h, carry):
            hs = pl.ds(pl.multiple_of(h * HEAD_DIM, HEAD_DIM), HEAD_DIM)
            slope = sl_ref[h]
            qc, qn = qc_ref[:, hs], qn_ref[:, hs]
            kc, kp = kc_ref[:, hs], kp_ref[:, hs]
            vc, vp = vc_ref[:, hs], vp_ref[:, hs]
            doc, don = dc_ref[:, hs], dn_ref[:, hs]
            lc, ln = lc_ref[:, hs][:, :1], ln_ref[:, hs][:, :1]
            dl_c = jnp.sum(doc.astype(F32) * oc_ref[:, hs], axis=-1, keepdims=True)
            dl_n = jnp.sum(don.astype(F32) * on_ref[:, hs], axis=-1, keepdims=True)

            def block(qq, kk_, vv, dd, ll_, dl, bias, ok):
                sc = lax.dot_general(qq, kk_, nt_dims, preferred_element_type=F32) * scale - slope * bias
                p = jnp.where(ok, jnp.exp(jnp.where(ok, sc, -1e30) - ll_), 0.0)
                dp = lax.dot_general(dd, vv, nt_dims, preferred_element_type=F32)
                ds = (p * (dp - dl) * scale).astype(BF16)
                return p.astype(BF16), ds

            p_a, ds_a = block(qc, kc, vc, doc, lc, dl_c, jcf, ok_c)
            _, ds_b = block(qc, kp, vp, doc, lc, dl_c, jpf, ok_p)
            p_c, ds_c = block(qn, kc, vc, don, ln, dl_n, jpf, ok_n)
            dq = jnp.dot(ds_a, kc, preferred_element_type=F32) + jnp.dot(ds_b, kp, preferred_element_type=F32)
            dk = lax.dot_general(ds_a, qc, tn_dims, preferred_element_type=F32)
            dk = dk + lax.dot_general(ds_c, qn, tn_dims, preferred_element_type=F32)
            dv = lax.dot_general(p_a, doc, tn_dims, preferred_element_type=F32)
            dv = dv + lax.dot_general(p_c, don, tn_dims, preferred_element_type=F32)
            if not first:
                dq = dq + aq_ref[:, hs]
                dk = dk + ak_ref[:, hs]
                dv = dv + av_ref[:, hs]
            dq_ref[:, hs] = dq
            dk_ref[:, hs] = dk
            dv_ref[:, hs] = dv
            return carry

        lax.fori_loop(0, nh, head, 0)

    cur = pl.BlockSpec((bq, dm), lambda r, nb: (nb, r))
    prv = pl.BlockSpec((bq, dm), lambda r, nb: (jnp.maximum(nb - 1, 0), r))
    nxt = pl.BlockSpec((bq, dm), lambda r, nb: (jnp.minimum(nb + 1, nblk - 1), r))
    smem = pl.BlockSpec(memory_space=pltpu.SMEM)
    ins = [slopes, view(q), view(q), view(k), view(k), view(v), view(v), view(o), view(o), view(lse), view(lse),
           view(do), view(do)]
    in_specs = [smem, cur, nxt, cur, prv, cur, prv, cur, nxt, cur, nxt, cur, nxt]
    if not first:
        ins += [view(z) for z in acc]
        in_specs += [cur, cur, cur]
    res = pl.pallas_call(
        body,
        name=name,
        grid=(dil, nblk),
        in_specs=in_specs,
        out_specs=[cur, cur, cur],
        out_shape=[jax.ShapeDtypeStruct((ll, dil * dm), F32) for _ in range(3)],
        compiler_params=pltpu.CompilerParams(dimension_semantics=("parallel", "parallel")),
    )(*ins)
    return tuple(z.reshape(s, dm) for z in res)


def _ep_id(accs, ex):
    return [accs[0]]


def _ep_all(accs, ex):
    return list(accs)


def _ep_sum(accs, ex):
    return [accs[0] + accs[1]]


def _ep_add(accs, ex):
    return [accs[0] + ex[0].astype(F32)]


def _ep_bias_res(accs, ex):
    return [accs[0] + ex[0] + ex[1]]


def _ep_glu(accs, ex):
    a = accs[0] + ex[0]
    gt = accs[1] + ex[1]
    return [a * _sigmoid(gt), a, gt]


def _ep_swiglu(accs, ex):
    g, u = accs
    return [g, u, g * _sigmoid(g) * u]


def _ep_swiglu_bwd(accs, ex):
    dact = accs[0]
    g = ex[0].astype(F32)
    u = ex[1].astype(F32)
    sg = _sigmoid(g)
    return [dact * u * (sg * (1.0 + g * (1.0 - sg))), dact * g * sg]


def _ffn_fwd(tag, h, gain, wg, wu, wd):
    s, d = h.shape
    f = wg.shape[-1]
    (n,) = _rms_fwd(f"{tag}_norm", h, gain)
    g, u, act = _mm(f"{tag}_gate_up", "nn", n, [(wg, None, 0), (wu, None, 0)], _ep_swiglu,
                    [(BF16,), (BF16,), (BF16,)], m=s, n=f, k=d)
    (out,) = _mm(f"{tag}_down", "nn", act, [(wd, None, 0)], _ep_add, [(F32,)], m=s, n=d, k=f,
                 extras=[(h, "mn", 0)], bn=256)
    return out, (n, g, u, act)


def _ffn_bwd(tag, h_in, gain, wg, wu, wd, saved, dh, dhb, after, emit):
    s, d = h_in.shape
    f = wg.shape[-1]
    n, g, u, act = saved
    dg, du = _mm(f"{tag}_bwd_dact", "nt", dhb, [(wd, None, 0)], _ep_swiglu_bwd, [(BF16,), (BF16,)],
                 m=s, n=f, k=d, extras=[(g, "mn", 0), (u, "mn", 0)], after=after)
    (dwd,) = _mm(f"{tag}_bwd_dwd", "tn", act, [(dhb, None, 0)], _ep_id, [(BF16,)], m=f, n=d, k=s,
                 bm=f // 4)
    dwg, dwu = _mm(f"{tag}_bwd_dwgu", "tn", n, [(dg, None, 0), (du, None, 0)], _ep_all, [(BF16,), (BF16,)],
                   m=d, n=f, k=s)
    pin = emit(dict(gate=dwg, up=dwu, down=dwd))
    (dn,) = _mm(f"{tag}_bwd_dn", "nt", [dg, du], [(wg, None, 0), (wu, None, 0)], _ep_sum, [(F32,)],
                m=s, n=d, k=f, bm=512, bn=256, after=pin)
    dx, dxb, dgain, cs = _rms_bwd(f"{tag}_bwd_norm", h_in, gain, [dn], dh)
    return dx, dxb, dgain, cs


def _local_step(x, target, w, fetch, emit, after=()):
    s, d = x.shape
    nh = d // HEAD_DIM
    bq = BRANCHES[0][0] // BRANCHES[0][1]
    assert all(win // dil == bq for win, dil in BRANCHES)
    slopes = _alibi_slopes(nh)
    nd = d // 512 if d >= 512 else 1
    bn = d // nd

    (n1,) = _rms_fwd("a_norm", x, w["a_norm_g"], after=after)
    glu, a, gt = _mm("conv_pw1_glu", "nn", n1, [(w["conv_w1"], None, 0), (w["conv_w1"], None, nd)], _ep_glu,
                     [(F32,), (BF16,), (BF16,)], m=s, n=d, k=d, bn=bn,
                     extras=[(w["conv_b1"], "n", 0), (w["conv_b1"], "n", nd)])
    c, sw = _dwconv_fwd("conv_dw_ln", glu, w["conv_dw"], w["conv_dw_b"], w["conv_ln_g"], w["conv_ln_b"])
    w_conv2 = fetch("conv2", sw)["conv_w2"]
    (h1,) = _mm("conv_pw2", "nn", sw, [(w_conv2, None, 0)], _ep_bias_res, [(F32,)], m=s, n=d, k=d,
                extras=[(w["conv_b2"], "n", 0), (x, "mn", 0)])
    wf0 = fetch("ffn0", h1)
    h2, ffn0 = _ffn_fwd("ffn0", h1, w["ffn_norm_g"][0:1], wf0["gate"], wf0["up"], wf0["down"])
    wa = fetch("attn", h2)
    kvn, qn = _rms_fwd("kvq_norm", h2, jnp.concatenate([w["kv_norm_g"], w["b_norm_g"]], axis=0))
    k, v = _mm("kv_proj", "nn", kvn, [(wa["w_k"], None, 0), (wa["w_v"], None, 0)], _ep_all, [(BF16,), (BF16,)],
               m=s, n=d, k=d)
    (q,) = _mm("q_proj", "nn", qn, [(wa["w_q"], None, 0)], _ep_id, [(BF16,)], m=s, n=d, k=d)
    acc = None
    for bi, (win, dil) in enumerate(BRANCHES):
        acc = _attn_fwd_branch(f"attn_fwd_{bi}", q, k, v, slopes, dil, bq, acc)
    att, lse = acc
    (h3,) = _mm("o_proj", "nn", att, [(wa["w_o"], None, 0)], _ep_add, [(F32,)], m=s, n=d, k=d,
                extras=[(h2, "mn", 0)])
    wf1 = fetch("ffn1", h3)
    h4, ffn1 = _ffn_fwd("ffn1", h3, w["ffn_norm_g"][1:2], wf1["gate"], wf1["up"], wf1["down"])
    dh4, dh4b, d_final_g, loss_cols = _final_loss("final_loss", h4, w["final_norm_g"], target)

    g = {}
    ga = {}
    dh3, dh3b, dgain1, _ = _ffn_bwd("ffn1", h3, w["ffn_norm_g"][1:2], wf1["gate"], wf1["up"], wf1["down"], ffn1,
                                    dh4, dh4b, (), functools.partial(emit, "ffn1"))
    (datt,) = _mm("o_proj_bwd_dx", "nt", dh3b, [(wa["w_o"], None, 0)], _ep_id, [(BF16,)], m=s, n=d, k=d)
    (ga["w_o"],) = _mm("o_proj_bwd_dw", "tn", att, [(dh3b, None, 0)], _ep_id, [(BF16,)], m=d, n=d, k=s)
    dacc = None
    for bi, (win, dil) in enumerate(BRANCHES):
        dacc = _attn_bwd_branch(f"attn_bwd_{bi}", q, k, v, att, lse, datt, slopes, dil, bq, dacc)
    dq, dk, dv = dacc
    (ga["w_q"],) = _mm("q_proj_bwd_dw", "tn", qn, [(dq, None, 0)], _ep_id, [(BF16,)], m=d, n=d, k=s)
    ga["w_k"], ga["w_v"] = _mm("kv_proj_bwd_dw", "tn", kvn, [(dk, None, 0), (dv, None, 0)], _ep_all,
                               [(BF16,), (BF16,)], m=d, n=d, k=s, bm=512)
    pin = emit("attn", ga)
    (dqn,) = _mm("q_proj_bwd_dx", "nt", dq, [(wa["w_q"], None, 0)], _ep_id, [(F32,)], m=s, n=d, k=d, after=pin)
    (dkvn,) = _mm("kv_proj_bwd_dx", "nt", [dk, dv], [(wa["w_k"], None, 0), (wa["w_v"], None, 0)], _ep_sum, [(F32,)],
                  m=s, n=d, k=d, bm=512)
    dh2, dh2b, dg_kvq, _ = _rms_bwd("kvq_norm_bwd", h2, jnp.concatenate([w["kv_norm_g"], w["b_norm_g"]], axis=0),
                                    [dkvn, dqn], dh3)
    dh1, dh1b, dgain0, cs_h1 = _ffn_bwd("ffn0", h1, w["ffn_norm_g"][0:1], wf0["gate"], wf0["up"], wf0["down"], ffn0,
                                        dh2, dh2b, (), functools.partial(emit, "ffn0"))
    (dsw,) = _mm("conv_pw2_bwd_dx", "nt", dh1b, [(w_conv2, None, 0)], _ep_id, [(F32,)], m=s, n=d, k=d)
    (g["conv_w2"],) = _mm("conv_pw2_bwd_dw", "tn", sw, [(dh1b, None, 0)], _ep_id, [(BF16,)], m=d, n=d, k=s)
    dc, ln_sums = _conv_ln_bwd("conv_ln_bwd", c, dsw, w["conv_ln_g"], w["conv_ln_b"])
    dpre, ddw, db1 = _conv_dw_bwd("conv_dw_bwd", dc, glu, a, gt, w["conv_dw"])
    (g["conv_w1"],) = _mm("conv_pw1_bwd_dw", "tn", n1, [(dpre, None, 0)], _ep_id, [(BF16,)], m=d, n=2 * d, k=s)
    (dn1,) = _mm("conv_pw1_bwd_dx", "nt", dpre, [(w["conv_w1"], None, 0)], _ep_id, [(F32,)], m=s, n=d, k=2 * d)
    dx, _, d_a_norm, _ = _rms_bwd("a_norm_bwd", x, w["a_norm_g"], [dn1], dh1)

    g.update(
        a_norm_g=d_a_norm, conv_b1=db1, conv_dw=ddw, conv_dw_b=ln_sums[2:3], conv_ln_g=ln_sums[0:1],
        conv_ln_b=ln_sums[1:2], conv_b2=cs_h1, kv_norm_g=dg_kvq[0:1], b_norm_g=dg_kvq[1:2],
        ffn_norm_g=jnp.concatenate([dgain0, dgain1], axis=0), final_norm_g=d_final_g,
    )
    return loss_cols, dx, g


HBM_SPEC = pl.BlockSpec(memory_space=pltpu.HBM)


def _mesh_place():
    x, y, c = lax.axis_index("x"), lax.axis_index("y"), lax.axis_index("c")
    chips = [(1 - x, y), (x, 1 - y), (1 - x, 1 - y)]
    return x, y, c, chips


def _shard_view(ref, kind, s, half=None):
    rows, cols = ref.shape
    if kind == "col":
        cw = cols // N_CHIPS
        if half is None:
            return ref.at[pl.ds(0, rows), pl.ds(s * cw, cw)]
        return ref.at[pl.ds(half * (rows // 2), rows // 2), pl.ds(s * cw, cw)]
    r = rows // N_CHIPS
    if half is None:
        return ref.at[pl.ds(s * r, r), pl.ds(0, cols)]
    return ref.at[pl.ds(s * r + half * (r // 2), r // 2), pl.ds(0, cols)]


def _full_shape(shard, kind):
    r, cw = shard.shape
    return (r, cw * N_CHIPS) if kind == "col" else (r * N_CHIPS, cw)


def _gather_weights(shards, kinds):
    nt = len(shards)
    fulls = [jax.ShapeDtypeStruct(_full_shape(sh, kind), sh.dtype) for sh, kind in zip(shards, kinds)]

    def body(*refs):
        src = refs[:nt]
        dst = refs[nt:2 * nt]
        send, recv, fsend, frecv, local = refs[2 * nt:]
        x, y, c, chips = _mesh_place()
        s = 2 * x + y
        sib = (x, y, 1 - c)

        def half_of_shard(t):
            r, cw = src[t].shape
            return src[t].at[pl.ds(c * (r // 2), r // 2), pl.ds(0, cw)]

        locals_ = [pltpu.make_async_copy(src[t], _shard_view(dst[t], kinds[t], s), local.at[t]) for t in range(nt)]
        for cp in locals_:
            cp.start()
        sends = []
        for t in range(nt):
            for j, chip in enumerate(chips):
                cp = pltpu.make_async_remote_copy(
                    src_ref=half_of_shard(t), dst_ref=_shard_view(dst[t], kinds[t], s, c),
                    send_sem=send.at[t, j], recv_sem=recv.at[t, j], device_id=(*chip, c), device_id_type=MESH)
                cp.start()
                sends.append(cp)
        for t in range(nt):
            for j, (px, py) in enumerate(chips):
                landed = _shard_view(dst[t], kinds[t], 2 * px + py, c)
                pltpu.make_async_remote_copy(
                    src_ref=half_of_shard(t), dst_ref=landed, send_sem=send.at[t, j], recv_sem=recv.at[t, j],
                    device_id=(px, py, c), device_id_type=MESH).wait_recv()
                cp = pltpu.make_async_remote_copy(
                    src_ref=landed, dst_ref=landed, send_sem=fsend.at[t, j], recv_sem=frecv.at[t, j],
                    device_id=sib, device_id_type=MESH)
                cp.start()
                sends.append(cp)
        for t in range(nt):
            for j, (px, py) in enumerate(chips):
                other = _shard_view(dst[t], kinds[t], 2 * px + py, 1 - c)
                pltpu.make_async_remote_copy(
                    src_ref=other, dst_ref=other, send_sem=fsend.at[t, j], recv_sem=frecv.at[t, j],
                    device_id=sib, device_id_type=MESH).wait_recv()
        for cp in sends:
            cp.wait_send()
        for cp in locals_:
            cp.wait()

    return pl.pallas_call(
        body,
        name="gather_weights",
        in_specs=[HBM_SPEC] * nt,
        out_specs=[HBM_SPEC] * nt,
        out_shape=fulls,
        scratch_shapes=[pltpu.SemaphoreType.DMA((nt, 3))] * 4 + [pltpu.SemaphoreType.DMA((nt,))],
    )(*shards)


def _place_own(name, srcs, lands, views):
    nt = len(srcs)

    def body(*refs):
        src, dst, sem = refs[:nt], refs[nt:2 * nt], refs[2 * nt]
        place = _mesh_place()
        cps = [pltpu.make_async_copy(*views(t, src[t], dst[t], place), sem.at[t]) for t in range(nt)]
        for cp in cps:
            cp.start()
        for cp in cps:
            cp.wait()

    return pl.pallas_call(
        body,
        name=name,
        in_specs=[HBM_SPEC] * nt,
        out_specs=[HBM_SPEC] * nt,
        out_shape=list(lands),
        scratch_shapes=[pltpu.SemaphoreType.DMA((nt,))],
    )(*srcs)


SEM_SPEC = pl.BlockSpec(memory_space=pltpu.SEMAPHORE)
SIDE_EFFECT = pltpu.SideEffectType.DATAFLOW_SIDE_EFFECTING


def _copies_start(name, srcs, lands, copies, n_sems, after):
    ns, nl, nf = len(srcs), len(lands), len(after)

    def body(*refs):
        src, land = refs[:ns], refs[ns:ns + nl]
        send, recv = refs[ns + nl + nf], refs[ns + nl + nf + 1]
        pin = refs[-1]
        for cp in copies(src, land, send, recv, _mesh_place()):
            cp.start()
        pin[...] = jnp.zeros_like(pin)

    arrs = list(srcs) + list(lands)
    res = pl.pallas_call(
        body,
        name=name,
        in_specs=[HBM_SPEC] * (ns + nl) + [ANY_SPEC] * nf,
        out_specs=[SEM_SPEC, SEM_SPEC] + [HBM_SPEC] * (ns + nl) + [pl.BlockSpec(memory_space=pltpu.VMEM)],
        out_shape=[pltpu.SemaphoreType.DMA((n_sems,)), pltpu.SemaphoreType.DMA((n_sems,))]
        + [pltpu.HBM(a.shape, a.dtype) for a in arrs] + [jax.ShapeDtypeStruct((8, 128), F32)],
        input_output_aliases={i: 2 + i for i in range(ns + nl)},
        compiler_params=pltpu.CompilerParams(has_side_effects=SIDE_EFFECT),
    )(*[pltpu.with_memory_space_constraint(a, pltpu.HBM) for a in arrs], *after)
    return res[0], res[1], list(res[2:2 + ns]), list(res[2 + ns:2 + ns + nl]), res[-1]


def _copies_wait(name, send, recv, srcs, lands, copies, after):
    ns, nl, nf = len(srcs), len(lands), len(after)

    def body(*refs):
        src, land = refs[:ns], refs[ns:ns + nl]
        send_sems, recv_sems = refs[ns + nl], refs[ns + nl + 1]
        cps = copies(src, land, send_sems, recv_sems, _mesh_place())
        for cp in cps:
            cp.wait_send()
        for cp in cps:
            cp.wait_recv()

    arrs = list(srcs) + list(lands)
    res = pl.pallas_call(
        body,
        name=name,
        in_specs=[HBM_SPEC] * (ns + nl) + [SEM_SPEC, SEM_SPEC] + [ANY_SPEC] * nf,
        out_specs=[HBM_SPEC] * (ns + nl),
        out_shape=[pltpu.HBM(a.shape, a.dtype) for a in arrs],
        input_output_aliases={i: i for i in range(ns + nl)},
        compiler_params=pltpu.CompilerParams(has_side_effects=SIDE_EFFECT),
    )(*arrs, send, recv, *after)
    return list(res[ns:])


def _gather_copies(kinds):
    def copies(src, land, send, recv, place):
        x, y, c, chips = place
        s = 2 * x + y
        return [
            pltpu.make_async_remote_copy(
                src_ref=src[t], dst_ref=_shard_view(land[t], kinds[t], s), send_sem=send.at[3 * t + j],
                recv_sem=recv.at[3 * t + j], device_id=(px, py, c), device_id_type=MESH)
            for t in range(len(kinds)) for j, (px, py) in enumerate(chips)
        ]

    def own(t, src_ref, land_ref, place):
        x, y, c, chips = place
        return src_ref, _shard_view(land_ref, kinds[t], 2 * x + y)

    return copies, own


def _grad_part(ref, kind, s):
    return ref if kind == "all" else _shard_view(ref, kind, s)


def _grad_copies(kinds):
    def peers(place):
        x, y, c, chips = place
        return [(x, y, 1 - c)] + [(px, py, c) for px, py in chips] + [(px, py, 1 - c) for px, py in chips]

    def copies(src, land, send, recv, place):
        x, y, c, chips = place
        me = 4 * x + 2 * y + c
        return [
            pltpu.make_async_remote_copy(
                src_ref=_grad_part(src[t], kinds[t], 2 * px + py), dst_ref=land[t].at[me], send_sem=send.at[7 * t + k],
                recv_sem=recv.at[7 * t + k], device_id=(px, py, pc), device_id_type=MESH)
            for t in range(len(kinds)) for k, (px, py, pc) in enumerate(peers(place))
        ]

    def own(t, src_ref, land_ref, place):
        x, y, c, chips = place
        return _grad_part(src_ref, kinds[t], 2 * x + y), land_ref.at[4 * x + 2 * y + c]

    return copies, own


def _land_shape(grad, kind):
    rows, cols = grad.shape
    if kind == "col":
        return (N_DEV, rows, cols // N_CHIPS)
    if kind == "row":
        return (N_DEV, rows // N_CHIPS, cols)
    return (N_DEV, rows, cols)


def _adamw_reduce(name, contrib, w, m, v, layer=None, prev=None):
    rows, cols = w.shape[-2:]
    t = 128 if rows % 128 == 0 else rows
    c1 = 1.0 - ADAM_B1 ** ADAM_STEP
    c2 = 1.0 - ADAM_B2 ** ADAM_STEP

    n_prev = 0 if prev is None else 4

    def body(c_ref, w_ref, m_ref, v_ref, *refs):
        g_ref, d_ref, nm_ref, nv_ref = refs[n_prev:]
        g = c_ref[0].astype(F32)
        for q in range(1, N_DEV):
            g = g + c_ref[q].astype(F32)
        nm = ADAM_B1 * m_ref[...] + (1.0 - ADAM_B1) * g
        nv = ADAM_B2 * v_ref[...] + (1.0 - ADAM_B2) * (g * g)
        g_ref[...] = g
        nm_ref[...] = nm
        nv_ref[...] = nv
        d_ref[...] = -ADAM_LR * ((nm / c1) / (jnp.sqrt(nv / c2) + ADAM_EPS) + ADAM_WD * w_ref[...])

    if layer is None:
        blk = pl.BlockSpec((t, cols), lambda i: (i, 0))
    else:
        blk = pl.BlockSpec((None, t, cols), lambda i: (layer, i, 0))
    return pl.pallas_call(
        body,
        name=name,
        grid=(rows // t,),
        in_specs=[pl.BlockSpec((N_DEV, t, cols), lambda i: (0, i, 0)), blk, blk, blk] + [ANY_SPEC] * n_prev,
        out_specs=[blk] * 4,
        out_shape=[jax.ShapeDtypeStruct(w.shape, F32)] * 4,
        input_output_aliases={4 + i: i for i in range(n_prev)},
        compiler_params=pltpu.CompilerParams(dimension_semantics=("parallel",)),
    )(contrib, w, m, v, *(prev or ()))


WEIGHT_NAMES = ("a_norm_g", "conv_w1", "conv_b1", "conv_dw", "conv_dw_b", "conv_ln_g", "conv_ln_b", "conv_w2",
                "conv_b2", "kv_norm_g", "w_k", "w_v", "b_norm_g", "w_q", "w_o", "ffn_norm_g", "ffn_w_gate",
                "ffn_w_up", "ffn_w_down", "final_norm_g")
GROUPS = {
    "conv2": (("conv_w2", "conv_w2", None, "row"),),
    "ffn0": (("gate", "ffn_w_gate", 0, "col"), ("up", "ffn_w_up", 0, "col"), ("down", "ffn_w_down", 0, "row")),
    "attn": (("w_k", "w_k", None, "row"), ("w_v", "w_v", None, "row"), ("w_q", "w_q", None, "row"),
             ("w_o", "w_o", None, "row")),
    "ffn1": (("gate", "ffn_w_gate", 1, "col"), ("up", "ffn_w_up", 1, "col"), ("down", "ffn_w_down", 1, "row")),
}
FETCH_ORDER = ("conv2", "ffn0", "attn", "ffn1")
EMIT_ORDER = ("ffn1", "attn", "ffn0")
PACKED = (("a_norm_g", 0, 1), ("conv_b1", 8, 2), ("conv_dw", 16, CONV_WIDTH), ("conv_dw_b", 48, 1),
          ("conv_ln_g", 56, 1), ("conv_ln_b", 64, 1), ("conv_b2", 72, 1))
PACK_ROWS = 80
WHOLE = (("kv_norm_g", 0, 1), ("b_norm_g", 1, 1), ("ffn_norm_g", 2, 2), ("final_norm_g", 4, 1))
WHOLE_ROWS = 8


def _pack_rows(parts, total, width):
    out, at = [], 0
    for arr, first in parts:
        if first > at:
            out.append(jnp.zeros((first - at, width), F32))
        rows8 = -(-arr.shape[0] // 8) * 8
        out.append(jnp.pad(arr, ((0, rows8 - arr.shape[0]), (0, 0))))
        at = first + rows8
    if total > at:
        out.append(jnp.zeros((total - at, width), F32))
    return jnp.concatenate(out, axis=0)


def kernel(x, a_norm_g, conv_w1, conv_b1, conv_dw, conv_dw_b, conv_ln_g, conv_ln_b, conv_w2, conv_b2, kv_norm_g, w_k, w_v, b_norm_g, w_q, w_o, ffn_norm_g, ffn_w_gate, ffn_w_up, ffn_w_down, final_norm_g, loss_target, m_a_norm_g, m_conv_w1, m_conv_b1, m_conv_dw, m_conv_dw_b, m_conv_ln_g, m_conv_ln_b, m_conv_w2, m_conv_b2, m_kv_norm_g, m_w_k, m_w_v, m_b_norm_g, m_w_q, m_w_o, m_ffn_norm_g, m_ffn_w_gate, m_ffn_w_up, m_ffn_w_down, m_final_norm_g, v_a_norm_g, v_conv_w1, v_conv_b1, v_conv_dw, v_conv_dw_b, v_conv_ln_g, v_conv_ln_b, v_conv_w2, v_conv_b2, v_kv_norm_g, v_w_k, v_w_v, v_b_norm_g, v_w_q, v_w_o, v_ffn_norm_g, v_ffn_w_gate, v_ffn_w_up, v_ffn_w_down, v_final_norm_g):
    args = locals()
    wts = {n: args[n] for n in WEIGHT_NAMES}
    mom = {n: args["m_" + n] for n in WEIGHT_NAMES}
    vel = {n: args["v_" + n] for n in WEIGHT_NAMES}
    s, d = x.shape[-2:]
    dq = d // N_CHIPS
    x2 = x.reshape(s, d)
    tgt = loss_target.reshape(s, d)

    def shard_bf16(name, layer):
        a = wts[name]
        a = a[layer] if layer is not None else a.reshape(a.shape[-2:])
        return a.astype(BF16)

    def pack_shard(src):
        return _pack_rows([(src[n].reshape(-1, dq), first) for n, first, _ in PACKED], PACK_ROWS, dq)

    def pack_whole(src):
        return _pack_rows([(jnp.concatenate([src[n].reshape(-1, d) for n, _, _ in WHOLE], axis=0), 0)], WHOLE_ROWS, d)

    conv_w1_full, packed_full = _gather_weights([shard_bf16("conv_w1", None), pack_shard(wts)], ["col", "row"])
    w = {"conv_w1": conv_w1_full}
    gathers = {}
    pins = []
    for grp in FETCH_ORDER:
        srcs = [shard_bf16(n, layer) for _, n, layer, _ in GROUPS[grp]]
        kinds = [kind for _, _, _, kind in GROUPS[grp]]
        copies, own = _gather_copies(kinds)
        lands = _place_own("gather_place_" + grp, srcs,
                           [jax.ShapeDtypeStruct(_full_shape(sh, kd), BF16) for sh, kd in zip(srcs, kinds)], own)
        send, recv, srcs, lands, pin = _copies_start("gather_start_" + grp, srcs, lands, copies, 3 * len(srcs), pins)
        gathers[grp] = (send, recv, srcs, lands, copies)
        pins = [pin]

    def fetch(grp, after):
        send, recv, srcs, lands, copies = gathers[grp]
        fulls = _copies_wait("gather_wait_" + grp, send, recv, srcs, lands, copies, [after])
        return {key: full for (key, _, _, _), full in zip(GROUPS[grp], fulls)}

    packed = packed_full.reshape(N_CHIPS, PACK_ROWS, dq)
    for n, first, rows in PACKED:
        part = packed[:, first:first + rows, :]
        if n == "conv_dw":
            w[n] = part.transpose(1, 0, 2).reshape(rows, d)
        else:
            w[n] = part.reshape(1, N_CHIPS * rows * dq)
    for n, _, rows in WHOLE:
        w[n] = wts[n].reshape(rows, d)

    exchanges = {}

    def emit_grads(tag, grads, kinds):
        copies, own = _grad_copies(kinds)
        lands = _place_own("grads_place_" + tag, grads,
                           [jax.ShapeDtypeStruct(_land_shape(gr, kd), gr.dtype) for gr, kd in zip(grads, kinds)], own)
        send, recv, srcs, lands, pin = _copies_start("grads_start_" + tag, grads, lands, copies, 7 * len(grads), ())
        exchanges[tag] = (send, recv, srcs, lands, copies)
        return [pin]

    def emit(grp, grads):
        return emit_grads(grp, [grads[key] for key, _, _, _ in GROUPS[grp]], [kind for _, _, _, kind in GROUPS[grp]])

    loss_cols, dx, g = _local_step(x2, tgt, w, fetch, emit, after=pins)
    loss = lax.psum(jnp.sum(loss_cols), ("x", "y", "c"))

    gp = []
    for n, first, rows in PACKED:
        if n == "conv_dw":
            part = g[n].reshape(rows, N_CHIPS, dq).transpose(1, 0, 2)
        else:
            part = g[n].reshape(N_CHIPS, rows, dq)
        gp.append((part, first))
    g_packed = jnp.concatenate(
        [_pack_rows([(p[ci], first) for p, first in gp], PACK_ROWS, dq) for ci in range(N_CHIPS)], axis=0)
    emit_grads("conv", [g["conv_w2"], g["conv_w1"], g_packed, pack_whole(g)], ["row", "col", "row", "all"])

    contribs = {}
    for tag in EMIT_ORDER + ("conv",):
        send, recv, srcs, lands, copies = exchanges[tag]
        contribs[tag] = _copies_wait("grads_wait_" + tag, send, recv, srcs, lands, copies, [dx])

    res = {}

    def adamw(n, contrib, layer=None, prev=None):
        if layer is None:
            shape = wts[n].shape
            r2 = shape[-2:]
            outs = _adamw_reduce("adamw_" + n, contrib, wts[n].reshape(r2), mom[n].reshape(r2), vel[n].reshape(r2))
            return [o.reshape(shape) for o in outs]
        return _adamw_reduce(f"adamw_{n}_{layer}", contrib, wts[n], mom[n], vel[n], layer, prev)

    for grp in ("conv2", "attn"):
        tag = "conv" if grp == "conv2" else grp
        for (key, n, _, _), contrib in zip(GROUPS[grp], contribs[tag]):
            res[n] = adamw(n, contrib)
    res["conv_w1"] = adamw("conv_w1", contribs["conv"][1])
    for (key, n, _, _), c0, c1 in zip(GROUPS["ffn0"], contribs["ffn0"], contribs["ffn1"]):
        res[n] = adamw(n, c1, 1, adamw(n, c0, 0))
    outs = _adamw_reduce("adamw_packed", contribs["conv"][2], pack_shard(wts), pack_shard(mom), pack_shard(vel))
    for n, first, rows in PACKED:
        res[n] = [o[first:first + rows].reshape(wts[n].shape) for o in outs]
    outs = _adamw_reduce("adamw_whole", contribs["conv"][3], pack_whole(wts), pack_whole(mom), pack_whole(vel))
    for n, first, rows in WHOLE:
        res[n] = [o[first:first + rows].reshape(wts[n].shape) for o in outs]

    out = [loss, dx.reshape(x.shape)]
    for which in range(4):
        out += [res[n][which] for n in WEIGHT_NAMES]
    return tuple(out)
```

```python
import functools
import math

import jax
import jax.numpy as jnp
from jax import lax
from jax.experimental import pallas as pl
from jax.experimental.pallas import tpu as pltpu

F32 = jnp.float32
BF16 = jnp.bfloat16

HEAD_DIM = 128
BRANCHES = ((128, 1), (512, 4), (2048, 16))
CONV_WIDTH = 31
CONV_HALO = 32
RMS_EPS = 1e-6
LN_EPS = 1e-5
ADAM_LR = 0.001
ADAM_B1 = 0.9
ADAM_B2 = 0.999
ADAM_EPS = 1e-08
ADAM_WD = 0.01
ADAM_STEP = 10
N_CHIPS = 4
N_DEV = 8
MESH = pl.DeviceIdType.MESH


def _sigmoid(x):
    return 1.0 / (1.0 + jnp.exp(-x))


def _row_tile(rows, want):
    t = min(rows, want)
    assert rows % t == 0, (rows, want)
    return t


_DOT_DIMS = {"nn": ((1,), (0,)), "nt": ((1,), (1,)), "tn": ((0,), (0,))}


ANY_SPEC = pl.BlockSpec(memory_space=pl.ANY)


def _mm(name, mode, a, bs, epilogue, outs, *, m, n, k, extras=(), bm=1024, bn=512, bk=None, after=()):
    bm, bn = min(bm, m), min(bn, n)
    bk = k if bk is None else min(bk, k)
    assert m % bm == 0 and n % bn == 0 and k % bk == 0, (name, m, n, k, bm, bn, bk)
    nk = k // bk
    a_list = list(a) if isinstance(a, (list, tuple)) else [a]
    na, nb, ne, no = len(a_list), len(bs), len(extras), len(outs)
    assert na in (1, nb)

    if mode == "tn":
        a_spec = pl.BlockSpec((bk, bm), lambda i, j, kk: (kk, i))
    else:
        a_spec = pl.BlockSpec((bm, bk), lambda i, j, kk: (i, kk))

    def b_spec(lead, off):
        if mode == "nt":
            blk, idx = (bn, bk), (lambda i, j, kk: (j + off, kk))
        else:
            blk, idx = (bk, bn), (lambda i, j, kk: (kk, j + off))
        if lead is None:
            return pl.BlockSpec(blk, idx)
        return pl.BlockSpec((None,) + blk, lambda i, j, kk: (lead,) + idx(i, j, kk))

    def e_spec(kind, off):
        if kind == "mn":
            return pl.BlockSpec((bm, bn), lambda i, j, kk: (i, j + off))
        return pl.BlockSpec((1, bn), lambda i, j, kk: (0, j + off))

    nf = len(after)
    in_specs = [a_spec] * na + [b_spec(l, o) for _, l, o in bs] + [e_spec(kd, o) for _, kd, o in extras]
    in_specs += [ANY_SPEC] * nf
    out_specs = [pl.BlockSpec((bm, bn), lambda i, j, kk: (i, j)) for _ in outs]
    out_shape = [jax.ShapeDtypeStruct((m, n), dt) for (dt,) in outs]
    dims = (_DOT_DIMS[mode], ((), ()))

    def body(*refs):
        a_refs = refs[:na]
        b_refs = refs[na:na + nb]
        e_refs = refs[na + nb:na + nb + ne]
        o_refs = refs[na + nb + ne + nf:na + nb + ne + nf + no]
        acc_refs = refs[na + nb + ne + nf + no:]
        avs = [a_ref[...].astype(BF16) for a_ref in a_refs]
        prods = [lax.dot_general(avs[bi % na], b_ref[...].astype(BF16), dims, preferred_element_type=F32)
                 for bi, b_ref in enumerate(b_refs)]

        def finish(accs):
            res = epilogue(accs, [e_ref[...] for e_ref in e_refs])
            for o_ref, r in zip(o_refs, res):
                o_ref[...] = r.astype(o_ref.dtype)

        if nk == 1:
            finish(prods)
        else:
            kk = pl.program_id(2)

            @pl.when(kk == 0)
            def _():
                for acc_ref, p in zip(acc_refs, prods):
                    acc_ref[...] = p

            @pl.when(kk > 0)
            def _():
                for acc_ref, p in zip(acc_refs, prods):
                    acc_ref[...] += p

            @pl.when(kk == nk - 1)
            def _():
                finish([acc_ref[...] for acc_ref in acc_refs])

    scratch = [] if nk == 1 else [pltpu.VMEM((bm, bn), F32) for _ in bs]
    res = pl.pallas_call(
        body,
        name=name,
        grid=(m // bm, n // bn, nk),
        in_specs=in_specs,
        out_specs=out_specs,
        out_shape=out_shape,
        scratch_shapes=scratch,
        compiler_params=pltpu.CompilerParams(dimension_semantics=("parallel", "parallel", "arbitrary")),
    )(*a_list, *[b for b, _, _ in bs], *[e for e, _, _ in extras], *after)
    return res


def _rms_fwd(name, x, gains, after=()):
    s, d = x.shape
    ng = gains.shape[0]
    t = _row_tile(s, 256)
    nf = len(after)

    def body(x_ref, g_ref, *refs):
        o_refs = refs[nf:]
        xv = x_ref[...]
        r = lax.rsqrt(jnp.mean(xv * xv, axis=-1, keepdims=True) + RMS_EPS)
        xh = xv * r
        for gi, o_ref in enumerate(o_refs):
            o_ref[...] = (xh * g_ref[gi:gi + 1, :]).astype(o_ref.dtype)

    return pl.pallas_call(
        body,
        name=name,
        grid=(s // t,),
        in_specs=[pl.BlockSpec((t, d), lambda i: (i, 0)), pl.BlockSpec((ng, d), lambda i: (0, 0))] + [ANY_SPEC] * nf,
        out_specs=[pl.BlockSpec((t, d), lambda i: (i, 0)) for _ in range(ng)],
        out_shape=[jax.ShapeDtypeStruct((s, d), BF16) for _ in range(ng)],
        compiler_params=pltpu.CompilerParams(dimension_semantics=("parallel",)),
    )(x, gains, *after)


def _rms_bwd(name, x, gains, dns, dres):
    s, d = x.shape
    ng = gains.shape[0]
    t = _row_tile(s, 256)

    def body(x_ref, g_ref, dres_ref, *refs):
        dn_refs = refs[:ng]
        dx_ref, dxb_ref, dg_ref, cs_ref = refs[ng:]
        i = pl.program_id(0)
        xv = x_ref[...]
        r = lax.rsqrt(jnp.mean(xv * xv, axis=-1, keepdims=True) + RMS_EPS)
        xh = xv * r
        dx = dres_ref[...]
        dgs = []
        for gi in range(ng):
            dn = dn_refs[gi][...].astype(F32)
            dxh = dn * g_ref[gi:gi + 1, :]
            dgs.append(jnp.sum(dn * xh, axis=0, keepdims=True))
            dx = dx + r * (dxh - xh * jnp.mean(dxh * xh, axis=-1, keepdims=True))
        dx_ref[...] = dx
        dxb_ref[...] = dx.astype(BF16)
        dg = jnp.concatenate(dgs, axis=0) if ng > 1 else dgs[0]
        cs = jnp.sum(dx, axis=0, keepdims=True)

        @pl.when(i == 0)
        def _():
            dg_ref[...] = dg
            cs_ref[...] = cs

        @pl.when(i > 0)
        def _():
            dg_ref[...] += dg
            cs_ref[...] += cs

    row = pl.BlockSpec((t, d), lambda i: (i, 0))
    return pl.pallas_call(
        body,
        name=name,
        grid=(s // t,),
        in_specs=[row, pl.BlockSpec((ng, d), lambda i: (0, 0)), row] + [row] * ng,
        out_specs=[row, row, pl.BlockSpec((ng, d), lambda i: (0, 0)), pl.BlockSpec((1, d), lambda i: (0, 0))],
        out_shape=[
            jax.ShapeDtypeStruct((s, d), F32),
            jax.ShapeDtypeStruct((s, d), BF16),
            jax.ShapeDtypeStruct((ng, d), F32),
            jax.ShapeDtypeStruct((1, d), F32),
        ],
        compiler_params=pltpu.CompilerParams(dimension_semantics=("arbitrary",)),
    )(x, gains, dres, *dns)


def _final_loss(name, h, gain, target):
    s, d = h.shape
    t = _row_tile(s, 256)

    def body(h_ref, g_ref, t_ref, dh_ref, dhb_ref, dg_ref, ls_ref):
        i = pl.program_id(0)
        xv = h_ref[...]
        g = g_ref[...]
        r = lax.rsqrt(jnp.mean(xv * xv, axis=-1, keepdims=True) + RMS_EPS)
        xh = xv * r
        err = xh * g - t_ref[...]
        ls = jnp.sum(err * err, axis=0, keepdims=True) * (0.5 / d)
        dy = err * (1.0 / d)
        dxh = dy * g
        dg = jnp.sum(dy * xh, axis=0, keepdims=True)
        dx = r * (dxh - xh * jnp.mean(dxh * xh, axis=-1, keepdims=True))
        dh_ref[...] = dx
        dhb_ref[...] = dx.astype(BF16)

        @pl.when(i == 0)
        def _():
            dg_ref[...] = dg
            ls_ref[...] = ls

        @pl.when(i > 0)
        def _():
            dg_ref[...] += dg
            ls_ref[...] += ls

    row = pl.BlockSpec((t, d), lambda i: (i, 0))
    vec = pl.BlockSpec((1, d), lambda i: (0, 0))
    return pl.pallas_call(
        body,
        name=name,
        grid=(s // t,),
        in_specs=[row, vec, row],
        out_specs=[row, row, vec, vec],
        out_shape=[
            jax.ShapeDtypeStruct((s, d), F32),
            jax.ShapeDtypeStruct((s, d), BF16),
            jax.ShapeDtypeStruct((1, d), F32),
            jax.ShapeDtypeStruct((1, d), F32),
        ],
        compiler_params=pltpu.CompilerParams(dimension_semantics=("arbitrary",)),
    )(h, gain, target)


def _conv_tiles(s):
    t = _row_tile(s, 128)
    assert t % CONV_HALO == 0
    return t, t // CONV_HALO


def _dwconv_fwd(name, u, dw, dw_b, ln_g, ln_b):
    s, d = u.shape
    t, hb = _conv_tiles(s)
    w = dw.shape[0]
    lo = CONV_HALO - (w - 1)

    def body(cur_ref, prev_ref, dw_ref, dwb_ref, lg_ref, lb_ref, c_ref, sw_ref, cat_ref):
        i = pl.program_id(0)
        cat_ref[CONV_HALO:, :] = cur_ref[...]

        @pl.when(i == 0)
        def _():
            cat_ref[:CONV_HALO, :] = jnp.zeros((CONV_HALO, d), F32)

        @pl.when(i > 0)
        def _():
            cat_ref[:CONV_HALO, :] = prev_ref[...]

        acc = jnp.zeros((t, d), F32) + dwb_ref[...]
        for kk in range(w):
            acc = acc + dw_ref[kk:kk + 1, :] * cat_ref[lo + kk:lo + kk + t, :]
        c_ref[...] = acc
        mu = jnp.mean(acc, axis=-1, keepdims=True)
        cc = acc - mu
        var = jnp.mean(cc * cc, axis=-1, keepdims=True)
        ln = cc * lax.rsqrt(var + LN_EPS) * lg_ref[...] + lb_ref[...]
        sw_ref[...] = (ln * _sigmoid(ln)).astype(BF16)

    row = pl.BlockSpec((t, d), lambda i: (i, 0))
    prev = pl.BlockSpec((CONV_HALO, d), lambda i: (jnp.maximum(i * hb - 1, 0), 0))
    vec = pl.BlockSpec((1, d), lambda i: (0, 0))
    return pl.pallas_call(
        body,
        name=name,
        grid=(s // t,),
        in_specs=[row, prev, pl.BlockSpec((w, d), lambda i: (0, 0)), vec, vec, vec],
        out_specs=[row, row],
        out_shape=[jax.ShapeDtypeStruct((s, d), F32), jax.ShapeDtypeStruct((s, d), BF16)],
        scratch_shapes=[pltpu.VMEM((CONV_HALO + t, d), F32)],
        compiler_params=pltpu.CompilerParams(dimension_semantics=("parallel",)),
    )(u, u, dw, dw_b, ln_g, ln_b)


def _conv_ln_bwd(name, c, dsw, ln_g, ln_b):
    s, d = c.shape
    t = _row_tile(s, 256)

    def body(c_ref, dsw_ref, lg_ref, lb_ref, dc_ref, sums_ref):
        i = pl.program_id(0)
        cv = c_ref[...]
        g = lg_ref[...]
        mu = jnp.mean(cv, axis=-1, keepdims=True)
        cc = cv - mu
        rstd = lax.rsqrt(jnp.mean(cc * cc, axis=-1, keepdims=True) + LN_EPS)
        ch = cc * rstd
        ln = ch * g + lb_ref[...]
        sg = _sigmoid(ln)
        dln = dsw_ref[...] * (sg * (1.0 + ln * (1.0 - sg)))
        dch = dln * g
        dc = rstd * (dch - jnp.mean(dch, axis=-1, keepdims=True) - ch * jnp.mean(dch * ch, axis=-1, keepdims=True))
        dc_ref[...] = dc
        sums = jnp.concatenate(
            [
                jnp.sum(dln * ch, axis=0, keepdims=True),
                jnp.sum(dln, axis=0, keepdims=True),
                jnp.sum(dc, axis=0, keepdims=True),
                jnp.zeros((1, d), F32),
            ],
            axis=0,
        )

        @pl.when(i == 0)
        def _():
            sums_ref[...] = sums

        @pl.when(i > 0)
        def _():
            sums_ref[...] += sums

    row = pl.BlockSpec((t, d), lambda i: (i, 0))
    vec = pl.BlockSpec((1, d), lambda i: (0, 0))
    return pl.pallas_call(
        body,
        name=name,
        grid=(s // t,),
        in_specs=[row, row, vec, vec],
        out_specs=[row, pl.BlockSpec((4, d), lambda i: (0, 0))],
        out_shape=[jax.ShapeDtypeStruct((s, d), F32), jax.ShapeDtypeStruct((4, d), F32)],
        compiler_params=pltpu.CompilerParams(dimension_semantics=("arbitrary",)),
    )(c, dsw, ln_g, ln_b)


def _conv_dw_bwd(name, dc, u, a, gt, dw):
    s, d = dc.shape
    t, hb = _conv_tiles(s)
    w = dw.shape[0]
    lo = CONV_HALO - (w - 1)
    nt = s // t

    def body(dc_ref, dcn_ref, u_ref, up_ref, a_ref, gt_ref, dw_ref, dpre_ref, ddw_ref, db_ref, dcat_ref, ucat_ref):
        i = pl.program_id(0)
        dcat_ref[:t, :] = dc_ref[...]
        ucat_ref[CONV_HALO:, :] = u_ref[...]

        @pl.when(i == nt - 1)
        def _():
            dcat_ref[t:, :] = jnp.zeros((CONV_HALO, d), F32)

        @pl.when(i < nt - 1)
        def _():
            dcat_ref[t:, :] = dcn_ref[...]

        @pl.when(i == 0)
        def _():
            ucat_ref[:CONV_HALO, :] = jnp.zeros((CONV_HALO, d), F32)

        @pl.when(i > 0)
        def _():
            ucat_ref[:CONV_HALO, :] = up_ref[...]

        dcv = dc_ref[...]
        du = jnp.zeros((t, d), F32)
        rows = []
        for kk in range(w):
            sh = w - 1 - kk
            du = du + dw_ref[kk:kk + 1, :] * dcat_ref[sh:sh + t, :]
            rows.append(jnp.sum(dcv * ucat_ref[lo + kk:lo + kk + t, :], axis=0, keepdims=True))
        ddw = jnp.concatenate(rows, axis=0)
        av = a_ref[...].astype(F32)
        sg = _sigmoid(gt_ref[...].astype(F32))
        da = du * sg
        dgt = du * av * sg * (1.0 - sg)
        dpre_ref[:, :d] = da.astype(BF16)
        dpre_ref[:, d:] = dgt.astype(BF16)
        db = jnp.concatenate([jnp.sum(da, axis=0, keepdims=True), jnp.sum(dgt, axis=0, keepdims=True)], axis=1)

        @pl.when(i == 0)
        def _():
            ddw_ref[...] = ddw
            db_ref[...] = db

        @pl.when(i > 0)
        def _():
            ddw_ref[...] += ddw
            db_ref[...] += db

    row = pl.BlockSpec((t, d), lambda i: (i, 0))
    nxt = pl.BlockSpec((CONV_HALO, d), lambda i: (jnp.minimum((i + 1) * hb, s // CONV_HALO - 1), 0))
    prev = pl.BlockSpec((CONV_HALO, d), lambda i: (jnp.maximum(i * hb - 1, 0), 0))
    return pl.pallas_call(
        body,
        name=name,
        grid=(nt,),
        in_specs=[row, nxt, row, prev, row, row, pl.BlockSpec((w, d), lambda i: (0, 0))],
        out_specs=[
            pl.BlockSpec((t, 2 * d), lambda i: (i, 0)),
            pl.BlockSpec((w, d), lambda i: (0, 0)),
            pl.BlockSpec((1, 2 * d), lambda i: (0, 0)),
        ],
        out_shape=[
            jax.ShapeDtypeStruct((s, 2 * d), BF16),
            jax.ShapeDtypeStruct((w, d), F32),
            jax.ShapeDtypeStruct((1, 2 * d), F32),
        ],
        scratch_shapes=[pltpu.VMEM((t + CONV_HALO, d), F32), pltpu.VMEM((CONV_HALO + t, d), F32)],
        compiler_params=pltpu.CompilerParams(dimension_semantics=("arbitrary",)),
    )(dc, dc, u, u, a, gt, dw)


def _alibi_slopes(n_heads):
    h = jnp.arange(1, n_heads + 1, dtype=F32)
    return jnp.exp2(-8.0 * h / n_heads)


def _band_masks(bq):
    qi = lax.broadcasted_iota(jnp.int32, (bq, bq), 0)
    kj = lax.broadcasted_iota(jnp.int32, (bq, bq), 1)
    return qi - kj, qi - kj + bq


def _attn_fwd_branch(name, q, k, v, slopes, dil, bq, acc):
    s, dm = q.shape
    nh = dm // HEAD_DIM
    ll = s // dil
    assert ll % bq == 0
    nblk = ll // bq
    scale = HEAD_DIM ** -0.5
    view = lambda z: z.reshape(ll, dil * dm)
    first = acc is None

    def body(sl_ref, q_ref, kc_ref, kp_ref, vc_ref, vp_ref, *refs):
        if first:
            o_ref, l_ref = refs
        else:
            oa_ref, la_ref, o_ref, l_ref = refs
        nb = pl.program_id(1)
        jc, jp = _band_masks(bq)
        ok_c = jc >= 0
        ok_p = (jp <= bq) & (nb > 0)
        jcf = jc.astype(F32) * float(dil)
        jpf = jp.astype(F32) * float(dil)

        def head(h, carry):
            hs = pl.ds(pl.multiple_of(h * HEAD_DIM, HEAD_DIM), HEAD_DIM)
            slope = sl_ref[h]
            qh = q_ref[:, hs]
            dn = (((1,), (1,)), ((), ()))
            sc = lax.dot_general(qh, kc_ref[:, hs], dn, preferred_element_type=F32) * scale - slope * jcf
            sp = lax.dot_general(qh, kp_ref[:, hs], dn, preferred_element_type=F32) * scale - slope * jpf
            sc = jnp.where(ok_c, sc, -1e30)
            sp = jnp.where(ok_p, sp, -1e30)
            mx = jnp.maximum(jnp.max(sc, axis=-1, keepdims=True), jnp.max(sp, axis=-1, keepdims=True))
            pc = jnp.exp(sc - mx)
            pp = jnp.exp(sp - mx)
            den = jnp.sum(pc, axis=-1, keepdims=True) + jnp.sum(pp, axis=-1, keepdims=True)
            inv = 1.0 / den
            o = jnp.dot((pc * inv).astype(BF16), vc_ref[:, hs], preferred_element_type=F32)
            o = o + jnp.dot((pp * inv).astype(BF16), vp_ref[:, hs], preferred_element_type=F32)
            lse = jnp.broadcast_to(mx + jnp.log(den), (bq, HEAD_DIM))
            if not first:
                lo = la_ref[:, hs]
                mm = jnp.maximum(lo, lse)
                ea = jnp.exp(lo - mm)
                eb = jnp.exp(lse - mm)
                tot = ea + eb
                o = (ea * oa_ref[:, hs] + eb * o) / tot
                lse = mm + jnp.log(tot)
            o_ref[:, hs] = o
            l_ref[:, hs] = lse
            return carry

        lax.fori_loop(0, nh, head, 0)

    cur = pl.BlockSpec((bq, dm), lambda r, nb: (nb, r))
    prv = pl.BlockSpec((bq, dm), lambda r, nb: (jnp.maximum(nb - 1, 0), r))
    smem = pl.BlockSpec(memory_space=pltpu.SMEM)
    ins = [slopes, view(q), view(k), view(k), view(v), view(v)]
    in_specs = [smem, cur, cur, prv, cur, prv]
    if not first:
        ins += [view(acc[0]), view(acc[1])]
        in_specs += [cur, cur]
    o, lse = pl.pallas_call(
        body,
        name=name,
        grid=(dil, nblk),
        in_specs=in_specs,
        out_specs=[cur, cur],
        out_shape=[jax.ShapeDtypeStruct((ll, dil * dm), F32), jax.ShapeDtypeStruct((ll, dil * dm), F32)],
        compiler_params=pltpu.CompilerParams(dimension_semantics=("parallel", "parallel")),
    )(*ins)
    return o.reshape(s, dm), lse.reshape(s, dm)


def _attn_bwd_branch(name, q, k, v, o, lse, do, slopes, dil, bq, acc):
    s, dm = q.shape
    nh = dm // HEAD_DIM
    ll = s // dil
    nblk = ll // bq
    scale = HEAD_DIM ** -0.5
    view = lambda z: z.reshape(ll, dil * dm)
    first = acc is None

    def body(sl_ref, qc_ref, qn_ref, kc_ref, kp_ref, vc_ref, vp_ref, oc_ref, on_ref, lc_ref, ln_ref, dc_ref, dn_ref,
             *refs):
        if first:
            dq_ref, dk_ref, dv_ref = refs
        else:
            aq_ref, ak_ref, av_ref, dq_ref, dk_ref, dv_ref = refs
        nb = pl.program_id(1)
        jc, jp = _band_masks(bq)
        ok_c = jc >= 0
        ok_p = (jp <= bq) & (nb > 0)
        ok_n = (jp <= bq) & (nb < nblk - 1)
        jcf = jc.astype(F32) * float(dil)
        jpf = jp.astype(F32) * float(dil)
        nt_dims = (((1,), (1,)), ((), ()))
        tn_dims = (((0,), (0,)), ((), ()))

        def head(h, carry):
            hs = pl.ds(pl.multiple_of(h * HEAD_DIM, HEAD_DIM), HEAD_DIM)
            slope = sl_ref[h]
            qc, qn = qc_ref[:, hs], qn_ref[:, hs]
            kc, kp = kc_ref[:, hs], kp_ref[:, hs]
            vc, vp = vc_ref[:, hs], vp_ref[:, hs]
            doc, don = dc_ref[:, hs], dn_ref[:, hs]
            lc, ln = lc_ref[:, hs][:, :1], ln_ref[:, hs][:, :1]
            dl_c = jnp.sum(doc.astype(F32) * oc_ref[:, hs], axis=-1, keepdims=True)
            dl_n = jnp.sum(don.astype(F32) * on_ref[:, hs], axis=-1, keepdims=True)

            def block(qq, kk_, vv, dd, ll_, dl, bias, ok):
                sc = lax.dot_general(qq, kk_, nt_dims, preferred_element_type=F32) * scale - slope * bias
                p = jnp.where(ok, jnp.exp(jnp.where(ok, sc, -1e30) - ll_), 0.0)
                dp = lax.dot_general(dd, vv, nt_dims, preferred_element_type=F32)
                ds = (p * (dp - dl) * scale).astype(BF16)
                return p.astype(BF16), ds

            p_a, ds_a = block(qc, kc, vc, doc, lc, dl_c, jcf, ok_c)
            _, ds_b = block(qc, kp, vp, doc, lc, dl_c, jpf, ok_p)
            p_c, ds_c = block(qn, kc, vc, don, ln, dl_n, jpf, ok_n)
            dq = jnp.dot(ds_a, kc, preferred_element_type=F32) + jnp.dot(ds_b, kp, preferred_element_type=F32)
            dk = lax.dot_general(ds_a, qc, tn_dims, preferred_element_type=F32)
            dk = dk + lax.dot_general(ds_c, qn, tn_dims, preferred_element_type=F32)
            dv = lax.dot_general(p_a, doc, tn_dims, preferred_element_type=F32)
            dv = dv + lax.dot_general(p_c, don, tn_dims, preferred_element_type=F32)
            if not first:
                dq = dq + aq_ref[:, hs]
                dk = dk + ak_ref[:, hs]
                dv = dv + av_ref[:, hs]
            dq_ref[:, hs] = dq
            dk_ref[:, hs] = dk
            dv_ref[:, hs] = dv
            return carry

        lax.fori_loop(0, nh, head, 0)

    cur = pl.BlockSpec((bq, dm), lambda r, nb: (nb, r))
    prv = pl.BlockSpec((bq, dm), lambda r, nb: (jnp.maximum(nb - 1, 0), r))
    nxt = pl.BlockSpec((bq, dm), lambda r, nb: (jnp.minimum(nb + 1, nblk - 1), r))
    smem = pl.BlockSpec(memory_space=pltpu.SMEM)
    ins = [slopes, view(q), view(q), view(k), view(k), view(v), view(v), view(o), view(o), view(lse), view(lse),
           view(do), view(do)]
    in_specs = [smem, cur, nxt, cur, prv, cur, prv, cur, nxt, cur, nxt, cur, nxt]
    if not first:
        ins += [view(z) for z in acc]
        in_specs += [cur, cur, cur]
    res = pl.pallas_call(
        body,
        name=name,
        grid=(dil, nblk),
        in_specs=in_specs,
        out_specs=[cur, cur, cur],
        out_shape=[jax.ShapeDtypeStruct((ll, dil * dm), F32) for _ in range(3)],
        compiler_params=pltpu.CompilerParams(dimension_semantics=("parallel", "parallel")),
    )(*ins)
    return tuple(z.reshape(s, dm) for z in res)


def _ep_id(accs, ex):
    return [accs[0]]


def _ep_all(accs, ex):
    return list(accs)


def _ep_sum(accs, ex):
    return [accs[0] + accs[1]]


def _ep_add(accs, ex):
    return [accs[0] + ex[0].astype(F32)]


def _ep_bias_res(accs, ex):
    return [accs[0] + ex[0] + ex[1]]


def _ep_glu(accs, ex):
    a = accs[0] + ex[0]
    gt = accs[1] + ex[1]
    return [a * _sigmoid(gt), a, gt]


def _ep_swiglu(accs, ex):
    g, u = accs
    return [g, u, g * _sigmoid(g) * u]


def _ep_swiglu_bwd(accs, ex):
    dact = accs[0]
    g = ex[0].astype(F32)
    u = ex[1].astype(F32)
    sg = _sigmoid(g)
    return [dact * u * (sg * (1.0 + g * (1.0 - sg))), dact * g * sg]


def _ffn_fwd(tag, h, gain, wg, wu, wd):
    s, d = h.shape
    f = wg.shape[-1]
    (n,) = _rms_fwd(f"{tag}_norm", h, gain)
    g, u, act = _mm(f"{tag}_gate_up", "nn", n, [(wg, None, 0), (wu, None, 0)], _ep_swiglu,
                    [(BF16,), (BF16,), (BF16,)], m=s, n=f, k=d)
    (out,) = _mm(f"{tag}_down", "nn", act, [(wd, None, 0)], _ep_add, [(F32,)], m=s, n=d, k=f,
                 extras=[(h, "mn", 0)], bn=256)
    return out, (n, g, u, act)


def _ffn_bwd(tag, h_in, gain, wg, wu, wd, saved, dh, dhb, after, emit):
    s, d = h_in.shape
    f = wg.shape[-1]
    n, g, u, act = saved
    dg, du = _mm(f"{tag}_bwd_dact", "nt", dhb, [(wd, None, 0)], _ep_swiglu_bwd, [(BF16,), (BF16,)],
                 m=s, n=f, k=d, extras=[(g, "mn", 0), (u, "mn", 0)], after=after)
    (dwd,) = _mm(f"{tag}_bwd_dwd", "tn", act, [(dhb, None, 0)], _ep_id, [(BF16,)], m=f, n=d, k=s,
                 bm=f // 4)
    dwg, dwu = _mm(f"{tag}_bwd_dwgu", "tn", n, [(dg, None, 0), (du, None, 0)], _ep_all, [(BF16,), (BF16,)],
                   m=d, n=f, k=s)
    pin = emit(dict(gate=dwg, up=dwu, down=dwd))
    (dn,) = _mm(f"{tag}_bwd_dn", "nt", [dg, du], [(wg, None, 0), (wu, None, 0)], _ep_sum, [(F32,)],
                m=s, n=d, k=f, bm=512, bn=256, after=pin)
    dx, dxb, dgain, cs = _rms_bwd(f"{tag}_bwd_norm", h_in, gain, [dn], dh)
    return dx, dxb, dgain, cs


def _local_step(x, target, w, fetch, emit, after=()):
    s, d = x.shape
    nh = d // HEAD_DIM
    bq = BRANCHES[0][0] // BRANCHES[0][1]
    assert all(win // dil == bq for win, dil in BRANCHES)
    slopes = _alibi_slopes(nh)
    nd = d // 512 if d >= 512 else 1
    bn = d // nd

    (n1,) = _rms_fwd("a_norm", x, w["a_norm_g"], after=after)
    glu, a, gt = _mm("conv_pw1_glu", "nn", n1, [(w["conv_w1"], None, 0), (w["conv_w1"], None, nd)], _ep_glu,
                     [(F32,), (BF16,), (BF16,)], m=s, n=d, k=d, bn=bn,
                     extras=[(w["conv_b1"], "n", 0), (w["conv_b1"], "n", nd)])
    c, sw = _dwconv_fwd("conv_dw_ln", glu, w["conv_dw"], w["conv_dw_b"], w["conv_ln_g"], w["conv_ln_b"])
    w_conv2 = fetch("conv2", sw)["conv_w2"]
    (h1,) = _mm("conv_pw2", "nn", sw, [(w_conv2, None, 0)], _ep_bias_res, [(F32,)], m=s, n=d, k=d,
                extras=[(w["conv_b2"], "n", 0), (x, "mn", 0)])
    wf0 = fetch("ffn0", h1)
    h2, ffn0 = _ffn_fwd("ffn0", h1, w["ffn_norm_g"][0:1], wf0["gate"], wf0["up"], wf0["down"])
    wa = fetch("attn", h2)
    kvn, qn = _rms_fwd("kvq_norm", h2, jnp.concatenate([w["kv_norm_g"], w["b_norm_g"]], axis=0))
    k, v = _mm("kv_proj", "nn", kvn, [(wa["w_k"], None, 0), (wa["w_v"], None, 0)], _ep_all, [(BF16,), (BF16,)],
               m=s, n=d, k=d)
    (q,) = _mm("q_proj", "nn", qn, [(wa["w_q"], None, 0)], _ep_id, [(BF16,)], m=s, n=d, k=d)
    acc = None
    for bi, (win, dil) in enumerate(BRANCHES):
        acc = _attn_fwd_branch(f"attn_fwd_{bi}", q, k, v, slopes, dil, bq, acc)
    att, lse = acc
    (h3,) = _mm("o_proj", "nn", att, [(wa["w_o"], None, 0)], _ep_add, [(F32,)], m=s, n=d, k=d,
                extras=[(h2, "mn", 0)])
    wf1 = fetch("ffn1", h3)
    h4, ffn1 = _ffn_fwd("ffn1", h3, w["ffn_norm_g"][1:2], wf1["gate"], wf1["up"], wf1["down"])
    dh4, dh4b, d_final_g, loss_cols = _final_loss("final_loss", h4, w["final_norm_g"], target)

    g = {}
    ga = {}
    dh3, dh3b, dgain1, _ = _ffn_bwd("ffn1", h3, w["ffn_norm_g"][1:2], wf1["gate"], wf1["up"], wf1["down"], ffn1,
                                    dh4, dh4b, (), functools.partial(emit, "ffn1"))
    (datt,) = _mm("o_proj_bwd_dx", "nt", dh3b, [(wa["w_o"], None, 0)], _ep_id, [(BF16,)], m=s, n=d, k=d)
    (ga["w_o"],) = _mm("o_proj_bwd_dw", "tn", att, [(dh3b, None, 0)], _ep_id, [(BF16,)], m=d, n=d, k=s)
    dacc = None
    for bi, (win, dil) in enumerate(BRANCHES):
        dacc = _attn_bwd_branch(f"attn_bwd_{bi}", q, k, v, att, lse, datt, slopes, dil, bq, dacc)
    dq, dk, dv = dacc
    (ga["w_q"],) = _mm("q_proj_bwd_dw", "tn", qn, [(dq, None, 0)], _ep_id, [(BF16,)], m=d, n=d, k=s)
    ga["w_k"], ga["w_v"] = _mm("kv_proj_bwd_dw", "tn", kvn, [(dk, None, 0), (dv, None, 0)], _ep_all,
                               [(BF16,), (BF16,)], m=d, n=d, k=s, bm=512)
    pin = emit("attn", ga)
    (dqn,) = _mm("q_proj_bwd_dx", "nt", dq, [(wa["w_q"], None, 0)], _ep_id, [(F32,)], m=s, n=d, k=d, after=pin)
    (dkvn,) = _mm("kv_proj_bwd_dx", "nt", [dk, dv], [(wa["w_k"], None, 0), (wa["w_v"], None, 0)], _ep_sum, [(F32,)],
                  m=s, n=d, k=d, bm=512)
    dh2, dh2b, dg_kvq, _ = _rms_bwd("kvq_norm_bwd", h2, jnp.concatenate([w["kv_norm_g"], w["b_norm_g"]], axis=0),
                                    [dkvn, dqn], dh3)
    dh1, dh1b, dgain0, cs_h1 = _ffn_bwd("ffn0", h1, w["ffn_norm_g"][0:1], wf0["gate"], wf0["up"], wf0["down"], ffn0,
                                        dh2, dh2b, (), functools.partial(emit, "ffn0"))
    (dsw,) = _mm("conv_pw2_bwd_dx", "nt", dh1b, [(w_conv2, None, 0)], _ep_id, [(F32,)], m=s, n=d, k=d)
    (g["conv_w2"],) = _mm("conv_pw2_bwd_dw", "tn", sw, [(dh1b, None, 0)], _ep_id, [(BF16,)], m=d, n=d, k=s)
    dc, ln_sums = _conv_ln_bwd("conv_ln_bwd", c, dsw, w["conv_ln_g"], w["conv_ln_b"])
    dpre, ddw, db1 = _conv_dw_bwd("conv_dw_bwd", dc, glu, a, gt, w["conv_dw"])
    (g["conv_w1"],) = _mm("conv_pw1_bwd_dw", "tn", n1, [(dpre, None, 0)], _ep_id, [(BF16,)], m=d, n=2 * d, k=s)
    (dn1,) = _mm("conv_pw1_bwd_dx", "nt", dpre, [(w["conv_w1"], None, 0)], _ep_id, [(F32,)], m=s, n=d, k=2 * d)
    dx, _, d_a_norm, _ = _rms_bwd("a_norm_bwd", x, w["a_norm_g"], [dn1], dh1)

    g.update(
        a_norm_g=d_a_norm, conv_b1=db1, conv_dw=ddw, conv_dw_b=ln_sums[2:3], conv_ln_g=ln_sums[0:1],
        conv_ln_b=ln_sums[1:2], conv_b2=cs_h1, kv_norm_g=dg_kvq[0:1], b_norm_g=dg_kvq[1:2],
        ffn_norm_g=jnp.concatenate([dgain0, dgain1], axis=0), final_norm_g=d_final_g,
    )
    return loss_cols, dx, g


HBM_SPEC = pl.BlockSpec(memory_space=pltpu.HBM)


def _mesh_place():
    x, y, c = lax.axis_index("x"), lax.axis_index("y"), lax.axis_index("c")
    chips = [(1 - x, y), (x, 1 - y), (1 - x, 1 - y)]
    return x, y, c, chips


def _shard_view(ref, kind, s, half=None):
    rows, cols = ref.shape
    if kind == "col":
        cw = cols // N_CHIPS
        if half is None:
            return ref.at[pl.ds(0, rows), pl.ds(s * cw, cw)]
        return ref.at[pl.ds(half * (rows // 2), rows // 2), pl.ds(s * cw, cw)]
    r = rows // N_CHIPS
    if half is None:
        return ref.at[pl.ds(s * r, r), pl.ds(0, cols)]
    return ref.at[pl.ds(s * r + half * (r // 2), r // 2), pl.ds(0, cols)]


def _full_shape(shard, kind):
    r, cw = shard.shape[-2:]
    return (r, cw * N_CHIPS) if kind == "col" else (r * N_CHIPS, cw)


def _gather_weights(shards, kinds):
    nt = len(shards)
    fulls = [jax.ShapeDtypeStruct(_full_shape(sh, kind), sh.dtype) for sh, kind in zip(shards, kinds)]

    def body(*refs):
        src = refs[:nt]
        dst = refs[nt:2 * nt]
        send, recv, fsend, frecv, local = refs[2 * nt:]
        x, y, c, chips = _mesh_place()
        s = 2 * x + y
        sib = (x, y, 1 - c)

        def half_of_shard(t):
            r, cw = src[t].shape
            return src[t].at[pl.ds(c * (r // 2), r // 2), pl.ds(0, cw)]

        locals_ = [pltpu.make_async_copy(src[t], _shard_view(dst[t], kinds[t], s), local.at[t]) for t in range(nt)]
        for cp in locals_:
            cp.start()
        sends = []
        for t in range(nt):
            for j, chip in enumerate(chips):
                cp = pltpu.make_async_remote_copy(
                    src_ref=half_of_shard(t), dst_ref=_shard_view(dst[t], kinds[t], s, c),
                    send_sem=send.at[t, j], recv_sem=recv.at[t, j], device_id=(*chip, c), device_id_type=MESH)
                cp.start()
                sends.append(cp)
        for t in range(nt):
            for j, (px, py) in enumerate(chips):
                landed = _shard_view(dst[t], kinds[t], 2 * px + py, c)
                pltpu.make_async_remote_copy(
                    src_ref=half_of_shard(t), dst_ref=landed, send_sem=send.at[t, j], recv_sem=recv.at[t, j],
                    device_id=(px, py, c), device_id_type=MESH).wait_recv()
                cp = pltpu.make_async_remote_copy(
                    src_ref=landed, dst_ref=landed, send_sem=fsend.at[t, j], recv_sem=frecv.at[t, j],
                    device_id=sib, device_id_type=MESH)
                cp.start()
                sends.append(cp)
        for t in range(nt):
            for j, (px, py) in enumerate(chips):
                other = _shard_view(dst[t], kinds[t], 2 * px + py, 1 - c)
                pltpu.make_async_remote_copy(
                    src_ref=other, dst_ref=other, send_sem=fsend.at[t, j], recv_sem=frecv.at[t, j],
                    device_id=sib, device_id_type=MESH).wait_recv()
        for cp in sends:
            cp.wait_send()
        for cp in locals_:
            cp.wait()

    return pl.pallas_call(
        body,
        name="gather_weights",
        in_specs=[HBM_SPEC] * nt,
        out_specs=[HBM_SPEC] * nt,
        out_shape=fulls,
        scratch_shapes=[pltpu.SemaphoreType.DMA((nt, 3))] * 4 + [pltpu.SemaphoreType.DMA((nt,))],
    )(*shards)


def _row_blocks(rows, want=512):
    nb = 1
    while rows // nb > want or rows % nb or (rows // nb) % 16:
        nb += 1
        if nb > rows:
            return rows, 1
    return rows // nb, nb


def _place_own(name, src, kind, land, ids, into_slot, lead=0):
    if into_slot:
        _, r, cw = land.shape
    else:
        r, cw = src.shape[-2:]
    tr, nb = _row_blocks(r)
    if not into_slot:
        if src.ndim == 3:
            src_spec = pl.BlockSpec((None, tr, cw), lambda i, ids_ref: (lead, i, 0))
        else:
            src_spec = pl.BlockSpec((tr, cw), lambda i, ids_ref: (i, 0))
        if kind == "col":
            dst_spec = pl.BlockSpec((tr, cw), lambda i, ids_ref: (i, ids_ref[0]))
        else:
            dst_spec = pl.BlockSpec((tr, cw), lambda i, ids_ref: (ids_ref[0] * nb + i, 0))
    else:
        if kind == "col":
            src_spec = pl.BlockSpec((tr, cw), lambda i, ids_ref: (i, ids_ref[0]))
        elif kind == "row":
            src_spec = pl.BlockSpec((tr, cw), lambda i, ids_ref: (ids_ref[0] * nb + i, 0))
        else:
            src_spec = pl.BlockSpec((tr, cw), lambda i, ids_ref: (i, 0))
        dst_spec = pl.BlockSpec((None, tr, cw), lambda i, ids_ref: (ids_ref[1], i, 0))

    def body(ids_ref, s_ref, o_ref):
        o_ref[...] = s_ref[...].astype(o_ref.dtype)

    return pl.pallas_call(
        body,
        name=name,
        grid_spec=pltpu.PrefetchScalarGridSpec(num_scalar_prefetch=1, grid=(nb,), in_specs=[src_spec],
                                               out_specs=dst_spec),
        out_shape=land,
        compiler_params=pltpu.CompilerParams(dimension_semantics=("parallel",)),
    )(ids, src)


SEM_SPEC = pl.BlockSpec(memory_space=pltpu.SEMAPHORE)
SIDE_EFFECT = pltpu.SideEffectType.DATAFLOW_SIDE_EFFECTING


def _copies_start(name, srcs, lands, copies, n_sems, after):
    ns, nl, nf = len(srcs), len(lands), len(after)

    def body(*refs):
        src, land = refs[:ns], refs[ns:ns + nl]
        send, recv = refs[ns + nl + nf], refs[ns + nl + nf + 1]
        pin = refs[-1]
        for cp in copies(src, land, send, recv, _mesh_place()):
            cp.start()
        pin[...] = jnp.zeros_like(pin)

    arrs = list(srcs) + list(lands)
    res = pl.pallas_call(
        body,
        name=name,
        in_specs=[HBM_SPEC] * (ns + nl) + [ANY_SPEC] * nf,
        out_specs=[SEM_SPEC, SEM_SPEC] + [HBM_SPEC] * (ns + nl) + [pl.BlockSpec(memory_space=pltpu.VMEM)],
        out_shape=[pltpu.SemaphoreType.DMA((n_sems,)), pltpu.SemaphoreType.DMA((n_sems,))]
        + [pltpu.HBM(a.shape, a.dtype) for a in arrs] + [jax.ShapeDtypeStruct((8, 128), F32)],
        input_output_aliases={i: 2 + i for i in range(ns + nl)},
        compiler_params=pltpu.CompilerParams(has_side_effects=SIDE_EFFECT),
    )(*[pltpu.with_memory_space_constraint(a, pltpu.HBM) for a in arrs], *after)
    return res[0], res[1], list(res[2:2 + ns]), list(res[2 + ns:2 + ns + nl]), res[-1]


def _copies_wait(name, send, recv, srcs, lands, copies, after):
    ns, nl, nf = len(srcs), len(lands), len(after)

    def body(*refs):
        src, land = refs[:ns], refs[ns:ns + nl]
        send_sems, recv_sems = refs[ns + nl], refs[ns + nl + 1]
        cps = copies(src, land, send_sems, recv_sems, _mesh_place())
        for cp in cps:
            cp.wait_send()
        for cp in cps:
            cp.wait_recv()

    arrs = list(srcs) + list(lands)
    res = pl.pallas_call(
        body,
        name=name,
        in_specs=[HBM_SPEC] * (ns + nl) + [SEM_SPEC, SEM_SPEC] + [ANY_SPEC] * nf,
        out_specs=[HBM_SPEC] * (ns + nl),
        out_shape=[pltpu.HBM(a.shape, a.dtype) for a in arrs],
        input_output_aliases={i: i for i in range(ns + nl)},
        compiler_params=pltpu.CompilerParams(has_side_effects=SIDE_EFFECT),
    )(*arrs, send, recv, *after)
    return list(res[ns:])


def _gather_copies(kinds):
    def copies(src, land, send, recv, place):
        x, y, c, chips = place
        s = 2 * x + y
        return [
            pltpu.make_async_remote_copy(
                src_ref=_shard_view(land[t], kinds[t], s), dst_ref=_shard_view(land[t], kinds[t], s),
                send_sem=send.at[3 * t + j], recv_sem=recv.at[3 * t + j], device_id=(px, py, c), device_id_type=MESH)
            for t in range(len(kinds)) for j, (px, py) in enumerate(chips)
        ]

    return copies


def _grad_part(ref, kind, s):
    return ref if kind == "all" else _shard_view(ref, kind, s)


def _grad_copies(kinds):
    def peers(place):
        x, y, c, chips = place
        return [(x, y, 1 - c)] + [(px, py, c) for px, py in chips]

    def copies(src, land, send, recv, place):
        x, y, c, chips = place
        me = 4 * x + 2 * y + c
        return [
            pltpu.make_async_remote_copy(
                src_ref=_grad_part(src[t], kinds[t], 2 * px + py), dst_ref=land[t].at[me],
                send_sem=send.at[GRAD_PEERS * t + k], recv_sem=recv.at[GRAD_PEERS * t + k], device_id=(px, py, pc),
                device_id_type=MESH)
            for t in range(len(kinds)) for k, (px, py, pc) in enumerate(peers(place))
        ]

    return copies


def _pass_on(lands):
    nt = len(lands)

    def body(*refs):
        buf = refs[nt:2 * nt]
        send, recv = refs[2 * nt:]
        x, y, c, chips = _mesh_place()
        sib = (x, y, 1 - c)
        sends = []
        for t in range(nt):
            for j, (px, py) in enumerate(chips):
                mine = buf[t].at[4 * px + 2 * py + c]
                cp = pltpu.make_async_remote_copy(src_ref=mine, dst_ref=mine, send_sem=send.at[t, j],
                                                  recv_sem=recv.at[t, j], device_id=sib, device_id_type=MESH)
                cp.start()
                sends.append(cp)
        for t in range(nt):
            for j, (px, py) in enumerate(chips):
                theirs = buf[t].at[4 * px + 2 * py + (1 - c)]
                pltpu.make_async_remote_copy(src_ref=theirs, dst_ref=theirs, send_sem=send.at[t, j],
                                             recv_sem=recv.at[t, j], device_id=sib, device_id_type=MESH).wait_recv()
        for cp in sends:
            cp.wait_send()

    return pl.pallas_call(
        body,
        name="grads_pass_on",
        in_specs=[HBM_SPEC] * nt,
        out_specs=[HBM_SPEC] * nt,
        out_shape=[jax.ShapeDtypeStruct(a.shape, a.dtype) for a in lands],
        input_output_aliases={i: i for i in range(nt)},
        scratch_shapes=[pltpu.SemaphoreType.DMA((nt, 3))] * 2,
    )(*lands)


GRAD_PEERS = 4


def _land_shape(grad, kind):
    rows, cols = grad.shape
    if kind == "col":
        return (N_DEV, rows, cols // N_CHIPS)
    if kind == "row":
        return (N_DEV, rows // N_CHIPS, cols)
    return (N_DEV, rows, cols)


def _adamw_reduce(name, contrib, w, m, v, layer=None, prev=None):
    rows, cols = w.shape[-2:]
    t = 128 if rows % 128 == 0 else rows
    c1 = 1.0 - ADAM_B1 ** ADAM_STEP
    c2 = 1.0 - ADAM_B2 ** ADAM_STEP

    n_prev = 0 if prev is None else 4

    def body(c_ref, w_ref, m_ref, v_ref, *refs):
        g_ref, d_ref, nm_ref, nv_ref = refs[n_prev:]
        g = c_ref[0].astype(F32)
        for q in range(1, N_DEV):
            g = g + c_ref[q].astype(F32)
        nm = ADAM_B1 * m_ref[...] + (1.0 - ADAM_B1) * g
        nv = ADAM_B2 * v_ref[...] + (1.0 - ADAM_B2) * (g * g)
        g_ref[...] = g
        nm_ref[...] = nm
        nv_ref[...] = nv
        d_ref[...] = -ADAM_LR * ((nm / c1) / (jnp.sqrt(nv / c2) + ADAM_EPS) + ADAM_WD * w_ref[...])

    if layer is None:
        blk = pl.BlockSpec((t, cols), lambda i: (i, 0))
    else:
        blk = pl.BlockSpec((None, t, cols), lambda i: (layer, i, 0))
    return pl.pallas_call(
        body,
        name=name,
        grid=(rows // t,),
        in_specs=[pl.BlockSpec((N_DEV, t, cols), lambda i: (0, i, 0)), blk, blk, blk] + [ANY_SPEC] * n_prev,
        out_specs=[blk] * 4,
        out_shape=[jax.ShapeDtypeStruct(w.shape, F32)] * 4,
        input_output_aliases={4 + i: i for i in range(n_prev)},
        compiler_params=pltpu.CompilerParams(dimension_semantics=("parallel",)),
    )(contrib, w, m, v, *(prev or ()))


WEIGHT_NAMES = ("a_norm_g", "conv_w1", "conv_b1", "conv_dw", "conv_dw_b", "conv_ln_g", "conv_ln_b", "conv_w2",
                "conv_b2", "kv_norm_g", "w_k", "w_v", "b_norm_g", "w_q", "w_o", "ffn_norm_g", "ffn_w_gate",
                "ffn_w_up", "ffn_w_down", "final_norm_g")
GROUPS = {
    "conv2": (("conv_w2", "conv_w2", None, "row"),),
    "ffn0": (("gate", "ffn_w_gate", 0, "col"), ("up", "ffn_w_up", 0, "col"), ("down", "ffn_w_down", 0, "row")),
    "attn": (("w_k", "w_k", None, "row"), ("w_v", "w_v", None, "row"), ("w_q", "w_q", None, "row"),
             ("w_o", "w_o", None, "row")),
    "ffn1": (("gate", "ffn_w_gate", 1, "col"), ("up", "ffn_w_up", 1, "col"), ("down", "ffn_w_down", 1, "row")),
}
FETCH_ORDER = ("conv2", "ffn0", "attn", "ffn1")
EMIT_ORDER = ("ffn1", "attn", "ffn0")
PACKED = (("a_norm_g", 0, 1), ("conv_b1", 8, 2), ("conv_dw", 16, CONV_WIDTH), ("conv_dw_b", 48, 1),
          ("conv_ln_g", 56, 1), ("conv_ln_b", 64, 1), ("conv_b2", 72, 1))
PACK_ROWS = 80
WHOLE = (("kv_norm_g", 0, 1), ("b_norm_g", 1, 1), ("ffn_norm_g", 2, 2), ("final_norm_g", 4, 1))
WHOLE_ROWS = 8


def _pack_rows(parts, total, width):
    out, at = [], 0
    for arr, first in parts:
        if first > at:
            out.append(jnp.zeros((first - at, width), F32))
        rows8 = -(-arr.shape[0] // 8) * 8
        out.append(jnp.pad(arr, ((0, rows8 - arr.shape[0]), (0, 0))))
        at = first + rows8
    if total > at:
        out.append(jnp.zeros((total - at, width), F32))
    return jnp.concatenate(out, axis=0)


def kernel(x, a_norm_g, conv_w1, conv_b1, conv_dw, conv_dw_b, conv_ln_g, conv_ln_b, conv_w2, conv_b2, kv_norm_g, w_k, w_v, b_norm_g, w_q, w_o, ffn_norm_g, ffn_w_gate, ffn_w_up, ffn_w_down, final_norm_g, loss_target, m_a_norm_g, m_conv_w1, m_conv_b1, m_conv_dw, m_conv_dw_b, m_conv_ln_g, m_conv_ln_b, m_conv_w2, m_conv_b2, m_kv_norm_g, m_w_k, m_w_v, m_b_norm_g, m_w_q, m_w_o, m_ffn_norm_g, m_ffn_w_gate, m_ffn_w_up, m_ffn_w_down, m_final_norm_g, v_a_norm_g, v_conv_w1, v_conv_b1, v_conv_dw, v_conv_dw_b, v_conv_ln_g, v_conv_ln_b, v_conv_w2, v_conv_b2, v_kv_norm_g, v_w_k, v_w_v, v_b_norm_g, v_w_q, v_w_o, v_ffn_norm_g, v_ffn_w_gate, v_ffn_w_up, v_ffn_w_down, v_final_norm_g):
    args = locals()
    wts = {n: args[n] for n in WEIGHT_NAMES}
    mom = {n: args["m_" + n] for n in WEIGHT_NAMES}
    vel = {n: args["v_" + n] for n in WEIGHT_NAMES}
    s, d = x.shape[-2:]
    dq = d // N_CHIPS
    x2 = x.reshape(s, d)
    tgt = loss_target.reshape(s, d)

    def shard_bf16(name, layer):
        a = wts[name]
        a = a[layer] if layer is not None else a.reshape(a.shape[-2:])
        return a.astype(BF16)

    def pack_shard(src):
        return _pack_rows([(src[n].reshape(-1, dq), first) for n, first, _ in PACKED], PACK_ROWS, dq)

    def pack_whole(src):
        return _pack_rows([(jnp.concatenate([src[n].reshape(-1, d) for n, _, _ in WHOLE], axis=0), 0)], WHOLE_ROWS, d)

    conv_w1_full, packed_full = _gather_weights([shard_bf16("conv_w1", None), pack_shard(wts)], ["col", "row"])
    w = {"conv_w1": conv_w1_full}
    gathers = {}
    pins = []
    ids = jnp.stack([2 * lax.axis_index("x") + lax.axis_index("y"),
                     4 * lax.axis_index("x") + 2 * lax.axis_index("y") + lax.axis_index("c")]).astype(jnp.int32)
    for grp in FETCH_ORDER:
        kinds = [kind for _, _, _, kind in GROUPS[grp]]
        lands = []
        for key, n, layer, kind in GROUPS[grp]:
            shard = wts[n] if layer is not None else wts[n].reshape(wts[n].shape[-2:])
            full = jax.ShapeDtypeStruct(_full_shape(shard, kind), BF16)
            lands.append(_place_own(f"gather_place_{grp}_{key}", shard, kind, full, ids, False, layer))
        copies = _gather_copies(kinds)
        send, recv, _, lands, pin = _copies_start("gather_start_" + grp, [], lands, copies, 3 * len(lands), pins)
        gathers[grp] = (send, recv, [], lands, copies)
        pins = [pin]

    def fetch(grp, after):
        send, recv, srcs, lands, copies = gathers[grp]
        fulls = _copies_wait("gather_wait_" + grp, send, recv, srcs, lands, copies, [after])
        return {key: full for (key, _, _, _), full in zip(GROUPS[grp], fulls)}

    packed = packed_full.reshape(N_CHIPS, PACK_ROWS, dq)
    for n, first, rows in PACKED:
        part = packed[:, first:first + rows, :]
        if n == "conv_dw":
            w[n] = part.transpose(1, 0, 2).reshape(rows, d)
        else:
            w[n] = part.reshape(1, N_CHIPS * rows * dq)
    for n, _, rows in WHOLE:
        w[n] = wts[n].reshape(rows, d)

    exchanges = {}

    def emit_grads(tag, grads, kinds):
        copies = _grad_copies(kinds)
        lands = [_place_own(f"grads_place_{tag}_{t}", gr, kd, jax.ShapeDtypeStruct(_land_shape(gr, kd), gr.dtype),
                            ids, True) for t, (gr, kd) in enumerate(zip(grads, kinds))]
        send, recv, srcs, lands, pin = _copies_start("grads_start_" + tag, grads, lands, copies,
                                                     GRAD_PEERS * len(grads), ())
        exchanges[tag] = (send, recv, srcs, lands, copies)
        return [pin]

    def emit(grp, grads):
        return emit_grads(grp, [grads[key] for key, _, _, _ in GROUPS[grp]], [kind for _, _, _, kind in GROUPS[grp]])

    loss_cols, dx, g = _local_step(x2, tgt, w, fetch, emit, after=pins)
    loss = lax.psum(jnp.sum(loss_cols), ("x", "y", "c"))

    gp = []
    for n, first, rows in PACKED:
        if n == "conv_dw":
            part = g[n].reshape(rows, N_CHIPS, dq).transpose(1, 0, 2)
        else:
            part = g[n].reshape(N_CHIPS, rows, dq)
        gp.append((part, first))
    g_packed = jnp.concatenate(
        [_pack_rows([(p[ci], first) for p, first in gp], PACK_ROWS, dq) for ci in range(N_CHIPS)], axis=0)
    emit_grads("conv", [g["conv_w2"], g["conv_w1"], g_packed, pack_whole(g)], ["row", "col", "row", "all"])

    contribs = {}
    for tag in EMIT_ORDER + ("conv",):
        send, recv, srcs, lands, copies = exchanges[tag]
        contribs[tag] = _copies_wait("grads_wait_" + tag, send, recv, srcs, lands, copies, [dx])
    passed = _pass_on([a for tag in EMIT_ORDER + ("conv",) for a in contribs[tag]])
    for tag in EMIT_ORDER + ("conv",):
        contribs[tag], passed = passed[:len(contribs[tag])], passed[len(contribs[tag]):]

    res = {}

    def adamw(n, contrib, layer=None, prev=None):
        if layer is None:
            shape = wts[n].shape
            r2 = shape[-2:]
            outs = _adamw_reduce("adamw_" + n, contrib, wts[n].reshape(r2), mom[n].reshape(r2), vel[n].reshape(r2))
            return [o.reshape(shape) for o in outs]
        return _adamw_reduce(f"adamw_{n}_{layer}", contrib, wts[n], mom[n], vel[n], layer, prev)

    for grp in ("conv2", "attn"):
        tag = "conv" if grp == "conv2" else grp
        for (key, n, _, _), contrib in zip(GROUPS[grp], contribs[tag]):
            res[n] = adamw(n, contrib)
    res["conv_w1"] = adamw("conv_w1", contribs["conv"][1])
    for (key, n, _, _), c0, c1 in zip(GROUPS["ffn0"], contribs["ffn0"], contribs["ffn1"]):
        res[n] = adamw(n, c1, 1, adamw(n, c0, 0))
    outs = _adamw_reduce("adamw_packed", contribs["conv"][2], pack_shard(wts), pack_shard(mom), pack_shard(vel))
    for n, first, rows in PACKED:
        res[n] = [o[first:first + rows].reshape(wts[n].shape) for o in outs]
    outs = _adamw_reduce("adamw_whole", contribs["conv"][3], pack_whole(wts), pack_whole(mom), pack_whole(vel))
    for n, first, rows in WHOLE:
        res[n] = [o[first:first + rows].reshape(wts[n].shape) for o in outs]

    out = [loss, dx.reshape(x.shape)]
    for which in range(4):
        out += [res[n][which] for n in WEIGHT_NAMES]
    return tuple(out)
```

```python
import functools
import math

import jax
import jax.numpy as jnp
from jax import lax
from jax.experimental import pallas as pl
from jax.experimental.pallas import tpu as pltpu

F32 = jnp.float32
BF16 = jnp.bfloat16

HEAD_DIM = 128
BRANCHES = ((128, 1), (512, 4), (2048, 16))
CONV_WIDTH = 31
CONV_HALO = 32
RMS_EPS = 1e-6
LN_EPS = 1e-5
ADAM_LR = 0.001
ADAM_B1 = 0.9
ADAM_B2 = 0.999
ADAM_EPS = 1e-08
ADAM_WD = 0.01
ADAM_STEP = 10
N_CHIPS = 4
N_DEV = 8
MESH = pl.DeviceIdType.MESH


def _sigmoid(x):
    return 1.0 / (1.0 + jnp.exp(-x))


def _row_tile(rows, want):
    t = min(rows, want)
    assert rows % t == 0, (rows, want)
    return t


_DOT_DIMS = {"nn": ((1,), (0,)), "nt": ((1,), (1,)), "tn": ((0,), (0,))}


ANY_SPEC = pl.BlockSpec(memory_space=pl.ANY)


def _mm(name, mode, a, bs, epilogue, outs, *, m, n, k, extras=(), bm=1024, bn=512, bk=None, after=()):
    bm, bn = min(bm, m), min(bn, n)
    bk = k if bk is None else min(bk, k)
    assert m % bm == 0 and n % bn == 0 and k % bk == 0, (name, m, n, k, bm, bn, bk)
    nk = k // bk
    a_list = list(a) if isinstance(a, (list, tuple)) else [a]
    na, nb, ne, no = len(a_list), len(bs), len(extras), len(outs)
    assert na in (1, nb)

    if mode == "tn":
        a_spec = pl.BlockSpec((bk, bm), lambda i, j, kk: (kk, i))
    else:
        a_spec = pl.BlockSpec((bm, bk), lambda i, j, kk: (i, kk))

    def b_spec(lead, off):
        if mode == "nt":
            blk, idx = (bn, bk), (lambda i, j, kk: (j + off, kk))
        else:
            blk, idx = (bk, bn), (lambda i, j, kk: (kk, j + off))
        if lead is None:
            return pl.BlockSpec(blk, idx)
        return pl.BlockSpec((None,) + blk, lambda i, j, kk: (lead,) + idx(i, j, kk))

    def e_spec(kind, off):
        if kind == "mn":
            return pl.BlockSpec((bm, bn), lambda i, j, kk: (i, j + off))
        return pl.BlockSpec((1, bn), lambda i, j, kk: (0, j + off))

    nf = len(after)
    in_specs = [a_spec] * na + [b_spec(l, o) for _, l, o in bs] + [e_spec(kd, o) for _, kd, o in extras]
    in_specs += [ANY_SPEC] * nf
    out_specs = [pl.BlockSpec((bm, bn), lambda i, j, kk: (i, j)) for _ in outs]
    out_shape = [jax.ShapeDtypeStruct((m, n), dt) for (dt,) in outs]
    dims = (_DOT_DIMS[mode], ((), ()))

    def body(*refs):
        a_refs = refs[:na]
        b_refs = refs[na:na + nb]
        e_refs = refs[na + nb:na + nb + ne]
        o_refs = refs[na + nb + ne + nf:na + nb + ne + nf + no]
        acc_refs = refs[na + nb + ne + nf + no:]
        avs = [a_ref[...].astype(BF16) for a_ref in a_refs]
        prods = [lax.dot_general(avs[bi % na], b_ref[...].astype(BF16), dims, preferred_element_type=F32)
                 for bi, b_ref in enumerate(b_refs)]

        def finish(accs):
            res = epilogue(accs, [e_ref[...] for e_ref in e_refs])
            for o_ref, r in zip(o_refs, res):
                o_ref[...] = r.astype(o_ref.dtype)

        if nk == 1:
            finish(prods)
        else:
            kk = pl.program_id(2)

            @pl.when(kk == 0)
            def _():
                for acc_ref, p in zip(acc_refs, prods):
                    acc_ref[...] = p

            @pl.when(kk > 0)
            def _():
                for acc_ref, p in zip(acc_refs, prods):
                    acc_ref[...] += p

            @pl.when(kk == nk - 1)
            def _():
                finish([acc_ref[...] for acc_ref in acc_refs])

    scratch = [] if nk == 1 else [pltpu.VMEM((bm, bn), F32) for _ in bs]
    res = pl.pallas_call(
        body,
        name=name,
        grid=(m // bm, n // bn, nk),
        in_specs=in_specs,
        out_specs=out_specs,
        out_shape=out_shape,
        scratch_shapes=scratch,
        compiler_params=pltpu.CompilerParams(dimension_semantics=("parallel", "parallel", "arbitrary")),
    )(*a_list, *[b for b, _, _ in bs], *[e for e, _, _ in extras], *after)
    return res


def _rms_fwd(name, x, gains, after=()):
    s, d = x.shape
    ng = gains.shape[0]
    t = _row_tile(s, 256)
    nf = len(after)

    def body(x_ref, g_ref, *refs):
        o_refs = refs[nf:]
        xv = x_ref[...]
        r = lax.rsqrt(jnp.mean(xv * xv, axis=-1, keepdims=True) + RMS_EPS)
        xh = xv * r
        for gi, o_ref in enumerate(o_refs):
            o_ref[...] = (xh * g_ref[gi:gi + 1, :]).astype(o_ref.dtype)

    return pl.pallas_call(
        body,
        name=name,
        grid=(s // t,),
        in_specs=[pl.BlockSpec((t, d), lambda i: (i, 0)), pl.BlockSpec((ng, d), lambda i: (0, 0))] + [ANY_SPEC] * nf,
        out_specs=[pl.BlockSpec((t, d), lambda i: (i, 0)) for _ in range(ng)],
        out_shape=[jax.ShapeDtypeStruct((s, d), BF16) for _ in range(ng)],
        compiler_params=pltpu.CompilerParams(dimension_semantics=("parallel",)),
    )(x, gains, *after)


def _rms_bwd(name, x, gains, dns, dres):
    s, d = x.shape
    ng = gains.shape[0]
    t = _row_tile(s, 256)

    def body(x_ref, g_ref, dres_ref, *refs):
        dn_refs = refs[:ng]
        dx_ref, dxb_ref, dg_ref, cs_ref = refs[ng:]
        i = pl.program_id(0)
        xv = x_ref[...]
        r = lax.rsqrt(jnp.mean(xv * xv, axis=-1, keepdims=True) + RMS_EPS)
        xh = xv * r
        dx = dres_ref[...]
        dgs = []
        for gi in range(ng):
            dn = dn_refs[gi][...].astype(F32)
            dxh = dn * g_ref[gi:gi + 1, :]
            dgs.append(jnp.sum(dn * xh, axis=0, keepdims=True))
            dx = dx + r * (dxh - xh * jnp.mean(dxh * xh, axis=-1, keepdims=True))
        dx_ref[...] = dx
        dxb_ref[...] = dx.astype(BF16)
        dg = jnp.concatenate(dgs, axis=0) if ng > 1 else dgs[0]
        cs = jnp.sum(dx, axis=0, keepdims=True)

        @pl.when(i == 0)
        def _():
            dg_ref[...] = dg
            cs_ref[...] = cs

        @pl.when(i > 0)
        def _():
            dg_ref[...] += dg
            cs_ref[...] += cs

    row = pl.BlockSpec((t, d), lambda i: (i, 0))
    return pl.pallas_call(
        body,
        name=name,
        grid=(s // t,),
        in_specs=[row, pl.BlockSpec((ng, d), lambda i: (0, 0)), row] + [row] * ng,
        out_specs=[row, row, pl.BlockSpec((ng, d), lambda i: (0, 0)), pl.BlockSpec((1, d), lambda i: (0, 0))],
        out_shape=[
            jax.ShapeDtypeStruct((s, d), F32),
            jax.ShapeDtypeStruct((s, d), BF16),
            jax.ShapeDtypeStruct((ng, d), F32),
            jax.ShapeDtypeStruct((1, d), F32),
        ],
        compiler_params=pltpu.CompilerParams(dimension_semantics=("arbitrary",)),
    )(x, gains, dres, *dns)


def _final_loss(name, h, gain, target):
    s, d = h.shape
    t = _row_tile(s, 256)

    def body(h_ref, g_ref, t_ref, dh_ref, dhb_ref, dg_ref, ls_ref):
        i = pl.program_id(0)
        xv = h_ref[...]
        g = g_ref[...]
        r = lax.rsqrt(jnp.mean(xv * xv, axis=-1, keepdims=True) + RMS_EPS)
        xh = xv * r
        err = xh * g - t_ref[...]
        ls = jnp.sum(err * err, axis=0, keepdims=True) * (0.5 / d)
        dy = err * (1.0 / d)
        dxh = dy * g
        dg = jnp.sum(dy * xh, axis=0, keepdims=True)
        dx = r * (dxh - xh * jnp.mean(dxh * xh, axis=-1, keepdims=True))
        dh_ref[...] = dx
        dhb_ref[...] = dx.astype(BF16)

        @pl.when(i == 0)
        def _():
            dg_ref[...] = dg
            ls_ref[...] = ls

        @pl.when(i > 0)
        def _():
            dg_ref[...] += dg
            ls_ref[...] += ls

    row = pl.BlockSpec((t, d), lambda i: (i, 0))
    vec = pl.BlockSpec((1, d), lambda i: (0, 0))
    return pl.pallas_call(
        body,
        name=name,
        grid=(s // t,),
        in_specs=[row, vec, row],
        out_specs=[row, row, vec, vec],
        out_shape=[
            jax.ShapeDtypeStruct((s, d), F32),
            jax.ShapeDtypeStruct((s, d), BF16),
            jax.ShapeDtypeStruct((1, d), F32),
            jax.ShapeDtypeStruct((1, d), F32),
        ],
        compiler_params=pltpu.CompilerParams(dimension_semantics=("arbitrary",)),
    )(h, gain, target)


def _conv_tiles(s):
    t = _row_tile(s, 128)
    assert t % CONV_HALO == 0
    return t, t // CONV_HALO


def _dwconv_fwd(name, u, dw, dw_b, ln_g, ln_b):
    s, d = u.shape
    t, hb = _conv_tiles(s)
    w = dw.shape[0]
    lo = CONV_HALO - (w - 1)

    def body(cur_ref, prev_ref, dw_ref, dwb_ref, lg_ref, lb_ref, c_ref, sw_ref, cat_ref):
        i = pl.program_id(0)
        cat_ref[CONV_HALO:, :] = cur_ref[...]

        @pl.when(i == 0)
        def _():
            cat_ref[:CONV_HALO, :] = jnp.zeros((CONV_HALO, d), F32)

        @pl.when(i > 0)
        def _():
            cat_ref[:CONV_HALO, :] = prev_ref[...]

        acc = jnp.zeros((t, d), F32) + dwb_ref[...]
        for kk in range(w):
            acc = acc + dw_ref[kk:kk + 1, :] * cat_ref[lo + kk:lo + kk + t, :]
        c_ref[...] = acc
        mu = jnp.mean(acc, axis=-1, keepdims=True)
        cc = acc - mu
        var = jnp.mean(cc * cc, axis=-1, keepdims=True)
        ln = cc * lax.rsqrt(var + LN_EPS) * lg_ref[...] + lb_ref[...]
        sw_ref[...] = (ln * _sigmoid(ln)).astype(BF16)

    row = pl.BlockSpec((t, d), lambda i: (i, 0))
    prev = pl.BlockSpec((CONV_HALO, d), lambda i: (jnp.maximum(i * hb - 1, 0), 0))
    vec = pl.BlockSpec((1, d), lambda i: (0, 0))
    return pl.pallas_call(
        body,
        name=name,
        grid=(s // t,),
        in_specs=[row, prev, pl.BlockSpec((w, d), lambda i: (0, 0)), vec, vec, vec],
        out_specs=[row, row],
        out_shape=[jax.ShapeDtypeStruct((s, d), F32), jax.ShapeDtypeStruct((s, d), BF16)],
        scratch_shapes=[pltpu.VMEM((CONV_HALO + t, d), F32)],
        compiler_params=pltpu.CompilerParams(dimension_semantics=("parallel",)),
    )(u, u, dw, dw_b, ln_g, ln_b)


def _conv_ln_bwd(name, c, dsw, ln_g, ln_b):
    s, d = c.shape
    t = _row_tile(s, 256)

    def body(c_ref, dsw_ref, lg_ref, lb_ref, dc_ref, sums_ref):
        i = pl.program_id(0)
        cv = c_ref[...]
        g = lg_ref[...]
        mu = jnp.mean(cv, axis=-1, keepdims=True)
        cc = cv - mu
        rstd = lax.rsqrt(jnp.mean(cc * cc, axis=-1, keepdims=True) + LN_EPS)
        ch = cc * rstd
        ln = ch * g + lb_ref[...]
        sg = _sigmoid(ln)
        dln = dsw_ref[...] * (sg * (1.0 + ln * (1.0 - sg)))
        dch = dln * g
        dc = rstd * (dch - jnp.mean(dch, axis=-1, keepdims=True) - ch * jnp.mean(dch * ch, axis=-1, keepdims=True))
        dc_ref[...] = dc
        sums = jnp.concatenate(
            [
                jnp.sum(dln * ch, axis=0, keepdims=True),
                jnp.sum(dln, axis=0, keepdims=True),
                jnp.sum(dc, axis=0, keepdims=True),
                jnp.zeros((1, d), F32),
            ],
            axis=0,
        )

        @pl.when(i == 0)
        def _():
            sums_ref[...] = sums

        @pl.when(i > 0)
        def _():
            sums_ref[...] += sums

    row = pl.BlockSpec((t, d), lambda i: (i, 0))
    vec = pl.BlockSpec((1, d), lambda i: (0, 0))
    return pl.pallas_call(
        body,
        name=name,
        grid=(s // t,),
        in_specs=[row, row, vec, vec],
        out_specs=[row, pl.BlockSpec((4, d), lambda i: (0, 0))],
        out_shape=[jax.ShapeDtypeStruct((s, d), F32), jax.ShapeDtypeStruct((4, d), F32)],
        compiler_params=pltpu.CompilerParams(dimension_semantics=("arbitrary",)),
    )(c, dsw, ln_g, ln_b)


def _conv_dw_bwd(name, dc, u, a, gt, dw):
    s, d = dc.shape
    t, hb = _conv_tiles(s)
    w = dw.shape[0]
    lo = CONV_HALO - (w - 1)
    nt = s // t

    def body(dc_ref, dcn_ref, u_ref, up_ref, a_ref, gt_ref, dw_ref, dpre_ref, ddw_ref, db_ref, dcat_ref, ucat_ref):
        i = pl.program_id(0)
        dcat_ref[:t, :] = dc_ref[...]
        ucat_ref[CONV_HALO:, :] = u_ref[...]

        @pl.when(i == nt - 1)
        def _():
            dcat_ref[t:, :] = jnp.zeros((CONV_HALO, d), F32)

        @pl.when(i < nt - 1)
        def _():
            dcat_ref[t:, :] = dcn_ref[...]

        @pl.when(i == 0)
        def _():
            ucat_ref[:CONV_HALO, :] = jnp.zeros((CONV_HALO, d), F32)

        @pl.when(i > 0)
        def _():
            ucat_ref[:CONV_HALO, :] = up_ref[...]

        dcv = dc_ref[...]
        du = jnp.zeros((t, d), F32)
        rows = []
        for kk in range(w):
            sh = w - 1 - kk
            du = du + dw_ref[kk:kk + 1, :] * dcat_ref[sh:sh + t, :]
            rows.append(jnp.sum(dcv * ucat_ref[lo + kk:lo + kk + t, :], axis=0, keepdims=True))
        ddw = jnp.concatenate(rows, axis=0)
        av = a_ref[...].astype(F32)
        sg = _sigmoid(gt_ref[...].astype(F32))
        da = du * sg
        dgt = du * av * sg * (1.0 - sg)
        dpre_ref[:, :d] = da.astype(BF16)
        dpre_ref[:, d:] = dgt.astype(BF16)
        db = jnp.concatenate([jnp.sum(da, axis=0, keepdims=True), jnp.sum(dgt, axis=0, keepdims=True)], axis=1)

        @pl.when(i == 0)
        def _():
            ddw_ref[...] = ddw
            db_ref[...] = db

        @pl.when(i > 0)
        def _():
            ddw_ref[...] += ddw
            db_ref[...] += db

    row = pl.BlockSpec((t, d), lambda i: (i, 0))
    nxt = pl.BlockSpec((CONV_HALO, d), lambda i: (jnp.minimum((i + 1) * hb, s // CONV_HALO - 1), 0))
    prev = pl.BlockSpec((CONV_HALO, d), lambda i: (jnp.maximum(i * hb - 1, 0), 0))
    return pl.pallas_call(
        body,
        name=name,
        grid=(nt,),
        in_specs=[row, nxt, row, prev, row, row, pl.BlockSpec((w, d), lambda i: (0, 0))],
        out_specs=[
            pl.BlockSpec((t, 2 * d), lambda i: (i, 0)),
            pl.BlockSpec((w, d), lambda i: (0, 0)),
            pl.BlockSpec((1, 2 * d), lambda i: (0, 0)),
        ],
        out_shape=[
            jax.ShapeDtypeStruct((s, 2 * d), BF16),
            jax.ShapeDtypeStruct((w, d), F32),
            jax.ShapeDtypeStruct((1, 2 * d), F32),
        ],
        scratch_shapes=[pltpu.VMEM((t + CONV_HALO, d), F32), pltpu.VMEM((CONV_HALO + t, d), F32)],
        compiler_params=pltpu.CompilerParams(dimension_semantics=("arbitrary",)),
    )(dc, dc, u, u, a, gt, dw)


def _alibi_slopes(n_heads):
    h = jnp.arange(1, n_heads + 1, dtype=F32)
    return jnp.exp2(-8.0 * h / n_heads)


def _band_masks(bq):
    qi = lax.broadcasted_iota(jnp.int32, (bq, bq), 0)
    kj = lax.broadcasted_iota(jnp.int32, (bq, bq), 1)
    return qi - kj, qi - kj + bq


def _attn_fwd(name, q, k, v, slopes, bq):
    s, dm = q.shape
    nh = dm // HEAD_DIM
    nt = s // bq
    nbr = len(BRANCHES)
    scale = HEAD_DIM ** -0.5
    nt_dims = (((1,), (1,)), ((), ()))

    def body(sl_ref, q_ref, k_ref, v_ref, o_ref, ob_ref, l_ref, tmp, qr, kr, vr, orm, lrm, onat, lnat):
        slope = sl_ref[pl.program_id(0)]
        jc, jp = _band_masks(bq)
        ok_c = jc >= 0
        ok_p0 = jp <= bq
        for bi, (win, dil) in enumerate(BRANCHES):
            ll = s // dil
            nblk = ll // bq
            if dil == 1:
                sq, sk, sv = q_ref, k_ref, v_ref
                d_o, d_l = onat.at[bi], lnat.at[bi]
            else:
                for src, dst in ((q_ref, qr), (k_ref, kr), (v_ref, vr)):
                    tmp[...] = src[...].astype(F32)
                    for r in range(dil):
                        dst[r * ll:(r + 1) * ll, :] = tmp[pl.ds(r, ll, stride=dil), :].astype(BF16)
                sq, sk, sv = qr, kr, vr
                d_o, d_l = orm, lrm
            bias_c = jc.astype(F32) * (slope * dil)
            bias_p = jp.astype(F32) * (slope * dil)

            def tile(ti, carry):
                row = pl.multiple_of(ti * bq, bq)
                prow = pl.multiple_of(jnp.maximum(ti - 1, 0) * bq, bq)
                ok_p = ok_p0 & (lax.rem(ti, nblk) > 0)
                qh = sq[pl.ds(row, bq), :]
                sc = lax.dot_general(qh, sk[pl.ds(row, bq), :], nt_dims, preferred_element_type=F32) * scale - bias_c
                sp = lax.dot_general(qh, sk[pl.ds(prow, bq), :], nt_dims, preferred_element_type=F32) * scale - bias_p
                sc = jnp.where(ok_c, sc, -1e30)
                sp = jnp.where(ok_p, sp, -1e30)
                mx = jnp.maximum(jnp.max(sc, axis=-1, keepdims=True), jnp.max(sp, axis=-1, keepdims=True))
                pc = jnp.exp(sc - mx)
                pp = jnp.exp(sp - mx)
                den = jnp.sum(pc, axis=-1, keepdims=True) + jnp.sum(pp, axis=-1, keepdims=True)
                inv = 1.0 / den
                o = jnp.dot((pc * inv).astype(BF16), sv[pl.ds(row, bq), :], preferred_element_type=F32)
                o = o + jnp.dot((pp * inv).astype(BF16), sv[pl.ds(prow, bq), :], preferred_element_type=F32)
                d_o[pl.ds(row, bq), :] = o
                d_l[pl.ds(row, bq), :] = jnp.broadcast_to(mx + jnp.log(den), (bq, HEAD_DIM))
                return carry

            lax.fori_loop(0, nt, tile, 0, unroll=2)
            if dil > 1:
                for r in range(dil):
                    onat[bi, pl.ds(r, ll, stride=dil), :] = orm[r * ll:(r + 1) * ll, :]
                    lnat[bi, pl.ds(r, ll, stride=dil), :] = lrm[r * ll:(r + 1) * ll, :]

        def merge(ti, carry):
            rows = pl.ds(pl.multiple_of(ti * bq, bq), bq)
            ls = [lnat[bi, rows, :] for bi in range(nbr)]
            mx = functools.reduce(jnp.maximum, ls)
            es = [jnp.exp(l - mx) for l in ls]
            tot = functools.reduce(jnp.add, es)
            inv = 1.0 / tot
            o = functools.reduce(jnp.add, [e * inv * onat[bi, rows, :] for bi, e in enumerate(es)])
            o_ref[rows, :] = o
            ob_ref[rows, :] = o.astype(BF16)
            l_ref[rows, :] = mx + jnp.log(tot)
            return carry

        lax.fori_loop(0, nt, merge, 0)

    head = pl.BlockSpec((s, HEAD_DIM), lambda h: (0, h))
    return pl.pallas_call(
        body,
        name=name,
        grid=(nh,),
        in_specs=[pl.BlockSpec(memory_space=pltpu.SMEM), head, head, head],
        out_specs=[head, head, head],
        out_shape=[jax.ShapeDtypeStruct((s, dm), F32), jax.ShapeDtypeStruct((s, dm), BF16),
                   jax.ShapeDtypeStruct((s, dm), F32)],
        scratch_shapes=[pltpu.VMEM((s, HEAD_DIM), F32)] + [pltpu.VMEM((s, HEAD_DIM), BF16)] * 3
        + [pltpu.VMEM((s, HEAD_DIM), F32)] * 2 + [pltpu.VMEM((nbr, s, HEAD_DIM), F32)] * 2,
        compiler_params=pltpu.CompilerParams(dimension_semantics=("parallel",)),
    )(slopes, q, k, v)


def _attn_bwd(name, q, k, v, o, lse, do, slopes, bq):
    s, dm = q.shape
    nh = dm // HEAD_DIM
    nt = s // bq
    scale = HEAD_DIM ** -0.5
    nt_dims = (((1,), (1,)), ((), ()))
    tn_dims = (((0,), (0,)), ((), ()))

    def body(sl_ref, q_ref, k_ref, v_ref, o_ref, l_ref, do_ref, dq_ref, dk_ref, dv_ref,
             tmp, qr, kr, vr, dor, lr, dlr, dln, dqr, dkr, dvr, aq, ak, av):
        slope = sl_ref[pl.program_id(0)]
        jc, jp = _band_masks(bq)
        ok_c = jc >= 0
        ok_p0 = jp <= bq

        def delta(ti, carry):
            rows = pl.ds(pl.multiple_of(ti * bq, bq), bq)
            dl = jnp.sum(do_ref[rows, :].astype(F32) * o_ref[rows, :], axis=-1, keepdims=True)
            dln[rows, :] = jnp.broadcast_to(dl, (bq, HEAD_DIM))
            return carry

        lax.fori_loop(0, nt, delta, 0)

        for bi, (win, dil) in enumerate(BRANCHES):
            ll = s // dil
            nblk = ll // bq
            if dil == 1:
                sq, sk, sv, sdo, sl, sdl = q_ref, k_ref, v_ref, do_ref, l_ref, dln
                gq, gk, gv = aq, ak, av
            else:
                for src, dst in ((q_ref, qr), (k_ref, kr), (v_ref, vr), (do_ref, dor)):
                    tmp[...] = src[...].astype(F32)
                    for r in range(dil):
                        dst[r * ll:(r + 1) * ll, :] = tmp[pl.ds(r, ll, stride=dil), :].astype(BF16)
                for r in range(dil):
                    lr[r * ll:(r + 1) * ll, :] = l_ref[pl.ds(r, ll, stride=dil), :]
                    dlr[r * ll:(r + 1) * ll, :] = dln[pl.ds(r, ll, stride=dil), :]
                sq, sk, sv, sdo, sl, sdl = qr, kr, vr, dor, lr, dlr
                gq, gk, gv = dqr, dkr, dvr
            if bi == 0 or dil > 1:
                gk[...] = jnp.zeros((s, HEAD_DIM), F32)
                gv[...] = jnp.zeros((s, HEAD_DIM), F32)
            bias_c = jc.astype(F32) * (slope * dil)
            bias_p = jp.astype(F32) * (slope * dil)
            first = bi == 0

            def tile(ti, carry):
                row = pl.ds(pl.multiple_of(ti * bq, bq), bq)
                prow = pl.ds(pl.multiple_of(jnp.maximum(ti - 1, 0) * bq, bq), bq)
                ok_p = ok_p0 & (lax.rem(ti, nblk) > 0)
                qh, doh = sq[row, :], sdo[row, :]
                lc = sl[row, :][:, :1]
                dl = sdl[row, :][:, :1]

                def block(kk_, vv, bias, ok):
                    sc = lax.dot_general(qh, kk_, nt_dims, preferred_element_type=F32) * scale - bias
                    p = jnp.where(ok, jnp.exp(jnp.where(ok, sc, -1e30) - lc), 0.0)
                    dp = lax.dot_general(doh, vv, nt_dims, preferred_element_type=F32)
                    return p.astype(BF16), (p * (dp - dl) * scale).astype(BF16)

                kc, kp = sk[row, :], sk[prow, :]
                p_c, ds_c = block(kc, sv[row, :], bias_c, ok_c)
                p_p, ds_p = block(kp, sv[prow, :], bias_p, ok_p)
                dq = jnp.dot(ds_c, kc, preferred_element_type=F32) + jnp.dot(ds_p, kp, preferred_element_type=F32)
                if first or dil > 1:
                    gq[row, :] = dq
                else:
                    gq[row, :] += dq
                gk[prow, :] += lax.dot_general(ds_p, qh, tn_dims, preferred_element_type=F32)
                gk[row, :] += lax.dot_general(ds_c, qh, tn_dims, preferred_element_type=F32)
                gv[prow, :] += lax.dot_general(p_p, doh, tn_dims, preferred_element_type=F32)
                gv[row, :] += lax.dot_general(p_c, doh, tn_dims, preferred_element_type=F32)
                return carry

            lax.fori_loop(0, nt, tile, 0)
            if dil > 1:
                for acc, rm in ((aq, dqr), (ak, dkr), (av, dvr)):
                    for r in range(dil):
                        acc[pl.ds(r, ll, stride=dil), :] += rm[r * ll:(r + 1) * ll, :]

        dq_ref[...] = aq[...].astype(BF16)
        dk_ref[...] = ak[...].astype(BF16)
        dv_ref[...] = av[...].astype(BF16)

    head = pl.BlockSpec((s, HEAD_DIM), lambda h: (0, h))
    f32buf = pltpu.VMEM((s, HEAD_DIM), F32)
    b16buf = pltpu.VMEM((s, HEAD_DIM), BF16)
    return pl.pallas_call(
        body,
        name=name,
        grid=(nh,),
        in_specs=[pl.BlockSpec(memory_space=pltpu.SMEM)] + [head] * 6,
        out_specs=[head] * 3,
        out_shape=[jax.ShapeDtypeStruct((s, dm), BF16)] * 3,
        scratch_shapes=[f32buf] + [b16buf] * 4 + [f32buf] * 9,
        compiler_params=pltpu.CompilerParams(dimension_semantics=("parallel",)),
    )(slopes, q, k, v, o, lse, do)


def _ep_id(accs, ex):
    return [accs[0]]


def _ep_all(accs, ex):
    return list(accs)


def _ep_sum(accs, ex):
    return [accs[0] + accs[1]]


def _ep_add(accs, ex):
    return [accs[0] + ex[0].astype(F32)]


def _ep_bias_res(accs, ex):
    return [accs[0] + ex[0] + ex[1]]


def _ep_glu(accs, ex):
    a = accs[0] + ex[0]
    gt = accs[1] + ex[1]
    return [a * _sigmoid(gt), a, gt]


def _ep_swiglu(accs, ex):
    g, u = accs
    return [g, u, g * _sigmoid(g) * u]


def _ep_swiglu_bwd(accs, ex):
    dact = accs[0]
    g = ex[0].astype(F32)
    u = ex[1].astype(F32)
    sg = _sigmoid(g)
    return [dact * u * (sg * (1.0 + g * (1.0 - sg))), dact * g * sg]


def _ffn_fwd(tag, h, gain, wg, wu, wd):
    s, d = h.shape
    f = wg.shape[-1]
    (n,) = _rms_fwd(f"{tag}_norm", h, gain)
    g, u, act = _mm(f"{tag}_gate_up", "nn", n, [(wg, None, 0), (wu, None, 0)], _ep_swiglu,
                    [(BF16,), (BF16,), (BF16,)], m=s, n=f, k=d)
    (out,) = _mm(f"{tag}_down", "nn", act, [(wd, None, 0)], _ep_add, [(F32,)], m=s, n=d, k=f,
                 extras=[(h, "mn", 0)], bn=256)
    return out, (n, g, u, act)


def _ffn_bwd(tag, h_in, gain, wg, wu, wd, saved, dh, dhb, after, emit):
    s, d = h_in.shape
    f = wg.shape[-1]
    n, g, u, act = saved
    dg, du = _mm(f"{tag}_bwd_dact", "nt", dhb, [(wd, None, 0)], _ep_swiglu_bwd, [(BF16,), (BF16,)],
                 m=s, n=f, k=d, extras=[(g, "mn", 0), (u, "mn", 0)], after=after)
    (dwd,) = _mm(f"{tag}_bwd_dwd", "tn", act, [(dhb, None, 0)], _ep_id, [(BF16,)], m=f, n=d, k=s,
                 bm=f // 4)
    dwg, dwu = _mm(f"{tag}_bwd_dwgu", "tn", n, [(dg, None, 0), (du, None, 0)], _ep_all, [(BF16,), (BF16,)],
                   m=d, n=f, k=s)
    pin = emit(dict(gate=dwg, up=dwu, down=dwd))
    (dn,) = _mm(f"{tag}_bwd_dn", "nt", [dg, du], [(wg, None, 0), (wu, None, 0)], _ep_sum, [(F32,)],
                m=s, n=d, k=f, bm=512, bn=256, after=pin)
    dx, dxb, dgain, cs = _rms_bwd(f"{tag}_bwd_norm", h_in, gain, [dn], dh)
    return dx, dxb, dgain, cs


def _local_step(x, target, w, fetch, emit, after=()):
    s, d = x.shape
    nh = d // HEAD_DIM
    bq = BRANCHES[0][0] // BRANCHES[0][1]
    assert all(win // dil == bq for win, dil in BRANCHES) and BRANCHES[0][1] == 1
    slopes = _alibi_slopes(nh)
    nd = d // 512 if d >= 512 else 1
    bn = d // nd

    (n1,) = _rms_fwd("a_norm", x, w["a_norm_g"], after=after)
    glu, a, gt = _mm("conv_pw1_glu", "nn", n1, [(w["conv_w1"], None, 0), (w["conv_w1"], None, nd)], _ep_glu,
                     [(F32,), (BF16,), (BF16,)], m=s, n=d, k=d, bn=bn,
                     extras=[(w["conv_b1"], "n", 0), (w["conv_b1"], "n", nd)])
    c, sw = _dwconv_fwd("conv_dw_ln", glu, w["conv_dw"], w["conv_dw_b"], w["conv_ln_g"], w["conv_ln_b"])
    w_conv2 = fetch("conv2", sw)["conv_w2"]
    (h1,) = _mm("conv_pw2", "nn", sw, [(w_conv2, None, 0)], _ep_bias_res, [(F32,)], m=s, n=d, k=d,
                extras=[(w["conv_b2"], "n", 0), (x, "mn", 0)])
    wf0 = fetch("ffn0", h1)
    h2, ffn0 = _ffn_fwd("ffn0", h1, w["ffn_norm_g"][0:1], wf0["gate"], wf0["up"], wf0["down"])
    wa = fetch("attn", h2)
    kvn, qn = _rms_fwd("kvq_norm", h2, jnp.concatenate([w["kv_norm_g"], w["b_norm_g"]], axis=0))
    k, v = _mm("kv_proj", "nn", kvn, [(wa["w_k"], None, 0), (wa["w_v"], None, 0)], _ep_all, [(BF16,), (BF16,)],
               m=s, n=d, k=d)
    (q,) = _mm("q_proj", "nn", qn, [(wa["w_q"], None, 0)], _ep_id, [(BF16,)], m=s, n=d, k=d)
    att, attb, lse = _attn_fwd("attn_fwd", q, k, v, slopes, bq)
    (h3,) = _mm("o_proj", "nn", attb, [(wa["w_o"], None, 0)], _ep_add, [(F32,)], m=s, n=d, k=d,
                extras=[(h2, "mn", 0)])
    wf1 = fetch("ffn1", h3)
    h4, ffn1 = _ffn_fwd("ffn1", h3, w["ffn_norm_g"][1:2], wf1["gate"], wf1["up"], wf1["down"])
    dh4, dh4b, d_final_g, loss_cols = _final_loss("final_loss", h4, w["final_norm_g"], target)

    g = {}
    ga = {}
    dh3, dh3b, dgain1, _ = _ffn_bwd("ffn1", h3, w["ffn_norm_g"][1:2], wf1["gate"], wf1["up"], wf1["down"], ffn1,
                                    dh4, dh4b, (), functools.partial(emit, "ffn1"))
    (datt,) = _mm("o_proj_bwd_dx", "nt", dh3b, [(wa["w_o"], None, 0)], _ep_id, [(BF16,)], m=s, n=d, k=d)
    (ga["w_o"],) = _mm("o_proj_bwd_dw", "tn", attb, [(dh3b, None, 0)], _ep_id, [(BF16,)], m=d, n=d, k=s)
    dq, dk, dv = _attn_bwd("attn_bwd", q, k, v, att, lse, datt, slopes, bq)
    (ga["w_q"],) = _mm("q_proj_bwd_dw", "tn", qn, [(dq, None, 0)], _ep_id, [(BF16,)], m=d, n=d, k=s)
    ga["w_k"], ga["w_v"] = _mm("kv_proj_bwd_dw", "tn", kvn, [(dk, None, 0), (dv, None, 0)], _ep_all,
                               [(BF16,), (BF16,)], m=d, n=d, k=s)
    pin = emit("attn", ga)
    (dqn,) = _mm("q_proj_bwd_dx", "nt", dq, [(wa["w_q"], None, 0)], _ep_id, [(F32,)], m=s, n=d, k=d, after=pin)
    (dkvn,) = _mm("kv_proj_bwd_dx", "nt", [dk, dv], [(wa["w_k"], None, 0), (wa["w_v"], None, 0)], _ep_sum, [(F32,)],
                  m=s, n=d, k=d)
    dh2, dh2b, dg_kvq, _ = _rms_bwd("kvq_norm_bwd", h2, jnp.concatenate([w["kv_norm_g"], w["b_norm_g"]], axis=0),
                                    [dkvn, dqn], dh3)
    dh1, dh1b, dgain0, cs_h1 = _ffn_bwd("ffn0", h1, w["ffn_norm_g"][0:1], wf0["gate"], wf0["up"], wf0["down"], ffn0,
                                        dh2, dh2b, (), functools.partial(emit, "ffn0"))
    (dsw,) = _mm("conv_pw2_bwd_dx", "nt", dh1b, [(w_conv2, None, 0)], _ep_id, [(F32,)], m=s, n=d, k=d)
    (g["conv_w2"],) = _mm("conv_pw2_bwd_dw", "tn", sw, [(dh1b, None, 0)], _ep_id, [(BF16,)], m=d, n=d, k=s)
    dc, ln_sums = _conv_ln_bwd("conv_ln_bwd", c, dsw, w["conv_ln_g"], w["conv_ln_b"])
    dpre, ddw, db1 = _conv_dw_bwd("conv_dw_bwd", dc, glu, a, gt, w["conv_dw"])
    (g["conv_w1"],) = _mm("conv_pw1_bwd_dw", "tn", n1, [(dpre, None, 0)], _ep_id, [(BF16,)], m=d, n=2 * d, k=s)
    (dn1,) = _mm("conv_pw1_bwd_dx", "nt", dpre, [(w["conv_w1"], None, 0)], _ep_id, [(F32,)], m=s, n=d, k=2 * d)
    dx, _, d_a_norm, _ = _rms_bwd("a_norm_bwd", x, w["a_norm_g"], [dn1], dh1)

    g.update(
        a_norm_g=d_a_norm, conv_b1=db1, conv_dw=ddw, conv_dw_b=ln_sums[2:3], conv_ln_g=ln_sums[0:1],
        conv_ln_b=ln_sums[1:2], conv_b2=cs_h1, kv_norm_g=dg_kvq[0:1], b_norm_g=dg_kvq[1:2],
        ffn_norm_g=jnp.concatenate([dgain0, dgain1], axis=0), final_norm_g=d_final_g,
    )
    return loss_cols, dx, g


HBM_SPEC = pl.BlockSpec(memory_space=pltpu.HBM)


def _mesh_place():
    x, y, c = lax.axis_index("x"), lax.axis_index("y"), lax.axis_index("c")
    chips = [(1 - x, y), (x, 1 - y), (1 - x, 1 - y)]
    return x, y, c, chips


def _shard_view(ref, kind, s, half=None):
    rows, cols = ref.shape
    if kind == "col":
        cw = cols // N_CHIPS
        if half is None:
            return ref.at[pl.ds(0, rows), pl.ds(s * cw, cw)]
        return ref.at[pl.ds(half * (rows // 2), rows // 2), pl.ds(s * cw, cw)]
    r = rows // N_CHIPS
    if half is None:
        return ref.at[pl.ds(s * r, r), pl.ds(0, cols)]
    return ref.at[pl.ds(s * r + half * (r // 2), r // 2), pl.ds(0, cols)]


def _full_shape(shard, kind):
    r, cw = shard.shape[-2:]
    return (r, cw * N_CHIPS) if kind == "col" else (r * N_CHIPS, cw)


def _gather_weights(shards, kinds):
    nt = len(shards)
    fulls = [jax.ShapeDtypeStruct(_full_shape(sh, kind), sh.dtype) for sh, kind in zip(shards, kinds)]

    def body(*refs):
        src = refs[:nt]
        dst = refs[nt:2 * nt]
        send, recv, fsend, frecv, local = refs[2 * nt:]
        x, y, c, chips = _mesh_place()
        s = 2 * x + y
        sib = (x, y, 1 - c)

        def half_of_shard(t):
            r, cw = src[t].shape
            return src[t].at[pl.ds(c * (r // 2), r // 2), pl.ds(0, cw)]

        locals_ = [pltpu.make_async_copy(src[t], _shard_view(dst[t], kinds[t], s), local.at[t]) for t in range(nt)]
        for cp in locals_:
            cp.start()
        sends = []
        for t in range(nt):
            for j, chip in enumerate(chips):
                cp = pltpu.make_async_remote_copy(
                    src_ref=half_of_shard(t), dst_ref=_shard_view(dst[t], kinds[t], s, c),
                    send_sem=send.at[t, j], recv_sem=recv.at[t, j], device_id=(*chip, c), device_id_type=MESH)
                cp.start()
                sends.append(cp)
        for t in range(nt):
            for j, (px, py) in enumerate(chips):
                landed = _shard_view(dst[t], kinds[t], 2 * px + py, c)
                pltpu.make_async_remote_copy(
                    src_ref=half_of_shard(t), dst_ref=landed, send_sem=send.at[t, j], recv_sem=recv.at[t, j],
                    device_id=(px, py, c), device_id_type=MESH).wait_recv()
                cp = pltpu.make_async_remote_copy(
                    src_ref=landed, dst_ref=landed, send_sem=fsend.at[t, j], recv_sem=frecv.at[t, j],
                    device_id=sib, device_id_type=MESH)
                cp.start()
                sends.append(cp)
        for t in range(nt):
            for j, (px, py) in enumerate(chips):
                other = _shard_view(dst[t], kinds[t], 2 * px + py, 1 - c)
                pltpu.make_async_remote_copy(
                    src_ref=other, dst_ref=other, send_sem=fsend.at[t, j], recv_sem=frecv.at[t, j],
                    device_id=sib, device_id_type=MESH).wait_recv()
        for cp in sends:
            cp.wait_send()
        for cp in locals_:
            cp.wait()

    return pl.pallas_call(
        body,
        name="gather_weights",
        in_specs=[HBM_SPEC] * nt,
        out_specs=[HBM_SPEC] * nt,
        out_shape=fulls,
        scratch_shapes=[pltpu.SemaphoreType.DMA((nt, 3))] * 4 + [pltpu.SemaphoreType.DMA((nt,))],
    )(*shards)


def _row_blocks(rows, want=512):
    nb = 1
    while rows // nb > want or rows % nb or (rows // nb) % 16:
        nb += 1
        if nb > rows:
            return rows, 1
    return rows // nb, nb


def _place_own(name, src, kind, land, ids, into_slot, lead=0):
    if into_slot:
        _, r, cw = land.shape
    else:
        r, cw = src.shape[-2:]
    tr, nb = _row_blocks(r)
    if not into_slot:
        if src.ndim == 3:
            src_spec = pl.BlockSpec((None, tr, cw), lambda i, ids_ref: (lead, i, 0))
        else:
            src_spec = pl.BlockSpec((tr, cw), lambda i, ids_ref: (i, 0))
        if kind == "col":
            dst_spec = pl.BlockSpec((tr, cw), lambda i, ids_ref: (i, ids_ref[0]))
        else:
            dst_spec = pl.BlockSpec((tr, cw), lambda i, ids_ref: (ids_ref[0] * nb + i, 0))
    else:
        if kind == "col":
            src_spec = pl.BlockSpec((tr, cw), lambda i, ids_ref: (i, ids_ref[0]))
        elif kind == "row":
            src_spec = pl.BlockSpec((tr, cw), lambda i, ids_ref: (ids_ref[0] * nb + i, 0))
        else:
            src_spec = pl.BlockSpec((tr, cw), lambda i, ids_ref: (i, 0))
        dst_spec = pl.BlockSpec((None, tr, cw), lambda i, ids_ref: (ids_ref[1], i, 0))

    def body(ids_ref, s_ref, o_ref):
        o_ref[...] = s_ref[...].astype(o_ref.dtype)

    return pl.pallas_call(
        body,
        name=name,
        grid_spec=pltpu.PrefetchScalarGridSpec(num_scalar_prefetch=1, grid=(nb,), in_specs=[src_spec],
                                               out_specs=dst_spec),
        out_shape=land,
        compiler_params=pltpu.CompilerParams(dimension_semantics=("parallel",)),
    )(ids, src)


SEM_SPEC = pl.BlockSpec(memory_space=pltpu.SEMAPHORE)
SIDE_EFFECT = pltpu.SideEffectType.DATAFLOW_SIDE_EFFECTING


def _copies_start(name, srcs, lands, copies, n_sems, after):
    ns, nl, nf = len(srcs), len(lands), len(after)

    def body(*refs):
        src, land = refs[:ns], refs[ns:ns + nl]
        send, recv = refs[ns + nl + nf], refs[ns + nl + nf + 1]
        pin = refs[-1]
        for cp in copies(src, land, send, recv, _mesh_place()):
            cp.start()
        pin[...] = jnp.zeros_like(pin)

    arrs = list(srcs) + list(lands)
    res = pl.pallas_call(
        body,
        name=name,
        in_specs=[HBM_SPEC] * (ns + nl) + [ANY_SPEC] * nf,
        out_specs=[SEM_SPEC, SEM_SPEC] + [HBM_SPEC] * (ns + nl) + [pl.BlockSpec(memory_space=pltpu.VMEM)],
        out_shape=[pltpu.SemaphoreType.DMA((n_sems,)), pltpu.SemaphoreType.DMA((n_sems,))]
        + [pltpu.HBM(a.shape, a.dtype) for a in arrs] + [jax.ShapeDtypeStruct((8, 128), F32)],
        input_output_aliases={i: 2 + i for i in range(ns + nl)},
        compiler_params=pltpu.CompilerParams(has_side_effects=SIDE_EFFECT),
    )(*[pltpu.with_memory_space_constraint(a, pltpu.HBM) for a in arrs], *after)
    return res[0], res[1], list(res[2:2 + ns]), list(res[2 + ns:2 + ns + nl]), res[-1]


def _copies_wait(name, send, recv, srcs, lands, copies, after):
    ns, nl, nf = len(srcs), len(lands), len(after)

    def body(*refs):
        src, land = refs[:ns], refs[ns:ns + nl]
        send_sems, recv_sems = refs[ns + nl], refs[ns + nl + 1]
        cps = copies(src, land, send_sems, recv_sems, _mesh_place())
        for cp in cps:
            cp.wait_send()
        for cp in cps:
            cp.wait_recv()

    arrs = list(srcs) + list(lands)
    res = pl.pallas_call(
        body,
        name=name,
        in_specs=[HBM_SPEC] * (ns + nl) + [SEM_SPEC, SEM_SPEC] + [ANY_SPEC] * nf,
        out_specs=[HBM_SPEC] * (ns + nl),
        out_shape=[pltpu.HBM(a.shape, a.dtype) for a in arrs],
        input_output_aliases={i: i for i in range(ns + nl)},
        compiler_params=pltpu.CompilerParams(has_side_effects=SIDE_EFFECT),
    )(*arrs, send, recv, *after)
    return list(res[ns:])


def _gather_copies(kinds):
    def copies(src, land, send, recv, place):
        x, y, c, chips = place
        s = 2 * x + y
        return [
            pltpu.make_async_remote_copy(
                src_ref=_shard_view(land[t], kinds[t], s), dst_ref=_shard_view(land[t], kinds[t], s),
                send_sem=send.at[3 * t + j], recv_sem=recv.at[3 * t + j], device_id=(px, py, c), device_id_type=MESH)
            for t in range(len(kinds)) for j, (px, py) in enumerate(chips)
        ]

    return copies


def _grad_part(ref, kind, s):
    return ref if kind == "all" else _shard_view(ref, kind, s)


def _grad_copies(kinds):
    def peers(place):
        x, y, c, chips = place
        return [(x, y, 1 - c)] + [(px, py, c) for px, py in chips]

    def copies(src, land, send, recv, place):
        x, y, c, chips = place
        me = 4 * x + 2 * y + c
        return [
            pltpu.make_async_remote_copy(
                src_ref=_grad_part(src[t], kinds[t], 2 * px + py), dst_ref=land[t].at[me],
                send_sem=send.at[GRAD_PEERS * t + k], recv_sem=recv.at[GRAD_PEERS * t + k], device_id=(px, py, pc),
                device_id_type=MESH)
            for t in range(len(kinds)) for k, (px, py, pc) in enumerate(peers(place))
        ]

    return copies


def _pass_on(lands):
    nt = len(lands)

    def body(*refs):
        buf = refs[nt:2 * nt]
        send, recv = refs[2 * nt:]
        x, y, c, chips = _mesh_place()
        sib = (x, y, 1 - c)
        sends = []
        for t in range(nt):
            for j, (px, py) in enumerate(chips):
                mine = buf[t].at[4 * px + 2 * py + c]
                cp = pltpu.make_async_remote_copy(src_ref=mine, dst_ref=mine, send_sem=send.at[t, j],
                                                  recv_sem=recv.at[t, j], device_id=sib, device_id_type=MESH)
                cp.start()
                sends.append(cp)
        for t in range(nt):
            for j, (px, py) in enumerate(chips):
                theirs = buf[t].at[4 * px + 2 * py + (1 - c)]
                pltpu.make_async_remote_copy(src_ref=theirs, dst_ref=theirs, send_sem=send.at[t, j],
                                             recv_sem=recv.at[t, j], device_id=sib, device_id_type=MESH).wait_recv()
        for cp in sends:
            cp.wait_send()

    return pl.pallas_call(
        body,
        name="grads_pass_on",
        in_specs=[HBM_SPEC] * nt,
        out_specs=[HBM_SPEC] * nt,
        out_shape=[jax.ShapeDtypeStruct(a.shape, a.dtype) for a in lands],
        input_output_aliases={i: i for i in range(nt)},
        scratch_shapes=[pltpu.SemaphoreType.DMA((nt, 3))] * 2,
    )(*lands)


GRAD_PEERS = 4


def _land_shape(grad, kind):
    rows, cols = grad.shape
    if kind == "col":
        return (N_DEV, rows, cols // N_CHIPS)
    if kind == "row":
        return (N_DEV, rows // N_CHIPS, cols)
    return (N_DEV, rows, cols)


def _adamw_reduce(name, contrib, w, m, v, layer=None, prev=None):
    rows, cols = w.shape[-2:]
    t = 128 if rows % 128 == 0 else rows
    c1 = 1.0 - ADAM_B1 ** ADAM_STEP
    c2 = 1.0 - ADAM_B2 ** ADAM_STEP

    n_prev = 0 if prev is None else 4

    def body(c_ref, w_ref, m_ref, v_ref, *refs):
        g_ref, d_ref, nm_ref, nv_ref = refs[n_prev:]
        g = c_ref[0].astype(F32)
        for q in range(1, N_DEV):
            g = g + c_ref[q].astype(F32)
        nm = ADAM_B1 * m_ref[...] + (1.0 - ADAM_B1) * g
        nv = ADAM_B2 * v_ref[...] + (1.0 - ADAM_B2) * (g * g)
        g_ref[...] = g
        nm_ref[...] = nm
        nv_ref[...] = nv
        d_ref[...] = -ADAM_LR * ((nm / c1) / (jnp.sqrt(nv / c2) + ADAM_EPS) + ADAM_WD * w_ref[...])

    if layer is None:
        blk = pl.BlockSpec((t, cols), lambda i: (i, 0))
    else:
        blk = pl.BlockSpec((None, t, cols), lambda i: (layer, i, 0))
    return pl.pallas_call(
        body,
        name=name,
        grid=(rows // t,),
        in_specs=[pl.BlockSpec((N_DEV, t, cols), lambda i: (0, i, 0)), blk, blk, blk] + [ANY_SPEC] * n_prev,
        out_specs=[blk] * 4,
        out_shape=[jax.ShapeDtypeStruct(w.shape, F32)] * 4,
        input_output_aliases={4 + i: i for i in range(n_prev)},
        compiler_params=pltpu.CompilerParams(dimension_semantics=("parallel",)),
    )(contrib, w, m, v, *(prev or ()))


WEIGHT_NAMES = ("a_norm_g", "conv_w1", "conv_b1", "conv_dw", "conv_dw_b", "conv_ln_g", "conv_ln_b", "conv_w2",
                "conv_b2", "kv_norm_g", "w_k", "w_v", "b_norm_g", "w_q", "w_o", "ffn_norm_g", "ffn_w_gate",
                "ffn_w_up", "ffn_w_down", "final_norm_g")
GROUPS = {
    "conv2": (("conv_w2", "conv_w2", None, "row"),),
    "ffn0": (("gate", "ffn_w_gate", 0, "col"), ("up", "ffn_w_up", 0, "col"), ("down", "ffn_w_down", 0, "row")),
    "attn": (("w_k", "w_k", None, "row"), ("w_v", "w_v", None, "row"), ("w_q", "w_q", None, "row"),
             ("w_o", "w_o", None, "row")),
    "ffn1": (("gate", "ffn_w_gate", 1, "col"), ("up", "ffn_w_up", 1, "col"), ("down", "ffn_w_down", 1, "row")),
}
FETCH_ORDER = ("conv2", "ffn0", "attn", "ffn1")
EMIT_ORDER = ("ffn1", "attn", "ffn0")
PACKED = (("a_norm_g", 0, 1), ("conv_b1", 8, 2), ("conv_dw", 16, CONV_WIDTH), ("conv_dw_b", 48, 1),
          ("conv_ln_g", 56, 1), ("conv_ln_b", 64, 1), ("conv_b2", 72, 1))
PACK_ROWS = 80
WHOLE = (("kv_norm_g", 0, 1), ("b_norm_g", 1, 1), ("ffn_norm_g", 2, 2), ("final_norm_g", 4, 1))
WHOLE_ROWS = 8


def _pack_rows(parts, total, width):
    out, at = [], 0
    for arr, first in parts:
        if first > at:
            out.append(jnp.zeros((first - at, width), F32))
        rows8 = -(-arr.shape[0] // 8) * 8
        out.append(jnp.pad(arr, ((0, rows8 - arr.shape[0]), (0, 0))))
        at = first + rows8
    if total > at:
        out.append(jnp.zeros((total - at, width), F32))
    return jnp.concatenate(out, axis=0)


def kernel(x, a_norm_g, conv_w1, conv_b1, conv_dw, conv_dw_b, conv_ln_g, conv_ln_b, conv_w2, conv_b2, kv_norm_g, w_k, w_v, b_norm_g, w_q, w_o, ffn_norm_g, ffn_w_gate, ffn_w_up, ffn_w_down, final_norm_g, loss_target, m_a_norm_g, m_conv_w1, m_conv_b1, m_conv_dw, m_conv_dw_b, m_conv_ln_g, m_conv_ln_b, m_conv_w2, m_conv_b2, m_kv_norm_g, m_w_k, m_w_v, m_b_norm_g, m_w_q, m_w_o, m_ffn_norm_g, m_ffn_w_gate, m_ffn_w_up, m_ffn_w_down, m_final_norm_g, v_a_norm_g, v_conv_w1, v_conv_b1, v_conv_dw, v_conv_dw_b, v_conv_ln_g, v_conv_ln_b, v_conv_w2, v_conv_b2, v_kv_norm_g, v_w_k, v_w_v, v_b_norm_g, v_w_q, v_w_o, v_ffn_norm_g, v_ffn_w_gate, v_ffn_w_up, v_ffn_w_down, v_final_norm_g):
    args = locals()
    wts = {n: args[n] for n in WEIGHT_NAMES}
    mom = {n: args["m_" + n] for n in WEIGHT_NAMES}
    vel = {n: args["v_" + n] for n in WEIGHT_NAMES}
    s, d = x.shape[-2:]
    dq = d // N_CHIPS
    x2 = x.reshape(s, d)
    tgt = loss_target.reshape(s, d)

    def shard_bf16(name, layer):
        a = wts[name]
        a = a[layer] if layer is not None else a.reshape(a.shape[-2:])
        return a.astype(BF16)

    def pack_shard(src):
        return _pack_rows([(src[n].reshape(-1, dq), first) for n, first, _ in PACKED], PACK_ROWS, dq)

    def pack_whole(src):
        return _pack_rows([(jnp.concatenate([src[n].reshape(-1, d) for n, _, _ in WHOLE], axis=0), 0)], WHOLE_ROWS, d)

    conv_w1_full, packed_full = _gather_weights([shard_bf16("conv_w1", None), pack_shard(wts)], ["col", "row"])
    w = {"conv_w1": conv_w1_full}
    gathers = {}
    pins = []
    ids = jnp.stack([2 * lax.axis_index("x") + lax.axis_index("y"),
                     4 * lax.axis_index("x") + 2 * lax.axis_index("y") + lax.axis_index("c")]).astype(jnp.int32)
    for grp in FETCH_ORDER:
        kinds = [kind for _, _, _, kind in GROUPS[grp]]
        lands = []
        for key, n, layer, kind in GROUPS[grp]:
            shard = wts[n] if layer is not None else wts[n].reshape(wts[n].shape[-2:])
            full = jax.ShapeDtypeStruct(_full_shape(shard, kind), BF16)
            lands.append(_place_own(f"gather_place_{grp}_{key}", shard, kind, full, ids, False, layer))
        copies = _gather_copies(kinds)
        send, recv, _, lands, pin = _copies_start("gather_start_" + grp, [], lands, copies, 3 * len(lands), pins)
        gathers[grp] = (send, recv, [], lands, copies)
        pins = [pin]

    def fetch(grp, after):
        send, recv, srcs, lands, copies = gathers[grp]
        fulls = _copies_wait("gather_wait_" + grp, send, recv, srcs, lands, copies, [after])
        return {key: full for (key, _, _, _), full in zip(GROUPS[grp], fulls)}

    packed = packed_full.reshape(N_CHIPS, PACK_ROWS, dq)
    for n, first, rows in PACKED:
        part = packed[:, first:first + rows, :]
        if n == "conv_dw":
            w[n] = part.transpose(1, 0, 2).reshape(rows, d)
        else:
            w[n] = part.reshape(1, N_CHIPS * rows * dq)
    for n, _, rows in WHOLE:
        w[n] = wts[n].reshape(rows, d)

    exchanges = {}

    def emit_grads(tag, grads, kinds):
        copies = _grad_copies(kinds)
        lands = [_place_own(f"grads_place_{tag}_{t}", gr, kd, jax.ShapeDtypeStruct(_land_shape(gr, kd), gr.dtype),
                            ids, True) for t, (gr, kd) in enumerate(zip(grads, kinds))]
        send, recv, srcs, lands, pin = _copies_start("grads_start_" + tag, grads, lands, copies,
                                                     GRAD_PEERS * len(grads), ())
        exchanges[tag] = (send, recv, srcs, lands, copies)
        return [pin]

    def emit(grp, grads):
        return emit_grads(grp, [grads[key] for key, _, _, _ in GROUPS[grp]], [kind for _, _, _, kind in GROUPS[grp]])

    loss_cols, dx, g = _local_step(x2, tgt, w, fetch, emit, after=pins)
    loss = lax.psum(jnp.sum(loss_cols), ("x", "y", "c"))

    gp = []
    for n, first, rows in PACKED:
        if n == "conv_dw":
            part = g[n].reshape(rows, N_CHIPS, dq).transpose(1, 0, 2)
        else:
            part = g[n].reshape(N_CHIPS, rows, dq)
        gp.append((part, first))
    g_packed = jnp.concatenate(
        [_pack_rows([(p[ci], first) for p, first in gp], PACK_ROWS, dq) for ci in range(N_CHIPS)], axis=0)
    emit_grads("conv", [g["conv_w2"], g["conv_w1"], g_packed, pack_whole(g)], ["row", "col", "row", "all"])

    contribs = {}
    for tag in EMIT_ORDER + ("conv",):
        send, recv, srcs, lands, copies = exchanges[tag]
        contribs[tag] = _copies_wait("grads_wait_" + tag, send, recv, srcs, lands, copies, [dx])
    passed = _pass_on([a for tag in EMIT_ORDER + ("conv",) for a in contribs[tag]])
    for tag in EMIT_ORDER + ("conv",):
        contribs[tag], passed = passed[:len(contribs[tag])], passed[len(contribs[tag]):]

    res = {}

    def adamw(n, contrib, layer=None, prev=None):
        if layer is None:
            shape = wts[n].shape
            r2 = shape[-2:]
            outs = _adamw_reduce("adamw_" + n, contrib, wts[n].reshape(r2), mom[n].reshape(r2), vel[n].reshape(r2))
            return [o.reshape(shape) for o in outs]
        return _adamw_reduce(f"adamw_{n}_{layer}", contrib, wts[n], mom[n], vel[n], layer, prev)

    for grp in ("conv2", "attn"):
        tag = "conv" if grp == "conv2" else grp
        for (key, n, _, _), contrib in zip(GROUPS[grp], contribs[tag]):
            res[n] = adamw(n, contrib)
    res["conv_w1"] = adamw("conv_w1", contribs["conv"][1])
    for (key, n, _, _), c0, c1 in zip(GROUPS["ffn0"], contribs["ffn0"], contribs["ffn1"]):
        res[n] = adamw(n, c1, 1, adamw(n, c0, 0))
    outs = _adamw_reduce("adamw_packed", contribs["conv"][2], pack_shard(wts), pack_shard(mom), pack_shard(vel))
    for n, first, rows in PACKED:
        res[n] = [o[first:first + rows].reshape(wts[n].shape) for o in outs]
    outs = _adamw_reduce("adamw_whole", contribs["conv"][3], pack_whole(wts), pack_whole(mom), pack_whole(vel))
    for n, first, rows in WHOLE:
        res[n] = [o[first:first + rows].reshape(wts[n].shape) for o in outs]

    out = [loss, dx.reshape(x.shape)]
    for which in range(4):
        out += [res[n][which] for n in WEIGHT_NAMES]
    return tuple(out)
```

```python
import functools
import math

import jax
import jax.numpy as jnp
from jax import lax
from jax.experimental import pallas as pl
from jax.experimental.pallas import tpu as pltpu

F32 = jnp.float32
BF16 = jnp.bfloat16

HEAD_DIM = 128
BRANCHES = ((128, 1), (512, 4), (2048, 16))
CONV_WIDTH = 31
CONV_HALO = 32
RMS_EPS = 1e-6
LN_EPS = 1e-5
ADAM_LR = 0.001
ADAM_B1 = 0.9
ADAM_B2 = 0.999
ADAM_EPS = 1e-08
ADAM_WD = 0.01
ADAM_STEP = 10
N_CHIPS = 4
N_DEV = 8
MESH = pl.DeviceIdType.MESH


def _sigmoid(x):
    return 1.0 / (1.0 + jnp.exp(-x))


def _row_tile(rows, want):
    t = min(rows, want)
    assert rows % t == 0, (rows, want)
    return t


_DOT_DIMS = {"nn": ((1,), (0,)), "nt": ((1,), (1,)), "tn": ((0,), (0,))}


ANY_SPEC = pl.BlockSpec(memory_space=pl.ANY)


def _mm(name, mode, a, bs, epilogue, outs, *, m, n, k, extras=(), bm=1024, bn=512, bk=None, after=()):
    bm, bn = min(bm, m), min(bn, n)
    bk = k if bk is None else min(bk, k)
    assert m % bm == 0 and n % bn == 0 and k % bk == 0, (name, m, n, k, bm, bn, bk)
    nk = k // bk
    a_list = list(a) if isinstance(a, (list, tuple)) else [a]
    na, nb, ne, no = len(a_list), len(bs), len(extras), len(outs)
    assert na in (1, nb)

    if mode == "tn":
        a_spec = pl.BlockSpec((bk, bm), lambda i, j, kk: (kk, i))
    else:
        a_spec = pl.BlockSpec((bm, bk), lambda i, j, kk: (i, kk))

    def b_spec(lead, off):
        if mode == "nt":
            blk, idx = (bn, bk), (lambda i, j, kk: (j + off, kk))
        else:
            blk, idx = (bk, bn), (lambda i, j, kk: (kk, j + off))
        if lead is None:
            return pl.BlockSpec(blk, idx)
        return pl.BlockSpec((None,) + blk, lambda i, j, kk: (lead,) + idx(i, j, kk))

    def e_spec(kind, off):
        if kind == "mn":
            return pl.BlockSpec((bm, bn), lambda i, j, kk: (i, j + off))
        return pl.BlockSpec((1, bn), lambda i, j, kk: (0, j + off))

    nf = len(after)
    in_specs = [a_spec] * na + [b_spec(l, o) for _, l, o in bs] + [e_spec(kd, o) for _, kd, o in extras]
    in_specs += [ANY_SPEC] * nf
    out_specs = [pl.BlockSpec((bm, bn), lambda i, j, kk: (i, j)) for _ in outs]
    out_shape = [jax.ShapeDtypeStruct((m, n), dt) for (dt,) in outs]
    dims = (_DOT_DIMS[mode], ((), ()))

    def body(*refs):
        a_refs = refs[:na]
        b_refs = refs[na:na + nb]
        e_refs = refs[na + nb:na + nb + ne]
        o_refs = refs[na + nb + ne + nf:na + nb + ne + nf + no]
        acc_refs = refs[na + nb + ne + nf + no:]
        avs = [a_ref[...].astype(BF16) for a_ref in a_refs]
        prods = [lax.dot_general(avs[bi % na], b_ref[...].astype(BF16), dims, preferred_element_type=F32)
                 for bi, b_ref in enumerate(b_refs)]

        def finish(accs):
            res = epilogue(accs, [e_ref[...] for e_ref in e_refs])
            for o_ref, r in zip(o_refs, res):
                o_ref[...] = r.astype(o_ref.dtype)

        if nk == 1:
            finish(prods)
        else:
            kk = pl.program_id(2)

            @pl.when(kk == 0)
            def _():
                for acc_ref, p in zip(acc_refs, prods):
                    acc_ref[...] = p

            @pl.when(kk > 0)
            def _():
                for acc_ref, p in zip(acc_refs, prods):
                    acc_ref[...] += p

            @pl.when(kk == nk - 1)
            def _():
                finish([acc_ref[...] for acc_ref in acc_refs])

    scratch = [] if nk == 1 else [pltpu.VMEM((bm, bn), F32) for _ in bs]
    res = pl.pallas_call(
        body,
        name=name,
        grid=(m // bm, n // bn, nk),
        in_specs=in_specs,
        out_specs=out_specs,
        out_shape=out_shape,
        scratch_shapes=scratch,
        compiler_params=pltpu.CompilerParams(dimension_semantics=("parallel", "parallel", "arbitrary")),
    )(*a_list, *[b for b, _, _ in bs], *[e for e, _, _ in extras], *after)
    return res


def _rms_fwd(name, x, gains, after=()):
    s, d = x.shape
    ng = gains.shape[0]
    t = _row_tile(s, 256)
    nf = len(after)

    def body(x_ref, g_ref, *refs):
        o_refs = refs[nf:]
        xv = x_ref[...]
        r = lax.rsqrt(jnp.mean(xv * xv, axis=-1, keepdims=True) + RMS_EPS)
        xh = xv * r
        for gi, o_ref in enumerate(o_refs):
            o_ref[...] = (xh * g_ref[gi:gi + 1, :]).astype(o_ref.dtype)

    return pl.pallas_call(
        body,
        name=name,
        grid=(s // t,),
        in_specs=[pl.BlockSpec((t, d), lambda i: (i, 0)), pl.BlockSpec((ng, d), lambda i: (0, 0))] + [ANY_SPEC] * nf,
        out_specs=[pl.BlockSpec((t, d), lambda i: (i, 0)) for _ in range(ng)],
        out_shape=[jax.ShapeDtypeStruct((s, d), BF16) for _ in range(ng)],
        compiler_params=pltpu.CompilerParams(dimension_semantics=("parallel",)),
    )(x, gains, *after)


def _rms_bwd(name, x, gains, dns, dres):
    s, d = x.shape
    ng = gains.shape[0]
    t = _row_tile(s, 256)

    def body(x_ref, g_ref, dres_ref, *refs):
        dn_refs = refs[:ng]
        dx_ref, dxb_ref, dg_ref, cs_ref = refs[ng:]
        i = pl.program_id(0)
        xv = x_ref[...]
        r = lax.rsqrt(jnp.mean(xv * xv, axis=-1, keepdims=True) + RMS_EPS)
        xh = xv * r
        dx = dres_ref[...]
        dgs = []
        for gi in range(ng):
            dn = dn_refs[gi][...].astype(F32)
            dxh = dn * g_ref[gi:gi + 1, :]
            dgs.append(jnp.sum(dn * xh, axis=0, keepdims=True))
            dx = dx + r * (dxh - xh * jnp.mean(dxh * xh, axis=-1, keepdims=True))
        dx_ref[...] = dx
        dxb_ref[...] = dx.astype(BF16)
        dg = jnp.concatenate(dgs, axis=0) if ng > 1 else dgs[0]
        cs = jnp.sum(dx, axis=0, keepdims=True)

        @pl.when(i == 0)
        def _():
            dg_ref[...] = dg
            cs_ref[...] = cs

        @pl.when(i > 0)
        def _():
            dg_ref[...] += dg
            cs_ref[...] += cs

    row = pl.BlockSpec((t, d), lambda i: (i, 0))
    return pl.pallas_call(
        body,
        name=name,
        grid=(s // t,),
        in_specs=[row, pl.BlockSpec((ng, d), lambda i: (0, 0)), row] + [row] * ng,
        out_specs=[row, row, pl.BlockSpec((ng, d), lambda i: (0, 0)), pl.BlockSpec((1, d), lambda i: (0, 0))],
        out_shape=[
            jax.ShapeDtypeStruct((s, d), F32),
            jax.ShapeDtypeStruct((s, d), BF16),
            jax.ShapeDtypeStruct((ng, d), F32),
            jax.ShapeDtypeStruct((1, d), F32),
        ],
        compiler_params=pltpu.CompilerParams(dimension_semantics=("arbitrary",)),
    )(x, gains, dres, *dns)


def _final_loss(name, h, gain, target):
    s, d = h.shape
    t = _row_tile(s, 256)

    def body(h_ref, g_ref, t_ref, dh_ref, dhb_ref, dg_ref, ls_ref):
        i = pl.program_id(0)
        xv = h_ref[...]
        g = g_ref[...]
        r = lax.rsqrt(jnp.mean(xv * xv, axis=-1, keepdims=True) + RMS_EPS)
        xh = xv * r
        err = xh * g - t_ref[...]
        ls = jnp.sum(err * err, axis=0, keepdims=True) * (0.5 / d)
        dy = err * (1.0 / d)
        dxh = dy * g
        dg = jnp.sum(dy * xh, axis=0, keepdims=True)
        dx = r * (dxh - xh * jnp.mean(dxh * xh, axis=-1, keepdims=True))
        dh_ref[...] = dx
        dhb_ref[...] = dx.astype(BF16)

        @pl.when(i == 0)
        def _():
            dg_ref[...] = dg
            ls_ref[...] = ls

        @pl.when(i > 0)
        def _():
            dg_ref[...] += dg
            ls_ref[...] += ls

    row = pl.BlockSpec((t, d), lambda i: (i, 0))
    vec = pl.BlockSpec((1, d), lambda i: (0, 0))
    return pl.pallas_call(
        body,
        name=name,
        grid=(s // t,),
        in_specs=[row, vec, row],
        out_specs=[row, row, vec, vec],
        out_shape=[
            jax.ShapeDtypeStruct((s, d), F32),
            jax.ShapeDtypeStruct((s, d), BF16),
            jax.ShapeDtypeStruct((1, d), F32),
            jax.ShapeDtypeStruct((1, d), F32),
        ],
        compiler_params=pltpu.CompilerParams(dimension_semantics=("arbitrary",)),
    )(h, gain, target)


def _conv_tiles(s):
    t = _row_tile(s, 128)
    assert t % CONV_HALO == 0
    return t, t // CONV_HALO


def _dwconv_fwd(name, u, dw, dw_b, ln_g, ln_b):
    s, d = u.shape
    t, hb = _conv_tiles(s)
    w = dw.shape[0]
    lo = CONV_HALO - (w - 1)

    def body(cur_ref, prev_ref, dw_ref, dwb_ref, lg_ref, lb_ref, c_ref, sw_ref, cat_ref):
        i = pl.program_id(0)
        cat_ref[CONV_HALO:, :] = cur_ref[...]

        @pl.when(i == 0)
        def _():
            cat_ref[:CONV_HALO, :] = jnp.zeros((CONV_HALO, d), F32)

        @pl.when(i > 0)
        def _():
            cat_ref[:CONV_HALO, :] = prev_ref[...]

        acc = jnp.zeros((t, d), F32) + dwb_ref[...]
        for kk in range(w):
            acc = acc + dw_ref[kk:kk + 1, :] * cat_ref[lo + kk:lo + kk + t, :]
        c_ref[...] = acc
        mu = jnp.mean(acc, axis=-1, keepdims=True)
        cc = acc - mu
        var = jnp.mean(cc * cc, axis=-1, keepdims=True)
        ln = cc * lax.rsqrt(var + LN_EPS) * lg_ref[...] + lb_ref[...]
        sw_ref[...] = (ln * _sigmoid(ln)).astype(BF16)

    row = pl.BlockSpec((t, d), lambda i: (i, 0))
    prev = pl.BlockSpec((CONV_HALO, d), lambda i: (jnp.maximum(i * hb - 1, 0), 0))
    vec = pl.BlockSpec((1, d), lambda i: (0, 0))
    return pl.pallas_call(
        body,
        name=name,
        grid=(s // t,),
        in_specs=[row, prev, pl.BlockSpec((w, d), lambda i: (0, 0)), vec, vec, vec],
        out_specs=[row, row],
        out_shape=[jax.ShapeDtypeStruct((s, d), F32), jax.ShapeDtypeStruct((s, d), BF16)],
        scratch_shapes=[pltpu.VMEM((CONV_HALO + t, d), F32)],
        compiler_params=pltpu.CompilerParams(dimension_semantics=("parallel",)),
    )(u, u, dw, dw_b, ln_g, ln_b)


def _conv_ln_bwd(name, c, dsw, ln_g, ln_b, after=()):
    s, d = c.shape
    t = _row_tile(s, 256)
    nf = len(after)

    def body(c_ref, dsw_ref, lg_ref, lb_ref, *refs):
        dc_ref, sums_ref = refs[nf:]
        i = pl.program_id(0)
        cv = c_ref[...]
        g = lg_ref[...]
        mu = jnp.mean(cv, axis=-1, keepdims=True)
        cc = cv - mu
        rstd = lax.rsqrt(jnp.mean(cc * cc, axis=-1, keepdims=True) + LN_EPS)
        ch = cc * rstd
        ln = ch * g + lb_ref[...]
        sg = _sigmoid(ln)
        dln = dsw_ref[...] * (sg * (1.0 + ln * (1.0 - sg)))
        dch = dln * g
        dc = rstd * (dch - jnp.mean(dch, axis=-1, keepdims=True) - ch * jnp.mean(dch * ch, axis=-1, keepdims=True))
        dc_ref[...] = dc
        sums = jnp.concatenate(
            [
                jnp.sum(dln * ch, axis=0, keepdims=True),
                jnp.sum(dln, axis=0, keepdims=True),
                jnp.sum(dc, axis=0, keepdims=True),
                jnp.zeros((1, d), F32),
            ],
            axis=0,
        )

        @pl.when(i == 0)
        def _():
            sums_ref[...] = sums

        @pl.when(i > 0)
        def _():
            sums_ref[...] += sums

    row = pl.BlockSpec((t, d), lambda i: (i, 0))
    vec = pl.BlockSpec((1, d), lambda i: (0, 0))
    return pl.pallas_call(
        body,
        name=name,
        grid=(s // t,),
        in_specs=[row, row, vec, vec] + [ANY_SPEC] * nf,
        out_specs=[row, pl.BlockSpec((4, d), lambda i: (0, 0))],
        out_shape=[jax.ShapeDtypeStruct((s, d), F32), jax.ShapeDtypeStruct((4, d), F32)],
        compiler_params=pltpu.CompilerParams(dimension_semantics=("arbitrary",)),
    )(c, dsw, ln_g, ln_b, *after)


def _conv_dw_bwd(name, dc, u, a, gt, dw):
    s, d = dc.shape
    t, hb = _conv_tiles(s)
    w = dw.shape[0]
    lo = CONV_HALO - (w - 1)
    nt = s // t

    def body(dc_ref, dcn_ref, u_ref, up_ref, a_ref, gt_ref, dw_ref, dpre_ref, ddw_ref, db_ref, dcat_ref, ucat_ref):
        i = pl.program_id(0)
        dcat_ref[:t, :] = dc_ref[...]
        ucat_ref[CONV_HALO:, :] = u_ref[...]

        @pl.when(i == nt - 1)
        def _():
            dcat_ref[t:, :] = jnp.zeros((CONV_HALO, d), F32)

        @pl.when(i < nt - 1)
        def _():
            dcat_ref[t:, :] = dcn_ref[...]

        @pl.when(i == 0)
        def _():
            ucat_ref[:CONV_HALO, :] = jnp.zeros((CONV_HALO, d), F32)

        @pl.when(i > 0)
        def _():
            ucat_ref[:CONV_HALO, :] = up_ref[...]

        dcv = dc_ref[...]
        du = jnp.zeros((t, d), F32)
        rows = []
        for kk in range(w):
            sh = w - 1 - kk
            du = du + dw_ref[kk:kk + 1, :] * dcat_ref[sh:sh + t, :]
            rows.append(jnp.sum(dcv * ucat_ref[lo + kk:lo + kk + t, :], axis=0, keepdims=True))
        ddw = jnp.concatenate(rows, axis=0)
        av = a_ref[...].astype(F32)
        sg = _sigmoid(gt_ref[...].astype(F32))
        da = du * sg
        dgt = du * av * sg * (1.0 - sg)
        dpre_ref[:, :d] = da.astype(BF16)
        dpre_ref[:, d:] = dgt.astype(BF16)
        db = jnp.concatenate([jnp.sum(da, axis=0, keepdims=True), jnp.sum(dgt, axis=0, keepdims=True)], axis=1)

        @pl.when(i == 0)
        def _():
            ddw_ref[...] = ddw
            db_ref[...] = db

        @pl.when(i > 0)
        def _():
            ddw_ref[...] += ddw
            db_ref[...] += db

    row = pl.BlockSpec((t, d), lambda i: (i, 0))
    nxt = pl.BlockSpec((CONV_HALO, d), lambda i: (jnp.minimum((i + 1) * hb, s // CONV_HALO - 1), 0))
    prev = pl.BlockSpec((CONV_HALO, d), lambda i: (jnp.maximum(i * hb - 1, 0), 0))
    return pl.pallas_call(
        body,
        name=name,
        grid=(nt,),
        in_specs=[row, nxt, row, prev, row, row, pl.BlockSpec((w, d), lambda i: (0, 0))],
        out_specs=[
            pl.BlockSpec((t, 2 * d), lambda i: (i, 0)),
            pl.BlockSpec((w, d), lambda i: (0, 0)),
            pl.BlockSpec((1, 2 * d), lambda i: (0, 0)),
        ],
        out_shape=[
            jax.ShapeDtypeStruct((s, 2 * d), BF16),
            jax.ShapeDtypeStruct((w, d), F32),
            jax.ShapeDtypeStruct((1, 2 * d), F32),
        ],
        scratch_shapes=[pltpu.VMEM((t + CONV_HALO, d), F32), pltpu.VMEM((CONV_HALO + t, d), F32)],
        compiler_params=pltpu.CompilerParams(dimension_semantics=("arbitrary",)),
    )(dc, dc, u, u, a, gt, dw)


def _alibi_slopes(n_heads):
    h = jnp.arange(1, n_heads + 1, dtype=F32)
    return jnp.exp2(-8.0 * h / n_heads)


def _band_masks(bq):
    qi = lax.broadcasted_iota(jnp.int32, (bq, bq), 0)
    kj = lax.broadcasted_iota(jnp.int32, (bq, bq), 1)
    return qi - kj, qi - kj + bq


def _attn_fwd(name, q, k, v, slopes, bq):
    s, dm = q.shape
    nh = dm // HEAD_DIM
    nt = s // bq
    nbr = len(BRANCHES)
    scale = HEAD_DIM ** -0.5
    nt_dims = (((1,), (1,)), ((), ()))

    def body(sl_ref, q_ref, k_ref, v_ref, o_ref, ob_ref, l_ref, tmp, qr, kr, vr, orm, lrm, onat, lnat):
        slope = sl_ref[pl.program_id(0)]
        jc, jp = _band_masks(bq)
        ok_c = jc >= 0
        ok_p0 = jp <= bq
        for bi, (win, dil) in enumerate(BRANCHES):
            ll = s // dil
            nblk = ll // bq
            if dil == 1:
                sq, sk, sv = q_ref, k_ref, v_ref
                d_o, d_l = onat.at[bi], lnat.at[bi]
            else:
                for src, dst in ((q_ref, qr), (k_ref, kr), (v_ref, vr)):
                    tmp[...] = src[...].astype(F32)
                    for r in range(dil):
                        dst[r * ll:(r + 1) * ll, :] = tmp[pl.ds(r, ll, stride=dil), :].astype(BF16)
                sq, sk, sv = qr, kr, vr
                d_o, d_l = orm, lrm
            bias_c = jc.astype(F32) * (slope * dil)
            bias_p = jp.astype(F32) * (slope * dil)

            def tile(ti, carry):
                row = pl.multiple_of(ti * bq, bq)
                prow = pl.multiple_of(jnp.maximum(ti - 1, 0) * bq, bq)
                ok_p = ok_p0 & (lax.rem(ti, nblk) > 0)
                qh = sq[pl.ds(row, bq), :]
                sc = lax.dot_general(qh, sk[pl.ds(row, bq), :], nt_dims, preferred_element_type=F32) * scale - bias_c
                sp = lax.dot_general(qh, sk[pl.ds(prow, bq), :], nt_dims, preferred_element_type=F32) * scale - bias_p
                sc = jnp.where(ok_c, sc, -1e30)
                sp = jnp.where(ok_p, sp, -1e30)
                mx = jnp.maximum(jnp.max(sc, axis=-1, keepdims=True), jnp.max(sp, axis=-1, keepdims=True))
                pc = jnp.exp(sc - mx)
                pp = jnp.exp(sp - mx)
                den = jnp.sum(pc, axis=-1, keepdims=True) + jnp.sum(pp, axis=-1, keepdims=True)
                inv = 1.0 / den
                o = jnp.dot((pc * inv).astype(BF16), sv[pl.ds(row, bq), :], preferred_element_type=F32)
                o = o + jnp.dot((pp * inv).astype(BF16), sv[pl.ds(prow, bq), :], preferred_element_type=F32)
                d_o[pl.ds(row, bq), :] = o
                d_l[pl.ds(row, bq), :] = jnp.broadcast_to(mx + jnp.log(den), (bq, HEAD_DIM))
                return carry

            lax.fori_loop(0, nt, tile, 0, unroll=2)
            if dil > 1:
                for r in range(dil):
                    onat[bi, pl.ds(r, ll, stride=dil), :] = orm[r * ll:(r + 1) * ll, :]
                    lnat[bi, pl.ds(r, ll, stride=dil), :] = lrm[r * ll:(r + 1) * ll, :]

        def merge(ti, carry):
            rows = pl.ds(pl.multiple_of(ti * bq, bq), bq)
            ls = [lnat[bi, rows, :] for bi in range(nbr)]
            mx = functools.reduce(jnp.maximum, ls)
            es = [jnp.exp(l - mx) for l in ls]
            tot = functools.reduce(jnp.add, es)
            inv = 1.0 / tot
            o = functools.reduce(jnp.add, [e * inv * onat[bi, rows, :] for bi, e in enumerate(es)])
            o_ref[rows, :] = o
            ob_ref[rows, :] = o.astype(BF16)
            l_ref[rows, :] = mx + jnp.log(tot)
            return carry

        lax.fori_loop(0, nt, merge, 0)

    head = pl.BlockSpec((s, HEAD_DIM), lambda h: (0, h))
    return pl.pallas_call(
        body,
        name=name,
        grid=(nh,),
        in_specs=[pl.BlockSpec(memory_space=pltpu.SMEM), head, head, head],
        out_specs=[head, head, head],
        out_shape=[jax.ShapeDtypeStruct((s, dm), F32), jax.ShapeDtypeStruct((s, dm), BF16),
                   jax.ShapeDtypeStruct((s, dm), F32)],
        scratch_shapes=[pltpu.VMEM((s, HEAD_DIM), F32)] + [pltpu.VMEM((s, HEAD_DIM), BF16)] * 3
        + [pltpu.VMEM((s, HEAD_DIM), F32)] * 2 + [pltpu.VMEM((nbr, s, HEAD_DIM), F32)] * 2,
        compiler_params=pltpu.CompilerParams(dimension_semantics=("parallel",)),
    )(slopes, q, k, v)


def _attn_bwd(name, q, k, v, o, lse, do, slopes, bq):
    s, dm = q.shape
    nh = dm // HEAD_DIM
    nt = s // bq
    scale = HEAD_DIM ** -0.5
    nt_dims = (((1,), (1,)), ((), ()))
    tn_dims = (((0,), (0,)), ((), ()))

    def body(sl_ref, q_ref, k_ref, v_ref, o_ref, l_ref, do_ref, dq_ref, dk_ref, dv_ref,
             tmp, qr, kr, vr, dor, lr, dlr, dln, dqr, dkr, dvr, aq, ak, av):
        slope = sl_ref[pl.program_id(0)]
        jc, jp = _band_masks(bq)
        ok_c = jc >= 0
        ok_p0 = jp <= bq

        def delta(ti, carry):
            rows = pl.ds(pl.multiple_of(ti * bq, bq), bq)
            dl = jnp.sum(do_ref[rows, :].astype(F32) * o_ref[rows, :], axis=-1, keepdims=True)
            dln[rows, :] = jnp.broadcast_to(dl, (bq, HEAD_DIM))
            return carry

        lax.fori_loop(0, nt, delta, 0)

        for bi, (win, dil) in enumerate(BRANCHES):
            ll = s // dil
            nblk = ll // bq
            if dil == 1:
                sq, sk, sv, sdo, sl, sdl = q_ref, k_ref, v_ref, do_ref, l_ref, dln
                gq, gk, gv = aq, ak, av
            else:
                for src, dst in ((q_ref, qr), (k_ref, kr), (v_ref, vr), (do_ref, dor)):
                    tmp[...] = src[...].astype(F32)
                    for r in range(dil):
                        dst[r * ll:(r + 1) * ll, :] = tmp[pl.ds(r, ll, stride=dil), :].astype(BF16)
                for r in range(dil):
                    lr[r * ll:(r + 1) * ll, :] = l_ref[pl.ds(r, ll, stride=dil), :]
                    dlr[r * ll:(r + 1) * ll, :] = dln[pl.ds(r, ll, stride=dil), :]
                sq, sk, sv, sdo, sl, sdl = qr, kr, vr, dor, lr, dlr
                gq, gk, gv = dqr, dkr, dvr
            if bi == 0 or dil > 1:
                gk[...] = jnp.zeros((s, HEAD_DIM), F32)
                gv[...] = jnp.zeros((s, HEAD_DIM), F32)
            bias_c = jc.astype(F32) * (slope * dil)
            bias_p = jp.astype(F32) * (slope * dil)
            first = bi == 0

            def tile(ti, carry):
                row = pl.ds(pl.multiple_of(ti * bq, bq), bq)
                prow = pl.ds(pl.multiple_of(jnp.maximum(ti - 1, 0) * bq, bq), bq)
                ok_p = ok_p0 & (lax.rem(ti, nblk) > 0)
                qh, doh = sq[row, :], sdo[row, :]
                lc = sl[row, :][:, :1]
                dl = sdl[row, :][:, :1]

                def block(kk_, vv, bias, ok):
                    sc = lax.dot_general(qh, kk_, nt_dims, preferred_element_type=F32) * scale - bias
                    p = jnp.where(ok, jnp.exp(jnp.where(ok, sc, -1e30) - lc), 0.0)
                    dp = lax.dot_general(doh, vv, nt_dims, preferred_element_type=F32)
                    return p.astype(BF16), (p * (dp - dl) * scale).astype(BF16)

                kc, kp = sk[row, :], sk[prow, :]
                p_c, ds_c = block(kc, sv[row, :], bias_c, ok_c)
                p_p, ds_p = block(kp, sv[prow, :], bias_p, ok_p)
                dq = jnp.dot(ds_c, kc, preferred_element_type=F32) + jnp.dot(ds_p, kp, preferred_element_type=F32)
                if first or dil > 1:
                    gq[row, :] = dq
                else:
                    gq[row, :] += dq
                gk[prow, :] += lax.dot_general(ds_p, qh, tn_dims, preferred_element_type=F32)
                gk[row, :] += lax.dot_general(ds_c, qh, tn_dims, preferred_element_type=F32)
                gv[prow, :] += lax.dot_general(p_p, doh, tn_dims, preferred_element_type=F32)
                gv[row, :] += lax.dot_general(p_c, doh, tn_dims, preferred_element_type=F32)
                return carry

            lax.fori_loop(0, nt, tile, 0)
            if dil > 1:
                for acc, rm in ((aq, dqr), (ak, dkr), (av, dvr)):
                    for r in range(dil):
                        acc[pl.ds(r, ll, stride=dil), :] += rm[r * ll:(r + 1) * ll, :]

        dq_ref[...] = aq[...].astype(BF16)
        dk_ref[...] = ak[...].astype(BF16)
        dv_ref[...] = av[...].astype(BF16)

    head = pl.BlockSpec((s, HEAD_DIM), lambda h: (0, h))
    f32buf = pltpu.VMEM((s, HEAD_DIM), F32)
    b16buf = pltpu.VMEM((s, HEAD_DIM), BF16)
    return pl.pallas_call(
        body,
        name=name,
        grid=(nh,),
        in_specs=[pl.BlockSpec(memory_space=pltpu.SMEM)] + [head] * 6,
        out_specs=[head] * 3,
        out_shape=[jax.ShapeDtypeStruct((s, dm), BF16)] * 3,
        scratch_shapes=[f32buf] + [b16buf] * 4 + [f32buf] * 9,
        compiler_params=pltpu.CompilerParams(dimension_semantics=("parallel",)),
    )(slopes, q, k, v, o, lse, do)


def _ep_id(accs, ex):
    return [accs[0]]


def _ep_all(accs, ex):
    return list(accs)


def _ep_sum(accs, ex):
    return [accs[0] + accs[1]]


def _ep_add(accs, ex):
    return [accs[0] + ex[0].astype(F32)]


def _ep_bias_res(accs, ex):
    return [accs[0] + ex[0] + ex[1]]


def _ep_glu(accs, ex):
    a = accs[0] + ex[0]
    gt = accs[1] + ex[1]
    return [a * _sigmoid(gt), a, gt]


def _ep_swiglu(accs, ex):
    g, u = accs
    return [g, u, g * _sigmoid(g) * u]


def _ep_swiglu_bwd(accs, ex):
    dact = accs[0]
    g = ex[0].astype(F32)
    u = ex[1].astype(F32)
    sg = _sigmoid(g)
    return [dact * u * (sg * (1.0 + g * (1.0 - sg))), dact * g * sg]


def _ffn_fwd(tag, h, gain, wg, wu, wd):
    s, d = h.shape
    f = wg.shape[-1]
    (n,) = _rms_fwd(f"{tag}_norm", h, gain)
    g, u, act = _mm(f"{tag}_gate_up", "nn", n, [(wg, None, 0), (wu, None, 0)], _ep_swiglu,
                    [(BF16,), (BF16,), (BF16,)], m=s, n=f, k=d)
    (out,) = _mm(f"{tag}_down", "nn", act, [(wd, None, 0)], _ep_add, [(F32,)], m=s, n=d, k=f,
                 extras=[(h, "mn", 0)], bn=256)
    return out, (n, g, u, act)


def _ffn_bwd(tag, h_in, gain, wg, wu, wd, saved, dh, dhb, after, emit):
    s, d = h_in.shape
    f = wg.shape[-1]
    n, g, u, act = saved
    dg, du = _mm(f"{tag}_bwd_dact", "nt", dhb, [(wd, None, 0)], _ep_swiglu_bwd, [(BF16,), (BF16,)],
                 m=s, n=f, k=d, extras=[(g, "mn", 0), (u, "mn", 0)], after=after)
    (dwd,) = _mm(f"{tag}_bwd_dwd", "tn", act, [(dhb, None, 0)], _ep_id, [(BF16,)], m=f, n=d, k=s,
                 bm=f // 4)
    dwg, dwu = _mm(f"{tag}_bwd_dwgu", "tn", n, [(dg, None, 0), (du, None, 0)], _ep_all, [(BF16,), (BF16,)],
                   m=d, n=f, k=s)
    pin = emit(dict(gate=dwg, up=dwu, down=dwd))
    (dn,) = _mm(f"{tag}_bwd_dn", "nt", [dg, du], [(wg, None, 0), (wu, None, 0)], _ep_sum, [(F32,)],
                m=s, n=d, k=f, bm=512, bn=256, after=pin)
    dx, dxb, dgain, cs = _rms_bwd(f"{tag}_bwd_norm", h_in, gain, [dn], dh)
    return dx, dxb, dgain, cs


def _local_step(x, target, w, fetch, emit, after=()):
    s, d = x.shape
    nh = d // HEAD_DIM
    bq = BRANCHES[0][0] // BRANCHES[0][1]
    assert all(win // dil == bq for win, dil in BRANCHES) and BRANCHES[0][1] == 1
    slopes = _alibi_slopes(nh)
    nd = d // 512 if d >= 512 else 1
    bn = d // nd

    (n1,) = _rms_fwd("a_norm", x, w["a_norm_g"], after=after)
    glu, a, gt = _mm("conv_pw1_glu", "nn", n1, [(w["conv_w1"], None, 0), (w["conv_w1"], None, nd)], _ep_glu,
                     [(F32,), (BF16,), (BF16,)], m=s, n=d, k=d, bn=bn,
                     extras=[(w["conv_b1"], "n", 0), (w["conv_b1"], "n", nd)])
    c, sw = _dwconv_fwd("conv_dw_ln", glu, w["conv_dw"], w["conv_dw_b"], w["conv_ln_g"], w["conv_ln_b"])
    w_conv2 = fetch("conv2", sw)["conv_w2"]
    (h1,) = _mm("conv_pw2", "nn", sw, [(w_conv2, None, 0)], _ep_bias_res, [(F32,)], m=s, n=d, k=d,
                extras=[(w["conv_b2"], "n", 0), (x, "mn", 0)])
    wf0 = fetch("ffn0", h1)
    h2, ffn0 = _ffn_fwd("ffn0", h1, w["ffn_norm_g"][0:1], wf0["gate"], wf0["up"], wf0["down"])
    wa = fetch("attn", h2)
    kvn, qn = _rms_fwd("kvq_norm", h2, jnp.concatenate([w["kv_norm_g"], w["b_norm_g"]], axis=0))
    k, v = _mm("kv_proj", "nn", kvn, [(wa["w_k"], None, 0), (wa["w_v"], None, 0)], _ep_all, [(BF16,), (BF16,)],
               m=s, n=d, k=d)
    (q,) = _mm("q_proj", "nn", qn, [(wa["w_q"], None, 0)], _ep_id, [(BF16,)], m=s, n=d, k=d)
    att, attb, lse = _attn_fwd("attn_fwd", q, k, v, slopes, bq)
    (h3,) = _mm("o_proj", "nn", attb, [(wa["w_o"], None, 0)], _ep_add, [(F32,)], m=s, n=d, k=d,
                extras=[(h2, "mn", 0)])
    wf1 = fetch("ffn1", h3)
    h4, ffn1 = _ffn_fwd("ffn1", h3, w["ffn_norm_g"][1:2], wf1["gate"], wf1["up"], wf1["down"])
    dh4, dh4b, d_final_g, loss_cols = _final_loss("final_loss", h4, w["final_norm_g"], target)

    g = {}
    ga = {}
    dh3, dh3b, dgain1, _ = _ffn_bwd("ffn1", h3, w["ffn_norm_g"][1:2], wf1["gate"], wf1["up"], wf1["down"], ffn1,
                                    dh4, dh4b, (), functools.partial(emit, "ffn1"))
    (datt,) = _mm("o_proj_bwd_dx", "nt", dh3b, [(wa["w_o"], None, 0)], _ep_id, [(BF16,)], m=s, n=d, k=d)
    (ga["w_o"],) = _mm("o_proj_bwd_dw", "tn", attb, [(dh3b, None, 0)], _ep_id, [(BF16,)], m=d, n=d, k=s)
    dq, dk, dv = _attn_bwd("attn_bwd", q, k, v, att, lse, datt, slopes, bq)
    (ga["w_q"],) = _mm("q_proj_bwd_dw", "tn", qn, [(dq, None, 0)], _ep_id, [(BF16,)], m=d, n=d, k=s)
    ga["w_k"], ga["w_v"] = _mm("kv_proj_bwd_dw", "tn", kvn, [(dk, None, 0), (dv, None, 0)], _ep_all,
                               [(BF16,), (BF16,)], m=d, n=d, k=s)
    pin = emit("attn", ga)
    (dqn,) = _mm("q_proj_bwd_dx", "nt", dq, [(wa["w_q"], None, 0)], _ep_id, [(F32,)], m=s, n=d, k=d, after=pin)
    (dkvn,) = _mm("kv_proj_bwd_dx", "nt", [dk, dv], [(wa["w_k"], None, 0), (wa["w_v"], None, 0)], _ep_sum, [(F32,)],
                  m=s, n=d, k=d)
    dh2, dh2b, dg_kvq, _ = _rms_bwd("kvq_norm_bwd", h2, jnp.concatenate([w["kv_norm_g"], w["b_norm_g"]], axis=0),
                                    [dkvn, dqn], dh3)
    dh1, dh1b, dgain0, cs_h1 = _ffn_bwd("ffn0", h1, w["ffn_norm_g"][0:1], wf0["gate"], wf0["up"], wf0["down"], ffn0,
                                        dh2, dh2b, (), functools.partial(emit, "ffn0"))
    (dsw,) = _mm("conv_pw2_bwd_dx", "nt", dh1b, [(w_conv2, None, 0)], _ep_id, [(F32,)], m=s, n=d, k=d)
    (dw2,) = _mm("conv_pw2_bwd_dw", "tn", sw, [(dh1b, None, 0)], _ep_id, [(BF16,)], m=d, n=d, k=s)
    pin = emit("conv2", dict(conv_w2=dw2))
    dc, ln_sums = _conv_ln_bwd("conv_ln_bwd", c, dsw, w["conv_ln_g"], w["conv_ln_b"], after=pin)
    dpre, ddw, db1 = _conv_dw_bwd("conv_dw_bwd", dc, glu, a, gt, w["conv_dw"])
    (dw1,) = _mm("conv_pw1_bwd_dw", "tn", n1, [(dpre, None, 0)], _ep_id, [(BF16,)], m=d, n=2 * d, k=s)
    pin = emit("conv1", dict(conv_w1=dw1))
    (dn1,) = _mm("conv_pw1_bwd_dx", "nt", dpre, [(w["conv_w1"], None, 0)], _ep_id, [(F32,)], m=s, n=d, k=2 * d,
                 after=pin)
    dx, _, d_a_norm, _ = _rms_bwd("a_norm_bwd", x, w["a_norm_g"], [dn1], dh1)

    g.update(
        a_norm_g=d_a_norm, conv_b1=db1, conv_dw=ddw, conv_dw_b=ln_sums[2:3], conv_ln_g=ln_sums[0:1],
        conv_ln_b=ln_sums[1:2], conv_b2=cs_h1, kv_norm_g=dg_kvq[0:1], b_norm_g=dg_kvq[1:2],
        ffn_norm_g=jnp.concatenate([dgain0, dgain1], axis=0), final_norm_g=d_final_g,
    )
    return loss_cols, dx, g


HBM_SPEC = pl.BlockSpec(memory_space=pltpu.HBM)


def _mesh_place():
    x, y, c = lax.axis_index("x"), lax.axis_index("y"), lax.axis_index("c")
    chips = [(1 - x, y), (x, 1 - y), (1 - x, 1 - y)]
    return x, y, c, chips


def _shard_view(ref, kind, s, half=None):
    rows, cols = ref.shape
    if kind == "col":
        cw = cols // N_CHIPS
        if half is None:
            return ref.at[pl.ds(0, rows), pl.ds(s * cw, cw)]
        return ref.at[pl.ds(half * (rows // 2), rows // 2), pl.ds(s * cw, cw)]
    r = rows // N_CHIPS
    if half is None:
        return ref.at[pl.ds(s * r, r), pl.ds(0, cols)]
    return ref.at[pl.ds(s * r + half * (r // 2), r // 2), pl.ds(0, cols)]


def _full_shape(shard, kind):
    r, cw = shard.shape[-2:]
    return (r, cw * N_CHIPS) if kind == "col" else (r * N_CHIPS, cw)


def _gather_weights(shards, kinds, after=()):
    nt = len(shards)
    nf = len(after)
    fulls = [jax.ShapeDtypeStruct(_full_shape(sh, kind), sh.dtype) for sh, kind in zip(shards, kinds)]

    def body(*refs):
        src = refs[:nt]
        dst = refs[nt + nf:2 * nt + nf]
        send, recv, fsend, frecv, local = refs[2 * nt + nf:]
        x, y, c, chips = _mesh_place()
        s = 2 * x + y
        sib = (x, y, 1 - c)

        def half_of_shard(t):
            r, cw = src[t].shape
            return src[t].at[pl.ds(c * (r // 2), r // 2), pl.ds(0, cw)]

        locals_ = [pltpu.make_async_copy(src[t], _shard_view(dst[t], kinds[t], s), local.at[t]) for t in range(nt)]
        for cp in locals_:
            cp.start()
        sends = []
        for t in range(nt):
            for j, chip in enumerate(chips):
                cp = pltpu.make_async_remote_copy(
                    src_ref=half_of_shard(t), dst_ref=_shard_view(dst[t], kinds[t], s, c),
                    send_sem=send.at[t, j], recv_sem=recv.at[t, j], device_id=(*chip, c), device_id_type=MESH)
                cp.start()
                sends.append(cp)
        for t in range(nt):
            for j, (px, py) in enumerate(chips):
                landed = _shard_view(dst[t], kinds[t], 2 * px + py, c)
                pltpu.make_async_remote_copy(
                    src_ref=half_of_shard(t), dst_ref=landed, send_sem=send.at[t, j], recv_sem=recv.at[t, j],
                    device_id=(px, py, c), device_id_type=MESH).wait_recv()
                cp = pltpu.make_async_remote_copy(
                    src_ref=landed, dst_ref=landed, send_sem=fsend.at[t, j], recv_sem=frecv.at[t, j],
                    device_id=sib, device_id_type=MESH)
                cp.start()
                sends.append(cp)
        for t in range(nt):
            for j, (px, py) in enumerate(chips):
                other = _shard_view(dst[t], kinds[t], 2 * px + py, 1 - c)
                pltpu.make_async_remote_copy(
                    src_ref=other, dst_ref=other, send_sem=fsend.at[t, j], recv_sem=frecv.at[t, j],
                    device_id=sib, device_id_type=MESH).wait_recv()
        for cp in sends:
            cp.wait_send()
        for cp in locals_:
            cp.wait()

    return pl.pallas_call(
        body,
        name="gather_weights",
        in_specs=[HBM_SPEC] * nt + [ANY_SPEC] * nf,
        out_specs=[HBM_SPEC] * nt,
        out_shape=fulls,
        scratch_shapes=[pltpu.SemaphoreType.DMA((nt, 3))] * 4 + [pltpu.SemaphoreType.DMA((nt,))],
    )(*shards, *after)


def _row_blocks(rows, want=512):
    nb = 1
    while rows // nb > want or rows % nb or (rows // nb) % 16:
        nb += 1
        if nb > rows:
            return rows, 1
    return rows // nb, nb


def _place_own(name, src, kind, land, ids, into_slot, lead=0):
    if into_slot:
        _, r, cw = land.shape
    else:
        r, cw = src.shape[-2:]
    tr, nb = _row_blocks(r)
    if not into_slot:
        if src.ndim == 3:
            src_spec = pl.BlockSpec((None, tr, cw), lambda i, ids_ref: (lead, i, 0))
        else:
            src_spec = pl.BlockSpec((tr, cw), lambda i, ids_ref: (i, 0))
        if kind == "col":
            dst_spec = pl.BlockSpec((tr, cw), lambda i, ids_ref: (i, ids_ref[0]))
        else:
            dst_spec = pl.BlockSpec((tr, cw), lambda i, ids_ref: (ids_ref[0] * nb + i, 0))
    else:
        if kind == "col":
            src_spec = pl.BlockSpec((tr, cw), lambda i, ids_ref: (i, ids_ref[0]))
        elif kind == "row":
            src_spec = pl.BlockSpec((tr, cw), lambda i, ids_ref: (ids_ref[0] * nb + i, 0))
        else:
            src_spec = pl.BlockSpec((tr, cw), lambda i, ids_ref: (i, 0))
        dst_spec = pl.BlockSpec((None, tr, cw), lambda i, ids_ref: (ids_ref[1], i, 0))

    def body(ids_ref, s_ref, o_ref):
        o_ref[...] = s_ref[...].astype(o_ref.dtype)

    return pl.pallas_call(
        body,
        name=name,
        grid_spec=pltpu.PrefetchScalarGridSpec(num_scalar_prefetch=1, grid=(nb,), in_specs=[src_spec],
                                               out_specs=dst_spec),
        out_shape=land,
        compiler_params=pltpu.CompilerParams(dimension_semantics=("parallel",)),
    )(ids, src)


SEM_SPEC = pl.BlockSpec(memory_space=pltpu.SEMAPHORE)
SIDE_EFFECT = pltpu.SideEffectType.DATAFLOW_SIDE_EFFECTING


def _copies_start(name, srcs, lands, copies, n_sems, after):
    ns, nl, nf = len(srcs), len(lands), len(after)

    def body(*refs):
        src, land = refs[:ns], refs[ns:ns + nl]
        send, recv = refs[ns + nl + nf], refs[ns + nl + nf + 1]
        pin = refs[-1]
        for cp in copies(src, land, send, recv, _mesh_place()):
            cp.start()
        pin[...] = jnp.zeros_like(pin)

    arrs = list(srcs) + list(lands)
    res = pl.pallas_call(
        body,
        name=name,
        in_specs=[HBM_SPEC] * (ns + nl) + [ANY_SPEC] * nf,
        out_specs=[SEM_SPEC, SEM_SPEC] + [HBM_SPEC] * (ns + nl) + [pl.BlockSpec(memory_space=pltpu.VMEM)],
        out_shape=[pltpu.SemaphoreType.DMA((n_sems,)), pltpu.SemaphoreType.DMA((n_sems,))]
        + [pltpu.HBM(a.shape, a.dtype) for a in arrs] + [jax.ShapeDtypeStruct((8, 128), F32)],
        input_output_aliases={i: 2 + i for i in range(ns + nl)},
        compiler_params=pltpu.CompilerParams(has_side_effects=SIDE_EFFECT),
    )(*[pltpu.with_memory_space_constraint(a, pltpu.HBM) for a in arrs], *after)
    return res[0], res[1], list(res[2:2 + ns]), list(res[2 + ns:2 + ns + nl]), res[-1]


def _copies_wait(name, send, recv, srcs, lands, copies, after):
    ns, nl, nf = len(srcs), len(lands), len(after)

    def body(*refs):
        src, land = refs[:ns], refs[ns:ns + nl]
        send_sems, recv_sems = refs[ns + nl], refs[ns + nl + 1]
        cps = copies(src, land, send_sems, recv_sems, _mesh_place())
        for cp in cps:
            cp.wait_send()
        for cp in cps:
            cp.wait_recv()

    arrs = list(srcs) + list(lands)
    res = pl.pallas_call(
        body,
        name=name,
        in_specs=[HBM_SPEC] * (ns + nl) + [SEM_SPEC, SEM_SPEC] + [ANY_SPEC] * nf,
        out_specs=[HBM_SPEC] * (ns + nl),
        out_shape=[pltpu.HBM(a.shape, a.dtype) for a in arrs],
        input_output_aliases={i: i for i in range(ns + nl)},
        compiler_params=pltpu.CompilerParams(has_side_effects=SIDE_EFFECT),
    )(*arrs, send, recv, *after)
    return list(res[ns:])


def _gather_copies(kinds, halves):
    def copies(src, land, send, recv, place):
        x, y, c, chips = place
        s = 2 * x + y
        mine = [_shard_view(land[t], kinds[t], s, c if halves else None) for t in range(len(kinds))]
        return [
            pltpu.make_async_remote_copy(
                src_ref=mine[t], dst_ref=mine[t], send_sem=send.at[3 * t + j], recv_sem=recv.at[3 * t + j],
                device_id=(px, py, c), device_id_type=MESH)
            for t in range(len(kinds)) for j, (px, py) in enumerate(chips)
        ]

    return copies


def _gather_pass_on(name, lands, kinds):
    nt = len(lands)

    def body(*refs):
        buf = refs[nt:2 * nt]
        send, recv = refs[2 * nt:]
        x, y, c, chips = _mesh_place()
        sib = (x, y, 1 - c)
        sends = []
        for t in range(nt):
            for j, (px, py) in enumerate(chips):
                mine = _shard_view(buf[t], kinds[t], 2 * px + py, c)
                cp = pltpu.make_async_remote_copy(src_ref=mine, dst_ref=mine, send_sem=send.at[t, j],
                                                  recv_sem=recv.at[t, j], device_id=sib, device_id_type=MESH)
                cp.start()
                sends.append(cp)
        for t in range(nt):
            for j, (px, py) in enumerate(chips):
                theirs = _shard_view(buf[t], kinds[t], 2 * px + py, 1 - c)
                pltpu.make_async_remote_copy(src_ref=theirs, dst_ref=theirs, send_sem=send.at[t, j],
                                             recv_sem=recv.at[t, j], device_id=sib, device_id_type=MESH).wait_recv()
        for cp in sends:
            cp.wait_send()

    return pl.pallas_call(
        body,
        name=name,
        in_specs=[HBM_SPEC] * nt,
        out_specs=[HBM_SPEC] * nt,
        out_shape=[jax.ShapeDtypeStruct(a.shape, a.dtype) for a in lands],
        input_output_aliases={i: i for i in range(nt)},
        scratch_shapes=[pltpu.SemaphoreType.DMA((nt, 3))] * 2,
    )(*lands)


def _grad_part(ref, kind, s):
    return ref if kind == "all" else _shard_view(ref, kind, s)


def _grad_copies(kinds):
    def peers(place):
        x, y, c, chips = place
        return [(x, y, 1 - c)] + [(px, py, c) for px, py in chips]

    def copies(src, land, send, recv, place):
        x, y, c, chips = place
        me = 4 * x + 2 * y + c
        return [
            pltpu.make_async_remote_copy(
                src_ref=_grad_part(src[t], kinds[t], 2 * px + py), dst_ref=land[t].at[me],
                send_sem=send.at[GRAD_PEERS * t + k], recv_sem=recv.at[GRAD_PEERS * t + k], device_id=(px, py, pc),
                device_id_type=MESH)
            for t in range(len(kinds)) for k, (px, py, pc) in enumerate(peers(place))
        ]

    return copies


def _pass_copies(n):
    def copies(src, land, send, recv, place):
        x, y, c, chips = place
        return [
            pltpu.make_async_remote_copy(
                src_ref=land[t].at[4 * px + 2 * py + c], dst_ref=land[t].at[4 * px + 2 * py + c],
                send_sem=send.at[3 * t + j], recv_sem=recv.at[3 * t + j], device_id=(x, y, 1 - c),
                device_id_type=MESH)
            for t in range(n) for j, (px, py) in enumerate(chips)
        ]

    return copies


GRAD_PEERS = 4


def _land_shape(grad, kind):
    rows, cols = grad.shape
    if kind == "col":
        return (N_DEV, rows, cols // N_CHIPS)
    if kind == "row":
        return (N_DEV, rows // N_CHIPS, cols)
    return (N_DEV, rows, cols)


def _adamw_reduce(name, contrib, w, m, v, layer=None, prev=None):
    rows, cols = w.shape[-2:]
    t = 128 if rows % 128 == 0 else rows
    c1 = 1.0 - ADAM_B1 ** ADAM_STEP
    c2 = 1.0 - ADAM_B2 ** ADAM_STEP

    n_prev = 0 if prev is None else 4

    def body(c_ref, w_ref, m_ref, v_ref, *refs):
        g_ref, d_ref, nm_ref, nv_ref = refs[n_prev:]
        g = c_ref[0].astype(F32)
        for q in range(1, N_DEV):
            g = g + c_ref[q].astype(F32)
        nm = ADAM_B1 * m_ref[...] + (1.0 - ADAM_B1) * g
        nv = ADAM_B2 * v_ref[...] + (1.0 - ADAM_B2) * (g * g)
        g_ref[...] = g
        nm_ref[...] = nm
        nv_ref[...] = nv
        d_ref[...] = -ADAM_LR * ((nm / c1) / (jnp.sqrt(nv / c2) + ADAM_EPS) + ADAM_WD * w_ref[...])

    if layer is None:
        blk = pl.BlockSpec((t, cols), lambda i: (i, 0))
    else:
        blk = pl.BlockSpec((None, t, cols), lambda i: (layer, i, 0))
    return pl.pallas_call(
        body,
        name=name,
        grid=(rows // t,),
        in_specs=[pl.BlockSpec((N_DEV, t, cols), lambda i: (0, i, 0)), blk, blk, blk] + [ANY_SPEC] * n_prev,
        out_specs=[blk] * 4,
        out_shape=[jax.ShapeDtypeStruct(w.shape, F32)] * 4,
        input_output_aliases={4 + i: i for i in range(n_prev)},
        compiler_params=pltpu.CompilerParams(dimension_semantics=("parallel",)),
    )(contrib, w, m, v, *(prev or ()))


WEIGHT_NAMES = ("a_norm_g", "conv_w1", "conv_b1", "conv_dw", "conv_dw_b", "conv_ln_g", "conv_ln_b", "conv_w2",
                "conv_b2", "kv_norm_g", "w_k", "w_v", "b_norm_g", "w_q", "w_o", "ffn_norm_g", "ffn_w_gate",
                "ffn_w_up", "ffn_w_down", "final_norm_g")
GROUPS = {
    "conv2": (("conv_w2", "conv_w2", None, "row"),),
    "ffn0": (("gate", "ffn_w_gate", 0, "col"), ("up", "ffn_w_up", 0, "col"), ("down", "ffn_w_down", 0, "row")),
    "attn": (("w_k", "w_k", None, "row"), ("w_v", "w_v", None, "row"), ("w_q", "w_q", None, "row"),
             ("w_o", "w_o", None, "row")),
    "ffn1": (("gate", "ffn_w_gate", 1, "col"), ("up", "ffn_w_up", 1, "col"), ("down", "ffn_w_down", 1, "row")),
}
GROUPS["conv1"] = (("conv_w1", "conv_w1", None, "col"),)
FETCH_ORDER = ("conv2", "ffn0", "attn", "ffn1")
HALVED = ("conv2", "ffn0")
EMIT_ORDER = ("ffn1", "attn", "ffn0", "conv2", "conv1", "vec")
RETIRE_AT = {"attn": ("ffn1",), "ffn0": ("attn",), "conv1": ("ffn0", "conv2")}
PACKED = (("a_norm_g", 0, 1), ("conv_b1", 8, 2), ("conv_dw", 16, CONV_WIDTH), ("conv_dw_b", 48, 1),
          ("conv_ln_g", 56, 1), ("conv_ln_b", 64, 1), ("conv_b2", 72, 1))
PACK_ROWS = 80
WHOLE = (("kv_norm_g", 0, 1), ("b_norm_g", 1, 1), ("ffn_norm_g", 2, 2), ("final_norm_g", 4, 1))
WHOLE_ROWS = 8


def _pack_rows(parts, total, width):
    out, at = [], 0
    for arr, first in parts:
        if first > at:
            out.append(jnp.zeros((first - at, width), F32))
        rows8 = -(-arr.shape[0] // 8) * 8
        out.append(jnp.pad(arr, ((0, rows8 - arr.shape[0]), (0, 0))))
        at = first + rows8
    if total > at:
        out.append(jnp.zeros((total - at, width), F32))
    return jnp.concatenate(out, axis=0)


def kernel(x, a_norm_g, conv_w1, conv_b1, conv_dw, conv_dw_b, conv_ln_g, conv_ln_b, conv_w2, conv_b2, kv_norm_g, w_k, w_v, b_norm_g, w_q, w_o, ffn_norm_g, ffn_w_gate, ffn_w_up, ffn_w_down, final_norm_g, loss_target, m_a_norm_g, m_conv_w1, m_conv_b1, m_conv_dw, m_conv_dw_b, m_conv_ln_g, m_conv_ln_b, m_conv_w2, m_conv_b2, m_kv_norm_g, m_w_k, m_w_v, m_b_norm_g, m_w_q, m_w_o, m_ffn_norm_g, m_ffn_w_gate, m_ffn_w_up, m_ffn_w_down, m_final_norm_g, v_a_norm_g, v_conv_w1, v_conv_b1, v_conv_dw, v_conv_dw_b, v_conv_ln_g, v_conv_ln_b, v_conv_w2, v_conv_b2, v_kv_norm_g, v_w_k, v_w_v, v_b_norm_g, v_w_q, v_w_o, v_ffn_norm_g, v_ffn_w_gate, v_ffn_w_up, v_ffn_w_down, v_final_norm_g):
    args = locals()
    wts = {n: args[n] for n in WEIGHT_NAMES}
    mom = {n: args["m_" + n] for n in WEIGHT_NAMES}
    vel = {n: args["v_" + n] for n in WEIGHT_NAMES}
    s, d = x.shape[-2:]
    dq = d // N_CHIPS
    x2 = x.reshape(s, d)
    tgt = loss_target.reshape(s, d)

    def shard_bf16(name, layer):
        a = wts[name]
        a = a[layer] if layer is not None else a.reshape(a.shape[-2:])
        return a.astype(BF16)

    def pack_shard(src):
        return _pack_rows([(src[n].reshape(-1, dq), first) for n, first, _ in PACKED], PACK_ROWS, dq)

    def pack_whole(src):
        return _pack_rows([(jnp.concatenate([src[n].reshape(-1, d) for n, _, _ in WHOLE], axis=0), 0)], WHOLE_ROWS, d)

    gathers = {}
    pins = []
    ids = jnp.stack([2 * lax.axis_index("x") + lax.axis_index("y"),
                     4 * lax.axis_index("x") + 2 * lax.axis_index("y") + lax.axis_index("c")]).astype(jnp.int32)

    def start_gather(grp, pins):
        kinds = [kind for _, _, _, kind in GROUPS[grp]]
        lands = []
        for key, n, layer, kind in GROUPS[grp]:
            shard = wts[n] if layer is not None else wts[n].reshape(wts[n].shape[-2:])
            full = jax.ShapeDtypeStruct(_full_shape(shard, kind), BF16)
            lands.append(_place_own(f"gather_place_{grp}_{key}", shard, kind, full, ids, False, layer))
        copies = _gather_copies(kinds, grp in HALVED)
        send, recv, _, lands, pin = _copies_start("gather_start_" + grp, [], lands, copies, 3 * len(lands), pins)
        gathers[grp] = (send, recv, [], lands, copies, kinds)
        return [pin]

    for grp in HALVED:
        pins = start_gather(grp, pins)
    conv_w1_full, packed_full = _gather_weights([shard_bf16("conv_w1", None), pack_shard(wts)], ["col", "row"], pins)
    w = {"conv_w1": conv_w1_full}
    pins = [packed_full]
    for grp in FETCH_ORDER:
        if grp not in HALVED:
            pins = start_gather(grp, pins)

    def fetch(grp, after):
        send, recv, srcs, lands, copies, kinds = gathers[grp]
        fulls = _copies_wait("gather_wait_" + grp, send, recv, srcs, lands, copies, [after])
        if grp in HALVED:
            fulls = _gather_pass_on("gather_pass_" + grp, fulls, kinds)
        return {key: full for (key, _, _, _), full in zip(GROUPS[grp], fulls)}

    packed = packed_full.reshape(N_CHIPS, PACK_ROWS, dq)
    for n, first, rows in PACKED:
        part = packed[:, first:first + rows, :]
        if n == "conv_dw":
            w[n] = part.transpose(1, 0, 2).reshape(rows, d)
        else:
            w[n] = part.reshape(1, N_CHIPS * rows * dq)
    for n, _, rows in WHOLE:
        w[n] = wts[n].reshape(rows, d)

    exchanges = {}
    passing = {}

    def retire(tag, after):
        send, recv, srcs, lands, copies = exchanges[tag]
        lands = _copies_wait("grads_wait_" + tag, send, recv, srcs, lands, copies, after)
        copies = _pass_copies(len(lands))
        send, recv, _, lands, pin = _copies_start("grads_pass_" + tag, [], lands, copies, 3 * len(lands), ())
        passing[tag] = (send, recv, lands, copies)
        return pin

    def emit_grads(tag, grads, kinds):
        pins = [retire(old, [grads[0]]) for old in RETIRE_AT.get(tag, ())]
        copies = _grad_copies(kinds)
        lands = [_place_own(f"grads_place_{tag}_{t}", gr, kd, jax.ShapeDtypeStruct(_land_shape(gr, kd), gr.dtype),
                            ids, True) for t, (gr, kd) in enumerate(zip(grads, kinds))]
        send, recv, srcs, lands, pin = _copies_start("grads_start_" + tag, grads, lands, copies,
                                                     GRAD_PEERS * len(grads), pins)
        exchanges[tag] = (send, recv, srcs, lands, copies)
        return [pin]

    def emit(grp, grads):
        return emit_grads(grp, [grads[key] for key, _, _, _ in GROUPS[grp]], [kind for _, _, _, kind in GROUPS[grp]])

    loss_cols, dx, g = _local_step(x2, tgt, w, fetch, emit, after=pins)
    loss = lax.psum(jnp.sum(loss_cols), ("x", "y", "c"))

    gp = []
    for n, first, rows in PACKED:
        if n == "conv_dw":
            part = g[n].reshape(rows, N_CHIPS, dq).transpose(1, 0, 2)
        else:
            part = g[n].reshape(N_CHIPS, rows, dq)
        gp.append((part, first))
    g_packed = jnp.concatenate(
        [_pack_rows([(p[ci], first) for p, first in gp], PACK_ROWS, dq) for ci in range(N_CHIPS)], axis=0)
    emit_grads("vec", [g_packed, pack_whole(g)], ["row", "all"])
    for tag in EMIT_ORDER:
        if tag not in passing:
            retire(tag, [dx])
    contribs = {}
    for tag in EMIT_ORDER:
        send, recv, lands, copies = passing[tag]
        contribs[tag] = _copies_wait("grads_passed_" + tag, send, recv, [], lands, copies, [dx])

    res = {}

    def adamw(n, contrib, layer=None, prev=None):
        if layer is None:
            shape = wts[n].shape
            r2 = shape[-2:]
            outs = _adamw_reduce("adamw_" + n, contrib, wts[n].reshape(r2), mom[n].reshape(r2), vel[n].reshape(r2))
            return [o.reshape(shape) for o in outs]
        return _adamw_reduce(f"adamw_{n}_{layer}", contrib, wts[n], mom[n], vel[n], layer, prev)

    for grp in ("attn", "conv2", "conv1"):
        for (key, n, _, _), contrib in zip(GROUPS[grp], contribs[grp]):
            res[n] = adamw(n, contrib)
    for (key, n, _, _), c0, c1 in zip(GROUPS["ffn0"], contribs["ffn0"], contribs["ffn1"]):
        res[n] = adamw(n, c1, 1, adamw(n, c0, 0))
    outs = _adamw_reduce("adamw_packed", contribs["vec"][0], pack_shard(wts), pack_shard(mom), pack_shard(vel))
    for n, first, rows in PACKED:
        res[n] = [o[first:first + rows].reshape(wts[n].shape) for o in outs]
    outs = _adamw_reduce("adamw_whole", contribs["vec"][1], pack_whole(wts), pack_whole(mom), pack_whole(vel))
    for n, first, rows in WHOLE:
        res[n] = [o[first:first + rows].reshape(wts[n].shape) for o in outs]

    out = [loss, dx.reshape(x.shape)]
    for which in range(4):
        out += [res[n][which] for n in WEIGHT_NAMES]
    return tuple(out)
```

```python
import functools
import math

import jax
import jax.numpy as jnp
from jax import lax
from jax.experimental import pallas as pl
from jax.experimental.pallas import tpu as pltpu

F32 = jnp.float32
BF16 = jnp.bfloat16

HEAD_DIM = 128
BRANCHES = ((128, 1), (512, 4), (2048, 16))
CONV_WIDTH = 31
CONV_HALO = 32
RMS_EPS = 1e-6
LN_EPS = 1e-5
ADAM_LR = 0.001
ADAM_B1 = 0.9
ADAM_B2 = 0.999
ADAM_EPS = 1e-08
ADAM_WD = 0.01
ADAM_STEP = 10
N_CHIPS = 4
N_DEV = 8
MESH = pl.DeviceIdType.MESH


def _sigmoid(x):
    return 1.0 / (1.0 + jnp.exp(-x))


def _row_tile(rows, want):
    t = min(rows, want)
    assert rows % t == 0, (rows, want)
    return t


_DOT_DIMS = {"nn": ((1,), (0,)), "nt": ((1,), (1,)), "tn": ((0,), (0,))}


ANY_SPEC = pl.BlockSpec(memory_space=pl.ANY)


def _mm(name, mode, a, bs, epilogue, outs, *, m, n, k, extras=(), bm=1024, bn=512, bk=None, after=()):
    bm, bn = min(bm, m), min(bn, n)
    bk = k if bk is None else min(bk, k)
    assert m % bm == 0 and n % bn == 0 and k % bk == 0, (name, m, n, k, bm, bn, bk)
    nk = k // bk
    a_list = list(a) if isinstance(a, (list, tuple)) else [a]
    na, nb, ne, no = len(a_list), len(bs), len(extras), len(outs)
    assert na in (1, nb)

    if mode == "tn":
        a_spec = pl.BlockSpec((bk, bm), lambda i, j, kk: (kk, i))
    else:
        a_spec = pl.BlockSpec((bm, bk), lambda i, j, kk: (i, kk))

    def b_spec(lead, off):
        if mode == "nt":
            blk, idx = (bn, bk), (lambda i, j, kk: (j + off, kk))
        else:
            blk, idx = (bk, bn), (lambda i, j, kk: (kk, j + off))
        if lead is None:
            return pl.BlockSpec(blk, idx)
        return pl.BlockSpec((None,) + blk, lambda i, j, kk: (lead,) + idx(i, j, kk))

    def e_spec(kind, off):
        if kind == "mn":
            return pl.BlockSpec((bm, bn), lambda i, j, kk: (i, j + off))
        return pl.BlockSpec((1, bn), lambda i, j, kk: (0, j + off))

    nf = len(after)
    in_specs = [a_spec] * na + [b_spec(l, o) for _, l, o in bs] + [e_spec(kd, o) for _, kd, o in extras]
    in_specs += [ANY_SPEC] * nf
    out_specs = [pl.BlockSpec((bm, bn), lambda i, j, kk: (i, j)) for _ in outs]
    out_shape = [jax.ShapeDtypeStruct((m, n), dt) for (dt,) in outs]
    dims = (_DOT_DIMS[mode], ((), ()))

    def body(*refs):
        a_refs = refs[:na]
        b_refs = refs[na:na + nb]
        e_refs = refs[na + nb:na + nb + ne]
        o_refs = refs[na + nb + ne + nf:na + nb + ne + nf + no]
        acc_refs = refs[na + nb + ne + nf + no:]
        avs = [a_ref[...].astype(BF16) for a_ref in a_refs]
        prods = [lax.dot_general(avs[bi % na], b_ref[...].astype(BF16), dims, preferred_element_type=F32)
                 for bi, b_ref in enumerate(b_refs)]

        def finish(accs):
            res = epilogue(accs, [e_ref[...] for e_ref in e_refs])
            for o_ref, r in zip(o_refs, res):
                o_ref[...] = r.astype(o_ref.dtype)

        if nk == 1:
            finish(prods)
        else:
            kk = pl.program_id(2)

            @pl.when(kk == 0)
            def _():
                for acc_ref, p in zip(acc_refs, prods):
                    acc_ref[...] = p

            @pl.when(kk > 0)
            def _():
                for acc_ref, p in zip(acc_refs, prods):
                    acc_ref[...] += p

            @pl.when(kk == nk - 1)
            def _():
                finish([acc_ref[...] for acc_ref in acc_refs])

    scratch = [] if nk == 1 else [pltpu.VMEM((bm, bn), F32) for _ in bs]
    res = pl.pallas_call(
        body,
        name=name,
        grid=(m // bm, n // bn, nk),
        in_specs=in_specs,
        out_specs=out_specs,
        out_shape=out_shape,
        scratch_shapes=scratch,
        compiler_params=pltpu.CompilerParams(dimension_semantics=("parallel", "parallel", "arbitrary")),
    )(*a_list, *[b for b, _, _ in bs], *[e for e, _, _ in extras], *after)
    return res


def _rms_fwd(name, x, gains, after=()):
    s, d = x.shape
    ng = gains.shape[0]
    t = _row_tile(s, 256)
    nf = len(after)

    def body(x_ref, g_ref, *refs):
        o_refs = refs[nf:]
        xv = x_ref[...]
        r = lax.rsqrt(jnp.mean(xv * xv, axis=-1, keepdims=True) + RMS_EPS)
        xh = xv * r
        for gi, o_ref in enumerate(o_refs):
            o_ref[...] = (xh * g_ref[gi:gi + 1, :]).astype(o_ref.dtype)

    return pl.pallas_call(
        body,
        name=name,
        grid=(s // t,),
        in_specs=[pl.BlockSpec((t, d), lambda i: (i, 0)), pl.BlockSpec((ng, d), lambda i: (0, 0))] + [ANY_SPEC] * nf,
        out_specs=[pl.BlockSpec((t, d), lambda i: (i, 0)) for _ in range(ng)],
        out_shape=[jax.ShapeDtypeStruct((s, d), BF16) for _ in range(ng)],
        compiler_params=pltpu.CompilerParams(dimension_semantics=("parallel",)),
    )(x, gains, *after)


def _rms_bwd(name, x, gains, dns, dres):
    s, d = x.shape
    ng = gains.shape[0]
    t = _row_tile(s, 256)

    def body(x_ref, g_ref, dres_ref, *refs):
        dn_refs = refs[:ng]
        dx_ref, dxb_ref, dg_ref, cs_ref = refs[ng:]
        i = pl.program_id(0)
        xv = x_ref[...]
        r = lax.rsqrt(jnp.mean(xv * xv, axis=-1, keepdims=True) + RMS_EPS)
        xh = xv * r
        dx = dres_ref[...]
        dgs = []
        for gi in range(ng):
            dn = dn_refs[gi][...].astype(F32)
            dxh = dn * g_ref[gi:gi + 1, :]
            dgs.append(jnp.sum(dn * xh, axis=0, keepdims=True))
            dx = dx + r * (dxh - xh * jnp.mean(dxh * xh, axis=-1, keepdims=True))
        dx_ref[...] = dx
        dxb_ref[...] = dx.astype(BF16)
        dg = jnp.concatenate(dgs, axis=0) if ng > 1 else dgs[0]
        cs = jnp.sum(dx, axis=0, keepdims=True)

        @pl.when(i == 0)
        def _():
            dg_ref[...] = dg
            cs_ref[...] = cs

        @pl.when(i > 0)
        def _():
            dg_ref[...] += dg
            cs_ref[...] += cs

    row = pl.BlockSpec((t, d), lambda i: (i, 0))
    return pl.pallas_call(
        body,
        name=name,
        grid=(s // t,),
        in_specs=[row, pl.BlockSpec((ng, d), lambda i: (0, 0)), row] + [row] * ng,
        out_specs=[row, row, pl.BlockSpec((ng, d), lambda i: (0, 0)), pl.BlockSpec((1, d), lambda i: (0, 0))],
        out_shape=[
            jax.ShapeDtypeStruct((s, d), F32),
            jax.ShapeDtypeStruct((s, d), BF16),
            jax.ShapeDtypeStruct((ng, d), F32),
            jax.ShapeDtypeStruct((1, d), F32),
        ],
        compiler_params=pltpu.CompilerParams(dimension_semantics=("arbitrary",)),
    )(x, gains, dres, *dns)


def _final_loss(name, h, gain, target):
    s, d = h.shape
    t = _row_tile(s, 256)

    def body(h_ref, g_ref, t_ref, dh_ref, dhb_ref, dg_ref, ls_ref):
        i = pl.program_id(0)
        xv = h_ref[...]
        g = g_ref[...]
        r = lax.rsqrt(jnp.mean(xv * xv, axis=-1, keepdims=True) + RMS_EPS)
        xh = xv * r
        err = xh * g - t_ref[...]
        ls = jnp.sum(err * err, axis=0, keepdims=True) * (0.5 / d)
        dy = err * (1.0 / d)
        dxh = dy * g
        dg = jnp.sum(dy * xh, axis=0, keepdims=True)
        dx = r * (dxh - xh * jnp.mean(dxh * xh, axis=-1, keepdims=True))
        dh_ref[...] = dx
        dhb_ref[...] = dx.astype(BF16)

        @pl.when(i == 0)
        def _():
            dg_ref[...] = dg
            ls_ref[...] = ls

        @pl.when(i > 0)
        def _():
            dg_ref[...] += dg
            ls_ref[...] += ls

    row = pl.BlockSpec((t, d), lambda i: (i, 0))
    vec = pl.BlockSpec((1, d), lambda i: (0, 0))
    return pl.pallas_call(
        body,
        name=name,
        grid=(s // t,),
        in_specs=[row, vec, row],
        out_specs=[row, row, vec, vec],
        out_shape=[
            jax.ShapeDtypeStruct((s, d), F32),
            jax.ShapeDtypeStruct((s, d), BF16),
            jax.ShapeDtypeStruct((1, d), F32),
            jax.ShapeDtypeStruct((1, d), F32),
        ],
        compiler_params=pltpu.CompilerParams(dimension_semantics=("arbitrary",)),
    )(h, gain, target)


def _conv_tiles(s):
    t = _row_tile(s, 128)
    assert t % CONV_HALO == 0
    return t, t // CONV_HALO


def _dwconv_fwd(name, u, dw, dw_b, ln_g, ln_b):
    s, d = u.shape
    t, hb = _conv_tiles(s)
    w = dw.shape[0]
    lo = CONV_HALO - (w - 1)

    def body(cur_ref, prev_ref, dw_ref, dwb_ref, lg_ref, lb_ref, c_ref, sw_ref, cat_ref):
        i = pl.program_id(0)
        cat_ref[CONV_HALO:, :] = cur_ref[...]

        @pl.when(i == 0)
        def _():
            cat_ref[:CONV_HALO, :] = jnp.zeros((CONV_HALO, d), F32)

        @pl.when(i > 0)
        def _():
            cat_ref[:CONV_HALO, :] = prev_ref[...]

        acc = jnp.zeros((t, d), F32) + dwb_ref[...]
        for kk in range(w):
            acc = acc + dw_ref[kk:kk + 1, :] * cat_ref[lo + kk:lo + kk + t, :]
        c_ref[...] = acc
        mu = jnp.mean(acc, axis=-1, keepdims=True)
        cc = acc - mu
        var = jnp.mean(cc * cc, axis=-1, keepdims=True)
        ln = cc * lax.rsqrt(var + LN_EPS) * lg_ref[...] + lb_ref[...]
        sw_ref[...] = (ln * _sigmoid(ln)).astype(BF16)

    row = pl.BlockSpec((t, d), lambda i: (i, 0))
    prev = pl.BlockSpec((CONV_HALO, d), lambda i: (jnp.maximum(i * hb - 1, 0), 0))
    vec = pl.BlockSpec((1, d), lambda i: (0, 0))
    return pl.pallas_call(
        body,
        name=name,
        grid=(s // t,),
        in_specs=[row, prev, pl.BlockSpec((w, d), lambda i: (0, 0)), vec, vec, vec],
        out_specs=[row, row],
        out_shape=[jax.ShapeDtypeStruct((s, d), F32), jax.ShapeDtypeStruct((s, d), BF16)],
        scratch_shapes=[pltpu.VMEM((CONV_HALO + t, d), F32)],
        compiler_params=pltpu.CompilerParams(dimension_semantics=("parallel",)),
    )(u, u, dw, dw_b, ln_g, ln_b)


def _conv_ln_bwd(name, c, dsw, ln_g, ln_b, after=()):
    s, d = c.shape
    t = _row_tile(s, 256)
    nf = len(after)

    def body(c_ref, dsw_ref, lg_ref, lb_ref, *refs):
        dc_ref, sums_ref = refs[nf:]
        i = pl.program_id(0)
        cv = c_ref[...]
        g = lg_ref[...]
        mu = jnp.mean(cv, axis=-1, keepdims=True)
        cc = cv - mu
        rstd = lax.rsqrt(jnp.mean(cc * cc, axis=-1, keepdims=True) + LN_EPS)
        ch = cc * rstd
        ln = ch * g + lb_ref[...]
        sg = _sigmoid(ln)
        dln = dsw_ref[...] * (sg * (1.0 + ln * (1.0 - sg)))
        dch = dln * g
        dc = rstd * (dch - jnp.mean(dch, axis=-1, keepdims=True) - ch * jnp.mean(dch * ch, axis=-1, keepdims=True))
        dc_ref[...] = dc
        sums = jnp.concatenate(
            [
                jnp.sum(dln * ch, axis=0, keepdims=True),
                jnp.sum(dln, axis=0, keepdims=True),
                jnp.sum(dc, axis=0, keepdims=True),
                jnp.zeros((1, d), F32),
            ],
            axis=0,
        )

        @pl.when(i == 0)
        def _():
            sums_ref[...] = sums

        @pl.when(i > 0)
        def _():
            sums_ref[...] += sums

    row = pl.BlockSpec((t, d), lambda i: (i, 0))
    vec = pl.BlockSpec((1, d), lambda i: (0, 0))
    return pl.pallas_call(
        body,
        name=name,
        grid=(s // t,),
        in_specs=[row, row, vec, vec] + [ANY_SPEC] * nf,
        out_specs=[row, pl.BlockSpec((4, d), lambda i: (0, 0))],
        out_shape=[jax.ShapeDtypeStruct((s, d), F32), jax.ShapeDtypeStruct((4, d), F32)],
        compiler_params=pltpu.CompilerParams(dimension_semantics=("arbitrary",)),
    )(c, dsw, ln_g, ln_b, *after)


def _conv_dw_bwd(name, dc, u, a, gt, dw):
    s, d = dc.shape
    t, hb = _conv_tiles(s)
    w = dw.shape[0]
    lo = CONV_HALO - (w - 1)
    nt = s // t

    def body(dc_ref, dcn_ref, u_ref, up_ref, a_ref, gt_ref, dw_ref, dpre_ref, ddw_ref, db_ref, dcat_ref, ucat_ref):
        i = pl.program_id(0)
        dcat_ref[:t, :] = dc_ref[...]
        ucat_ref[CONV_HALO:, :] = u_ref[...]

        @pl.when(i == nt - 1)
        def _():
            dcat_ref[t:, :] = jnp.zeros((CONV_HALO, d), F32)

        @pl.when(i < nt - 1)
        def _():
            dcat_ref[t:, :] = dcn_ref[...]

        @pl.when(i == 0)
        def _():
            ucat_ref[:CONV_HALO, :] = jnp.zeros((CONV_HALO, d), F32)

        @pl.when(i > 0)
        def _():
            ucat_ref[:CONV_HALO, :] = up_ref[...]

        dcv = dc_ref[...]
        du = jnp.zeros((t, d), F32)
        rows = []
        for kk in range(w):
            sh = w - 1 - kk
            du = du + dw_ref[kk:kk + 1, :] * dcat_ref[sh:sh + t, :]
            rows.append(jnp.sum(dcv * ucat_ref[lo + kk:lo + kk + t, :], axis=0, keepdims=True))
        ddw = jnp.concatenate(rows, axis=0)
        av = a_ref[...].astype(F32)
        sg = _sigmoid(gt_ref[...].astype(F32))
        da = du * sg
        dgt = du * av * sg * (1.0 - sg)
        dpre_ref[:, :d] = da.astype(BF16)
        dpre_ref[:, d:] = dgt.astype(BF16)
        db = jnp.concatenate([jnp.sum(da, axis=0, keepdims=True), jnp.sum(dgt, axis=0, keepdims=True)], axis=1)

        @pl.when(i == 0)
        def _():
            ddw_ref[...] = ddw
            db_ref[...] = db

        @pl.when(i > 0)
        def _():
            ddw_ref[...] += ddw
            db_ref[...] += db

    row = pl.BlockSpec((t, d), lambda i: (i, 0))
    nxt = pl.BlockSpec((CONV_HALO, d), lambda i: (jnp.minimum((i + 1) * hb, s // CONV_HALO - 1), 0))
    prev = pl.BlockSpec((CONV_HALO, d), lambda i: (jnp.maximum(i * hb - 1, 0), 0))
    return pl.pallas_call(
        body,
        name=name,
        grid=(nt,),
        in_specs=[row, nxt, row, prev, row, row, pl.BlockSpec((w, d), lambda i: (0, 0))],
        out_specs=[
            pl.BlockSpec((t, 2 * d), lambda i: (i, 0)),
            pl.BlockSpec((w, d), lambda i: (0, 0)),
            pl.BlockSpec((1, 2 * d), lambda i: (0, 0)),
        ],
        out_shape=[
            jax.ShapeDtypeStruct((s, 2 * d), BF16),
            jax.ShapeDtypeStruct((w, d), F32),
            jax.ShapeDtypeStruct((1, 2 * d), F32),
        ],
        scratch_shapes=[pltpu.VMEM((t + CONV_HALO, d), F32), pltpu.VMEM((CONV_HALO + t, d), F32)],
        compiler_params=pltpu.CompilerParams(dimension_semantics=("arbitrary",)),
    )(dc, dc, u, u, a, gt, dw)


def _alibi_slopes(n_heads):
    h = jnp.arange(1, n_heads + 1, dtype=F32)
    return jnp.exp2(-8.0 * h / n_heads)


def _band_masks(bq):
    qi = lax.broadcasted_iota(jnp.int32, (bq, bq), 0)
    kj = lax.broadcasted_iota(jnp.int32, (bq, bq), 1)
    return qi - kj, qi - kj + bq


ATTN_GROUP = 8


def _attn_fwd(name, q, k, v, slopes, bq):
    s, dm = q.shape
    nh = dm // HEAD_DIM
    nt = s // bq
    nbr = len(BRANCHES)
    scale = HEAD_DIM ** -0.5
    nt_dims = (((1,), (1,)), ((), ()))

    def body(sl_ref, q_ref, k_ref, v_ref, o_ref, ob_ref, l_ref, tmp, qr, kr, va, orm, lrm, onat, lnat, sbuf, mbuf):
        slope = sl_ref[pl.program_id(0)]
        jc, jp = _band_masks(bq)
        va[:, HEAD_DIM:] = jnp.ones((s, HEAD_DIM), BF16)
        for bi, (win, dil) in enumerate(BRANCHES):
            ll = s // dil
            nblk = ll // bq
            if dil == 1:
                sq, sk = q_ref, k_ref
                va[:, :HEAD_DIM] = v_ref[...]
                d_o, d_l = onat.at[bi], lnat.at[bi]
            else:
                for src, dst, wide in ((q_ref, qr, False), (k_ref, kr, False), (v_ref, va, True)):
                    tmp[...] = src[...].astype(F32)
                    for r in range(dil):
                        part = tmp[pl.ds(r, ll, stride=dil), :].astype(BF16)
                        if wide:
                            dst[r * ll:(r + 1) * ll, :HEAD_DIM] = part
                        else:
                            dst[r * ll:(r + 1) * ll, :] = part
                sq, sk = qr, kr
                d_o, d_l = orm, lrm
            bias_c = jnp.where(jc >= 0, jc.astype(F32) * (slope * dil), 1e30)
            bias_p = jnp.where(jp <= bq, jp.astype(F32) * (slope * dil), 1e30)

            def group(gi, carry):
                rows = []
                for g in range(ATTN_GROUP):
                    ti = gi * ATTN_GROUP + g
                    row = pl.ds(pl.multiple_of(ti * bq, bq), bq)
                    prow = pl.ds(pl.multiple_of(jnp.maximum(ti - 1, 0) * bq, bq), bq)
                    rows.append((row, prow))
                    qh = sq[row, :]
                    sc = lax.dot_general(qh, sk[row, :], nt_dims, preferred_element_type=F32) * scale - bias_c
                    sp = lax.dot_general(qh, sk[prow, :], nt_dims, preferred_element_type=F32) * scale - bias_p
                    sp = jnp.where(lax.rem(ti, nblk) > 0, sp, -1e30)
                    sbuf[g, :, :bq] = sc
                    sbuf[g, :, bq:] = sp
                    mbuf[g] = jnp.maximum(jnp.max(sc, axis=-1, keepdims=True), jnp.max(sp, axis=-1, keepdims=True))
                for g, (row, prow) in enumerate(rows):
                    mx = mbuf[g]
                    p = jnp.exp(sbuf[g] - mx).astype(BF16)
                    ov = jnp.dot(p[:, :bq], va[row, :], preferred_element_type=F32)
                    ov = ov + jnp.dot(p[:, bq:], va[prow, :], preferred_element_type=F32)
                    den = ov[:, HEAD_DIM:]
                    d_o[row, :] = ov[:, :HEAD_DIM] / den
                    d_l[row, :] = mx + jnp.log(den)
                return carry

            lax.fori_loop(0, nt // ATTN_GROUP, group, 0)
            if dil > 1:
                for r in range(dil):
                    onat[bi, pl.ds(r, ll, stride=dil), :] = orm[r * ll:(r + 1) * ll, :]
                    lnat[bi, pl.ds(r, ll, stride=dil), :] = lrm[r * ll:(r + 1) * ll, :]

        def merge(ti, carry):
            rows = pl.ds(pl.multiple_of(ti * bq, bq), bq)
            ls = [lnat[bi, rows, :] for bi in range(nbr)]
            mx = functools.reduce(jnp.maximum, ls)
            es = [jnp.exp(l - mx) for l in ls]
            tot = functools.reduce(jnp.add, es)
            inv = 1.0 / tot
            o = functools.reduce(jnp.add, [e * inv * onat[bi, rows, :] for bi, e in enumerate(es)])
            o_ref[rows, :] = o
            ob_ref[rows, :] = o.astype(BF16)
            l_ref[rows, :] = mx + jnp.log(tot)
            return carry

        lax.fori_loop(0, nt, merge, 0)

    head = pl.BlockSpec((s, HEAD_DIM), lambda h: (0, h))
    return pl.pallas_call(
        body,
        name=name,
        grid=(nh,),
        in_specs=[pl.BlockSpec(memory_space=pltpu.SMEM), head, head, head],
        out_specs=[head, head, head],
        out_shape=[jax.ShapeDtypeStruct((s, dm), F32), jax.ShapeDtypeStruct((s, dm), BF16),
                   jax.ShapeDtypeStruct((s, dm), F32)],
        scratch_shapes=[pltpu.VMEM((s, HEAD_DIM), F32)] + [pltpu.VMEM((s, HEAD_DIM), BF16)] * 2
        + [pltpu.VMEM((s, 2 * HEAD_DIM), BF16)] + [pltpu.VMEM((s, HEAD_DIM), F32)] * 2
        + [pltpu.VMEM((nbr, s, HEAD_DIM), F32)] * 2
        + [pltpu.VMEM((ATTN_GROUP, bq, 2 * bq), F32), pltpu.VMEM((ATTN_GROUP, bq, 1), F32)],
        compiler_params=pltpu.CompilerParams(dimension_semantics=("parallel",)),
    )(slopes, q, k, v)


def _attn_bwd(name, q, k, v, o, lse, do, slopes, bq):
    s, dm = q.shape
    nh = dm // HEAD_DIM
    nt = s // bq
    scale = HEAD_DIM ** -0.5
    nt_dims = (((1,), (1,)), ((), ()))
    tn_dims = (((0,), (0,)), ((), ()))

    def body(sl_ref, q_ref, k_ref, v_ref, o_ref, l_ref, do_ref, dq_ref, dk_ref, dv_ref,
             tmp, qr, kr, vr, dor, lr, dlr, dln, dqr, dkr, dvr, aq, ak, av, pbuf, dsbuf):
        slope = sl_ref[pl.program_id(0)]
        jc, jp = _band_masks(bq)

        def delta(ti, carry):
            rows = pl.ds(pl.multiple_of(ti * bq, bq), bq)
            dl = jnp.sum(do_ref[rows, :].astype(F32) * o_ref[rows, :], axis=-1, keepdims=True)
            dln[rows, :] = jnp.broadcast_to(dl, (bq, HEAD_DIM))
            return carry

        lax.fori_loop(0, nt, delta, 0)

        for bi, (win, dil) in enumerate(BRANCHES):
            ll = s // dil
            nblk = ll // bq
            if dil == 1:
                sq, sk, sv, sdo, sl, sdl = q_ref, k_ref, v_ref, do_ref, l_ref, dln
                gq, gk, gv = aq, ak, av
            else:
                for src, dst in ((q_ref, qr), (k_ref, kr), (v_ref, vr), (do_ref, dor)):
                    tmp[...] = src[...].astype(F32)
                    for r in range(dil):
                        dst[r * ll:(r + 1) * ll, :] = tmp[pl.ds(r, ll, stride=dil), :].astype(BF16)
                for r in range(dil):
                    lr[r * ll:(r + 1) * ll, :] = l_ref[pl.ds(r, ll, stride=dil), :]
                    dlr[r * ll:(r + 1) * ll, :] = dln[pl.ds(r, ll, stride=dil), :]
                sq, sk, sv, sdo, sl, sdl = qr, kr, vr, dor, lr, dlr
                gq, gk, gv = dqr, dkr, dvr
            bias_c = jnp.where(jc >= 0, jc.astype(F32) * (slope * dil), 1e30)
            bias_p = jnp.where(jp <= bq, jp.astype(F32) * (slope * dil), 1e30)

            def group(gi, carry):
                rows = []
                for g in range(ATTN_GROUP):
                    ti = gi * ATTN_GROUP + g
                    row = pl.ds(pl.multiple_of(ti * bq, bq), bq)
                    prow = pl.ds(pl.multiple_of(jnp.maximum(ti - 1, 0) * bq, bq), bq)
                    rows.append((row, prow))
                    has_prev = lax.rem(ti, nblk) > 0
                    qh, doh = sq[row, :], sdo[row, :]
                    lc = sl[row, :][:, :1]
                    dl = sdl[row, :][:, :1]
                    for half, kv_rows, bias in ((0, row, bias_c), (1, prow, bias_p)):
                        sc = lax.dot_general(qh, sk[kv_rows, :], nt_dims, preferred_element_type=F32) * scale - bias
                        if half:
                            sc = jnp.where(has_prev, sc, -1e30)
                        p = jnp.exp(sc - lc)
                        dp = lax.dot_general(doh, sv[kv_rows, :], nt_dims, preferred_element_type=F32)
                        pbuf[g, :, half * bq:(half + 1) * bq] = p.astype(BF16)
                        dsbuf[g, :, half * bq:(half + 1) * bq] = (p * (dp - dl) * scale).astype(BF16)
                carry_k = carry_v = None
                for g, (row, prow) in enumerate(rows):
                    qh, doh = sq[row, :], sdo[row, :]
                    ds_c, ds_p = dsbuf[g, :, :bq], dsbuf[g, :, bq:]
                    p_c, p_p = pbuf[g, :, :bq], pbuf[g, :, bq:]
                    dq = jnp.dot(ds_c, sk[row, :], preferred_element_type=F32)
                    dq = dq + jnp.dot(ds_p, sk[prow, :], preferred_element_type=F32)
                    gq[row, :] = dq
                    dk_p = lax.dot_general(ds_p, qh, tn_dims, preferred_element_type=F32)
                    dv_p = lax.dot_general(p_p, doh, tn_dims, preferred_element_type=F32)
                    if g == 0:
                        @pl.when(gi > 0)
                        def _():
                            gk[prow, :] += dk_p
                            gv[prow, :] += dv_p
                    else:
                        gk[rows[g - 1][0], :] = carry_k + dk_p
                        gv[rows[g - 1][0], :] = carry_v + dv_p
                    carry_k = lax.dot_general(ds_c, qh, tn_dims, preferred_element_type=F32)
                    carry_v = lax.dot_general(p_c, doh, tn_dims, preferred_element_type=F32)
                gk[rows[-1][0], :] = carry_k
                gv[rows[-1][0], :] = carry_v
                return carry

            lax.fori_loop(0, nt // ATTN_GROUP, group, 0)
            if dil > 1:
                for acc, rm in ((aq, dqr), (ak, dkr), (av, dvr)):
                    for r in range(dil):
                        acc[pl.ds(r, ll, stride=dil), :] += rm[r * ll:(r + 1) * ll, :]

        dq_ref[...] = aq[...].astype(BF16)
        dk_ref[...] = ak[...].astype(BF16)
        dv_ref[...] = av[...].astype(BF16)

    head = pl.BlockSpec((s, HEAD_DIM), lambda h: (0, h))
    f32buf = pltpu.VMEM((s, HEAD_DIM), F32)
    b16buf = pltpu.VMEM((s, HEAD_DIM), BF16)
    return pl.pallas_call(
        body,
        name=name,
        grid=(nh,),
        in_specs=[pl.BlockSpec(memory_space=pltpu.SMEM)] + [head] * 6,
        out_specs=[head] * 3,
        out_shape=[jax.ShapeDtypeStruct((s, dm), BF16)] * 3,
        scratch_shapes=[f32buf] + [b16buf] * 4 + [f32buf] * 9 + [pltpu.VMEM((ATTN_GROUP, bq, 2 * bq), BF16)] * 2,
        compiler_params=pltpu.CompilerParams(dimension_semantics=("parallel",)),
    )(slopes, q, k, v, o, lse, do)


def _ep_id(accs, ex):
    return [accs[0]]


def _ep_all(accs, ex):
    return list(accs)


def _ep_sum(accs, ex):
    return [accs[0] + accs[1]]


def _ep_add(accs, ex):
    return [accs[0] + ex[0].astype(F32)]


def _ep_bias_res(accs, ex):
    return [accs[0] + ex[0] + ex[1]]


def _ep_glu(accs, ex):
    a = accs[0] + ex[0]
    gt = accs[1] + ex[1]
    return [a * _sigmoid(gt), a, gt]


def _ep_swiglu(accs, ex):
    g, u = accs
    return [g, u, g * _sigmoid(g) * u]


def _ep_swiglu_bwd(accs, ex):
    dact = accs[0]
    g = ex[0].astype(F32)
    u = ex[1].astype(F32)
    sg = _sigmoid(g)
    return [dact * u * (sg * (1.0 + g * (1.0 - sg))), dact * g * sg]


def _ffn_fwd(tag, h, gain, wg, wu, wd):
    s, d = h.shape
    f = wg.shape[-1]
    (n,) = _rms_fwd(f"{tag}_norm", h, gain)
    g, u, act = _mm(f"{tag}_gate_up", "nn", n, [(wg, None, 0), (wu, None, 0)], _ep_swiglu,
                    [(BF16,), (BF16,), (BF16,)], m=s, n=f, k=d)
    (out,) = _mm(f"{tag}_down", "nn", act, [(wd, None, 0)], _ep_add, [(F32,)], m=s, n=d, k=f,
                 extras=[(h, "mn", 0)], bn=256)
    return out, (n, g, u, act)


def _ffn_bwd(tag, h_in, gain, wg, wu, wd, saved, dh, dhb, after, emit):
    s, d = h_in.shape
    f = wg.shape[-1]
    n, g, u, act = saved
    dg, du = _mm(f"{tag}_bwd_dact", "nt", dhb, [(wd, None, 0)], _ep_swiglu_bwd, [(BF16,), (BF16,)],
                 m=s, n=f, k=d, extras=[(g, "mn", 0), (u, "mn", 0)], after=after)
    (dwd,) = _mm(f"{tag}_bwd_dwd", "tn", act, [(dhb, None, 0)], _ep_id, [(BF16,)], m=f, n=d, k=s,
                 bm=f // 4)
    dwg, dwu = _mm(f"{tag}_bwd_dwgu", "tn", n, [(dg, None, 0), (du, None, 0)], _ep_all, [(BF16,), (BF16,)],
                   m=d, n=f, k=s)
    pin = emit(dict(gate=dwg, up=dwu, down=dwd))
    (dn,) = _mm(f"{tag}_bwd_dn", "nt", [dg, du], [(wg, None, 0), (wu, None, 0)], _ep_sum, [(F32,)],
                m=s, n=d, k=f, bm=512, bn=256, after=pin)
    dx, dxb, dgain, cs = _rms_bwd(f"{tag}_bwd_norm", h_in, gain, [dn], dh)
    return dx, dxb, dgain, cs


def _local_step(x, target, w, fetch, emit, after=()):
    s, d = x.shape
    nh = d // HEAD_DIM
    bq = BRANCHES[0][0] // BRANCHES[0][1]
    assert all(win // dil == bq for win, dil in BRANCHES)
    assert BRANCHES[0][1] == 1 and all(dil > 1 for _, dil in BRANCHES[1:])
    slopes = _alibi_slopes(nh)
    nd = d // 512 if d >= 512 else 1
    bn = d // nd

    (n1,) = _rms_fwd("a_norm", x, w["a_norm_g"], after=after)
    w_conv1 = fetch("conv1", n1)["conv_w1"]
    glu, a, gt = _mm("conv_pw1_glu", "nn", n1, [(w_conv1, None, 0), (w_conv1, None, nd)], _ep_glu,
                     [(F32,), (BF16,), (BF16,)], m=s, n=d, k=d, bn=bn,
                     extras=[(w["conv_b1"], "n", 0), (w["conv_b1"], "n", nd)])
    c, sw = _dwconv_fwd("conv_dw_ln", glu, w["conv_dw"], w["conv_dw_b"], w["conv_ln_g"], w["conv_ln_b"])
    w_conv2 = fetch("conv2", sw)["conv_w2"]
    (h1,) = _mm("conv_pw2", "nn", sw, [(w_conv2, None, 0)], _ep_bias_res, [(F32,)], m=s, n=d, k=d,
                extras=[(w["conv_b2"], "n", 0), (x, "mn", 0)])
    wf0 = fetch("ffn0", h1)
    h2, ffn0 = _ffn_fwd("ffn0", h1, w["ffn_norm_g"][0:1], wf0["gate"], wf0["up"], wf0["down"])
    wa = fetch("attn", h2)
    kvn, qn = _rms_fwd("kvq_norm", h2, jnp.concatenate([w["kv_norm_g"], w["b_norm_g"]], axis=0))
    k, v = _mm("kv_proj", "nn", kvn, [(wa["w_k"], None, 0), (wa["w_v"], None, 0)], _ep_all, [(BF16,), (BF16,)],
               m=s, n=d, k=d)
    (q,) = _mm("q_proj", "nn", qn, [(wa["w_q"], None, 0)], _ep_id, [(BF16,)], m=s, n=d, k=d)
    att, attb, lse = _attn_fwd("attn_fwd", q, k, v, slopes, bq)
    (h3,) = _mm("o_proj", "nn", attb, [(wa["w_o"], None, 0)], _ep_add, [(F32,)], m=s, n=d, k=d,
                extras=[(h2, "mn", 0)])
    wf1 = fetch("ffn1", h3)
    h4, ffn1 = _ffn_fwd("ffn1", h3, w["ffn_norm_g"][1:2], wf1["gate"], wf1["up"], wf1["down"])
    dh4, dh4b, d_final_g, loss_cols = _final_loss("final_loss", h4, w["final_norm_g"], target)

    g = {}
    ga = {}
    dh3, dh3b, dgain1, _ = _ffn_bwd("ffn1", h3, w["ffn_norm_g"][1:2], wf1["gate"], wf1["up"], wf1["down"], ffn1,
                                    dh4, dh4b, (), functools.partial(emit, "ffn1"))
    (datt,) = _mm("o_proj_bwd_dx", "nt", dh3b, [(wa["w_o"], None, 0)], _ep_id, [(BF16,)], m=s, n=d, k=d)
    (ga["w_o"],) = _mm("o_proj_bwd_dw", "tn", attb, [(dh3b, None, 0)], _ep_id, [(BF16,)], m=d, n=d, k=s)
    dq, dk, dv = _attn_bwd("attn_bwd", q, k, v, att, lse, datt, slopes, bq)
    (ga["w_q"],) = _mm("q_proj_bwd_dw", "tn", qn, [(dq, None, 0)], _ep_id, [(BF16,)], m=d, n=d, k=s)
    ga["w_k"], ga["w_v"] = _mm("kv_proj_bwd_dw", "tn", kvn, [(dk, None, 0), (dv, None, 0)], _ep_all,
                               [(BF16,), (BF16,)], m=d, n=d, k=s)
    pin = emit("attn", ga)
    (dqn,) = _mm("q_proj_bwd_dx", "nt", dq, [(wa["w_q"], None, 0)], _ep_id, [(F32,)], m=s, n=d, k=d, after=pin)
    (dkvn,) = _mm("kv_proj_bwd_dx", "nt", [dk, dv], [(wa["w_k"], None, 0), (wa["w_v"], None, 0)], _ep_sum, [(F32,)],
                  m=s, n=d, k=d)
    dh2, dh2b, dg_kvq, _ = _rms_bwd("kvq_norm_bwd", h2, jnp.concatenate([w["kv_norm_g"], w["b_norm_g"]], axis=0),
                                    [dkvn, dqn], dh3)
    dh1, dh1b, dgain0, cs_h1 = _ffn_bwd("ffn0", h1, w["ffn_norm_g"][0:1], wf0["gate"], wf0["up"], wf0["down"], ffn0,
                                        dh2, dh2b, (), functools.partial(emit, "ffn0"))
    (dsw,) = _mm("conv_pw2_bwd_dx", "nt", dh1b, [(w_conv2, None, 0)], _ep_id, [(F32,)], m=s, n=d, k=d)
    (dw2,) = _mm("conv_pw2_bwd_dw", "tn", sw, [(dh1b, None, 0)], _ep_id, [(BF16,)], m=d, n=d, k=s)
    pin = emit("conv2", dict(conv_w2=dw2))
    dc, ln_sums = _conv_ln_bwd("conv_ln_bwd", c, dsw, w["conv_ln_g"], w["conv_ln_b"], after=pin)
    dpre, ddw, db1 = _conv_dw_bwd("conv_dw_bwd", dc, glu, a, gt, w["conv_dw"])
    (dw1,) = _mm("conv_pw1_bwd_dw", "tn", n1, [(dpre, None, 0)], _ep_id, [(BF16,)], m=d, n=2 * d, k=s)
    pin = emit("conv1", dict(conv_w1=dw1))
    (dn1,) = _mm("conv_pw1_bwd_dx", "nt", dpre, [(w_conv1, None, 0)], _ep_id, [(F32,)], m=s, n=d, k=2 * d,
                 after=pin)
    dx, _, d_a_norm, _ = _rms_bwd("a_norm_bwd", x, w["a_norm_g"], [dn1], dh1)

    g.update(
        a_norm_g=d_a_norm, conv_b1=db1, conv_dw=ddw, conv_dw_b=ln_sums[2:3], conv_ln_g=ln_sums[0:1],
        conv_ln_b=ln_sums[1:2], conv_b2=cs_h1, kv_norm_g=dg_kvq[0:1], b_norm_g=dg_kvq[1:2],
        ffn_norm_g=jnp.concatenate([dgain0, dgain1], axis=0), final_norm_g=d_final_g,
    )
    return loss_cols, dx, g


HBM_SPEC = pl.BlockSpec(memory_space=pltpu.HBM)


def _mesh_place():
    x, y, c = lax.axis_index("x"), lax.axis_index("y"), lax.axis_index("c")
    chips = [(1 - x, y), (x, 1 - y), (1 - x, 1 - y)]
    return x, y, c, chips


def _shard_view(ref, kind, s, half=None):
    rows, cols = ref.shape
    if kind == "col":
        cw = cols // N_CHIPS
        if half is None:
            return ref.at[pl.ds(0, rows), pl.ds(s * cw, cw)]
        return ref.at[pl.ds(half * (rows // 2), rows // 2), pl.ds(s * cw, cw)]
    r = rows // N_CHIPS
    if half is None:
        return ref.at[pl.ds(s * r, r), pl.ds(0, cols)]
    return ref.at[pl.ds(s * r + half * (r // 2), r // 2), pl.ds(0, cols)]


def _full_shape(shard, kind):
    r, cw = shard.shape[-2:]
    return (r, cw * N_CHIPS) if kind == "col" else (r * N_CHIPS, cw)


def _gather_weights(shards, kinds, after=()):
    nt = len(shards)
    nf = len(after)
    fulls = [jax.ShapeDtypeStruct(_full_shape(sh, kind), sh.dtype) for sh, kind in zip(shards, kinds)]

    def body(*refs):
        src = refs[:nt]
        dst = refs[nt + nf:2 * nt + nf]
        send, recv, fsend, frecv, local = refs[2 * nt + nf:]
        x, y, c, chips = _mesh_place()
        s = 2 * x + y
        sib = (x, y, 1 - c)

        def half_of_shard(t):
            r, cw = src[t].shape
            return src[t].at[pl.ds(c * (r // 2), r // 2), pl.ds(0, cw)]

        locals_ = [pltpu.make_async_copy(src[t], _shard_view(dst[t], kinds[t], s), local.at[t]) for t in range(nt)]
        for cp in locals_:
            cp.start()
        sends = []
        for t in range(nt):
            for j, chip in enumerate(chips):
                cp = pltpu.make_async_remote_copy(
                    src_ref=half_of_shard(t), dst_ref=_shard_view(dst[t], kinds[t], s, c),
                    send_sem=send.at[t, j], recv_sem=recv.at[t, j], device_id=(*chip, c), device_id_type=MESH)
                cp.start()
                sends.append(cp)
        for t in range(nt):
            for j, (px, py) in enumerate(chips):
                landed = _shard_view(dst[t], kinds[t], 2 * px + py, c)
                pltpu.make_async_remote_copy(
                    src_ref=half_of_shard(t), dst_ref=landed, send_sem=send.at[t, j], recv_sem=recv.at[t, j],
                    device_id=(px, py, c), device_id_type=MESH).wait_recv()
                cp = pltpu.make_async_remote_copy(
                    src_ref=landed, dst_ref=landed, send_sem=fsend.at[t, j], recv_sem=frecv.at[t, j],
                    device_id=sib, device_id_type=MESH)
                cp.start()
                sends.append(cp)
        for t in range(nt):
            for j, (px, py) in enumerate(chips):
                other = _shard_view(dst[t], kinds[t], 2 * px + py, 1 - c)
                pltpu.make_async_remote_copy(
                    src_ref=other, dst_ref=other, send_sem=fsend.at[t, j], recv_sem=frecv.at[t, j],
                    device_id=sib, device_id_type=MESH).wait_recv()
        for cp in sends:
            cp.wait_send()
        for cp in locals_:
            cp.wait()

    return pl.pallas_call(
        body,
        name="gather_weights",
        in_specs=[HBM_SPEC] * nt + [ANY_SPEC] * nf,
        out_specs=[HBM_SPEC] * nt,
        out_shape=fulls,
        scratch_shapes=[pltpu.SemaphoreType.DMA((nt, 3))] * 4 + [pltpu.SemaphoreType.DMA((nt,))],
    )(*shards, *after)


def _row_blocks(rows, want=512):
    nb = 1
    while rows // nb > want or rows % nb or (rows // nb) % 16:
        nb += 1
        if nb > rows:
            return rows, 1
    return rows // nb, nb


def _place_own(name, src, kind, land, ids, into_slot, lead=0):
    if into_slot:
        _, r, cw = land.shape
    else:
        r, cw = src.shape[-2:]
    tr, nb = _row_blocks(r)
    if not into_slot:
        if src.ndim == 3:
            src_spec = pl.BlockSpec((None, tr, cw), lambda i, ids_ref: (lead, i, 0))
        else:
            src_spec = pl.BlockSpec((tr, cw), lambda i, ids_ref: (i, 0))
        if kind == "col":
            dst_spec = pl.BlockSpec((tr, cw), lambda i, ids_ref: (i, ids_ref[0]))
        else:
            dst_spec = pl.BlockSpec((tr, cw), lambda i, ids_ref: (ids_ref[0] * nb + i, 0))
    else:
        if kind == "col":
            src_spec = pl.BlockSpec((tr, cw), lambda i, ids_ref: (i, ids_ref[0]))
        elif kind == "row":
            src_spec = pl.BlockSpec((tr, cw), lambda i, ids_ref: (ids_ref[0] * nb + i, 0))
        else:
            src_spec = pl.BlockSpec((tr, cw), lambda i, ids_ref: (i, 0))
        dst_spec = pl.BlockSpec((None, tr, cw), lambda i, ids_ref: (ids_ref[1], i, 0))

    def body(ids_ref, s_ref, o_ref):
        o_ref[...] = s_ref[...].astype(o_ref.dtype)

    return pl.pallas_call(
        body,
        name=name,
        grid_spec=pltpu.PrefetchScalarGridSpec(num_scalar_prefetch=1, grid=(nb,), in_specs=[src_spec],
                                               out_specs=dst_spec),
        out_shape=land,
        compiler_params=pltpu.CompilerParams(dimension_semantics=("parallel",)),
    )(ids, src)


SEM_SPEC = pl.BlockSpec(memory_space=pltpu.SEMAPHORE)
SIDE_EFFECT = pltpu.SideEffectType.DATAFLOW_SIDE_EFFECTING


def _copies_start(name, srcs, lands, copies, n_sems, after):
    ns, nl, nf = len(srcs), len(lands), len(after)

    def body(*refs):
        src, land = refs[:ns], refs[ns:ns + nl]
        send, recv = refs[ns + nl + nf], refs[ns + nl + nf + 1]
        pin = refs[-1]
        for cp in copies(src, land, send, recv, _mesh_place()):
            cp.start()
        pin[...] = jnp.zeros_like(pin)

    arrs = list(srcs) + list(lands)
    res = pl.pallas_call(
        body,
        name=name,
        in_specs=[HBM_SPEC] * (ns + nl) + [ANY_SPEC] * nf,
        out_specs=[SEM_SPEC, SEM_SPEC] + [HBM_SPEC] * (ns + nl) + [pl.BlockSpec(memory_space=pltpu.VMEM)],
        out_shape=[pltpu.SemaphoreType.DMA((n_sems,)), pltpu.SemaphoreType.DMA((n_sems,))]
        + [pltpu.HBM(a.shape, a.dtype) for a in arrs] + [jax.ShapeDtypeStruct((8, 128), F32)],
        input_output_aliases={i: 2 + i for i in range(ns + nl)},
        compiler_params=pltpu.CompilerParams(has_side_effects=SIDE_EFFECT),
    )(*[pltpu.with_memory_space_constraint(a, pltpu.HBM) for a in arrs], *after)
    return res[0], res[1], list(res[2:2 + ns]), list(res[2 + ns:2 + ns + nl]), res[-1]


def _copies_wait(name, send, recv, srcs, lands, copies, after):
    ns, nl, nf = len(srcs), len(lands), len(after)

    def body(*refs):
        src, land = refs[:ns], refs[ns:ns + nl]
        send_sems, recv_sems = refs[ns + nl], refs[ns + nl + 1]
        cps = copies(src, land, send_sems, recv_sems, _mesh_place())
        for cp in cps:
            cp.wait_send()
        for cp in cps:
            cp.wait_recv()

    arrs = list(srcs) + list(lands)
    res = pl.pallas_call(
        body,
        name=name,
        in_specs=[HBM_SPEC] * (ns + nl) + [SEM_SPEC, SEM_SPEC] + [ANY_SPEC] * nf,
        out_specs=[HBM_SPEC] * (ns + nl),
        out_shape=[pltpu.HBM(a.shape, a.dtype) for a in arrs],
        input_output_aliases={i: i for i in range(ns + nl)},
        compiler_params=pltpu.CompilerParams(has_side_effects=SIDE_EFFECT),
    )(*arrs, send, recv, *after)
    return list(res[ns:])


def _gather_copies(kinds, halves):
    def copies(src, land, send, recv, place):
        x, y, c, chips = place
        s = 2 * x + y
        mine = [_shard_view(land[t], kinds[t], s, c if halves else None) for t in range(len(kinds))]
        return [
            pltpu.make_async_remote_copy(
                src_ref=mine[t], dst_ref=mine[t], send_sem=send.at[3 * t + j], recv_sem=recv.at[3 * t + j],
                device_id=(px, py, c), device_id_type=MESH)
            for t in range(len(kinds)) for j, (px, py) in enumerate(chips)
        ]

    return copies


def _gather_pass_on(name, lands, kinds):
    nt = len(lands)

    def body(*refs):
        buf = refs[nt:2 * nt]
        send, recv = refs[2 * nt:]
        x, y, c, chips = _mesh_place()
        sib = (x, y, 1 - c)
        sends = []
        for t in range(nt):
            for j, (px, py) in enumerate(chips):
                mine = _shard_view(buf[t], kinds[t], 2 * px + py, c)
                cp = pltpu.make_async_remote_copy(src_ref=mine, dst_ref=mine, send_sem=send.at[t, j],
                                                  recv_sem=recv.at[t, j], device_id=sib, device_id_type=MESH)
                cp.start()
                sends.append(cp)
        for t in range(nt):
            for j, (px, py) in enumerate(chips):
                theirs = _shard_view(buf[t], kinds[t], 2 * px + py, 1 - c)
                pltpu.make_async_remote_copy(src_ref=theirs, dst_ref=theirs, send_sem=send.at[t, j],
                                             recv_sem=recv.at[t, j], device_id=sib, device_id_type=MESH).wait_recv()
        for cp in sends:
            cp.wait_send()

    return pl.pallas_call(
        body,
        name=name,
        in_specs=[HBM_SPEC] * nt,
        out_specs=[HBM_SPEC] * nt,
        out_shape=[jax.ShapeDtypeStruct(a.shape, a.dtype) for a in lands],
        input_output_aliases={i: i for i in range(nt)},
        scratch_shapes=[pltpu.SemaphoreType.DMA((nt, 3))] * 2,
    )(*lands)


def _grad_part(ref, kind, s):
    return ref if kind == "all" else _shard_view(ref, kind, s)


def _grad_copies(kinds):
    def peers(place):
        x, y, c, chips = place
        return [(x, y, 1 - c)] + [(px, py, c) for px, py in chips]

    def copies(src, land, send, recv, place):
        x, y, c, chips = place
        me = 4 * x + 2 * y + c
        return [
            pltpu.make_async_remote_copy(
                src_ref=_grad_part(src[t], kinds[t], 2 * px + py), dst_ref=land[t].at[me],
                send_sem=send.at[GRAD_PEERS * t + k], recv_sem=recv.at[GRAD_PEERS * t + k], device_id=(px, py, pc),
                device_id_type=MESH)
            for t in range(len(kinds)) for k, (px, py, pc) in enumerate(peers(place))
        ]

    return copies


def _pass_copies(n):
    def copies(src, land, send, recv, place):
        x, y, c, chips = place
        return [
            pltpu.make_async_remote_copy(
                src_ref=land[t].at[4 * px + 2 * py + c], dst_ref=land[t].at[4 * px + 2 * py + c],
                send_sem=send.at[3 * t + j], recv_sem=recv.at[3 * t + j], device_id=(x, y, 1 - c),
                device_id_type=MESH)
            for t in range(n) for j, (px, py) in enumerate(chips)
        ]

    return copies


GRAD_PEERS = 4


def _land_shape(grad, kind):
    rows, cols = grad.shape
    if kind == "col":
        return (N_DEV, rows, cols // N_CHIPS)
    if kind == "row":
        return (N_DEV, rows // N_CHIPS, cols)
    return (N_DEV, rows, cols)


def _adamw_reduce(name, contrib, w, m, v, layer=None, prev=None):
    rows, cols = w.shape[-2:]
    t = 128 if rows % 128 == 0 else rows
    c1 = 1.0 - ADAM_B1 ** ADAM_STEP
    c2 = 1.0 - ADAM_B2 ** ADAM_STEP

    n_prev = 0 if prev is None else 4

    def body(c_ref, w_ref, m_ref, v_ref, *refs):
        g_ref, d_ref, nm_ref, nv_ref = refs[n_prev:]
        g = c_ref[0].astype(F32)
        for q in range(1, N_DEV):
            g = g + c_ref[q].astype(F32)
        nm = ADAM_B1 * m_ref[...] + (1.0 - ADAM_B1) * g
        nv = ADAM_B2 * v_ref[...] + (1.0 - ADAM_B2) * (g * g)
        g_ref[...] = g
        nm_ref[...] = nm
        nv_ref[...] = nv
        d_ref[...] = -ADAM_LR * ((nm / c1) / (jnp.sqrt(nv / c2) + ADAM_EPS) + ADAM_WD * w_ref[...])

    if layer is None:
        blk = pl.BlockSpec((t, cols), lambda i: (i, 0))
    else:
        blk = pl.BlockSpec((None, t, cols), lambda i: (layer, i, 0))
    return pl.pallas_call(
        body,
        name=name,
        grid=(rows // t,),
        in_specs=[pl.BlockSpec((N_DEV, t, cols), lambda i: (0, i, 0)), blk, blk, blk] + [ANY_SPEC] * n_prev,
        out_specs=[blk] * 4,
        out_shape=[jax.ShapeDtypeStruct(w.shape, F32)] * 4,
        input_output_aliases={4 + i: i for i in range(n_prev)},
        compiler_params=pltpu.CompilerParams(dimension_semantics=("parallel",)),
    )(contrib, w, m, v, *(prev or ()))


WEIGHT_NAMES = ("a_norm_g", "conv_w1", "conv_b1", "conv_dw", "conv_dw_b", "conv_ln_g", "conv_ln_b", "conv_w2",
                "conv_b2", "kv_norm_g", "w_k", "w_v", "b_norm_g", "w_q", "w_o", "ffn_norm_g", "ffn_w_gate",
                "ffn_w_up", "ffn_w_down", "final_norm_g")
GROUPS = {
    "conv2": (("conv_w2", "conv_w2", None, "row"),),
    "ffn0": (("gate", "ffn_w_gate", 0, "col"), ("up", "ffn_w_up", 0, "col"), ("down", "ffn_w_down", 0, "row")),
    "attn": (("w_k", "w_k", None, "row"), ("w_v", "w_v", None, "row"), ("w_q", "w_q", None, "row"),
             ("w_o", "w_o", None, "row")),
    "ffn1": (("gate", "ffn_w_gate", 1, "col"), ("up", "ffn_w_up", 1, "col"), ("down", "ffn_w_down", 1, "row")),
}
GROUPS["conv1"] = (("conv_w1", "conv_w1", None, "col"),)
FETCH_ORDER = ("conv1", "conv2", "ffn0", "attn", "ffn1")
HALVED = ("conv1", "conv2", "ffn0")
EMIT_ORDER = ("ffn1", "attn", "ffn0", "conv2", "conv1", "vec")
RETIRE_AT = {"attn": ("ffn1",), "ffn0": ("attn",), "conv1": ("ffn0", "conv2")}
PACKED = (("a_norm_g", 0, 1), ("conv_b1", 8, 2), ("conv_dw", 16, CONV_WIDTH), ("conv_dw_b", 48, 1),
          ("conv_ln_g", 56, 1), ("conv_ln_b", 64, 1), ("conv_b2", 72, 1))
PACK_ROWS = 80
WHOLE = (("kv_norm_g", 0, 1), ("b_norm_g", 1, 1), ("ffn_norm_g", 2, 2), ("final_norm_g", 4, 1))
WHOLE_ROWS = 8


def _pack_rows(parts, total, width):
    out, at = [], 0
    for arr, first in parts:
        if first > at:
            out.append(jnp.zeros((first - at, width), F32))
        rows8 = -(-arr.shape[0] // 8) * 8
        out.append(jnp.pad(arr, ((0, rows8 - arr.shape[0]), (0, 0))))
        at = first + rows8
    if total > at:
        out.append(jnp.zeros((total - at, width), F32))
    return jnp.concatenate(out, axis=0)


def kernel(x, a_norm_g, conv_w1, conv_b1, conv_dw, conv_dw_b, conv_ln_g, conv_ln_b, conv_w2, conv_b2, kv_norm_g, w_k, w_v, b_norm_g, w_q, w_o, ffn_norm_g, ffn_w_gate, ffn_w_up, ffn_w_down, final_norm_g, loss_target, m_a_norm_g, m_conv_w1, m_conv_b1, m_conv_dw, m_conv_dw_b, m_conv_ln_g, m_conv_ln_b, m_conv_w2, m_conv_b2, m_kv_norm_g, m_w_k, m_w_v, m_b_norm_g, m_w_q, m_w_o, m_ffn_norm_g, m_ffn_w_gate, m_ffn_w_up, m_ffn_w_down, m_final_norm_g, v_a_norm_g, v_conv_w1, v_conv_b1, v_conv_dw, v_conv_dw_b, v_conv_ln_g, v_conv_ln_b, v_conv_w2, v_conv_b2, v_kv_norm_g, v_w_k, v_w_v, v_b_norm_g, v_w_q, v_w_o, v_ffn_norm_g, v_ffn_w_gate, v_ffn_w_up, v_ffn_w_down, v_final_norm_g):
    args = locals()
    wts = {n: args[n] for n in WEIGHT_NAMES}
    mom = {n: args["m_" + n] for n in WEIGHT_NAMES}
    vel = {n: args["v_" + n] for n in WEIGHT_NAMES}
    s, d = x.shape[-2:]
    dq = d // N_CHIPS
    x2 = x.reshape(s, d)
    tgt = loss_target.reshape(s, d)

    def pack_shard(src):
        return _pack_rows([(src[n].reshape(-1, dq), first) for n, first, _ in PACKED], PACK_ROWS, dq)

    def pack_whole(src):
        return _pack_rows([(jnp.concatenate([src[n].reshape(-1, d) for n, _, _ in WHOLE], axis=0), 0)], WHOLE_ROWS, d)

    gathers = {}
    pins = []
    ids = jnp.stack([2 * lax.axis_index("x") + lax.axis_index("y"),
                     4 * lax.axis_index("x") + 2 * lax.axis_index("y") + lax.axis_index("c")]).astype(jnp.int32)

    def start_gather(grp, pins):
        kinds = [kind for _, _, _, kind in GROUPS[grp]]
        lands = []
        for key, n, layer, kind in GROUPS[grp]:
            shard = wts[n] if layer is not None else wts[n].reshape(wts[n].shape[-2:])
            full = jax.ShapeDtypeStruct(_full_shape(shard, kind), BF16)
            lands.append(_place_own(f"gather_place_{grp}_{key}", shard, kind, full, ids, False, layer))
        copies = _gather_copies(kinds, grp in HALVED)
        send, recv, _, lands, pin = _copies_start("gather_start_" + grp, [], lands, copies, 3 * len(lands), pins)
        gathers[grp] = (send, recv, [], lands, copies, kinds)
        return [pin]

    (packed_full,) = _gather_weights([pack_shard(wts)], ["row"])
    w = {}
    pins = [packed_full]
    for grp in FETCH_ORDER:
        pins = start_gather(grp, pins)

    def fetch(grp, after):
        send, recv, srcs, lands, copies, kinds = gathers[grp]
        fulls = _copies_wait("gather_wait_" + grp, send, recv, srcs, lands, copies, [after])
        if grp in HALVED:
            fulls = _gather_pass_on("gather_pass_" + grp, fulls, kinds)
        return {key: full for (key, _, _, _), full in zip(GROUPS[grp], fulls)}

    packed = packed_full.reshape(N_CHIPS, PACK_ROWS, dq)
    for n, first, rows in PACKED:
        part = packed[:, first:first + rows, :]
        if n == "conv_dw":
            w[n] = part.transpose(1, 0, 2).reshape(rows, d)
        else:
            w[n] = part.reshape(1, N_CHIPS * rows * dq)
    for n, _, rows in WHOLE:
        w[n] = wts[n].reshape(rows, d)

    exchanges = {}
    passing = {}

    def retire(tag, after):
        send, recv, srcs, lands, copies = exchanges[tag]
        lands = _copies_wait("grads_wait_" + tag, send, recv, srcs, lands, copies, after)
        copies = _pass_copies(len(lands))
        send, recv, _, lands, pin = _copies_start("grads_pass_" + tag, [], lands, copies, 3 * len(lands), ())
        passing[tag] = (send, recv, lands, copies)
        return pin

    def emit_grads(tag, grads, kinds):
        pins = [retire(old, [grads[0]]) for old in RETIRE_AT.get(tag, ())]
        copies = _grad_copies(kinds)
        lands = [_place_own(f"grads_place_{tag}_{t}", gr, kd, jax.ShapeDtypeStruct(_land_shape(gr, kd), gr.dtype),
                            ids, True) for t, (gr, kd) in enumerate(zip(grads, kinds))]
        send, recv, srcs, lands, pin = _copies_start("grads_start_" + tag, grads, lands, copies,
                                                     GRAD_PEERS * len(grads), pins)
        exchanges[tag] = (send, recv, srcs, lands, copies)
        return [pin]

    def emit(grp, grads):
        return emit_grads(grp, [grads[key] for key, _, _, _ in GROUPS[grp]], [kind for _, _, _, kind in GROUPS[grp]])

    loss_cols, dx, g = _local_step(x2, tgt, w, fetch, emit, after=pins)
    loss = lax.psum(jnp.sum(loss_cols), ("x", "y", "c"))

    gp = []
    for n, first, rows in PACKED:
        if n == "conv_dw":
            part = g[n].reshape(rows, N_CHIPS, dq).transpose(1, 0, 2)
        else:
            part = g[n].reshape(N_CHIPS, rows, dq)
        gp.append((part, first))
    g_packed = jnp.concatenate(
        [_pack_rows([(p[ci], first) for p, first in gp], PACK_ROWS, dq) for ci in range(N_CHIPS)], axis=0)
    emit_grads("vec", [g_packed, pack_whole(g)], ["row", "all"])
    for tag in EMIT_ORDER:
        if tag not in passing:
            retire(tag, [dx])
    contribs = {}
    for tag in EMIT_ORDER:
        send, recv, lands, copies = passing[tag]
        contribs[tag] = _copies_wait("grads_passed_" + tag, send, recv, [], lands, copies, [dx])

    res = {}

    def adamw(n, contrib, layer=None, prev=None):
        if layer is None:
            shape = wts[n].shape
            r2 = shape[-2:]
            outs = _adamw_reduce("adamw_" + n, contrib, wts[n].reshape(r2), mom[n].reshape(r2), vel[n].reshape(r2))
            return [o.reshape(shape) for o in outs]
        return _adamw_reduce(f"adamw_{n}_{layer}", contrib, wts[n], mom[n], vel[n], layer, prev)

    for grp in ("attn", "conv2", "conv1"):
        for (key, n, _, _), contrib in zip(GROUPS[grp], contribs[grp]):
            res[n] = adamw(n, contrib)
    for (key, n, _, _), c0, c1 in zip(GROUPS["ffn0"], contribs["ffn0"], contribs["ffn1"]):
        res[n] = adamw(n, c1, 1, adamw(n, c0, 0))
    outs = _adamw_reduce("adamw_packed", contribs["vec"][0], pack_shard(wts), pack_shard(mom), pack_shard(vel))
    for n, first, rows in PACKED:
        res[n] = [o[first:first + rows].reshape(wts[n].shape) for o in outs]
    outs = _adamw_reduce("adamw_whole", contribs["vec"][1], pack_whole(wts), pack_whole(mom), pack_whole(vel))
    for n, first, rows in WHOLE:
        res[n] = [o[first:first + rows].reshape(wts[n].shape) for o in outs]

    out = [loss, dx.reshape(x.shape)]
    for which in range(4):
        out += [res[n][which] for n in WEIGHT_NAMES]
    return tuple(out)
```

```python
import functools
import math

import jax
import jax.numpy as jnp
from jax import lax
from jax.experimental import pallas as pl
from jax.experimental.pallas import tpu as pltpu

F32 = jnp.float32
BF16 = jnp.bfloat16

HEAD_DIM = 128
BRANCHES = ((128, 1), (512, 4), (2048, 16))
CONV_WIDTH = 31
CONV_HALO = 32
RMS_EPS = 1e-6
LN_EPS = 1e-5
ADAM_LR = 0.001
ADAM_B1 = 0.9
ADAM_B2 = 0.999
ADAM_EPS = 1e-08
ADAM_WD = 0.01
ADAM_STEP = 10
N_CHIPS = 4
N_DEV = 8
MESH = pl.DeviceIdType.MESH


def _sigmoid(x):
    return 1.0 / (1.0 + jnp.exp(-x))


def _row_tile(rows, want):
    t = min(rows, want)
    assert rows % t == 0, (rows, want)
    return t


_DOT_DIMS = {"nn": ((1,), (0,)), "nt": ((1,), (1,)), "tn": ((0,), (0,))}


ANY_SPEC = pl.BlockSpec(memory_space=pl.ANY)


def _mm(name, mode, a, bs, epilogue, outs, *, m, n, k, extras=(), bm=1024, bn=512, bk=None, after=()):
    bm, bn = min(bm, m), min(bn, n)
    bk = k if bk is None else min(bk, k)
    assert m % bm == 0 and n % bn == 0 and k % bk == 0, (name, m, n, k, bm, bn, bk)
    nk = k // bk
    a_list = list(a) if isinstance(a, (list, tuple)) else [a]
    na, nb, ne, no = len(a_list), len(bs), len(extras), len(outs)
    assert na in (1, nb)

    if mode == "tn":
        a_spec = pl.BlockSpec((bk, bm), lambda i, j, kk: (kk, i))
    else:
        a_spec = pl.BlockSpec((bm, bk), lambda i, j, kk: (i, kk))

    def b_spec(lead, off):
        if mode == "nt":
            blk, idx = (bn, bk), (lambda i, j, kk: (j + off, kk))
        else:
            blk, idx = (bk, bn), (lambda i, j, kk: (kk, j + off))
        if lead is None:
            return pl.BlockSpec(blk, idx)
        return pl.BlockSpec((None,) + blk, lambda i, j, kk: (lead,) + idx(i, j, kk))

    def e_spec(kind, off):
        if kind == "mn":
            return pl.BlockSpec((bm, bn), lambda i, j, kk: (i, j + off))
        return pl.BlockSpec((1, bn), lambda i, j, kk: (0, j + off))

    nf = len(after)
    in_specs = [a_spec] * na + [b_spec(l, o) for _, l, o in bs] + [e_spec(kd, o) for _, kd, o in extras]
    in_specs += [ANY_SPEC] * nf
    out_specs = [pl.BlockSpec((bm, bn), lambda i, j, kk: (i, j)) for _ in outs]
    out_shape = [jax.ShapeDtypeStruct((m, n), dt) for (dt,) in outs]
    dims = (_DOT_DIMS[mode], ((), ()))

    def body(*refs):
        a_refs = refs[:na]
        b_refs = refs[na:na + nb]
        e_refs = refs[na + nb:na + nb + ne]
        o_refs = refs[na + nb + ne + nf:na + nb + ne + nf + no]
        acc_refs = refs[na + nb + ne + nf + no:]
        avs = [a_ref[...].astype(BF16) for a_ref in a_refs]
        prods = [lax.dot_general(avs[bi % na], b_ref[...].astype(BF16), dims, preferred_element_type=F32)
                 for bi, b_ref in enumerate(b_refs)]

        def finish(accs):
            res = epilogue(accs, [e_ref[...] for e_ref in e_refs])
            for o_ref, r in zip(o_refs, res):
                o_ref[...] = r.astype(o_ref.dtype)

        if nk == 1:
            finish(prods)
        else:
            kk = pl.program_id(2)

            @pl.when(kk == 0)
            def _():
                for acc_ref, p in zip(acc_refs, prods):
                    acc_ref[...] = p

            @pl.when(kk > 0)
            def _():
                for acc_ref, p in zip(acc_refs, prods):
                    acc_ref[...] += p

            @pl.when(kk == nk - 1)
            def _():
                finish([acc_ref[...] for acc_ref in acc_refs])

    scratch = [] if nk == 1 else [pltpu.VMEM((bm, bn), F32) for _ in bs]
    res = pl.pallas_call(
        body,
        name=name,
        grid=(m // bm, n // bn, nk),
        in_specs=in_specs,
        out_specs=out_specs,
        out_shape=out_shape,
        scratch_shapes=scratch,
        compiler_params=pltpu.CompilerParams(dimension_semantics=("parallel", "parallel", "arbitrary")),
    )(*a_list, *[b for b, _, _ in bs], *[e for e, _, _ in extras], *after)
    return res


def _rms_fwd(name, x, gains, after=()):
    s, d = x.shape
    ng = gains.shape[0]
    t = _row_tile(s, 256)
    nf = len(after)

    def body(x_ref, g_ref, *refs):
        o_refs = refs[nf:]
        xv = x_ref[...]
        r = lax.rsqrt(jnp.mean(xv * xv, axis=-1, keepdims=True) + RMS_EPS)
        xh = xv * r
        for gi, o_ref in enumerate(o_refs):
            o_ref[...] = (xh * g_ref[gi:gi + 1, :]).astype(o_ref.dtype)

    return pl.pallas_call(
        body,
        name=name,
        grid=(s // t,),
        in_specs=[pl.BlockSpec((t, d), lambda i: (i, 0)), pl.BlockSpec((ng, d), lambda i: (0, 0))] + [ANY_SPEC] * nf,
        out_specs=[pl.BlockSpec((t, d), lambda i: (i, 0)) for _ in range(ng)],
        out_shape=[jax.ShapeDtypeStruct((s, d), BF16) for _ in range(ng)],
        compiler_params=pltpu.CompilerParams(dimension_semantics=("parallel",)),
    )(x, gains, *after)


def _rms_bwd(name, x, gains, dns, dres):
    s, d = x.shape
    ng = gains.shape[0]
    t = _row_tile(s, 256)

    def body(x_ref, g_ref, dres_ref, *refs):
        dn_refs = refs[:ng]
        dx_ref, dxb_ref, dg_ref, cs_ref = refs[ng:]
        i = pl.program_id(0)
        xv = x_ref[...]
        r = lax.rsqrt(jnp.mean(xv * xv, axis=-1, keepdims=True) + RMS_EPS)
        xh = xv * r
        dx = dres_ref[...]
        dgs = []
        for gi in range(ng):
            dn = dn_refs[gi][...].astype(F32)
            dxh = dn * g_ref[gi:gi + 1, :]
            dgs.append(jnp.sum(dn * xh, axis=0, keepdims=True))
            dx = dx + r * (dxh - xh * jnp.mean(dxh * xh, axis=-1, keepdims=True))
        dx_ref[...] = dx
        dxb_ref[...] = dx.astype(BF16)
        dg = jnp.concatenate(dgs, axis=0) if ng > 1 else dgs[0]
        cs = jnp.sum(dx, axis=0, keepdims=True)

        @pl.when(i == 0)
        def _():
            dg_ref[...] = dg
            cs_ref[...] = cs

        @pl.when(i > 0)
        def _():
            dg_ref[...] += dg
            cs_ref[...] += cs

    row = pl.BlockSpec((t, d), lambda i: (i, 0))
    return pl.pallas_call(
        body,
        name=name,
        grid=(s // t,),
        in_specs=[row, pl.BlockSpec((ng, d), lambda i: (0, 0)), row] + [row] * ng,
        out_specs=[row, row, pl.BlockSpec((ng, d), lambda i: (0, 0)), pl.BlockSpec((1, d), lambda i: (0, 0))],
        out_shape=[
            jax.ShapeDtypeStruct((s, d), F32),
            jax.ShapeDtypeStruct((s, d), BF16),
            jax.ShapeDtypeStruct((ng, d), F32),
            jax.ShapeDtypeStruct((1, d), F32),
        ],
        compiler_params=pltpu.CompilerParams(dimension_semantics=("arbitrary",)),
    )(x, gains, dres, *dns)


def _final_loss(name, h, gain, target):
    s, d = h.shape
    t = _row_tile(s, 256)

    def body(h_ref, g_ref, t_ref, dh_ref, dhb_ref, dg_ref, ls_ref):
        i = pl.program_id(0)
        xv = h_ref[...]
        g = g_ref[...]
        r = lax.rsqrt(jnp.mean(xv * xv, axis=-1, keepdims=True) + RMS_EPS)
        xh = xv * r
        err = xh * g - t_ref[...]
        ls = jnp.sum(err * err, axis=0, keepdims=True) * (0.5 / d)
        dy = err * (1.0 / d)
        dxh = dy * g
        dg = jnp.sum(dy * xh, axis=0, keepdims=True)
        dx = r * (dxh - xh * jnp.mean(dxh * xh, axis=-1, keepdims=True))
        dh_ref[...] = dx
        dhb_ref[...] = dx.astype(BF16)

        @pl.when(i == 0)
        def _():
            dg_ref[...] = dg
            ls_ref[...] = ls

        @pl.when(i > 0)
        def _():
            dg_ref[...] += dg
            ls_ref[...] += ls

    row = pl.BlockSpec((t, d), lambda i: (i, 0))
    vec = pl.BlockSpec((1, d), lambda i: (0, 0))
    return pl.pallas_call(
        body,
        name=name,
        grid=(s // t,),
        in_specs=[row, vec, row],
        out_specs=[row, row, vec, vec],
        out_shape=[
            jax.ShapeDtypeStruct((s, d), F32),
            jax.ShapeDtypeStruct((s, d), BF16),
            jax.ShapeDtypeStruct((1, d), F32),
            jax.ShapeDtypeStruct((1, d), F32),
        ],
        compiler_params=pltpu.CompilerParams(dimension_semantics=("arbitrary",)),
    )(h, gain, target)


SUBLANES = 8
CONV_LANES = 512
CONV_ROWS = 32
NORM_ROWS = 16


def _conv_tiles(s):
    t = _row_tile(s, 128)
    assert t % CONV_HALO == 0 and t % CONV_ROWS == 0
    return t, t // CONV_HALO


def _shifted_copies(dst, src, lanes, rows):
    for m in range(SUBLANES):
        n = rows if m == 0 else rows - SUBLANES
        dst[m, :n, :] = src[m:m + n, lanes]


def _shifted(copies, offset, n):
    m = offset % SUBLANES
    return copies[m, offset - m:offset - m + n, :]


def _dwconv_fwd(name, u, dw, dw_b, ln_g, ln_b):
    s, d = u.shape
    t, hb = _conv_tiles(s)
    w = dw.shape[0]
    lo = CONV_HALO - (w - 1)

    lw = min(CONV_LANES, d)

    def body(cur_ref, prev_ref, dw_ref, dwb_ref, lg_ref, lb_ref, c_ref, sw_ref, cat_ref, sh_ref):
        i = pl.program_id(0)
        cat_ref[CONV_HALO:, :] = cur_ref[...]

        @pl.when(i == 0)
        def _():
            cat_ref[:CONV_HALO, :] = jnp.zeros((CONV_HALO, d), F32)

        @pl.when(i > 0)
        def _():
            cat_ref[:CONV_HALO, :] = prev_ref[...]

        for lc in range(d // lw):
            lanes = slice(lc * lw, (lc + 1) * lw)
            _shifted_copies(sh_ref, cat_ref, lanes, t + CONV_HALO)
            for rc in range(t // CONV_ROWS):
                acc = jnp.broadcast_to(dwb_ref[:, lanes], (CONV_ROWS, lw))
                for kk in range(w):
                    acc = acc + dw_ref[kk:kk + 1, lanes] * _shifted(sh_ref, lo + kk + rc * CONV_ROWS, CONV_ROWS)
                c_ref[rc * CONV_ROWS:(rc + 1) * CONV_ROWS, lanes] = acc

        def norm_rows(ri, carry):
            rows = pl.ds(pl.multiple_of(ri * NORM_ROWS, NORM_ROWS), NORM_ROWS)
            cv = c_ref[rows, :]
            cc = cv - jnp.mean(cv, axis=-1, keepdims=True)
            var = jnp.mean(cc * cc, axis=-1, keepdims=True)
            ln = cc * lax.rsqrt(var + LN_EPS) * lg_ref[...] + lb_ref[...]
            sw_ref[rows, :] = (ln * _sigmoid(ln)).astype(BF16)
            return carry

        lax.fori_loop(0, t // NORM_ROWS, norm_rows, 0)

    row = pl.BlockSpec((t, d), lambda i: (i, 0))
    prev = pl.BlockSpec((CONV_HALO, d), lambda i: (jnp.maximum(i * hb - 1, 0), 0))
    vec = pl.BlockSpec((1, d), lambda i: (0, 0))
    return pl.pallas_call(
        body,
        name=name,
        grid=(s // t,),
        in_specs=[row, prev, pl.BlockSpec((w, d), lambda i: (0, 0)), vec, vec, vec],
        out_specs=[row, row],
        out_shape=[jax.ShapeDtypeStruct((s, d), F32), jax.ShapeDtypeStruct((s, d), BF16)],
        scratch_shapes=[pltpu.VMEM((CONV_HALO + t, d), F32), pltpu.VMEM((SUBLANES, CONV_HALO + t, lw), F32)],
        compiler_params=pltpu.CompilerParams(dimension_semantics=("parallel",)),
    )(u, u, dw, dw_b, ln_g, ln_b)


def _conv_ln_bwd(name, c, dsw, ln_g, ln_b, after=()):
    s, d = c.shape
    t = _row_tile(s, 256)
    nf = len(after)

    def body(c_ref, dsw_ref, lg_ref, lb_ref, *refs):
        dc_ref, sums_ref = refs[nf:]
        i = pl.program_id(0)
        cv = c_ref[...]
        g = lg_ref[...]
        mu = jnp.mean(cv, axis=-1, keepdims=True)
        cc = cv - mu
        rstd = lax.rsqrt(jnp.mean(cc * cc, axis=-1, keepdims=True) + LN_EPS)
        ch = cc * rstd
        ln = ch * g + lb_ref[...]
        sg = _sigmoid(ln)
        dln = dsw_ref[...] * (sg * (1.0 + ln * (1.0 - sg)))
        dch = dln * g
        dc = rstd * (dch - jnp.mean(dch, axis=-1, keepdims=True) - ch * jnp.mean(dch * ch, axis=-1, keepdims=True))
        dc_ref[...] = dc
        sums = jnp.concatenate(
            [
                jnp.sum(dln * ch, axis=0, keepdims=True),
                jnp.sum(dln, axis=0, keepdims=True),
                jnp.sum(dc, axis=0, keepdims=True),
                jnp.zeros((1, d), F32),
            ],
            axis=0,
        )

        @pl.when(i == 0)
        def _():
            sums_ref[...] = sums

        @pl.when(i > 0)
        def _():
            sums_ref[...] += sums

    row = pl.BlockSpec((t, d), lambda i: (i, 0))
    vec = pl.BlockSpec((1, d), lambda i: (0, 0))
    return pl.pallas_call(
        body,
        name=name,
        grid=(s // t,),
        in_specs=[row, row, vec, vec] + [ANY_SPEC] * nf,
        out_specs=[row, pl.BlockSpec((4, d), lambda i: (0, 0))],
        out_shape=[jax.ShapeDtypeStruct((s, d), F32), jax.ShapeDtypeStruct((4, d), F32)],
        compiler_params=pltpu.CompilerParams(dimension_semantics=("arbitrary",)),
    )(c, dsw, ln_g, ln_b, *after)


def _conv_dw_bwd(name, dc, u, a, gt, dw):
    s, d = dc.shape
    t, hb = _conv_tiles(s)
    w = dw.shape[0]
    lo = CONV_HALO - (w - 1)
    nt = s // t
    lw = min(CONV_LANES, d)

    def body(dc_ref, dcn_ref, u_ref, up_ref, a_ref, gt_ref, dw_ref, dpre_ref, ddw_ref, db_ref, dcat_ref, ucat_ref,
             dsh_ref, ush_ref, ddw_acc, db_acc):
        i = pl.program_id(0)
        dcat_ref[:t, :] = dc_ref[...]
        ucat_ref[CONV_HALO:, :] = u_ref[...]

        @pl.when(i == nt - 1)
        def _():
            dcat_ref[t:, :] = jnp.zeros((CONV_HALO, d), F32)

        @pl.when(i < nt - 1)
        def _():
            dcat_ref[t:, :] = dcn_ref[...]

        @pl.when(i == 0)
        def _():
            ucat_ref[:CONV_HALO, :] = jnp.zeros((CONV_HALO, d), F32)

        @pl.when(i > 0)
        def _():
            ucat_ref[:CONV_HALO, :] = up_ref[...]

        @pl.when(i == 0)
        def _():
            ddw_acc[...] = jnp.zeros(ddw_acc.shape, F32)
            db_acc[...] = jnp.zeros(db_acc.shape, F32)

        def fold(p):
            return functools.reduce(jnp.add, [p[j:j + SUBLANES] for j in range(0, CONV_ROWS, SUBLANES)])

        for lc in range(d // lw):
            lanes = slice(lc * lw, (lc + 1) * lw)
            gate_lanes = slice(d + lc * lw, d + (lc + 1) * lw)
            _shifted_copies(dsh_ref, dcat_ref, lanes, t + CONV_HALO)
            _shifted_copies(ush_ref, ucat_ref, lanes, t + CONV_HALO)
            for rc in range(t // CONV_ROWS):
                r0 = rc * CONV_ROWS
                rows = slice(r0, r0 + CONV_ROWS)
                dcv = dc_ref[rows, lanes]
                du = jnp.zeros((CONV_ROWS, lw), F32)
                for kk in range(w):
                    du = du + dw_ref[kk:kk + 1, lanes] * _shifted(dsh_ref, w - 1 - kk + r0, CONV_ROWS)
                    ddw_acc[kk, :, lanes] += fold(dcv * _shifted(ush_ref, lo + kk + r0, CONV_ROWS))
                av = a_ref[rows, lanes].astype(F32)
                sg = _sigmoid(gt_ref[rows, lanes].astype(F32))
                da = du * sg
                dgt = du * av * sg * (1.0 - sg)
                dpre_ref[rows, lanes] = da.astype(BF16)
                dpre_ref[rows, gate_lanes] = dgt.astype(BF16)
                db_acc[:, lanes] += fold(da)
                db_acc[:, gate_lanes] += fold(dgt)

        @pl.when(i == nt - 1)
        def _():
            ddw_ref[...] = jnp.sum(ddw_acc[...], axis=1)
            db_ref[...] = jnp.sum(db_acc[...], axis=0, keepdims=True)

    row = pl.BlockSpec((t, d), lambda i: (i, 0))
    nxt = pl.BlockSpec((CONV_HALO, d), lambda i: (jnp.minimum((i + 1) * hb, s // CONV_HALO - 1), 0))
    prev = pl.BlockSpec((CONV_HALO, d), lambda i: (jnp.maximum(i * hb - 1, 0), 0))
    return pl.pallas_call(
        body,
        name=name,
        grid=(nt,),
        in_specs=[row, nxt, row, prev, row, row, pl.BlockSpec((w, d), lambda i: (0, 0))],
        out_specs=[
            pl.BlockSpec((t, 2 * d), lambda i: (i, 0)),
            pl.BlockSpec((w, d), lambda i: (0, 0)),
            pl.BlockSpec((1, 2 * d), lambda i: (0, 0)),
        ],
        out_shape=[
            jax.ShapeDtypeStruct((s, 2 * d), BF16),
            jax.ShapeDtypeStruct((w, d), F32),
            jax.ShapeDtypeStruct((1, 2 * d), F32),
        ],
        scratch_shapes=[pltpu.VMEM((t + CONV_HALO, d), F32), pltpu.VMEM((CONV_HALO + t, d), F32)]
        + [pltpu.VMEM((SUBLANES, CONV_HALO + t, lw), F32)] * 2
        + [pltpu.VMEM((w, SUBLANES, d), F32), pltpu.VMEM((SUBLANES, 2 * d), F32)],
        compiler_params=pltpu.CompilerParams(dimension_semantics=("arbitrary",)),
    )(dc, dc, u, u, a, gt, dw)


def _alibi_slopes(n_heads):
    h = jnp.arange(1, n_heads + 1, dtype=F32)
    return jnp.exp2(-8.0 * h / n_heads)


def _band_masks(bq):
    qi = lax.broadcasted_iota(jnp.int32, (bq, bq), 0)
    kj = lax.broadcasted_iota(jnp.int32, (bq, bq), 1)
    return qi - kj, qi - kj + bq


ATTN_GROUP = 8


def _attn_fwd(name, q, k, v, slopes, bq):
    s, dm = q.shape
    nh = dm // HEAD_DIM
    nt = s // bq
    nbr = len(BRANCHES)
    scale = HEAD_DIM ** -0.5
    nt_dims = (((1,), (1,)), ((), ()))

    def body(sl_ref, q_ref, k_ref, v_ref, o_ref, ob_ref, l_ref, tmp, qr, kr, va, orm, lrm, onat, lnat, sbuf, mbuf):
        slope = sl_ref[pl.program_id(0)]
        jc, jp = _band_masks(bq)
        va[:, HEAD_DIM:] = jnp.ones((s, HEAD_DIM), BF16)
        for bi, (win, dil) in enumerate(BRANCHES):
            ll = s // dil
            nblk = ll // bq
            if dil == 1:
                sq, sk = q_ref, k_ref
                va[:, :HEAD_DIM] = v_ref[...]
                d_o, d_l = onat.at[bi], lnat.at[bi]
            else:
                for src, dst, wide in ((q_ref, qr, False), (k_ref, kr, False), (v_ref, va, True)):
                    tmp[...] = src[...].astype(F32)
                    for r in range(dil):
                        part = tmp[pl.ds(r, ll, stride=dil), :].astype(BF16)
                        if wide:
                            dst[r * ll:(r + 1) * ll, :HEAD_DIM] = part
                        else:
                            dst[r * ll:(r + 1) * ll, :] = part
                sq, sk = qr, kr
                d_o, d_l = orm, lrm
            bias_c = jnp.where(jc >= 0, jc.astype(F32) * (slope * dil), 1e30)
            bias_p = jnp.where(jp <= bq, jp.astype(F32) * (slope * dil), 1e30)

            def group(gi, carry):
                rows = []
                for g in range(ATTN_GROUP):
                    ti = gi * ATTN_GROUP + g
                    row = pl.ds(pl.multiple_of(ti * bq, bq), bq)
                    prow = pl.ds(pl.multiple_of(jnp.maximum(ti - 1, 0) * bq, bq), bq)
                    rows.append((row, prow))
                    qh = sq[row, :]
                    sc = lax.dot_general(qh, sk[row, :], nt_dims, preferred_element_type=F32) * scale - bias_c
                    sp = lax.dot_general(qh, sk[prow, :], nt_dims, preferred_element_type=F32) * scale - bias_p
                    sp = jnp.where(lax.rem(ti, nblk) > 0, sp, -1e30)
                    sbuf[g, :, :bq] = sc
                    sbuf[g, :, bq:] = sp
                    mbuf[g] = jnp.maximum(jnp.max(sc, axis=-1, keepdims=True), jnp.max(sp, axis=-1, keepdims=True))
                for g, (row, prow) in enumerate(rows):
                    mx = mbuf[g]
                    p = jnp.exp(sbuf[g] - mx).astype(BF16)
                    ov = jnp.dot(p[:, :bq], va[row, :], preferred_element_type=F32)
                    ov = ov + jnp.dot(p[:, bq:], va[prow, :], preferred_element_type=F32)
                    den = ov[:, HEAD_DIM:]
                    d_o[row, :] = ov[:, :HEAD_DIM] / den
                    d_l[row, :] = mx + jnp.log(den)
                return carry

            lax.fori_loop(0, nt // ATTN_GROUP, group, 0)
            if dil > 1:
                for r in range(dil):
                    onat[bi, pl.ds(r, ll, stride=dil), :] = orm[r * ll:(r + 1) * ll, :]
                    lnat[bi, pl.ds(r, ll, stride=dil), :] = lrm[r * ll:(r + 1) * ll, :]

        def merge(ti, carry):
            rows = pl.ds(pl.multiple_of(ti * bq, bq), bq)
            ls = [lnat[bi, rows, :] for bi in range(nbr)]
            mx = functools.reduce(jnp.maximum, ls)
            es = [jnp.exp(l - mx) for l in ls]
            tot = functools.reduce(jnp.add, es)
            inv = 1.0 / tot
            o = functools.reduce(jnp.add, [e * inv * onat[bi, rows, :] for bi, e in enumerate(es)])
            o_ref[rows, :] = o
            ob_ref[rows, :] = o.astype(BF16)
            l_ref[rows, :] = mx + jnp.log(tot)
            return carry

        lax.fori_loop(0, nt, merge, 0)

    head = pl.BlockSpec((s, HEAD_DIM), lambda h: (0, h))
    return pl.pallas_call(
        body,
        name=name,
        grid=(nh,),
        in_specs=[pl.BlockSpec(memory_space=pltpu.SMEM), head, head, head],
        out_specs=[head, head, head],
        out_shape=[jax.ShapeDtypeStruct((s, dm), F32), jax.ShapeDtypeStruct((s, dm), BF16),
                   jax.ShapeDtypeStruct((s, dm), F32)],
        scratch_shapes=[pltpu.VMEM((s, HEAD_DIM), F32)] + [pltpu.VMEM((s, HEAD_DIM), BF16)] * 2
        + [pltpu.VMEM((s, 2 * HEAD_DIM), BF16)] + [pltpu.VMEM((s, HEAD_DIM), F32)] * 2
        + [pltpu.VMEM((nbr, s, HEAD_DIM), F32)] * 2
        + [pltpu.VMEM((ATTN_GROUP, bq, 2 * bq), F32), pltpu.VMEM((ATTN_GROUP, bq, 1), F32)],
        compiler_params=pltpu.CompilerParams(dimension_semantics=("parallel",)),
    )(slopes, q, k, v)


def _attn_bwd(name, q, k, v, o, lse, do, slopes, bq):
    s, dm = q.shape
    nh = dm // HEAD_DIM
    nt = s // bq
    scale = HEAD_DIM ** -0.5
    nt_dims = (((1,), (1,)), ((), ()))
    tn_dims = (((0,), (0,)), ((), ()))

    def body(sl_ref, q_ref, k_ref, v_ref, o_ref, l_ref, do_ref, dq_ref, dk_ref, dv_ref,
             tmp, qr, kr, vr, dor, lr, dlr, dln, dqr, dkr, dvr, aq, ak, av, pbuf, dsbuf):
        slope = sl_ref[pl.program_id(0)]
        jc, jp = _band_masks(bq)

        def delta(ti, carry):
            rows = pl.ds(pl.multiple_of(ti * bq, bq), bq)
            dl = jnp.sum(do_ref[rows, :].astype(F32) * o_ref[rows, :], axis=-1, keepdims=True)
            dln[rows, :] = jnp.broadcast_to(dl, (bq, HEAD_DIM))
            return carry

        lax.fori_loop(0, nt, delta, 0)

        for bi, (win, dil) in enumerate(BRANCHES):
            ll = s // dil
            nblk = ll // bq
            if dil == 1:
                sq, sk, sv, sdo, sl, sdl = q_ref, k_ref, v_ref, do_ref, l_ref, dln
                gq, gk, gv = aq, ak, av
            else:
                for src, dst in ((q_ref, qr), (k_ref, kr), (v_ref, vr), (do_ref, dor)):
                    tmp[...] = src[...].astype(F32)
                    for r in range(dil):
                        dst[r * ll:(r + 1) * ll, :] = tmp[pl.ds(r, ll, stride=dil), :].astype(BF16)
                for r in range(dil):
                    lr[r * ll:(r + 1) * ll, :] = l_ref[pl.ds(r, ll, stride=dil), :]
                    dlr[r * ll:(r + 1) * ll, :] = dln[pl.ds(r, ll, stride=dil), :]
                sq, sk, sv, sdo, sl, sdl = qr, kr, vr, dor, lr, dlr
                gq, gk, gv = dqr, dkr, dvr
            bias_c = jnp.where(jc >= 0, jc.astype(F32) * (slope * dil), 1e30)
            bias_p = jnp.where(jp <= bq, jp.astype(F32) * (slope * dil), 1e30)

            def group(gi, carry):
                rows = []
                for g in range(ATTN_GROUP):
                    ti = gi * ATTN_GROUP + g
                    row = pl.ds(pl.multiple_of(ti * bq, bq), bq)
                    prow = pl.ds(pl.multiple_of(jnp.maximum(ti - 1, 0) * bq, bq), bq)
                    rows.append((row, prow))
                    has_prev = lax.rem(ti, nblk) > 0
                    qh, doh = sq[row, :], sdo[row, :]
                    lc = sl[row, :][:, :1]
                    dl = sdl[row, :][:, :1]
                    for half, kv_rows, bias in ((0, row, bias_c), (1, prow, bias_p)):
                        sc = lax.dot_general(qh, sk[kv_rows, :], nt_dims, preferred_element_type=F32) * scale - bias
                        if half:
                            sc = jnp.where(has_prev, sc, -1e30)
                        p = jnp.exp(sc - lc)
                        dp = lax.dot_general(doh, sv[kv_rows, :], nt_dims, preferred_element_type=F32)
                        pbuf[g, :, half * bq:(half + 1) * bq] = p.astype(BF16)
                        dsbuf[g, :, half * bq:(half + 1) * bq] = (p * (dp - dl) * scale).astype(BF16)
                carry_k = carry_v = None
                for g, (row, prow) in enumerate(rows):
                    qh, doh = sq[row, :], sdo[row, :]
                    ds_c, ds_p = dsbuf[g, :, :bq], dsbuf[g, :, bq:]
                    p_c, p_p = pbuf[g, :, :bq], pbuf[g, :, bq:]
                    dq = jnp.dot(ds_c, sk[row, :], preferred_element_type=F32)
                    dq = dq + jnp.dot(ds_p, sk[prow, :], preferred_element_type=F32)
                    gq[row, :] = dq
                    dk_p = lax.dot_general(ds_p, qh, tn_dims, preferred_element_type=F32)
                    dv_p = lax.dot_general(p_p, doh, tn_dims, preferred_element_type=F32)
                    if g == 0:
                        @pl.when(gi > 0)
                        def _():
                            gk[prow, :] += dk_p
                            gv[prow, :] += dv_p
                    else:
                        gk[rows[g - 1][0], :] = carry_k + dk_p
                        gv[rows[g - 1][0], :] = carry_v + dv_p
                    carry_k = lax.dot_general(ds_c, qh, tn_dims, preferred_element_type=F32)
                    carry_v = lax.dot_general(p_c, doh, tn_dims, preferred_element_type=F32)
                gk[rows[-1][0], :] = carry_k
                gv[rows[-1][0], :] = carry_v
                return carry

            lax.fori_loop(0, nt // ATTN_GROUP, group, 0)
            if dil > 1:
                for acc, rm in ((aq, dqr), (ak, dkr), (av, dvr)):
                    for r in range(dil):
                        acc[pl.ds(r, ll, stride=dil), :] += rm[r * ll:(r + 1) * ll, :]

        dq_ref[...] = aq[...].astype(BF16)
        dk_ref[...] = ak[...].astype(BF16)
        dv_ref[...] = av[...].astype(BF16)

    head = pl.BlockSpec((s, HEAD_DIM), lambda h: (0, h))
    f32buf = pltpu.VMEM((s, HEAD_DIM), F32)
    b16buf = pltpu.VMEM((s, HEAD_DIM), BF16)
    return pl.pallas_call(
        body,
        name=name,
        grid=(nh,),
        in_specs=[pl.BlockSpec(memory_space=pltpu.SMEM)] + [head] * 6,
        out_specs=[head] * 3,
        out_shape=[jax.ShapeDtypeStruct((s, dm), BF16)] * 3,
        scratch_shapes=[f32buf] + [b16buf] * 4 + [f32buf] * 9 + [pltpu.VMEM((ATTN_GROUP, bq, 2 * bq), BF16)] * 2,
        compiler_params=pltpu.CompilerParams(dimension_semantics=("parallel",)),
    )(slopes, q, k, v, o, lse, do)


def _ep_id(accs, ex):
    return [accs[0]]


def _ep_all(accs, ex):
    return list(accs)


def _ep_sum(accs, ex):
    return [accs[0] + accs[1]]


def _ep_add(accs, ex):
    return [accs[0] + ex[0].astype(F32)]


def _ep_bias_res(accs, ex):
    return [accs[0] + ex[0] + ex[1]]


def _ep_glu(accs, ex):
    a = accs[0] + ex[0]
    gt = accs[1] + ex[1]
    return [a * _sigmoid(gt), a, gt]


def _ep_swiglu(accs, ex):
    g, u = accs
    return [g, u, g * _sigmoid(g) * u]


def _ep_swiglu_bwd(accs, ex):
    dact = accs[0]
    g = ex[0].astype(F32)
    u = ex[1].astype(F32)
    sg = _sigmoid(g)
    return [dact * u * (sg * (1.0 + g * (1.0 - sg))), dact * g * sg]


def _ffn_fwd(tag, h, gain, wg, wu, wd):
    s, d = h.shape
    f = wg.shape[-1]
    (n,) = _rms_fwd(f"{tag}_norm", h, gain)
    g, u, act = _mm(f"{tag}_gate_up", "nn", n, [(wg, None, 0), (wu, None, 0)], _ep_swiglu,
                    [(BF16,), (BF16,), (BF16,)], m=s, n=f, k=d)
    (out,) = _mm(f"{tag}_down", "nn", act, [(wd, None, 0)], _ep_add, [(F32,)], m=s, n=d, k=f,
                 extras=[(h, "mn", 0)], bn=256)
    return out, (n, g, u, act)


def _ffn_bwd(tag, h_in, gain, wg, wu, wd, saved, dh, dhb, after, emit):
    s, d = h_in.shape
    f = wg.shape[-1]
    n, g, u, act = saved
    dg, du = _mm(f"{tag}_bwd_dact", "nt", dhb, [(wd, None, 0)], _ep_swiglu_bwd, [(BF16,), (BF16,)],
                 m=s, n=f, k=d, extras=[(g, "mn", 0), (u, "mn", 0)], after=after)
    (dwd,) = _mm(f"{tag}_bwd_dwd", "tn", act, [(dhb, None, 0)], _ep_id, [(BF16,)], m=f, n=d, k=s,
                 bm=f // 4)
    dwg, dwu = _mm(f"{tag}_bwd_dwgu", "tn", n, [(dg, None, 0), (du, None, 0)], _ep_all, [(BF16,), (BF16,)],
                   m=d, n=f, k=s)
    pin = emit(dict(gate=dwg, up=dwu, down=dwd))
    (dn,) = _mm(f"{tag}_bwd_dn", "nt", [dg, du], [(wg, None, 0), (wu, None, 0)], _ep_sum, [(F32,)],
                m=s, n=d, k=f, bm=512, bn=256, after=pin)
    dx, dxb, dgain, cs = _rms_bwd(f"{tag}_bwd_norm", h_in, gain, [dn], dh)
    return dx, dxb, dgain, cs


def _local_step(x, target, w, fetch, emit, after=()):
    s, d = x.shape
    nh = d // HEAD_DIM
    bq = BRANCHES[0][0] // BRANCHES[0][1]
    assert all(win // dil == bq for win, dil in BRANCHES)
    assert BRANCHES[0][1] == 1 and all(dil > 1 for _, dil in BRANCHES[1:])
    slopes = _alibi_slopes(nh)
    nd = d // 512 if d >= 512 else 1
    bn = d // nd

    (n1,) = _rms_fwd("a_norm", x, w["a_norm_g"], after=after)
    w_conv1 = fetch("conv1", n1)["conv_w1"]
    glu, a, gt = _mm("conv_pw1_glu", "nn", n1, [(w_conv1, None, 0), (w_conv1, None, nd)], _ep_glu,
                     [(F32,), (BF16,), (BF16,)], m=s, n=d, k=d, bn=bn,
                     extras=[(w["conv_b1"], "n", 0), (w["conv_b1"], "n", nd)])
    c, sw = _dwconv_fwd("conv_dw_ln", glu, w["conv_dw"], w["conv_dw_b"], w["conv_ln_g"], w["conv_ln_b"])
    w_conv2 = fetch("conv2", sw)["conv_w2"]
    (h1,) = _mm("conv_pw2", "nn", sw, [(w_conv2, None, 0)], _ep_bias_res, [(F32,)], m=s, n=d, k=d,
                extras=[(w["conv_b2"], "n", 0), (x, "mn", 0)])
    wf0 = fetch("ffn0", h1)
    h2, ffn0 = _ffn_fwd("ffn0", h1, w["ffn_norm_g"][0:1], wf0["gate"], wf0["up"], wf0["down"])
    wa = fetch("attn", h2)
    kvn, qn = _rms_fwd("kvq_norm", h2, jnp.concatenate([w["kv_norm_g"], w["b_norm_g"]], axis=0))
    k, v = _mm("kv_proj", "nn", kvn, [(wa["w_k"], None, 0), (wa["w_v"], None, 0)], _ep_all, [(BF16,), (BF16,)],
               m=s, n=d, k=d)
    (q,) = _mm("q_proj", "nn", qn, [(wa["w_q"], None, 0)], _ep_id, [(BF16,)], m=s, n=d, k=d)
    att, attb, lse = _attn_fwd("attn_fwd", q, k, v, slopes, bq)
    (h3,) = _mm("o_proj", "nn", attb, [(wa["w_o"], None, 0)], _ep_add, [(F32,)], m=s, n=d, k=d,
                extras=[(h2, "mn", 0)])
    wf1 = fetch("ffn1", h3)
    h4, ffn1 = _ffn_fwd("ffn1", h3, w["ffn_norm_g"][1:2], wf1["gate"], wf1["up"], wf1["down"])
    dh4, dh4b, d_final_g, loss_cols = _final_loss("final_loss", h4, w["final_norm_g"], target)

    g = {}
    ga = {}
    dh3, dh3b, dgain1, _ = _ffn_bwd("ffn1", h3, w["ffn_norm_g"][1:2], wf1["gate"], wf1["up"], wf1["down"], ffn1,
                                    dh4, dh4b, (), functools.partial(emit, "ffn1"))
    (datt,) = _mm("o_proj_bwd_dx", "nt", dh3b, [(wa["w_o"], None, 0)], _ep_id, [(BF16,)], m=s, n=d, k=d)
    (ga["w_o"],) = _mm("o_proj_bwd_dw", "tn", attb, [(dh3b, None, 0)], _ep_id, [(BF16,)], m=d, n=d, k=s)
    dq, dk, dv = _attn_bwd("attn_bwd", q, k, v, att, lse, datt, slopes, bq)
    (ga["w_q"],) = _mm("q_proj_bwd_dw", "tn", qn, [(dq, None, 0)], _ep_id, [(BF16,)], m=d, n=d, k=s)
    ga["w_k"], ga["w_v"] = _mm("kv_proj_bwd_dw", "tn", kvn, [(dk, None, 0), (dv, None, 0)], _ep_all,
                               [(BF16,), (BF16,)], m=d, n=d, k=s)
    pin = emit("attn", ga)
    (dqn,) = _mm("q_proj_bwd_dx", "nt", dq, [(wa["w_q"], None, 0)], _ep_id, [(F32,)], m=s, n=d, k=d, after=pin)
    (dkvn,) = _mm("kv_proj_bwd_dx", "nt", [dk, dv], [(wa["w_k"], None, 0), (wa["w_v"], None, 0)], _ep_sum, [(F32,)],
                  m=s, n=d, k=d)
    dh2, dh2b, dg_kvq, _ = _rms_bwd("kvq_norm_bwd", h2, jnp.concatenate([w["kv_norm_g"], w["b_norm_g"]], axis=0),
                                    [dkvn, dqn], dh3)
    dh1, dh1b, dgain0, cs_h1 = _ffn_bwd("ffn0", h1, w["ffn_norm_g"][0:1], wf0["gate"], wf0["up"], wf0["down"], ffn0,
                                        dh2, dh2b, (), functools.partial(emit, "ffn0"))
    (dsw,) = _mm("conv_pw2_bwd_dx", "nt", dh1b, [(w_conv2, None, 0)], _ep_id, [(F32,)], m=s, n=d, k=d)
    (dw2,) = _mm("conv_pw2_bwd_dw", "tn", sw, [(dh1b, None, 0)], _ep_id, [(BF16,)], m=d, n=d, k=s)
    pin = emit("conv2", dict(conv_w2=dw2))
    dc, ln_sums = _conv_ln_bwd("conv_ln_bwd", c, dsw, w["conv_ln_g"], w["conv_ln_b"], after=pin)
    dpre, ddw, db1 = _conv_dw_bwd("conv_dw_bwd", dc, glu, a, gt, w["conv_dw"])
    (dw1,) = _mm("conv_pw1_bwd_dw", "tn", n1, [(dpre, None, 0)], _ep_id, [(BF16,)], m=d, n=2 * d, k=s)
    pin = emit("conv1", dict(conv_w1=dw1))
    (dn1,) = _mm("conv_pw1_bwd_dx", "nt", dpre, [(w_conv1, None, 0)], _ep_id, [(F32,)], m=s, n=d, k=2 * d,
                 after=pin)
    dx, _, d_a_norm, _ = _rms_bwd("a_norm_bwd", x, w["a_norm_g"], [dn1], dh1)

    g.update(
        a_norm_g=d_a_norm, conv_b1=db1, conv_dw=ddw, conv_dw_b=ln_sums[2:3], conv_ln_g=ln_sums[0:1],
        conv_ln_b=ln_sums[1:2], conv_b2=cs_h1, kv_norm_g=dg_kvq[0:1], b_norm_g=dg_kvq[1:2],
        ffn_norm_g=jnp.concatenate([dgain0, dgain1], axis=0), final_norm_g=d_final_g,
    )
    return loss_cols, dx, g


HBM_SPEC = pl.BlockSpec(memory_space=pltpu.HBM)


def _mesh_place():
    x, y, c = lax.axis_index("x"), lax.axis_index("y"), lax.axis_index("c")
    chips = [(1 - x, y), (x, 1 - y), (1 - x, 1 - y)]
    return x, y, c, chips


def _shard_view(ref, kind, s, half=None):
    rows, cols = ref.shape
    if kind == "col":
        cw = cols // N_CHIPS
        if half is None:
            return ref.at[pl.ds(0, rows), pl.ds(s * cw, cw)]
        return ref.at[pl.ds(half * (rows // 2), rows // 2), pl.ds(s * cw, cw)]
    r = rows // N_CHIPS
    if half is None:
        return ref.at[pl.ds(s * r, r), pl.ds(0, cols)]
    return ref.at[pl.ds(s * r + half * (r // 2), r // 2), pl.ds(0, cols)]


def _full_shape(shard, kind):
    r, cw = shard.shape[-2:]
    return (r, cw * N_CHIPS) if kind == "col" else (r * N_CHIPS, cw)


def _gather_weights(shards, kinds, after=()):
    nt = len(shards)
    nf = len(after)
    fulls = [jax.ShapeDtypeStruct(_full_shape(sh, kind), sh.dtype) for sh, kind in zip(shards, kinds)]

    def body(*refs):
        src = refs[:nt]
        dst = refs[nt + nf:2 * nt + nf]
        send, recv, fsend, frecv, local = refs[2 * nt + nf:]
        x, y, c, chips = _mesh_place()
        s = 2 * x + y
        sib = (x, y, 1 - c)

        def half_of_shard(t):
            r, cw = src[t].shape
            return src[t].at[pl.ds(c * (r // 2), r // 2), pl.ds(0, cw)]

        locals_ = [pltpu.make_async_copy(src[t], _shard_view(dst[t], kinds[t], s), local.at[t]) for t in range(nt)]
        for cp in locals_:
            cp.start()
        sends = []
        for t in range(nt):
            for j, chip in enumerate(chips):
                cp = pltpu.make_async_remote_copy(
                    src_ref=half_of_shard(t), dst_ref=_shard_view(dst[t], kinds[t], s, c),
                    send_sem=send.at[t, j], recv_sem=recv.at[t, j], device_id=(*chip, c), device_id_type=MESH)
                cp.start()
                sends.append(cp)
        for t in range(nt):
            for j, (px, py) in enumerate(chips):
                landed = _shard_view(dst[t], kinds[t], 2 * px + py, c)
                pltpu.make_async_remote_copy(
                    src_ref=half_of_shard(t), dst_ref=landed, send_sem=send.at[t, j], recv_sem=recv.at[t, j],
                    device_id=(px, py, c), device_id_type=MESH).wait_recv()
                cp = pltpu.make_async_remote_copy(
                    src_ref=landed, dst_ref=landed, send_sem=fsend.at[t, j], recv_sem=frecv.at[t, j],
                    device_id=sib, device_id_type=MESH)
                cp.start()
                sends.append(cp)
        for t in range(nt):
            for j, (px, py) in enumerate(chips):
                other = _shard_view(dst[t], kinds[t], 2 * px + py, 1 - c)
                pltpu.make_async_remote_copy(
                    src_ref=other, dst_ref=other, send_sem=fsend.at[t, j], recv_sem=frecv.at[t, j],
                    device_id=sib, device_id_type=MESH).wait_recv()
        for cp in sends:
            cp.wait_send()
        for cp in locals_:
            cp.wait()

    return pl.pallas_call(
        body,
        name="gather_weights",
        in_specs=[HBM_SPEC] * nt + [ANY_SPEC] * nf,
        out_specs=[HBM_SPEC] * nt,
        out_shape=fulls,
        scratch_shapes=[pltpu.SemaphoreType.DMA((nt, 3))] * 4 + [pltpu.SemaphoreType.DMA((nt,))],
    )(*shards, *after)


def _row_blocks(rows, want=512):
    nb = 1
    while rows // nb > want or rows % nb or (rows // nb) % 16:
        nb += 1
        if nb > rows:
            return rows, 1
    return rows // nb, nb


def _place_own(name, src, kind, land, ids, into_slot, lead=0):
    if into_slot:
        _, r, cw = land.shape
    else:
        r, cw = src.shape[-2:]
    tr, nb = _row_blocks(r)
    if not into_slot:
        if src.ndim == 3:
            src_spec = pl.BlockSpec((None, tr, cw), lambda i, ids_ref: (lead, i, 0))
        else:
            src_spec = pl.BlockSpec((tr, cw), lambda i, ids_ref: (i, 0))
        if kind == "col":
            dst_spec = pl.BlockSpec((tr, cw), lambda i, ids_ref: (i, ids_ref[0]))
        else:
            dst_spec = pl.BlockSpec((tr, cw), lambda i, ids_ref: (ids_ref[0] * nb + i, 0))
    else:
        if kind == "col":
            src_spec = pl.BlockSpec((tr, cw), lambda i, ids_ref: (i, ids_ref[0]))
        elif kind == "row":
            src_spec = pl.BlockSpec((tr, cw), lambda i, ids_ref: (ids_ref[0] * nb + i, 0))
        else:
            src_spec = pl.BlockSpec((tr, cw), lambda i, ids_ref: (i, 0))
        dst_spec = pl.BlockSpec((None, tr, cw), lambda i, ids_ref: (ids_ref[1], i, 0))

    def body(ids_ref, s_ref, o_ref):
        o_ref[...] = s_ref[...].astype(o_ref.dtype)

    return pl.pallas_call(
        body,
        name=name,
        grid_spec=pltpu.PrefetchScalarGridSpec(num_scalar_prefetch=1, grid=(nb,), in_specs=[src_spec],
                                               out_specs=dst_spec),
        out_shape=land,
        compiler_params=pltpu.CompilerParams(dimension_semantics=("parallel",)),
    )(ids, src)


SEM_SPEC = pl.BlockSpec(memory_space=pltpu.SEMAPHORE)
SIDE_EFFECT = pltpu.SideEffectType.DATAFLOW_SIDE_EFFECTING


def _copies_start(name, srcs, lands, copies, n_sems, after):
    ns, nl, nf = len(srcs), len(lands), len(after)

    def body(*refs):
        src, land = refs[:ns], refs[ns:ns + nl]
        send, recv = refs[ns + nl + nf], refs[ns + nl + nf + 1]
        pin = refs[-1]
        for cp in copies(src, land, send, recv, _mesh_place()):
            cp.start()
        pin[...] = jnp.zeros_like(pin)

    arrs = list(srcs) + list(lands)
    res = pl.pallas_call(
        body,
        name=name,
        in_specs=[HBM_SPEC] * (ns + nl) + [ANY_SPEC] * nf,
        out_specs=[SEM_SPEC, SEM_SPEC] + [HBM_SPEC] * (ns + nl) + [pl.BlockSpec(memory_space=pltpu.VMEM)],
        out_shape=[pltpu.SemaphoreType.DMA((n_sems,)), pltpu.SemaphoreType.DMA((n_sems,))]
        + [pltpu.HBM(a.shape, a.dtype) for a in arrs] + [jax.ShapeDtypeStruct((8, 128), F32)],
        input_output_aliases={i: 2 + i for i in range(ns + nl)},
        compiler_params=pltpu.CompilerParams(has_side_effects=SIDE_EFFECT),
    )(*[pltpu.with_memory_space_constraint(a, pltpu.HBM) for a in arrs], *after)
    return res[0], res[1], list(res[2:2 + ns]), list(res[2 + ns:2 + ns + nl]), res[-1]


def _copies_wait(name, send, recv, srcs, lands, copies, after):
    ns, nl, nf = len(srcs), len(lands), len(after)

    def body(*refs):
        src, land = refs[:ns], refs[ns:ns + nl]
        send_sems, recv_sems = refs[ns + nl], refs[ns + nl + 1]
        cps = copies(src, land, send_sems, recv_sems, _mesh_place())
        for cp in cps:
            cp.wait_send()
        for cp in cps:
            cp.wait_recv()

    arrs = list(srcs) + list(lands)
    res = pl.pallas_call(
        body,
        name=name,
        in_specs=[HBM_SPEC] * (ns + nl) + [SEM_SPEC, SEM_SPEC] + [ANY_SPEC] * nf,
        out_specs=[HBM_SPEC] * (ns + nl),
        out_shape=[pltpu.HBM(a.shape, a.dtype) for a in arrs],
        input_output_aliases={i: i for i in range(ns + nl)},
        compiler_params=pltpu.CompilerParams(has_side_effects=SIDE_EFFECT),
    )(*arrs, send, recv, *after)
    return list(res[ns:])


def _gather_copies(kinds, halves):
    def copies(src, land, send, recv, place):
        x, y, c, chips = place
        s = 2 * x + y
        mine = [_shard_view(land[t], kinds[t], s, c if halves else None) for t in range(len(kinds))]
        return [
            pltpu.make_async_remote_copy(
                src_ref=mine[t], dst_ref=mine[t], send_sem=send.at[3 * t + j], recv_sem=recv.at[3 * t + j],
                device_id=(px, py, c), device_id_type=MESH)
            for t in range(len(kinds)) for j, (px, py) in enumerate(chips)
        ]

    return copies


def _gather_pass_on(name, lands, kinds):
    nt = len(lands)

    def body(*refs):
        buf = refs[nt:2 * nt]
        send, recv = refs[2 * nt:]
        x, y, c, chips = _mesh_place()
        sib = (x, y, 1 - c)
        sends = []
        for t in range(nt):
            for j, (px, py) in enumerate(chips):
                mine = _shard_view(buf[t], kinds[t], 2 * px + py, c)
                cp = pltpu.make_async_remote_copy(src_ref=mine, dst_ref=mine, send_sem=send.at[t, j],
                                                  recv_sem=recv.at[t, j], device_id=sib, device_id_type=MESH)
                cp.start()
                sends.append(cp)
        for t in range(nt):
            for j, (px, py) in enumerate(chips):
                theirs = _shard_view(buf[t], kinds[t], 2 * px + py, 1 - c)
                pltpu.make_async_remote_copy(src_ref=theirs, dst_ref=theirs, send_sem=send.at[t, j],
                                             recv_sem=recv.at[t, j], device_id=sib, device_id_type=MESH).wait_recv()
        for cp in sends:
            cp.wait_send()

    return pl.pallas_call(
        body,
        name=name,
        in_specs=[HBM_SPEC] * nt,
        out_specs=[HBM_SPEC] * nt,
        out_shape=[jax.ShapeDtypeStruct(a.shape, a.dtype) for a in lands],
        input_output_aliases={i: i for i in range(nt)},
        scratch_shapes=[pltpu.SemaphoreType.DMA((nt, 3))] * 2,
    )(*lands)


def _grad_part(ref, kind, s):
    return ref if kind == "all" else _shard_view(ref, kind, s)


def _grad_copies(kinds):
    def peers(place):
        x, y, c, chips = place
        return [(x, y, 1 - c)] + [(px, py, c) for px, py in chips]

    def copies(src, land, send, recv, place):
        x, y, c, chips = place
        me = 4 * x + 2 * y + c
        return [
            pltpu.make_async_remote_copy(
                src_ref=_grad_part(src[t], kinds[t], 2 * px + py), dst_ref=land[t].at[me],
                send_sem=send.at[GRAD_PEERS * t + k], recv_sem=recv.at[GRAD_PEERS * t + k], device_id=(px, py, pc),
                device_id_type=MESH)
            for t in range(len(kinds)) for k, (px, py, pc) in enumerate(peers(place))
        ]

    return copies


def _pass_copies(n):
    def copies(src, land, send, recv, place):
        x, y, c, chips = place
        return [
            pltpu.make_async_remote_copy(
                src_ref=land[t].at[4 * px + 2 * py + c], dst_ref=land[t].at[4 * px + 2 * py + c],
                send_sem=send.at[3 * t + j], recv_sem=recv.at[3 * t + j], device_id=(x, y, 1 - c),
                device_id_type=MESH)
            for t in range(n) for j, (px, py) in enumerate(chips)
        ]

    return copies


GRAD_PEERS = 4


def _land_shape(grad, kind):
    rows, cols = grad.shape
    if kind == "col":
        return (N_DEV, rows, cols // N_CHIPS)
    if kind == "row":
        return (N_DEV, rows // N_CHIPS, cols)
    return (N_DEV, rows, cols)


def _adamw_reduce(name, contrib, w, m, v, layer=None, prev=None):
    rows, cols = w.shape[-2:]
    t = 128 if rows % 128 == 0 else rows
    c1 = 1.0 - ADAM_B1 ** ADAM_STEP
    c2 = 1.0 - ADAM_B2 ** ADAM_STEP

    n_prev = 0 if prev is None else 4

    def body(c_ref, w_ref, m_ref, v_ref, *refs):
        g_ref, d_ref, nm_ref, nv_ref = refs[n_prev:]
        g = c_ref[0].astype(F32)
        for q in range(1, N_DEV):
            g = g + c_ref[q].astype(F32)
        nm = ADAM_B1 * m_ref[...] + (1.0 - ADAM_B1) * g
        nv = ADAM_B2 * v_ref[...] + (1.0 - ADAM_B2) * (g * g)
        g_ref[...] = g
        nm_ref[...] = nm
        nv_ref[...] = nv
        d_ref[...] = -ADAM_LR * ((nm / c1) / (jnp.sqrt(nv / c2) + ADAM_EPS) + ADAM_WD * w_ref[...])

    if layer is None:
        blk = pl.BlockSpec((t, cols), lambda i: (i, 0))
    else:
        blk = pl.BlockSpec((None, t, cols), lambda i: (layer, i, 0))
    return pl.pallas_call(
        body,
        name=name,
        grid=(rows // t,),
        in_specs=[pl.BlockSpec((N_DEV, t, cols), lambda i: (0, i, 0)), blk, blk, blk] + [ANY_SPEC] * n_prev,
        out_specs=[blk] * 4,
        out_shape=[jax.ShapeDtypeStruct(w.shape, F32)] * 4,
        input_output_aliases={4 + i: i for i in range(n_prev)},
        compiler_params=pltpu.CompilerParams(dimension_semantics=("parallel",)),
    )(contrib, w, m, v, *(prev or ()))


WEIGHT_NAMES = ("a_norm_g", "conv_w1", "conv_b1", "conv_dw", "conv_dw_b", "conv_ln_g", "conv_ln_b", "conv_w2",
                "conv_b2", "kv_norm_g", "w_k", "w_v", "b_norm_g", "w_q", "w_o", "ffn_norm_g", "ffn_w_gate",
                "ffn_w_up", "ffn_w_down", "final_norm_g")
GROUPS = {
    "conv2": (("conv_w2", "conv_w2", None, "row"),),
    "ffn0": (("gate", "ffn_w_gate", 0, "col"), ("up", "ffn_w_up", 0, "col"), ("down", "ffn_w_down", 0, "row")),
    "attn": (("w_k", "w_k", None, "row"), ("w_v", "w_v", None, "row"), ("w_q", "w_q", None, "row"),
             ("w_o", "w_o", None, "row")),
    "ffn1": (("gate", "ffn_w_gate", 1, "col"), ("up", "ffn_w_up", 1, "col"), ("down", "ffn_w_down", 1, "row")),
}
GROUPS["conv1"] = (("conv_w1", "conv_w1", None, "col"),)
FETCH_ORDER = ("conv1", "conv2", "ffn0", "attn", "ffn1")
HALVED = ("conv1", "conv2", "ffn0")
EMIT_ORDER = ("ffn1", "attn", "ffn0", "conv2", "conv1", "vec")
RETIRE_AT = {"attn": ("ffn1",), "ffn0": ("attn",), "conv1": ("ffn0", "conv2")}
PACKED = (("a_norm_g", 0, 1), ("conv_b1", 8, 2), ("conv_dw", 16, CONV_WIDTH), ("conv_dw_b", 48, 1),
          ("conv_ln_g", 56, 1), ("conv_ln_b", 64, 1), ("conv_b2", 72, 1))
PACK_ROWS = 80
WHOLE = (("kv_norm_g", 0, 1), ("b_norm_g", 1, 1), ("ffn_norm_g", 2, 2), ("final_norm_g", 4, 1))
WHOLE_ROWS = 8


def _pack_rows(parts, total, width):
    out, at = [], 0
    for arr, first in parts:
        if first > at:
            out.append(jnp.zeros((first - at, width), F32))
        rows8 = -(-arr.shape[0] // 8) * 8
        out.append(jnp.pad(arr, ((0, rows8 - arr.shape[0]), (0, 0))))
        at = first + rows8
    if total > at:
        out.append(jnp.zeros((total - at, width), F32))
    return jnp.concatenate(out, axis=0)


def kernel(x, a_norm_g, conv_w1, conv_b1, conv_dw, conv_dw_b, conv_ln_g, conv_ln_b, conv_w2, conv_b2, kv_norm_g, w_k, w_v, b_norm_g, w_q, w_o, ffn_norm_g, ffn_w_gate, ffn_w_up, ffn_w_down, final_norm_g, loss_target, m_a_norm_g, m_conv_w1, m_conv_b1, m_conv_dw, m_conv_dw_b, m_conv_ln_g, m_conv_ln_b, m_conv_w2, m_conv_b2, m_kv_norm_g, m_w_k, m_w_v, m_b_norm_g, m_w_q, m_w_o, m_ffn_norm_g, m_ffn_w_gate, m_ffn_w_up, m_ffn_w_down, m_final_norm_g, v_a_norm_g, v_conv_w1, v_conv_b1, v_conv_dw, v_conv_dw_b, v_conv_ln_g, v_conv_ln_b, v_conv_w2, v_conv_b2, v_kv_norm_g, v_w_k, v_w_v, v_b_norm_g, v_w_q, v_w_o, v_ffn_norm_g, v_ffn_w_gate, v_ffn_w_up, v_ffn_w_down, v_final_norm_g):
    args = locals()
    wts = {n: args[n] for n in WEIGHT_NAMES}
    mom = {n: args["m_" + n] for n in WEIGHT_NAMES}
    vel = {n: args["v_" + n] for n in WEIGHT_NAMES}
    s, d = x.shape[-2:]
    dq = d // N_CHIPS
    x2 = x.reshape(s, d)
    tgt = loss_target.reshape(s, d)

    def pack_shard(src):
        return _pack_rows([(src[n].reshape(-1, dq), first) for n, first, _ in PACKED], PACK_ROWS, dq)

    def pack_whole(src):
        return _pack_rows([(jnp.concatenate([src[n].reshape(-1, d) for n, _, _ in WHOLE], axis=0), 0)], WHOLE_ROWS, d)

    gathers = {}
    pins = []
    ids = jnp.stack([2 * lax.axis_index("x") + lax.axis_index("y"),
                     4 * lax.axis_index("x") + 2 * lax.axis_index("y") + lax.axis_index("c")]).astype(jnp.int32)

    def start_gather(grp, pins):
        kinds = [kind for _, _, _, kind in GROUPS[grp]]
        lands = []
        for key, n, layer, kind in GROUPS[grp]:
            shard = wts[n] if layer is not None else wts[n].reshape(wts[n].shape[-2:])
            full = jax.ShapeDtypeStruct(_full_shape(shard, kind), BF16)
            lands.append(_place_own(f"gather_place_{grp}_{key}", shard, kind, full, ids, False, layer))
        copies = _gather_copies(kinds, grp in HALVED)
        send, recv, _, lands, pin = _copies_start("gather_start_" + grp, [], lands, copies, 3 * len(lands), pins)
        gathers[grp] = (send, recv, [], lands, copies, kinds)
        return [pin]

    (packed_full,) = _gather_weights([pack_shard(wts)], ["row"])
    w = {}
    pins = [packed_full]
    for grp in FETCH_ORDER:
        pins = start_gather(grp, pins)

    def fetch(grp, after):
        send, recv, srcs, lands, copies, kinds = gathers[grp]
        fulls = _copies_wait("gather_wait_" + grp, send, recv, srcs, lands, copies, [after])
        if grp in HALVED:
            fulls = _gather_pass_on("gather_pass_" + grp, fulls, kinds)
        return {key: full for (key, _, _, _), full in zip(GROUPS[grp], fulls)}

    packed = packed_full.reshape(N_CHIPS, PACK_ROWS, dq)
    for n, first, rows in PACKED:
        part = packed[:, first:first + rows, :]
        if n == "conv_dw":
            w[n] = part.transpose(1, 0, 2).reshape(rows, d)
        else:
            w[n] = part.reshape(1, N_CHIPS * rows * dq)
    for n, _, rows in WHOLE:
        w[n] = wts[n].reshape(rows, d)

    exchanges = {}
    passing = {}

    def retire(tag, after):
        send, recv, srcs, lands, copies = exchanges[tag]
        lands = _copies_wait("grads_wait_" + tag, send, recv, srcs, lands, copies, after)
        copies = _pass_copies(len(lands))
        send, recv, _, lands, pin = _copies_start("grads_pass_" + tag, [], lands, copies, 3 * len(lands), ())
        passing[tag] = (send, recv, lands, copies)
        return pin

    def emit_grads(tag, grads, kinds):
        pins = [retire(old, [grads[0]]) for old in RETIRE_AT.get(tag, ())]
        copies = _grad_copies(kinds)
        lands = [_place_own(f"grads_place_{tag}_{t}", gr, kd, jax.ShapeDtypeStruct(_land_shape(gr, kd), gr.dtype),
                            ids, True) for t, (gr, kd) in enumerate(zip(grads, kinds))]
        send, recv, srcs, lands, pin = _copies_start("grads_start_" + tag, grads, lands, copies,
                                                     GRAD_PEERS * len(grads), pins)
        exchanges[tag] = (send, recv, srcs, lands, copies)
        return [pin]

    def emit(grp, grads):
        return emit_grads(grp, [grads[key] for key, _, _, _ in GROUPS[grp]], [kind for _, _, _, kind in GROUPS[grp]])

    loss_cols, dx, g = _local_step(x2, tgt, w, fetch, emit, after=pins)
    loss = lax.psum(jnp.sum(loss_cols), ("x", "y", "c"))

    gp = []
    for n, first, rows in PACKED:
        if n == "conv_dw":
            part = g[n].reshape(rows, N_CHIPS, dq).transpose(1, 0, 2)
        else:
            part = g[n].reshape(N_CHIPS, rows, dq)
        gp.append((part, first))
    g_packed = jnp.concatenate(
        [_pack_rows([(p[ci], first) for p, first in gp], PACK_ROWS, dq) for ci in range(N_CHIPS)], axis=0)
    emit_grads("vec", [g_packed, pack_whole(g)], ["row", "all"])
    for tag in EMIT_ORDER:
        if tag not in passing:
            retire(tag, [dx])
    contribs = {}
    for tag in EMIT_ORDER:
        send, recv, lands, copies = passing[tag]
        contribs[tag] = _copies_wait("grads_passed_" + tag, send, recv, [], lands, copies, [dx])

    res = {}

    def adamw(n, contrib, layer=None, prev=None):
        if layer is None:
            shape = wts[n].shape
            r2 = shape[-2:]
            outs = _adamw_reduce("adamw_" + n, contrib, wts[n].reshape(r2), mom[n].reshape(r2), vel[n].reshape(r2))
            return [o.reshape(shape) for o in outs]
        return _adamw_reduce(f"adamw_{n}_{layer}", contrib, wts[n], mom[n], vel[n], layer, prev)

    for grp in ("attn", "conv2", "conv1"):
        for (key, n, _, _), contrib in zip(GROUPS[grp], contribs[grp]):
            res[n] = adamw(n, contrib)
    for (key, n, _, _), c0, c1 in zip(GROUPS["ffn0"], contribs["ffn0"], contribs["ffn1"]):
        res[n] = adamw(n, c1, 1, adamw(n, c0, 0))
    outs = _adamw_reduce("adamw_packed", contribs["vec"][0], pack_shard(wts), pack_shard(mom), pack_shard(vel))
    for n, first, rows in PACKED:
        res[n] = [o[first:first + rows].reshape(wts[n].shape) for o in outs]
    outs = _adamw_reduce("adamw_whole", contribs["vec"][1], pack_whole(wts), pack_whole(mom), pack_whole(vel))
    for n, first, rows in WHOLE:
        res[n] = [o[first:first + rows].reshape(wts[n].shape) for o in outs]

    out = [loss, dx.reshape(x.shape)]
    for which in range(4):
        out += [res[n][which] for n in WEIGHT_NAMES]
    return tuple(out)
```

```python
import functools
import math

import jax
import jax.numpy as jnp
from jax import lax
from jax.experimental import pallas as pl
from jax.experimental.pallas import tpu as pltpu

F32 = jnp.float32
BF16 = jnp.bfloat16

HEAD_DIM = 128
BRANCHES = ((128, 1), (512, 4), (2048, 16))
CONV_WIDTH = 31
CONV_HALO = 32
RMS_EPS = 1e-6
LN_EPS = 1e-5
ADAM_LR = 0.001
ADAM_B1 = 0.9
ADAM_B2 = 0.999
ADAM_EPS = 1e-08
ADAM_WD = 0.01
ADAM_STEP = 10
N_CHIPS = 4
N_DEV = 8
MESH = pl.DeviceIdType.MESH


def _sigmoid(x):
    return 1.0 / (1.0 + jnp.exp(-x))


def _row_tile(rows, want):
    t = min(rows, want)
    assert rows % t == 0, (rows, want)
    return t


_DOT_DIMS = {"nn": ((1,), (0,)), "nt": ((1,), (1,)), "tn": ((0,), (0,))}


ANY_SPEC = pl.BlockSpec(memory_space=pl.ANY)
WIDE_BN = 1024


def _mm(name, mode, a, bs, epilogue, outs, *, m, n, k, extras=(), bm=1024, bn=512, bk=None, after=()):
    bm, bn = min(bm, m), min(bn, n)
    bk = k if bk is None else min(bk, k)
    assert m % bm == 0 and n % bn == 0 and k % bk == 0, (name, m, n, k, bm, bn, bk)
    nk = k // bk
    a_list = list(a) if isinstance(a, (list, tuple)) else [a]
    na, nb, ne, no = len(a_list), len(bs), len(extras), len(outs)
    assert na in (1, nb)

    if mode == "tn":
        a_spec = pl.BlockSpec((bk, bm), lambda i, j, kk: (kk, i))
    else:
        a_spec = pl.BlockSpec((bm, bk), lambda i, j, kk: (i, kk))

    def b_spec(lead, off):
        if mode == "nt":
            blk, idx = (bn, bk), (lambda i, j, kk: (j + off, kk))
        else:
            blk, idx = (bk, bn), (lambda i, j, kk: (kk, j + off))
        if lead is None:
            return pl.BlockSpec(blk, idx)
        return pl.BlockSpec((None,) + blk, lambda i, j, kk: (lead,) + idx(i, j, kk))

    def e_spec(kind, off):
        if kind == "mn":
            return pl.BlockSpec((bm, bn), lambda i, j, kk: (i, j + off))
        return pl.BlockSpec((1, bn), lambda i, j, kk: (0, j + off))

    nf = len(after)
    in_specs = [a_spec] * na + [b_spec(l, o) for _, l, o in bs] + [e_spec(kd, o) for _, kd, o in extras]
    in_specs += [ANY_SPEC] * nf
    out_specs = [pl.BlockSpec((bm, bn), lambda i, j, kk: (i, j)) for _ in outs]
    out_shape = [jax.ShapeDtypeStruct((m, n), dt) for (dt,) in outs]
    dims = (_DOT_DIMS[mode], ((), ()))

    def body(*refs):
        a_refs = refs[:na]
        b_refs = refs[na:na + nb]
        e_refs = refs[na + nb:na + nb + ne]
        o_refs = refs[na + nb + ne + nf:na + nb + ne + nf + no]
        acc_refs = refs[na + nb + ne + nf + no:]
        avs = [a_ref[...].astype(BF16) for a_ref in a_refs]
        prods = [lax.dot_general(avs[bi % na], b_ref[...].astype(BF16), dims, preferred_element_type=F32)
                 for bi, b_ref in enumerate(b_refs)]

        def finish(accs):
            res = epilogue(accs, [e_ref[...] for e_ref in e_refs])
            for o_ref, r in zip(o_refs, res):
                o_ref[...] = r.astype(o_ref.dtype)

        if nk == 1:
            finish(prods)
        else:
            kk = pl.program_id(2)

            @pl.when(kk == 0)
            def _():
                for acc_ref, p in zip(acc_refs, prods):
                    acc_ref[...] = p

            @pl.when(kk > 0)
            def _():
                for acc_ref, p in zip(acc_refs, prods):
                    acc_ref[...] += p

            @pl.when(kk == nk - 1)
            def _():
                finish([acc_ref[...] for acc_ref in acc_refs])

    scratch = [] if nk == 1 else [pltpu.VMEM((bm, bn), F32) for _ in bs]
    res = pl.pallas_call(
        body,
        name=name,
        grid=(m // bm, n // bn, nk),
        in_specs=in_specs,
        out_specs=out_specs,
        out_shape=out_shape,
        scratch_shapes=scratch,
        compiler_params=pltpu.CompilerParams(dimension_semantics=("parallel", "parallel", "arbitrary")),
    )(*a_list, *[b for b, _, _ in bs], *[e for e, _, _ in extras], *after)
    return res


def _rms_fwd(name, x, gains, after=()):
    s, d = x.shape
    ng = gains.shape[0]
    t = _row_tile(s, 256)
    nf = len(after)

    def body(x_ref, g_ref, *refs):
        o_refs = refs[nf:]
        xv = x_ref[...]
        r = lax.rsqrt(jnp.mean(xv * xv, axis=-1, keepdims=True) + RMS_EPS)
        xh = xv * r
        for gi, o_ref in enumerate(o_refs):
            o_ref[...] = (xh * g_ref[gi:gi + 1, :]).astype(o_ref.dtype)

    return pl.pallas_call(
        body,
        name=name,
        grid=(s // t,),
        in_specs=[pl.BlockSpec((t, d), lambda i: (i, 0)), pl.BlockSpec((ng, d), lambda i: (0, 0))] + [ANY_SPEC] * nf,
        out_specs=[pl.BlockSpec((t, d), lambda i: (i, 0)) for _ in range(ng)],
        out_shape=[jax.ShapeDtypeStruct((s, d), BF16) for _ in range(ng)],
        compiler_params=pltpu.CompilerParams(dimension_semantics=("parallel",)),
    )(x, gains, *after)


def _rms_bwd(name, x, gains, dns, dres):
    s, d = x.shape
    ng = gains.shape[0]
    t = _row_tile(s, 256)

    def body(x_ref, g_ref, dres_ref, *refs):
        dn_refs = refs[:ng]
        dx_ref, dxb_ref, dg_ref, cs_ref = refs[ng:]
        i = pl.program_id(0)
        xv = x_ref[...]
        r = lax.rsqrt(jnp.mean(xv * xv, axis=-1, keepdims=True) + RMS_EPS)
        xh = xv * r
        dx = dres_ref[...]
        dgs = []
        for gi in range(ng):
            dn = dn_refs[gi][...].astype(F32)
            dxh = dn * g_ref[gi:gi + 1, :]
            dgs.append(jnp.sum(dn * xh, axis=0, keepdims=True))
            dx = dx + r * (dxh - xh * jnp.mean(dxh * xh, axis=-1, keepdims=True))
        dx_ref[...] = dx
        dxb_ref[...] = dx.astype(BF16)
        dg = jnp.concatenate(dgs, axis=0) if ng > 1 else dgs[0]
        cs = jnp.sum(dx, axis=0, keepdims=True)

        @pl.when(i == 0)
        def _():
            dg_ref[...] = dg
            cs_ref[...] = cs

        @pl.when(i > 0)
        def _():
            dg_ref[...] += dg
            cs_ref[...] += cs

    row = pl.BlockSpec((t, d), lambda i: (i, 0))
    return pl.pallas_call(
        body,
        name=name,
        grid=(s // t,),
        in_specs=[row, pl.BlockSpec((ng, d), lambda i: (0, 0)), row] + [row] * ng,
        out_specs=[row, row, pl.BlockSpec((ng, d), lambda i: (0, 0)), pl.BlockSpec((1, d), lambda i: (0, 0))],
        out_shape=[
            jax.ShapeDtypeStruct((s, d), F32),
            jax.ShapeDtypeStruct((s, d), BF16),
            jax.ShapeDtypeStruct((ng, d), F32),
            jax.ShapeDtypeStruct((1, d), F32),
        ],
        compiler_params=pltpu.CompilerParams(dimension_semantics=("arbitrary",)),
    )(x, gains, dres, *dns)


def _final_loss(name, h, gain, target):
    s, d = h.shape
    t = _row_tile(s, 256)

    def body(h_ref, g_ref, t_ref, dh_ref, dhb_ref, dg_ref, ls_ref):
        i = pl.program_id(0)
        xv = h_ref[...]
        g = g_ref[...]
        r = lax.rsqrt(jnp.mean(xv * xv, axis=-1, keepdims=True) + RMS_EPS)
        xh = xv * r
        err = xh * g - t_ref[...]
        ls = jnp.sum(err * err, axis=0, keepdims=True) * (0.5 / d)
        dy = err * (1.0 / d)
        dxh = dy * g
        dg = jnp.sum(dy * xh, axis=0, keepdims=True)
        dx = r * (dxh - xh * jnp.mean(dxh * xh, axis=-1, keepdims=True))
        dh_ref[...] = dx
        dhb_ref[...] = dx.astype(BF16)

        @pl.when(i == 0)
        def _():
            dg_ref[...] = dg
            ls_ref[...] = ls

        @pl.when(i > 0)
        def _():
            dg_ref[...] += dg
            ls_ref[...] += ls

    row = pl.BlockSpec((t, d), lambda i: (i, 0))
    vec = pl.BlockSpec((1, d), lambda i: (0, 0))
    return pl.pallas_call(
        body,
        name=name,
        grid=(s // t,),
        in_specs=[row, vec, row],
        out_specs=[row, row, vec, vec],
        out_shape=[
            jax.ShapeDtypeStruct((s, d), F32),
            jax.ShapeDtypeStruct((s, d), BF16),
            jax.ShapeDtypeStruct((1, d), F32),
            jax.ShapeDtypeStruct((1, d), F32),
        ],
        compiler_params=pltpu.CompilerParams(dimension_semantics=("arbitrary",)),
    )(h, gain, target)


SUBLANES = 8
CONV_LANES = 512
CONV_ROWS = 32
NORM_ROWS = 16


def _conv_tiles(s):
    t = _row_tile(s, 128)
    assert t % CONV_HALO == 0 and t % CONV_ROWS == 0
    return t, t // CONV_HALO


def _shifted_copies(dst, src, lanes, rows):
    for m in range(SUBLANES):
        n = rows if m == 0 else rows - SUBLANES
        dst[m, :n, :] = src[m:m + n, lanes]


def _shifted(copies, offset, n):
    m = offset % SUBLANES
    return copies[m, offset - m:offset - m + n, :]


def _dwconv_fwd(name, u, dw, dw_b, ln_g, ln_b):
    s, d = u.shape
    t, hb = _conv_tiles(s)
    w = dw.shape[0]
    lo = CONV_HALO - (w - 1)

    lw = min(CONV_LANES, d)

    def body(cur_ref, prev_ref, dw_ref, dwb_ref, lg_ref, lb_ref, c_ref, sw_ref, cat_ref, sh_ref):
        i = pl.program_id(0)
        cat_ref[CONV_HALO:, :] = cur_ref[...]

        @pl.when(i == 0)
        def _():
            cat_ref[:CONV_HALO, :] = jnp.zeros((CONV_HALO, d), F32)

        @pl.when(i > 0)
        def _():
            cat_ref[:CONV_HALO, :] = prev_ref[...]

        for lc in range(d // lw):
            lanes = slice(lc * lw, (lc + 1) * lw)
            _shifted_copies(sh_ref, cat_ref, lanes, t + CONV_HALO)
            for rc in range(t // CONV_ROWS):
                acc = jnp.broadcast_to(dwb_ref[:, lanes], (CONV_ROWS, lw))
                for kk in range(w):
                    acc = acc + dw_ref[kk:kk + 1, lanes] * _shifted(sh_ref, lo + kk + rc * CONV_ROWS, CONV_ROWS)
                c_ref[rc * CONV_ROWS:(rc + 1) * CONV_ROWS, lanes] = acc

        def norm_rows(ri, carry):
            rows = pl.ds(pl.multiple_of(ri * NORM_ROWS, NORM_ROWS), NORM_ROWS)
            cv = c_ref[rows, :]
            cc = cv - jnp.mean(cv, axis=-1, keepdims=True)
            var = jnp.mean(cc * cc, axis=-1, keepdims=True)
            ln = cc * lax.rsqrt(var + LN_EPS) * lg_ref[...] + lb_ref[...]
            sw_ref[rows, :] = (ln * _sigmoid(ln)).astype(BF16)
            return carry

        lax.fori_loop(0, t // NORM_ROWS, norm_rows, 0)

    row = pl.BlockSpec((t, d), lambda i: (i, 0))
    prev = pl.BlockSpec((CONV_HALO, d), lambda i: (jnp.maximum(i * hb - 1, 0), 0))
    vec = pl.BlockSpec((1, d), lambda i: (0, 0))
    return pl.pallas_call(
        body,
        name=name,
        grid=(s // t,),
        in_specs=[row, prev, pl.BlockSpec((w, d), lambda i: (0, 0)), vec, vec, vec],
        out_specs=[row, row],
        out_shape=[jax.ShapeDtypeStruct((s, d), F32), jax.ShapeDtypeStruct((s, d), BF16)],
        scratch_shapes=[pltpu.VMEM((CONV_HALO + t, d), F32), pltpu.VMEM((SUBLANES, CONV_HALO + t, lw), F32)],
        compiler_params=pltpu.CompilerParams(dimension_semantics=("parallel",)),
    )(u, u, dw, dw_b, ln_g, ln_b)


def _conv_ln_bwd(name, c, dsw, ln_g, ln_b, after=()):
    s, d = c.shape
    t = _row_tile(s, 256)
    nf = len(after)

    def body(c_ref, dsw_ref, lg_ref, lb_ref, *refs):
        dc_ref, sums_ref = refs[nf:]
        i = pl.program_id(0)
        cv = c_ref[...]
        g = lg_ref[...]
        mu = jnp.mean(cv, axis=-1, keepdims=True)
        cc = cv - mu
        rstd = lax.rsqrt(jnp.mean(cc * cc, axis=-1, keepdims=True) + LN_EPS)
        ch = cc * rstd
        ln = ch * g + lb_ref[...]
        sg = _sigmoid(ln)
        dln = dsw_ref[...] * (sg * (1.0 + ln * (1.0 - sg)))
        dch = dln * g
        dc = rstd * (dch - jnp.mean(dch, axis=-1, keepdims=True) - ch * jnp.mean(dch * ch, axis=-1, keepdims=True))
        dc_ref[...] = dc
        sums = jnp.concatenate(
            [
                jnp.sum(dln * ch, axis=0, keepdims=True),
                jnp.sum(dln, axis=0, keepdims=True),
                jnp.sum(dc, axis=0, keepdims=True),
                jnp.zeros((1, d), F32),
            ],
            axis=0,
        )

        @pl.when(i == 0)
        def _():
            sums_ref[...] = sums

        @pl.when(i > 0)
        def _():
            sums_ref[...] += sums

    row = pl.BlockSpec((t, d), lambda i: (i, 0))
    vec = pl.BlockSpec((1, d), lambda i: (0, 0))
    return pl.pallas_call(
        body,
        name=name,
        grid=(s // t,),
        in_specs=[row, row, vec, vec] + [ANY_SPEC] * nf,
        out_specs=[row, pl.BlockSpec((4, d), lambda i: (0, 0))],
        out_shape=[jax.ShapeDtypeStruct((s, d), F32), jax.ShapeDtypeStruct((4, d), F32)],
        compiler_params=pltpu.CompilerParams(dimension_semantics=("arbitrary",)),
    )(c, dsw, ln_g, ln_b, *after)


def _conv_dw_bwd(name, dc, u, a, gt, dw):
    s, d = dc.shape
    t, hb = _conv_tiles(s)
    w = dw.shape[0]
    lo = CONV_HALO - (w - 1)
    nt = s // t
    lw = min(CONV_LANES, d)

    def body(dc_ref, dcn_ref, u_ref, up_ref, a_ref, gt_ref, dw_ref, dpre_ref, ddw_ref, db_ref, dcat_ref, ucat_ref,
             dsh_ref, ush_ref, ddw_acc, db_acc):
        i = pl.program_id(0)
        dcat_ref[:t, :] = dc_ref[...]
        ucat_ref[CONV_HALO:, :] = u_ref[...]

        @pl.when(i == nt - 1)
        def _():
            dcat_ref[t:, :] = jnp.zeros((CONV_HALO, d), F32)

        @pl.when(i < nt - 1)
        def _():
            dcat_ref[t:, :] = dcn_ref[...]

        @pl.when(i == 0)
        def _():
            ucat_ref[:CONV_HALO, :] = jnp.zeros((CONV_HALO, d), F32)

        @pl.when(i > 0)
        def _():
            ucat_ref[:CONV_HALO, :] = up_ref[...]

        @pl.when(i == 0)
        def _():
            ddw_acc[...] = jnp.zeros(ddw_acc.shape, F32)
            db_acc[...] = jnp.zeros(db_acc.shape, F32)

        def fold(p):
            return functools.reduce(jnp.add, [p[j:j + SUBLANES] for j in range(0, CONV_ROWS, SUBLANES)])

        for lc in range(d // lw):
            lanes = slice(lc * lw, (lc + 1) * lw)
            gate_lanes = slice(d + lc * lw, d + (lc + 1) * lw)
            _shifted_copies(dsh_ref, dcat_ref, lanes, t + CONV_HALO)
            _shifted_copies(ush_ref, ucat_ref, lanes, t + CONV_HALO)
            for rc in range(t // CONV_ROWS):
                r0 = rc * CONV_ROWS
                rows = slice(r0, r0 + CONV_ROWS)
                dcv = dc_ref[rows, lanes]
                du = jnp.zeros((CONV_ROWS, lw), F32)
                for kk in range(w):
                    du = du + dw_ref[kk:kk + 1, lanes] * _shifted(dsh_ref, w - 1 - kk + r0, CONV_ROWS)
                    ddw_acc[kk, :, lanes] += fold(dcv * _shifted(ush_ref, lo + kk + r0, CONV_ROWS))
                av = a_ref[rows, lanes].astype(F32)
                sg = _sigmoid(gt_ref[rows, lanes].astype(F32))
                da = du * sg
                dgt = du * av * sg * (1.0 - sg)
                dpre_ref[rows, lanes] = da.astype(BF16)
                dpre_ref[rows, gate_lanes] = dgt.astype(BF16)
                db_acc[:, lanes] += fold(da)
                db_acc[:, gate_lanes] += fold(dgt)

        @pl.when(i == nt - 1)
        def _():
            ddw_ref[...] = jnp.sum(ddw_acc[...], axis=1)
            db_ref[...] = jnp.sum(db_acc[...], axis=0, keepdims=True)

    row = pl.BlockSpec((t, d), lambda i: (i, 0))
    nxt = pl.BlockSpec((CONV_HALO, d), lambda i: (jnp.minimum((i + 1) * hb, s // CONV_HALO - 1), 0))
    prev = pl.BlockSpec((CONV_HALO, d), lambda i: (jnp.maximum(i * hb - 1, 0), 0))
    return pl.pallas_call(
        body,
        name=name,
        grid=(nt,),
        in_specs=[row, nxt, row, prev, row, row, pl.BlockSpec((w, d), lambda i: (0, 0))],
        out_specs=[
            pl.BlockSpec((t, 2 * d), lambda i: (i, 0)),
            pl.BlockSpec((w, d), lambda i: (0, 0)),
            pl.BlockSpec((1, 2 * d), lambda i: (0, 0)),
        ],
        out_shape=[
            jax.ShapeDtypeStruct((s, 2 * d), BF16),
            jax.ShapeDtypeStruct((w, d), F32),
            jax.ShapeDtypeStruct((1, 2 * d), F32),
        ],
        scratch_shapes=[pltpu.VMEM((t + CONV_HALO, d), F32), pltpu.VMEM((CONV_HALO + t, d), F32)]
        + [pltpu.VMEM((SUBLANES, CONV_HALO + t, lw), F32)] * 2
        + [pltpu.VMEM((w, SUBLANES, d), F32), pltpu.VMEM((SUBLANES, 2 * d), F32)],
        compiler_params=pltpu.CompilerParams(dimension_semantics=("arbitrary",)),
    )(dc, dc, u, u, a, gt, dw)


def _alibi_slopes(n_heads):
    h = jnp.arange(1, n_heads + 1, dtype=F32)
    return jnp.exp2(-8.0 * h / n_heads)


def _band_masks(bq):
    qi = lax.broadcasted_iota(jnp.int32, (bq, bq), 0)
    kj = lax.broadcasted_iota(jnp.int32, (bq, bq), 1)
    return qi - kj, qi - kj + bq


ATTN_GROUP = 16


def _attn_fwd(name, q, k, v, slopes, bq):
    s, dm = q.shape
    nh = dm // HEAD_DIM
    nt = s // bq
    nbr = len(BRANCHES)
    scale = HEAD_DIM ** -0.5
    nt_dims = (((1,), (1,)), ((), ()))

    def body(sl_ref, q_ref, k_ref, v_ref, o_ref, ob_ref, l_ref, tmp, qr, kr, va, orm, lrm, onat, lnat, sbuf, mbuf):
        slope = sl_ref[pl.program_id(0)]
        jc, jp = _band_masks(bq)
        va[:, HEAD_DIM:] = jnp.ones((s, HEAD_DIM), BF16)
        for bi, (win, dil) in enumerate(BRANCHES):
            ll = s // dil
            nblk = ll // bq
            if dil == 1:
                sq, sk = q_ref, k_ref
                va[:, :HEAD_DIM] = v_ref[...]
                d_o, d_l = onat.at[bi], lnat.at[bi]
            else:
                for src, dst, wide in ((q_ref, qr, False), (k_ref, kr, False), (v_ref, va, True)):
                    tmp[...] = src[...].astype(F32)
                    for r in range(dil):
                        part = tmp[pl.ds(r, ll, stride=dil), :].astype(BF16)
                        if wide:
                            dst[r * ll:(r + 1) * ll, :HEAD_DIM] = part
                        else:
                            dst[r * ll:(r + 1) * ll, :] = part
                sq, sk = qr, kr
                d_o, d_l = orm, lrm
            bias_c = jnp.where(jc >= 0, jc.astype(F32) * (slope * dil), 1e30)
            bias_p = jnp.where(jp <= bq, jp.astype(F32) * (slope * dil), 1e30)

            def group(gi, carry):
                rows = []
                for g in range(ATTN_GROUP):
                    ti = gi * ATTN_GROUP + g
                    row = pl.ds(pl.multiple_of(ti * bq, bq), bq)
                    prow = pl.ds(pl.multiple_of(jnp.maximum(ti - 1, 0) * bq, bq), bq)
                    rows.append((row, prow))
                    qh = sq[row, :]
                    sc = lax.dot_general(qh, sk[row, :], nt_dims, preferred_element_type=F32) * scale - bias_c
                    sp = lax.dot_general(qh, sk[prow, :], nt_dims, preferred_element_type=F32) * scale - bias_p
                    sp = jnp.where(lax.rem(ti, nblk) > 0, sp, -1e30)
                    sbuf[g, :, :bq] = sc
                    sbuf[g, :, bq:] = sp
                    mbuf[g] = jnp.maximum(jnp.max(sc, axis=-1, keepdims=True), jnp.max(sp, axis=-1, keepdims=True))
                for g, (row, prow) in enumerate(rows):
                    mx = mbuf[g]
                    p = jnp.exp(sbuf[g] - mx).astype(BF16)
                    ov = jnp.dot(p[:, :bq], va[row, :], preferred_element_type=F32)
                    ov = ov + jnp.dot(p[:, bq:], va[prow, :], preferred_element_type=F32)
                    den = ov[:, HEAD_DIM:]
                    d_o[row, :] = ov[:, :HEAD_DIM] / den
                    d_l[row, :] = mx + jnp.log(den)
                return carry

            lax.fori_loop(0, nt // ATTN_GROUP, group, 0)
            if dil > 1:
                for r in range(dil):
                    onat[bi, pl.ds(r, ll, stride=dil), :] = orm[r * ll:(r + 1) * ll, :]
                    lnat[bi, pl.ds(r, ll, stride=dil), :] = lrm[r * ll:(r + 1) * ll, :]

        def merge(ti, carry):
            rows = pl.ds(pl.multiple_of(ti * bq, bq), bq)
            ls = [lnat[bi, rows, :] for bi in range(nbr)]
            mx = functools.reduce(jnp.maximum, ls)
            es = [jnp.exp(l - mx) for l in ls]
            tot = functools.reduce(jnp.add, es)
            inv = 1.0 / tot
            o = functools.reduce(jnp.add, [e * inv * onat[bi, rows, :] for bi, e in enumerate(es)])
            o_ref[rows, :] = o
            ob_ref[rows, :] = o.astype(BF16)
            l_ref[rows, :] = mx + jnp.log(tot)
            return carry

        lax.fori_loop(0, nt, merge, 0)

    head = pl.BlockSpec((s, HEAD_DIM), lambda h: (0, h))
    return pl.pallas_call(
        body,
        name=name,
        grid=(nh,),
        in_specs=[pl.BlockSpec(memory_space=pltpu.SMEM), head, head, head],
        out_specs=[head, head, head],
        out_shape=[jax.ShapeDtypeStruct((s, dm), F32), jax.ShapeDtypeStruct((s, dm), BF16),
                   jax.ShapeDtypeStruct((s, dm), F32)],
        scratch_shapes=[pltpu.VMEM((s, HEAD_DIM), F32)] + [pltpu.VMEM((s, HEAD_DIM), BF16)] * 2
        + [pltpu.VMEM((s, 2 * HEAD_DIM), BF16)] + [pltpu.VMEM((s, HEAD_DIM), F32)] * 2
        + [pltpu.VMEM((nbr, s, HEAD_DIM), F32)] * 2
        + [pltpu.VMEM((ATTN_GROUP, bq, 2 * bq), F32), pltpu.VMEM((ATTN_GROUP, bq, 1), F32)],
        compiler_params=pltpu.CompilerParams(dimension_semantics=("parallel",)),
    )(slopes, q, k, v)


def _attn_bwd(name, q, k, v, o, lse, do, slopes, bq):
    s, dm = q.shape
    nh = dm // HEAD_DIM
    nt = s // bq
    scale = HEAD_DIM ** -0.5
    nt_dims = (((1,), (1,)), ((), ()))
    tn_dims = (((0,), (0,)), ((), ()))

    def body(sl_ref, q_ref, k_ref, v_ref, o_ref, l_ref, do_ref, dq_ref, dk_ref, dv_ref,
             tmp, qr, kr, vr, dor, lr, dlr, dln, dqr, dkr, dvr, aq, ak, av, pbuf, dsbuf):
        slope = sl_ref[pl.program_id(0)]
        jc, jp = _band_masks(bq)

        def delta(ti, carry):
            rows = pl.ds(pl.multiple_of(ti * bq, bq), bq)
            dl = jnp.sum(do_ref[rows, :].astype(F32) * o_ref[rows, :], axis=-1, keepdims=True)
            dln[rows, :] = jnp.broadcast_to(dl, (bq, HEAD_DIM))
            return carry

        lax.fori_loop(0, nt, delta, 0)

        for bi, (win, dil) in enumerate(BRANCHES):
            ll = s // dil
            nblk = ll // bq
            if dil == 1:
                sq, sk, sv, sdo, sl, sdl = q_ref, k_ref, v_ref, do_ref, l_ref, dln
                gq, gk, gv = aq, ak, av
            else:
                for src, dst in ((q_ref, qr), (k_ref, kr), (v_ref, vr), (do_ref, dor)):
                    tmp[...] = src[...].astype(F32)
                    for r in range(dil):
                        dst[r * ll:(r + 1) * ll, :] = tmp[pl.ds(r, ll, stride=dil), :].astype(BF16)
                for r in range(dil):
                    lr[r * ll:(r + 1) * ll, :] = l_ref[pl.ds(r, ll, stride=dil), :]
                    dlr[r * ll:(r + 1) * ll, :] = dln[pl.ds(r, ll, stride=dil), :]
                sq, sk, sv, sdo, sl, sdl = qr, kr, vr, dor, lr, dlr
                gq, gk, gv = dqr, dkr, dvr
            bias_c = jnp.where(jc >= 0, jc.astype(F32) * (slope * dil), 1e30)
            bias_p = jnp.where(jp <= bq, jp.astype(F32) * (slope * dil), 1e30)

            def group(gi, carry):
                rows = []
                for g in range(ATTN_GROUP):
                    ti = gi * ATTN_GROUP + g
                    row = pl.ds(pl.multiple_of(ti * bq, bq), bq)
                    prow = pl.ds(pl.multiple_of(jnp.maximum(ti - 1, 0) * bq, bq), bq)
                    rows.append((row, prow))
                    has_prev = lax.rem(ti, nblk) > 0
                    qh, doh = sq[row, :], sdo[row, :]
                    lc = sl[row, :][:, :1]
                    dl = sdl[row, :][:, :1]
                    for half, kv_rows, bias in ((0, row, bias_c), (1, prow, bias_p)):
                        sc = lax.dot_general(qh, sk[kv_rows, :], nt_dims, preferred_element_type=F32) * scale - bias
                        if half:
                            sc = jnp.where(has_prev, sc, -1e30)
                        p = jnp.exp(sc - lc)
                        dp = lax.dot_general(doh, sv[kv_rows, :], nt_dims, preferred_element_type=F32)
                        pbuf[g, :, half * bq:(half + 1) * bq] = p.astype(BF16)
                        dsbuf[g, :, half * bq:(half + 1) * bq] = (p * (dp - dl) * scale).astype(BF16)
                carry_k = carry_v = None
                for g, (row, prow) in enumerate(rows):
                    qh, doh = sq[row, :], sdo[row, :]
                    ds_c, ds_p = dsbuf[g, :, :bq], dsbuf[g, :, bq:]
                    p_c, p_p = pbuf[g, :, :bq], pbuf[g, :, bq:]
                    dq = jnp.dot(ds_c, sk[row, :], preferred_element_type=F32)
                    dq = dq + jnp.dot(ds_p, sk[prow, :], preferred_element_type=F32)
                    gq[row, :] = dq
                    dk_p = lax.dot_general(ds_p, qh, tn_dims, preferred_element_type=F32)
                    dv_p = lax.dot_general(p_p, doh, tn_dims, preferred_element_type=F32)
                    if g == 0:
                        @pl.when(gi > 0)
                        def _():
                            gk[prow, :] += dk_p
                            gv[prow, :] += dv_p
                    else:
                        gk[rows[g - 1][0], :] = carry_k + dk_p
                        gv[rows[g - 1][0], :] = carry_v + dv_p
                    carry_k = lax.dot_general(ds_c, qh, tn_dims, preferred_element_type=F32)
                    carry_v = lax.dot_general(p_c, doh, tn_dims, preferred_element_type=F32)
                gk[rows[-1][0], :] = carry_k
                gv[rows[-1][0], :] = carry_v
                return carry

            lax.fori_loop(0, nt // ATTN_GROUP, group, 0)
            if dil > 1:
                for acc, rm in ((aq, dqr), (ak, dkr), (av, dvr)):
                    for r in range(dil):
                        acc[pl.ds(r, ll, stride=dil), :] += rm[r * ll:(r + 1) * ll, :]

        dq_ref[...] = aq[...].astype(BF16)
        dk_ref[...] = ak[...].astype(BF16)
        dv_ref[...] = av[...].astype(BF16)

    head = pl.BlockSpec((s, HEAD_DIM), lambda h: (0, h))
    f32buf = pltpu.VMEM((s, HEAD_DIM), F32)
    b16buf = pltpu.VMEM((s, HEAD_DIM), BF16)
    return pl.pallas_call(
        body,
        name=name,
        grid=(nh,),
        in_specs=[pl.BlockSpec(memory_space=pltpu.SMEM)] + [head] * 6,
        out_specs=[head] * 3,
        out_shape=[jax.ShapeDtypeStruct((s, dm), BF16)] * 3,
        scratch_shapes=[f32buf] + [b16buf] * 4 + [f32buf] * 9 + [pltpu.VMEM((ATTN_GROUP, bq, 2 * bq), BF16)] * 2,
        compiler_params=pltpu.CompilerParams(dimension_semantics=("parallel",)),
    )(slopes, q, k, v, o, lse, do)


def _ep_id(accs, ex):
    return [accs[0]]


def _ep_all(accs, ex):
    return list(accs)


def _ep_sum(accs, ex):
    return [accs[0] + accs[1]]


def _ep_add(accs, ex):
    return [accs[0] + ex[0].astype(F32)]


def _ep_bias_res(accs, ex):
    return [accs[0] + ex[0] + ex[1]]


def _ep_glu(accs, ex):
    a = accs[0] + ex[0]
    gt = accs[1] + ex[1]
    return [a * _sigmoid(gt), a, gt]


def _ep_swiglu(accs, ex):
    g, u = accs
    return [g, u, g * _sigmoid(g) * u]


def _ep_swiglu_bwd(accs, ex):
    dact = accs[0]
    g = ex[0].astype(F32)
    u = ex[1].astype(F32)
    sg = _sigmoid(g)
    return [dact * u * (sg * (1.0 + g * (1.0 - sg))), dact * g * sg]


def _ffn_fwd(tag, h, gain, wg, wu, wd):
    s, d = h.shape
    f = wg.shape[-1]
    (n,) = _rms_fwd(f"{tag}_norm", h, gain)
    g, u, act = _mm(f"{tag}_gate_up", "nn", n, [(wg, None, 0), (wu, None, 0)], _ep_swiglu,
                    [(BF16,), (BF16,), (BF16,)], m=s, n=f, k=d)
    (out,) = _mm(f"{tag}_down", "nn", act, [(wd, None, 0)], _ep_add, [(F32,)], m=s, n=d, k=f,
                 extras=[(h, "mn", 0)], bn=256)
    return out, (n, g, u, act)


def _ffn_bwd(tag, h_in, gain, wg, wu, wd, saved, dh, dhb, after, emit):
    s, d = h_in.shape
    f = wg.shape[-1]
    n, g, u, act = saved
    dg, du = _mm(f"{tag}_bwd_dact", "nt", dhb, [(wd, None, 0)], _ep_swiglu_bwd, [(BF16,), (BF16,)],
                 m=s, n=f, k=d, extras=[(g, "mn", 0), (u, "mn", 0)], after=after)
    (dwd,) = _mm(f"{tag}_bwd_dwd", "tn", act, [(dhb, None, 0)], _ep_id, [(BF16,)], m=f, n=d, k=s,
                 bm=f // 4)
    dwg, dwu = _mm(f"{tag}_bwd_dwgu", "tn", n, [(dg, None, 0), (du, None, 0)], _ep_all, [(BF16,), (BF16,)],
                   m=d, n=f, k=s)
    pin = emit(dict(gate=dwg, up=dwu, down=dwd))
    (dn,) = _mm(f"{tag}_bwd_dn", "nt", [dg, du], [(wg, None, 0), (wu, None, 0)], _ep_sum, [(F32,)],
                m=s, n=d, k=f, bm=512, bn=256, after=pin)
    dx, dxb, dgain, cs = _rms_bwd(f"{tag}_bwd_norm", h_in, gain, [dn], dh)
    return dx, dxb, dgain, cs


def _local_step(x, target, w, fetch, emit, after=()):
    s, d = x.shape
    nh = d // HEAD_DIM
    bq = BRANCHES[0][0] // BRANCHES[0][1]
    assert all(win // dil == bq for win, dil in BRANCHES)
    assert BRANCHES[0][1] == 1 and all(dil > 1 for _, dil in BRANCHES[1:])
    slopes = _alibi_slopes(nh)
    nd = d // 512 if d >= 512 else 1
    bn = d // nd

    (n1,) = _rms_fwd("a_norm", x, w["a_norm_g"], after=after)
    w_conv1 = fetch("conv1", n1)["conv_w1"]
    glu, a, gt = _mm("conv_pw1_glu", "nn", n1, [(w_conv1, None, 0), (w_conv1, None, nd)], _ep_glu,
                     [(F32,), (BF16,), (BF16,)], m=s, n=d, k=d, bn=bn,
                     extras=[(w["conv_b1"], "n", 0), (w["conv_b1"], "n", nd)])
    c, sw = _dwconv_fwd("conv_dw_ln", glu, w["conv_dw"], w["conv_dw_b"], w["conv_ln_g"], w["conv_ln_b"])
    w_conv2 = fetch("conv2", sw)["conv_w2"]
    (h1,) = _mm("conv_pw2", "nn", sw, [(w_conv2, None, 0)], _ep_bias_res, [(F32,)], m=s, n=d, k=d,
                extras=[(w["conv_b2"], "n", 0), (x, "mn", 0)], bn=WIDE_BN)
    wf0 = fetch("ffn0", h1)
    h2, ffn0 = _ffn_fwd("ffn0", h1, w["ffn_norm_g"][0:1], wf0["gate"], wf0["up"], wf0["down"])
    wa = fetch("attn", h2)
    kvn, qn = _rms_fwd("kvq_norm", h2, jnp.concatenate([w["kv_norm_g"], w["b_norm_g"]], axis=0))
    k, v = _mm("kv_proj", "nn", kvn, [(wa["w_k"], None, 0), (wa["w_v"], None, 0)], _ep_all, [(BF16,), (BF16,)],
               m=s, n=d, k=d, bn=WIDE_BN)
    (q,) = _mm("q_proj", "nn", qn, [(wa["w_q"], None, 0)], _ep_id, [(BF16,)], m=s, n=d, k=d, bn=WIDE_BN)
    att, attb, lse = _attn_fwd("attn_fwd", q, k, v, slopes, bq)
    (h3,) = _mm("o_proj", "nn", attb, [(wa["w_o"], None, 0)], _ep_add, [(F32,)], m=s, n=d, k=d,
                extras=[(h2, "mn", 0)], bn=WIDE_BN)
    wf1 = fetch("ffn1", h3)
    h4, ffn1 = _ffn_fwd("ffn1", h3, w["ffn_norm_g"][1:2], wf1["gate"], wf1["up"], wf1["down"])
    dh4, dh4b, d_final_g, loss_cols = _final_loss("final_loss", h4, w["final_norm_g"], target)

    g = {}
    ga = {}
    dh3, dh3b, dgain1, _ = _ffn_bwd("ffn1", h3, w["ffn_norm_g"][1:2], wf1["gate"], wf1["up"], wf1["down"], ffn1,
                                    dh4, dh4b, (), functools.partial(emit, "ffn1"))
    (datt,) = _mm("o_proj_bwd_dx", "nt", dh3b, [(wa["w_o"], None, 0)], _ep_id, [(BF16,)], m=s, n=d, k=d, bn=WIDE_BN)
    (ga["w_o"],) = _mm("o_proj_bwd_dw", "tn", attb, [(dh3b, None, 0)], _ep_id, [(BF16,)], m=d, n=d, k=s, bn=WIDE_BN)
    dq, dk, dv = _attn_bwd("attn_bwd", q, k, v, att, lse, datt, slopes, bq)
    (ga["w_q"],) = _mm("q_proj_bwd_dw", "tn", qn, [(dq, None, 0)], _ep_id, [(BF16,)], m=d, n=d, k=s, bn=WIDE_BN)
    ga["w_k"], ga["w_v"] = _mm("kv_proj_bwd_dw", "tn", kvn, [(dk, None, 0), (dv, None, 0)], _ep_all,
                               [(BF16,), (BF16,)], m=d, n=d, k=s)
    pin = emit("attn", ga)
    (dqn,) = _mm("q_proj_bwd_dx", "nt", dq, [(wa["w_q"], None, 0)], _ep_id, [(F32,)], m=s, n=d, k=d, after=pin,
                 bn=WIDE_BN)
    (dkvn,) = _mm("kv_proj_bwd_dx", "nt", [dk, dv], [(wa["w_k"], None, 0), (wa["w_v"], None, 0)], _ep_sum, [(F32,)],
                  m=s, n=d, k=d, bn=WIDE_BN)
    dh2, dh2b, dg_kvq, _ = _rms_bwd("kvq_norm_bwd", h2, jnp.concatenate([w["kv_norm_g"], w["b_norm_g"]], axis=0),
                                    [dkvn, dqn], dh3)
    dh1, dh1b, dgain0, cs_h1 = _ffn_bwd("ffn0", h1, w["ffn_norm_g"][0:1], wf0["gate"], wf0["up"], wf0["down"], ffn0,
                                        dh2, dh2b, (), functools.partial(emit, "ffn0"))
    (dsw,) = _mm("conv_pw2_bwd_dx", "nt", dh1b, [(w_conv2, None, 0)], _ep_id, [(F32,)], m=s, n=d, k=d, bn=WIDE_BN)
    (dw2,) = _mm("conv_pw2_bwd_dw", "tn", sw, [(dh1b, None, 0)], _ep_id, [(BF16,)], m=d, n=d, k=s, bn=WIDE_BN)
    pin = emit("conv2", dict(conv_w2=dw2))
    dc, ln_sums = _conv_ln_bwd("conv_ln_bwd", c, dsw, w["conv_ln_g"], w["conv_ln_b"], after=pin)
    dpre, ddw, db1 = _conv_dw_bwd("conv_dw_bwd", dc, glu, a, gt, w["conv_dw"])
    (dw1,) = _mm("conv_pw1_bwd_dw", "tn", n1, [(dpre, None, 0)], _ep_id, [(BF16,)], m=d, n=2 * d, k=s)
    pin = emit("conv1", dict(conv_w1=dw1))
    (dn1,) = _mm("conv_pw1_bwd_dx", "nt", dpre, [(w_conv1, None, 0)], _ep_id, [(F32,)], m=s, n=d, k=2 * d,
                 after=pin)
    dx, _, d_a_norm, _ = _rms_bwd("a_norm_bwd", x, w["a_norm_g"], [dn1], dh1)

    g.update(
        a_norm_g=d_a_norm, conv_b1=db1, conv_dw=ddw, conv_dw_b=ln_sums[2:3], conv_ln_g=ln_sums[0:1],
        conv_ln_b=ln_sums[1:2], conv_b2=cs_h1, kv_norm_g=dg_kvq[0:1], b_norm_g=dg_kvq[1:2],
        ffn_norm_g=jnp.concatenate([dgain0, dgain1], axis=0), final_norm_g=d_final_g,
    )
    return loss_cols, dx, g


HBM_SPEC = pl.BlockSpec(memory_space=pltpu.HBM)


def _mesh_place():
    x, y, c = lax.axis_index("x"), lax.axis_index("y"), lax.axis_index("c")
    chips = [(1 - x, y), (x, 1 - y), (1 - x, 1 - y)]
    return x, y, c, chips


def _shard_view(ref, kind, s, half=None):
    rows, cols = ref.shape
    if kind == "col":
        cw = cols // N_CHIPS
        if half is None:
            return ref.at[pl.ds(0, rows), pl.ds(s * cw, cw)]
        return ref.at[pl.ds(half * (rows // 2), rows // 2), pl.ds(s * cw, cw)]
    r = rows // N_CHIPS
    if half is None:
        return ref.at[pl.ds(s * r, r), pl.ds(0, cols)]
    return ref.at[pl.ds(s * r + half * (r // 2), r // 2), pl.ds(0, cols)]


def _full_shape(shard, kind):
    r, cw = shard.shape[-2:]
    return (r, cw * N_CHIPS) if kind == "col" else (r * N_CHIPS, cw)


def _gather_weights(shards, kinds, after=()):
    nt = len(shards)
    nf = len(after)
    fulls = [jax.ShapeDtypeStruct(_full_shape(sh, kind), sh.dtype) for sh, kind in zip(shards, kinds)]

    def body(*refs):
        src = refs[:nt]
        dst = refs[nt + nf:2 * nt + nf]
        send, recv, fsend, frecv, local = refs[2 * nt + nf:]
        x, y, c, chips = _mesh_place()
        s = 2 * x + y
        sib = (x, y, 1 - c)

        def half_of_shard(t):
            r, cw = src[t].shape
            return src[t].at[pl.ds(c * (r // 2), r // 2), pl.ds(0, cw)]

        locals_ = [pltpu.make_async_copy(src[t], _shard_view(dst[t], kinds[t], s), local.at[t]) for t in range(nt)]
        for cp in locals_:
            cp.start()
        sends = []
        for t in range(nt):
            for j, chip in enumerate(chips):
                cp = pltpu.make_async_remote_copy(
                    src_ref=half_of_shard(t), dst_ref=_shard_view(dst[t], kinds[t], s, c),
                    send_sem=send.at[t, j], recv_sem=recv.at[t, j], device_id=(*chip, c), device_id_type=MESH)
                cp.start()
                sends.append(cp)
        for t in range(nt):
            for j, (px, py) in enumerate(chips):
                landed = _shard_view(dst[t], kinds[t], 2 * px + py, c)
                pltpu.make_async_remote_copy(
                    src_ref=half_of_shard(t), dst_ref=landed, send_sem=send.at[t, j], recv_sem=recv.at[t, j],
                    device_id=(px, py, c), device_id_type=MESH).wait_recv()
                cp = pltpu.make_async_remote_copy(
                    src_ref=landed, dst_ref=landed, send_sem=fsend.at[t, j], recv_sem=frecv.at[t, j],
                    device_id=sib, device_id_type=MESH)
                cp.start()
                sends.append(cp)
        for t in range(nt):
            for j, (px, py) in enumerate(chips):
                other = _shard_view(dst[t], kinds[t], 2 * px + py, 1 - c)
                pltpu.make_async_remote_copy(
                    src_ref=other, dst_ref=other, send_sem=fsend.at[t, j], recv_sem=frecv.at[t, j],
                    device_id=sib, device_id_type=MESH).wait_recv()
        for cp in sends:
            cp.wait_send()
        for cp in locals_:
            cp.wait()

    return pl.pallas_call(
        body,
        name="gather_weights",
        in_specs=[HBM_SPEC] * nt + [ANY_SPEC] * nf,
        out_specs=[HBM_SPEC] * nt,
        out_shape=fulls,
        scratch_shapes=[pltpu.SemaphoreType.DMA((nt, 3))] * 4 + [pltpu.SemaphoreType.DMA((nt,))],
    )(*shards, *after)


def _row_blocks(rows, want=512):
    nb = 1
    while rows // nb > want or rows % nb or (rows // nb) % 16:
        nb += 1
        if nb > rows:
            return rows, 1
    return rows // nb, nb


def _place_own(name, src, kind, land, ids, into_slot, lead=0):
    if into_slot:
        _, r, cw = land.shape
    else:
        r, cw = src.shape[-2:]
    tr, nb = _row_blocks(r)
    if not into_slot:
        if src.ndim == 3:
            src_spec = pl.BlockSpec((None, tr, cw), lambda i, ids_ref: (lead, i, 0))
        else:
            src_spec = pl.BlockSpec((tr, cw), lambda i, ids_ref: (i, 0))
        if kind == "col":
            dst_spec = pl.BlockSpec((tr, cw), lambda i, ids_ref: (i, ids_ref[0]))
        else:
            dst_spec = pl.BlockSpec((tr, cw), lambda i, ids_ref: (ids_ref[0] * nb + i, 0))
    else:
        if kind == "col":
            src_spec = pl.BlockSpec((tr, cw), lambda i, ids_ref: (i, ids_ref[0]))
        elif kind == "row":
            src_spec = pl.BlockSpec((tr, cw), lambda i, ids_ref: (ids_ref[0] * nb + i, 0))
        else:
            src_spec = pl.BlockSpec((tr, cw), lambda i, ids_ref: (i, 0))
        dst_spec = pl.BlockSpec((None, tr, cw), lambda i, ids_ref: (ids_ref[1], i, 0))

    def body(ids_ref, s_ref, o_ref):
        o_ref[...] = s_ref[...].astype(o_ref.dtype)

    return pl.pallas_call(
        body,
        name=name,
        grid_spec=pltpu.PrefetchScalarGridSpec(num_scalar_prefetch=1, grid=(nb,), in_specs=[src_spec],
                                               out_specs=dst_spec),
        out_shape=land,
        compiler_params=pltpu.CompilerParams(dimension_semantics=("parallel",)),
    )(ids, src)


SEM_SPEC = pl.BlockSpec(memory_space=pltpu.SEMAPHORE)
SIDE_EFFECT = pltpu.SideEffectType.DATAFLOW_SIDE_EFFECTING


def _copies_start(name, srcs, lands, copies, n_sems, after):
    ns, nl, nf = len(srcs), len(lands), len(after)

    def body(*refs):
        src, land = refs[:ns], refs[ns:ns + nl]
        send, recv = refs[ns + nl + nf], refs[ns + nl + nf + 1]
        pin = refs[-1]
        for cp in copies(src, land, send, recv, _mesh_place()):
            cp.start()
        pin[...] = jnp.zeros_like(pin)

    arrs = list(srcs) + list(lands)
    res = pl.pallas_call(
        body,
        name=name,
        in_specs=[HBM_SPEC] * (ns + nl) + [ANY_SPEC] * nf,
        out_specs=[SEM_SPEC, SEM_SPEC] + [HBM_SPEC] * (ns + nl) + [pl.BlockSpec(memory_space=pltpu.VMEM)],
        out_shape=[pltpu.SemaphoreType.DMA((n_sems,)), pltpu.SemaphoreType.DMA((n_sems,))]
        + [pltpu.HBM(a.shape, a.dtype) for a in arrs] + [jax.ShapeDtypeStruct((8, 128), F32)],
        input_output_aliases={i: 2 + i for i in range(ns + nl)},
        compiler_params=pltpu.CompilerParams(has_side_effects=SIDE_EFFECT),
    )(*[pltpu.with_memory_space_constraint(a, pltpu.HBM) for a in arrs], *after)
    return res[0], res[1], list(res[2:2 + ns]), list(res[2 + ns:2 + ns + nl]), res[-1]


def _copies_wait(name, send, recv, srcs, lands, copies, after):
    ns, nl, nf = len(srcs), len(lands), len(after)

    def body(*refs):
        src, land = refs[:ns], refs[ns:ns + nl]
        send_sems, recv_sems = refs[ns + nl], refs[ns + nl + 1]
        cps = copies(src, land, send_sems, recv_sems, _mesh_place())
        for cp in cps:
            cp.wait_send()
        for cp in cps:
            cp.wait_recv()

    arrs = list(srcs) + list(lands)
    res = pl.pallas_call(
        body,
        name=name,
        in_specs=[HBM_SPEC] * (ns + nl) + [SEM_SPEC, SEM_SPEC] + [ANY_SPEC] * nf,
        out_specs=[HBM_SPEC] * (ns + nl),
        out_shape=[pltpu.HBM(a.shape, a.dtype) for a in arrs],
        input_output_aliases={i: i for i in range(ns + nl)},
        compiler_params=pltpu.CompilerParams(has_side_effects=SIDE_EFFECT),
    )(*arrs, send, recv, *after)
    return list(res[:ns]), list(res[ns:])


def _gather_copies(kinds, halves):
    def copies(src, land, send, recv, place):
        x, y, c, chips = place
        s = 2 * x + y
        mine = [_shard_view(land[t], kinds[t], s, c if halves else None) for t in range(len(kinds))]
        return [
            pltpu.make_async_remote_copy(
                src_ref=mine[t], dst_ref=mine[t], send_sem=send.at[3 * t + j], recv_sem=recv.at[3 * t + j],
                device_id=(px, py, c), device_id_type=MESH)
            for t in range(len(kinds)) for j, (px, py) in enumerate(chips)
        ]

    return copies


def _gather_pass_on(name, lands, kinds):
    nt = len(lands)

    def body(*refs):
        buf = refs[nt:2 * nt]
        send, recv = refs[2 * nt:]
        x, y, c, chips = _mesh_place()
        sib = (x, y, 1 - c)
        sends = []
        for t in range(nt):
            for j, (px, py) in enumerate(chips):
                mine = _shard_view(buf[t], kinds[t], 2 * px + py, c)
                cp = pltpu.make_async_remote_copy(src_ref=mine, dst_ref=mine, send_sem=send.at[t, j],
                                                  recv_sem=recv.at[t, j], device_id=sib, device_id_type=MESH)
                cp.start()
                sends.append(cp)
        for t in range(nt):
            for j, (px, py) in enumerate(chips):
                theirs = _shard_view(buf[t], kinds[t], 2 * px + py, 1 - c)
                pltpu.make_async_remote_copy(src_ref=theirs, dst_ref=theirs, send_sem=send.at[t, j],
                                             recv_sem=recv.at[t, j], device_id=sib, device_id_type=MESH).wait_recv()
        for cp in sends:
            cp.wait_send()

    return pl.pallas_call(
        body,
        name=name,
        in_specs=[HBM_SPEC] * nt,
        out_specs=[HBM_SPEC] * nt,
        out_shape=[jax.ShapeDtypeStruct(a.shape, a.dtype) for a in lands],
        input_output_aliases={i: i for i in range(nt)},
        scratch_shapes=[pltpu.SemaphoreType.DMA((nt, 3))] * 2,
    )(*lands)


def _grad_part(ref, kind, s):
    return ref if kind == "all" else _shard_view(ref, kind, s)


def _grad_copies(kinds):
    def peers(place):
        x, y, c, chips = place
        return [(x, y, 1 - c)] + [(px, py, c) for px, py in chips]

    def copies(src, land, send, recv, place):
        x, y, c, chips = place
        me = 4 * x + 2 * y + c
        return [
            pltpu.make_async_remote_copy(
                src_ref=_grad_part(src[t], kinds[t], 2 * px + py), dst_ref=land[t].at[me],
                send_sem=send.at[GRAD_PEERS * t + k], recv_sem=recv.at[GRAD_PEERS * t + k], device_id=(px, py, pc),
                device_id_type=MESH)
            for t in range(len(kinds)) for k, (px, py, pc) in enumerate(peers(place))
        ]

    return copies


def _pass_copies(n):
    def copies(src, land, send, recv, place):
        x, y, c, chips = place
        return [
            pltpu.make_async_remote_copy(
                src_ref=land[t].at[4 * px + 2 * py + c], dst_ref=land[t].at[4 * px + 2 * py + c],
                send_sem=send.at[3 * t + j], recv_sem=recv.at[3 * t + j], device_id=(x, y, 1 - c),
                device_id_type=MESH)
            for t in range(n) for j, (px, py) in enumerate(chips)
        ]

    return copies


GRAD_PEERS = 4


def _land_shape(grad, kind):
    rows, cols = grad.shape
    if kind == "col":
        return (N_DEV, rows, cols // N_CHIPS)
    if kind == "row":
        return (N_DEV, rows // N_CHIPS, cols)
    return (N_DEV, rows, cols)


def _adamw_reduce(name, contrib, own, kind, ids, w, m, v, layer=None, prev=None):
    rows, cols = w.shape[-2:]
    t = 128 if rows % 128 == 0 else rows
    nb = rows // t
    c1 = 1.0 - ADAM_B1 ** ADAM_STEP
    c2 = 1.0 - ADAM_B2 ** ADAM_STEP

    n_prev = 0 if prev is None else 4

    def body(ids_ref, c_ref, own_ref, w_ref, m_ref, v_ref, *refs):
        g_ref, d_ref, nm_ref, nv_ref = refs[n_prev:]
        me = ids_ref[1]
        mine = own_ref[...].astype(F32)
        g = None
        for q in range(N_DEV):
            term = jnp.where(me == q, mine, c_ref[q].astype(F32))
            g = term if g is None else g + term
        nm = ADAM_B1 * m_ref[...] + (1.0 - ADAM_B1) * g
        nv = ADAM_B2 * v_ref[...] + (1.0 - ADAM_B2) * (g * g)
        g_ref[...] = g
        nm_ref[...] = nm
        nv_ref[...] = nv
        d_ref[...] = -ADAM_LR * ((nm / c1) / (jnp.sqrt(nv / c2) + ADAM_EPS) + ADAM_WD * w_ref[...])

    if layer is None:
        blk = pl.BlockSpec((t, cols), lambda i, ids_ref: (i, 0))
    else:
        blk = pl.BlockSpec((None, t, cols), lambda i, ids_ref: (layer, i, 0))
    if kind == "col":
        own_spec = pl.BlockSpec((t, cols), lambda i, ids_ref: (i, ids_ref[0]))
    elif kind == "row":
        own_spec = pl.BlockSpec((t, cols), lambda i, ids_ref: (ids_ref[0] * nb + i, 0))
    else:
        own_spec = pl.BlockSpec((t, cols), lambda i, ids_ref: (i, 0))
    return pl.pallas_call(
        body,
        name=name,
        grid_spec=pltpu.PrefetchScalarGridSpec(
            num_scalar_prefetch=1,
            grid=(nb,),
            in_specs=[pl.BlockSpec((N_DEV, t, cols), lambda i, ids_ref: (0, i, 0)), own_spec, blk, blk, blk]
            + [ANY_SPEC] * n_prev,
            out_specs=[blk] * 4,
        ),
        out_shape=[jax.ShapeDtypeStruct(w.shape, F32)] * 4,
        input_output_aliases={6 + i: i for i in range(n_prev)},
        compiler_params=pltpu.CompilerParams(dimension_semantics=("parallel",)),
    )(ids, contrib, own, w, m, v, *(prev or ()))


WEIGHT_NAMES = ("a_norm_g", "conv_w1", "conv_b1", "conv_dw", "conv_dw_b", "conv_ln_g", "conv_ln_b", "conv_w2",
                "conv_b2", "kv_norm_g", "w_k", "w_v", "b_norm_g", "w_q", "w_o", "ffn_norm_g", "ffn_w_gate",
                "ffn_w_up", "ffn_w_down", "final_norm_g")
GROUPS = {
    "conv2": (("conv_w2", "conv_w2", None, "row"),),
    "ffn0": (("gate", "ffn_w_gate", 0, "col"), ("up", "ffn_w_up", 0, "col"), ("down", "ffn_w_down", 0, "row")),
    "attn": (("w_k", "w_k", None, "row"), ("w_v", "w_v", None, "row"), ("w_q", "w_q", None, "row"),
             ("w_o", "w_o", None, "row")),
    "ffn1": (("gate", "ffn_w_gate", 1, "col"), ("up", "ffn_w_up", 1, "col"), ("down", "ffn_w_down", 1, "row")),
}
GROUPS["conv1"] = (("conv_w1", "conv_w1", None, "col"),)
FETCH_ORDER = ("conv1", "conv2", "ffn0", "attn", "ffn1")
HALVED = ("conv1", "conv2", "ffn0")
EMIT_ORDER = ("ffn1", "attn", "ffn0", "conv2", "conv1", "vec")
RETIRE_AT = {"attn": ("ffn1",), "ffn0": ("attn",), "conv1": ("ffn0", "conv2")}
PACKED = (("a_norm_g", 0, 1), ("conv_b1", 8, 2), ("conv_dw", 16, CONV_WIDTH), ("conv_dw_b", 48, 1),
          ("conv_ln_g", 56, 1), ("conv_ln_b", 64, 1), ("conv_b2", 72, 1))
PACK_ROWS = 80
WHOLE = (("kv_norm_g", 0, 1), ("b_norm_g", 1, 1), ("ffn_norm_g", 2, 2), ("final_norm_g", 4, 1))
WHOLE_ROWS = 8


def _pack_rows(parts, total, width):
    out, at = [], 0
    for arr, first in parts:
        if first > at:
            out.append(jnp.zeros((first - at, width), F32))
        rows8 = -(-arr.shape[0] // 8) * 8
        out.append(jnp.pad(arr, ((0, rows8 - arr.shape[0]), (0, 0))))
        at = first + rows8
    if total > at:
        out.append(jnp.zeros((total - at, width), F32))
    return jnp.concatenate(out, axis=0)


def kernel(x, a_norm_g, conv_w1, conv_b1, conv_dw, conv_dw_b, conv_ln_g, conv_ln_b, conv_w2, conv_b2, kv_norm_g, w_k, w_v, b_norm_g, w_q, w_o, ffn_norm_g, ffn_w_gate, ffn_w_up, ffn_w_down, final_norm_g, loss_target, m_a_norm_g, m_conv_w1, m_conv_b1, m_conv_dw, m_conv_dw_b, m_conv_ln_g, m_conv_ln_b, m_conv_w2, m_conv_b2, m_kv_norm_g, m_w_k, m_w_v, m_b_norm_g, m_w_q, m_w_o, m_ffn_norm_g, m_ffn_w_gate, m_ffn_w_up, m_ffn_w_down, m_final_norm_g, v_a_norm_g, v_conv_w1, v_conv_b1, v_conv_dw, v_conv_dw_b, v_conv_ln_g, v_conv_ln_b, v_conv_w2, v_conv_b2, v_kv_norm_g, v_w_k, v_w_v, v_b_norm_g, v_w_q, v_w_o, v_ffn_norm_g, v_ffn_w_gate, v_ffn_w_up, v_ffn_w_down, v_final_norm_g):
    args = locals()
    wts = {n: args[n] for n in WEIGHT_NAMES}
    mom = {n: args["m_" + n] for n in WEIGHT_NAMES}
    vel = {n: args["v_" + n] for n in WEIGHT_NAMES}
    s, d = x.shape[-2:]
    dq = d // N_CHIPS
    x2 = x.reshape(s, d)
    tgt = loss_target.reshape(s, d)

    def pack_shard(src):
        return _pack_rows([(src[n].reshape(-1, dq), first) for n, first, _ in PACKED], PACK_ROWS, dq)

    def pack_whole(src):
        return _pack_rows([(jnp.concatenate([src[n].reshape(-1, d) for n, _, _ in WHOLE], axis=0), 0)], WHOLE_ROWS, d)

    gathers = {}
    pins = []
    ids = jnp.stack([2 * lax.axis_index("x") + lax.axis_index("y"),
                     4 * lax.axis_index("x") + 2 * lax.axis_index("y") + lax.axis_index("c")]).astype(jnp.int32)

    def start_gather(grp, pins):
        kinds = [kind for _, _, _, kind in GROUPS[grp]]
        lands = []
        for key, n, layer, kind in GROUPS[grp]:
            shard = wts[n] if layer is not None else wts[n].reshape(wts[n].shape[-2:])
            full = jax.ShapeDtypeStruct(_full_shape(shard, kind), BF16)
            lands.append(_place_own(f"gather_place_{grp}_{key}", shard, kind, full, ids, False, layer))
        copies = _gather_copies(kinds, grp in HALVED)
        send, recv, _, lands, pin = _copies_start("gather_start_" + grp, [], lands, copies, 3 * len(lands), pins)
        gathers[grp] = (send, recv, [], lands, copies, kinds)
        return [pin]

    (packed_full,) = _gather_weights([pack_shard(wts)], ["row"])
    w = {}
    pins = [packed_full]
    for grp in FETCH_ORDER:
        pins = start_gather(grp, pins)

    def fetch(grp, after):
        send, recv, srcs, lands, copies, kinds = gathers[grp]
        _, fulls = _copies_wait("gather_wait_" + grp, send, recv, srcs, lands, copies, [after])
        if grp in HALVED:
            fulls = _gather_pass_on("gather_pass_" + grp, fulls, kinds)
        return {key: full for (key, _, _, _), full in zip(GROUPS[grp], fulls)}

    packed = packed_full.reshape(N_CHIPS, PACK_ROWS, dq)
    for n, first, rows in PACKED:
        part = packed[:, first:first + rows, :]
        if n == "conv_dw":
            w[n] = part.transpose(1, 0, 2).reshape(rows, d)
        else:
            w[n] = part.reshape(1, N_CHIPS * rows * dq)
    for n, _, rows in WHOLE:
        w[n] = wts[n].reshape(rows, d)

    exchanges = {}
    passing = {}
    own_grads = {}

    def retire(tag, after):
        send, recv, srcs, lands, copies, kinds = exchanges[tag]
        srcs, lands = _copies_wait("grads_wait_" + tag, send, recv, srcs, lands, copies, after)
        own_grads[tag] = list(zip(srcs, kinds))
        copies = _pass_copies(len(lands))
        send, recv, _, lands, pin = _copies_start("grads_pass_" + tag, [], lands, copies, 3 * len(lands), ())
        passing[tag] = (send, recv, lands, copies)
        return pin

    def emit_grads(tag, grads, kinds):
        pins = [retire(old, [grads[0]]) for old in RETIRE_AT.get(tag, ())]
        copies = _grad_copies(kinds)
        lands = [lax.empty(_land_shape(gr, kd), gr.dtype) for gr, kd in zip(grads, kinds)]
        send, recv, srcs, lands, pin = _copies_start("grads_start_" + tag, grads, lands, copies,
                                                     GRAD_PEERS * len(grads), pins)
        exchanges[tag] = (send, recv, srcs, lands, copies, kinds)
        return [pin]

    def emit(grp, grads):
        return emit_grads(grp, [grads[key] for key, _, _, _ in GROUPS[grp]], [kind for _, _, _, kind in GROUPS[grp]])

    loss_cols, dx, g = _local_step(x2, tgt, w, fetch, emit, after=pins)
    loss = lax.psum(jnp.sum(loss_cols), ("x", "y", "c"))

    gp = []
    for n, first, rows in PACKED:
        if n == "conv_dw":
            part = g[n].reshape(rows, N_CHIPS, dq).transpose(1, 0, 2)
        else:
            part = g[n].reshape(N_CHIPS, rows, dq)
        gp.append((part, first))
    g_packed = jnp.concatenate(
        [_pack_rows([(p[ci], first) for p, first in gp], PACK_ROWS, dq) for ci in range(N_CHIPS)], axis=0)
    emit_grads("vec", [g_packed, pack_whole(g)], ["row", "all"])
    for tag in EMIT_ORDER:
        if tag not in passing:
            retire(tag, [dx])
    contribs = {}
    for tag in EMIT_ORDER:
        send, recv, lands, copies = passing[tag]
        _, arrived = _copies_wait("grads_passed_" + tag, send, recv, [], lands, copies, [dx])
        contribs[tag] = [(c, own, kind) for c, (own, kind) in zip(arrived, own_grads[tag])]

    res = {}

    def adamw(n, contrib, layer=None, prev=None):
        arrived, own, kind = contrib
        if layer is None:
            shape = wts[n].shape
            r2 = shape[-2:]
            outs = _adamw_reduce("adamw_" + n, arrived, own, kind, ids, wts[n].reshape(r2), mom[n].reshape(r2),
                                 vel[n].reshape(r2))
            return [o.reshape(shape) for o in outs]
        return _adamw_reduce(f"adamw_{n}_{layer}", arrived, own, kind, ids, wts[n], mom[n], vel[n], layer, prev)

    for grp in ("attn", "conv2", "conv1"):
        for (key, n, _, _), contrib in zip(GROUPS[grp], contribs[grp]):
            res[n] = adamw(n, contrib)
    for (key, n, _, _), c0, c1 in zip(GROUPS["ffn0"], contribs["ffn0"], contribs["ffn1"]):
        res[n] = adamw(n, c1, 1, adamw(n, c0, 0))
    outs = _adamw_reduce("adamw_packed", *contribs["vec"][0], ids, pack_shard(wts), pack_shard(mom), pack_shard(vel))
    for n, first, rows in PACKED:
        res[n] = [o[first:first + rows].reshape(wts[n].shape) for o in outs]
    outs = _adamw_reduce("adamw_whole", *contribs["vec"][1], ids, pack_whole(wts), pack_whole(mom), pack_whole(vel))
    for n, first, rows in WHOLE:
        res[n] = [o[first:first + rows].reshape(wts[n].shape) for o in outs]

    out = [loss, dx.reshape(x.shape)]
    for which in range(4):
        out += [res[n][which] for n in WEIGHT_NAMES]
    return tuple(out)
```

```python
import functools
import math

import jax
import jax.numpy as jnp
from jax import lax
from jax.experimental import pallas as pl
from jax.experimental.pallas import tpu as pltpu

F32 = jnp.float32
BF16 = jnp.bfloat16

HEAD_DIM = 128
BRANCHES = ((128, 1), (512, 4), (2048, 16))
CONV_WIDTH = 31
CONV_HALO = 32
RMS_EPS = 1e-6
LN_EPS = 1e-5
ADAM_LR = 0.001
ADAM_B1 = 0.9
ADAM_B2 = 0.999
ADAM_EPS = 1e-08
ADAM_WD = 0.01
ADAM_STEP = 10
N_CHIPS = 4
N_DEV = 8
MESH = pl.DeviceIdType.MESH


def _sigmoid(x):
    return 1.0 / (1.0 + jnp.exp(-x))


def _row_tile(rows, want):
    t = min(rows, want)
    assert rows % t == 0, (rows, want)
    return t


_DOT_DIMS = {"nn": ((1,), (0,)), "nt": ((1,), (1,)), "tn": ((0,), (0,))}


ANY_SPEC = pl.BlockSpec(memory_space=pl.ANY)
WIDE_BN = 1024


def _mm(name, mode, a, bs, epilogue, outs, *, m, n, k, extras=(), bm=1024, bn=512, bk=None, after=()):
    bm, bn = min(bm, m), min(bn, n)
    bk = k if bk is None else min(bk, k)
    assert m % bm == 0 and n % bn == 0 and k % bk == 0, (name, m, n, k, bm, bn, bk)
    nk = k // bk
    a_list = list(a) if isinstance(a, (list, tuple)) else [a]
    na, nb, ne, no = len(a_list), len(bs), len(extras), len(outs)
    assert na in (1, nb)

    if mode == "tn":
        a_spec = pl.BlockSpec((bk, bm), lambda i, j, kk: (kk, i))
    else:
        a_spec = pl.BlockSpec((bm, bk), lambda i, j, kk: (i, kk))

    def b_spec(lead, off):
        if mode == "nt":
            blk, idx = (bn, bk), (lambda i, j, kk: (j + off, kk))
        else:
            blk, idx = (bk, bn), (lambda i, j, kk: (kk, j + off))
        if lead is None:
            return pl.BlockSpec(blk, idx)
        return pl.BlockSpec((None,) + blk, lambda i, j, kk: (lead,) + idx(i, j, kk))

    def e_spec(kind, off):
        if kind == "mn":
            return pl.BlockSpec((bm, bn), lambda i, j, kk: (i, j + off))
        return pl.BlockSpec((1, bn), lambda i, j, kk: (0, j + off))

    nf = len(after)
    in_specs = [a_spec] * na + [b_spec(l, o) for _, l, o in bs] + [e_spec(kd, o) for _, kd, o in extras]
    in_specs += [ANY_SPEC] * nf
    out_specs = [pl.BlockSpec((bm, bn), lambda i, j, kk: (i, j)) for _ in outs]
    out_shape = [jax.ShapeDtypeStruct((m, n), dt) for (dt,) in outs]
    dims = (_DOT_DIMS[mode], ((), ()))

    def body(*refs):
        a_refs = refs[:na]
        b_refs = refs[na:na + nb]
        e_refs = refs[na + nb:na + nb + ne]
        o_refs = refs[na + nb + ne + nf:na + nb + ne + nf + no]
        acc_refs = refs[na + nb + ne + nf + no:]
        avs = [a_ref[...].astype(BF16) for a_ref in a_refs]
        prods = [lax.dot_general(avs[bi % na], b_ref[...].astype(BF16), dims, preferred_element_type=F32)
                 for bi, b_ref in enumerate(b_refs)]

        def finish(accs):
            res = epilogue(accs, [e_ref[...] for e_ref in e_refs])
            for o_ref, r in zip(o_refs, res):
                o_ref[...] = r.astype(o_ref.dtype)

        if nk == 1:
            finish(prods)
        else:
            kk = pl.program_id(2)

            @pl.when(kk == 0)
            def _():
                for acc_ref, p in zip(acc_refs, prods):
                    acc_ref[...] = p

            @pl.when(kk > 0)
            def _():
                for acc_ref, p in zip(acc_refs, prods):
                    acc_ref[...] += p

            @pl.when(kk == nk - 1)
            def _():
                finish([acc_ref[...] for acc_ref in acc_refs])

    scratch = [] if nk == 1 else [pltpu.VMEM((bm, bn), F32) for _ in bs]
    res = pl.pallas_call(
        body,
        name=name,
        grid=(m // bm, n // bn, nk),
        in_specs=in_specs,
        out_specs=out_specs,
        out_shape=out_shape,
        scratch_shapes=scratch,
        compiler_params=pltpu.CompilerParams(dimension_semantics=("parallel", "parallel", "arbitrary")),
    )(*a_list, *[b for b, _, _ in bs], *[e for e, _, _ in extras], *after)
    return res


def _rms_fwd(name, x, gains, after=()):
    s, d = x.shape
    ng = gains.shape[0]
    t = _row_tile(s, 256)
    nf = len(after)

    def body(x_ref, g_ref, *refs):
        o_refs = refs[nf:]
        xv = x_ref[...]
        r = lax.rsqrt(jnp.mean(xv * xv, axis=-1, keepdims=True) + RMS_EPS)
        xh = xv * r
        for gi, o_ref in enumerate(o_refs):
            o_ref[...] = (xh * g_ref[gi:gi + 1, :]).astype(o_ref.dtype)

    return pl.pallas_call(
        body,
        name=name,
        grid=(s // t,),
        in_specs=[pl.BlockSpec((t, d), lambda i: (i, 0)), pl.BlockSpec((ng, d), lambda i: (0, 0))] + [ANY_SPEC] * nf,
        out_specs=[pl.BlockSpec((t, d), lambda i: (i, 0)) for _ in range(ng)],
        out_shape=[jax.ShapeDtypeStruct((s, d), BF16) for _ in range(ng)],
        compiler_params=pltpu.CompilerParams(dimension_semantics=("parallel",)),
    )(x, gains, *after)


def _rms_bwd(name, x, gains, dns, dres):
    s, d = x.shape
    ng = gains.shape[0]
    t = _row_tile(s, 256)

    def body(x_ref, g_ref, dres_ref, *refs):
        dn_refs = refs[:ng]
        dx_ref, dxb_ref, dg_ref, cs_ref = refs[ng:]
        i = pl.program_id(0)
        xv = x_ref[...]
        r = lax.rsqrt(jnp.mean(xv * xv, axis=-1, keepdims=True) + RMS_EPS)
        xh = xv * r
        dx = dres_ref[...]
        dgs = []
        for gi in range(ng):
            dn = dn_refs[gi][...].astype(F32)
            dxh = dn * g_ref[gi:gi + 1, :]
            dgs.append(jnp.sum(dn * xh, axis=0, keepdims=True))
            dx = dx + r * (dxh - xh * jnp.mean(dxh * xh, axis=-1, keepdims=True))
        dx_ref[...] = dx
        dxb_ref[...] = dx.astype(BF16)
        dg = jnp.concatenate(dgs, axis=0) if ng > 1 else dgs[0]
        cs = jnp.sum(dx, axis=0, keepdims=True)

        @pl.when(i == 0)
        def _():
            dg_ref[...] = dg
            cs_ref[...] = cs

        @pl.when(i > 0)
        def _():
            dg_ref[...] += dg
            cs_ref[...] += cs

    row = pl.BlockSpec((t, d), lambda i: (i, 0))
    return pl.pallas_call(
        body,
        name=name,
        grid=(s // t,),
        in_specs=[row, pl.BlockSpec((ng, d), lambda i: (0, 0)), row] + [row] * ng,
        out_specs=[row, row, pl.BlockSpec((ng, d), lambda i: (0, 0)), pl.BlockSpec((1, d), lambda i: (0, 0))],
        out_shape=[
            jax.ShapeDtypeStruct((s, d), F32),
            jax.ShapeDtypeStruct((s, d), BF16),
            jax.ShapeDtypeStruct((ng, d), F32),
            jax.ShapeDtypeStruct((1, d), F32),
        ],
        compiler_params=pltpu.CompilerParams(dimension_semantics=("arbitrary",)),
    )(x, gains, dres, *dns)


def _final_loss(name, h, gain, target):
    s, d = h.shape
    t = _row_tile(s, 256)

    def body(h_ref, g_ref, t_ref, dh_ref, dhb_ref, dg_ref, ls_ref):
        i = pl.program_id(0)
        xv = h_ref[...]
        g = g_ref[...]
        r = lax.rsqrt(jnp.mean(xv * xv, axis=-1, keepdims=True) + RMS_EPS)
        xh = xv * r
        err = xh * g - t_ref[...]
        ls = jnp.sum(err * err, axis=0, keepdims=True) * (0.5 / d)
        dy = err * (1.0 / d)
        dxh = dy * g
        dg = jnp.sum(dy * xh, axis=0, keepdims=True)
        dx = r * (dxh - xh * jnp.mean(dxh * xh, axis=-1, keepdims=True))
        dh_ref[...] = dx
        dhb_ref[...] = dx.astype(BF16)

        @pl.when(i == 0)
        def _():
            dg_ref[...] = dg
            ls_ref[...] = ls

        @pl.when(i > 0)
        def _():
            dg_ref[...] += dg
            ls_ref[...] += ls

    row = pl.BlockSpec((t, d), lambda i: (i, 0))
    vec = pl.BlockSpec((1, d), lambda i: (0, 0))
    return pl.pallas_call(
        body,
        name=name,
        grid=(s // t,),
        in_specs=[row, vec, row],
        out_specs=[row, row, vec, vec],
        out_shape=[
            jax.ShapeDtypeStruct((s, d), F32),
            jax.ShapeDtypeStruct((s, d), BF16),
            jax.ShapeDtypeStruct((1, d), F32),
            jax.ShapeDtypeStruct((1, d), F32),
        ],
        compiler_params=pltpu.CompilerParams(dimension_semantics=("arbitrary",)),
    )(h, gain, target)


SUBLANES = 8
CONV_LANES = 512
CONV_ROWS = 32
NORM_ROWS = 16


def _conv_tiles(s):
    t = _row_tile(s, 128)
    assert t % CONV_HALO == 0 and t % CONV_ROWS == 0
    return t, t // CONV_HALO


def _shifted_copies(dst, src, lanes, rows):
    for m in range(SUBLANES):
        n = rows if m == 0 else rows - SUBLANES
        dst[m, :n, :] = src[m:m + n, lanes]


def _shifted(copies, offset, n):
    m = offset % SUBLANES
    return copies[m, offset - m:offset - m + n, :]


def _dwconv_fwd(name, u, dw, dw_b, ln_g, ln_b):
    s, d = u.shape
    t, hb = _conv_tiles(s)
    w = dw.shape[0]
    lo = CONV_HALO - (w - 1)

    lw = min(CONV_LANES, d)

    def body(cur_ref, prev_ref, dw_ref, dwb_ref, lg_ref, lb_ref, c_ref, sw_ref, cat_ref, sh_ref):
        i = pl.program_id(0)
        cat_ref[CONV_HALO:, :] = cur_ref[...]

        @pl.when(i == 0)
        def _():
            cat_ref[:CONV_HALO, :] = jnp.zeros((CONV_HALO, d), F32)

        @pl.when(i > 0)
        def _():
            cat_ref[:CONV_HALO, :] = prev_ref[...]

        for lc in range(d // lw):
            lanes = slice(lc * lw, (lc + 1) * lw)
            _shifted_copies(sh_ref, cat_ref, lanes, t + CONV_HALO)
            for rc in range(t // CONV_ROWS):
                acc = jnp.broadcast_to(dwb_ref[:, lanes], (CONV_ROWS, lw))
                for kk in range(w):
                    acc = acc + dw_ref[kk:kk + 1, lanes] * _shifted(sh_ref, lo + kk + rc * CONV_ROWS, CONV_ROWS)
                c_ref[rc * CONV_ROWS:(rc + 1) * CONV_ROWS, lanes] = acc

        def norm_rows(ri, carry):
            rows = pl.ds(pl.multiple_of(ri * NORM_ROWS, NORM_ROWS), NORM_ROWS)
            cv = c_ref[rows, :]
            cc = cv - jnp.mean(cv, axis=-1, keepdims=True)
            var = jnp.mean(cc * cc, axis=-1, keepdims=True)
            ln = cc * lax.rsqrt(var + LN_EPS) * lg_ref[...] + lb_ref[...]
            sw_ref[rows, :] = (ln * _sigmoid(ln)).astype(BF16)
            return carry

        lax.fori_loop(0, t // NORM_ROWS, norm_rows, 0)

    row = pl.BlockSpec((t, d), lambda i: (i, 0))
    prev = pl.BlockSpec((CONV_HALO, d), lambda i: (jnp.maximum(i * hb - 1, 0), 0))
    vec = pl.BlockSpec((1, d), lambda i: (0, 0))
    return pl.pallas_call(
        body,
        name=name,
        grid=(s // t,),
        in_specs=[row, prev, pl.BlockSpec((w, d), lambda i: (0, 0)), vec, vec, vec],
        out_specs=[row, row],
        out_shape=[jax.ShapeDtypeStruct((s, d), F32), jax.ShapeDtypeStruct((s, d), BF16)],
        scratch_shapes=[pltpu.VMEM((CONV_HALO + t, d), F32), pltpu.VMEM((SUBLANES, CONV_HALO + t, lw), F32)],
        compiler_params=pltpu.CompilerParams(dimension_semantics=("parallel",)),
    )(u, u, dw, dw_b, ln_g, ln_b)


def _conv_ln_bwd(name, c, dsw, ln_g, ln_b, after=()):
    s, d = c.shape
    t = _row_tile(s, 256)
    nf = len(after)

    def body(c_ref, dsw_ref, lg_ref, lb_ref, *refs):
        dc_ref, sums_ref = refs[nf:]
        i = pl.program_id(0)
        cv = c_ref[...]
        g = lg_ref[...]
        mu = jnp.mean(cv, axis=-1, keepdims=True)
        cc = cv - mu
        rstd = lax.rsqrt(jnp.mean(cc * cc, axis=-1, keepdims=True) + LN_EPS)
        ch = cc * rstd
        ln = ch * g + lb_ref[...]
        sg = _sigmoid(ln)
        dln = dsw_ref[...] * (sg * (1.0 + ln * (1.0 - sg)))
        dch = dln * g
        dc = rstd * (dch - jnp.mean(dch, axis=-1, keepdims=True) - ch * jnp.mean(dch * ch, axis=-1, keepdims=True))
        dc_ref[...] = dc
        sums = jnp.concatenate(
            [
                jnp.sum(dln * ch, axis=0, keepdims=True),
                jnp.sum(dln, axis=0, keepdims=True),
                jnp.sum(dc, axis=0, keepdims=True),
                jnp.zeros((1, d), F32),
            ],
            axis=0,
        )

        @pl.when(i == 0)
        def _():
            sums_ref[...] = sums

        @pl.when(i > 0)
        def _():
            sums_ref[...] += sums

    row = pl.BlockSpec((t, d), lambda i: (i, 0))
    vec = pl.BlockSpec((1, d), lambda i: (0, 0))
    return pl.pallas_call(
        body,
        name=name,
        grid=(s // t,),
        in_specs=[row, row, vec, vec] + [ANY_SPEC] * nf,
        out_specs=[row, pl.BlockSpec((4, d), lambda i: (0, 0))],
        out_shape=[jax.ShapeDtypeStruct((s, d), F32), jax.ShapeDtypeStruct((4, d), F32)],
        compiler_params=pltpu.CompilerParams(dimension_semantics=("arbitrary",)),
    )(c, dsw, ln_g, ln_b, *after)


def _conv_dw_bwd(name, dc, u, a, gt, dw):
    s, d = dc.shape
    t, hb = _conv_tiles(s)
    w = dw.shape[0]
    lo = CONV_HALO - (w - 1)
    nt = s // t
    lw = min(CONV_LANES, d)

    def body(dc_ref, dcn_ref, u_ref, up_ref, a_ref, gt_ref, dw_ref, dpre_ref, ddw_ref, db_ref, dcat_ref, ucat_ref,
             dsh_ref, ush_ref, ddw_acc, db_acc):
        i = pl.program_id(0)
        dcat_ref[:t, :] = dc_ref[...]
        ucat_ref[CONV_HALO:, :] = u_ref[...]

        @pl.when(i == nt - 1)
        def _():
            dcat_ref[t:, :] = jnp.zeros((CONV_HALO, d), F32)

        @pl.when(i < nt - 1)
        def _():
            dcat_ref[t:, :] = dcn_ref[...]

        @pl.when(i == 0)
        def _():
            ucat_ref[:CONV_HALO, :] = jnp.zeros((CONV_HALO, d), F32)

        @pl.when(i > 0)
        def _():
            ucat_ref[:CONV_HALO, :] = up_ref[...]

        @pl.when(i == 0)
        def _():
            ddw_acc[...] = jnp.zeros(ddw_acc.shape, F32)
            db_acc[...] = jnp.zeros(db_acc.shape, F32)

        def fold(p):
            return functools.reduce(jnp.add, [p[j:j + SUBLANES] for j in range(0, CONV_ROWS, SUBLANES)])

        for lc in range(d // lw):
            lanes = slice(lc * lw, (lc + 1) * lw)
            gate_lanes = slice(d + lc * lw, d + (lc + 1) * lw)
            _shifted_copies(dsh_ref, dcat_ref, lanes, t + CONV_HALO)
            _shifted_copies(ush_ref, ucat_ref, lanes, t + CONV_HALO)
            for rc in range(t // CONV_ROWS):
                r0 = rc * CONV_ROWS
                rows = slice(r0, r0 + CONV_ROWS)
                dcv = dc_ref[rows, lanes]
                du = jnp.zeros((CONV_ROWS, lw), F32)
                for kk in range(w):
                    du = du + dw_ref[kk:kk + 1, lanes] * _shifted(dsh_ref, w - 1 - kk + r0, CONV_ROWS)
                    ddw_acc[kk, :, lanes] += fold(dcv * _shifted(ush_ref, lo + kk + r0, CONV_ROWS))
                av = a_ref[rows, lanes].astype(F32)
                sg = _sigmoid(gt_ref[rows, lanes].astype(F32))
                da = du * sg
                dgt = du * av * sg * (1.0 - sg)
                dpre_ref[rows, lanes] = da.astype(BF16)
                dpre_ref[rows, gate_lanes] = dgt.astype(BF16)
                db_acc[:, lanes] += fold(da)
                db_acc[:, gate_lanes] += fold(dgt)

        @pl.when(i == nt - 1)
        def _():
            ddw_ref[...] = jnp.sum(ddw_acc[...], axis=1)
            db_ref[...] = jnp.sum(db_acc[...], axis=0, keepdims=True)

    row = pl.BlockSpec((t, d), lambda i: (i, 0))
    nxt = pl.BlockSpec((CONV_HALO, d), lambda i: (jnp.minimum((i + 1) * hb, s // CONV_HALO - 1), 0))
    prev = pl.BlockSpec((CONV_HALO, d), lambda i: (jnp.maximum(i * hb - 1, 0), 0))
    return pl.pallas_call(
        body,
        name=name,
        grid=(nt,),
        in_specs=[row, nxt, row, prev, row, row, pl.BlockSpec((w, d), lambda i: (0, 0))],
        out_specs=[
            pl.BlockSpec((t, 2 * d), lambda i: (i, 0)),
            pl.BlockSpec((w, d), lambda i: (0, 0)),
            pl.BlockSpec((1, 2 * d), lambda i: (0, 0)),
        ],
        out_shape=[
            jax.ShapeDtypeStruct((s, 2 * d), BF16),
            jax.ShapeDtypeStruct((w, d), F32),
            jax.ShapeDtypeStruct((1, 2 * d), F32),
        ],
        scratch_shapes=[pltpu.VMEM((t + CONV_HALO, d), F32), pltpu.VMEM((CONV_HALO + t, d), F32)]
        + [pltpu.VMEM((SUBLANES, CONV_HALO + t, lw), F32)] * 2
        + [pltpu.VMEM((w, SUBLANES, d), F32), pltpu.VMEM((SUBLANES, 2 * d), F32)],
        compiler_params=pltpu.CompilerParams(dimension_semantics=("arbitrary",)),
    )(dc, dc, u, u, a, gt, dw)


def _alibi_slopes(n_heads):
    h = jnp.arange(1, n_heads + 1, dtype=F32)
    return jnp.exp2(-8.0 * h / n_heads)


def _band_masks(bq):
    qi = lax.broadcasted_iota(jnp.int32, (bq, bq), 0)
    kj = lax.broadcasted_iota(jnp.int32, (bq, bq), 1)
    return qi - kj, qi - kj + bq


ATTN_GROUP = 16


def _attn_fwd(name, q, k, v, slopes, bq):
    s, dm = q.shape
    nh = dm // HEAD_DIM
    nt = s // bq
    nbr = len(BRANCHES)
    scale = HEAD_DIM ** -0.5
    nt_dims = (((1,), (1,)), ((), ()))

    def body(sl_ref, q_ref, k_ref, v_ref, o_ref, ob_ref, l_ref, tmp, qr, kr, va, orm, lrm, onat, lnat, sbuf, mbuf):
        slope = sl_ref[pl.program_id(0)]
        jc, jp = _band_masks(bq)
        va[:, HEAD_DIM:] = jnp.ones((s, HEAD_DIM), BF16)
        for bi, (win, dil) in enumerate(BRANCHES):
            ll = s // dil
            nblk = ll // bq
            if dil == 1:
                sq, sk = q_ref, k_ref
                va[:, :HEAD_DIM] = v_ref[...]
                d_o, d_l = onat.at[bi], lnat.at[bi]
            else:
                for src, dst, wide in ((q_ref, qr, False), (k_ref, kr, False), (v_ref, va, True)):
                    tmp[...] = src[...].astype(F32)
                    for r in range(dil):
                        part = tmp[pl.ds(r, ll, stride=dil), :].astype(BF16)
                        if wide:
                            dst[r * ll:(r + 1) * ll, :HEAD_DIM] = part
                        else:
                            dst[r * ll:(r + 1) * ll, :] = part
                sq, sk = qr, kr
                d_o, d_l = orm, lrm
            bias_c = jnp.where(jc >= 0, jc.astype(F32) * (slope * dil), 1e30)
            bias_p = jnp.where(jp <= bq, jp.astype(F32) * (slope * dil), 1e30)

            def group(gi, carry):
                rows = []
                for g in range(ATTN_GROUP):
                    ti = gi * ATTN_GROUP + g
                    row = pl.ds(pl.multiple_of(ti * bq, bq), bq)
                    prow = pl.ds(pl.multiple_of(jnp.maximum(ti - 1, 0) * bq, bq), bq)
                    rows.append((row, prow))
                    qh = sq[row, :]
                    sc = lax.dot_general(qh, sk[row, :], nt_dims, preferred_element_type=F32) * scale - bias_c
                    sp = lax.dot_general(qh, sk[prow, :], nt_dims, preferred_element_type=F32) * scale - bias_p
                    sp = jnp.where(lax.rem(ti, nblk) > 0, sp, -1e30)
                    sbuf[g, :, :bq] = sc
                    sbuf[g, :, bq:] = sp
                    mbuf[g] = jnp.maximum(jnp.max(sc, axis=-1, keepdims=True), jnp.max(sp, axis=-1, keepdims=True))
                for g, (row, prow) in enumerate(rows):
                    mx = mbuf[g]
                    p = jnp.exp(sbuf[g] - mx).astype(BF16)
                    ov = jnp.dot(p[:, :bq], va[row, :], preferred_element_type=F32)
                    ov = ov + jnp.dot(p[:, bq:], va[prow, :], preferred_element_type=F32)
                    den = ov[:, HEAD_DIM:]
                    d_o[row, :] = ov[:, :HEAD_DIM] / den
                    d_l[row, :] = mx + jnp.log(den)
                return carry

            lax.fori_loop(0, nt // ATTN_GROUP, group, 0)
            if dil > 1:
                for r in range(dil):
                    onat[bi, pl.ds(r, ll, stride=dil), :] = orm[r * ll:(r + 1) * ll, :]
                    lnat[bi, pl.ds(r, ll, stride=dil), :] = lrm[r * ll:(r + 1) * ll, :]

        def merge(ti, carry):
            rows = pl.ds(pl.multiple_of(ti * bq, bq), bq)
            ls = [lnat[bi, rows, :] for bi in range(nbr)]
            mx = functools.reduce(jnp.maximum, ls)
            es = [jnp.exp(l - mx) for l in ls]
            tot = functools.reduce(jnp.add, es)
            inv = 1.0 / tot
            o = functools.reduce(jnp.add, [e * inv * onat[bi, rows, :] for bi, e in enumerate(es)])
            o_ref[rows, :] = o
            ob_ref[rows, :] = o.astype(BF16)
            l_ref[rows, :] = mx + jnp.log(tot)
            return carry

        lax.fori_loop(0, nt, merge, 0)

    head = pl.BlockSpec((s, HEAD_DIM), lambda h: (0, h))
    return pl.pallas_call(
        body,
        name=name,
        grid=(nh,),
        in_specs=[pl.BlockSpec(memory_space=pltpu.SMEM), head, head, head],
        out_specs=[head, head, head],
        out_shape=[jax.ShapeDtypeStruct((s, dm), F32), jax.ShapeDtypeStruct((s, dm), BF16),
                   jax.ShapeDtypeStruct((s, dm), F32)],
        scratch_shapes=[pltpu.VMEM((s, HEAD_DIM), F32)] + [pltpu.VMEM((s, HEAD_DIM), BF16)] * 2
        + [pltpu.VMEM((s, 2 * HEAD_DIM), BF16)] + [pltpu.VMEM((s, HEAD_DIM), F32)] * 2
        + [pltpu.VMEM((nbr, s, HEAD_DIM), F32)] * 2
        + [pltpu.VMEM((ATTN_GROUP, bq, 2 * bq), F32), pltpu.VMEM((ATTN_GROUP, bq, 1), F32)],
        compiler_params=pltpu.CompilerParams(dimension_semantics=("parallel",)),
    )(slopes, q, k, v)


def _attn_bwd(name, q, k, v, o, lse, do, slopes, bq):
    s, dm = q.shape
    nh = dm // HEAD_DIM
    nt = s // bq
    scale = HEAD_DIM ** -0.5
    nt_dims = (((1,), (1,)), ((), ()))
    tn_dims = (((0,), (0,)), ((), ()))

    def body(sl_ref, q_ref, k_ref, v_ref, o_ref, l_ref, do_ref, dq_ref, dk_ref, dv_ref,
             tmp, qr, kr, vr, dor, lr, dlr, dln, dqr, dkr, dvr, aq, ak, av, pbuf, dsbuf):
        slope = sl_ref[pl.program_id(0)]
        jc, jp = _band_masks(bq)

        def delta(ti, carry):
            rows = pl.ds(pl.multiple_of(ti * bq, bq), bq)
            dl = jnp.sum(do_ref[rows, :].astype(F32) * o_ref[rows, :], axis=-1, keepdims=True)
            dln[rows, :] = jnp.broadcast_to(dl, (bq, HEAD_DIM))
            return carry

        lax.fori_loop(0, nt, delta, 0)

        for bi, (win, dil) in enumerate(BRANCHES):
            ll = s // dil
            nblk = ll // bq
            if dil == 1:
                sq, sk, sv, sdo, sl, sdl = q_ref, k_ref, v_ref, do_ref, l_ref, dln
                gq, gk, gv = aq, ak, av
            else:
                for src, dst in ((q_ref, qr), (k_ref, kr), (v_ref, vr), (do_ref, dor)):
                    tmp[...] = src[...].astype(F32)
                    for r in range(dil):
                        dst[r * ll:(r + 1) * ll, :] = tmp[pl.ds(r, ll, stride=dil), :].astype(BF16)
                for r in range(dil):
                    lr[r * ll:(r + 1) * ll, :] = l_ref[pl.ds(r, ll, stride=dil), :]
                    dlr[r * ll:(r + 1) * ll, :] = dln[pl.ds(r, ll, stride=dil), :]
                sq, sk, sv, sdo, sl, sdl = qr, kr, vr, dor, lr, dlr
                gq, gk, gv = dqr, dkr, dvr
            bias_c = jnp.where(jc >= 0, jc.astype(F32) * (slope * dil), 1e30)
            bias_p = jnp.where(jp <= bq, jp.astype(F32) * (slope * dil), 1e30)

            def group(gi, carry):
                rows = []
                for g in range(ATTN_GROUP):
                    ti = gi * ATTN_GROUP + g
                    row = pl.ds(pl.multiple_of(ti * bq, bq), bq)
                    prow = pl.ds(pl.multiple_of(jnp.maximum(ti - 1, 0) * bq, bq), bq)
                    rows.append((row, prow))
                    has_prev = lax.rem(ti, nblk) > 0
                    qh, doh = sq[row, :], sdo[row, :]
                    lc = sl[row, :][:, :1]
                    dl = sdl[row, :][:, :1]
                    for half, kv_rows, bias in ((0, row, bias_c), (1, prow, bias_p)):
                        sc = lax.dot_general(qh, sk[kv_rows, :], nt_dims, preferred_element_type=F32) * scale - bias
                        if half:
                            sc = jnp.where(has_prev, sc, -1e30)
                        p = jnp.exp(sc - lc)
                        dp = lax.dot_general(doh, sv[kv_rows, :], nt_dims, preferred_element_type=F32)
                        pbuf[g, :, half * bq:(half + 1) * bq] = p.astype(BF16)
                        dsbuf[g, :, half * bq:(half + 1) * bq] = (p * (dp - dl) * scale).astype(BF16)
                carry_k = carry_v = None
                for g, (row, prow) in enumerate(rows):
                    qh, doh = sq[row, :], sdo[row, :]
                    ds_c, ds_p = dsbuf[g, :, :bq], dsbuf[g, :, bq:]
                    p_c, p_p = pbuf[g, :, :bq], pbuf[g, :, bq:]
                    dq = jnp.dot(ds_c, sk[row, :], preferred_element_type=F32)
                    dq = dq + jnp.dot(ds_p, sk[prow, :], preferred_element_type=F32)
                    gq[row, :] = dq
                    dk_p = lax.dot_general(ds_p, qh, tn_dims, preferred_element_type=F32)
                    dv_p = lax.dot_general(p_p, doh, tn_dims, preferred_element_type=F32)
                    if g == 0:
                        @pl.when(gi > 0)
                        def _():
                            gk[prow, :] += dk_p
                            gv[prow, :] += dv_p
                    else:
                        gk[rows[g - 1][0], :] = carry_k + dk_p
                        gv[rows[g - 1][0], :] = carry_v + dv_p
                    carry_k = lax.dot_general(ds_c, qh, tn_dims, preferred_element_type=F32)
                    carry_v = lax.dot_general(p_c, doh, tn_dims, preferred_element_type=F32)
                gk[rows[-1][0], :] = carry_k
                gv[rows[-1][0], :] = carry_v
                return carry

            lax.fori_loop(0, nt // ATTN_GROUP, group, 0)
            if dil > 1:
                for acc, rm in ((aq, dqr), (ak, dkr), (av, dvr)):
                    for r in range(dil):
                        acc[pl.ds(r, ll, stride=dil), :] += rm[r * ll:(r + 1) * ll, :]

        dq_ref[...] = aq[...].astype(BF16)
        dk_ref[...] = ak[...].astype(BF16)
        dv_ref[...] = av[...].astype(BF16)

    head = pl.BlockSpec((s, HEAD_DIM), lambda h: (0, h))
    f32buf = pltpu.VMEM((s, HEAD_DIM), F32)
    b16buf = pltpu.VMEM((s, HEAD_DIM), BF16)
    return pl.pallas_call(
        body,
        name=name,
        grid=(nh,),
        in_specs=[pl.BlockSpec(memory_space=pltpu.SMEM)] + [head] * 6,
        out_specs=[head] * 3,
        out_shape=[jax.ShapeDtypeStruct((s, dm), BF16)] * 3,
        scratch_shapes=[f32buf] + [b16buf] * 4 + [f32buf] * 9 + [pltpu.VMEM((ATTN_GROUP, bq, 2 * bq), BF16)] * 2,
        compiler_params=pltpu.CompilerParams(dimension_semantics=("parallel",)),
    )(slopes, q, k, v, o, lse, do)


def _ep_id(accs, ex):
    return [accs[0]]


def _ep_all(accs, ex):
    return list(accs)


def _ep_sum(accs, ex):
    return [accs[0] + accs[1]]


def _ep_add(accs, ex):
    return [accs[0] + ex[0].astype(F32)]


def _ep_bias_res(accs, ex):
    return [accs[0] + ex[0] + ex[1]]


def _ep_glu(accs, ex):
    a = accs[0] + ex[0]
    gt = accs[1] + ex[1]
    return [a * _sigmoid(gt), a, gt]


def _ep_swiglu(accs, ex):
    g, u = accs
    return [g, u, g * _sigmoid(g) * u]


def _ep_swiglu_bwd(accs, ex):
    dact = accs[0]
    g = ex[0].astype(F32)
    u = ex[1].astype(F32)
    sg = _sigmoid(g)
    return [dact * u * (sg * (1.0 + g * (1.0 - sg))), dact * g * sg]


def _ffn_fwd(tag, h, gain, wg, wu, wd):
    s, d = h.shape
    f = wg.shape[-1]
    (n,) = _rms_fwd(f"{tag}_norm", h, gain)
    g, u, act = _mm(f"{tag}_gate_up", "nn", n, [(wg, None, 0), (wu, None, 0)], _ep_swiglu,
                    [(BF16,), (BF16,), (BF16,)], m=s, n=f, k=d)
    (out,) = _mm(f"{tag}_down", "nn", act, [(wd, None, 0)], _ep_add, [(F32,)], m=s, n=d, k=f,
                 extras=[(h, "mn", 0)], bn=256)
    return out, (n, g, u, act)


def _ffn_bwd(tag, h_in, gain, wg, wu, wd, saved, dh, dhb, after, emit):
    s, d = h_in.shape
    f = wg.shape[-1]
    n, g, u, act = saved
    dg, du = _mm(f"{tag}_bwd_dact", "nt", dhb, [(wd, None, 0)], _ep_swiglu_bwd, [(BF16,), (BF16,)],
                 m=s, n=f, k=d, extras=[(g, "mn", 0), (u, "mn", 0)], after=after)
    (dwd,) = _mm(f"{tag}_bwd_dwd", "tn", act, [(dhb, None, 0)], _ep_id, [(BF16,)], m=f, n=d, k=s,
                 bm=f // 4)
    pin = emit("d", dict(down=dwd))
    dwg, dwu = _mm(f"{tag}_bwd_dwgu", "tn", n, [(dg, None, 0), (du, None, 0)], _ep_all, [(BF16,), (BF16,)],
                   m=d, n=f, k=s, after=pin)
    pin = emit("gu", dict(gate=dwg, up=dwu))
    (dn,) = _mm(f"{tag}_bwd_dn", "nt", [dg, du], [(wg, None, 0), (wu, None, 0)], _ep_sum, [(BF16,)],
                m=s, n=d, k=f, bm=512, bn=256, after=pin)
    dx, dxb, dgain, cs = _rms_bwd(f"{tag}_bwd_norm", h_in, gain, [dn], dh)
    return dx, dxb, dgain, cs


def _local_step(x, target, w, fetch, emit, after=()):
    s, d = x.shape
    nh = d // HEAD_DIM
    bq = BRANCHES[0][0] // BRANCHES[0][1]
    assert all(win // dil == bq for win, dil in BRANCHES)
    assert BRANCHES[0][1] == 1 and all(dil > 1 for _, dil in BRANCHES[1:])
    slopes = _alibi_slopes(nh)
    nd = d // 512 if d >= 512 else 1
    bn = d // nd

    (n1,) = _rms_fwd("a_norm", x, w["a_norm_g"], after=after)
    w_conv1 = fetch("conv1", n1)["conv_w1"]
    glu, a, gt = _mm("conv_pw1_glu", "nn", n1, [(w_conv1, None, 0), (w_conv1, None, nd)], _ep_glu,
                     [(F32,), (BF16,), (BF16,)], m=s, n=d, k=d, bn=bn,
                     extras=[(w["conv_b1"], "n", 0), (w["conv_b1"], "n", nd)])
    c, sw = _dwconv_fwd("conv_dw_ln", glu, w["conv_dw"], w["conv_dw_b"], w["conv_ln_g"], w["conv_ln_b"])
    w_conv2 = fetch("conv2", sw)["conv_w2"]
    (h1,) = _mm("conv_pw2", "nn", sw, [(w_conv2, None, 0)], _ep_bias_res, [(F32,)], m=s, n=d, k=d,
                extras=[(w["conv_b2"], "n", 0), (x, "mn", 0)], bn=WIDE_BN)
    wf0 = fetch("ffn0", h1)
    h2, ffn0 = _ffn_fwd("ffn0", h1, w["ffn_norm_g"][0:1], wf0["gate"], wf0["up"], wf0["down"])
    wa = fetch("attn", h2)
    kvn, qn = _rms_fwd("kvq_norm", h2, jnp.concatenate([w["kv_norm_g"], w["b_norm_g"]], axis=0))
    k, v = _mm("kv_proj", "nn", kvn, [(wa["w_k"], None, 0), (wa["w_v"], None, 0)], _ep_all, [(BF16,), (BF16,)],
               m=s, n=d, k=d, bn=WIDE_BN)
    (q,) = _mm("q_proj", "nn", qn, [(wa["w_q"], None, 0)], _ep_id, [(BF16,)], m=s, n=d, k=d, bn=WIDE_BN)
    att, attb, lse = _attn_fwd("attn_fwd", q, k, v, slopes, bq)
    (h3,) = _mm("o_proj", "nn", attb, [(wa["w_o"], None, 0)], _ep_add, [(F32,)], m=s, n=d, k=d,
                extras=[(h2, "mn", 0)], bn=WIDE_BN)
    wf1 = fetch("ffn1", h3)
    h4, ffn1 = _ffn_fwd("ffn1", h3, w["ffn_norm_g"][1:2], wf1["gate"], wf1["up"], wf1["down"])
    dh4, dh4b, d_final_g, loss_cols = _final_loss("final_loss", h4, w["final_norm_g"], target)

    g = {}
    ga = {}
    dh3, dh3b, dgain1, _ = _ffn_bwd("ffn1", h3, w["ffn_norm_g"][1:2], wf1["gate"], wf1["up"], wf1["down"], ffn1,
                                    dh4, dh4b, (), lambda part, grads: emit("ffn1" + part, grads))
    (datt,) = _mm("o_proj_bwd_dx", "nt", dh3b, [(wa["w_o"], None, 0)], _ep_id, [(BF16,)], m=s, n=d, k=d, bn=WIDE_BN)
    (ga["w_o"],) = _mm("o_proj_bwd_dw", "tn", attb, [(dh3b, None, 0)], _ep_id, [(BF16,)], m=d, n=d, k=s, bn=WIDE_BN)
    dq, dk, dv = _attn_bwd("attn_bwd", q, k, v, att, lse, datt, slopes, bq)
    (ga["w_q"],) = _mm("q_proj_bwd_dw", "tn", qn, [(dq, None, 0)], _ep_id, [(BF16,)], m=d, n=d, k=s, bn=WIDE_BN)
    ga["w_k"], ga["w_v"] = _mm("kv_proj_bwd_dw", "tn", kvn, [(dk, None, 0), (dv, None, 0)], _ep_all,
                               [(BF16,), (BF16,)], m=d, n=d, k=s)
    pin = emit("attn", ga)
    (dqn,) = _mm("q_proj_bwd_dx", "nt", dq, [(wa["w_q"], None, 0)], _ep_id, [(BF16,)], m=s, n=d, k=d, after=pin,
                 bn=WIDE_BN)
    (dkvn,) = _mm("kv_proj_bwd_dx", "nt", [dk, dv], [(wa["w_k"], None, 0), (wa["w_v"], None, 0)], _ep_sum, [(BF16,)],
                  m=s, n=d, k=d, bn=WIDE_BN)
    dh2, dh2b, dg_kvq, _ = _rms_bwd("kvq_norm_bwd", h2, jnp.concatenate([w["kv_norm_g"], w["b_norm_g"]], axis=0),
                                    [dkvn, dqn], dh3)
    dh1, dh1b, dgain0, cs_h1 = _ffn_bwd("ffn0", h1, w["ffn_norm_g"][0:1], wf0["gate"], wf0["up"], wf0["down"], ffn0,
                                        dh2, dh2b, (), lambda part, grads: emit("ffn0" + part, grads))
    (dsw,) = _mm("conv_pw2_bwd_dx", "nt", dh1b, [(w_conv2, None, 0)], _ep_id, [(F32,)], m=s, n=d, k=d, bn=WIDE_BN)
    (dw2,) = _mm("conv_pw2_bwd_dw", "tn", sw, [(dh1b, None, 0)], _ep_id, [(BF16,)], m=d, n=d, k=s, bn=WIDE_BN)
    pin = emit("conv2", dict(conv_w2=dw2))
    dc, ln_sums = _conv_ln_bwd("conv_ln_bwd", c, dsw, w["conv_ln_g"], w["conv_ln_b"], after=pin)
    dpre, ddw, db1 = _conv_dw_bwd("conv_dw_bwd", dc, glu, a, gt, w["conv_dw"])
    (dw1,) = _mm("conv_pw1_bwd_dw", "tn", n1, [(dpre, None, 0)], _ep_id, [(BF16,)], m=d, n=2 * d, k=s)
    pin = emit("conv1", dict(conv_w1=dw1))
    (dn1,) = _mm("conv_pw1_bwd_dx", "nt", dpre, [(w_conv1, None, 0)], _ep_id, [(BF16,)], m=s, n=d, k=2 * d,
                 after=pin)
    dx, _, d_a_norm, _ = _rms_bwd("a_norm_bwd", x, w["a_norm_g"], [dn1], dh1)

    g.update(
        a_norm_g=d_a_norm, conv_b1=db1, conv_dw=ddw, conv_dw_b=ln_sums[2:3], conv_ln_g=ln_sums[0:1],
        conv_ln_b=ln_sums[1:2], conv_b2=cs_h1, kv_norm_g=dg_kvq[0:1], b_norm_g=dg_kvq[1:2],
        ffn_norm_g=jnp.concatenate([dgain0, dgain1], axis=0), final_norm_g=d_final_g,
    )
    return loss_cols, dx, g


HBM_SPEC = pl.BlockSpec(memory_space=pltpu.HBM)


def _mesh_place():
    x, y, c = lax.axis_index("x"), lax.axis_index("y"), lax.axis_index("c")
    chips = [(1 - x, y), (x, 1 - y), (1 - x, 1 - y)]
    return x, y, c, chips


def _shard_view(ref, kind, s, half=None):
    rows, cols = ref.shape
    if kind == "col":
        cw = cols // N_CHIPS
        if half is None:
            return ref.at[pl.ds(0, rows), pl.ds(s * cw, cw)]
        return ref.at[pl.ds(half * (rows // 2), rows // 2), pl.ds(s * cw, cw)]
    r = rows // N_CHIPS
    if half is None:
        return ref.at[pl.ds(s * r, r), pl.ds(0, cols)]
    return ref.at[pl.ds(s * r + half * (r // 2), r // 2), pl.ds(0, cols)]


def _full_shape(shard, kind):
    r, cw = shard.shape[-2:]
    return (r, cw * N_CHIPS) if kind == "col" else (r * N_CHIPS, cw)


def _gather_weights(shards, kinds, after=()):
    nt = len(shards)
    nf = len(after)
    fulls = [jax.ShapeDtypeStruct(_full_shape(sh, kind), sh.dtype) for sh, kind in zip(shards, kinds)]

    def body(*refs):
        src = refs[:nt]
        dst = refs[nt + nf:2 * nt + nf]
        send, recv, fsend, frecv, local = refs[2 * nt + nf:]
        x, y, c, chips = _mesh_place()
        s = 2 * x + y
        sib = (x, y, 1 - c)

        def half_of_shard(t):
            r, cw = src[t].shape
            return src[t].at[pl.ds(c * (r // 2), r // 2), pl.ds(0, cw)]

        locals_ = [pltpu.make_async_copy(src[t], _shard_view(dst[t], kinds[t], s), local.at[t]) for t in range(nt)]
        for cp in locals_:
            cp.start()
        sends = []
        for t in range(nt):
            for j, chip in enumerate(chips):
                cp = pltpu.make_async_remote_copy(
                    src_ref=half_of_shard(t), dst_ref=_shard_view(dst[t], kinds[t], s, c),
                    send_sem=send.at[t, j], recv_sem=recv.at[t, j], device_id=(*chip, c), device_id_type=MESH)
                cp.start()
                sends.append(cp)
        for t in range(nt):
            for j, (px, py) in enumerate(chips):
                landed = _shard_view(dst[t], kinds[t], 2 * px + py, c)
                pltpu.make_async_remote_copy(
                    src_ref=half_of_shard(t), dst_ref=landed, send_sem=send.at[t, j], recv_sem=recv.at[t, j],
                    device_id=(px, py, c), device_id_type=MESH).wait_recv()
                cp = pltpu.make_async_remote_copy(
                    src_ref=landed, dst_ref=landed, send_sem=fsend.at[t, j], recv_sem=frecv.at[t, j],
                    device_id=sib, device_id_type=MESH)
                cp.start()
                sends.append(cp)
        for t in range(nt):
            for j, (px, py) in enumerate(chips):
                other = _shard_view(dst[t], kinds[t], 2 * px + py, 1 - c)
                pltpu.make_async_remote_copy(
                    src_ref=other, dst_ref=other, send_sem=fsend.at[t, j], recv_sem=frecv.at[t, j],
                    device_id=sib, device_id_type=MESH).wait_recv()
        for cp in sends:
            cp.wait_send()
        for cp in locals_:
            cp.wait()

    return pl.pallas_call(
        body,
        name="gather_weights",
        in_specs=[HBM_SPEC] * nt + [ANY_SPEC] * nf,
        out_specs=[HBM_SPEC] * nt,
        out_shape=fulls,
        scratch_shapes=[pltpu.SemaphoreType.DMA((nt, 3))] * 4 + [pltpu.SemaphoreType.DMA((nt,))],
    )(*shards, *after)


def _row_blocks(rows, want=512):
    nb = 1
    while rows // nb > want or rows % nb or (rows // nb) % 16:
        nb += 1
        if nb > rows:
            return rows, 1
    return rows // nb, nb


def _place_own(name, src, kind, land, ids, lead=0):
    r, cw = src.shape[-2:]
    tr, nb = _row_blocks(r)
    if src.ndim == 3:
        src_spec = pl.BlockSpec((None, tr, cw), lambda i, ids_ref: (lead, i, 0))
    else:
        src_spec = pl.BlockSpec((tr, cw), lambda i, ids_ref: (i, 0))
    if kind == "col":
        dst_spec = pl.BlockSpec((tr, cw), lambda i, ids_ref: (i, ids_ref[0]))
    else:
        dst_spec = pl.BlockSpec((tr, cw), lambda i, ids_ref: (ids_ref[0] * nb + i, 0))

    def body(ids_ref, s_ref, o_ref):
        o_ref[...] = s_ref[...].astype(o_ref.dtype)

    return pl.pallas_call(
        body,
        name=name,
        grid_spec=pltpu.PrefetchScalarGridSpec(num_scalar_prefetch=1, grid=(nb,), in_specs=[src_spec],
                                               out_specs=dst_spec),
        out_shape=land,
        compiler_params=pltpu.CompilerParams(dimension_semantics=("parallel",)),
    )(ids, src)


SEM_SPEC = pl.BlockSpec(memory_space=pltpu.SEMAPHORE)
SIDE_EFFECT = pltpu.SideEffectType.DATAFLOW_SIDE_EFFECTING


def _copies_start(name, srcs, lands, copies, n_sems, after):
    ns, nl, nf = len(srcs), len(lands), len(after)

    def body(*refs):
        src, land = refs[:ns], refs[ns:ns + nl]
        send, recv = refs[ns + nl + nf], refs[ns + nl + nf + 1]
        pin = refs[-1]
        for cp in copies(src, land, send, recv, _mesh_place()):
            cp.start()
        pin[...] = jnp.zeros_like(pin)

    arrs = list(srcs) + list(lands)
    res = pl.pallas_call(
        body,
        name=name,
        in_specs=[HBM_SPEC] * (ns + nl) + [ANY_SPEC] * nf,
        out_specs=[SEM_SPEC, SEM_SPEC] + [HBM_SPEC] * (ns + nl) + [pl.BlockSpec(memory_space=pltpu.VMEM)],
        out_shape=[pltpu.SemaphoreType.DMA((n_sems,)), pltpu.SemaphoreType.DMA((n_sems,))]
        + [pltpu.HBM(a.shape, a.dtype) for a in arrs] + [jax.ShapeDtypeStruct((8, 128), F32)],
        input_output_aliases={i: 2 + i for i in range(ns + nl)},
        compiler_params=pltpu.CompilerParams(has_side_effects=SIDE_EFFECT),
    )(*[pltpu.with_memory_space_constraint(a, pltpu.HBM) for a in arrs], *after)
    return res[0], res[1], list(res[2:2 + ns]), list(res[2 + ns:2 + ns + nl]), res[-1]


def _copies_wait(name, send, recv, srcs, lands, copies, after):
    ns, nl, nf = len(srcs), len(lands), len(after)

    def body(*refs):
        src, land = refs[:ns], refs[ns:ns + nl]
        send_sems, recv_sems = refs[ns + nl], refs[ns + nl + 1]
        cps = copies(src, land, send_sems, recv_sems, _mesh_place())
        for cp in cps:
            cp.wait_send()
        for cp in cps:
            cp.wait_recv()

    arrs = list(srcs) + list(lands)
    res = pl.pallas_call(
        body,
        name=name,
        in_specs=[HBM_SPEC] * (ns + nl) + [SEM_SPEC, SEM_SPEC] + [ANY_SPEC] * nf,
        out_specs=[HBM_SPEC] * (ns + nl),
        out_shape=[pltpu.HBM(a.shape, a.dtype) for a in arrs],
        input_output_aliases={i: i for i in range(ns + nl)},
        compiler_params=pltpu.CompilerParams(has_side_effects=SIDE_EFFECT),
    )(*arrs, send, recv, *after)
    return list(res[:ns]), list(res[ns:])


def _gather_copies(kinds, halves):
    def copies(src, land, send, recv, place):
        x, y, c, chips = place
        s = 2 * x + y
        mine = [_shard_view(land[t], kinds[t], s, c if halves else None) for t in range(len(kinds))]
        return [
            pltpu.make_async_remote_copy(
                src_ref=mine[t], dst_ref=mine[t], send_sem=send.at[3 * t + j], recv_sem=recv.at[3 * t + j],
                device_id=(px, py, c), device_id_type=MESH)
            for t in range(len(kinds)) for j, (px, py) in enumerate(chips)
        ]

    return copies


def _gather_pass_on(name, lands, kinds):
    nt = len(lands)

    def body(*refs):
        buf = refs[nt:2 * nt]
        send, recv = refs[2 * nt:]
        x, y, c, chips = _mesh_place()
        sib = (x, y, 1 - c)
        sends = []
        for t in range(nt):
            for j, (px, py) in enumerate(chips):
                mine = _shard_view(buf[t], kinds[t], 2 * px + py, c)
                cp = pltpu.make_async_remote_copy(src_ref=mine, dst_ref=mine, send_sem=send.at[t, j],
                                                  recv_sem=recv.at[t, j], device_id=sib, device_id_type=MESH)
                cp.start()
                sends.append(cp)
        for t in range(nt):
            for j, (px, py) in enumerate(chips):
                theirs = _shard_view(buf[t], kinds[t], 2 * px + py, 1 - c)
                pltpu.make_async_remote_copy(src_ref=theirs, dst_ref=theirs, send_sem=send.at[t, j],
                                             recv_sem=recv.at[t, j], device_id=sib, device_id_type=MESH).wait_recv()
        for cp in sends:
            cp.wait_send()

    return pl.pallas_call(
        body,
        name=name,
        in_specs=[HBM_SPEC] * nt,
        out_specs=[HBM_SPEC] * nt,
        out_shape=[jax.ShapeDtypeStruct(a.shape, a.dtype) for a in lands],
        input_output_aliases={i: i for i in range(nt)},
        scratch_shapes=[pltpu.SemaphoreType.DMA((nt, 3))] * 2,
    )(*lands)


def _grad_part(ref, kind, s):
    return ref if kind == "all" else _shard_view(ref, kind, s)


def _grad_copies(kinds):
    def peers(place):
        x, y, c, chips = place
        return [(x, y, 1 - c)] + [(px, py, c) for px, py in chips]

    def copies(src, land, send, recv, place):
        x, y, c, chips = place
        me = 4 * x + 2 * y + c
        return [
            pltpu.make_async_remote_copy(
                src_ref=_grad_part(src[t], kinds[t], 2 * px + py), dst_ref=land[t].at[me],
                send_sem=send.at[GRAD_PEERS * t + k], recv_sem=recv.at[GRAD_PEERS * t + k], device_id=(px, py, pc),
                device_id_type=MESH)
            for t in range(len(kinds)) for k, (px, py, pc) in enumerate(peers(place))
        ]

    return copies


def _pass_copies(n):
    def copies(src, land, send, recv, place):
        x, y, c, chips = place
        return [
            pltpu.make_async_remote_copy(
                src_ref=land[t].at[4 * px + 2 * py + c], dst_ref=land[t].at[4 * px + 2 * py + c],
                send_sem=send.at[3 * t + j], recv_sem=recv.at[3 * t + j], device_id=(x, y, 1 - c),
                device_id_type=MESH)
            for t in range(n) for j, (px, py) in enumerate(chips)
        ]

    return copies


GRAD_PEERS = 4


def _land_shape(grad, kind):
    rows, cols = grad.shape
    if kind == "col":
        return (N_DEV, rows, cols // N_CHIPS)
    if kind == "row":
        return (N_DEV, rows // N_CHIPS, cols)
    return (N_DEV, rows, cols)


def _adamw_reduce(name, contrib, own, kind, ids, w, m, v, layer=None, prev=None):
    rows, cols = w.shape[-2:]
    t = 128 if rows % 128 == 0 else rows
    nb = rows // t
    c1 = 1.0 - ADAM_B1 ** ADAM_STEP
    c2 = 1.0 - ADAM_B2 ** ADAM_STEP

    n_prev = 0 if prev is None else 4

    def body(ids_ref, c_ref, own_ref, w_ref, m_ref, v_ref, *refs):
        g_ref, d_ref, nm_ref, nv_ref = refs[n_prev:]
        me = ids_ref[1]
        mine = own_ref[...].astype(F32)
        g = None
        for q in range(N_DEV):
            term = jnp.where(me == q, mine, c_ref[q].astype(F32))
            g = term if g is None else g + term
        nm = ADAM_B1 * m_ref[...] + (1.0 - ADAM_B1) * g
        nv = ADAM_B2 * v_ref[...] + (1.0 - ADAM_B2) * (g * g)
        g_ref[...] = g
        nm_ref[...] = nm
        nv_ref[...] = nv
        d_ref[...] = -ADAM_LR * ((nm / c1) / (jnp.sqrt(nv / c2) + ADAM_EPS) + ADAM_WD * w_ref[...])

    if layer is None:
        blk = pl.BlockSpec((t, cols), lambda i, ids_ref: (i, 0))
    else:
        blk = pl.BlockSpec((None, t, cols), lambda i, ids_ref: (layer, i, 0))
    if kind == "col":
        own_spec = pl.BlockSpec((t, cols), lambda i, ids_ref: (i, ids_ref[0]))
    elif kind == "row":
        own_spec = pl.BlockSpec((t, cols), lambda i, ids_ref: (ids_ref[0] * nb + i, 0))
    else:
        own_spec = pl.BlockSpec((t, cols), lambda i, ids_ref: (i, 0))
    return pl.pallas_call(
        body,
        name=name,
        grid_spec=pltpu.PrefetchScalarGridSpec(
            num_scalar_prefetch=1,
            grid=(nb,),
            in_specs=[pl.BlockSpec((N_DEV, t, cols), lambda i, ids_ref: (0, i, 0)), own_spec, blk, blk, blk]
            + [ANY_SPEC] * n_prev,
            out_specs=[blk] * 4,
        ),
        out_shape=[jax.ShapeDtypeStruct(w.shape, F32)] * 4,
        input_output_aliases={6 + i: i for i in range(n_prev)},
        compiler_params=pltpu.CompilerParams(dimension_semantics=("parallel",)),
    )(ids, contrib, own, w, m, v, *(prev or ()))


WEIGHT_NAMES = ("a_norm_g", "conv_w1", "conv_b1", "conv_dw", "conv_dw_b", "conv_ln_g", "conv_ln_b", "conv_w2",
                "conv_b2", "kv_norm_g", "w_k", "w_v", "b_norm_g", "w_q", "w_o", "ffn_norm_g", "ffn_w_gate",
                "ffn_w_up", "ffn_w_down", "final_norm_g")
GROUPS = {
    "conv2": (("conv_w2", "conv_w2", None, "row"),),
    "ffn0": (("gate", "ffn_w_gate", 0, "col"), ("up", "ffn_w_up", 0, "col"), ("down", "ffn_w_down", 0, "row")),
    "attn": (("w_k", "w_k", None, "row"), ("w_v", "w_v", None, "row"), ("w_q", "w_q", None, "row"),
             ("w_o", "w_o", None, "row")),
    "ffn1": (("gate", "ffn_w_gate", 1, "col"), ("up", "ffn_w_up", 1, "col"), ("down", "ffn_w_down", 1, "row")),
}
GROUPS["conv1"] = (("conv_w1", "conv_w1", None, "col"),)
for _layer in (0, 1):
    GROUPS[f"ffn{_layer}d"] = tuple(it for it in GROUPS[f"ffn{_layer}"] if it[0] == "down")
    GROUPS[f"ffn{_layer}gu"] = tuple(it for it in GROUPS[f"ffn{_layer}"] if it[0] != "down")
FETCH_ORDER = ("conv1", "conv2", "ffn0", "attn", "ffn1")
HALVED = ("conv1", "conv2", "ffn0", "ffn1")
EMIT_ORDER = ("ffn1d", "ffn1gu", "attn", "ffn0d", "ffn0gu", "conv2", "conv1", "vec")
RETIRE_AT = {"attn": ("ffn1d", "ffn1gu"), "ffn0d": ("attn",), "conv1": ("ffn0d", "ffn0gu", "conv2")}
PACKED = (("a_norm_g", 0, 1), ("conv_b1", 8, 2), ("conv_dw", 16, CONV_WIDTH), ("conv_dw_b", 48, 1),
          ("conv_ln_g", 56, 1), ("conv_ln_b", 64, 1), ("conv_b2", 72, 1))
PACK_ROWS = 80
WHOLE = (("kv_norm_g", 0, 1), ("b_norm_g", 1, 1), ("ffn_norm_g", 2, 2), ("final_norm_g", 4, 1))
WHOLE_ROWS = 8


def _pack_rows(parts, total, width):
    out, at = [], 0
    for arr, first in parts:
        if first > at:
            out.append(jnp.zeros((first - at, width), F32))
        rows8 = -(-arr.shape[0] // 8) * 8
        out.append(jnp.pad(arr, ((0, rows8 - arr.shape[0]), (0, 0))))
        at = first + rows8
    if total > at:
        out.append(jnp.zeros((total - at, width), F32))
    return jnp.concatenate(out, axis=0)


def kernel(x, a_norm_g, conv_w1, conv_b1, conv_dw, conv_dw_b, conv_ln_g, conv_ln_b, conv_w2, conv_b2, kv_norm_g, w_k, w_v, b_norm_g, w_q, w_o, ffn_norm_g, ffn_w_gate, ffn_w_up, ffn_w_down, final_norm_g, loss_target, m_a_norm_g, m_conv_w1, m_conv_b1, m_conv_dw, m_conv_dw_b, m_conv_ln_g, m_conv_ln_b, m_conv_w2, m_conv_b2, m_kv_norm_g, m_w_k, m_w_v, m_b_norm_g, m_w_q, m_w_o, m_ffn_norm_g, m_ffn_w_gate, m_ffn_w_up, m_ffn_w_down, m_final_norm_g, v_a_norm_g, v_conv_w1, v_conv_b1, v_conv_dw, v_conv_dw_b, v_conv_ln_g, v_conv_ln_b, v_conv_w2, v_conv_b2, v_kv_norm_g, v_w_k, v_w_v, v_b_norm_g, v_w_q, v_w_o, v_ffn_norm_g, v_ffn_w_gate, v_ffn_w_up, v_ffn_w_down, v_final_norm_g):
    args = locals()
    wts = {n: args[n] for n in WEIGHT_NAMES}
    mom = {n: args["m_" + n] for n in WEIGHT_NAMES}
    vel = {n: args["v_" + n] for n in WEIGHT_NAMES}
    s, d = x.shape[-2:]
    dq = d // N_CHIPS
    x2 = x.reshape(s, d)
    tgt = loss_target.reshape(s, d)

    def pack_shard(src):
        return _pack_rows([(src[n].reshape(-1, dq), first) for n, first, _ in PACKED], PACK_ROWS, dq)

    def pack_whole(src):
        return _pack_rows([(jnp.concatenate([src[n].reshape(-1, d) for n, _, _ in WHOLE], axis=0), 0)], WHOLE_ROWS, d)

    gathers = {}
    pins = []
    ids = jnp.stack([2 * lax.axis_index("x") + lax.axis_index("y"),
                     4 * lax.axis_index("x") + 2 * lax.axis_index("y") + lax.axis_index("c")]).astype(jnp.int32)

    def start_gather(grp, pins):
        kinds = [kind for _, _, _, kind in GROUPS[grp]]
        lands = []
        for key, n, layer, kind in GROUPS[grp]:
            shard = wts[n] if layer is not None else wts[n].reshape(wts[n].shape[-2:])
            full = jax.ShapeDtypeStruct(_full_shape(shard, kind), BF16)
            lands.append(_place_own(f"gather_place_{grp}_{key}", shard, kind, full, ids, layer))
        copies = _gather_copies(kinds, grp in HALVED)
        send, recv, _, lands, pin = _copies_start("gather_start_" + grp, [], lands, copies, 3 * len(lands), pins)
        gathers[grp] = (send, recv, [], lands, copies, kinds)
        return [pin]

    (packed_full,) = _gather_weights([pack_shard(wts)], ["row"])
    w = {}
    pins = [packed_full]
    for grp in FETCH_ORDER:
        pins = start_gather(grp, pins)

    def fetch(grp, after):
        send, recv, srcs, lands, copies, kinds = gathers[grp]
        _, fulls = _copies_wait("gather_wait_" + grp, send, recv, srcs, lands, copies, [after])
        if grp in HALVED:
            fulls = _gather_pass_on("gather_pass_" + grp, fulls, kinds)
        return {key: full for (key, _, _, _), full in zip(GROUPS[grp], fulls)}

    packed = packed_full.reshape(N_CHIPS, PACK_ROWS, dq)
    for n, first, rows in PACKED:
        part = packed[:, first:first + rows, :]
        if n == "conv_dw":
            w[n] = part.transpose(1, 0, 2).reshape(rows, d)
        else:
            w[n] = part.reshape(1, N_CHIPS * rows * dq)
    for n, _, rows in WHOLE:
        w[n] = wts[n].reshape(rows, d)

    exchanges = {}
    passing = {}
    own_grads = {}

    def retire(tag, after):
        send, recv, srcs, lands, copies, kinds = exchanges[tag]
        srcs, lands = _copies_wait("grads_wait_" + tag, send, recv, srcs, lands, copies, after)
        own_grads[tag] = list(zip(srcs, kinds))
        copies = _pass_copies(len(lands))
        send, recv, _, lands, pin = _copies_start("grads_pass_" + tag, [], lands, copies, 3 * len(lands), ())
        passing[tag] = (send, recv, lands, copies)
        return pin

    def emit_grads(tag, grads, kinds):
        pins = [retire(old, [grads[0]]) for old in RETIRE_AT.get(tag, ())]
        copies = _grad_copies(kinds)
        lands = [lax.empty(_land_shape(gr, kd), gr.dtype) for gr, kd in zip(grads, kinds)]
        send, recv, srcs, lands, pin = _copies_start("grads_start_" + tag, grads, lands, copies,
                                                     GRAD_PEERS * len(grads), pins)
        exchanges[tag] = (send, recv, srcs, lands, copies, kinds)
        return [pin]

    def emit(grp, grads):
        return emit_grads(grp, [grads[key] for key, _, _, _ in GROUPS[grp]], [kind for _, _, _, kind in GROUPS[grp]])

    loss_cols, dx, g = _local_step(x2, tgt, w, fetch, emit, after=pins)
    loss = lax.psum(jnp.sum(loss_cols), ("x", "y", "c"))

    gp = []
    for n, first, rows in PACKED:
        if n == "conv_dw":
            part = g[n].reshape(rows, N_CHIPS, dq).transpose(1, 0, 2)
        else:
            part = g[n].reshape(N_CHIPS, rows, dq)
        gp.append((part, first))
    g_packed = jnp.concatenate(
        [_pack_rows([(p[ci], first) for p, first in gp], PACK_ROWS, dq) for ci in range(N_CHIPS)], axis=0)
    emit_grads("vec", [g_packed, pack_whole(g)], ["row", "all"])
    for tag in EMIT_ORDER:
        if tag not in passing:
            retire(tag, [dx])
    contribs = {}
    for tag in EMIT_ORDER:
        send, recv, lands, copies = passing[tag]
        _, arrived = _copies_wait("grads_passed_" + tag, send, recv, [], lands, copies, [dx])
        contribs[tag] = [(c, own, kind) for c, (own, kind) in zip(arrived, own_grads[tag])]

    res = {}

    def adamw(n, contrib, layer=None, prev=None):
        arrived, own, kind = contrib
        if layer is None:
            shape = wts[n].shape
            r2 = shape[-2:]
            outs = _adamw_reduce("adamw_" + n, arrived, own, kind, ids, wts[n].reshape(r2), mom[n].reshape(r2),
                                 vel[n].reshape(r2))
            return [o.reshape(shape) for o in outs]
        return _adamw_reduce(f"adamw_{n}_{layer}", arrived, own, kind, ids, wts[n], mom[n], vel[n], layer, prev)

    for grp in ("attn", "conv2", "conv1"):
        for (key, n, _, _), contrib in zip(GROUPS[grp], contribs[grp]):
            res[n] = adamw(n, contrib)
    for part in ("d", "gu"):
        for (key, n, _, _), c0, c1 in zip(GROUPS["ffn0" + part], contribs["ffn0" + part], contribs["ffn1" + part]):
            res[n] = adamw(n, c1, 1, adamw(n, c0, 0))
    outs = _adamw_reduce("adamw_packed", *contribs["vec"][0], ids, pack_shard(wts), pack_shard(mom), pack_shard(vel))
    for n, first, rows in PACKED:
        res[n] = [o[first:first + rows].reshape(wts[n].shape) for o in outs]
    outs = _adamw_reduce("adamw_whole", *contribs["vec"][1], ids, pack_whole(wts), pack_whole(mom), pack_whole(vel))
    for n, first, rows in WHOLE:
        res[n] = [o[first:first + rows].reshape(wts[n].shape) for o in outs]

    out = [loss, dx.reshape(x.shape)]
    for which in range(4):
        out += [res[n][which] for n in WEIGHT_NAMES]
    return tuple(out)
```

```python
import functools
import math

import jax
import jax.numpy as jnp
from jax import lax
from jax.experimental import pallas as pl
from jax.experimental.pallas import tpu as pltpu

F32 = jnp.float32
BF16 = jnp.bfloat16

HEAD_DIM = 128
BRANCHES = ((128, 1), (512, 4), (2048, 16))
CONV_WIDTH = 31
CONV_HALO = 32
RMS_EPS = 1e-6
LN_EPS = 1e-5
ADAM_LR = 0.001
ADAM_B1 = 0.9
ADAM_B2 = 0.999
ADAM_EPS = 1e-08
ADAM_WD = 0.01
ADAM_STEP = 10
N_CHIPS = 4
N_DEV = 8
MESH = pl.DeviceIdType.MESH


def _sigmoid(x):
    return 0.5 * jnp.tanh(0.5 * x) + 0.5


def _row_tile(rows, want):
    t = min(rows, want)
    assert rows % t == 0, (rows, want)
    return t


_DOT_DIMS = {"nn": ((1,), (0,)), "nt": ((1,), (1,)), "tn": ((0,), (0,))}


ANY_SPEC = pl.BlockSpec(memory_space=pl.ANY)
WIDE_BN = 1024


def _mm(name, mode, a, bs, epilogue, outs, *, m, n, k, extras=(), bm=1024, bn=512, bk=None, after=()):
    bm, bn = min(bm, m), min(bn, n)
    bk = k if bk is None else min(bk, k)
    assert m % bm == 0 and n % bn == 0 and k % bk == 0, (name, m, n, k, bm, bn, bk)
    nk = k // bk
    a_list = list(a) if isinstance(a, (list, tuple)) else [a]
    na, nb, ne, no = len(a_list), len(bs), len(extras), len(outs)
    assert na in (1, nb)

    if mode == "tn":
        a_spec = pl.BlockSpec((bk, bm), lambda i, j, kk: (kk, i))
    else:
        a_spec = pl.BlockSpec((bm, bk), lambda i, j, kk: (i, kk))

    def b_spec(lead, off):
        if mode == "nt":
            blk, idx = (bn, bk), (lambda i, j, kk: (j + off, kk))
        else:
            blk, idx = (bk, bn), (lambda i, j, kk: (kk, j + off))
        if lead is None:
            return pl.BlockSpec(blk, idx)
        return pl.BlockSpec((None,) + blk, lambda i, j, kk: (lead,) + idx(i, j, kk))

    def e_spec(kind, off):
        if kind == "mn":
            return pl.BlockSpec((bm, bn), lambda i, j, kk: (i, j + off))
        return pl.BlockSpec((1, bn), lambda i, j, kk: (0, j + off))

    nf = len(after)
    in_specs = [a_spec] * na + [b_spec(l, o) for _, l, o in bs] + [e_spec(kd, o) for _, kd, o in extras]
    in_specs += [ANY_SPEC] * nf
    out_specs = [pl.BlockSpec((bm, bn), lambda i, j, kk: (i, j)) for _ in outs]
    out_shape = [jax.ShapeDtypeStruct((m, n), dt) for (dt,) in outs]
    dims = (_DOT_DIMS[mode], ((), ()))

    def body(*refs):
        a_refs = refs[:na]
        b_refs = refs[na:na + nb]
        e_refs = refs[na + nb:na + nb + ne]
        o_refs = refs[na + nb + ne + nf:na + nb + ne + nf + no]
        acc_refs = refs[na + nb + ne + nf + no:]
        avs = [a_ref[...].astype(BF16) for a_ref in a_refs]
        prods = [lax.dot_general(avs[bi % na], b_ref[...].astype(BF16), dims, preferred_element_type=F32)
                 for bi, b_ref in enumerate(b_refs)]

        def finish(accs):
            res = epilogue(accs, [e_ref[...] for e_ref in e_refs])
            for o_ref, r in zip(o_refs, res):
                o_ref[...] = r.astype(o_ref.dtype)

        if nk == 1:
            finish(prods)
        else:
            kk = pl.program_id(2)

            @pl.when(kk == 0)
            def _():
                for acc_ref, p in zip(acc_refs, prods):
                    acc_ref[...] = p

            @pl.when(kk > 0)
            def _():
                for acc_ref, p in zip(acc_refs, prods):
                    acc_ref[...] += p

            @pl.when(kk == nk - 1)
            def _():
                finish([acc_ref[...] for acc_ref in acc_refs])

    scratch = [] if nk == 1 else [pltpu.VMEM((bm, bn), F32) for _ in bs]
    res = pl.pallas_call(
        body,
        name=name,
        grid=(m // bm, n // bn, nk),
        in_specs=in_specs,
        out_specs=out_specs,
        out_shape=out_shape,
        scratch_shapes=scratch,
        compiler_params=pltpu.CompilerParams(dimension_semantics=("parallel", "parallel", "arbitrary")),
    )(*a_list, *[b for b, _, _ in bs], *[e for e, _, _ in extras], *after)
    return res


def _rms_fwd(name, x, gains, after=()):
    s, d = x.shape
    ng = gains.shape[0]
    t = _row_tile(s, 256)
    nf = len(after)

    def body(x_ref, g_ref, *refs):
        o_refs = refs[nf:]
        xv = x_ref[...]
        r = lax.rsqrt(jnp.mean(xv * xv, axis=-1, keepdims=True) + RMS_EPS)
        xh = xv * r
        for gi, o_ref in enumerate(o_refs):
            o_ref[...] = (xh * g_ref[gi:gi + 1, :]).astype(o_ref.dtype)

    return pl.pallas_call(
        body,
        name=name,
        grid=(s // t,),
        in_specs=[pl.BlockSpec((t, d), lambda i: (i, 0)), pl.BlockSpec((ng, d), lambda i: (0, 0))] + [ANY_SPEC] * nf,
        out_specs=[pl.BlockSpec((t, d), lambda i: (i, 0)) for _ in range(ng)],
        out_shape=[jax.ShapeDtypeStruct((s, d), BF16) for _ in range(ng)],
        compiler_params=pltpu.CompilerParams(dimension_semantics=("parallel",)),
    )(x, gains, *after)


def _rms_bwd(name, x, gains, dns, dres):
    s, d = x.shape
    ng = gains.shape[0]
    t = _row_tile(s, 256)

    def body(x_ref, g_ref, dres_ref, *refs):
        dn_refs = refs[:ng]
        dx_ref, dxb_ref, dg_ref, cs_ref = refs[ng:]
        i = pl.program_id(0)
        xv = x_ref[...]
        r = lax.rsqrt(jnp.mean(xv * xv, axis=-1, keepdims=True) + RMS_EPS)
        xh = xv * r
        dx = dres_ref[...]
        dgs = []
        for gi in range(ng):
            dn = dn_refs[gi][...].astype(F32)
            dxh = dn * g_ref[gi:gi + 1, :]
            dgs.append(jnp.sum(dn * xh, axis=0, keepdims=True))
            dx = dx + r * (dxh - xh * jnp.mean(dxh * xh, axis=-1, keepdims=True))
        dx_ref[...] = dx
        dxb_ref[...] = dx.astype(BF16)
        dg = jnp.concatenate(dgs, axis=0) if ng > 1 else dgs[0]
        cs = jnp.sum(dx, axis=0, keepdims=True)

        @pl.when(i == 0)
        def _():
            dg_ref[...] = dg
            cs_ref[...] = cs

        @pl.when(i > 0)
        def _():
            dg_ref[...] += dg
            cs_ref[...] += cs

    row = pl.BlockSpec((t, d), lambda i: (i, 0))
    return pl.pallas_call(
        body,
        name=name,
        grid=(s // t,),
        in_specs=[row, pl.BlockSpec((ng, d), lambda i: (0, 0)), row] + [row] * ng,
        out_specs=[row, row, pl.BlockSpec((ng, d), lambda i: (0, 0)), pl.BlockSpec((1, d), lambda i: (0, 0))],
        out_shape=[
            jax.ShapeDtypeStruct((s, d), F32),
            jax.ShapeDtypeStruct((s, d), BF16),
            jax.ShapeDtypeStruct((ng, d), F32),
            jax.ShapeDtypeStruct((1, d), F32),
        ],
        compiler_params=pltpu.CompilerParams(dimension_semantics=("arbitrary",)),
    )(x, gains, dres, *dns)


def _final_loss(name, h, gain, target):
    s, d = h.shape
    t = _row_tile(s, 256)

    def body(h_ref, g_ref, t_ref, dh_ref, dhb_ref, dg_ref, ls_ref):
        i = pl.program_id(0)
        xv = h_ref[...]
        g = g_ref[...]
        r = lax.rsqrt(jnp.mean(xv * xv, axis=-1, keepdims=True) + RMS_EPS)
        xh = xv * r
        err = xh * g - t_ref[...]
        ls = jnp.sum(err * err, axis=0, keepdims=True) * (0.5 / d)
        dy = err * (1.0 / d)
        dxh = dy * g
        dg = jnp.sum(dy * xh, axis=0, keepdims=True)
        dx = r * (dxh - xh * jnp.mean(dxh * xh, axis=-1, keepdims=True))
        dh_ref[...] = dx
        dhb_ref[...] = dx.astype(BF16)

        @pl.when(i == 0)
        def _():
            dg_ref[...] = dg
            ls_ref[...] = ls

        @pl.when(i > 0)
        def _():
            dg_ref[...] += dg
            ls_ref[...] += ls

    row = pl.BlockSpec((t, d), lambda i: (i, 0))
    vec = pl.BlockSpec((1, d), lambda i: (0, 0))
    return pl.pallas_call(
        body,
        name=name,
        grid=(s // t,),
        in_specs=[row, vec, row],
        out_specs=[row, row, vec, vec],
        out_shape=[
            jax.ShapeDtypeStruct((s, d), F32),
            jax.ShapeDtypeStruct((s, d), BF16),
            jax.ShapeDtypeStruct((1, d), F32),
            jax.ShapeDtypeStruct((1, d), F32),
        ],
        compiler_params=pltpu.CompilerParams(dimension_semantics=("arbitrary",)),
    )(h, gain, target)


SUBLANES = 8
CONV_LANES = 512
CONV_ROWS = 32
NORM_ROWS = 16


def _conv_tiles(s):
    t = _row_tile(s, 128)
    assert t % CONV_HALO == 0 and t % CONV_ROWS == 0
    return t, t // CONV_HALO


def _shifted_copies(dst, src, lanes, rows):
    for m in range(SUBLANES):
        n = rows if m == 0 else rows - SUBLANES
        dst[m, :n, :] = src[m:m + n, lanes]


def _shifted(copies, offset, n):
    m = offset % SUBLANES
    return copies[m, offset - m:offset - m + n, :]


def _dwconv_fwd(name, u, dw, dw_b, ln_g, ln_b):
    s, d = u.shape
    t, hb = _conv_tiles(s)
    w = dw.shape[0]
    lo = CONV_HALO - (w - 1)

    lw = min(CONV_LANES, d)

    def body(cur_ref, prev_ref, dw_ref, dwb_ref, lg_ref, lb_ref, c_ref, sw_ref, cat_ref, sh_ref):
        i = pl.program_id(0)
        cat_ref[CONV_HALO:, :] = cur_ref[...]

        @pl.when(i == 0)
        def _():
            cat_ref[:CONV_HALO, :] = jnp.zeros((CONV_HALO, d), F32)

        @pl.when(i > 0)
        def _():
            cat_ref[:CONV_HALO, :] = prev_ref[...]

        for lc in range(d // lw):
            lanes = slice(lc * lw, (lc + 1) * lw)
            _shifted_copies(sh_ref, cat_ref, lanes, t + CONV_HALO)
            for rc in range(t // CONV_ROWS):
                acc = jnp.broadcast_to(dwb_ref[:, lanes], (CONV_ROWS, lw))
                for kk in range(w):
                    acc = acc + dw_ref[kk:kk + 1, lanes] * _shifted(sh_ref, lo + kk + rc * CONV_ROWS, CONV_ROWS)
                c_ref[rc * CONV_ROWS:(rc + 1) * CONV_ROWS, lanes] = acc

        def norm_rows(ri, carry):
            rows = pl.ds(pl.multiple_of(ri * NORM_ROWS, NORM_ROWS), NORM_ROWS)
            cv = c_ref[rows, :]
            cc = cv - jnp.mean(cv, axis=-1, keepdims=True)
            var = jnp.mean(cc * cc, axis=-1, keepdims=True)
            ln = cc * lax.rsqrt(var + LN_EPS) * lg_ref[...] + lb_ref[...]
            sw_ref[rows, :] = (ln * _sigmoid(ln)).astype(BF16)
            return carry

        lax.fori_loop(0, t // NORM_ROWS, norm_rows, 0)

    row = pl.BlockSpec((t, d), lambda i: (i, 0))
    prev = pl.BlockSpec((CONV_HALO, d), lambda i: (jnp.maximum(i * hb - 1, 0), 0))
    vec = pl.BlockSpec((1, d), lambda i: (0, 0))
    return pl.pallas_call(
        body,
        name=name,
        grid=(s // t,),
        in_specs=[row, prev, pl.BlockSpec((w, d), lambda i: (0, 0)), vec, vec, vec],
        out_specs=[row, row],
        out_shape=[jax.ShapeDtypeStruct((s, d), F32), jax.ShapeDtypeStruct((s, d), BF16)],
        scratch_shapes=[pltpu.VMEM((CONV_HALO + t, d), F32), pltpu.VMEM((SUBLANES, CONV_HALO + t, lw), F32)],
        compiler_params=pltpu.CompilerParams(dimension_semantics=("parallel",)),
    )(u, u, dw, dw_b, ln_g, ln_b)


def _conv_ln_bwd(name, c, dsw, ln_g, ln_b, after=()):
    s, d = c.shape
    t = _row_tile(s, 256)
    nf = len(after)

    def body(c_ref, dsw_ref, lg_ref, lb_ref, *refs):
        dc_ref, sums_ref = refs[nf:]
        i = pl.program_id(0)
        cv = c_ref[...]
        g = lg_ref[...]
        mu = jnp.mean(cv, axis=-1, keepdims=True)
        cc = cv - mu
        rstd = lax.rsqrt(jnp.mean(cc * cc, axis=-1, keepdims=True) + LN_EPS)
        ch = cc * rstd
        ln = ch * g + lb_ref[...]
        sg = _sigmoid(ln)
        dln = dsw_ref[...] * (sg * (1.0 + ln * (1.0 - sg)))
        dch = dln * g
        dc = rstd * (dch - jnp.mean(dch, axis=-1, keepdims=True) - ch * jnp.mean(dch * ch, axis=-1, keepdims=True))
        dc_ref[...] = dc
        sums = jnp.concatenate(
            [
                jnp.sum(dln * ch, axis=0, keepdims=True),
                jnp.sum(dln, axis=0, keepdims=True),
                jnp.sum(dc, axis=0, keepdims=True),
                jnp.zeros((1, d), F32),
            ],
            axis=0,
        )

        @pl.when(i == 0)
        def _():
            sums_ref[...] = sums

        @pl.when(i > 0)
        def _():
            sums_ref[...] += sums

    row = pl.BlockSpec((t, d), lambda i: (i, 0))
    vec = pl.BlockSpec((1, d), lambda i: (0, 0))
    return pl.pallas_call(
        body,
        name=name,
        grid=(s // t,),
        in_specs=[row, row, vec, vec] + [ANY_SPEC] * nf,
        out_specs=[row, pl.BlockSpec((4, d), lambda i: (0, 0))],
        out_shape=[jax.ShapeDtypeStruct((s, d), F32), jax.ShapeDtypeStruct((4, d), F32)],
        compiler_params=pltpu.CompilerParams(dimension_semantics=("arbitrary",)),
    )(c, dsw, ln_g, ln_b, *after)


def _conv_dw_bwd(name, dc, u, a, gt, dw):
    s, d = dc.shape
    t, hb = _conv_tiles(s)
    w = dw.shape[0]
    lo = CONV_HALO - (w - 1)
    nt = s // t
    lw = min(CONV_LANES, d)

    def body(dc_ref, dcn_ref, u_ref, up_ref, a_ref, gt_ref, dw_ref, dpre_ref, ddw_ref, db_ref, dcat_ref, ucat_ref,
             dsh_ref, ush_ref, ddw_acc, db_acc):
        i = pl.program_id(0)
        dcat_ref[:t, :] = dc_ref[...]
        ucat_ref[CONV_HALO:, :] = u_ref[...]

        @pl.when(i == nt - 1)
        def _():
            dcat_ref[t:, :] = jnp.zeros((CONV_HALO, d), F32)

        @pl.when(i < nt - 1)
        def _():
            dcat_ref[t:, :] = dcn_ref[...]

        @pl.when(i == 0)
        def _():
            ucat_ref[:CONV_HALO, :] = jnp.zeros((CONV_HALO, d), F32)

        @pl.when(i > 0)
        def _():
            ucat_ref[:CONV_HALO, :] = up_ref[...]

        @pl.when(i == 0)
        def _():
            ddw_acc[...] = jnp.zeros(ddw_acc.shape, F32)
            db_acc[...] = jnp.zeros(db_acc.shape, F32)

        def fold(p):
            return functools.reduce(jnp.add, [p[j:j + SUBLANES] for j in range(0, CONV_ROWS, SUBLANES)])

        for lc in range(d // lw):
            lanes = slice(lc * lw, (lc + 1) * lw)
            gate_lanes = slice(d + lc * lw, d + (lc + 1) * lw)
            _shifted_copies(dsh_ref, dcat_ref, lanes, t + CONV_HALO)
            _shifted_copies(ush_ref, ucat_ref, lanes, t + CONV_HALO)
            for rc in range(t // CONV_ROWS):
                r0 = rc * CONV_ROWS
                rows = slice(r0, r0 + CONV_ROWS)
                dcv = dc_ref[rows, lanes]
                du = jnp.zeros((CONV_ROWS, lw), F32)
                for kk in range(w):
                    du = du + dw_ref[kk:kk + 1, lanes] * _shifted(dsh_ref, w - 1 - kk + r0, CONV_ROWS)
                    ddw_acc[kk, :, lanes] += fold(dcv * _shifted(ush_ref, lo + kk + r0, CONV_ROWS))
                av = a_ref[rows, lanes].astype(F32)
                sg = _sigmoid(gt_ref[rows, lanes].astype(F32))
                da = du * sg
                dgt = du * av * sg * (1.0 - sg)
                dpre_ref[rows, lanes] = da.astype(BF16)
                dpre_ref[rows, gate_lanes] = dgt.astype(BF16)
                db_acc[:, lanes] += fold(da)
                db_acc[:, gate_lanes] += fold(dgt)

        @pl.when(i == nt - 1)
        def _():
            ddw_ref[...] = jnp.sum(ddw_acc[...], axis=1)
            db_ref[...] = jnp.sum(db_acc[...], axis=0, keepdims=True)

    row = pl.BlockSpec((t, d), lambda i: (i, 0))
    nxt = pl.BlockSpec((CONV_HALO, d), lambda i: (jnp.minimum((i + 1) * hb, s // CONV_HALO - 1), 0))
    prev = pl.BlockSpec((CONV_HALO, d), lambda i: (jnp.maximum(i * hb - 1, 0), 0))
    return pl.pallas_call(
        body,
        name=name,
        grid=(nt,),
        in_specs=[row, nxt, row, prev, row, row, pl.BlockSpec((w, d), lambda i: (0, 0))],
        out_specs=[
            pl.BlockSpec((t, 2 * d), lambda i: (i, 0)),
            pl.BlockSpec((w, d), lambda i: (0, 0)),
            pl.BlockSpec((1, 2 * d), lambda i: (0, 0)),
        ],
        out_shape=[
            jax.ShapeDtypeStruct((s, 2 * d), BF16),
            jax.ShapeDtypeStruct((w, d), F32),
            jax.ShapeDtypeStruct((1, 2 * d), F32),
        ],
        scratch_shapes=[pltpu.VMEM((t + CONV_HALO, d), F32), pltpu.VMEM((CONV_HALO + t, d), F32)]
        + [pltpu.VMEM((SUBLANES, CONV_HALO + t, lw), F32)] * 2
        + [pltpu.VMEM((w, SUBLANES, d), F32), pltpu.VMEM((SUBLANES, 2 * d), F32)],
        compiler_params=pltpu.CompilerParams(dimension_semantics=("arbitrary",)),
    )(dc, dc, u, u, a, gt, dw)


def _alibi_slopes(n_heads):
    h = jnp.arange(1, n_heads + 1, dtype=F32)
    return jnp.exp2(-8.0 * h / n_heads)


def _band_masks(bq):
    qi = lax.broadcasted_iota(jnp.int32, (bq, bq), 0)
    kj = lax.broadcasted_iota(jnp.int32, (bq, bq), 1)
    return qi - kj, qi - kj + bq


ATTN_GROUP = 16


def _attn_fwd(name, q, k, v, slopes, bq):
    s, dm = q.shape
    nh = dm // HEAD_DIM
    nt = s // bq
    nbr = len(BRANCHES)
    scale = HEAD_DIM ** -0.5
    nt_dims = (((1,), (1,)), ((), ()))

    def body(sl_ref, q_ref, k_ref, v_ref, o_ref, ob_ref, l_ref, tmp, qr, kr, va, orm, lrm, onat, lnat, sbuf, mbuf):
        slope = sl_ref[pl.program_id(0)]
        jc, jp = _band_masks(bq)
        va[:, HEAD_DIM:] = jnp.ones((s, HEAD_DIM), BF16)
        for bi, (win, dil) in enumerate(BRANCHES):
            ll = s // dil
            nblk = ll // bq
            if dil == 1:
                sq, sk = q_ref, k_ref
                va[:, :HEAD_DIM] = v_ref[...]
                d_o, d_l = onat.at[bi], lnat.at[bi]
            else:
                for src, dst, wide in ((q_ref, qr, False), (k_ref, kr, False), (v_ref, va, True)):
                    tmp[...] = src[...].astype(F32)
                    for r in range(dil):
                        part = tmp[pl.ds(r, ll, stride=dil), :].astype(BF16)
                        if wide:
                            dst[r * ll:(r + 1) * ll, :HEAD_DIM] = part
                        else:
                            dst[r * ll:(r + 1) * ll, :] = part
                sq, sk = qr, kr
                d_o, d_l = orm, lrm
            bias_c = jnp.where(jc >= 0, jc.astype(F32) * (slope * dil), 1e30)
            bias_p = jnp.where(jp <= bq, jp.astype(F32) * (slope * dil), 1e30)

            def group(gi, carry):
                rows = []
                for g in range(ATTN_GROUP):
                    ti = gi * ATTN_GROUP + g
                    row = pl.ds(pl.multiple_of(ti * bq, bq), bq)
                    prow = pl.ds(pl.multiple_of(jnp.maximum(ti - 1, 0) * bq, bq), bq)
                    rows.append((row, prow))
                    qh = sq[row, :]
                    sc = lax.dot_general(qh, sk[row, :], nt_dims, preferred_element_type=F32) * scale - bias_c
                    sp = lax.dot_general(qh, sk[prow, :], nt_dims, preferred_element_type=F32) * scale - bias_p
                    sp = jnp.where(lax.rem(ti, nblk) > 0, sp, -1e30)
                    sbuf[g, :, :bq] = sc
                    sbuf[g, :, bq:] = sp
                    mbuf[g] = jnp.maximum(jnp.max(sc, axis=-1, keepdims=True), jnp.max(sp, axis=-1, keepdims=True))
                for g, (row, prow) in enumerate(rows):
                    mx = mbuf[g]
                    p = jnp.exp(sbuf[g] - mx).astype(BF16)
                    ov = jnp.dot(p[:, :bq], va[row, :], preferred_element_type=F32)
                    ov = ov + jnp.dot(p[:, bq:], va[prow, :], preferred_element_type=F32)
                    den = ov[:, HEAD_DIM:]
                    d_o[row, :] = ov[:, :HEAD_DIM] / den
                    d_l[row, :] = mx + jnp.log(den)
                return carry

            lax.fori_loop(0, nt // ATTN_GROUP, group, 0)
            if dil > 1:
                for r in range(dil):
                    onat[bi, pl.ds(r, ll, stride=dil), :] = orm[r * ll:(r + 1) * ll, :]
                    lnat[bi, pl.ds(r, ll, stride=dil), :] = lrm[r * ll:(r + 1) * ll, :]

        def merge(ti, carry):
            rows = pl.ds(pl.multiple_of(ti * bq, bq), bq)
            ls = [lnat[bi, rows, :] for bi in range(nbr)]
            mx = functools.reduce(jnp.maximum, ls)
            es = [jnp.exp(l - mx) for l in ls]
            tot = functools.reduce(jnp.add, es)
            inv = 1.0 / tot
            o = functools.reduce(jnp.add, [e * inv * onat[bi, rows, :] for bi, e in enumerate(es)])
            o_ref[rows, :] = o
            ob_ref[rows, :] = o.astype(BF16)
            l_ref[rows, :] = mx + jnp.log(tot)
            return carry

        lax.fori_loop(0, nt, merge, 0)

    head = pl.BlockSpec((s, HEAD_DIM), lambda h: (0, h))
    return pl.pallas_call(
        body,
        name=name,
        grid=(nh,),
        in_specs=[pl.BlockSpec(memory_space=pltpu.SMEM), head, head, head],
        out_specs=[head, head, head],
        out_shape=[jax.ShapeDtypeStruct((s, dm), F32), jax.ShapeDtypeStruct((s, dm), BF16),
                   jax.ShapeDtypeStruct((s, dm), F32)],
        scratch_shapes=[pltpu.VMEM((s, HEAD_DIM), F32)] + [pltpu.VMEM((s, HEAD_DIM), BF16)] * 2
        + [pltpu.VMEM((s, 2 * HEAD_DIM), BF16)] + [pltpu.VMEM((s, HEAD_DIM), F32)] * 2
        + [pltpu.VMEM((nbr, s, HEAD_DIM), F32)] * 2
        + [pltpu.VMEM((ATTN_GROUP, bq, 2 * bq), F32), pltpu.VMEM((ATTN_GROUP, bq, 1), F32)],
        compiler_params=pltpu.CompilerParams(dimension_semantics=("parallel",)),
    )(slopes, q, k, v)


def _attn_bwd(name, q, k, v, o, lse, do, slopes, bq):
    s, dm = q.shape
    nh = dm // HEAD_DIM
    nt = s // bq
    scale = HEAD_DIM ** -0.5
    nt_dims = (((1,), (1,)), ((), ()))
    tn_dims = (((0,), (0,)), ((), ()))

    def body(sl_ref, q_ref, k_ref, v_ref, o_ref, l_ref, do_ref, dq_ref, dk_ref, dv_ref,
             tmp, qr, kr, vr, dor, lr, dlr, dln, dqr, dkr, dvr, aq, ak, av, pbuf, dsbuf):
        slope = sl_ref[pl.program_id(0)]
        jc, jp = _band_masks(bq)

        def delta(ti, carry):
            rows = pl.ds(pl.multiple_of(ti * bq, bq), bq)
            dl = jnp.sum(do_ref[rows, :].astype(F32) * o_ref[rows, :], axis=-1, keepdims=True)
            dln[rows, :] = jnp.broadcast_to(dl, (bq, HEAD_DIM))
            return carry

        lax.fori_loop(0, nt, delta, 0)

        for bi, (win, dil) in enumerate(BRANCHES):
            ll = s // dil
            nblk = ll // bq
            if dil == 1:
                sq, sk, sv, sdo, sl, sdl = q_ref, k_ref, v_ref, do_ref, l_ref, dln
                gq, gk, gv = aq, ak, av
            else:
                for src, dst in ((q_ref, qr), (k_ref, kr), (v_ref, vr), (do_ref, dor)):
                    tmp[...] = src[...].astype(F32)
                    for r in range(dil):
                        dst[r * ll:(r + 1) * ll, :] = tmp[pl.ds(r, ll, stride=dil), :].astype(BF16)
                for r in range(dil):
                    lr[r * ll:(r + 1) * ll, :] = l_ref[pl.ds(r, ll, stride=dil), :]
                    dlr[r * ll:(r + 1) * ll, :] = dln[pl.ds(r, ll, stride=dil), :]
                sq, sk, sv, sdo, sl, sdl = qr, kr, vr, dor, lr, dlr
                gq, gk, gv = dqr, dkr, dvr
            bias_c = jnp.where(jc >= 0, jc.astype(F32) * (slope * dil), 1e30)
            bias_p = jnp.where(jp <= bq, jp.astype(F32) * (slope * dil), 1e30)

            def group(gi, carry):
                rows = []
                for g in range(ATTN_GROUP):
                    ti = gi * ATTN_GROUP + g
                    row = pl.ds(pl.multiple_of(ti * bq, bq), bq)
                    prow = pl.ds(pl.multiple_of(jnp.maximum(ti - 1, 0) * bq, bq), bq)
                    rows.append((row, prow))
                    has_prev = lax.rem(ti, nblk) > 0
                    qh, doh = sq[row, :], sdo[row, :]
                    lc = sl[row, :][:, :1]
                    dl = sdl[row, :][:, :1]
                    for half, kv_rows, bias in ((0, row, bias_c), (1, prow, bias_p)):
                        sc = lax.dot_general(qh, sk[kv_rows, :], nt_dims, preferred_element_type=F32) * scale - bias
                        if half:
                            sc = jnp.where(has_prev, sc, -1e30)
                        p = jnp.exp(sc - lc)
                        dp = lax.dot_general(doh, sv[kv_rows, :], nt_dims, preferred_element_type=F32)
                        pbuf[g, :, half * bq:(half + 1) * bq] = p.astype(BF16)
                        dsbuf[g, :, half * bq:(half + 1) * bq] = (p * (dp - dl) * scale).astype(BF16)
                carry_k = carry_v = None
                for g, (row, prow) in enumerate(rows):
                    qh, doh = sq[row, :], sdo[row, :]
                    ds_c, ds_p = dsbuf[g, :, :bq], dsbuf[g, :, bq:]
                    p_c, p_p = pbuf[g, :, :bq], pbuf[g, :, bq:]
                    dq = jnp.dot(ds_c, sk[row, :], preferred_element_type=F32)
                    dq = dq + jnp.dot(ds_p, sk[prow, :], preferred_element_type=F32)
                    gq[row, :] = dq
                    dk_p = lax.dot_general(ds_p, qh, tn_dims, preferred_element_type=F32)
                    dv_p = lax.dot_general(p_p, doh, tn_dims, preferred_element_type=F32)
                    if g == 0:
                        @pl.when(gi > 0)
                        def _():
                            gk[prow, :] += dk_p
                            gv[prow, :] += dv_p
                    else:
                        gk[rows[g - 1][0], :] = carry_k + dk_p
                        gv[rows[g - 1][0], :] = carry_v + dv_p
                    carry_k = lax.dot_general(ds_c, qh, tn_dims, preferred_element_type=F32)
                    carry_v = lax.dot_general(p_c, doh, tn_dims, preferred_element_type=F32)
                gk[rows[-1][0], :] = carry_k
                gv[rows[-1][0], :] = carry_v
                return carry

            lax.fori_loop(0, nt // ATTN_GROUP, group, 0)
            if dil > 1:
                for acc, rm in ((aq, dqr), (ak, dkr), (av, dvr)):
                    for r in range(dil):
                        acc[pl.ds(r, ll, stride=dil), :] += rm[r * ll:(r + 1) * ll, :]

        dq_ref[...] = aq[...].astype(BF16)
        dk_ref[...] = ak[...].astype(BF16)
        dv_ref[...] = av[...].astype(BF16)

    head = pl.BlockSpec((s, HEAD_DIM), lambda h: (0, h))
    f32buf = pltpu.VMEM((s, HEAD_DIM), F32)
    b16buf = pltpu.VMEM((s, HEAD_DIM), BF16)
    return pl.pallas_call(
        body,
        name=name,
        grid=(nh,),
        in_specs=[pl.BlockSpec(memory_space=pltpu.SMEM)] + [head] * 6,
        out_specs=[head] * 3,
        out_shape=[jax.ShapeDtypeStruct((s, dm), BF16)] * 3,
        scratch_shapes=[f32buf] + [b16buf] * 4 + [f32buf] * 9 + [pltpu.VMEM((ATTN_GROUP, bq, 2 * bq), BF16)] * 2,
        compiler_params=pltpu.CompilerParams(dimension_semantics=("parallel",)),
    )(slopes, q, k, v, o, lse, do)


def _ep_id(accs, ex):
    return [accs[0]]


def _ep_all(accs, ex):
    return list(accs)


def _ep_sum(accs, ex):
    return [accs[0] + accs[1]]


def _ep_add(accs, ex):
    return [accs[0] + ex[0].astype(F32)]


def _ep_bias_res(accs, ex):
    return [accs[0] + ex[0] + ex[1]]


def _ep_glu(accs, ex):
    a = accs[0] + ex[0]
    gt = accs[1] + ex[1]
    return [a * _sigmoid(gt), a, gt]


def _ep_swiglu(accs, ex):
    g, u = accs
    return [g, u, g * _sigmoid(g) * u]


def _ep_swiglu_bwd(accs, ex):
    dact = accs[0]
    g = ex[0].astype(F32)
    u = ex[1].astype(F32)
    sg = _sigmoid(g)
    return [dact * u * (sg * (1.0 + g * (1.0 - sg))), dact * g * sg]


def _ffn_fwd(tag, h, gain, get_gate_up, get_down):
    s, d = h.shape
    (n,) = _rms_fwd(f"{tag}_norm", h, gain)
    wg, wu = get_gate_up(n)
    f = wg.shape[-1]
    g, u, act = _mm(f"{tag}_gate_up", "nn", n, [(wg, None, 0), (wu, None, 0)], _ep_swiglu,
                    [(BF16,), (BF16,), (BF16,)], m=s, n=f, k=d)
    wd = get_down(act)
    (out,) = _mm(f"{tag}_down", "nn", act, [(wd, None, 0)], _ep_add, [(F32,)], m=s, n=d, k=f,
                 extras=[(h, "mn", 0)], bn=256)
    return out, (n, g, u, act), dict(gate=wg, up=wu, down=wd)


def _ffn_bwd(tag, h_in, gain, wg, wu, wd, saved, dh, dhb, after, emit):
    s, d = h_in.shape
    f = wg.shape[-1]
    n, g, u, act = saved
    dg, du = _mm(f"{tag}_bwd_dact", "nt", dhb, [(wd, None, 0)], _ep_swiglu_bwd, [(BF16,), (BF16,)],
                 m=s, n=f, k=d, extras=[(g, "mn", 0), (u, "mn", 0)], after=after)
    (dwd,) = _mm(f"{tag}_bwd_dwd", "tn", act, [(dhb, None, 0)], _ep_id, [(BF16,)], m=f, n=d, k=s,
                 bm=f // 4)
    pin = emit("d", dict(down=dwd))
    dwg, dwu = _mm(f"{tag}_bwd_dwgu", "tn", n, [(dg, None, 0), (du, None, 0)], _ep_all, [(BF16,), (BF16,)],
                   m=d, n=f, k=s, after=pin)
    pin = emit("gu", dict(gate=dwg, up=dwu))
    (dn,) = _mm(f"{tag}_bwd_dn", "nt", [dg, du], [(wg, None, 0), (wu, None, 0)], _ep_sum, [(BF16,)],
                m=s, n=d, k=f, bm=512, bn=256, after=pin)
    dx, dxb, dgain, cs = _rms_bwd(f"{tag}_bwd_norm", h_in, gain, [dn], dh)
    return dx, dxb, dgain, cs


def _local_step(x, target, w, fetch, emit, prefetch=lambda group, after: None, after=()):
    s, d = x.shape
    nh = d // HEAD_DIM
    bq = BRANCHES[0][0] // BRANCHES[0][1]
    assert all(win // dil == bq for win, dil in BRANCHES)
    assert BRANCHES[0][1] == 1 and all(dil > 1 for _, dil in BRANCHES[1:])
    slopes = _alibi_slopes(nh)
    nd = d // 512 if d >= 512 else 1
    bn = d // nd

    (n1,) = _rms_fwd("a_norm", x, w["a_norm_g"], after=after)
    w_conv1 = fetch("conv1", n1)["conv_w1"]
    glu, a, gt = _mm("conv_pw1_glu", "nn", n1, [(w_conv1, None, 0), (w_conv1, None, nd)], _ep_glu,
                     [(F32,), (BF16,), (BF16,)], m=s, n=d, k=d, bn=bn,
                     extras=[(w["conv_b1"], "n", 0), (w["conv_b1"], "n", nd)])
    c, sw = _dwconv_fwd("conv_dw_ln", glu, w["conv_dw"], w["conv_dw_b"], w["conv_ln_g"], w["conv_ln_b"])
    w_conv2 = fetch("conv2", sw)["conv_w2"]
    (h1,) = _mm("conv_pw2", "nn", sw, [(w_conv2, None, 0)], _ep_bias_res, [(F32,)], m=s, n=d, k=d,
                extras=[(w["conv_b2"], "n", 0), (x, "mn", 0)], bn=WIDE_BN)
    def gate_up0(after):
        wts = fetch("ffn0gu", after)
        return wts["gate"], wts["up"]

    h2, ffn0, wf0 = _ffn_fwd("ffn0", h1, w["ffn_norm_g"][0:1], gate_up0, lambda after: fetch("ffn0d", after)["down"])
    wa = fetch("attn", h2)
    kvn, qn = _rms_fwd("kvq_norm", h2, jnp.concatenate([w["kv_norm_g"], w["b_norm_g"]], axis=0))
    k, v = _mm("kv_proj", "nn", kvn, [(wa["w_k"], None, 0), (wa["w_v"], None, 0)], _ep_all, [(BF16,), (BF16,)],
               m=s, n=d, k=d, bn=WIDE_BN)
    (q,) = _mm("q_proj", "nn", qn, [(wa["w_q"], None, 0)], _ep_id, [(BF16,)], m=s, n=d, k=d, bn=WIDE_BN)
    att, attb, lse = _attn_fwd("attn_fwd", q, k, v, slopes, bq)
    prefetch("ffn1", attb)
    (h3,) = _mm("o_proj", "nn", attb, [(wa["w_o"], None, 0)], _ep_add, [(F32,)], m=s, n=d, k=d,
                extras=[(h2, "mn", 0)], bn=WIDE_BN)
    got1 = {}

    def gate_up1(after):
        got1.update(fetch("ffn1", after))
        return got1["gate"], got1["up"]

    h4, ffn1, wf1 = _ffn_fwd("ffn1", h3, w["ffn_norm_g"][1:2], gate_up1, lambda after: got1["down"])
    dh4, dh4b, d_final_g, loss_cols = _final_loss("final_loss", h4, w["final_norm_g"], target)

    g = {}
    ga = {}
    dh3, dh3b, dgain1, _ = _ffn_bwd("ffn1", h3, w["ffn_norm_g"][1:2], wf1["gate"], wf1["up"], wf1["down"], ffn1,
                                    dh4, dh4b, (), lambda part, grads: emit("ffn1" + part, grads))
    (datt,) = _mm("o_proj_bwd_dx", "nt", dh3b, [(wa["w_o"], None, 0)], _ep_id, [(BF16,)], m=s, n=d, k=d, bn=WIDE_BN)
    (ga["w_o"],) = _mm("o_proj_bwd_dw", "tn", attb, [(dh3b, None, 0)], _ep_id, [(BF16,)], m=d, n=d, k=s, bn=WIDE_BN)
    dq, dk, dv = _attn_bwd("attn_bwd", q, k, v, att, lse, datt, slopes, bq)
    (ga["w_q"],) = _mm("q_proj_bwd_dw", "tn", qn, [(dq, None, 0)], _ep_id, [(BF16,)], m=d, n=d, k=s, bn=WIDE_BN)
    ga["w_k"], ga["w_v"] = _mm("kv_proj_bwd_dw", "tn", kvn, [(dk, None, 0), (dv, None, 0)], _ep_all,
                               [(BF16,), (BF16,)], m=d, n=d, k=s)
    pin = emit("attn", ga)
    (dqn,) = _mm("q_proj_bwd_dx", "nt", dq, [(wa["w_q"], None, 0)], _ep_id, [(BF16,)], m=s, n=d, k=d, after=pin,
                 bn=WIDE_BN)
    (dkvn,) = _mm("kv_proj_bwd_dx", "nt", [dk, dv], [(wa["w_k"], None, 0), (wa["w_v"], None, 0)], _ep_sum, [(BF16,)],
                  m=s, n=d, k=d, bn=WIDE_BN)
    dh2, dh2b, dg_kvq, _ = _rms_bwd("kvq_norm_bwd", h2, jnp.concatenate([w["kv_norm_g"], w["b_norm_g"]], axis=0),
                                    [dkvn, dqn], dh3)
    dh1, dh1b, dgain0, cs_h1 = _ffn_bwd("ffn0", h1, w["ffn_norm_g"][0:1], wf0["gate"], wf0["up"], wf0["down"], ffn0,
                                        dh2, dh2b, (), lambda part, grads: emit("ffn0" + part, grads))
    (dsw,) = _mm("conv_pw2_bwd_dx", "nt", dh1b, [(w_conv2, None, 0)], _ep_id, [(F32,)], m=s, n=d, k=d, bn=WIDE_BN)
    (dw2,) = _mm("conv_pw2_bwd_dw", "tn", sw, [(dh1b, None, 0)], _ep_id, [(BF16,)], m=d, n=d, k=s, bn=WIDE_BN)
    pin = emit("conv2", dict(conv_w2=dw2))
    dc, ln_sums = _conv_ln_bwd("conv_ln_bwd", c, dsw, w["conv_ln_g"], w["conv_ln_b"], after=pin)
    dpre, ddw, db1 = _conv_dw_bwd("conv_dw_bwd", dc, glu, a, gt, w["conv_dw"])
    (dw1,) = _mm("conv_pw1_bwd_dw", "tn", n1, [(dpre, None, 0)], _ep_id, [(BF16,)], m=d, n=2 * d, k=s)
    pin = emit("conv1", dict(conv_w1=dw1))
    (dn1,) = _mm("conv_pw1_bwd_dx", "nt", dpre, [(w_conv1, None, 0)], _ep_id, [(BF16,)], m=s, n=d, k=2 * d,
                 after=pin)
    dx, _, d_a_norm, _ = _rms_bwd("a_norm_bwd", x, w["a_norm_g"], [dn1], dh1)

    g.update(
        a_norm_g=d_a_norm, conv_b1=db1, conv_dw=ddw, conv_dw_b=ln_sums[2:3], conv_ln_g=ln_sums[0:1],
        conv_ln_b=ln_sums[1:2], conv_b2=cs_h1, kv_norm_g=dg_kvq[0:1], b_norm_g=dg_kvq[1:2],
        ffn_norm_g=jnp.concatenate([dgain0, dgain1], axis=0), final_norm_g=d_final_g,
    )
    return loss_cols, dx, g


HBM_SPEC = pl.BlockSpec(memory_space=pltpu.HBM)


def _mesh_place():
    x, y, c = lax.axis_index("x"), lax.axis_index("y"), lax.axis_index("c")
    chips = [(1 - x, y), (x, 1 - y), (1 - x, 1 - y)]
    return x, y, c, chips


def _shard_view(ref, kind, s, half=None):
    rows, cols = ref.shape
    if kind == "col":
        cw = cols // N_CHIPS
        if half is None:
            return ref.at[pl.ds(0, rows), pl.ds(s * cw, cw)]
        return ref.at[pl.ds(half * (rows // 2), rows // 2), pl.ds(s * cw, cw)]
    r = rows // N_CHIPS
    if half is None:
        return ref.at[pl.ds(s * r, r), pl.ds(0, cols)]
    return ref.at[pl.ds(s * r + half * (r // 2), r // 2), pl.ds(0, cols)]


def _full_shape(shard, kind):
    r, cw = shard.shape[-2:]
    return (r, cw * N_CHIPS) if kind == "col" else (r * N_CHIPS, cw)


def _gather_weights(shards, kinds, after=()):
    nt = len(shards)
    nf = len(after)
    fulls = [jax.ShapeDtypeStruct(_full_shape(sh, kind), sh.dtype) for sh, kind in zip(shards, kinds)]

    def body(*refs):
        src = refs[:nt]
        dst = refs[nt + nf:2 * nt + nf]
        send, recv, fsend, frecv, local = refs[2 * nt + nf:]
        x, y, c, chips = _mesh_place()
        s = 2 * x + y
        sib = (x, y, 1 - c)

        def half_of_shard(t):
            r, cw = src[t].shape
            return src[t].at[pl.ds(c * (r // 2), r // 2), pl.ds(0, cw)]

        locals_ = [pltpu.make_async_copy(src[t], _shard_view(dst[t], kinds[t], s), local.at[t]) for t in range(nt)]
        for cp in locals_:
            cp.start()
        sends = []
        for t in range(nt):
            for j, chip in enumerate(chips):
                cp = pltpu.make_async_remote_copy(
                    src_ref=half_of_shard(t), dst_ref=_shard_view(dst[t], kinds[t], s, c),
                    send_sem=send.at[t, j], recv_sem=recv.at[t, j], device_id=(*chip, c), device_id_type=MESH)
                cp.start()
                sends.append(cp)
        for t in range(nt):
            for j, (px, py) in enumerate(chips):
                landed = _shard_view(dst[t], kinds[t], 2 * px + py, c)
                pltpu.make_async_remote_copy(
                    src_ref=half_of_shard(t), dst_ref=landed, send_sem=send.at[t, j], recv_sem=recv.at[t, j],
                    device_id=(px, py, c), device_id_type=MESH).wait_recv()
                cp = pltpu.make_async_remote_copy(
                    src_ref=landed, dst_ref=landed, send_sem=fsend.at[t, j], recv_sem=frecv.at[t, j],
                    device_id=sib, device_id_type=MESH)
                cp.start()
                sends.append(cp)
        for t in range(nt):
            for j, (px, py) in enumerate(chips):
                other = _shard_view(dst[t], kinds[t], 2 * px + py, 1 - c)
                pltpu.make_async_remote_copy(
                    src_ref=other, dst_ref=other, send_sem=fsend.at[t, j], recv_sem=frecv.at[t, j],
                    device_id=sib, device_id_type=MESH).wait_recv()
        for cp in sends:
            cp.wait_send()
        for cp in locals_:
            cp.wait()

    return pl.pallas_call(
        body,
        name="gather_weights",
        in_specs=[HBM_SPEC] * nt + [ANY_SPEC] * nf,
        out_specs=[HBM_SPEC] * nt,
        out_shape=fulls,
        scratch_shapes=[pltpu.SemaphoreType.DMA((nt, 3))] * 4 + [pltpu.SemaphoreType.DMA((nt,))],
    )(*shards, *after)


def _row_blocks(rows, want=512):
    nb = 1
    while rows // nb > want or rows % nb or (rows // nb) % 16:
        nb += 1
        if nb > rows:
            return rows, 1
    return rows // nb, nb


def _place_own(name, src, kind, land, ids, lead=0):
    r, cw = src.shape[-2:]
    tr, nb = _row_blocks(r)
    if src.ndim == 3:
        src_spec = pl.BlockSpec((None, tr, cw), lambda i, ids_ref: (lead, i, 0))
    else:
        src_spec = pl.BlockSpec((tr, cw), lambda i, ids_ref: (i, 0))
    if kind == "col":
        dst_spec = pl.BlockSpec((tr, cw), lambda i, ids_ref: (i, ids_ref[0]))
    else:
        dst_spec = pl.BlockSpec((tr, cw), lambda i, ids_ref: (ids_ref[0] * nb + i, 0))

    def body(ids_ref, s_ref, o_ref):
        o_ref[...] = s_ref[...].astype(o_ref.dtype)

    return pl.pallas_call(
        body,
        name=name,
        grid_spec=pltpu.PrefetchScalarGridSpec(num_scalar_prefetch=1, grid=(nb,), in_specs=[src_spec],
                                               out_specs=dst_spec),
        out_shape=land,
        compiler_params=pltpu.CompilerParams(dimension_semantics=("parallel",)),
    )(ids, src)


SEM_SPEC = pl.BlockSpec(memory_space=pltpu.SEMAPHORE)
SIDE_EFFECT = pltpu.SideEffectType.DATAFLOW_SIDE_EFFECTING


def _copies_start(name, srcs, lands, copies, n_sems, after):
    ns, nl, nf = len(srcs), len(lands), len(after)

    def body(*refs):
        src, land = refs[:ns], refs[ns:ns + nl]
        send, recv = refs[ns + nl + nf], refs[ns + nl + nf + 1]
        pin = refs[-1]
        for cp in copies(src, land, send, recv, _mesh_place()):
            cp.start()
        pin[...] = jnp.zeros_like(pin)

    arrs = list(srcs) + list(lands)
    res = pl.pallas_call(
        body,
        name=name,
        in_specs=[HBM_SPEC] * (ns + nl) + [ANY_SPEC] * nf,
        out_specs=[SEM_SPEC, SEM_SPEC] + [HBM_SPEC] * (ns + nl) + [pl.BlockSpec(memory_space=pltpu.VMEM)],
        out_shape=[pltpu.SemaphoreType.DMA((n_sems,)), pltpu.SemaphoreType.DMA((n_sems,))]
        + [pltpu.HBM(a.shape, a.dtype) for a in arrs] + [jax.ShapeDtypeStruct((8, 128), F32)],
        input_output_aliases={i: 2 + i for i in range(ns + nl)},
        compiler_params=pltpu.CompilerParams(has_side_effects=SIDE_EFFECT),
    )(*[pltpu.with_memory_space_constraint(a, pltpu.HBM) for a in arrs], *after)
    return res[0], res[1], list(res[2:2 + ns]), list(res[2 + ns:2 + ns + nl]), res[-1]


def _copies_wait(name, send, recv, srcs, lands, copies, after):
    ns, nl, nf = len(srcs), len(lands), len(after)

    def body(*refs):
        src, land = refs[:ns], refs[ns:ns + nl]
        send_sems, recv_sems = refs[ns + nl], refs[ns + nl + 1]
        cps = copies(src, land, send_sems, recv_sems, _mesh_place())
        for cp in cps:
            cp.wait_send()
        for cp in cps:
            cp.wait_recv()

    arrs = list(srcs) + list(lands)
    res = pl.pallas_call(
        body,
        name=name,
        in_specs=[HBM_SPEC] * (ns + nl) + [SEM_SPEC, SEM_SPEC] + [ANY_SPEC] * nf,
        out_specs=[HBM_SPEC] * (ns + nl),
        out_shape=[pltpu.HBM(a.shape, a.dtype) for a in arrs],
        input_output_aliases={i: i for i in range(ns + nl)},
        compiler_params=pltpu.CompilerParams(has_side_effects=SIDE_EFFECT),
    )(*arrs, send, recv, *after)
    return list(res[:ns]), list(res[ns:])


def _gather_copies(kinds, halves):
    def copies(src, land, send, recv, place):
        x, y, c, chips = place
        s = 2 * x + y
        mine = [_shard_view(land[t], kinds[t], s, c if halves else None) for t in range(len(kinds))]
        return [
            pltpu.make_async_remote_copy(
                src_ref=mine[t], dst_ref=mine[t], send_sem=send.at[3 * t + j], recv_sem=recv.at[3 * t + j],
                device_id=(px, py, c), device_id_type=MESH)
            for t in range(len(kinds)) for j, (px, py) in enumerate(chips)
        ]

    return copies


def _gather_pass_copies(kinds):
    def copies(src, land, send, recv, place):
        x, y, c, chips = place
        views = [_shard_view(land[t], kinds[t], 2 * px + py, c) for t in range(len(kinds)) for px, py in chips]
        return [
            pltpu.make_async_remote_copy(src_ref=v, dst_ref=v, send_sem=send.at[i], recv_sem=recv.at[i],
                                         device_id=(x, y, 1 - c), device_id_type=MESH)
            for i, v in enumerate(views)
        ]

    return copies


def _gather_pass_on(name, lands, kinds):
    nt = len(lands)

    def body(*refs):
        buf = refs[nt:2 * nt]
        send, recv = refs[2 * nt:]
        x, y, c, chips = _mesh_place()
        sib = (x, y, 1 - c)
        sends = []
        for t in range(nt):
            for j, (px, py) in enumerate(chips):
                mine = _shard_view(buf[t], kinds[t], 2 * px + py, c)
                cp = pltpu.make_async_remote_copy(src_ref=mine, dst_ref=mine, send_sem=send.at[t, j],
                                                  recv_sem=recv.at[t, j], device_id=sib, device_id_type=MESH)
                cp.start()
                sends.append(cp)
        for t in range(nt):
            for j, (px, py) in enumerate(chips):
                theirs = _shard_view(buf[t], kinds[t], 2 * px + py, 1 - c)
                pltpu.make_async_remote_copy(src_ref=theirs, dst_ref=theirs, send_sem=send.at[t, j],
                                             recv_sem=recv.at[t, j], device_id=sib, device_id_type=MESH).wait_recv()
        for cp in sends:
            cp.wait_send()

    return pl.pallas_call(
        body,
        name=name,
        in_specs=[HBM_SPEC] * nt,
        out_specs=[HBM_SPEC] * nt,
        out_shape=[jax.ShapeDtypeStruct(a.shape, a.dtype) for a in lands],
        input_output_aliases={i: i for i in range(nt)},
        scratch_shapes=[pltpu.SemaphoreType.DMA((nt, 3))] * 2,
    )(*lands)


def _grad_part(ref, kind, s):
    return ref if kind == "all" else _shard_view(ref, kind, s)


def _grad_copies(kinds):
    def peers(place):
        x, y, c, chips = place
        return [(x, y, 1 - c)] + [(px, py, c) for px, py in chips]

    def copies(src, land, send, recv, place):
        x, y, c, chips = place
        me = 4 * x + 2 * y + c
        return [
            pltpu.make_async_remote_copy(
                src_ref=_grad_part(src[t], kinds[t], 2 * px + py), dst_ref=land[t].at[me],
                send_sem=send.at[GRAD_PEERS * t + k], recv_sem=recv.at[GRAD_PEERS * t + k], device_id=(px, py, pc),
                device_id_type=MESH)
            for t in range(len(kinds)) for k, (px, py, pc) in enumerate(peers(place))
        ]

    return copies


def _pass_copies(n):
    def copies(src, land, send, recv, place):
        x, y, c, chips = place
        return [
            pltpu.make_async_remote_copy(
                src_ref=land[t].at[4 * px + 2 * py + c], dst_ref=land[t].at[4 * px + 2 * py + c],
                send_sem=send.at[3 * t + j], recv_sem=recv.at[3 * t + j], device_id=(x, y, 1 - c),
                device_id_type=MESH)
            for t in range(n) for j, (px, py) in enumerate(chips)
        ]

    return copies


GRAD_PEERS = 4


def _land_shape(grad, kind):
    rows, cols = grad.shape
    if kind == "col":
        return (N_DEV, rows, cols // N_CHIPS)
    if kind == "row":
        return (N_DEV, rows // N_CHIPS, cols)
    return (N_DEV, rows, cols)


def _adamw_reduce(name, contrib, own, kind, ids, w, m, v, layer=None, prev=None):
    rows, cols = w.shape[-2:]
    t = 128 if rows % 128 == 0 else rows
    nb = rows // t
    c1 = 1.0 - ADAM_B1 ** ADAM_STEP
    c2 = 1.0 - ADAM_B2 ** ADAM_STEP

    n_prev = 0 if prev is None else 4

    def body(ids_ref, c_ref, own_ref, w_ref, m_ref, v_ref, *refs):
        g_ref, d_ref, nm_ref, nv_ref = refs[n_prev:]
        me = ids_ref[1]
        mine = own_ref[...].astype(F32)
        g = None
        for q in range(N_DEV):
            term = jnp.where(me == q, mine, c_ref[q].astype(F32))
            g = term if g is None else g + term
        nm = ADAM_B1 * m_ref[...] + (1.0 - ADAM_B1) * g
        nv = ADAM_B2 * v_ref[...] + (1.0 - ADAM_B2) * (g * g)
        g_ref[...] = g
        nm_ref[...] = nm
        nv_ref[...] = nv
        d_ref[...] = -ADAM_LR * ((nm / c1) / (jnp.sqrt(nv / c2) + ADAM_EPS) + ADAM_WD * w_ref[...])

    if layer is None:
        blk = pl.BlockSpec((t, cols), lambda i, ids_ref: (i, 0))
    else:
        blk = pl.BlockSpec((None, t, cols), lambda i, ids_ref: (layer, i, 0))
    if kind == "col":
        own_spec = pl.BlockSpec((t, cols), lambda i, ids_ref: (i, ids_ref[0]))
    elif kind == "row":
        own_spec = pl.BlockSpec((t, cols), lambda i, ids_ref: (ids_ref[0] * nb + i, 0))
    else:
        own_spec = pl.BlockSpec((t, cols), lambda i, ids_ref: (i, 0))
    return pl.pallas_call(
        body,
        name=name,
        grid_spec=pltpu.PrefetchScalarGridSpec(
            num_scalar_prefetch=1,
            grid=(nb,),
            in_specs=[pl.BlockSpec((N_DEV, t, cols), lambda i, ids_ref: (0, i, 0)), own_spec, blk, blk, blk]
            + [ANY_SPEC] * n_prev,
            out_specs=[blk] * 4,
        ),
        out_shape=[jax.ShapeDtypeStruct(w.shape, F32)] * 4,
        input_output_aliases={6 + i: i for i in range(n_prev)},
        compiler_params=pltpu.CompilerParams(dimension_semantics=("parallel",)),
    )(ids, contrib, own, w, m, v, *(prev or ()))


WEIGHT_NAMES = ("a_norm_g", "conv_w1", "conv_b1", "conv_dw", "conv_dw_b", "conv_ln_g", "conv_ln_b", "conv_w2",
                "conv_b2", "kv_norm_g", "w_k", "w_v", "b_norm_g", "w_q", "w_o", "ffn_norm_g", "ffn_w_gate",
                "ffn_w_up", "ffn_w_down", "final_norm_g")
GROUPS = {
    "conv2": (("conv_w2", "conv_w2", None, "row"),),
    "ffn0": (("gate", "ffn_w_gate", 0, "col"), ("up", "ffn_w_up", 0, "col"), ("down", "ffn_w_down", 0, "row")),
    "attn": (("w_k", "w_k", None, "row"), ("w_v", "w_v", None, "row"), ("w_q", "w_q", None, "row"),
             ("w_o", "w_o", None, "row")),
    "ffn1": (("gate", "ffn_w_gate", 1, "col"), ("up", "ffn_w_up", 1, "col"), ("down", "ffn_w_down", 1, "row")),
}
GROUPS["conv1"] = (("conv_w1", "conv_w1", None, "col"),)
for _layer in (0, 1):
    GROUPS[f"ffn{_layer}d"] = tuple(it for it in GROUPS[f"ffn{_layer}"] if it[0] == "down")
    GROUPS[f"ffn{_layer}gu"] = tuple(it for it in GROUPS[f"ffn{_layer}"] if it[0] != "down")
FETCH_ORDER = ("conv1", "conv2", "ffn0gu", "ffn0d", "attn", "ffn1")
HALVED = ("conv1", "conv2", "ffn0gu", "ffn0d", "ffn1")
EMIT_ORDER = ("ffn1d", "ffn1gu", "attn", "ffn0d", "ffn0gu", "conv2", "conv1", "vec")
RETIRE_AT = {"attn": ("ffn1d", "ffn1gu"), "ffn0d": ("attn",), "conv1": ("ffn0d", "ffn0gu", "conv2")}
PACKED = (("a_norm_g", 0, 1), ("conv_b1", 8, 2), ("conv_dw", 16, CONV_WIDTH), ("conv_dw_b", 48, 1),
          ("conv_ln_g", 56, 1), ("conv_ln_b", 64, 1), ("conv_b2", 72, 1))
PACK_ROWS = 80
WHOLE = (("kv_norm_g", 0, 1), ("b_norm_g", 1, 1), ("ffn_norm_g", 2, 2), ("final_norm_g", 4, 1))
WHOLE_ROWS = 8


def _pack_rows(parts, total, width):
    out, at = [], 0
    for arr, first in parts:
        if first > at:
            out.append(jnp.zeros((first - at, width), F32))
        rows8 = -(-arr.shape[0] // 8) * 8
        out.append(jnp.pad(arr, ((0, rows8 - arr.shape[0]), (0, 0))))
        at = first + rows8
    if total > at:
        out.append(jnp.zeros((total - at, width), F32))
    return jnp.concatenate(out, axis=0)


def kernel(x, a_norm_g, conv_w1, conv_b1, conv_dw, conv_dw_b, conv_ln_g, conv_ln_b, conv_w2, conv_b2, kv_norm_g, w_k, w_v, b_norm_g, w_q, w_o, ffn_norm_g, ffn_w_gate, ffn_w_up, ffn_w_down, final_norm_g, loss_target, m_a_norm_g, m_conv_w1, m_conv_b1, m_conv_dw, m_conv_dw_b, m_conv_ln_g, m_conv_ln_b, m_conv_w2, m_conv_b2, m_kv_norm_g, m_w_k, m_w_v, m_b_norm_g, m_w_q, m_w_o, m_ffn_norm_g, m_ffn_w_gate, m_ffn_w_up, m_ffn_w_down, m_final_norm_g, v_a_norm_g, v_conv_w1, v_conv_b1, v_conv_dw, v_conv_dw_b, v_conv_ln_g, v_conv_ln_b, v_conv_w2, v_conv_b2, v_kv_norm_g, v_w_k, v_w_v, v_b_norm_g, v_w_q, v_w_o, v_ffn_norm_g, v_ffn_w_gate, v_ffn_w_up, v_ffn_w_down, v_final_norm_g):
    args = locals()
    wts = {n: args[n] for n in WEIGHT_NAMES}
    mom = {n: args["m_" + n] for n in WEIGHT_NAMES}
    vel = {n: args["v_" + n] for n in WEIGHT_NAMES}
    s, d = x.shape[-2:]
    dq = d // N_CHIPS
    x2 = x.reshape(s, d)
    tgt = loss_target.reshape(s, d)

    def pack_shard(src):
        return _pack_rows([(src[n].reshape(-1, dq), first) for n, first, _ in PACKED], PACK_ROWS, dq)

    def pack_whole(src):
        return _pack_rows([(jnp.concatenate([src[n].reshape(-1, d) for n, _, _ in WHOLE], axis=0), 0)], WHOLE_ROWS, d)

    gathers = {}
    pins = []
    ids = jnp.stack([2 * lax.axis_index("x") + lax.axis_index("y"),
                     4 * lax.axis_index("x") + 2 * lax.axis_index("y") + lax.axis_index("c")]).astype(jnp.int32)

    def start_gather(grp, pins):
        kinds = [kind for _, _, _, kind in GROUPS[grp]]
        lands = []
        for key, n, layer, kind in GROUPS[grp]:
            shard = wts[n] if layer is not None else wts[n].reshape(wts[n].shape[-2:])
            full = jax.ShapeDtypeStruct(_full_shape(shard, kind), BF16)
            lands.append(_place_own(f"gather_place_{grp}_{key}", shard, kind, full, ids, layer))
        copies = _gather_copies(kinds, grp in HALVED)
        send, recv, _, lands, pin = _copies_start("gather_start_" + grp, [], lands, copies, 3 * len(lands), pins)
        gathers[grp] = (send, recv, [], lands, copies, kinds)
        return [pin]

    (packed_full,) = _gather_weights([pack_shard(wts)], ["row"])
    w = {}
    pins = [packed_full]
    for grp in FETCH_ORDER:
        pins = start_gather(grp, pins)

    swapping = {}

    def prefetch(grp, after):
        send, recv, srcs, lands, copies, kinds = gathers[grp]
        _, fulls = _copies_wait("gather_wait_" + grp, send, recv, srcs, lands, copies, [after])
        copies = _gather_pass_copies(kinds)
        send, recv, _, fulls, _ = _copies_start("gather_swap_" + grp, [], fulls, copies, 3 * len(fulls), ())
        swapping[grp] = (send, recv, fulls, copies)

    def fetch(grp, after):
        if grp in swapping:
            send, recv, fulls, copies = swapping[grp]
            _, fulls = _copies_wait("gather_swapped_" + grp, send, recv, [], fulls, copies, [after])
        else:
            send, recv, srcs, lands, copies, kinds = gathers[grp]
            _, fulls = _copies_wait("gather_wait_" + grp, send, recv, srcs, lands, copies, [after])
            if grp in HALVED:
                fulls = _gather_pass_on("gather_pass_" + grp, fulls, kinds)
        return {key: full for (key, _, _, _), full in zip(GROUPS[grp], fulls)}

    packed = packed_full.reshape(N_CHIPS, PACK_ROWS, dq)
    for n, first, rows in PACKED:
        part = packed[:, first:first + rows, :]
        if n == "conv_dw":
            w[n] = part.transpose(1, 0, 2).reshape(rows, d)
        else:
            w[n] = part.reshape(1, N_CHIPS * rows * dq)
    for n, _, rows in WHOLE:
        w[n] = wts[n].reshape(rows, d)

    exchanges = {}
    passing = {}
    own_grads = {}

    def retire(tag, after):
        send, recv, srcs, lands, copies, kinds = exchanges[tag]
        srcs, lands = _copies_wait("grads_wait_" + tag, send, recv, srcs, lands, copies, after)
        own_grads[tag] = list(zip(srcs, kinds))
        copies = _pass_copies(len(lands))
        send, recv, _, lands, pin = _copies_start("grads_pass_" + tag, [], lands, copies, 3 * len(lands), ())
        passing[tag] = (send, recv, lands, copies)
        return pin

    def emit_grads(tag, grads, kinds):
        pins = [retire(old, [grads[0]]) for old in RETIRE_AT.get(tag, ())]
        copies = _grad_copies(kinds)
        lands = [lax.empty(_land_shape(gr, kd), gr.dtype) for gr, kd in zip(grads, kinds)]
        send, recv, srcs, lands, pin = _copies_start("grads_start_" + tag, grads, lands, copies,
                                                     GRAD_PEERS * len(grads), pins)
        exchanges[tag] = (send, recv, srcs, lands, copies, kinds)
        return [pin]

    def emit(grp, grads):
        return emit_grads(grp, [grads[key] for key, _, _, _ in GROUPS[grp]], [kind for _, _, _, kind in GROUPS[grp]])

    loss_cols, dx, g = _local_step(x2, tgt, w, fetch, emit, prefetch, after=pins)
    loss = lax.psum(jnp.sum(loss_cols), ("x", "y", "c"))

    gp = []
    for n, first, rows in PACKED:
        if n == "conv_dw":
            part = g[n].reshape(rows, N_CHIPS, dq).transpose(1, 0, 2)
        else:
            part = g[n].reshape(N_CHIPS, rows, dq)
        gp.append((part, first))
    g_packed = jnp.concatenate(
        [_pack_rows([(p[ci], first) for p, first in gp], PACK_ROWS, dq) for ci in range(N_CHIPS)], axis=0)
    emit_grads("vec", [g_packed, pack_whole(g)], ["row", "all"])
    for tag in EMIT_ORDER:
        if tag not in passing:
            retire(tag, [dx])
    contribs = {}
    for tag in EMIT_ORDER:
        send, recv, lands, copies = passing[tag]
        _, arrived = _copies_wait("grads_passed_" + tag, send, recv, [], lands, copies, [dx])
        contribs[tag] = [(c, own, kind) for c, (own, kind) in zip(arrived, own_grads[tag])]

    res = {}

    def adamw(n, contrib, layer=None, prev=None):
        arrived, own, kind = contrib
        if layer is None:
            shape = wts[n].shape
            r2 = shape[-2:]
            outs = _adamw_reduce("adamw_" + n, arrived, own, kind, ids, wts[n].reshape(r2), mom[n].reshape(r2),
                                 vel[n].reshape(r2))
            return [o.reshape(shape) for o in outs]
        return _adamw_reduce(f"adamw_{n}_{layer}", arrived, own, kind, ids, wts[n], mom[n], vel[n], layer, prev)

    for grp in ("attn", "conv2", "conv1"):
        for (key, n, _, _), contrib in zip(GROUPS[grp], contribs[grp]):
            res[n] = adamw(n, contrib)
    for part in ("d", "gu"):
        for (key, n, _, _), c0, c1 in zip(GROUPS["ffn0" + part], contribs["ffn0" + part], contribs["ffn1" + part]):
            res[n] = adamw(n, c1, 1, adamw(n, c0, 0))
    outs = _adamw_reduce("adamw_packed", *contribs["vec"][0], ids, pack_shard(wts), pack_shard(mom), pack_shard(vel))
    for n, first, rows in PACKED:
        res[n] = [o[first:first + rows].reshape(wts[n].shape) for o in outs]
    outs = _adamw_reduce("adamw_whole", *contribs["vec"][1], ids, pack_whole(wts), pack_whole(mom), pack_whole(vel))
    for n, first, rows in WHOLE:
        res[n] = [o[first:first + rows].reshape(wts[n].shape) for o in outs]

    out = [loss, dx.reshape(x.shape)]
    for which in range(4):
        out += [res[n][which] for n in WEIGHT_NAMES]
    return tuple(out)
```

```python
import functools

import jax
import jax.numpy as jnp
from jax import lax
from jax.experimental import pallas as pl
from jax.experimental.pallas import tpu as pltpu

F32 = jnp.float32
BF16 = jnp.bfloat16

HEAD_DIM = 128
BRANCHES = ((128, 1), (512, 4), (2048, 16))
CONV_WIDTH = 31
CONV_HALO = 32
RMS_EPS = 1e-6
LN_EPS = 1e-5
ADAM_LR = 0.001
ADAM_B1 = 0.9
ADAM_B2 = 0.999
ADAM_EPS = 1e-08
ADAM_WD = 0.01
ADAM_STEP = 10
N_CHIPS = 4
N_DEV = 8
MESH = pl.DeviceIdType.MESH


def _sigmoid(x):
    return 0.5 * jnp.tanh(0.5 * x) + 0.5


def _row_tile(rows, want):
    t = min(rows, want)
    assert rows % t == 0, (rows, want)
    return t


_DOT_DIMS = {"nn": ((1,), (0,)), "nt": ((1,), (1,)), "tn": ((0,), (0,))}


ANY_SPEC = pl.BlockSpec(memory_space=pl.ANY)
WIDE_BN = 1024


def _mm(name, mode, a, bs, epilogue, outs, *, m, n, k, extras=(), bm=1024, bn=512, bk=None, after=()):
    bm, bn = min(bm, m), min(bn, n)
    bk = k if bk is None else min(bk, k)
    assert m % bm == 0 and n % bn == 0 and k % bk == 0, (name, m, n, k, bm, bn, bk)
    nk = k // bk
    a_list = list(a) if isinstance(a, (list, tuple)) else [a]
    na, nb, ne, no = len(a_list), len(bs), len(extras), len(outs)
    assert na in (1, nb)

    if mode == "tn":
        a_spec = pl.BlockSpec((bk, bm), lambda i, j, kk: (kk, i))
    else:
        a_spec = pl.BlockSpec((bm, bk), lambda i, j, kk: (i, kk))

    def b_spec(lead, off):
        if mode == "nt":
            blk, idx = (bn, bk), (lambda i, j, kk: (j + off, kk))
        else:
            blk, idx = (bk, bn), (lambda i, j, kk: (kk, j + off))
        if lead is None:
            return pl.BlockSpec(blk, idx)
        return pl.BlockSpec((None,) + blk, lambda i, j, kk: (lead,) + idx(i, j, kk))

    def e_spec(kind, off):
        if kind == "mn":
            return pl.BlockSpec((bm, bn), lambda i, j, kk: (i, j + off))
        return pl.BlockSpec((1, bn), lambda i, j, kk: (0, j + off))

    nf = len(after)
    in_specs = [a_spec] * na + [b_spec(l, o) for _, l, o in bs] + [e_spec(kd, o) for _, kd, o in extras]
    in_specs += [ANY_SPEC] * nf
    out_specs = [pl.BlockSpec((bm, bn), lambda i, j, kk: (i, j)) for _ in outs]
    out_shape = [jax.ShapeDtypeStruct((m, n), dt) for (dt,) in outs]
    dims = (_DOT_DIMS[mode], ((), ()))

    def body(*refs):
        a_refs = refs[:na]
        b_refs = refs[na:na + nb]
        e_refs = refs[na + nb:na + nb + ne]
        o_refs = refs[na + nb + ne + nf:na + nb + ne + nf + no]
        acc_refs = refs[na + nb + ne + nf + no:]
        avs = [a_ref[...].astype(BF16) for a_ref in a_refs]
        prods = [lax.dot_general(avs[bi % na], b_ref[...].astype(BF16), dims, preferred_element_type=F32)
                 for bi, b_ref in enumerate(b_refs)]

        def finish(accs):
            res = epilogue(accs, [e_ref[...] for e_ref in e_refs])
            for o_ref, r in zip(o_refs, res):
                o_ref[...] = r.astype(o_ref.dtype)

        if nk == 1:
            finish(prods)
        else:
            kk = pl.program_id(2)

            @pl.when(kk == 0)
            def _():
                for acc_ref, p in zip(acc_refs, prods):
                    acc_ref[...] = p

            @pl.when(kk > 0)
            def _():
                for acc_ref, p in zip(acc_refs, prods):
                    acc_ref[...] += p

            @pl.when(kk == nk - 1)
            def _():
                finish([acc_ref[...] for acc_ref in acc_refs])

    scratch = [] if nk == 1 else [pltpu.VMEM((bm, bn), F32) for _ in bs]
    res = pl.pallas_call(
        body,
        name=name,
        grid=(m // bm, n // bn, nk),
        in_specs=in_specs,
        out_specs=out_specs,
        out_shape=out_shape,
        scratch_shapes=scratch,
        compiler_params=pltpu.CompilerParams(dimension_semantics=("parallel", "parallel", "arbitrary")),
    )(*a_list, *[b for b, _, _ in bs], *[e for e, _, _ in extras], *after)
    return res


def _rms_fwd(name, x, gains, after=()):
    s, d = x.shape
    ng = gains.shape[0]
    t = _row_tile(s, 256)
    nf = len(after)

    def body(x_ref, g_ref, *refs):
        o_refs = refs[nf:]
        xv = x_ref[...]
        r = lax.rsqrt(jnp.mean(xv * xv, axis=-1, keepdims=True) + RMS_EPS)
        xh = xv * r
        for gi, o_ref in enumerate(o_refs):
            o_ref[...] = (xh * g_ref[gi:gi + 1, :]).astype(o_ref.dtype)

    return pl.pallas_call(
        body,
        name=name,
        grid=(s // t,),
        in_specs=[pl.BlockSpec((t, d), lambda i: (i, 0)), pl.BlockSpec((ng, d), lambda i: (0, 0))] + [ANY_SPEC] * nf,
        out_specs=[pl.BlockSpec((t, d), lambda i: (i, 0)) for _ in range(ng)],
        out_shape=[jax.ShapeDtypeStruct((s, d), BF16) for _ in range(ng)],
        compiler_params=pltpu.CompilerParams(dimension_semantics=("parallel",)),
    )(x, gains, *after)


def _rms_bwd(name, x, gains, dns, dres):
    s, d = x.shape
    ng = gains.shape[0]
    t = _row_tile(s, 256)

    def body(x_ref, g_ref, dres_ref, *refs):
        dn_refs = refs[:ng]
        dx_ref, dxb_ref, dg_ref, cs_ref = refs[ng:]
        i = pl.program_id(0)
        xv = x_ref[...]
        r = lax.rsqrt(jnp.mean(xv * xv, axis=-1, keepdims=True) + RMS_EPS)
        xh = xv * r
        dx = dres_ref[...]
        dgs = []
        for gi in range(ng):
            dn = dn_refs[gi][...].astype(F32)
            dxh = dn * g_ref[gi:gi + 1, :]
            dgs.append(jnp.sum(dn * xh, axis=0, keepdims=True))
            dx = dx + r * (dxh - xh * jnp.mean(dxh * xh, axis=-1, keepdims=True))
        dx_ref[...] = dx
        dxb_ref[...] = dx.astype(BF16)
        dg = jnp.concatenate(dgs, axis=0) if ng > 1 else dgs[0]
        cs = jnp.sum(dx, axis=0, keepdims=True)

        @pl.when(i == 0)
        def _():
            dg_ref[...] = dg
            cs_ref[...] = cs

        @pl.when(i > 0)
        def _():
            dg_ref[...] += dg
            cs_ref[...] += cs

    row = pl.BlockSpec((t, d), lambda i: (i, 0))
    return pl.pallas_call(
        body,
        name=name,
        grid=(s // t,),
        in_specs=[row, pl.BlockSpec((ng, d), lambda i: (0, 0)), row] + [row] * ng,
        out_specs=[row, row, pl.BlockSpec((ng, d), lambda i: (0, 0)), pl.BlockSpec((1, d), lambda i: (0, 0))],
        out_shape=[
            jax.ShapeDtypeStruct((s, d), F32),
            jax.ShapeDtypeStruct((s, d), BF16),
            jax.ShapeDtypeStruct((ng, d), F32),
            jax.ShapeDtypeStruct((1, d), F32),
        ],
        compiler_params=pltpu.CompilerParams(dimension_semantics=("arbitrary",)),
    )(x, gains, dres, *dns)


def _final_loss(name, h, gain, target):
    s, d = h.shape
    t = _row_tile(s, 256)

    def body(h_ref, g_ref, t_ref, dh_ref, dhb_ref, dg_ref, ls_ref):
        i = pl.program_id(0)
        xv = h_ref[...]
        g = g_ref[...]
        r = lax.rsqrt(jnp.mean(xv * xv, axis=-1, keepdims=True) + RMS_EPS)
        xh = xv * r
        err = xh * g - t_ref[...]
        ls = jnp.sum(err * err, axis=0, keepdims=True) * (0.5 / d)
        dy = err * (1.0 / d)
        dxh = dy * g
        dg = jnp.sum(dy * xh, axis=0, keepdims=True)
        dx = r * (dxh - xh * jnp.mean(dxh * xh, axis=-1, keepdims=True))
        dh_ref[...] = dx
        dhb_ref[...] = dx.astype(BF16)

        @pl.when(i == 0)
        def _():
            dg_ref[...] = dg
            ls_ref[...] = ls

        @pl.when(i > 0)
        def _():
            dg_ref[...] += dg
            ls_ref[...] += ls

    row = pl.BlockSpec((t, d), lambda i: (i, 0))
    vec = pl.BlockSpec((1, d), lambda i: (0, 0))
    return pl.pallas_call(
        body,
        name=name,
        grid=(s // t,),
        in_specs=[row, vec, row],
        out_specs=[row, row, vec, vec],
        out_shape=[
            jax.ShapeDtypeStruct((s, d), F32),
            jax.ShapeDtypeStruct((s, d), BF16),
            jax.ShapeDtypeStruct((1, d), F32),
            jax.ShapeDtypeStruct((1, d), F32),
        ],
        compiler_params=pltpu.CompilerParams(dimension_semantics=("arbitrary",)),
    )(h, gain, target)


SUBLANES = 8
CONV_LANES = 512
CONV_ROWS = 32
NORM_ROWS = 16


def _conv_tiles(s):
    t = _row_tile(s, 128)
    assert t % CONV_HALO == 0 and t % CONV_ROWS == 0
    return t, t // CONV_HALO


def _shifted_copies(dst, src, lanes, rows):
    for m in range(SUBLANES):
        n = rows if m == 0 else rows - SUBLANES
        dst[m, :n, :] = src[m:m + n, lanes]


def _shifted(copies, offset, n):
    m = offset % SUBLANES
    return copies[m, offset - m:offset - m + n, :]


def _dwconv_fwd(name, u, dw, dw_b, ln_g, ln_b):
    s, d = u.shape
    t, hb = _conv_tiles(s)
    w = dw.shape[0]
    lo = CONV_HALO - (w - 1)

    lw = min(CONV_LANES, d)

    def body(cur_ref, prev_ref, dw_ref, dwb_ref, lg_ref, lb_ref, c_ref, sw_ref, cat_ref, sh_ref):
        i = pl.program_id(0)
        cat_ref[CONV_HALO:, :] = cur_ref[...]

        @pl.when(i == 0)
        def _():
            cat_ref[:CONV_HALO, :] = jnp.zeros((CONV_HALO, d), F32)

        @pl.when(i > 0)
        def _():
            cat_ref[:CONV_HALO, :] = prev_ref[...]

        for lc in range(d // lw):
            lanes = slice(lc * lw, (lc + 1) * lw)
            _shifted_copies(sh_ref, cat_ref, lanes, t + CONV_HALO)
            for rc in range(t // CONV_ROWS):
                acc = jnp.broadcast_to(dwb_ref[:, lanes], (CONV_ROWS, lw))
                for kk in range(w):
                    acc = acc + dw_ref[kk:kk + 1, lanes] * _shifted(sh_ref, lo + kk + rc * CONV_ROWS, CONV_ROWS)
                c_ref[rc * CONV_ROWS:(rc + 1) * CONV_ROWS, lanes] = acc

        def norm_rows(ri, carry):
            rows = pl.ds(pl.multiple_of(ri * NORM_ROWS, NORM_ROWS), NORM_ROWS)
            cv = c_ref[rows, :]
            cc = cv - jnp.mean(cv, axis=-1, keepdims=True)
            var = jnp.mean(cc * cc, axis=-1, keepdims=True)
            ln = cc * lax.rsqrt(var + LN_EPS) * lg_ref[...] + lb_ref[...]
            sw_ref[rows, :] = (ln * _sigmoid(ln)).astype(BF16)
            return carry

        lax.fori_loop(0, t // NORM_ROWS, norm_rows, 0)

    row = pl.BlockSpec((t, d), lambda i: (i, 0))
    prev = pl.BlockSpec((CONV_HALO, d), lambda i: (jnp.maximum(i * hb - 1, 0), 0))
    vec = pl.BlockSpec((1, d), lambda i: (0, 0))
    return pl.pallas_call(
        body,
        name=name,
        grid=(s // t,),
        in_specs=[row, prev, pl.BlockSpec((w, d), lambda i: (0, 0)), vec, vec, vec],
        out_specs=[row, row],
        out_shape=[jax.ShapeDtypeStruct((s, d), F32), jax.ShapeDtypeStruct((s, d), BF16)],
        scratch_shapes=[pltpu.VMEM((CONV_HALO + t, d), F32), pltpu.VMEM((SUBLANES, CONV_HALO + t, lw), F32)],
        compiler_params=pltpu.CompilerParams(dimension_semantics=("parallel",)),
    )(u, u, dw, dw_b, ln_g, ln_b)


def _conv_ln_bwd(name, c, dsw, ln_g, ln_b, after=()):
    s, d = c.shape
    t = _row_tile(s, 256)
    nf = len(after)

    def body(c_ref, dsw_ref, lg_ref, lb_ref, *refs):
        dc_ref, sums_ref = refs[nf:]
        i = pl.program_id(0)
        cv = c_ref[...]
        g = lg_ref[...]
        mu = jnp.mean(cv, axis=-1, keepdims=True)
        cc = cv - mu
        rstd = lax.rsqrt(jnp.mean(cc * cc, axis=-1, keepdims=True) + LN_EPS)
        ch = cc * rstd
        ln = ch * g + lb_ref[...]
        sg = _sigmoid(ln)
        dln = dsw_ref[...] * (sg * (1.0 + ln * (1.0 - sg)))
        dch = dln * g
        dc = rstd * (dch - jnp.mean(dch, axis=-1, keepdims=True) - ch * jnp.mean(dch * ch, axis=-1, keepdims=True))
        dc_ref[...] = dc
        sums = jnp.concatenate(
            [
                jnp.sum(dln * ch, axis=0, keepdims=True),
                jnp.sum(dln, axis=0, keepdims=True),
                jnp.sum(dc, axis=0, keepdims=True),
                jnp.zeros((1, d), F32),
            ],
            axis=0,
        )

        @pl.when(i == 0)
        def _():
            sums_ref[...] = sums

        @pl.when(i > 0)
        def _():
            sums_ref[...] += sums

    row = pl.BlockSpec((t, d), lambda i: (i, 0))
    vec = pl.BlockSpec((1, d), lambda i: (0, 0))
    return pl.pallas_call(
        body,
        name=name,
        grid=(s // t,),
        in_specs=[row, row, vec, vec] + [ANY_SPEC] * nf,
        out_specs=[row, pl.BlockSpec((4, d), lambda i: (0, 0))],
        out_shape=[jax.ShapeDtypeStruct((s, d), F32), jax.ShapeDtypeStruct((4, d), F32)],
        compiler_params=pltpu.CompilerParams(dimension_semantics=("arbitrary",)),
    )(c, dsw, ln_g, ln_b, *after)


def _conv_dw_bwd(name, dc, u, a, gt, dw):
    s, d = dc.shape
    t, hb = _conv_tiles(s)
    w = dw.shape[0]
    lo = CONV_HALO - (w - 1)
    nt = s // t
    lw = min(CONV_LANES, d)

    def body(dc_ref, dcn_ref, u_ref, up_ref, a_ref, gt_ref, dw_ref, dpre_ref, ddw_ref, db_ref, dcat_ref, ucat_ref,
             dsh_ref, ush_ref, ddw_acc, db_acc):
        i = pl.program_id(0)
        dcat_ref[:t, :] = dc_ref[...]
        ucat_ref[CONV_HALO:, :] = u_ref[...]

        @pl.when(i == nt - 1)
        def _():
            dcat_ref[t:, :] = jnp.zeros((CONV_HALO, d), F32)

        @pl.when(i < nt - 1)
        def _():
            dcat_ref[t:, :] = dcn_ref[...]

        @pl.when(i == 0)
        def _():
            ucat_ref[:CONV_HALO, :] = jnp.zeros((CONV_HALO, d), F32)

        @pl.when(i > 0)
        def _():
            ucat_ref[:CONV_HALO, :] = up_ref[...]

        @pl.when(i == 0)
        def _():
            ddw_acc[...] = jnp.zeros(ddw_acc.shape, F32)
            db_acc[...] = jnp.zeros(db_acc.shape, F32)

        def fold(p):
            return functools.reduce(jnp.add, [p[j:j + SUBLANES] for j in range(0, CONV_ROWS, SUBLANES)])

        for lc in range(d // lw):
            lanes = slice(lc * lw, (lc + 1) * lw)
            gate_lanes = slice(d + lc * lw, d + (lc + 1) * lw)
            _shifted_copies(dsh_ref, dcat_ref, lanes, t + CONV_HALO)
            _shifted_copies(ush_ref, ucat_ref, lanes, t + CONV_HALO)
            for rc in range(t // CONV_ROWS):
                r0 = rc * CONV_ROWS
                rows = slice(r0, r0 + CONV_ROWS)
                dcv = dc_ref[rows, lanes]
                du = jnp.zeros((CONV_ROWS, lw), F32)
                for kk in range(w):
                    du = du + dw_ref[kk:kk + 1, lanes] * _shifted(dsh_ref, w - 1 - kk + r0, CONV_ROWS)
                    ddw_acc[kk, :, lanes] += fold(dcv * _shifted(ush_ref, lo + kk + r0, CONV_ROWS))
                av = a_ref[rows, lanes].astype(F32)
                sg = _sigmoid(gt_ref[rows, lanes].astype(F32))
                da = du * sg
                dgt = du * av * sg * (1.0 - sg)
                dpre_ref[rows, lanes] = da.astype(BF16)
                dpre_ref[rows, gate_lanes] = dgt.astype(BF16)
                db_acc[:, lanes] += fold(da)
                db_acc[:, gate_lanes] += fold(dgt)

        @pl.when(i == nt - 1)
        def _():
            ddw_ref[...] = jnp.sum(ddw_acc[...], axis=1)
            db_ref[...] = jnp.sum(db_acc[...], axis=0, keepdims=True)

    row = pl.BlockSpec((t, d), lambda i: (i, 0))
    nxt = pl.BlockSpec((CONV_HALO, d), lambda i: (jnp.minimum((i + 1) * hb, s // CONV_HALO - 1), 0))
    prev = pl.BlockSpec((CONV_HALO, d), lambda i: (jnp.maximum(i * hb - 1, 0), 0))
    return pl.pallas_call(
        body,
        name=name,
        grid=(nt,),
        in_specs=[row, nxt, row, prev, row, row, pl.BlockSpec((w, d), lambda i: (0, 0))],
        out_specs=[
            pl.BlockSpec((t, 2 * d), lambda i: (i, 0)),
            pl.BlockSpec((w, d), lambda i: (0, 0)),
            pl.BlockSpec((1, 2 * d), lambda i: (0, 0)),
        ],
        out_shape=[
            jax.ShapeDtypeStruct((s, 2 * d), BF16),
            jax.ShapeDtypeStruct((w, d), F32),
            jax.ShapeDtypeStruct((1, 2 * d), F32),
        ],
        scratch_shapes=[pltpu.VMEM((t + CONV_HALO, d), F32), pltpu.VMEM((CONV_HALO + t, d), F32)]
        + [pltpu.VMEM((SUBLANES, CONV_HALO + t, lw), F32)] * 2
        + [pltpu.VMEM((w, SUBLANES, d), F32), pltpu.VMEM((SUBLANES, 2 * d), F32)],
        compiler_params=pltpu.CompilerParams(dimension_semantics=("arbitrary",)),
    )(dc, dc, u, u, a, gt, dw)


def _alibi_slopes(n_heads):
    h = jnp.arange(1, n_heads + 1, dtype=F32)
    return jnp.exp2(-8.0 * h / n_heads)


def _band_masks(bq):
    qi = lax.broadcasted_iota(jnp.int32, (bq, bq), 0)
    kj = lax.broadcasted_iota(jnp.int32, (bq, bq), 1)
    return qi - kj, qi - kj + bq


ATTN_GROUP = 16


def _attn_fwd(name, q, k, v, slopes, bq):
    s, dm = q.shape
    nh = dm // HEAD_DIM
    nt = s // bq
    nbr = len(BRANCHES)
    scale = HEAD_DIM ** -0.5
    nt_dims = (((1,), (1,)), ((), ()))

    def body(sl_ref, q_ref, k_ref, v_ref, o_ref, ob_ref, l_ref, tmp, qr, kr, va, orm, lrm, onat, lnat, sbuf, mbuf):
        slope = sl_ref[pl.program_id(0)]
        jc, jp = _band_masks(bq)
        va[:, HEAD_DIM:] = jnp.ones((s, HEAD_DIM), BF16)
        for bi, (win, dil) in enumerate(BRANCHES):
            ll = s // dil
            nblk = ll // bq
            if dil == 1:
                sq, sk = q_ref, k_ref
                va[:, :HEAD_DIM] = v_ref[...]
                d_o, d_l = onat.at[bi], lnat.at[bi]
            else:
                for src, dst, wide in ((q_ref, qr, False), (k_ref, kr, False), (v_ref, va, True)):
                    tmp[...] = src[...].astype(F32)
                    for r in range(dil):
                        part = tmp[pl.ds(r, ll, stride=dil), :].astype(BF16)
                        if wide:
                            dst[r * ll:(r + 1) * ll, :HEAD_DIM] = part
                        else:
                            dst[r * ll:(r + 1) * ll, :] = part
                sq, sk = qr, kr
                d_o, d_l = orm, lrm
            bias_c = jnp.where(jc >= 0, jc.astype(F32) * (slope * dil), 1e30)
            bias_p = jnp.where(jp <= bq, jp.astype(F32) * (slope * dil), 1e30)

            def group(gi, carry):
                rows = []
                for g in range(ATTN_GROUP):
                    ti = gi * ATTN_GROUP + g
                    row = pl.ds(pl.multiple_of(ti * bq, bq), bq)
                    prow = pl.ds(pl.multiple_of(jnp.maximum(ti - 1, 0) * bq, bq), bq)
                    rows.append((row, prow))
                    qh = sq[row, :]
                    sc = lax.dot_general(qh, sk[row, :], nt_dims, preferred_element_type=F32) * scale - bias_c
                    sp = lax.dot_general(qh, sk[prow, :], nt_dims, preferred_element_type=F32) * scale - bias_p
                    sp = jnp.where(lax.rem(ti, nblk) > 0, sp, -1e30)
                    sbuf[g, :, :bq] = sc
                    sbuf[g, :, bq:] = sp
                    mbuf[g] = jnp.maximum(jnp.max(sc, axis=-1, keepdims=True), jnp.max(sp, axis=-1, keepdims=True))
                for g, (row, prow) in enumerate(rows):
                    mx = mbuf[g]
                    p = jnp.exp(sbuf[g] - mx).astype(BF16)
                    ov = jnp.dot(p[:, :bq], va[row, :], preferred_element_type=F32)
                    ov = ov + jnp.dot(p[:, bq:], va[prow, :], preferred_element_type=F32)
                    den = ov[:, HEAD_DIM:]
                    d_o[row, :] = ov[:, :HEAD_DIM] / den
                    d_l[row, :] = mx + jnp.log(den)
                return carry

            lax.fori_loop(0, nt // ATTN_GROUP, group, 0)
            if dil > 1:
                for r in range(dil):
                    onat[bi, pl.ds(r, ll, stride=dil), :] = orm[r * ll:(r + 1) * ll, :]
                    lnat[bi, pl.ds(r, ll, stride=dil), :] = lrm[r * ll:(r + 1) * ll, :]

        def merge(ti, carry):
            rows = pl.ds(pl.multiple_of(ti * bq, bq), bq)
            ls = [lnat[bi, rows, :] for bi in range(nbr)]
            mx = functools.reduce(jnp.maximum, ls)
            es = [jnp.exp(l - mx) for l in ls]
            tot = functools.reduce(jnp.add, es)
            inv = 1.0 / tot
            o = functools.reduce(jnp.add, [e * inv * onat[bi, rows, :] for bi, e in enumerate(es)])
            o_ref[rows, :] = o
            ob_ref[rows, :] = o.astype(BF16)
            l_ref[rows, :] = mx + jnp.log(tot)
            return carry

        lax.fori_loop(0, nt, merge, 0)

    head = pl.BlockSpec((s, HEAD_DIM), lambda h: (0, h))
    return pl.pallas_call(
        body,
        name=name,
        grid=(nh,),
        in_specs=[pl.BlockSpec(memory_space=pltpu.SMEM), head, head, head],
        out_specs=[head, head, head],
        out_shape=[jax.ShapeDtypeStruct((s, dm), F32), jax.ShapeDtypeStruct((s, dm), BF16),
                   jax.ShapeDtypeStruct((s, dm), F32)],
        scratch_shapes=[pltpu.VMEM((s, HEAD_DIM), F32)] + [pltpu.VMEM((s, HEAD_DIM), BF16)] * 2
        + [pltpu.VMEM((s, 2 * HEAD_DIM), BF16)] + [pltpu.VMEM((s, HEAD_DIM), F32)] * 2
        + [pltpu.VMEM((nbr, s, HEAD_DIM), F32)] * 2
        + [pltpu.VMEM((ATTN_GROUP, bq, 2 * bq), F32), pltpu.VMEM((ATTN_GROUP, bq, 1), F32)],
        compiler_params=pltpu.CompilerParams(dimension_semantics=("parallel",)),
    )(slopes, q, k, v)


def _attn_bwd(name, q, k, v, o, lse, do, slopes, bq):
    s, dm = q.shape
    nh = dm // HEAD_DIM
    nt = s // bq
    scale = HEAD_DIM ** -0.5
    nt_dims = (((1,), (1,)), ((), ()))
    tn_dims = (((0,), (0,)), ((), ()))

    def body(sl_ref, q_ref, k_ref, v_ref, o_ref, l_ref, do_ref, dq_ref, dk_ref, dv_ref,
             tmp, qr, kr, vr, dor, lr, dlr, dln, dqr, dkr, dvr, aq, ak, av, pbuf, dsbuf):
        slope = sl_ref[pl.program_id(0)]
        jc, jp = _band_masks(bq)

        def delta(ti, carry):
            rows = pl.ds(pl.multiple_of(ti * bq, bq), bq)
            dl = jnp.sum(do_ref[rows, :].astype(F32) * o_ref[rows, :], axis=-1, keepdims=True)
            dln[rows, :] = jnp.broadcast_to(dl, (bq, HEAD_DIM))
            return carry

        lax.fori_loop(0, nt, delta, 0)

        for bi, (win, dil) in enumerate(BRANCHES):
            ll = s // dil
            nblk = ll // bq
            if dil == 1:
                sq, sk, sv, sdo, sl, sdl = q_ref, k_ref, v_ref, do_ref, l_ref, dln
                gq, gk, gv = aq, ak, av
            else:
                for src, dst in ((q_ref, qr), (k_ref, kr), (v_ref, vr), (do_ref, dor)):
                    tmp[...] = src[...].astype(F32)
                    for r in range(dil):
                        dst[r * ll:(r + 1) * ll, :] = tmp[pl.ds(r, ll, stride=dil), :].astype(BF16)
                for r in range(dil):
                    lr[r * ll:(r + 1) * ll, :] = l_ref[pl.ds(r, ll, stride=dil), :]
                    dlr[r * ll:(r + 1) * ll, :] = dln[pl.ds(r, ll, stride=dil), :]
                sq, sk, sv, sdo, sl, sdl = qr, kr, vr, dor, lr, dlr
                gq, gk, gv = dqr, dkr, dvr
            bias_c = jnp.where(jc >= 0, jc.astype(F32) * (slope * dil), 1e30)
            bias_p = jnp.where(jp <= bq, jp.astype(F32) * (slope * dil), 1e30)

            def group(gi, carry):
                rows = []
                for g in range(ATTN_GROUP):
                    ti = gi * ATTN_GROUP + g
                    row = pl.ds(pl.multiple_of(ti * bq, bq), bq)
                    prow = pl.ds(pl.multiple_of(jnp.maximum(ti - 1, 0) * bq, bq), bq)
                    rows.append((row, prow))
                    has_prev = lax.rem(ti, nblk) > 0
                    qh, doh = sq[row, :], sdo[row, :]
                    lc = sl[row, :][:, :1]
                    dl = sdl[row, :][:, :1]
                    for half, kv_rows, bias in ((0, row, bias_c), (1, prow, bias_p)):
                        sc = lax.dot_general(qh, sk[kv_rows, :], nt_dims, preferred_element_type=F32) * scale - bias
                        if half:
                            sc = jnp.where(has_prev, sc, -1e30)
                        p = jnp.exp(sc - lc)
                        dp = lax.dot_general(doh, sv[kv_rows, :], nt_dims, preferred_element_type=F32)
                        pbuf[g, :, half * bq:(half + 1) * bq] = p.astype(BF16)
                        dsbuf[g, :, half * bq:(half + 1) * bq] = (p * (dp - dl) * scale).astype(BF16)
                carry_k = carry_v = None
                for g, (row, prow) in enumerate(rows):
                    qh, doh = sq[row, :], sdo[row, :]
                    ds_c, ds_p = dsbuf[g, :, :bq], dsbuf[g, :, bq:]
                    p_c, p_p = pbuf[g, :, :bq], pbuf[g, :, bq:]
                    dq = jnp.dot(ds_c, sk[row, :], preferred_element_type=F32)
                    dq = dq + jnp.dot(ds_p, sk[prow, :], preferred_element_type=F32)
                    gq[row, :] = dq
                    dk_p = lax.dot_general(ds_p, qh, tn_dims, preferred_element_type=F32)
                    dv_p = lax.dot_general(p_p, doh, tn_dims, preferred_element_type=F32)
                    if g == 0:
                        @pl.when(gi > 0)
                        def _():
                            gk[prow, :] += dk_p
                            gv[prow, :] += dv_p
                    else:
                        gk[rows[g - 1][0], :] = carry_k + dk_p
                        gv[rows[g - 1][0], :] = carry_v + dv_p
                    carry_k = lax.dot_general(ds_c, qh, tn_dims, preferred_element_type=F32)
                    carry_v = lax.dot_general(p_c, doh, tn_dims, preferred_element_type=F32)
                gk[rows[-1][0], :] = carry_k
                gv[rows[-1][0], :] = carry_v
                return carry

            lax.fori_loop(0, nt // ATTN_GROUP, group, 0)
            if dil > 1:
                for acc, rm in ((aq, dqr), (ak, dkr), (av, dvr)):
                    for r in range(dil):
                        acc[pl.ds(r, ll, stride=dil), :] += rm[r * ll:(r + 1) * ll, :]

        dq_ref[...] = aq[...].astype(BF16)
        dk_ref[...] = ak[...].astype(BF16)
        dv_ref[...] = av[...].astype(BF16)

    head = pl.BlockSpec((s, HEAD_DIM), lambda h: (0, h))
    f32buf = pltpu.VMEM((s, HEAD_DIM), F32)
    b16buf = pltpu.VMEM((s, HEAD_DIM), BF16)
    return pl.pallas_call(
        body,
        name=name,
        grid=(nh,),
        in_specs=[pl.BlockSpec(memory_space=pltpu.SMEM)] + [head] * 6,
        out_specs=[head] * 3,
        out_shape=[jax.ShapeDtypeStruct((s, dm), BF16)] * 3,
        scratch_shapes=[f32buf] + [b16buf] * 4 + [f32buf] * 9 + [pltpu.VMEM((ATTN_GROUP, bq, 2 * bq), BF16)] * 2,
        compiler_params=pltpu.CompilerParams(dimension_semantics=("parallel",)),
    )(slopes, q, k, v, o, lse, do)


def _ep_id(accs, ex):
    return [accs[0]]


def _ep_all(accs, ex):
    return list(accs)


def _ep_sum(accs, ex):
    return [accs[0] + accs[1]]


def _ep_add(accs, ex):
    return [accs[0] + ex[0].astype(F32)]


def _ep_bias_res(accs, ex):
    return [accs[0] + ex[0] + ex[1]]


def _ep_glu(accs, ex):
    a = accs[0] + ex[0]
    gt = accs[1] + ex[1]
    return [a * _sigmoid(gt), a, gt]


def _ep_swiglu(accs, ex):
    g, u = accs
    return [g, u, g * _sigmoid(g) * u]


def _ep_swiglu_bwd(accs, ex):
    dact = accs[0]
    g = ex[0].astype(F32)
    u = ex[1].astype(F32)
    sg = _sigmoid(g)
    return [dact * u * (sg * (1.0 + g * (1.0 - sg))), dact * g * sg]


def _ffn_fwd(tag, h, gain, get_gate_up, get_down):
    s, d = h.shape
    (n,) = _rms_fwd(f"{tag}_norm", h, gain)
    wg, wu = get_gate_up(n)
    f = wg.shape[-1]
    g, u, act = _mm(f"{tag}_gate_up", "nn", n, [(wg, None, 0), (wu, None, 0)], _ep_swiglu,
                    [(BF16,), (BF16,), (BF16,)], m=s, n=f, k=d)
    wd = get_down(act)
    (out,) = _mm(f"{tag}_down", "nn", act, [(wd, None, 0)], _ep_add, [(F32,)], m=s, n=d, k=f,
                 extras=[(h, "mn", 0)])
    return out, (n, g, u, act), dict(gate=wg, up=wu, down=wd)


def _ffn_bwd(tag, h_in, gain, wg, wu, wd, saved, dh, dhb, after, emit):
    s, d = h_in.shape
    f = wg.shape[-1]
    n, g, u, act = saved
    dg, du = _mm(f"{tag}_bwd_dact", "nt", dhb, [(wd, None, 0)], _ep_swiglu_bwd, [(BF16,), (BF16,)],
                 m=s, n=f, k=d, extras=[(g, "mn", 0), (u, "mn", 0)], after=after)
    (dwd,) = _mm(f"{tag}_bwd_dwd", "tn", act, [(dhb, None, 0)], _ep_id, [(BF16,)], m=f, n=d, k=s,
                 bm=f // 4)
    pin = emit("d", dict(down=dwd))
    dwg, dwu = _mm(f"{tag}_bwd_dwgu", "tn", n, [(dg, None, 0), (du, None, 0)], _ep_all, [(BF16,), (BF16,)],
                   m=d, n=f, k=s, after=pin)
    pin = emit("gu", dict(gate=dwg, up=dwu))
    (dn,) = _mm(f"{tag}_bwd_dn", "nt", [dg, du], [(wg, None, 0), (wu, None, 0)], _ep_sum, [(BF16,)],
                m=s, n=d, k=f, bm=512, bn=256, after=pin)
    dx, dxb, dgain, cs = _rms_bwd(f"{tag}_bwd_norm", h_in, gain, [dn], dh)
    return dx, dxb, dgain, cs


def _local_step(x, target, w, fetch, emit, prefetch=lambda group, after: None, after=()):
    s, d = x.shape
    nh = d // HEAD_DIM
    bq = BRANCHES[0][0] // BRANCHES[0][1]
    assert all(win // dil == bq for win, dil in BRANCHES)
    assert BRANCHES[0][1] == 1 and all(dil > 1 for _, dil in BRANCHES[1:])
    slopes = _alibi_slopes(nh)
    nd = d // 512 if d >= 512 else 1
    bn = d // nd

    (n1,) = _rms_fwd("a_norm", x, w["a_norm_g"], after=after)
    w_conv1 = fetch("conv1", n1)["conv_w1"]
    glu, a, gt = _mm("conv_pw1_glu", "nn", n1, [(w_conv1, None, 0), (w_conv1, None, nd)], _ep_glu,
                     [(F32,), (BF16,), (BF16,)], m=s, n=d, k=d, bn=bn,
                     extras=[(w["conv_b1"], "n", 0), (w["conv_b1"], "n", nd)])
    c, sw = _dwconv_fwd("conv_dw_ln", glu, w["conv_dw"], w["conv_dw_b"], w["conv_ln_g"], w["conv_ln_b"])
    w_conv2 = fetch("conv2", sw)["conv_w2"]
    (h1,) = _mm("conv_pw2", "nn", sw, [(w_conv2, None, 0)], _ep_bias_res, [(F32,)], m=s, n=d, k=d,
                extras=[(w["conv_b2"], "n", 0), (x, "mn", 0)], bn=WIDE_BN)
    def gate_up0(after):
        wts = fetch("ffn0gu", after)
        return wts["gate"], wts["up"]

    h2, ffn0, wf0 = _ffn_fwd("ffn0", h1, w["ffn_norm_g"][0:1], gate_up0, lambda after: fetch("ffn0d", after)["down"])
    wa = fetch("attn", h2)
    kvn, qn = _rms_fwd("kvq_norm", h2, jnp.concatenate([w["kv_norm_g"], w["b_norm_g"]], axis=0))
    k, v = _mm("kv_proj", "nn", kvn, [(wa["w_k"], None, 0), (wa["w_v"], None, 0)], _ep_all, [(BF16,), (BF16,)],
               m=s, n=d, k=d, bn=WIDE_BN)
    (q,) = _mm("q_proj", "nn", qn, [(wa["w_q"], None, 0)], _ep_id, [(BF16,)], m=s, n=d, k=d, bn=WIDE_BN)
    att, attb, lse = _attn_fwd("attn_fwd", q, k, v, slopes, bq)
    prefetch("ffn1", attb)
    (h3,) = _mm("o_proj", "nn", attb, [(wa["w_o"], None, 0)], _ep_add, [(F32,)], m=s, n=d, k=d,
                extras=[(h2, "mn", 0)], bn=WIDE_BN)
    got1 = {}

    def gate_up1(after):
        got1.update(fetch("ffn1", after))
        return got1["gate"], got1["up"]

    h4, ffn1, wf1 = _ffn_fwd("ffn1", h3, w["ffn_norm_g"][1:2], gate_up1, lambda after: got1["down"])
    dh4, dh4b, d_final_g, loss_cols = _final_loss("final_loss", h4, w["final_norm_g"], target)

    g = {}
    ga = {}
    dh3, dh3b, dgain1, _ = _ffn_bwd("ffn1", h3, w["ffn_norm_g"][1:2], wf1["gate"], wf1["up"], wf1["down"], ffn1,
                                    dh4, dh4b, (), lambda part, grads: emit("ffn1" + part, grads))
    (datt,) = _mm("o_proj_bwd_dx", "nt", dh3b, [(wa["w_o"], None, 0)], _ep_id, [(BF16,)], m=s, n=d, k=d, bn=WIDE_BN)
    (ga["w_o"],) = _mm("o_proj_bwd_dw", "tn", attb, [(dh3b, None, 0)], _ep_id, [(BF16,)], m=d, n=d, k=s, bn=WIDE_BN)
    dq, dk, dv = _attn_bwd("attn_bwd", q, k, v, att, lse, datt, slopes, bq)
    (ga["w_q"],) = _mm("q_proj_bwd_dw", "tn", qn, [(dq, None, 0)], _ep_id, [(BF16,)], m=d, n=d, k=s, bn=WIDE_BN)
    ga["w_k"], ga["w_v"] = _mm("kv_proj_bwd_dw", "tn", kvn, [(dk, None, 0), (dv, None, 0)], _ep_all,
                               [(BF16,), (BF16,)], m=d, n=d, k=s)
    pin = emit("attn", ga)
    (dqn,) = _mm("q_proj_bwd_dx", "nt", dq, [(wa["w_q"], None, 0)], _ep_id, [(BF16,)], m=s, n=d, k=d, after=pin,
                 bn=WIDE_BN)
    (dkvn,) = _mm("kv_proj_bwd_dx", "nt", [dk, dv], [(wa["w_k"], None, 0), (wa["w_v"], None, 0)], _ep_sum, [(BF16,)],
                  m=s, n=d, k=d, bn=WIDE_BN)
    dh2, dh2b, dg_kvq, _ = _rms_bwd("kvq_norm_bwd", h2, jnp.concatenate([w["kv_norm_g"], w["b_norm_g"]], axis=0),
                                    [dkvn, dqn], dh3)
    dh1, dh1b, dgain0, cs_h1 = _ffn_bwd("ffn0", h1, w["ffn_norm_g"][0:1], wf0["gate"], wf0["up"], wf0["down"], ffn0,
                                        dh2, dh2b, (), lambda part, grads: emit("ffn0" + part, grads))
    (dsw,) = _mm("conv_pw2_bwd_dx", "nt", dh1b, [(w_conv2, None, 0)], _ep_id, [(F32,)], m=s, n=d, k=d, bn=WIDE_BN)
    (dw2,) = _mm("conv_pw2_bwd_dw", "tn", sw, [(dh1b, None, 0)], _ep_id, [(BF16,)], m=d, n=d, k=s, bn=WIDE_BN)
    pin = emit("conv2", dict(conv_w2=dw2))
    dc, ln_sums = _conv_ln_bwd("conv_ln_bwd", c, dsw, w["conv_ln_g"], w["conv_ln_b"], after=pin)
    dpre, ddw, db1 = _conv_dw_bwd("conv_dw_bwd", dc, glu, a, gt, w["conv_dw"])
    (dw1,) = _mm("conv_pw1_bwd_dw", "tn", n1, [(dpre, None, 0)], _ep_id, [(BF16,)], m=d, n=2 * d, k=s)
    pin = emit("conv1", dict(conv_w1=dw1))
    (dn1,) = _mm("conv_pw1_bwd_dx", "nt", dpre, [(w_conv1, None, 0)], _ep_id, [(BF16,)], m=s, n=d, k=2 * d,
                 after=pin)
    dx, _, d_a_norm, _ = _rms_bwd("a_norm_bwd", x, w["a_norm_g"], [dn1], dh1)

    g.update(
        a_norm_g=d_a_norm, conv_b1=db1, conv_dw=ddw, conv_dw_b=ln_sums[2:3], conv_ln_g=ln_sums[0:1],
        conv_ln_b=ln_sums[1:2], conv_b2=cs_h1, kv_norm_g=dg_kvq[0:1], b_norm_g=dg_kvq[1:2],
        ffn_norm_g=jnp.concatenate([dgain0, dgain1], axis=0), final_norm_g=d_final_g,
    )
    return loss_cols, dx, g


HBM_SPEC = pl.BlockSpec(memory_space=pltpu.HBM)


def _mesh_place():
    x, y, c = lax.axis_index("x"), lax.axis_index("y"), lax.axis_index("c")
    chips = [(1 - x, y), (x, 1 - y), (1 - x, 1 - y)]
    return x, y, c, chips


def _shard_view(ref, kind, s, half=None):
    rows, cols = ref.shape
    if kind == "col":
        cw = cols // N_CHIPS
        if half is None:
            return ref.at[pl.ds(0, rows), pl.ds(s * cw, cw)]
        return ref.at[pl.ds(half * (rows // 2), rows // 2), pl.ds(s * cw, cw)]
    r = rows // N_CHIPS
    if half is None:
        return ref.at[pl.ds(s * r, r), pl.ds(0, cols)]
    return ref.at[pl.ds(s * r + half * (r // 2), r // 2), pl.ds(0, cols)]


def _full_shape(shard, kind):
    r, cw = shard.shape[-2:]
    return (r, cw * N_CHIPS) if kind == "col" else (r * N_CHIPS, cw)


def _gather_weights(shards, kinds, after=()):
    nt = len(shards)
    nf = len(after)
    fulls = [jax.ShapeDtypeStruct(_full_shape(sh, kind), sh.dtype) for sh, kind in zip(shards, kinds)]

    def body(*refs):
        src = refs[:nt]
        dst = refs[nt + nf:2 * nt + nf]
        send, recv, fsend, frecv, local = refs[2 * nt + nf:]
        x, y, c, chips = _mesh_place()
        s = 2 * x + y
        sib = (x, y, 1 - c)

        def half_of_shard(t):
            r, cw = src[t].shape
            return src[t].at[pl.ds(c * (r // 2), r // 2), pl.ds(0, cw)]

        locals_ = [pltpu.make_async_copy(src[t], _shard_view(dst[t], kinds[t], s), local.at[t]) for t in range(nt)]
        for cp in locals_:
            cp.start()
        sends = []
        for t in range(nt):
            for j, chip in enumerate(chips):
                cp = pltpu.make_async_remote_copy(
                    src_ref=half_of_shard(t), dst_ref=_shard_view(dst[t], kinds[t], s, c),
                    send_sem=send.at[t, j], recv_sem=recv.at[t, j], device_id=(*chip, c), device_id_type=MESH)
                cp.start()
                sends.append(cp)
        for t in range(nt):
            for j, (px, py) in enumerate(chips):
                landed = _shard_view(dst[t], kinds[t], 2 * px + py, c)
                pltpu.make_async_remote_copy(
                    src_ref=half_of_shard(t), dst_ref=landed, send_sem=send.at[t, j], recv_sem=recv.at[t, j],
                    device_id=(px, py, c), device_id_type=MESH).wait_recv()
                cp = pltpu.make_async_remote_copy(
                    src_ref=landed, dst_ref=landed, send_sem=fsend.at[t, j], recv_sem=frecv.at[t, j],
                    device_id=sib, device_id_type=MESH)
                cp.start()
                sends.append(cp)
        for t in range(nt):
            for j, (px, py) in enumerate(chips):
                other = _shard_view(dst[t], kinds[t], 2 * px + py, 1 - c)
                pltpu.make_async_remote_copy(
                    src_ref=other, dst_ref=other, send_sem=fsend.at[t, j], recv_sem=frecv.at[t, j],
                    device_id=sib, device_id_type=MESH).wait_recv()
        for cp in sends:
            cp.wait_send()
        for cp in locals_:
            cp.wait()

    return pl.pallas_call(
        body,
        name="gather_weights",
        in_specs=[HBM_SPEC] * nt + [ANY_SPEC] * nf,
        out_specs=[HBM_SPEC] * nt,
        out_shape=fulls,
        scratch_shapes=[pltpu.SemaphoreType.DMA((nt, 3))] * 4 + [pltpu.SemaphoreType.DMA((nt,))],
    )(*shards, *after)


def _row_blocks(rows, want=512):
    nb = 1
    while rows // nb > want or rows % nb or (rows // nb) % 16:
        nb += 1
        if nb > rows:
            return rows, 1
    return rows // nb, nb


def _place_own(name, src, kind, land, ids, lead=0):
    r, cw = src.shape[-2:]
    tr, nb = _row_blocks(r)
    if src.ndim == 3:
        src_spec = pl.BlockSpec((None, tr, cw), lambda i, ids_ref: (lead, i, 0))
    else:
        src_spec = pl.BlockSpec((tr, cw), lambda i, ids_ref: (i, 0))
    if kind == "col":
        dst_spec = pl.BlockSpec((tr, cw), lambda i, ids_ref: (i, ids_ref[0]))
    else:
        dst_spec = pl.BlockSpec((tr, cw), lambda i, ids_ref: (ids_ref[0] * nb + i, 0))

    def body(ids_ref, s_ref, o_ref):
        o_ref[...] = s_ref[...].astype(o_ref.dtype)

    return pl.pallas_call(
        body,
        name=name,
        grid_spec=pltpu.PrefetchScalarGridSpec(num_scalar_prefetch=1, grid=(nb,), in_specs=[src_spec],
                                               out_specs=dst_spec),
        out_shape=land,
        compiler_params=pltpu.CompilerParams(dimension_semantics=("parallel",)),
    )(ids, src)


SEM_SPEC = pl.BlockSpec(memory_space=pltpu.SEMAPHORE)
SIDE_EFFECT = pltpu.SideEffectType.DATAFLOW_SIDE_EFFECTING


def _copies_start(name, srcs, lands, copies, n_sems, after):
    ns, nl, nf = len(srcs), len(lands), len(after)

    def body(*refs):
        src, land = refs[:ns], refs[ns:ns + nl]
        send, recv = refs[ns + nl + nf], refs[ns + nl + nf + 1]
        pin = refs[-1]
        for cp in copies(src, land, send, recv, _mesh_place()):
            cp.start()
        pin[...] = jnp.zeros_like(pin)

    arrs = list(srcs) + list(lands)
    res = pl.pallas_call(
        body,
        name=name,
        in_specs=[HBM_SPEC] * (ns + nl) + [ANY_SPEC] * nf,
        out_specs=[SEM_SPEC, SEM_SPEC] + [HBM_SPEC] * (ns + nl) + [pl.BlockSpec(memory_space=pltpu.VMEM)],
        out_shape=[pltpu.SemaphoreType.DMA((n_sems,)), pltpu.SemaphoreType.DMA((n_sems,))]
        + [pltpu.HBM(a.shape, a.dtype) for a in arrs] + [jax.ShapeDtypeStruct((8, 128), F32)],
        input_output_aliases={i: 2 + i for i in range(ns + nl)},
        compiler_params=pltpu.CompilerParams(has_side_effects=SIDE_EFFECT),
    )(*[pltpu.with_memory_space_constraint(a, pltpu.HBM) for a in arrs], *after)
    return res[0], res[1], list(res[2:2 + ns]), list(res[2 + ns:2 + ns + nl]), res[-1]


def _copies_wait(name, send, recv, srcs, lands, copies, after):
    ns, nl, nf = len(srcs), len(lands), len(after)

    def body(*refs):
        src, land = refs[:ns], refs[ns:ns + nl]
        send_sems, recv_sems = refs[ns + nl], refs[ns + nl + 1]
        cps = copies(src, land, send_sems, recv_sems, _mesh_place())
        for cp in cps:
            cp.wait_send()
        for cp in cps:
            cp.wait_recv()

    arrs = list(srcs) + list(lands)
    res = pl.pallas_call(
        body,
        name=name,
        in_specs=[HBM_SPEC] * (ns + nl) + [SEM_SPEC, SEM_SPEC] + [ANY_SPEC] * nf,
        out_specs=[HBM_SPEC] * (ns + nl),
        out_shape=[pltpu.HBM(a.shape, a.dtype) for a in arrs],
        input_output_aliases={i: i for i in range(ns + nl)},
        compiler_params=pltpu.CompilerParams(has_side_effects=SIDE_EFFECT),
    )(*arrs, send, recv, *after)
    return list(res[:ns]), list(res[ns:])


def _gather_copies(kinds, halves):
    def copies(src, land, send, recv, place):
        x, y, c, chips = place
        s = 2 * x + y
        mine = [_shard_view(land[t], kinds[t], s, c if halves else None) for t in range(len(kinds))]
        return [
            pltpu.make_async_remote_copy(
                src_ref=mine[t], dst_ref=mine[t], send_sem=send.at[3 * t + j], recv_sem=recv.at[3 * t + j],
                device_id=(px, py, c), device_id_type=MESH)
            for t in range(len(kinds)) for j, (px, py) in enumerate(chips)
        ]

    return copies


def _gather_pass_copies(kinds):
    def copies(src, land, send, recv, place):
        x, y, c, chips = place
        views = [_shard_view(land[t], kinds[t], 2 * px + py, c) for t in range(len(kinds)) for px, py in chips]
        return [
            pltpu.make_async_remote_copy(src_ref=v, dst_ref=v, send_sem=send.at[i], recv_sem=recv.at[i],
                                         device_id=(x, y, 1 - c), device_id_type=MESH)
            for i, v in enumerate(views)
        ]

    return copies


def _gather_pass_on(name, lands, kinds):
    nt = len(lands)

    def body(*refs):
        buf = refs[nt:2 * nt]
        send, recv = refs[2 * nt:]
        x, y, c, chips = _mesh_place()
        sib = (x, y, 1 - c)
        sends = []
        for t in range(nt):
            for j, (px, py) in enumerate(chips):
                mine = _shard_view(buf[t], kinds[t], 2 * px + py, c)
                cp = pltpu.make_async_remote_copy(src_ref=mine, dst_ref=mine, send_sem=send.at[t, j],
                                                  recv_sem=recv.at[t, j], device_id=sib, device_id_type=MESH)
                cp.start()
                sends.append(cp)
        for t in range(nt):
            for j, (px, py) in enumerate(chips):
                theirs = _shard_view(buf[t], kinds[t], 2 * px + py, 1 - c)
                pltpu.make_async_remote_copy(src_ref=theirs, dst_ref=theirs, send_sem=send.at[t, j],
                                             recv_sem=recv.at[t, j], device_id=sib, device_id_type=MESH).wait_recv()
        for cp in sends:
            cp.wait_send()

    return pl.pallas_call(
        body,
        name=name,
        in_specs=[HBM_SPEC] * nt,
        out_specs=[HBM_SPEC] * nt,
        out_shape=[jax.ShapeDtypeStruct(a.shape, a.dtype) for a in lands],
        input_output_aliases={i: i for i in range(nt)},
        scratch_shapes=[pltpu.SemaphoreType.DMA((nt, 3))] * 2,
    )(*lands)


def _grad_part(ref, kind, s):
    return ref if kind == "all" else _shard_view(ref, kind, s)


def _grad_copies(kinds):
    def peers(place):
        x, y, c, chips = place
        return [(x, y, 1 - c)] + [(px, py, c) for px, py in chips]

    def copies(src, land, send, recv, place):
        x, y, c, chips = place
        me = 4 * x + 2 * y + c
        return [
            pltpu.make_async_remote_copy(
                src_ref=_grad_part(src[t], kinds[t], 2 * px + py), dst_ref=land[t].at[me],
                send_sem=send.at[GRAD_PEERS * t + k], recv_sem=recv.at[GRAD_PEERS * t + k], device_id=(px, py, pc),
                device_id_type=MESH)
            for t in range(len(kinds)) for k, (px, py, pc) in enumerate(peers(place))
        ]

    return copies


def _pass_copies(n):
    def copies(src, land, send, recv, place):
        x, y, c, chips = place
        return [
            pltpu.make_async_remote_copy(
                src_ref=land[t].at[4 * px + 2 * py + c], dst_ref=land[t].at[4 * px + 2 * py + c],
                send_sem=send.at[3 * t + j], recv_sem=recv.at[3 * t + j], device_id=(x, y, 1 - c),
                device_id_type=MESH)
            for t in range(n) for j, (px, py) in enumerate(chips)
        ]

    return copies


GRAD_PEERS = 4


def _land_shape(grad, kind):
    rows, cols = grad.shape
    if kind == "col":
        return (N_DEV, rows, cols // N_CHIPS)
    if kind == "row":
        return (N_DEV, rows // N_CHIPS, cols)
    return (N_DEV, rows, cols)


def _adamw_reduce(name, contrib, own, kind, ids, w, m, v, layer=None, prev=None):
    rows, cols = w.shape[-2:]
    t = rows
    for cand in (256, 128):
        if rows % cand == 0 and cand * cols <= 256 * 1408:
            t = cand
            break
    nb = rows // t
    c1 = 1.0 - ADAM_B1 ** ADAM_STEP
    c2 = 1.0 - ADAM_B2 ** ADAM_STEP

    n_prev = 0 if prev is None else 4

    def body(ids_ref, c_ref, own_ref, w_ref, m_ref, v_ref, *refs):
        g_ref, d_ref, nm_ref, nv_ref = refs[n_prev:]
        me = ids_ref[1]
        mine = own_ref[...].astype(F32)
        g = None
        for q in range(N_DEV):
            term = jnp.where(me == q, mine, c_ref[q].astype(F32))
            g = term if g is None else g + term
        nm = ADAM_B1 * m_ref[...] + (1.0 - ADAM_B1) * g
        nv = ADAM_B2 * v_ref[...] + (1.0 - ADAM_B2) * (g * g)
        g_ref[...] = g
        nm_ref[...] = nm
        nv_ref[...] = nv
        d_ref[...] = -ADAM_LR * ((nm / c1) / (jnp.sqrt(nv / c2) + ADAM_EPS) + ADAM_WD * w_ref[...])

    if layer is None:
        blk = pl.BlockSpec((t, cols), lambda i, ids_ref: (i, 0))
    else:
        blk = pl.BlockSpec((None, t, cols), lambda i, ids_ref: (layer, i, 0))
    if kind == "col":
        own_spec = pl.BlockSpec((t, cols), lambda i, ids_ref: (i, ids_ref[0]))
    elif kind == "row":
        own_spec = pl.BlockSpec((t, cols), lambda i, ids_ref: (ids_ref[0] * nb + i, 0))
    else:
        own_spec = pl.BlockSpec((t, cols), lambda i, ids_ref: (i, 0))
    return pl.pallas_call(
        body,
        name=name,
        grid_spec=pltpu.PrefetchScalarGridSpec(
            num_scalar_prefetch=1,
            grid=(nb,),
            in_specs=[pl.BlockSpec((N_DEV, t, cols), lambda i, ids_ref: (0, i, 0)), own_spec, blk, blk, blk]
            + [ANY_SPEC] * n_prev,
            out_specs=[blk] * 4,
        ),
        out_shape=[jax.ShapeDtypeStruct(w.shape, F32)] * 4,
        input_output_aliases={6 + i: i for i in range(n_prev)},
        compiler_params=pltpu.CompilerParams(dimension_semantics=("parallel",)),
    )(ids, contrib, own, w, m, v, *(prev or ()))


WEIGHT_NAMES = ("a_norm_g", "conv_w1", "conv_b1", "conv_dw", "conv_dw_b", "conv_ln_g", "conv_ln_b", "conv_w2",
                "conv_b2", "kv_norm_g", "w_k", "w_v", "b_norm_g", "w_q", "w_o", "ffn_norm_g", "ffn_w_gate",
                "ffn_w_up", "ffn_w_down", "final_norm_g")
GROUPS = {
    "conv2": (("conv_w2", "conv_w2", None, "row"),),
    "ffn0": (("gate", "ffn_w_gate", 0, "col"), ("up", "ffn_w_up", 0, "col"), ("down", "ffn_w_down", 0, "row")),
    "attn": (("w_k", "w_k", None, "row"), ("w_v", "w_v", None, "row"), ("w_q", "w_q", None, "row"),
             ("w_o", "w_o", None, "row")),
    "ffn1": (("gate", "ffn_w_gate", 1, "col"), ("up", "ffn_w_up", 1, "col"), ("down", "ffn_w_down", 1, "row")),
}
GROUPS["conv1"] = (("conv_w1", "conv_w1", None, "col"),)
for _layer in (0, 1):
    GROUPS[f"ffn{_layer}d"] = tuple(it for it in GROUPS[f"ffn{_layer}"] if it[0] == "down")
    GROUPS[f"ffn{_layer}gu"] = tuple(it for it in GROUPS[f"ffn{_layer}"] if it[0] != "down")
FETCH_ORDER = ("conv1", "conv2", "ffn0gu", "ffn0d", "attn", "ffn1")
HALVED = ("conv1", "conv2", "ffn0gu", "ffn0d", "ffn1")
EMIT_ORDER = ("ffn1d", "ffn1gu", "attn", "ffn0d", "ffn0gu", "conv2", "conv1", "vec")
RETIRE_AT = {"attn": ("ffn1d", "ffn1gu"), "ffn0d": ("attn",), "conv1": ("ffn0d", "ffn0gu", "conv2")}
PACKED = (("a_norm_g", 0, 1), ("conv_b1", 8, 2), ("conv_dw", 16, CONV_WIDTH), ("conv_dw_b", 48, 1),
          ("conv_ln_g", 56, 1), ("conv_ln_b", 64, 1), ("conv_b2", 72, 1))
PACK_ROWS = 80
WHOLE = (("kv_norm_g", 0, 1), ("b_norm_g", 1, 1), ("ffn_norm_g", 2, 2), ("final_norm_g", 4, 1))
WHOLE_ROWS = 8


def _pack_rows(parts, total, width):
    out, at = [], 0
    for arr, first in parts:
        if first > at:
            out.append(jnp.zeros((first - at, width), F32))
        rows8 = -(-arr.shape[0] // 8) * 8
        out.append(jnp.pad(arr, ((0, rows8 - arr.shape[0]), (0, 0))))
        at = first + rows8
    if total > at:
        out.append(jnp.zeros((total - at, width), F32))
    return jnp.concatenate(out, axis=0)


def kernel(x, a_norm_g, conv_w1, conv_b1, conv_dw, conv_dw_b, conv_ln_g, conv_ln_b, conv_w2, conv_b2, kv_norm_g, w_k, w_v, b_norm_g, w_q, w_o, ffn_norm_g, ffn_w_gate, ffn_w_up, ffn_w_down, final_norm_g, loss_target, m_a_norm_g, m_conv_w1, m_conv_b1, m_conv_dw, m_conv_dw_b, m_conv_ln_g, m_conv_ln_b, m_conv_w2, m_conv_b2, m_kv_norm_g, m_w_k, m_w_v, m_b_norm_g, m_w_q, m_w_o, m_ffn_norm_g, m_ffn_w_gate, m_ffn_w_up, m_ffn_w_down, m_final_norm_g, v_a_norm_g, v_conv_w1, v_conv_b1, v_conv_dw, v_conv_dw_b, v_conv_ln_g, v_conv_ln_b, v_conv_w2, v_conv_b2, v_kv_norm_g, v_w_k, v_w_v, v_b_norm_g, v_w_q, v_w_o, v_ffn_norm_g, v_ffn_w_gate, v_ffn_w_up, v_ffn_w_down, v_final_norm_g):
    args = locals()
    wts = {n: args[n] for n in WEIGHT_NAMES}
    mom = {n: args["m_" + n] for n in WEIGHT_NAMES}
    vel = {n: args["v_" + n] for n in WEIGHT_NAMES}
    s, d = x.shape[-2:]
    dq = d // N_CHIPS
    x2 = x.reshape(s, d)
    tgt = loss_target.reshape(s, d)

    def pack_shard(src):
        return _pack_rows([(src[n].reshape(-1, dq), first) for n, first, _ in PACKED], PACK_ROWS, dq)

    def pack_whole(src):
        return _pack_rows([(jnp.concatenate([src[n].reshape(-1, d) for n, _, _ in WHOLE], axis=0), 0)], WHOLE_ROWS, d)

    gathers = {}
    pins = []
    ids = jnp.stack([2 * lax.axis_index("x") + lax.axis_index("y"),
                     4 * lax.axis_index("x") + 2 * lax.axis_index("y") + lax.axis_index("c")]).astype(jnp.int32)

    def start_gather(grp, pins):
        kinds = [kind for _, _, _, kind in GROUPS[grp]]
        lands = []
        for key, n, layer, kind in GROUPS[grp]:
            shard = wts[n] if layer is not None else wts[n].reshape(wts[n].shape[-2:])
            full = jax.ShapeDtypeStruct(_full_shape(shard, kind), BF16)
            lands.append(_place_own(f"gather_place_{grp}_{key}", shard, kind, full, ids, layer))
        copies = _gather_copies(kinds, grp in HALVED)
        send, recv, _, lands, pin = _copies_start("gather_start_" + grp, [], lands, copies, 3 * len(lands), pins)
        gathers[grp] = (send, recv, [], lands, copies, kinds)
        return [pin]

    (packed_full,) = _gather_weights([pack_shard(wts)], ["row"])
    w = {}
    pins = [packed_full]
    for grp in FETCH_ORDER:
        pins = start_gather(grp, pins)

    swapping = {}

    def prefetch(grp, after):
        send, recv, srcs, lands, copies, kinds = gathers[grp]
        _, fulls = _copies_wait("gather_wait_" + grp, send, recv, srcs, lands, copies, [after])
        copies = _gather_pass_copies(kinds)
        send, recv, _, fulls, _ = _copies_start("gather_swap_" + grp, [], fulls, copies, 3 * len(fulls), ())
        swapping[grp] = (send, recv, fulls, copies)

    def fetch(grp, after):
        if grp in swapping:
            send, recv, fulls, copies = swapping[grp]
            _, fulls = _copies_wait("gather_swapped_" + grp, send, recv, [], fulls, copies, [after])
        else:
            send, recv, srcs, lands, copies, kinds = gathers[grp]
            _, fulls = _copies_wait("gather_wait_" + grp, send, recv, srcs, lands, copies, [after])
            if grp in HALVED:
                fulls = _gather_pass_on("gather_pass_" + grp, fulls, kinds)
        return {key: full for (key, _, _, _), full in zip(GROUPS[grp], fulls)}

    packed = packed_full.reshape(N_CHIPS, PACK_ROWS, dq)
    for n, first, rows in PACKED:
        part = packed[:, first:first + rows, :]
        if n == "conv_dw":
            w[n] = part.transpose(1, 0, 2).reshape(rows, d)
        else:
            w[n] = part.reshape(1, N_CHIPS * rows * dq)
    for n, _, rows in WHOLE:
        w[n] = wts[n].reshape(rows, d)

    exchanges = {}
    passing = {}
    own_grads = {}

    def retire(tag, after):
        send, recv, srcs, lands, copies, kinds = exchanges[tag]
        srcs, lands = _copies_wait("grads_wait_" + tag, send, recv, srcs, lands, copies, after)
        own_grads[tag] = list(zip(srcs, kinds))
        copies = _pass_copies(len(lands))
        send, recv, _, lands, pin = _copies_start("grads_pass_" + tag, [], lands, copies, 3 * len(lands), ())
        passing[tag] = (send, recv, lands, copies)
        return pin

    def emit_grads(tag, grads, kinds):
        pins = [retire(old, [grads[0]]) for old in RETIRE_AT.get(tag, ())]
        copies = _grad_copies(kinds)
        lands = [lax.empty(_land_shape(gr, kd), gr.dtype) for gr, kd in zip(grads, kinds)]
        send, recv, srcs, lands, pin = _copies_start("grads_start_" + tag, grads, lands, copies,
                                                     GRAD_PEERS * len(grads), pins)
        exchanges[tag] = (send, recv, srcs, lands, copies, kinds)
        return [pin]

    def emit(grp, grads):
        return emit_grads(grp, [grads[key] for key, _, _, _ in GROUPS[grp]], [kind for _, _, _, kind in GROUPS[grp]])

    loss_cols, dx, g = _local_step(x2, tgt, w, fetch, emit, prefetch, after=pins)
    loss = lax.psum(jnp.sum(loss_cols), ("x", "y", "c"))

    gp = []
    for n, first, rows in PACKED:
        if n == "conv_dw":
            part = g[n].reshape(rows, N_CHIPS, dq).transpose(1, 0, 2)
        else:
            part = g[n].reshape(N_CHIPS, rows, dq)
        gp.append((part, first))
    g_packed = jnp.concatenate(
        [_pack_rows([(p[ci], first) for p, first in gp], PACK_ROWS, dq) for ci in range(N_CHIPS)], axis=0)
    emit_grads("vec", [g_packed, pack_whole(g)], ["row", "all"])
    for tag in EMIT_ORDER:
        if tag not in passing:
            retire(tag, [dx])
    contribs = {}
    for tag in EMIT_ORDER:
        send, recv, lands, copies = passing[tag]
        _, arrived = _copies_wait("grads_passed_" + tag, send, recv, [], lands, copies, [dx])
        contribs[tag] = [(c, own, kind) for c, (own, kind) in zip(arrived, own_grads[tag])]

    res = {}

    def adamw(n, contrib, layer=None, prev=None):
        arrived, own, kind = contrib
        if layer is None:
            shape = wts[n].shape
            r2 = shape[-2:]
            outs = _adamw_reduce("adamw_" + n, arrived, own, kind, ids, wts[n].reshape(r2), mom[n].reshape(r2),
                                 vel[n].reshape(r2))
            return [o.reshape(shape) for o in outs]
        return _adamw_reduce(f"adamw_{n}_{layer}", arrived, own, kind, ids, wts[n], mom[n], vel[n], layer, prev)

    for grp in ("attn", "conv2", "conv1"):
        for (key, n, _, _), contrib in zip(GROUPS[grp], contribs[grp]):
            res[n] = adamw(n, contrib)
    for part in ("d", "gu"):
        for (key, n, _, _), c0, c1 in zip(GROUPS["ffn0" + part], contribs["ffn0" + part], contribs["ffn1" + part]):
            res[n] = adamw(n, c1, 1, adamw(n, c0, 0))
    outs = _adamw_reduce("adamw_packed", *contribs["vec"][0], ids, pack_shard(wts), pack_shard(mom), pack_shard(vel))
    for n, first, rows in PACKED:
        res[n] = [o[first:first + rows].reshape(wts[n].shape) for o in outs]
    outs = _adamw_reduce("adamw_whole", *contribs["vec"][1], ids, pack_whole(wts), pack_whole(mom), pack_whole(vel))
    for n, first, rows in WHOLE:
        res[n] = [o[first:first + rows].reshape(wts[n].shape) for o in outs]

    out = [loss, dx.reshape(x.shape)]
    for which in range(4):
        out += [res[n][which] for n in WEIGHT_NAMES]
    return tuple(out)
```

```python
import functools

import jax
import jax.numpy as jnp
from jax import lax
from jax.experimental import pallas as pl
from jax.experimental.pallas import tpu as pltpu

F32 = jnp.float32
BF16 = jnp.bfloat16

HEAD_DIM = 128
BRANCHES = ((128, 1), (512, 4), (2048, 16))
CONV_WIDTH = 31
CONV_HALO = 32
RMS_EPS = 1e-6
LN_EPS = 1e-5
ADAM_LR = 0.001
ADAM_B1 = 0.9
ADAM_B2 = 0.999
ADAM_EPS = 1e-08
ADAM_WD = 0.01
ADAM_STEP = 10
N_CHIPS = 4
N_DEV = 8
MESH = pl.DeviceIdType.MESH


def _sigmoid(x):
    return 0.5 * jnp.tanh(0.5 * x) + 0.5


def _row_tile(rows, want):
    t = min(rows, want)
    assert rows % t == 0, (rows, want)
    return t


_DOT_DIMS = {"nn": ((1,), (0,)), "nt": ((1,), (1,)), "tn": ((0,), (0,))}


ANY_SPEC = pl.BlockSpec(memory_space=pl.ANY)
WIDE_BN = 1024


def _mm(name, mode, a, bs, epilogue, outs, *, m, n, k, extras=(), bm=1024, bn=512, bk=None, after=()):
    bm, bn = min(bm, m), min(bn, n)
    bk = k if bk is None else min(bk, k)
    assert m % bm == 0 and n % bn == 0 and k % bk == 0, (name, m, n, k, bm, bn, bk)
    nk = k // bk
    a_list = list(a) if isinstance(a, (list, tuple)) else [a]
    na, nb, ne, no = len(a_list), len(bs), len(extras), len(outs)
    assert na in (1, nb)

    if mode == "tn":
        a_spec = pl.BlockSpec((bk, bm), lambda i, j, kk: (kk, i))
    else:
        a_spec = pl.BlockSpec((bm, bk), lambda i, j, kk: (i, kk))

    def b_spec(lead, off):
        if mode == "nt":
            blk, idx = (bn, bk), (lambda i, j, kk: (j + off, kk))
        else:
            blk, idx = (bk, bn), (lambda i, j, kk: (kk, j + off))
        if lead is None:
            return pl.BlockSpec(blk, idx)
        return pl.BlockSpec((None,) + blk, lambda i, j, kk: (lead,) + idx(i, j, kk))

    def e_spec(kind, off):
        if kind == "mn":
            return pl.BlockSpec((bm, bn), lambda i, j, kk: (i, j + off))
        return pl.BlockSpec((1, bn), lambda i, j, kk: (0, j + off))

    nf = len(after)
    in_specs = [a_spec] * na + [b_spec(l, o) for _, l, o in bs] + [e_spec(kd, o) for _, kd, o in extras]
    in_specs += [ANY_SPEC] * nf
    out_specs = [pl.BlockSpec((bm, bn), lambda i, j, kk: (i, j)) for _ in outs]
    out_shape = [jax.ShapeDtypeStruct((m, n), dt) for (dt,) in outs]
    dims = (_DOT_DIMS[mode], ((), ()))

    def body(*refs):
        a_refs = refs[:na]
        b_refs = refs[na:na + nb]
        e_refs = refs[na + nb:na + nb + ne]
        o_refs = refs[na + nb + ne + nf:na + nb + ne + nf + no]
        acc_refs = refs[na + nb + ne + nf + no:]
        avs = [a_ref[...].astype(BF16) for a_ref in a_refs]
        prods = [lax.dot_general(avs[bi % na], b_ref[...].astype(BF16), dims, preferred_element_type=F32)
                 for bi, b_ref in enumerate(b_refs)]

        def finish(accs):
            res = epilogue(accs, [e_ref[...] for e_ref in e_refs])
            for o_ref, r in zip(o_refs, res):
                o_ref[...] = r.astype(o_ref.dtype)

        if nk == 1:
            finish(prods)
        else:
            kk = pl.program_id(2)

            @pl.when(kk == 0)
            def _():
                for acc_ref, p in zip(acc_refs, prods):
                    acc_ref[...] = p

            @pl.when(kk > 0)
            def _():
                for acc_ref, p in zip(acc_refs, prods):
                    acc_ref[...] += p

            @pl.when(kk == nk - 1)
            def _():
                finish([acc_ref[...] for acc_ref in acc_refs])

    scratch = [] if nk == 1 else [pltpu.VMEM((bm, bn), F32) for _ in bs]
    res = pl.pallas_call(
        body,
        name=name,
        grid=(m // bm, n // bn, nk),
        in_specs=in_specs,
        out_specs=out_specs,
        out_shape=out_shape,
        scratch_shapes=scratch,
        compiler_params=pltpu.CompilerParams(dimension_semantics=("parallel", "parallel", "arbitrary")),
    )(*a_list, *[b for b, _, _ in bs], *[e for e, _, _ in extras], *after)
    return res


def _rms_fwd(name, x, gains, after=()):
    s, d = x.shape
    ng = gains.shape[0]
    t = _row_tile(s, 256)
    nf = len(after)

    def body(x_ref, g_ref, *refs):
        o_refs = refs[nf:]
        xv = x_ref[...]
        r = lax.rsqrt(jnp.mean(xv * xv, axis=-1, keepdims=True) + RMS_EPS)
        xh = xv * r
        for gi, o_ref in enumerate(o_refs):
            o_ref[...] = (xh * g_ref[gi:gi + 1, :]).astype(o_ref.dtype)

    return pl.pallas_call(
        body,
        name=name,
        grid=(s // t,),
        in_specs=[pl.BlockSpec((t, d), lambda i: (i, 0)), pl.BlockSpec((ng, d), lambda i: (0, 0))] + [ANY_SPEC] * nf,
        out_specs=[pl.BlockSpec((t, d), lambda i: (i, 0)) for _ in range(ng)],
        out_shape=[jax.ShapeDtypeStruct((s, d), BF16) for _ in range(ng)],
        compiler_params=pltpu.CompilerParams(dimension_semantics=("parallel",)),
    )(x, gains, *after)


def _rms_bwd(name, x, gains, dns, dres):
    s, d = x.shape
    ng = gains.shape[0]
    t = _row_tile(s, 256)

    def body(x_ref, g_ref, dres_ref, *refs):
        dn_refs = refs[:ng]
        dx_ref, dxb_ref, dg_ref, cs_ref = refs[ng:]
        i = pl.program_id(0)
        xv = x_ref[...]
        r = lax.rsqrt(jnp.mean(xv * xv, axis=-1, keepdims=True) + RMS_EPS)
        xh = xv * r
        dx = dres_ref[...]
        dgs = []
        for gi in range(ng):
            dn = dn_refs[gi][...].astype(F32)
            dxh = dn * g_ref[gi:gi + 1, :]
            dgs.append(jnp.sum(dn * xh, axis=0, keepdims=True))
            dx = dx + r * (dxh - xh * jnp.mean(dxh * xh, axis=-1, keepdims=True))
        dx_ref[...] = dx
        dxb_ref[...] = dx.astype(BF16)
        dg = jnp.concatenate(dgs, axis=0) if ng > 1 else dgs[0]
        cs = jnp.sum(dx, axis=0, keepdims=True)

        @pl.when(i == 0)
        def _():
            dg_ref[...] = dg
            cs_ref[...] = cs

        @pl.when(i > 0)
        def _():
            dg_ref[...] += dg
            cs_ref[...] += cs

    row = pl.BlockSpec((t, d), lambda i: (i, 0))
    return pl.pallas_call(
        body,
        name=name,
        grid=(s // t,),
        in_specs=[row, pl.BlockSpec((ng, d), lambda i: (0, 0)), row] + [row] * ng,
        out_specs=[row, row, pl.BlockSpec((ng, d), lambda i: (0, 0)), pl.BlockSpec((1, d), lambda i: (0, 0))],
        out_shape=[
            jax.ShapeDtypeStruct((s, d), F32),
            jax.ShapeDtypeStruct((s, d), BF16),
            jax.ShapeDtypeStruct((ng, d), F32),
            jax.ShapeDtypeStruct((1, d), F32),
        ],
        compiler_params=pltpu.CompilerParams(dimension_semantics=("arbitrary",)),
    )(x, gains, dres, *dns)


def _final_loss(name, h, gain, target):
    s, d = h.shape
    t = _row_tile(s, 256)

    def body(h_ref, g_ref, t_ref, dh_ref, dhb_ref, dg_ref, ls_ref):
        i = pl.program_id(0)
        xv = h_ref[...]
        g = g_ref[...]
        r = lax.rsqrt(jnp.mean(xv * xv, axis=-1, keepdims=True) + RMS_EPS)
        xh = xv * r
        err = xh * g - t_ref[...]
        ls = jnp.sum(err * err, axis=0, keepdims=True) * (0.5 / d)
        dy = err * (1.0 / d)
        dxh = dy * g
        dg = jnp.sum(dy * xh, axis=0, keepdims=True)
        dx = r * (dxh - xh * jnp.mean(dxh * xh, axis=-1, keepdims=True))
        dh_ref[...] = dx
        dhb_ref[...] = dx.astype(BF16)

        @pl.when(i == 0)
        def _():
            dg_ref[...] = dg
            ls_ref[...] = ls

        @pl.when(i > 0)
        def _():
            dg_ref[...] += dg
            ls_ref[...] += ls

    row = pl.BlockSpec((t, d), lambda i: (i, 0))
    vec = pl.BlockSpec((1, d), lambda i: (0, 0))
    return pl.pallas_call(
        body,
        name=name,
        grid=(s // t,),
        in_specs=[row, vec, row],
        out_specs=[row, row, vec, vec],
        out_shape=[
            jax.ShapeDtypeStruct((s, d), F32),
            jax.ShapeDtypeStruct((s, d), BF16),
            jax.ShapeDtypeStruct((1, d), F32),
            jax.ShapeDtypeStruct((1, d), F32),
        ],
        compiler_params=pltpu.CompilerParams(dimension_semantics=("arbitrary",)),
    )(h, gain, target)


SUBLANES = 8
CONV_LANES = 512
CONV_ROWS = 32
NORM_ROWS = 64


def _conv_tiles(s):
    t = _row_tile(s, 128)
    assert t % CONV_HALO == 0 and t % CONV_ROWS == 0
    return t, t // CONV_HALO


def _shifted_copies(dst, src, lanes, rows):
    for m in range(SUBLANES):
        n = rows if m == 0 else rows - SUBLANES
        dst[m, :n, :] = src[m:m + n, lanes]


def _shifted(copies, offset, n):
    m = offset % SUBLANES
    return copies[m, offset - m:offset - m + n, :]


def _dwconv_fwd(name, u, dw, dw_b, ln_g, ln_b):
    s, d = u.shape
    t, hb = _conv_tiles(s)
    w = dw.shape[0]
    lo = CONV_HALO - (w - 1)

    lw = min(CONV_LANES, d)

    def body(cur_ref, prev_ref, dw_ref, dwb_ref, lg_ref, lb_ref, c_ref, sw_ref, cat_ref, sh_ref):
        i = pl.program_id(0)
        cat_ref[CONV_HALO:, :] = cur_ref[...]

        @pl.when(i == 0)
        def _():
            cat_ref[:CONV_HALO, :] = jnp.zeros((CONV_HALO, d), F32)

        @pl.when(i > 0)
        def _():
            cat_ref[:CONV_HALO, :] = prev_ref[...]

        for lc in range(d // lw):
            lanes = slice(lc * lw, (lc + 1) * lw)
            _shifted_copies(sh_ref, cat_ref, lanes, t + CONV_HALO)
            for rc in range(t // CONV_ROWS):
                acc = jnp.broadcast_to(dwb_ref[:, lanes], (CONV_ROWS, lw))
                for kk in range(w):
                    acc = acc + dw_ref[kk:kk + 1, lanes] * _shifted(sh_ref, lo + kk + rc * CONV_ROWS, CONV_ROWS)
                c_ref[rc * CONV_ROWS:(rc + 1) * CONV_ROWS, lanes] = acc

        def norm_rows(ri, carry):
            rows = pl.ds(pl.multiple_of(ri * NORM_ROWS, NORM_ROWS), NORM_ROWS)
            cv = c_ref[rows, :]
            cc = cv - jnp.mean(cv, axis=-1, keepdims=True)
            var = jnp.mean(cc * cc, axis=-1, keepdims=True)
            ln = cc * lax.rsqrt(var + LN_EPS) * lg_ref[...] + lb_ref[...]
            sw_ref[rows, :] = (ln * _sigmoid(ln)).astype(BF16)
            return carry

        lax.fori_loop(0, t // NORM_ROWS, norm_rows, 0)

    row = pl.BlockSpec((t, d), lambda i: (i, 0))
    prev = pl.BlockSpec((CONV_HALO, d), lambda i: (jnp.maximum(i * hb - 1, 0), 0))
    vec = pl.BlockSpec((1, d), lambda i: (0, 0))
    return pl.pallas_call(
        body,
        name=name,
        grid=(s // t,),
        in_specs=[row, prev, pl.BlockSpec((w, d), lambda i: (0, 0)), vec, vec, vec],
        out_specs=[row, row],
        out_shape=[jax.ShapeDtypeStruct((s, d), F32), jax.ShapeDtypeStruct((s, d), BF16)],
        scratch_shapes=[pltpu.VMEM((CONV_HALO + t, d), F32), pltpu.VMEM((SUBLANES, CONV_HALO + t, lw), F32)],
        compiler_params=pltpu.CompilerParams(dimension_semantics=("parallel",)),
    )(u, u, dw, dw_b, ln_g, ln_b)


def _conv_ln_bwd(name, c, dsw, ln_g, ln_b, after=()):
    s, d = c.shape
    t = _row_tile(s, 256)
    nf = len(after)

    def body(c_ref, dsw_ref, lg_ref, lb_ref, *refs):
        dc_ref, sums_ref = refs[nf:]
        i = pl.program_id(0)
        cv = c_ref[...]
        g = lg_ref[...]
        mu = jnp.mean(cv, axis=-1, keepdims=True)
        cc = cv - mu
        rstd = lax.rsqrt(jnp.mean(cc * cc, axis=-1, keepdims=True) + LN_EPS)
        ch = cc * rstd
        ln = ch * g + lb_ref[...]
        sg = _sigmoid(ln)
        dln = dsw_ref[...] * (sg * (1.0 + ln * (1.0 - sg)))
        dch = dln * g
        dc = rstd * (dch - jnp.mean(dch, axis=-1, keepdims=True) - ch * jnp.mean(dch * ch, axis=-1, keepdims=True))
        dc_ref[...] = dc
        sums = jnp.concatenate(
            [
                jnp.sum(dln * ch, axis=0, keepdims=True),
                jnp.sum(dln, axis=0, keepdims=True),
                jnp.sum(dc, axis=0, keepdims=True),
                jnp.zeros((1, d), F32),
            ],
            axis=0,
        )

        @pl.when(i == 0)
        def _():
            sums_ref[...] = sums

        @pl.when(i > 0)
        def _():
            sums_ref[...] += sums

    row = pl.BlockSpec((t, d), lambda i: (i, 0))
    vec = pl.BlockSpec((1, d), lambda i: (0, 0))
    return pl.pallas_call(
        body,
        name=name,
        grid=(s // t,),
        in_specs=[row, row, vec, vec] + [ANY_SPEC] * nf,
        out_specs=[row, pl.BlockSpec((4, d), lambda i: (0, 0))],
        out_shape=[jax.ShapeDtypeStruct((s, d), F32), jax.ShapeDtypeStruct((4, d), F32)],
        compiler_params=pltpu.CompilerParams(dimension_semantics=("arbitrary",)),
    )(c, dsw, ln_g, ln_b, *after)


def _conv_dw_bwd(name, dc, u, a, gt, dw):
    s, d = dc.shape
    t, hb = _conv_tiles(s)
    w = dw.shape[0]
    lo = CONV_HALO - (w - 1)
    nt = s // t
    lw = min(CONV_LANES, d)

    def body(dc_ref, dcn_ref, u_ref, up_ref, a_ref, gt_ref, dw_ref, dpre_ref, ddw_ref, db_ref, dcat_ref, ucat_ref,
             dsh_ref, ush_ref, ddw_acc, db_acc):
        i = pl.program_id(0)
        dcat_ref[:t, :] = dc_ref[...]
        ucat_ref[CONV_HALO:, :] = u_ref[...]

        @pl.when(i == nt - 1)
        def _():
            dcat_ref[t:, :] = jnp.zeros((CONV_HALO, d), F32)

        @pl.when(i < nt - 1)
        def _():
            dcat_ref[t:, :] = dcn_ref[...]

        @pl.when(i == 0)
        def _():
            ucat_ref[:CONV_HALO, :] = jnp.zeros((CONV_HALO, d), F32)

        @pl.when(i > 0)
        def _():
            ucat_ref[:CONV_HALO, :] = up_ref[...]

        @pl.when(i == 0)
        def _():
            ddw_acc[...] = jnp.zeros(ddw_acc.shape, F32)
            db_acc[...] = jnp.zeros(db_acc.shape, F32)

        def fold(p):
            return functools.reduce(jnp.add, [p[j:j + SUBLANES] for j in range(0, CONV_ROWS, SUBLANES)])

        for lc in range(d // lw):
            lanes = slice(lc * lw, (lc + 1) * lw)
            gate_lanes = slice(d + lc * lw, d + (lc + 1) * lw)
            _shifted_copies(dsh_ref, dcat_ref, lanes, t + CONV_HALO)
            _shifted_copies(ush_ref, ucat_ref, lanes, t + CONV_HALO)
            for rc in range(t // CONV_ROWS):
                r0 = rc * CONV_ROWS
                rows = slice(r0, r0 + CONV_ROWS)
                dcv = dc_ref[rows, lanes]
                du = jnp.zeros((CONV_ROWS, lw), F32)
                for kk in range(w):
                    du = du + dw_ref[kk:kk + 1, lanes] * _shifted(dsh_ref, w - 1 - kk + r0, CONV_ROWS)
                    ddw_acc[kk, :, lanes] += fold(dcv * _shifted(ush_ref, lo + kk + r0, CONV_ROWS))
                av = a_ref[rows, lanes].astype(F32)
                sg = _sigmoid(gt_ref[rows, lanes].astype(F32))
                da = du * sg
                dgt = du * av * sg * (1.0 - sg)
                dpre_ref[rows, lanes] = da.astype(BF16)
                dpre_ref[rows, gate_lanes] = dgt.astype(BF16)
                db_acc[:, lanes] += fold(da)
                db_acc[:, gate_lanes] += fold(dgt)

        @pl.when(i == nt - 1)
        def _():
            ddw_ref[...] = jnp.sum(ddw_acc[...], axis=1)
            db_ref[...] = jnp.sum(db_acc[...], axis=0, keepdims=True)

    row = pl.BlockSpec((t, d), lambda i: (i, 0))
    nxt = pl.BlockSpec((CONV_HALO, d), lambda i: (jnp.minimum((i + 1) * hb, s // CONV_HALO - 1), 0))
    prev = pl.BlockSpec((CONV_HALO, d), lambda i: (jnp.maximum(i * hb - 1, 0), 0))
    return pl.pallas_call(
        body,
        name=name,
        grid=(nt,),
        in_specs=[row, nxt, row, prev, row, row, pl.BlockSpec((w, d), lambda i: (0, 0))],
        out_specs=[
            pl.BlockSpec((t, 2 * d), lambda i: (i, 0)),
            pl.BlockSpec((w, d), lambda i: (0, 0)),
            pl.BlockSpec((1, 2 * d), lambda i: (0, 0)),
        ],
        out_shape=[
            jax.ShapeDtypeStruct((s, 2 * d), BF16),
            jax.ShapeDtypeStruct((w, d), F32),
            jax.ShapeDtypeStruct((1, 2 * d), F32),
        ],
        scratch_shapes=[pltpu.VMEM((t + CONV_HALO, d), F32), pltpu.VMEM((CONV_HALO + t, d), F32)]
        + [pltpu.VMEM((SUBLANES, CONV_HALO + t, lw), F32)] * 2
        + [pltpu.VMEM((w, SUBLANES, d), F32), pltpu.VMEM((SUBLANES, 2 * d), F32)],
        compiler_params=pltpu.CompilerParams(dimension_semantics=("arbitrary",)),
    )(dc, dc, u, u, a, gt, dw)


def _alibi_slopes(n_heads):
    h = jnp.arange(1, n_heads + 1, dtype=F32)
    return jnp.exp2(-8.0 * h / n_heads)


def _band_masks(bq):
    qi = lax.broadcasted_iota(jnp.int32, (bq, bq), 0)
    kj = lax.broadcasted_iota(jnp.int32, (bq, bq), 1)
    return qi - kj, qi - kj + bq


ATTN_GROUP = 16


def _attn_fwd(name, q, k, v, slopes, bq):
    s, dm = q.shape
    nh = dm // HEAD_DIM
    nt = s // bq
    nbr = len(BRANCHES)
    scale = HEAD_DIM ** -0.5
    nt_dims = (((1,), (1,)), ((), ()))

    def body(sl_ref, q_ref, k_ref, v_ref, o_ref, ob_ref, l_ref, tmp, qr, kr, va, orm, lrm, onat, lnat, sbuf, mbuf):
        slope = sl_ref[pl.program_id(0)]
        jc, jp = _band_masks(bq)
        va[:, HEAD_DIM:] = jnp.ones((s, HEAD_DIM), BF16)
        for bi, (win, dil) in enumerate(BRANCHES):
            ll = s // dil
            nblk = ll // bq
            if dil == 1:
                sq, sk = q_ref, k_ref
                va[:, :HEAD_DIM] = v_ref[...]
                d_o, d_l = onat.at[bi], lnat.at[bi]
            else:
                for src, dst, wide in ((q_ref, qr, False), (k_ref, kr, False), (v_ref, va, True)):
                    tmp[...] = src[...].astype(F32)
                    for r in range(dil):
                        part = tmp[pl.ds(r, ll, stride=dil), :].astype(BF16)
                        if wide:
                            dst[r * ll:(r + 1) * ll, :HEAD_DIM] = part
                        else:
                            dst[r * ll:(r + 1) * ll, :] = part
                sq, sk = qr, kr
                d_o, d_l = orm, lrm
            bias_c = jnp.where(jc >= 0, jc.astype(F32) * (slope * dil), 1e30)
            bias_p = jnp.where(jp <= bq, jp.astype(F32) * (slope * dil), 1e30)

            def group(gi, carry):
                rows = []
                for g in range(ATTN_GROUP):
                    ti = gi * ATTN_GROUP + g
                    row = pl.ds(pl.multiple_of(ti * bq, bq), bq)
                    prow = pl.ds(pl.multiple_of(jnp.maximum(ti - 1, 0) * bq, bq), bq)
                    rows.append((row, prow))
                    qh = sq[row, :]
                    sc = lax.dot_general(qh, sk[row, :], nt_dims, preferred_element_type=F32) * scale - bias_c
                    sp = lax.dot_general(qh, sk[prow, :], nt_dims, preferred_element_type=F32) * scale - bias_p
                    sp = jnp.where(lax.rem(ti, nblk) > 0, sp, -1e30)
                    sbuf[g, :, :bq] = sc
                    sbuf[g, :, bq:] = sp
                    mbuf[g] = jnp.maximum(jnp.max(sc, axis=-1, keepdims=True), jnp.max(sp, axis=-1, keepdims=True))
                for g, (row, prow) in enumerate(rows):
                    mx = mbuf[g]
                    p = jnp.exp(sbuf[g] - mx).astype(BF16)
                    ov = jnp.dot(p[:, :bq], va[row, :], preferred_element_type=F32)
                    ov = ov + jnp.dot(p[:, bq:], va[prow, :], preferred_element_type=F32)
                    den = ov[:, HEAD_DIM:]
                    d_o[row, :] = ov[:, :HEAD_DIM] / den
                    d_l[row, :] = mx + jnp.log(den)
                return carry

            lax.fori_loop(0, nt // ATTN_GROUP, group, 0)
            if dil > 1:
                for r in range(dil):
                    onat[bi, pl.ds(r, ll, stride=dil), :] = orm[r * ll:(r + 1) * ll, :]
                    lnat[bi, pl.ds(r, ll, stride=dil), :] = lrm[r * ll:(r + 1) * ll, :]

        def merge(ti, carry):
            rows = pl.ds(pl.multiple_of(ti * bq, bq), bq)
            ls = [lnat[bi, rows, :] for bi in range(nbr)]
            mx = functools.reduce(jnp.maximum, ls)
            es = [jnp.exp(l - mx) for l in ls]
            tot = functools.reduce(jnp.add, es)
            inv = 1.0 / tot
            o = functools.reduce(jnp.add, [e * inv * onat[bi, rows, :] for bi, e in enumerate(es)])
            o_ref[rows, :] = o
            ob_ref[rows, :] = o.astype(BF16)
            l_ref[rows, :] = mx + jnp.log(tot)
            return carry

        lax.fori_loop(0, nt, merge, 0)

    head = pl.BlockSpec((s, HEAD_DIM), lambda h: (0, h))
    return pl.pallas_call(
        body,
        name=name,
        grid=(nh,),
        in_specs=[pl.BlockSpec(memory_space=pltpu.SMEM), head, head, head],
        out_specs=[head, head, head],
        out_shape=[jax.ShapeDtypeStruct((s, dm), F32), jax.ShapeDtypeStruct((s, dm), BF16),
                   jax.ShapeDtypeStruct((s, dm), F32)],
        scratch_shapes=[pltpu.VMEM((s, HEAD_DIM), F32)] + [pltpu.VMEM((s, HEAD_DIM), BF16)] * 2
        + [pltpu.VMEM((s, 2 * HEAD_DIM), BF16)] + [pltpu.VMEM((s, HEAD_DIM), F32)] * 2
        + [pltpu.VMEM((nbr, s, HEAD_DIM), F32)] * 2
        + [pltpu.VMEM((ATTN_GROUP, bq, 2 * bq), F32), pltpu.VMEM((ATTN_GROUP, bq, 1), F32)],
        compiler_params=pltpu.CompilerParams(dimension_semantics=("parallel",)),
    )(slopes, q, k, v)


def _attn_bwd(name, q, k, v, o, lse, do, slopes, bq):
    s, dm = q.shape
    nh = dm // HEAD_DIM
    nt = s // bq
    scale = HEAD_DIM ** -0.5
    nt_dims = (((1,), (1,)), ((), ()))
    tn_dims = (((0,), (0,)), ((), ()))

    def body(sl_ref, q_ref, k_ref, v_ref, o_ref, l_ref, do_ref, dq_ref, dk_ref, dv_ref,
             tmp, qr, kr, vr, dor, lr, dlr, dln, dqr, dkr, dvr, aq, ak, av, pbuf, dsbuf):
        slope = sl_ref[pl.program_id(0)]
        jc, jp = _band_masks(bq)

        def delta(ti, carry):
            rows = pl.ds(pl.multiple_of(ti * bq, bq), bq)
            dl = jnp.sum(do_ref[rows, :].astype(F32) * o_ref[rows, :], axis=-1, keepdims=True)
            dln[rows, :] = jnp.broadcast_to(dl, (bq, HEAD_DIM))
            return carry

        lax.fori_loop(0, nt, delta, 0)

        for bi, (win, dil) in enumerate(BRANCHES):
            ll = s // dil
            nblk = ll // bq
            if dil == 1:
                sq, sk, sv, sdo, sl, sdl = q_ref, k_ref, v_ref, do_ref, l_ref, dln
                gq, gk, gv = aq, ak, av
            else:
                for src, dst in ((q_ref, qr), (k_ref, kr), (v_ref, vr), (do_ref, dor)):
                    tmp[...] = src[...].astype(F32)
                    for r in range(dil):
                        dst[r * ll:(r + 1) * ll, :] = tmp[pl.ds(r, ll, stride=dil), :].astype(BF16)
                for r in range(dil):
                    lr[r * ll:(r + 1) * ll, :] = l_ref[pl.ds(r, ll, stride=dil), :]
                    dlr[r * ll:(r + 1) * ll, :] = dln[pl.ds(r, ll, stride=dil), :]
                sq, sk, sv, sdo, sl, sdl = qr, kr, vr, dor, lr, dlr
                gq, gk, gv = dqr, dkr, dvr
            bias_c = jnp.where(jc >= 0, jc.astype(F32) * (slope * dil), 1e30)
            bias_p = jnp.where(jp <= bq, jp.astype(F32) * (slope * dil), 1e30)

            def group(gi, carry):
                rows = []
                for g in range(ATTN_GROUP):
                    ti = gi * ATTN_GROUP + g
                    row = pl.ds(pl.multiple_of(ti * bq, bq), bq)
                    prow = pl.ds(pl.multiple_of(jnp.maximum(ti - 1, 0) * bq, bq), bq)
                    rows.append((row, prow))
                    has_prev = lax.rem(ti, nblk) > 0
                    qh, doh = sq[row, :], sdo[row, :]
                    lc = sl[row, :][:, :1]
                    dl = sdl[row, :][:, :1]
                    for half, kv_rows, bias in ((0, row, bias_c), (1, prow, bias_p)):
                        sc = lax.dot_general(qh, sk[kv_rows, :], nt_dims, preferred_element_type=F32) * scale - bias
                        if half:
                            sc = jnp.where(has_prev, sc, -1e30)
                        p = jnp.exp(sc - lc)
                        dp = lax.dot_general(doh, sv[kv_rows, :], nt_dims, preferred_element_type=F32)
                        pbuf[g, :, half * bq:(half + 1) * bq] = p.astype(BF16)
                        dsbuf[g, :, half * bq:(half + 1) * bq] = (p * (dp - dl) * scale).astype(BF16)
                carry_k = carry_v = None
                for g, (row, prow) in enumerate(rows):
                    qh, doh = sq[row, :], sdo[row, :]
                    ds_c, ds_p = dsbuf[g, :, :bq], dsbuf[g, :, bq:]
                    p_c, p_p = pbuf[g, :, :bq], pbuf[g, :, bq:]
                    dq = jnp.dot(ds_c, sk[row, :], preferred_element_type=F32)
                    dq = dq + jnp.dot(ds_p, sk[prow, :], preferred_element_type=F32)
                    gq[row, :] = dq
                    dk_p = lax.dot_general(ds_p, qh, tn_dims, preferred_element_type=F32)
                    dv_p = lax.dot_general(p_p, doh, tn_dims, preferred_element_type=F32)
                    if g == 0:
                        @pl.when(gi > 0)
                        def _():
                            gk[prow, :] += dk_p
                            gv[prow, :] += dv_p
                    else:
                        gk[rows[g - 1][0], :] = carry_k + dk_p
                        gv[rows[g - 1][0], :] = carry_v + dv_p
                    carry_k = lax.dot_general(ds_c, qh, tn_dims, preferred_element_type=F32)
                    carry_v = lax.dot_general(p_c, doh, tn_dims, preferred_element_type=F32)
                gk[rows[-1][0], :] = carry_k
                gv[rows[-1][0], :] = carry_v
                return carry

            lax.fori_loop(0, nt // ATTN_GROUP, group, 0)
            if dil > 1:
                for acc, rm in ((aq, dqr), (ak, dkr), (av, dvr)):
                    for r in range(dil):
                        acc[pl.ds(r, ll, stride=dil), :] += rm[r * ll:(r + 1) * ll, :]

        dq_ref[...] = aq[...].astype(BF16)
        dk_ref[...] = ak[...].astype(BF16)
        dv_ref[...] = av[...].astype(BF16)

    head = pl.BlockSpec((s, HEAD_DIM), lambda h: (0, h))
    f32buf = pltpu.VMEM((s, HEAD_DIM), F32)
    b16buf = pltpu.VMEM((s, HEAD_DIM), BF16)
    return pl.pallas_call(
        body,
        name=name,
        grid=(nh,),
        in_specs=[pl.BlockSpec(memory_space=pltpu.SMEM)] + [head] * 6,
        out_specs=[head] * 3,
        out_shape=[jax.ShapeDtypeStruct((s, dm), BF16)] * 3,
        scratch_shapes=[f32buf] + [b16buf] * 4 + [f32buf] * 9 + [pltpu.VMEM((ATTN_GROUP, bq, 2 * bq), BF16)] * 2,
        compiler_params=pltpu.CompilerParams(dimension_semantics=("parallel",)),
    )(slopes, q, k, v, o, lse, do)


def _ep_id(accs, ex):
    return [accs[0]]


def _ep_all(accs, ex):
    return list(accs)


def _ep_sum(accs, ex):
    return [accs[0] + accs[1]]


def _ep_add(accs, ex):
    return [accs[0] + ex[0].astype(F32)]


def _ep_bias_res(accs, ex):
    return [accs[0] + ex[0] + ex[1]]


def _ep_glu(accs, ex):
    a = accs[0] + ex[0]
    gt = accs[1] + ex[1]
    return [a * _sigmoid(gt), a, gt]


def _ep_swiglu(accs, ex):
    g, u = accs
    return [g, u, g * _sigmoid(g) * u]


def _ep_swiglu_bwd(accs, ex):
    dact = accs[0]
    g = ex[0].astype(F32)
    u = ex[1].astype(F32)
    sg = _sigmoid(g)
    return [dact * u * (sg * (1.0 + g * (1.0 - sg))), dact * g * sg]


def _ffn_fwd(tag, h, gain, get_gate_up, get_down):
    s, d = h.shape
    (n,) = _rms_fwd(f"{tag}_norm", h, gain)
    wg, wu = get_gate_up(n)
    f = wg.shape[-1]
    g, u, act = _mm(f"{tag}_gate_up", "nn", n, [(wg, None, 0), (wu, None, 0)], _ep_swiglu,
                    [(BF16,), (BF16,), (BF16,)], m=s, n=f, k=d)
    wd = get_down(act)
    (out,) = _mm(f"{tag}_down", "nn", act, [(wd, None, 0)], _ep_add, [(F32,)], m=s, n=d, k=f,
                 extras=[(h, "mn", 0)])
    return out, (n, g, u, act), dict(gate=wg, up=wu, down=wd)


def _ffn_bwd(tag, h_in, gain, wg, wu, wd, saved, dh, dhb, after, emit):
    s, d = h_in.shape
    f = wg.shape[-1]
    n, g, u, act = saved
    dg, du = _mm(f"{tag}_bwd_dact", "nt", dhb, [(wd, None, 0)], _ep_swiglu_bwd, [(BF16,), (BF16,)],
                 m=s, n=f, k=d, extras=[(g, "mn", 0), (u, "mn", 0)], after=after)
    (dwd,) = _mm(f"{tag}_bwd_dwd", "tn", act, [(dhb, None, 0)], _ep_id, [(BF16,)], m=f, n=d, k=s,
                 bm=f // 4)
    pin = emit("d", dict(down=dwd))
    dwg, dwu = _mm(f"{tag}_bwd_dwgu", "tn", n, [(dg, None, 0), (du, None, 0)], _ep_all, [(BF16,), (BF16,)],
                   m=d, n=f, k=s, after=pin)
    pin = emit("gu", dict(gate=dwg, up=dwu))
    (dn,) = _mm(f"{tag}_bwd_dn", "nt", [dg, du], [(wg, None, 0), (wu, None, 0)], _ep_sum, [(BF16,)],
                m=s, n=d, k=f, bm=512, bn=256, after=pin)
    dx, dxb, dgain, cs = _rms_bwd(f"{tag}_bwd_norm", h_in, gain, [dn], dh)
    return dx, dxb, dgain, cs


def _local_step(x, target, w, fetch, emit, prefetch=lambda group, after: None, after=()):
    s, d = x.shape
    nh = d // HEAD_DIM
    bq = BRANCHES[0][0] // BRANCHES[0][1]
    assert all(win // dil == bq for win, dil in BRANCHES)
    assert BRANCHES[0][1] == 1 and all(dil > 1 for _, dil in BRANCHES[1:])
    slopes = _alibi_slopes(nh)
    nd = d // 512 if d >= 512 else 1
    bn = d // nd

    (n1,) = _rms_fwd("a_norm", x, w["a_norm_g"], after=after)
    prefetch("rest", n1)
    w_conv1 = fetch("conv1", n1)["conv_w1"]
    glu, a, gt = _mm("conv_pw1_glu", "nn", n1, [(w_conv1, None, 0), (w_conv1, None, nd)], _ep_glu,
                     [(F32,), (BF16,), (BF16,)], m=s, n=d, k=d, bn=bn,
                     extras=[(w["conv_b1"], "n", 0), (w["conv_b1"], "n", nd)])
    c, sw = _dwconv_fwd("conv_dw_ln", glu, w["conv_dw"], w["conv_dw_b"], w["conv_ln_g"], w["conv_ln_b"])
    w_conv2 = fetch("conv2", sw)["conv_w2"]
    (h1,) = _mm("conv_pw2", "nn", sw, [(w_conv2, None, 0)], _ep_bias_res, [(F32,)], m=s, n=d, k=d,
                extras=[(w["conv_b2"], "n", 0), (x, "mn", 0)], bn=WIDE_BN)
    def gate_up0(after):
        wts = fetch("ffn0gu", after)
        return wts["gate"], wts["up"]

    h2, ffn0, wf0 = _ffn_fwd("ffn0", h1, w["ffn_norm_g"][0:1], gate_up0, lambda after: fetch("ffn0d", after)["down"])
    wa = fetch("attn", h2)
    kvn, qn = _rms_fwd("kvq_norm", h2, jnp.concatenate([w["kv_norm_g"], w["b_norm_g"]], axis=0))
    k, v = _mm("kv_proj", "nn", kvn, [(wa["w_k"], None, 0), (wa["w_v"], None, 0)], _ep_all, [(BF16,), (BF16,)],
               m=s, n=d, k=d, bn=WIDE_BN)
    (q,) = _mm("q_proj", "nn", qn, [(wa["w_q"], None, 0)], _ep_id, [(BF16,)], m=s, n=d, k=d, bn=WIDE_BN)
    att, attb, lse = _attn_fwd("attn_fwd", q, k, v, slopes, bq)
    prefetch("ffn1", attb)
    (h3,) = _mm("o_proj", "nn", attb, [(wa["w_o"], None, 0)], _ep_add, [(F32,)], m=s, n=d, k=d,
                extras=[(h2, "mn", 0)], bn=WIDE_BN)
    got1 = {}

    def gate_up1(after):
        got1.update(fetch("ffn1", after))
        return got1["gate"], got1["up"]

    h4, ffn1, wf1 = _ffn_fwd("ffn1", h3, w["ffn_norm_g"][1:2], gate_up1, lambda after: got1["down"])
    dh4, dh4b, d_final_g, loss_cols = _final_loss("final_loss", h4, w["final_norm_g"], target)

    g = {}
    ga = {}
    dh3, dh3b, dgain1, _ = _ffn_bwd("ffn1", h3, w["ffn_norm_g"][1:2], wf1["gate"], wf1["up"], wf1["down"], ffn1,
                                    dh4, dh4b, (), lambda part, grads: emit("ffn1" + part, grads))
    (datt,) = _mm("o_proj_bwd_dx", "nt", dh3b, [(wa["w_o"], None, 0)], _ep_id, [(BF16,)], m=s, n=d, k=d, bn=WIDE_BN)
    (ga["w_o"],) = _mm("o_proj_bwd_dw", "tn", attb, [(dh3b, None, 0)], _ep_id, [(BF16,)], m=d, n=d, k=s, bn=WIDE_BN)
    dq, dk, dv = _attn_bwd("attn_bwd", q, k, v, att, lse, datt, slopes, bq)
    (ga["w_q"],) = _mm("q_proj_bwd_dw", "tn", qn, [(dq, None, 0)], _ep_id, [(BF16,)], m=d, n=d, k=s, bn=WIDE_BN)
    ga["w_k"], ga["w_v"] = _mm("kv_proj_bwd_dw", "tn", kvn, [(dk, None, 0), (dv, None, 0)], _ep_all,
                               [(BF16,), (BF16,)], m=d, n=d, k=s)
    pin = emit("attn", ga)
    (dqn,) = _mm("q_proj_bwd_dx", "nt", dq, [(wa["w_q"], None, 0)], _ep_id, [(BF16,)], m=s, n=d, k=d, after=pin,
                 bn=WIDE_BN)
    (dkvn,) = _mm("kv_proj_bwd_dx", "nt", [dk, dv], [(wa["w_k"], None, 0), (wa["w_v"], None, 0)], _ep_sum, [(BF16,)],
                  m=s, n=d, k=d, bn=WIDE_BN)
    dh2, dh2b, dg_kvq, _ = _rms_bwd("kvq_norm_bwd", h2, jnp.concatenate([w["kv_norm_g"], w["b_norm_g"]], axis=0),
                                    [dkvn, dqn], dh3)
    dh1, dh1b, dgain0, cs_h1 = _ffn_bwd("ffn0", h1, w["ffn_norm_g"][0:1], wf0["gate"], wf0["up"], wf0["down"], ffn0,
                                        dh2, dh2b, (), lambda part, grads: emit("ffn0" + part, grads))
    (dsw,) = _mm("conv_pw2_bwd_dx", "nt", dh1b, [(w_conv2, None, 0)], _ep_id, [(F32,)], m=s, n=d, k=d, bn=WIDE_BN)
    (dw2,) = _mm("conv_pw2_bwd_dw", "tn", sw, [(dh1b, None, 0)], _ep_id, [(BF16,)], m=d, n=d, k=s, bn=WIDE_BN)
    pin = emit("conv2", dict(conv_w2=dw2))
    dc, ln_sums = _conv_ln_bwd("conv_ln_bwd", c, dsw, w["conv_ln_g"], w["conv_ln_b"], after=pin)
    dpre, ddw, db1 = _conv_dw_bwd("conv_dw_bwd", dc, glu, a, gt, w["conv_dw"])
    (dw1,) = _mm("conv_pw1_bwd_dw", "tn", n1, [(dpre, None, 0)], _ep_id, [(BF16,)], m=d, n=2 * d, k=s)
    pin = emit("conv1", dict(conv_w1=dw1))
    (dn1,) = _mm("conv_pw1_bwd_dx", "nt", dpre, [(w_conv1, None, 0)], _ep_id, [(BF16,)], m=s, n=d, k=2 * d,
                 after=pin)
    dx, _, d_a_norm, _ = _rms_bwd("a_norm_bwd", x, w["a_norm_g"], [dn1], dh1)

    g.update(
        a_norm_g=d_a_norm, conv_b1=db1, conv_dw=ddw, conv_dw_b=ln_sums[2:3], conv_ln_g=ln_sums[0:1],
        conv_ln_b=ln_sums[1:2], conv_b2=cs_h1, kv_norm_g=dg_kvq[0:1], b_norm_g=dg_kvq[1:2],
        ffn_norm_g=jnp.concatenate([dgain0, dgain1], axis=0), final_norm_g=d_final_g,
    )
    return loss_cols, dx, g


HBM_SPEC = pl.BlockSpec(memory_space=pltpu.HBM)


def _mesh_place():
    x, y, c = lax.axis_index("x"), lax.axis_index("y"), lax.axis_index("c")
    chips = [(1 - x, y), (x, 1 - y), (1 - x, 1 - y)]
    return x, y, c, chips


def _shard_view(ref, kind, s, half=None):
    rows, cols = ref.shape
    if kind == "col":
        cw = cols // N_CHIPS
        if half is None:
            return ref.at[pl.ds(0, rows), pl.ds(s * cw, cw)]
        return ref.at[pl.ds(half * (rows // 2), rows // 2), pl.ds(s * cw, cw)]
    r = rows // N_CHIPS
    if half is None:
        return ref.at[pl.ds(s * r, r), pl.ds(0, cols)]
    return ref.at[pl.ds(s * r + half * (r // 2), r // 2), pl.ds(0, cols)]


def _full_shape(shard, kind):
    r, cw = shard.shape[-2:]
    return (r, cw * N_CHIPS) if kind == "col" else (r * N_CHIPS, cw)


def _gather_weights(shards, kinds, after=()):
    nt = len(shards)
    nf = len(after)
    fulls = [jax.ShapeDtypeStruct(_full_shape(sh, kind), sh.dtype) for sh, kind in zip(shards, kinds)]

    def body(*refs):
        src = refs[:nt]
        dst = refs[nt + nf:2 * nt + nf]
        send, recv, fsend, frecv, local = refs[2 * nt + nf:]
        x, y, c, chips = _mesh_place()
        s = 2 * x + y
        sib = (x, y, 1 - c)

        def half_of_shard(t):
            r, cw = src[t].shape
            return src[t].at[pl.ds(c * (r // 2), r // 2), pl.ds(0, cw)]

        locals_ = [pltpu.make_async_copy(src[t], _shard_view(dst[t], kinds[t], s), local.at[t]) for t in range(nt)]
        for cp in locals_:
            cp.start()
        sends = []
        for t in range(nt):
            for j, chip in enumerate(chips):
                cp = pltpu.make_async_remote_copy(
                    src_ref=half_of_shard(t), dst_ref=_shard_view(dst[t], kinds[t], s, c),
                    send_sem=send.at[t, j], recv_sem=recv.at[t, j], device_id=(*chip, c), device_id_type=MESH)
                cp.start()
                sends.append(cp)
        for t in range(nt):
            for j, (px, py) in enumerate(chips):
                landed = _shard_view(dst[t], kinds[t], 2 * px + py, c)
                pltpu.make_async_remote_copy(
                    src_ref=half_of_shard(t), dst_ref=landed, send_sem=send.at[t, j], recv_sem=recv.at[t, j],
                    device_id=(px, py, c), device_id_type=MESH).wait_recv()
                cp = pltpu.make_async_remote_copy(
                    src_ref=landed, dst_ref=landed, send_sem=fsend.at[t, j], recv_sem=frecv.at[t, j],
                    device_id=sib, device_id_type=MESH)
                cp.start()
                sends.append(cp)
        for t in range(nt):
            for j, (px, py) in enumerate(chips):
                other = _shard_view(dst[t], kinds[t], 2 * px + py, 1 - c)
                pltpu.make_async_remote_copy(
                    src_ref=other, dst_ref=other, send_sem=fsend.at[t, j], recv_sem=frecv.at[t, j],
                    device_id=sib, device_id_type=MESH).wait_recv()
        for cp in sends:
            cp.wait_send()
        for cp in locals_:
            cp.wait()

    return pl.pallas_call(
        body,
        name="gather_weights",
        in_specs=[HBM_SPEC] * nt + [ANY_SPEC] * nf,
        out_specs=[HBM_SPEC] * nt,
        out_shape=fulls,
        scratch_shapes=[pltpu.SemaphoreType.DMA((nt, 3))] * 4 + [pltpu.SemaphoreType.DMA((nt,))],
    )(*shards, *after)


def _row_blocks(rows, want=512):
    nb = 1
    while rows // nb > want or rows % nb or (rows // nb) % 16:
        nb += 1
        if nb > rows:
            return rows, 1
    return rows // nb, nb


def _place_own(name, src, kind, land, ids, lead=0):
    r, cw = src.shape[-2:]
    tr, nb = _row_blocks(r)
    if src.ndim == 3:
        src_spec = pl.BlockSpec((None, tr, cw), lambda i, ids_ref: (lead, i, 0))
    else:
        src_spec = pl.BlockSpec((tr, cw), lambda i, ids_ref: (i, 0))
    if kind == "col":
        dst_spec = pl.BlockSpec((tr, cw), lambda i, ids_ref: (i, ids_ref[0]))
    else:
        dst_spec = pl.BlockSpec((tr, cw), lambda i, ids_ref: (ids_ref[0] * nb + i, 0))

    def body(ids_ref, s_ref, o_ref):
        o_ref[...] = s_ref[...].astype(o_ref.dtype)

    return pl.pallas_call(
        body,
        name=name,
        grid_spec=pltpu.PrefetchScalarGridSpec(num_scalar_prefetch=1, grid=(nb,), in_specs=[src_spec],
                                               out_specs=dst_spec),
        out_shape=land,
        compiler_params=pltpu.CompilerParams(dimension_semantics=("parallel",)),
    )(ids, src)


SEM_SPEC = pl.BlockSpec(memory_space=pltpu.SEMAPHORE)
SIDE_EFFECT = pltpu.SideEffectType.DATAFLOW_SIDE_EFFECTING


def _copies_start(name, srcs, lands, copies, n_sems, after):
    ns, nl, nf = len(srcs), len(lands), len(after)

    def body(*refs):
        src, land = refs[:ns], refs[ns:ns + nl]
        send, recv = refs[ns + nl + nf], refs[ns + nl + nf + 1]
        pin = refs[-1]
        for cp in copies(src, land, send, recv, _mesh_place()):
            cp.start()
        pin[...] = jnp.zeros_like(pin)

    arrs = list(srcs) + list(lands)
    res = pl.pallas_call(
        body,
        name=name,
        in_specs=[HBM_SPEC] * (ns + nl) + [ANY_SPEC] * nf,
        out_specs=[SEM_SPEC, SEM_SPEC] + [HBM_SPEC] * (ns + nl) + [pl.BlockSpec(memory_space=pltpu.VMEM)],
        out_shape=[pltpu.SemaphoreType.DMA((n_sems,)), pltpu.SemaphoreType.DMA((n_sems,))]
        + [pltpu.HBM(a.shape, a.dtype) for a in arrs] + [jax.ShapeDtypeStruct((8, 128), F32)],
        input_output_aliases={i: 2 + i for i in range(ns + nl)},
        compiler_params=pltpu.CompilerParams(has_side_effects=SIDE_EFFECT),
    )(*[pltpu.with_memory_space_constraint(a, pltpu.HBM) for a in arrs], *after)
    return res[0], res[1], list(res[2:2 + ns]), list(res[2 + ns:2 + ns + nl]), res[-1]


def _copies_wait(name, send, recv, srcs, lands, copies, after):
    ns, nl, nf = len(srcs), len(lands), len(after)

    def body(*refs):
        src, land = refs[:ns], refs[ns:ns + nl]
        send_sems, recv_sems = refs[ns + nl], refs[ns + nl + 1]
        cps = copies(src, land, send_sems, recv_sems, _mesh_place())
        for cp in cps:
            cp.wait_send()
        for cp in cps:
            cp.wait_recv()

    arrs = list(srcs) + list(lands)
    res = pl.pallas_call(
        body,
        name=name,
        in_specs=[HBM_SPEC] * (ns + nl) + [SEM_SPEC, SEM_SPEC] + [ANY_SPEC] * nf,
        out_specs=[HBM_SPEC] * (ns + nl),
        out_shape=[pltpu.HBM(a.shape, a.dtype) for a in arrs],
        input_output_aliases={i: i for i in range(ns + nl)},
        compiler_params=pltpu.CompilerParams(has_side_effects=SIDE_EFFECT),
    )(*arrs, send, recv, *after)
    return list(res[:ns]), list(res[ns:])


def _gather_copies(kinds, halves):
    def copies(src, land, send, recv, place):
        x, y, c, chips = place
        s = 2 * x + y
        mine = [_shard_view(land[t], kinds[t], s, c if halves else None) for t in range(len(kinds))]
        return [
            pltpu.make_async_remote_copy(
                src_ref=mine[t], dst_ref=mine[t], send_sem=send.at[3 * t + j], recv_sem=recv.at[3 * t + j],
                device_id=(px, py, c), device_id_type=MESH)
            for t in range(len(kinds)) for j, (px, py) in enumerate(chips)
        ]

    return copies


def _gather_pass_copies(kinds):
    def copies(src, land, send, recv, place):
        x, y, c, chips = place
        views = [_shard_view(land[t], kinds[t], 2 * px + py, c) for t in range(len(kinds)) for px, py in chips]
        return [
            pltpu.make_async_remote_copy(src_ref=v, dst_ref=v, send_sem=send.at[i], recv_sem=recv.at[i],
                                         device_id=(x, y, 1 - c), device_id_type=MESH)
            for i, v in enumerate(views)
        ]

    return copies


def _gather_pass_on(name, lands, kinds):
    nt = len(lands)

    def body(*refs):
        buf = refs[nt:2 * nt]
        send, recv = refs[2 * nt:]
        x, y, c, chips = _mesh_place()
        sib = (x, y, 1 - c)
        sends = []
        for t in range(nt):
            for j, (px, py) in enumerate(chips):
                mine = _shard_view(buf[t], kinds[t], 2 * px + py, c)
                cp = pltpu.make_async_remote_copy(src_ref=mine, dst_ref=mine, send_sem=send.at[t, j],
                                                  recv_sem=recv.at[t, j], device_id=sib, device_id_type=MESH)
                cp.start()
                sends.append(cp)
        for t in range(nt):
            for j, (px, py) in enumerate(chips):
                theirs = _shard_view(buf[t], kinds[t], 2 * px + py, 1 - c)
                pltpu.make_async_remote_copy(src_ref=theirs, dst_ref=theirs, send_sem=send.at[t, j],
                                             recv_sem=recv.at[t, j], device_id=sib, device_id_type=MESH).wait_recv()
        for cp in sends:
            cp.wait_send()

    return pl.pallas_call(
        body,
        name=name,
        in_specs=[HBM_SPEC] * nt,
        out_specs=[HBM_SPEC] * nt,
        out_shape=[jax.ShapeDtypeStruct(a.shape, a.dtype) for a in lands],
        input_output_aliases={i: i for i in range(nt)},
        scratch_shapes=[pltpu.SemaphoreType.DMA((nt, 3))] * 2,
    )(*lands)


def _grad_part(ref, kind, s):
    return ref if kind == "all" else _shard_view(ref, kind, s)


def _grad_copies(kinds):
    def peers(place):
        x, y, c, chips = place
        return [(x, y, 1 - c)] + [(px, py, c) for px, py in chips]

    def copies(src, land, send, recv, place):
        x, y, c, chips = place
        me = 4 * x + 2 * y + c
        return [
            pltpu.make_async_remote_copy(
                src_ref=_grad_part(src[t], kinds[t], 2 * px + py), dst_ref=land[t].at[me],
                send_sem=send.at[GRAD_PEERS * t + k], recv_sem=recv.at[GRAD_PEERS * t + k], device_id=(px, py, pc),
                device_id_type=MESH)
            for t in range(len(kinds)) for k, (px, py, pc) in enumerate(peers(place))
        ]

    return copies


def _pass_copies(n):
    def copies(src, land, send, recv, place):
        x, y, c, chips = place
        return [
            pltpu.make_async_remote_copy(
                src_ref=land[t].at[4 * px + 2 * py + c], dst_ref=land[t].at[4 * px + 2 * py + c],
                send_sem=send.at[3 * t + j], recv_sem=recv.at[3 * t + j], device_id=(x, y, 1 - c),
                device_id_type=MESH)
            for t in range(n) for j, (px, py) in enumerate(chips)
        ]

    return copies


GRAD_PEERS = 4


def _land_shape(grad, kind):
    rows, cols = grad.shape
    if kind == "col":
        return (N_DEV, rows, cols // N_CHIPS)
    if kind == "row":
        return (N_DEV, rows // N_CHIPS, cols)
    return (N_DEV, rows, cols)


def _adamw_reduce(name, contrib, own, kind, ids, w, m, v, layer=None, prev=None):
    rows, cols = w.shape[-2:]
    t = rows
    for cand in (256, 128):
        if rows % cand == 0 and cand * cols <= 256 * 1408:
            t = cand
            break
    nb = rows // t
    c1 = 1.0 - ADAM_B1 ** ADAM_STEP
    c2 = 1.0 - ADAM_B2 ** ADAM_STEP

    n_prev = 0 if prev is None else 4

    def body(ids_ref, c_ref, own_ref, w_ref, m_ref, v_ref, *refs):
        g_ref, d_ref, nm_ref, nv_ref = refs[n_prev:]
        me = ids_ref[1]
        mine = own_ref[...].astype(F32)
        g = None
        for q in range(N_DEV):
            term = jnp.where(me == q, mine, c_ref[q].astype(F32))
            g = term if g is None else g + term
        nm = ADAM_B1 * m_ref[...] + (1.0 - ADAM_B1) * g
        nv = ADAM_B2 * v_ref[...] + (1.0 - ADAM_B2) * (g * g)
        g_ref[...] = g
        nm_ref[...] = nm
        nv_ref[...] = nv
        d_ref[...] = -ADAM_LR * ((nm / c1) / (jnp.sqrt(nv / c2) + ADAM_EPS) + ADAM_WD * w_ref[...])

    if layer is None:
        blk = pl.BlockSpec((t, cols), lambda i, ids_ref: (i, 0))
    else:
        blk = pl.BlockSpec((None, t, cols), lambda i, ids_ref: (layer, i, 0))
    if kind == "col":
        own_spec = pl.BlockSpec((t, cols), lambda i, ids_ref: (i, ids_ref[0]))
    elif kind == "row":
        own_spec = pl.BlockSpec((t, cols), lambda i, ids_ref: (ids_ref[0] * nb + i, 0))
    else:
        own_spec = pl.BlockSpec((t, cols), lambda i, ids_ref: (i, 0))
    return pl.pallas_call(
        body,
        name=name,
        grid_spec=pltpu.PrefetchScalarGridSpec(
            num_scalar_prefetch=1,
            grid=(nb,),
            in_specs=[pl.BlockSpec((N_DEV, t, cols), lambda i, ids_ref: (0, i, 0)), own_spec, blk, blk, blk]
            + [ANY_SPEC] * n_prev,
            out_specs=[blk] * 4,
        ),
        out_shape=[jax.ShapeDtypeStruct(w.shape, F32)] * 4,
        input_output_aliases={6 + i: i for i in range(n_prev)},
        compiler_params=pltpu.CompilerParams(dimension_semantics=("parallel",)),
    )(ids, contrib, own, w, m, v, *(prev or ()))


WEIGHT_NAMES = ("a_norm_g", "conv_w1", "conv_b1", "conv_dw", "conv_dw_b", "conv_ln_g", "conv_ln_b", "conv_w2",
                "conv_b2", "kv_norm_g", "w_k", "w_v", "b_norm_g", "w_q", "w_o", "ffn_norm_g", "ffn_w_gate",
                "ffn_w_up", "ffn_w_down", "final_norm_g")
GROUPS = {
    "conv2": (("conv_w2", "conv_w2", None, "row"),),
    "ffn0": (("gate", "ffn_w_gate", 0, "col"), ("up", "ffn_w_up", 0, "col"), ("down", "ffn_w_down", 0, "row")),
    "attn": (("w_k", "w_k", None, "row"), ("w_v", "w_v", None, "row"), ("w_q", "w_q", None, "row"),
             ("w_o", "w_o", None, "row")),
    "ffn1": (("gate", "ffn_w_gate", 1, "col"), ("up", "ffn_w_up", 1, "col"), ("down", "ffn_w_down", 1, "row")),
}
GROUPS["conv1"] = (("conv_w1", "conv_w1", None, "col"),)
for _layer in (0, 1):
    GROUPS[f"ffn{_layer}d"] = tuple(it for it in GROUPS[f"ffn{_layer}"] if it[0] == "down")
    GROUPS[f"ffn{_layer}gu"] = tuple(it for it in GROUPS[f"ffn{_layer}"] if it[0] != "down")
FETCH_ORDER = ("conv1", "conv2", "ffn0gu", "ffn0d", "attn", "ffn1")
HALVED = ("conv1", "conv2", "ffn0gu", "ffn0d", "ffn1")
EMIT_ORDER = ("ffn1d", "ffn1gu", "attn", "ffn0d", "ffn0gu", "conv2", "conv1", "vec")
RETIRE_AT = {"attn": ("ffn1d", "ffn1gu"), "ffn0d": ("attn",), "conv1": ("ffn0d", "ffn0gu", "conv2")}
PACKED = (("a_norm_g", 0, 1), ("conv_b1", 8, 2), ("conv_dw", 16, CONV_WIDTH), ("conv_dw_b", 48, 1),
          ("conv_ln_g", 56, 1), ("conv_ln_b", 64, 1), ("conv_b2", 72, 1))
PACK_ROWS = 80
WHOLE = (("kv_norm_g", 0, 1), ("b_norm_g", 1, 1), ("ffn_norm_g", 2, 2), ("final_norm_g", 4, 1))
WHOLE_ROWS = 8


def _pack_rows(parts, total, width):
    out, at = [], 0
    for arr, first in parts:
        if first > at:
            out.append(jnp.zeros((first - at, width), F32))
        rows8 = -(-arr.shape[0] // 8) * 8
        out.append(jnp.pad(arr, ((0, rows8 - arr.shape[0]), (0, 0))))
        at = first + rows8
    if total > at:
        out.append(jnp.zeros((total - at, width), F32))
    return jnp.concatenate(out, axis=0)


def kernel(x, a_norm_g, conv_w1, conv_b1, conv_dw, conv_dw_b, conv_ln_g, conv_ln_b, conv_w2, conv_b2, kv_norm_g, w_k, w_v, b_norm_g, w_q, w_o, ffn_norm_g, ffn_w_gate, ffn_w_up, ffn_w_down, final_norm_g, loss_target, m_a_norm_g, m_conv_w1, m_conv_b1, m_conv_dw, m_conv_dw_b, m_conv_ln_g, m_conv_ln_b, m_conv_w2, m_conv_b2, m_kv_norm_g, m_w_k, m_w_v, m_b_norm_g, m_w_q, m_w_o, m_ffn_norm_g, m_ffn_w_gate, m_ffn_w_up, m_ffn_w_down, m_final_norm_g, v_a_norm_g, v_conv_w1, v_conv_b1, v_conv_dw, v_conv_dw_b, v_conv_ln_g, v_conv_ln_b, v_conv_w2, v_conv_b2, v_kv_norm_g, v_w_k, v_w_v, v_b_norm_g, v_w_q, v_w_o, v_ffn_norm_g, v_ffn_w_gate, v_ffn_w_up, v_ffn_w_down, v_final_norm_g):
    args = locals()
    wts = {n: args[n] for n in WEIGHT_NAMES}
    mom = {n: args["m_" + n] for n in WEIGHT_NAMES}
    vel = {n: args["v_" + n] for n in WEIGHT_NAMES}
    s, d = x.shape[-2:]
    dq = d // N_CHIPS
    x2 = x.reshape(s, d)
    tgt = loss_target.reshape(s, d)

    def pack_shard(src):
        return _pack_rows([(src[n].reshape(-1, dq), first) for n, first, _ in PACKED], PACK_ROWS, dq)

    def pack_whole(src):
        return _pack_rows([(jnp.concatenate([src[n].reshape(-1, d) for n, _, _ in WHOLE], axis=0), 0)], WHOLE_ROWS, d)

    gathers = {}
    pins = []
    ids = jnp.stack([2 * lax.axis_index("x") + lax.axis_index("y"),
                     4 * lax.axis_index("x") + 2 * lax.axis_index("y") + lax.axis_index("c")]).astype(jnp.int32)

    def start_gather(grp, pins):
        kinds = [kind for _, _, _, kind in GROUPS[grp]]
        lands = []
        for key, n, layer, kind in GROUPS[grp]:
            shard = wts[n] if layer is not None else wts[n].reshape(wts[n].shape[-2:])
            full = jax.ShapeDtypeStruct(_full_shape(shard, kind), BF16)
            lands.append(_place_own(f"gather_place_{grp}_{key}", shard, kind, full, ids, layer))
        copies = _gather_copies(kinds, grp in HALVED)
        send, recv, _, lands, pin = _copies_start("gather_start_" + grp, [], lands, copies, 3 * len(lands), pins)
        gathers[grp] = (send, recv, [], lands, copies, kinds)
        return [pin]

    (packed_full,) = _gather_weights([pack_shard(wts)], ["row"])
    w = {}
    pins = start_gather(FETCH_ORDER[0], [packed_full])
    started = {}
    swapping = {}

    def prefetch(grp, after):
        if grp == "rest":
            last = [after]
            for later in FETCH_ORDER[1:]:
                last = start_gather(later, last)
            started["pin"] = last
            return
        send, recv, srcs, lands, copies, kinds = gathers[grp]
        _, fulls = _copies_wait("gather_wait_" + grp, send, recv, srcs, lands, copies, [after])
        copies = _gather_pass_copies(kinds)
        send, recv, _, fulls, _ = _copies_start("gather_swap_" + grp, [], fulls, copies, 3 * len(fulls), ())
        swapping[grp] = (send, recv, fulls, copies)

    def fetch(grp, after):
        if grp in swapping:
            send, recv, fulls, copies = swapping[grp]
            _, fulls = _copies_wait("gather_swapped_" + grp, send, recv, [], fulls, copies, [after])
        else:
            send, recv, srcs, lands, copies, kinds = gathers[grp]
            behind = [after] + (started.get("pin", []) if grp == FETCH_ORDER[0] else [])
            _, fulls = _copies_wait("gather_wait_" + grp, send, recv, srcs, lands, copies, behind)
            if grp in HALVED:
                fulls = _gather_pass_on("gather_pass_" + grp, fulls, kinds)
        return {key: full for (key, _, _, _), full in zip(GROUPS[grp], fulls)}

    packed = packed_full.reshape(N_CHIPS, PACK_ROWS, dq)
    for n, first, rows in PACKED:
        part = packed[:, first:first + rows, :]
        if n == "conv_dw":
            w[n] = part.transpose(1, 0, 2).reshape(rows, d)
        else:
            w[n] = part.reshape(1, N_CHIPS * rows * dq)
    for n, _, rows in WHOLE:
        w[n] = wts[n].reshape(rows, d)

    exchanges = {}
    passing = {}
    own_grads = {}

    def retire(tag, after):
        send, recv, srcs, lands, copies, kinds = exchanges[tag]
        srcs, lands = _copies_wait("grads_wait_" + tag, send, recv, srcs, lands, copies, after)
        own_grads[tag] = list(zip(srcs, kinds))
        copies = _pass_copies(len(lands))
        send, recv, _, lands, pin = _copies_start("grads_pass_" + tag, [], lands, copies, 3 * len(lands), ())
        passing[tag] = (send, recv, lands, copies)
        return pin

    def emit_grads(tag, grads, kinds):
        pins = [retire(old, [grads[0]]) for old in RETIRE_AT.get(tag, ())]
        copies = _grad_copies(kinds)
        lands = [lax.empty(_land_shape(gr, kd), gr.dtype) for gr, kd in zip(grads, kinds)]
        send, recv, srcs, lands, pin = _copies_start("grads_start_" + tag, grads, lands, copies,
                                                     GRAD_PEERS * len(grads), pins)
        exchanges[tag] = (send, recv, srcs, lands, copies, kinds)
        return [pin]

    def emit(grp, grads):
        return emit_grads(grp, [grads[key] for key, _, _, _ in GROUPS[grp]], [kind for _, _, _, kind in GROUPS[grp]])

    loss_cols, dx, g = _local_step(x2, tgt, w, fetch, emit, prefetch, after=pins)
    loss = lax.psum(jnp.sum(loss_cols), ("x", "y", "c"))

    gp = []
    for n, first, rows in PACKED:
        if n == "conv_dw":
            part = g[n].reshape(rows, N_CHIPS, dq).transpose(1, 0, 2)
        else:
            part = g[n].reshape(N_CHIPS, rows, dq)
        gp.append((part, first))
    g_packed = jnp.concatenate(
        [_pack_rows([(p[ci], first) for p, first in gp], PACK_ROWS, dq) for ci in range(N_CHIPS)], axis=0)
    emit_grads("vec", [g_packed, pack_whole(g)], ["row", "all"])
    for tag in EMIT_ORDER:
        if tag not in passing:
            retire(tag, [dx])
    contribs = {}
    for tag in EMIT_ORDER:
        send, recv, lands, copies = passing[tag]
        _, arrived = _copies_wait("grads_passed_" + tag, send, recv, [], lands, copies, [dx])
        contribs[tag] = [(c, own, kind) for c, (own, kind) in zip(arrived, own_grads[tag])]

    res = {}

    def adamw(n, contrib, layer=None, prev=None):
        arrived, own, kind = contrib
        if layer is None:
            shape = wts[n].shape
            r2 = shape[-2:]
            outs = _adamw_reduce("adamw_" + n, arrived, own, kind, ids, wts[n].reshape(r2), mom[n].reshape(r2),
                                 vel[n].reshape(r2))
            return [o.reshape(shape) for o in outs]
        return _adamw_reduce(f"adamw_{n}_{layer}", arrived, own, kind, ids, wts[n], mom[n], vel[n], layer, prev)

    for grp in ("attn", "conv2", "conv1"):
        for (key, n, _, _), contrib in zip(GROUPS[grp], contribs[grp]):
            res[n] = adamw(n, contrib)
    for part in ("d", "gu"):
        for (key, n, _, _), c0, c1 in zip(GROUPS["ffn0" + part], contribs["ffn0" + part], contribs["ffn1" + part]):
            res[n] = adamw(n, c1, 1, adamw(n, c0, 0))
    outs = _adamw_reduce("adamw_packed", *contribs["vec"][0], ids, pack_shard(wts), pack_shard(mom), pack_shard(vel))
    for n, first, rows in PACKED:
        res[n] = [o[first:first + rows].reshape(wts[n].shape) for o in outs]
    outs = _adamw_reduce("adamw_whole", *contribs["vec"][1], ids, pack_whole(wts), pack_whole(mom), pack_whole(vel))
    for n, first, rows in WHOLE:
        res[n] = [o[first:first + rows].reshape(wts[n].shape) for o in outs]

    out = [loss, dx.reshape(x.shape)]
    for which in range(4):
        out += [res[n][which] for n in WEIGHT_NAMES]
    return tuple(out)
```

```python
import functools

import jax
import jax.numpy as jnp
from jax import lax
from jax.experimental import pallas as pl
from jax.experimental.pallas import tpu as pltpu

F32 = jnp.float32
BF16 = jnp.bfloat16

HEAD_DIM = 128
BRANCHES = ((128, 1), (512, 4), (2048, 16))
CONV_WIDTH = 31
CONV_HALO = 32
RMS_EPS = 1e-6
LN_EPS = 1e-5
ADAM_LR = 0.001
ADAM_B1 = 0.9
ADAM_B2 = 0.999
ADAM_EPS = 1e-08
ADAM_WD = 0.01
ADAM_STEP = 10
N_CHIPS = 4
N_DEV = 8
MESH = pl.DeviceIdType.MESH


def _sigmoid(x):
    return 0.5 * jnp.tanh(0.5 * x) + 0.5


def _row_tile(rows, want):
    t = min(rows, want)
    assert rows % t == 0, (rows, want)
    return t


_DOT_DIMS = {"nn": ((1,), (0,)), "nt": ((1,), (1,)), "tn": ((0,), (0,))}


ANY_SPEC = pl.BlockSpec(memory_space=pl.ANY)
WIDE_BN = 1024


def _mm(name, mode, a, bs, epilogue, outs, *, m, n, k, extras=(), bm=1024, bn=512, bk=None, after=()):
    bm, bn = min(bm, m), min(bn, n)
    bk = k if bk is None else min(bk, k)
    assert m % bm == 0 and n % bn == 0 and k % bk == 0, (name, m, n, k, bm, bn, bk)
    nk = k // bk
    a_list = list(a) if isinstance(a, (list, tuple)) else [a]
    na, nb, ne, no = len(a_list), len(bs), len(extras), len(outs)
    assert na in (1, nb)

    if mode == "tn":
        a_spec = pl.BlockSpec((bk, bm), lambda i, j, kk: (kk, i))
    else:
        a_spec = pl.BlockSpec((bm, bk), lambda i, j, kk: (i, kk))

    def b_spec(lead, off):
        if mode == "nt":
            blk, idx = (bn, bk), (lambda i, j, kk: (j + off, kk))
        else:
            blk, idx = (bk, bn), (lambda i, j, kk: (kk, j + off))
        if lead is None:
            return pl.BlockSpec(blk, idx)
        return pl.BlockSpec((None,) + blk, lambda i, j, kk: (lead,) + idx(i, j, kk))

    def e_spec(kind, off):
        if kind == "mn":
            return pl.BlockSpec((bm, bn), lambda i, j, kk: (i, j + off))
        return pl.BlockSpec((1, bn), lambda i, j, kk: (0, j + off))

    nf = len(after)
    in_specs = [a_spec] * na + [b_spec(l, o) for _, l, o in bs] + [e_spec(kd, o) for _, kd, o in extras]
    in_specs += [ANY_SPEC] * nf
    out_specs = [pl.BlockSpec((bm, bn), lambda i, j, kk: (i, j)) for _ in outs]
    out_shape = [jax.ShapeDtypeStruct((m, n), dt) for (dt,) in outs]
    dims = (_DOT_DIMS[mode], ((), ()))

    def body(*refs):
        a_refs = refs[:na]
        b_refs = refs[na:na + nb]
        e_refs = refs[na + nb:na + nb + ne]
        o_refs = refs[na + nb + ne + nf:na + nb + ne + nf + no]
        acc_refs = refs[na + nb + ne + nf + no:]
        avs = [a_ref[...].astype(BF16) for a_ref in a_refs]
        prods = [lax.dot_general(avs[bi % na], b_ref[...].astype(BF16), dims, preferred_element_type=F32)
                 for bi, b_ref in enumerate(b_refs)]

        def finish(accs):
            res = epilogue(accs, [e_ref[...] for e_ref in e_refs])
            for o_ref, r in zip(o_refs, res):
                o_ref[...] = r.astype(o_ref.dtype)

        if nk == 1:
            finish(prods)
        else:
            kk = pl.program_id(2)

            @pl.when(kk == 0)
            def _():
                for acc_ref, p in zip(acc_refs, prods):
                    acc_ref[...] = p

            @pl.when(kk > 0)
            def _():
                for acc_ref, p in zip(acc_refs, prods):
                    acc_ref[...] += p

            @pl.when(kk == nk - 1)
            def _():
                finish([acc_ref[...] for acc_ref in acc_refs])

    scratch = [] if nk == 1 else [pltpu.VMEM((bm, bn), F32) for _ in bs]
    res = pl.pallas_call(
        body,
        name=name,
        grid=(m // bm, n // bn, nk),
        in_specs=in_specs,
        out_specs=out_specs,
        out_shape=out_shape,
        scratch_shapes=scratch,
        compiler_params=pltpu.CompilerParams(dimension_semantics=("parallel", "parallel", "arbitrary")),
    )(*a_list, *[b for b, _, _ in bs], *[e for e, _, _ in extras], *after)
    return res


def _rms_fwd(name, x, gains, after=()):
    s, d = x.shape
    ng = gains.shape[0]
    t = _row_tile(s, 256)
    nf = len(after)

    def body(x_ref, g_ref, *refs):
        o_refs = refs[nf:]
        xv = x_ref[...]
        r = lax.rsqrt(jnp.mean(xv * xv, axis=-1, keepdims=True) + RMS_EPS)
        xh = xv * r
        for gi, o_ref in enumerate(o_refs):
            o_ref[...] = (xh * g_ref[gi:gi + 1, :]).astype(o_ref.dtype)

    return pl.pallas_call(
        body,
        name=name,
        grid=(s // t,),
        in_specs=[pl.BlockSpec((t, d), lambda i: (i, 0)), pl.BlockSpec((ng, d), lambda i: (0, 0))] + [ANY_SPEC] * nf,
        out_specs=[pl.BlockSpec((t, d), lambda i: (i, 0)) for _ in range(ng)],
        out_shape=[jax.ShapeDtypeStruct((s, d), BF16) for _ in range(ng)],
        compiler_params=pltpu.CompilerParams(dimension_semantics=("parallel",)),
    )(x, gains, *after)


def _rms_bwd(name, x, gains, dns, dres):
    s, d = x.shape
    ng = gains.shape[0]
    t = _row_tile(s, 256)

    def body(x_ref, g_ref, dres_ref, *refs):
        dn_refs = refs[:ng]
        dx_ref, dxb_ref, dg_ref, cs_ref = refs[ng:]
        i = pl.program_id(0)
        xv = x_ref[...]
        r = lax.rsqrt(jnp.mean(xv * xv, axis=-1, keepdims=True) + RMS_EPS)
        xh = xv * r
        dx = dres_ref[...]
        dgs = []
        for gi in range(ng):
            dn = dn_refs[gi][...].astype(F32)
            dxh = dn * g_ref[gi:gi + 1, :]
            dgs.append(jnp.sum(dn * xh, axis=0, keepdims=True))
            dx = dx + r * (dxh - xh * jnp.mean(dxh * xh, axis=-1, keepdims=True))
        dx_ref[...] = dx
        dxb_ref[...] = dx.astype(BF16)
        dg = jnp.concatenate(dgs, axis=0) if ng > 1 else dgs[0]
        cs = jnp.sum(dx, axis=0, keepdims=True)

        @pl.when(i == 0)
        def _():
            dg_ref[...] = dg
            cs_ref[...] = cs

        @pl.when(i > 0)
        def _():
            dg_ref[...] += dg
            cs_ref[...] += cs

    row = pl.BlockSpec((t, d), lambda i: (i, 0))
    return pl.pallas_call(
        body,
        name=name,
        grid=(s // t,),
        in_specs=[row, pl.BlockSpec((ng, d), lambda i: (0, 0)), row] + [row] * ng,
        out_specs=[row, row, pl.BlockSpec((ng, d), lambda i: (0, 0)), pl.BlockSpec((1, d), lambda i: (0, 0))],
        out_shape=[
            jax.ShapeDtypeStruct((s, d), F32),
            jax.ShapeDtypeStruct((s, d), BF16),
            jax.ShapeDtypeStruct((ng, d), F32),
            jax.ShapeDtypeStruct((1, d), F32),
        ],
        compiler_params=pltpu.CompilerParams(dimension_semantics=("arbitrary",)),
    )(x, gains, dres, *dns)


def _final_loss(name, h, gain, target):
    s, d = h.shape
    t = _row_tile(s, 256)

    def body(h_ref, g_ref, t_ref, dh_ref, dhb_ref, dg_ref, ls_ref):
        i = pl.program_id(0)
        xv = h_ref[...]
        g = g_ref[...]
        r = lax.rsqrt(jnp.mean(xv * xv, axis=-1, keepdims=True) + RMS_EPS)
        xh = xv * r
        err = xh * g - t_ref[...]
        ls = jnp.sum(err * err, axis=0, keepdims=True) * (0.5 / d)
        dy = err * (1.0 / d)
        dxh = dy * g
        dg = jnp.sum(dy * xh, axis=0, keepdims=True)
        dx = r * (dxh - xh * jnp.mean(dxh * xh, axis=-1, keepdims=True))
        dh_ref[...] = dx
        dhb_ref[...] = dx.astype(BF16)

        @pl.when(i == 0)
        def _():
            dg_ref[...] = dg
            ls_ref[...] = ls

        @pl.when(i > 0)
        def _():
            dg_ref[...] += dg
            ls_ref[...] += ls

    row = pl.BlockSpec((t, d), lambda i: (i, 0))
    vec = pl.BlockSpec((1, d), lambda i: (0, 0))
    return pl.pallas_call(
        body,
        name=name,
        grid=(s // t,),
        in_specs=[row, vec, row],
        out_specs=[row, row, vec, vec],
        out_shape=[
            jax.ShapeDtypeStruct((s, d), F32),
            jax.ShapeDtypeStruct((s, d), BF16),
            jax.ShapeDtypeStruct((1, d), F32),
            jax.ShapeDtypeStruct((1, d), F32),
        ],
        compiler_params=pltpu.CompilerParams(dimension_semantics=("arbitrary",)),
    )(h, gain, target)


SUBLANES = 8
CONV_LANES = 512
CONV_ROWS = 32
NORM_ROWS = 64


def _conv_tiles(s):
    t = _row_tile(s, 128)
    assert t % CONV_HALO == 0 and t % CONV_ROWS == 0
    return t, t // CONV_HALO


def _shifted_copies(dst, src, lanes, rows):
    for m in range(SUBLANES):
        n = rows if m == 0 else rows - SUBLANES
        dst[m, :n, :] = src[m:m + n, lanes]


def _shifted(copies, offset, n):
    m = offset % SUBLANES
    return copies[m, offset - m:offset - m + n, :]


def _dwconv_fwd(name, u, dw, dw_b, ln_g, ln_b):
    s, d = u.shape
    t, hb = _conv_tiles(s)
    w = dw.shape[0]
    lo = CONV_HALO - (w - 1)

    lw = min(CONV_LANES, d)

    def body(cur_ref, prev_ref, dw_ref, dwb_ref, lg_ref, lb_ref, c_ref, sw_ref, cat_ref, sh_ref):
        i = pl.program_id(0)
        cat_ref[CONV_HALO:, :] = cur_ref[...]

        @pl.when(i == 0)
        def _():
            cat_ref[:CONV_HALO, :] = jnp.zeros((CONV_HALO, d), F32)

        @pl.when(i > 0)
        def _():
            cat_ref[:CONV_HALO, :] = prev_ref[...]

        for lc in range(d // lw):
            lanes = slice(lc * lw, (lc + 1) * lw)
            _shifted_copies(sh_ref, cat_ref, lanes, t + CONV_HALO)
            for rc in range(t // CONV_ROWS):
                acc = jnp.broadcast_to(dwb_ref[:, lanes], (CONV_ROWS, lw))
                for kk in range(w):
                    acc = acc + dw_ref[kk:kk + 1, lanes] * _shifted(sh_ref, lo + kk + rc * CONV_ROWS, CONV_ROWS)
                c_ref[rc * CONV_ROWS:(rc + 1) * CONV_ROWS, lanes] = acc

        def norm_rows(ri, carry):
            rows = pl.ds(pl.multiple_of(ri * NORM_ROWS, NORM_ROWS), NORM_ROWS)
            cv = c_ref[rows, :]
            cc = cv - jnp.mean(cv, axis=-1, keepdims=True)
            var = jnp.mean(cc * cc, axis=-1, keepdims=True)
            ln = cc * lax.rsqrt(var + LN_EPS) * lg_ref[...] + lb_ref[...]
            sw_ref[rows, :] = (ln * _sigmoid(ln)).astype(BF16)
            return carry

        lax.fori_loop(0, t // NORM_ROWS, norm_rows, 0)

    row = pl.BlockSpec((t, d), lambda i: (i, 0))
    prev = pl.BlockSpec((CONV_HALO, d), lambda i: (jnp.maximum(i * hb - 1, 0), 0))
    vec = pl.BlockSpec((1, d), lambda i: (0, 0))
    return pl.pallas_call(
        body,
        name=name,
        grid=(s // t,),
        in_specs=[row, prev, pl.BlockSpec((w, d), lambda i: (0, 0)), vec, vec, vec],
        out_specs=[row, row],
        out_shape=[jax.ShapeDtypeStruct((s, d), F32), jax.ShapeDtypeStruct((s, d), BF16)],
        scratch_shapes=[pltpu.VMEM((CONV_HALO + t, d), F32), pltpu.VMEM((SUBLANES, CONV_HALO + t, lw), F32)],
        compiler_params=pltpu.CompilerParams(dimension_semantics=("parallel",)),
    )(u, u, dw, dw_b, ln_g, ln_b)


def _conv_ln_bwd(name, c, dsw, ln_g, ln_b, after=()):
    s, d = c.shape
    t = _row_tile(s, 256)
    nf = len(after)

    def body(c_ref, dsw_ref, lg_ref, lb_ref, *refs):
        dc_ref, sums_ref = refs[nf:]
        i = pl.program_id(0)
        cv = c_ref[...]
        g = lg_ref[...]
        mu = jnp.mean(cv, axis=-1, keepdims=True)
        cc = cv - mu
        rstd = lax.rsqrt(jnp.mean(cc * cc, axis=-1, keepdims=True) + LN_EPS)
        ch = cc * rstd
        ln = ch * g + lb_ref[...]
        sg = _sigmoid(ln)
        dln = dsw_ref[...] * (sg * (1.0 + ln * (1.0 - sg)))
        dch = dln * g
        dc = rstd * (dch - jnp.mean(dch, axis=-1, keepdims=True) - ch * jnp.mean(dch * ch, axis=-1, keepdims=True))
        dc_ref[...] = dc
        sums = jnp.concatenate(
            [
                jnp.sum(dln * ch, axis=0, keepdims=True),
                jnp.sum(dln, axis=0, keepdims=True),
                jnp.sum(dc, axis=0, keepdims=True),
                jnp.zeros((1, d), F32),
            ],
            axis=0,
        )

        @pl.when(i == 0)
        def _():
            sums_ref[...] = sums

        @pl.when(i > 0)
        def _():
            sums_ref[...] += sums

    row = pl.BlockSpec((t, d), lambda i: (i, 0))
    vec = pl.BlockSpec((1, d), lambda i: (0, 0))
    return pl.pallas_call(
        body,
        name=name,
        grid=(s // t,),
        in_specs=[row, row, vec, vec] + [ANY_SPEC] * nf,
        out_specs=[row, pl.BlockSpec((4, d), lambda i: (0, 0))],
        out_shape=[jax.ShapeDtypeStruct((s, d), F32), jax.ShapeDtypeStruct((4, d), F32)],
        compiler_params=pltpu.CompilerParams(dimension_semantics=("arbitrary",)),
    )(c, dsw, ln_g, ln_b, *after)


def _conv_dw_bwd(name, dc, u, a, gt, dw):
    s, d = dc.shape
    t, hb = _conv_tiles(s)
    w = dw.shape[0]
    lo = CONV_HALO - (w - 1)
    nt = s // t
    lw = min(CONV_LANES, d)

    def body(dc_ref, dcn_ref, u_ref, up_ref, a_ref, gt_ref, dw_ref, dpre_ref, ddw_ref, db_ref, dcat_ref, ucat_ref,
             dsh_ref, ush_ref, ddw_acc, db_acc):
        i = pl.program_id(0)
        dcat_ref[:t, :] = dc_ref[...]
        ucat_ref[CONV_HALO:, :] = u_ref[...]

        @pl.when(i == nt - 1)
        def _():
            dcat_ref[t:, :] = jnp.zeros((CONV_HALO, d), F32)

        @pl.when(i < nt - 1)
        def _():
            dcat_ref[t:, :] = dcn_ref[...]

        @pl.when(i == 0)
        def _():
            ucat_ref[:CONV_HALO, :] = jnp.zeros((CONV_HALO, d), F32)

        @pl.when(i > 0)
        def _():
            ucat_ref[:CONV_HALO, :] = up_ref[...]

        @pl.when(i == 0)
        def _():
            ddw_acc[...] = jnp.zeros(ddw_acc.shape, F32)
            db_acc[...] = jnp.zeros(db_acc.shape, F32)

        def fold(p):
            return functools.reduce(jnp.add, [p[j:j + SUBLANES] for j in range(0, CONV_ROWS, SUBLANES)])

        for lc in range(d // lw):
            lanes = slice(lc * lw, (lc + 1) * lw)
            gate_lanes = slice(d + lc * lw, d + (lc + 1) * lw)
            _shifted_copies(dsh_ref, dcat_ref, lanes, t + CONV_HALO)
            _shifted_copies(ush_ref, ucat_ref, lanes, t + CONV_HALO)
            for rc in range(t // CONV_ROWS):
                r0 = rc * CONV_ROWS
                rows = slice(r0, r0 + CONV_ROWS)
                dcv = dc_ref[rows, lanes]
                du = jnp.zeros((CONV_ROWS, lw), F32)
                for kk in range(w):
                    du = du + dw_ref[kk:kk + 1, lanes] * _shifted(dsh_ref, w - 1 - kk + r0, CONV_ROWS)
                    ddw_acc[kk, :, lanes] += fold(dcv * _shifted(ush_ref, lo + kk + r0, CONV_ROWS))
                av = a_ref[rows, lanes].astype(F32)
                sg = _sigmoid(gt_ref[rows, lanes].astype(F32))
                da = du * sg
                dgt = du * av * sg * (1.0 - sg)
                dpre_ref[rows, lanes] = da.astype(BF16)
                dpre_ref[rows, gate_lanes] = dgt.astype(BF16)
                db_acc[:, lanes] += fold(da)
                db_acc[:, gate_lanes] += fold(dgt)

        @pl.when(i == nt - 1)
        def _():
            ddw_ref[...] = jnp.sum(ddw_acc[...], axis=1)
            db_ref[...] = jnp.sum(db_acc[...], axis=0, keepdims=True)

    row = pl.BlockSpec((t, d), lambda i: (i, 0))
    nxt = pl.BlockSpec((CONV_HALO, d), lambda i: (jnp.minimum((i + 1) * hb, s // CONV_HALO - 1), 0))
    prev = pl.BlockSpec((CONV_HALO, d), lambda i: (jnp.maximum(i * hb - 1, 0), 0))
    return pl.pallas_call(
        body,
        name=name,
        grid=(nt,),
        in_specs=[row, nxt, row, prev, row, row, pl.BlockSpec((w, d), lambda i: (0, 0))],
        out_specs=[
            pl.BlockSpec((t, 2 * d), lambda i: (i, 0)),
            pl.BlockSpec((w, d), lambda i: (0, 0)),
            pl.BlockSpec((1, 2 * d), lambda i: (0, 0)),
        ],
        out_shape=[
            jax.ShapeDtypeStruct((s, 2 * d), BF16),
            jax.ShapeDtypeStruct((w, d), F32),
            jax.ShapeDtypeStruct((1, 2 * d), F32),
        ],
        scratch_shapes=[pltpu.VMEM((t + CONV_HALO, d), F32), pltpu.VMEM((CONV_HALO + t, d), F32)]
        + [pltpu.VMEM((SUBLANES, CONV_HALO + t, lw), F32)] * 2
        + [pltpu.VMEM((w, SUBLANES, d), F32), pltpu.VMEM((SUBLANES, 2 * d), F32)],
        compiler_params=pltpu.CompilerParams(dimension_semantics=("arbitrary",)),
    )(dc, dc, u, u, a, gt, dw)


def _alibi_slopes(n_heads):
    h = jnp.arange(1, n_heads + 1, dtype=F32)
    return jnp.exp2(-8.0 * h / n_heads)


def _band_masks(bq):
    qi = lax.broadcasted_iota(jnp.int32, (bq, bq), 0)
    kj = lax.broadcasted_iota(jnp.int32, (bq, bq), 1)
    return qi - kj, qi - kj + bq


ATTN_GROUP = 16


def _attn_fwd(name, q, k, v, slopes, bq):
    s, dm = q.shape
    nh = dm // HEAD_DIM
    nt = s // bq
    nbr = len(BRANCHES)
    scale = HEAD_DIM ** -0.5
    nt_dims = (((1,), (1,)), ((), ()))

    def body(sl_ref, q_ref, k_ref, v_ref, o_ref, ob_ref, l_ref, tmp, qr, kr, va, orm, lrm, onat, lnat, sbuf, mbuf):
        slope = sl_ref[pl.program_id(0)]
        jc, jp = _band_masks(bq)
        va[:, HEAD_DIM:] = jnp.ones((s, HEAD_DIM), BF16)
        for bi, (win, dil) in enumerate(BRANCHES):
            ll = s // dil
            nblk = ll // bq
            if dil == 1:
                sq, sk = q_ref, k_ref
                va[:, :HEAD_DIM] = v_ref[...]
                d_o, d_l = onat.at[bi], lnat.at[bi]
            else:
                for src, dst, wide in ((q_ref, qr, False), (k_ref, kr, False), (v_ref, va, True)):
                    tmp[...] = src[...].astype(F32)
                    for r in range(dil):
                        part = tmp[pl.ds(r, ll, stride=dil), :].astype(BF16)
                        if wide:
                            dst[r * ll:(r + 1) * ll, :HEAD_DIM] = part
                        else:
                            dst[r * ll:(r + 1) * ll, :] = part
                sq, sk = qr, kr
                d_o, d_l = orm, lrm
            bias_c = jnp.where(jc >= 0, jc.astype(F32) * (slope * dil), 1e30)
            bias_p = jnp.where(jp <= bq, jp.astype(F32) * (slope * dil), 1e30)

            def group(gi, carry):
                rows = []
                for g in range(ATTN_GROUP):
                    ti = gi * ATTN_GROUP + g
                    row = pl.ds(pl.multiple_of(ti * bq, bq), bq)
                    prow = pl.ds(pl.multiple_of(jnp.maximum(ti - 1, 0) * bq, bq), bq)
                    rows.append((row, prow))
                    qh = sq[row, :]
                    sc = lax.dot_general(qh, sk[row, :], nt_dims, preferred_element_type=F32) * scale - bias_c
                    sp = lax.dot_general(qh, sk[prow, :], nt_dims, preferred_element_type=F32) * scale - bias_p
                    sp = jnp.where(lax.rem(ti, nblk) > 0, sp, -1e30)
                    sbuf[g, :, :bq] = sc
                    sbuf[g, :, bq:] = sp
                    mbuf[g] = jnp.maximum(jnp.max(sc, axis=-1, keepdims=True), jnp.max(sp, axis=-1, keepdims=True))
                for g, (row, prow) in enumerate(rows):
                    mx = mbuf[g]
                    p = jnp.exp(sbuf[g] - mx).astype(BF16)
                    ov = jnp.dot(p[:, :bq], va[row, :], preferred_element_type=F32)
                    ov = ov + jnp.dot(p[:, bq:], va[prow, :], preferred_element_type=F32)
                    den = ov[:, HEAD_DIM:]
                    d_o[row, :] = ov[:, :HEAD_DIM] / den
                    d_l[row, :] = mx + jnp.log(den)
                return carry

            lax.fori_loop(0, nt // ATTN_GROUP, group, 0)
            if dil > 1:
                for r in range(dil):
                    onat[bi, pl.ds(r, ll, stride=dil), :] = orm[r * ll:(r + 1) * ll, :]
                    lnat[bi, pl.ds(r, ll, stride=dil), :] = lrm[r * ll:(r + 1) * ll, :]

        def merge(ti, carry):
            rows = pl.ds(pl.multiple_of(ti * bq, bq), bq)
            ls = [lnat[bi, rows, :] for bi in range(nbr)]
            mx = functools.reduce(jnp.maximum, ls)
            es = [jnp.exp(l - mx) for l in ls]
            tot = functools.reduce(jnp.add, es)
            inv = 1.0 / tot
            o = functools.reduce(jnp.add, [e * inv * onat[bi, rows, :] for bi, e in enumerate(es)])
            o_ref[rows, :] = o
            ob_ref[rows, :] = o.astype(BF16)
            l_ref[rows, :] = mx + jnp.log(tot)
            return carry

        lax.fori_loop(0, nt, merge, 0)

    head = pl.BlockSpec((s, HEAD_DIM), lambda h: (0, h))
    return pl.pallas_call(
        body,
        name=name,
        grid=(nh,),
        in_specs=[pl.BlockSpec(memory_space=pltpu.SMEM), head, head, head],
        out_specs=[head, head, head],
        out_shape=[jax.ShapeDtypeStruct((s, dm), F32), jax.ShapeDtypeStruct((s, dm), BF16),
                   jax.ShapeDtypeStruct((s, dm), F32)],
        scratch_shapes=[pltpu.VMEM((s, HEAD_DIM), F32)] + [pltpu.VMEM((s, HEAD_DIM), BF16)] * 2
        + [pltpu.VMEM((s, 2 * HEAD_DIM), BF16)] + [pltpu.VMEM((s, HEAD_DIM), F32)] * 2
        + [pltpu.VMEM((nbr, s, HEAD_DIM), F32)] * 2
        + [pltpu.VMEM((ATTN_GROUP, bq, 2 * bq), F32), pltpu.VMEM((ATTN_GROUP, bq, 1), F32)],
        compiler_params=pltpu.CompilerParams(dimension_semantics=("parallel",)),
    )(slopes, q, k, v)


def _attn_bwd(name, q, k, v, o, lse, do, slopes, bq):
    s, dm = q.shape
    nh = dm // HEAD_DIM
    nt = s // bq
    scale = HEAD_DIM ** -0.5
    nt_dims = (((1,), (1,)), ((), ()))
    tn_dims = (((0,), (0,)), ((), ()))

    def body(sl_ref, q_ref, k_ref, v_ref, o_ref, l_ref, do_ref, dq_ref, dk_ref, dv_ref,
             tmp, qr, kr, vr, dor, rown, rowr, dqr, dkr, dvr, aq, ak, av, pbuf, dsbuf):
        slope = sl_ref[pl.program_id(0)]
        jc, jp = _band_masks(bq)
        lane = lax.broadcasted_iota(jnp.int32, (bq, HEAD_DIM), 1)

        def row_terms(ti, carry):
            rows = pl.ds(pl.multiple_of(ti * bq, bq), bq)
            dl = jnp.sum(do_ref[rows, :].astype(F32) * o_ref[rows, :], axis=-1, keepdims=True)
            rown[rows, :] = jnp.where(lane == 0, l_ref[rows, :], dl)
            return carry

        lax.fori_loop(0, nt, row_terms, 0)

        for bi, (win, dil) in enumerate(BRANCHES):
            ll = s // dil
            nblk = ll // bq
            if dil == 1:
                sq, sk, sv, sdo, srow = q_ref, k_ref, v_ref, do_ref, rown
                gq, gk, gv = aq, ak, av
            else:
                for src, dst in ((q_ref, qr), (k_ref, kr), (v_ref, vr), (do_ref, dor)):
                    tmp[...] = src[...].astype(F32)
                    for r in range(dil):
                        dst[r * ll:(r + 1) * ll, :] = tmp[pl.ds(r, ll, stride=dil), :].astype(BF16)
                for r in range(dil):
                    rowr[r * ll:(r + 1) * ll, :] = rown[pl.ds(r, ll, stride=dil), :]
                sq, sk, sv, sdo, srow = qr, kr, vr, dor, rowr
                gq, gk, gv = dqr, dkr, dvr
            bias_c = jnp.where(jc >= 0, jc.astype(F32) * (slope * dil), 1e30)
            bias_p = jnp.where(jp <= bq, jp.astype(F32) * (slope * dil), 1e30)

            def group(gi, carry):
                rows = []
                for g in range(ATTN_GROUP):
                    ti = gi * ATTN_GROUP + g
                    row = pl.ds(pl.multiple_of(ti * bq, bq), bq)
                    prow = pl.ds(pl.multiple_of(jnp.maximum(ti - 1, 0) * bq, bq), bq)
                    rows.append((row, prow))
                    has_prev = lax.rem(ti, nblk) > 0
                    qh, doh = sq[row, :], sdo[row, :]
                    terms = srow[row, :]
                    lc = terms[:, 0:1]
                    dl = terms[:, 1:2]
                    for half, kv_rows, bias in ((0, row, bias_c), (1, prow, bias_p)):
                        sc = lax.dot_general(qh, sk[kv_rows, :], nt_dims, preferred_element_type=F32) * scale - bias
                        if half:
                            sc = jnp.where(has_prev, sc, -1e30)
                        p = jnp.exp(sc - lc)
                        dp = lax.dot_general(doh, sv[kv_rows, :], nt_dims, preferred_element_type=F32)
                        pbuf[g, :, half * bq:(half + 1) * bq] = p.astype(BF16)
                        dsbuf[g, :, half * bq:(half + 1) * bq] = (p * (dp - dl) * scale).astype(BF16)
                carry_k = carry_v = None
                for g, (row, prow) in enumerate(rows):
                    qh, doh = sq[row, :], sdo[row, :]
                    ds_c, ds_p = dsbuf[g, :, :bq], dsbuf[g, :, bq:]
                    p_c, p_p = pbuf[g, :, :bq], pbuf[g, :, bq:]
                    dq = jnp.dot(ds_c, sk[row, :], preferred_element_type=F32)
                    dq = dq + jnp.dot(ds_p, sk[prow, :], preferred_element_type=F32)
                    gq[row, :] = dq
                    dk_p = lax.dot_general(ds_p, qh, tn_dims, preferred_element_type=F32)
                    dv_p = lax.dot_general(p_p, doh, tn_dims, preferred_element_type=F32)
                    if g == 0:
                        @pl.when(gi > 0)
                        def _():
                            gk[prow, :] += dk_p
                            gv[prow, :] += dv_p
                    else:
                        gk[rows[g - 1][0], :] = carry_k + dk_p
                        gv[rows[g - 1][0], :] = carry_v + dv_p
                    carry_k = lax.dot_general(ds_c, qh, tn_dims, preferred_element_type=F32)
                    carry_v = lax.dot_general(p_c, doh, tn_dims, preferred_element_type=F32)
                gk[rows[-1][0], :] = carry_k
                gv[rows[-1][0], :] = carry_v
                return carry

            lax.fori_loop(0, nt // ATTN_GROUP, group, 0)
            if dil > 1:
                for acc, rm in ((aq, dqr), (ak, dkr), (av, dvr)):
                    for r in range(dil):
                        acc[pl.ds(r, ll, stride=dil), :] += rm[r * ll:(r + 1) * ll, :]

        dq_ref[...] = aq[...].astype(BF16)
        dk_ref[...] = ak[...].astype(BF16)
        dv_ref[...] = av[...].astype(BF16)

    head = pl.BlockSpec((s, HEAD_DIM), lambda h: (0, h))
    f32buf = pltpu.VMEM((s, HEAD_DIM), F32)
    b16buf = pltpu.VMEM((s, HEAD_DIM), BF16)
    return pl.pallas_call(
        body,
        name=name,
        grid=(nh,),
        in_specs=[pl.BlockSpec(memory_space=pltpu.SMEM)] + [head] * 6,
        out_specs=[head] * 3,
        out_shape=[jax.ShapeDtypeStruct((s, dm), BF16)] * 3,
        scratch_shapes=[f32buf] + [b16buf] * 4 + [f32buf] * 8 + [pltpu.VMEM((ATTN_GROUP, bq, 2 * bq), BF16)] * 2,
        compiler_params=pltpu.CompilerParams(dimension_semantics=("parallel",)),
    )(slopes, q, k, v, o, lse, do)


def _ep_id(accs, ex):
    return [accs[0]]


def _ep_all(accs, ex):
    return list(accs)


def _ep_sum(accs, ex):
    return [accs[0] + accs[1]]


def _ep_add(accs, ex):
    return [accs[0] + ex[0].astype(F32)]


def _ep_bias_res(accs, ex):
    return [accs[0] + ex[0] + ex[1]]


def _ep_glu(accs, ex):
    a = accs[0] + ex[0]
    gt = accs[1] + ex[1]
    return [a * _sigmoid(gt), a, gt]


def _ep_swiglu(accs, ex):
    g, u = accs
    return [g, u, g * _sigmoid(g) * u]


def _ep_swiglu_bwd(accs, ex):
    dact = accs[0]
    g = ex[0].astype(F32)
    u = ex[1].astype(F32)
    sg = _sigmoid(g)
    return [dact * u * (sg * (1.0 + g * (1.0 - sg))), dact * g * sg]


def _ffn_fwd(tag, h, gain, get_gate_up, get_down):
    s, d = h.shape
    (n,) = _rms_fwd(f"{tag}_norm", h, gain)
    wg, wu = get_gate_up(n)
    f = wg.shape[-1]
    g, u, act = _mm(f"{tag}_gate_up", "nn", n, [(wg, None, 0), (wu, None, 0)], _ep_swiglu,
                    [(BF16,), (BF16,), (BF16,)], m=s, n=f, k=d)
    wd = get_down(act)
    (out,) = _mm(f"{tag}_down", "nn", act, [(wd, None, 0)], _ep_add, [(F32,)], m=s, n=d, k=f,
                 extras=[(h, "mn", 0)])
    return out, (n, g, u, act), dict(gate=wg, up=wu, down=wd)


def _ffn_bwd(tag, h_in, gain, wg, wu, wd, saved, dh, dhb, after, emit):
    s, d = h_in.shape
    f = wg.shape[-1]
    n, g, u, act = saved
    dg, du = _mm(f"{tag}_bwd_dact", "nt", dhb, [(wd, None, 0)], _ep_swiglu_bwd, [(BF16,), (BF16,)],
                 m=s, n=f, k=d, extras=[(g, "mn", 0), (u, "mn", 0)], after=after)
    (dwd,) = _mm(f"{tag}_bwd_dwd", "tn", act, [(dhb, None, 0)], _ep_id, [(BF16,)], m=f, n=d, k=s,
                 bm=f // 4)
    pin = emit("d", dict(down=dwd))
    dwg, dwu = _mm(f"{tag}_bwd_dwgu", "tn", n, [(dg, None, 0), (du, None, 0)], _ep_all, [(BF16,), (BF16,)],
                   m=d, n=f, k=s, after=pin)
    pin = emit("gu", dict(gate=dwg, up=dwu))
    (dn,) = _mm(f"{tag}_bwd_dn", "nt", [dg, du], [(wg, None, 0), (wu, None, 0)], _ep_sum, [(BF16,)],
                m=s, n=d, k=f, bm=512, bn=256, after=pin)
    dx, dxb, dgain, cs = _rms_bwd(f"{tag}_bwd_norm", h_in, gain, [dn], dh)
    return dx, dxb, dgain, cs


def _local_step(x, target, w, fetch, emit, prefetch=lambda group, after: None, after=()):
    s, d = x.shape
    nh = d // HEAD_DIM
    bq = BRANCHES[0][0] // BRANCHES[0][1]
    assert all(win // dil == bq for win, dil in BRANCHES)
    assert BRANCHES[0][1] == 1 and all(dil > 1 for _, dil in BRANCHES[1:])
    slopes = _alibi_slopes(nh)
    nd = d // 512 if d >= 512 else 1
    bn = d // nd

    (n1,) = _rms_fwd("a_norm", x, w["a_norm_g"], after=after)
    prefetch("rest", n1)
    w_conv1 = fetch("conv1", n1)["conv_w1"]
    glu, a, gt = _mm("conv_pw1_glu", "nn", n1, [(w_conv1, None, 0), (w_conv1, None, nd)], _ep_glu,
                     [(F32,), (BF16,), (BF16,)], m=s, n=d, k=d, bn=bn,
                     extras=[(w["conv_b1"], "n", 0), (w["conv_b1"], "n", nd)])
    c, sw = _dwconv_fwd("conv_dw_ln", glu, w["conv_dw"], w["conv_dw_b"], w["conv_ln_g"], w["conv_ln_b"])
    w_conv2 = fetch("conv2", sw)["conv_w2"]
    (h1,) = _mm("conv_pw2", "nn", sw, [(w_conv2, None, 0)], _ep_bias_res, [(F32,)], m=s, n=d, k=d,
                extras=[(w["conv_b2"], "n", 0), (x, "mn", 0)], bn=WIDE_BN)
    def gate_up0(after):
        wts = fetch("ffn0gu", after)
        return wts["gate"], wts["up"]

    h2, ffn0, wf0 = _ffn_fwd("ffn0", h1, w["ffn_norm_g"][0:1], gate_up0, lambda after: fetch("ffn0d", after)["down"])
    wa = fetch("attn", h2)
    kvn, qn = _rms_fwd("kvq_norm", h2, jnp.concatenate([w["kv_norm_g"], w["b_norm_g"]], axis=0))
    k, v = _mm("kv_proj", "nn", kvn, [(wa["w_k"], None, 0), (wa["w_v"], None, 0)], _ep_all, [(BF16,), (BF16,)],
               m=s, n=d, k=d, bn=WIDE_BN)
    (q,) = _mm("q_proj", "nn", qn, [(wa["w_q"], None, 0)], _ep_id, [(BF16,)], m=s, n=d, k=d, bn=WIDE_BN)
    att, attb, lse = _attn_fwd("attn_fwd", q, k, v, slopes, bq)
    prefetch("ffn1", attb)
    (h3,) = _mm("o_proj", "nn", attb, [(wa["w_o"], None, 0)], _ep_add, [(F32,)], m=s, n=d, k=d,
                extras=[(h2, "mn", 0)], bn=WIDE_BN)
    got1 = {}

    def gate_up1(after):
        got1.update(fetch("ffn1", after))
        return got1["gate"], got1["up"]

    h4, ffn1, wf1 = _ffn_fwd("ffn1", h3, w["ffn_norm_g"][1:2], gate_up1, lambda after: got1["down"])
    dh4, dh4b, d_final_g, loss_cols = _final_loss("final_loss", h4, w["final_norm_g"], target)

    g = {}
    ga = {}
    dh3, dh3b, dgain1, _ = _ffn_bwd("ffn1", h3, w["ffn_norm_g"][1:2], wf1["gate"], wf1["up"], wf1["down"], ffn1,
                                    dh4, dh4b, (), lambda part, grads: emit("ffn1" + part, grads))
    (datt,) = _mm("o_proj_bwd_dx", "nt", dh3b, [(wa["w_o"], None, 0)], _ep_id, [(BF16,)], m=s, n=d, k=d, bn=WIDE_BN)
    (ga["w_o"],) = _mm("o_proj_bwd_dw", "tn", attb, [(dh3b, None, 0)], _ep_id, [(BF16,)], m=d, n=d, k=s, bn=WIDE_BN)
    dq, dk, dv = _attn_bwd("attn_bwd", q, k, v, att, lse, datt, slopes, bq)
    (ga["w_q"],) = _mm("q_proj_bwd_dw", "tn", qn, [(dq, None, 0)], _ep_id, [(BF16,)], m=d, n=d, k=s, bn=WIDE_BN)
    ga["w_k"], ga["w_v"] = _mm("kv_proj_bwd_dw", "tn", kvn, [(dk, None, 0), (dv, None, 0)], _ep_all,
                               [(BF16,), (BF16,)], m=d, n=d, k=s)
    pin = emit("attn", ga)
    (dqn,) = _mm("q_proj_bwd_dx", "nt", dq, [(wa["w_q"], None, 0)], _ep_id, [(BF16,)], m=s, n=d, k=d, after=pin,
                 bn=WIDE_BN)
    (dkvn,) = _mm("kv_proj_bwd_dx", "nt", [dk, dv], [(wa["w_k"], None, 0), (wa["w_v"], None, 0)], _ep_sum, [(BF16,)],
                  m=s, n=d, k=d, bn=WIDE_BN)
    dh2, dh2b, dg_kvq, _ = _rms_bwd("kvq_norm_bwd", h2, jnp.concatenate([w["kv_norm_g"], w["b_norm_g"]], axis=0),
                                    [dkvn, dqn], dh3)
    dh1, dh1b, dgain0, cs_h1 = _ffn_bwd("ffn0", h1, w["ffn_norm_g"][0:1], wf0["gate"], wf0["up"], wf0["down"], ffn0,
                                        dh2, dh2b, (), lambda part, grads: emit("ffn0" + part, grads))
    (dsw,) = _mm("conv_pw2_bwd_dx", "nt", dh1b, [(w_conv2, None, 0)], _ep_id, [(F32,)], m=s, n=d, k=d, bn=WIDE_BN)
    (dw2,) = _mm("conv_pw2_bwd_dw", "tn", sw, [(dh1b, None, 0)], _ep_id, [(BF16,)], m=d, n=d, k=s, bn=WIDE_BN)
    pin = emit("conv2", dict(conv_w2=dw2))
    dc, ln_sums = _conv_ln_bwd("conv_ln_bwd", c, dsw, w["conv_ln_g"], w["conv_ln_b"], after=pin)
    dpre, ddw, db1 = _conv_dw_bwd("conv_dw_bwd", dc, glu, a, gt, w["conv_dw"])
    (dw1,) = _mm("conv_pw1_bwd_dw", "tn", n1, [(dpre, None, 0)], _ep_id, [(BF16,)], m=d, n=2 * d, k=s)
    pin = emit("conv1", dict(conv_w1=dw1))
    (dn1,) = _mm("conv_pw1_bwd_dx", "nt", dpre, [(w_conv1, None, 0)], _ep_id, [(BF16,)], m=s, n=d, k=2 * d,
                 after=pin)
    dx, _, d_a_norm, _ = _rms_bwd("a_norm_bwd", x, w["a_norm_g"], [dn1], dh1)

    g.update(
        a_norm_g=d_a_norm, conv_b1=db1, conv_dw=ddw, conv_dw_b=ln_sums[2:3], conv_ln_g=ln_sums[0:1],
        conv_ln_b=ln_sums[1:2], conv_b2=cs_h1, kv_norm_g=dg_kvq[0:1], b_norm_g=dg_kvq[1:2],
        ffn_norm_g=jnp.concatenate([dgain0, dgain1], axis=0), final_norm_g=d_final_g,
    )
    return loss_cols, dx, g


HBM_SPEC = pl.BlockSpec(memory_space=pltpu.HBM)


def _mesh_place():
    x, y, c = lax.axis_index("x"), lax.axis_index("y"), lax.axis_index("c")
    chips = [(1 - x, y), (x, 1 - y), (1 - x, 1 - y)]
    return x, y, c, chips


def _shard_view(ref, kind, s, half=None):
    rows, cols = ref.shape
    if kind == "col":
        cw = cols // N_CHIPS
        if half is None:
            return ref.at[pl.ds(0, rows), pl.ds(s * cw, cw)]
        return ref.at[pl.ds(half * (rows // 2), rows // 2), pl.ds(s * cw, cw)]
    r = rows // N_CHIPS
    if half is None:
        return ref.at[pl.ds(s * r, r), pl.ds(0, cols)]
    return ref.at[pl.ds(s * r + half * (r // 2), r // 2), pl.ds(0, cols)]


def _full_shape(shard, kind):
    r, cw = shard.shape[-2:]
    return (r, cw * N_CHIPS) if kind == "col" else (r * N_CHIPS, cw)


def _gather_weights(shards, kinds, after=()):
    nt = len(shards)
    nf = len(after)
    fulls = [jax.ShapeDtypeStruct(_full_shape(sh, kind), sh.dtype) for sh, kind in zip(shards, kinds)]

    def body(*refs):
        src = refs[:nt]
        dst = refs[nt + nf:2 * nt + nf]
        send, recv, fsend, frecv, local = refs[2 * nt + nf:]
        x, y, c, chips = _mesh_place()
        s = 2 * x + y
        sib = (x, y, 1 - c)

        def half_of_shard(t):
            r, cw = src[t].shape
            return src[t].at[pl.ds(c * (r // 2), r // 2), pl.ds(0, cw)]

        locals_ = [pltpu.make_async_copy(src[t], _shard_view(dst[t], kinds[t], s), local.at[t]) for t in range(nt)]
        for cp in locals_:
            cp.start()
        sends = []
        for t in range(nt):
            for j, chip in enumerate(chips):
                cp = pltpu.make_async_remote_copy(
                    src_ref=half_of_shard(t), dst_ref=_shard_view(dst[t], kinds[t], s, c),
                    send_sem=send.at[t, j], recv_sem=recv.at[t, j], device_id=(*chip, c), device_id_type=MESH)
                cp.start()
                sends.append(cp)
        for t in range(nt):
            for j, (px, py) in enumerate(chips):
                landed = _shard_view(dst[t], kinds[t], 2 * px + py, c)
                pltpu.make_async_remote_copy(
                    src_ref=half_of_shard(t), dst_ref=landed, send_sem=send.at[t, j], recv_sem=recv.at[t, j],
                    device_id=(px, py, c), device_id_type=MESH).wait_recv()
                cp = pltpu.make_async_remote_copy(
                    src_ref=landed, dst_ref=landed, send_sem=fsend.at[t, j], recv_sem=frecv.at[t, j],
                    device_id=sib, device_id_type=MESH)
                cp.start()
                sends.append(cp)
        for t in range(nt):
            for j, (px, py) in enumerate(chips):
                other = _shard_view(dst[t], kinds[t], 2 * px + py, 1 - c)
                pltpu.make_async_remote_copy(
                    src_ref=other, dst_ref=other, send_sem=fsend.at[t, j], recv_sem=frecv.at[t, j],
                    device_id=sib, device_id_type=MESH).wait_recv()
        for cp in sends:
            cp.wait_send()
        for cp in locals_:
            cp.wait()

    return pl.pallas_call(
        body,
        name="gather_weights",
        in_specs=[HBM_SPEC] * nt + [ANY_SPEC] * nf,
        out_specs=[HBM_SPEC] * nt,
        out_shape=fulls,
        scratch_shapes=[pltpu.SemaphoreType.DMA((nt, 3))] * 4 + [pltpu.SemaphoreType.DMA((nt,))],
    )(*shards, *after)


def _row_blocks(rows, want=512):
    nb = 1
    while rows // nb > want or rows % nb or (rows // nb) % 16:
        nb += 1
        if nb > rows:
            return rows, 1
    return rows // nb, nb


def _place_own(name, src, kind, land, ids, lead=0):
    r, cw = src.shape[-2:]
    tr, nb = _row_blocks(r)
    if src.ndim == 3:
        src_spec = pl.BlockSpec((None, tr, cw), lambda i, ids_ref: (lead, i, 0))
    else:
        src_spec = pl.BlockSpec((tr, cw), lambda i, ids_ref: (i, 0))
    if kind == "col":
        dst_spec = pl.BlockSpec((tr, cw), lambda i, ids_ref: (i, ids_ref[0]))
    else:
        dst_spec = pl.BlockSpec((tr, cw), lambda i, ids_ref: (ids_ref[0] * nb + i, 0))

    def body(ids_ref, s_ref, o_ref):
        o_ref[...] = s_ref[...].astype(o_ref.dtype)

    return pl.pallas_call(
        body,
        name=name,
        grid_spec=pltpu.PrefetchScalarGridSpec(num_scalar_prefetch=1, grid=(nb,), in_specs=[src_spec],
                                               out_specs=dst_spec),
        out_shape=land,
        compiler_params=pltpu.CompilerParams(dimension_semantics=("parallel",)),
    )(ids, src)


SEM_SPEC = pl.BlockSpec(memory_space=pltpu.SEMAPHORE)
SIDE_EFFECT = pltpu.SideEffectType.DATAFLOW_SIDE_EFFECTING


def _copies_start(name, srcs, lands, copies, n_sems, after):
    ns, nl, nf = len(srcs), len(lands), len(after)

    def body(*refs):
        src, land = refs[:ns], refs[ns:ns + nl]
        send, recv = refs[ns + nl + nf], refs[ns + nl + nf + 1]
        pin = refs[-1]
        for cp in copies(src, land, send, recv, _mesh_place()):
            cp.start()
        pin[...] = jnp.zeros_like(pin)

    arrs = list(srcs) + list(lands)
    res = pl.pallas_call(
        body,
        name=name,
        in_specs=[HBM_SPEC] * (ns + nl) + [ANY_SPEC] * nf,
        out_specs=[SEM_SPEC, SEM_SPEC] + [HBM_SPEC] * (ns + nl) + [pl.BlockSpec(memory_space=pltpu.VMEM)],
        out_shape=[pltpu.SemaphoreType.DMA((n_sems,)), pltpu.SemaphoreType.DMA((n_sems,))]
        + [pltpu.HBM(a.shape, a.dtype) for a in arrs] + [jax.ShapeDtypeStruct((8, 128), F32)],
        input_output_aliases={i: 2 + i for i in range(ns + nl)},
        compiler_params=pltpu.CompilerParams(has_side_effects=SIDE_EFFECT),
    )(*[pltpu.with_memory_space_constraint(a, pltpu.HBM) for a in arrs], *after)
    return res[0], res[1], list(res[2:2 + ns]), list(res[2 + ns:2 + ns + nl]), res[-1]


def _copies_wait(name, send, recv, srcs, lands, copies, after):
    ns, nl, nf = len(srcs), len(lands), len(after)

    def body(*refs):
        src, land = refs[:ns], refs[ns:ns + nl]
        send_sems, recv_sems = refs[ns + nl], refs[ns + nl + 1]
        cps = copies(src, land, send_sems, recv_sems, _mesh_place())
        for cp in cps:
            cp.wait_send()
        for cp in cps:
            cp.wait_recv()

    arrs = list(srcs) + list(lands)
    res = pl.pallas_call(
        body,
        name=name,
        in_specs=[HBM_SPEC] * (ns + nl) + [SEM_SPEC, SEM_SPEC] + [ANY_SPEC] * nf,
        out_specs=[HBM_SPEC] * (ns + nl),
        out_shape=[pltpu.HBM(a.shape, a.dtype) for a in arrs],
        input_output_aliases={i: i for i in range(ns + nl)},
        compiler_params=pltpu.CompilerParams(has_side_effects=SIDE_EFFECT),
    )(*arrs, send, recv, *after)
    return list(res[:ns]), list(res[ns:])


def _gather_copies(kinds, halves):
    def copies(src, land, send, recv, place):
        x, y, c, chips = place
        s = 2 * x + y
        mine = [_shard_view(land[t], kinds[t], s, c if halves else None) for t in range(len(kinds))]
        return [
            pltpu.make_async_remote_copy(
                src_ref=mine[t], dst_ref=mine[t], send_sem=send.at[3 * t + j], recv_sem=recv.at[3 * t + j],
                device_id=(px, py, c), device_id_type=MESH)
            for t in range(len(kinds)) for j, (px, py) in enumerate(chips)
        ]

    return copies


def _gather_pass_copies(kinds):
    def copies(src, land, send, recv, place):
        x, y, c, chips = place
        views = [_shard_view(land[t], kinds[t], 2 * px + py, c) for t in range(len(kinds)) for px, py in chips]
        return [
            pltpu.make_async_remote_copy(src_ref=v, dst_ref=v, send_sem=send.at[i], recv_sem=recv.at[i],
                                         device_id=(x, y, 1 - c), device_id_type=MESH)
            for i, v in enumerate(views)
        ]

    return copies


def _gather_pass_on(name, lands, kinds):
    nt = len(lands)

    def body(*refs):
        buf = refs[nt:2 * nt]
        send, recv = refs[2 * nt:]
        x, y, c, chips = _mesh_place()
        sib = (x, y, 1 - c)
        sends = []
        for t in range(nt):
            for j, (px, py) in enumerate(chips):
                mine = _shard_view(buf[t], kinds[t], 2 * px + py, c)
                cp = pltpu.make_async_remote_copy(src_ref=mine, dst_ref=mine, send_sem=send.at[t, j],
                                                  recv_sem=recv.at[t, j], device_id=sib, device_id_type=MESH)
                cp.start()
                sends.append(cp)
        for t in range(nt):
            for j, (px, py) in enumerate(chips):
                theirs = _shard_view(buf[t], kinds[t], 2 * px + py, 1 - c)
                pltpu.make_async_remote_copy(src_ref=theirs, dst_ref=theirs, send_sem=send.at[t, j],
                                             recv_sem=recv.at[t, j], device_id=sib, device_id_type=MESH).wait_recv()
        for cp in sends:
            cp.wait_send()

    return pl.pallas_call(
        body,
        name=name,
        in_specs=[HBM_SPEC] * nt,
        out_specs=[HBM_SPEC] * nt,
        out_shape=[jax.ShapeDtypeStruct(a.shape, a.dtype) for a in lands],
        input_output_aliases={i: i for i in range(nt)},
        scratch_shapes=[pltpu.SemaphoreType.DMA((nt, 3))] * 2,
    )(*lands)


def _grad_part(ref, kind, s):
    return ref if kind == "all" else _shard_view(ref, kind, s)


def _grad_copies(kinds):
    def peers(place):
        x, y, c, chips = place
        return [(x, y, 1 - c)] + [(px, py, c) for px, py in chips]

    def copies(src, land, send, recv, place):
        x, y, c, chips = place
        me = 4 * x + 2 * y + c
        return [
            pltpu.make_async_remote_copy(
                src_ref=_grad_part(src[t], kinds[t], 2 * px + py), dst_ref=land[t].at[me],
                send_sem=send.at[GRAD_PEERS * t + k], recv_sem=recv.at[GRAD_PEERS * t + k], device_id=(px, py, pc),
                device_id_type=MESH)
            for t in range(len(kinds)) for k, (px, py, pc) in enumerate(peers(place))
        ]

    return copies


def _pass_copies(n):
    def copies(src, land, send, recv, place):
        x, y, c, chips = place
        return [
            pltpu.make_async_remote_copy(
                src_ref=land[t].at[4 * px + 2 * py + c], dst_ref=land[t].at[4 * px + 2 * py + c],
                send_sem=send.at[3 * t + j], recv_sem=recv.at[3 * t + j], device_id=(x, y, 1 - c),
                device_id_type=MESH)
            for t in range(n) for j, (px, py) in enumerate(chips)
        ]

    return copies


GRAD_PEERS = 4


def _land_shape(grad, kind):
    rows, cols = grad.shape
    if kind == "col":
        return (N_DEV, rows, cols // N_CHIPS)
    if kind == "row":
        return (N_DEV, rows // N_CHIPS, cols)
    return (N_DEV, rows, cols)


def _adamw_reduce(name, contrib, own, kind, ids, w, m, v, layer=None, prev=None):
    rows, cols = w.shape[-2:]
    t = rows
    for cand in (256, 128):
        if rows % cand == 0 and cand * cols <= 256 * 1408:
            t = cand
            break
    nb = rows // t
    c1 = 1.0 - ADAM_B1 ** ADAM_STEP
    c2 = 1.0 - ADAM_B2 ** ADAM_STEP

    n_prev = 0 if prev is None else 4

    def body(ids_ref, c_ref, own_ref, w_ref, m_ref, v_ref, *refs):
        g_ref, d_ref, nm_ref, nv_ref = refs[n_prev:]
        me = ids_ref[1]
        mine = own_ref[...].astype(F32)
        g = None
        for q in range(N_DEV):
            term = jnp.where(me == q, mine, c_ref[q].astype(F32))
            g = term if g is None else g + term
        nm = ADAM_B1 * m_ref[...] + (1.0 - ADAM_B1) * g
        nv = ADAM_B2 * v_ref[...] + (1.0 - ADAM_B2) * (g * g)
        g_ref[...] = g
        nm_ref[...] = nm
        nv_ref[...] = nv
        d_ref[...] = -ADAM_LR * ((nm / c1) / (jnp.sqrt(nv / c2) + ADAM_EPS) + ADAM_WD * w_ref[...])

    if layer is None:
        blk = pl.BlockSpec((t, cols), lambda i, ids_ref: (i, 0))
    else:
        blk = pl.BlockSpec((None, t, cols), lambda i, ids_ref: (layer, i, 0))
    if kind == "col":
        own_spec = pl.BlockSpec((t, cols), lambda i, ids_ref: (i, ids_ref[0]))
    elif kind == "row":
        own_spec = pl.BlockSpec((t, cols), lambda i, ids_ref: (ids_ref[0] * nb + i, 0))
    else:
        own_spec = pl.BlockSpec((t, cols), lambda i, ids_ref: (i, 0))
    return pl.pallas_call(
        body,
        name=name,
        grid_spec=pltpu.PrefetchScalarGridSpec(
            num_scalar_prefetch=1,
            grid=(nb,),
            in_specs=[pl.BlockSpec((N_DEV, t, cols), lambda i, ids_ref: (0, i, 0)), own_spec, blk, blk, blk]
            + [ANY_SPEC] * n_prev,
            out_specs=[blk] * 4,
        ),
        out_shape=[jax.ShapeDtypeStruct(w.shape, F32)] * 4,
        input_output_aliases={6 + i: i for i in range(n_prev)},
        compiler_params=pltpu.CompilerParams(dimension_semantics=("parallel",)),
    )(ids, contrib, own, w, m, v, *(prev or ()))


WEIGHT_NAMES = ("a_norm_g", "conv_w1", "conv_b1", "conv_dw", "conv_dw_b", "conv_ln_g", "conv_ln_b", "conv_w2",
                "conv_b2", "kv_norm_g", "w_k", "w_v", "b_norm_g", "w_q", "w_o", "ffn_norm_g", "ffn_w_gate",
                "ffn_w_up", "ffn_w_down", "final_norm_g")
GROUPS = {
    "conv2": (("conv_w2", "conv_w2", None, "row"),),
    "ffn0": (("gate", "ffn_w_gate", 0, "col"), ("up", "ffn_w_up", 0, "col"), ("down", "ffn_w_down", 0, "row")),
    "attn": (("w_k", "w_k", None, "row"), ("w_v", "w_v", None, "row"), ("w_q", "w_q", None, "row"),
             ("w_o", "w_o", None, "row")),
    "ffn1": (("gate", "ffn_w_gate", 1, "col"), ("up", "ffn_w_up", 1, "col"), ("down", "ffn_w_down", 1, "row")),
}
GROUPS["conv1"] = (("conv_w1", "conv_w1", None, "col"),)
for _layer in (0, 1):
    GROUPS[f"ffn{_layer}d"] = tuple(it for it in GROUPS[f"ffn{_layer}"] if it[0] == "down")
    GROUPS[f"ffn{_layer}gu"] = tuple(it for it in GROUPS[f"ffn{_layer}"] if it[0] != "down")
FETCH_ORDER = ("conv1", "conv2", "ffn0gu", "ffn0d", "attn", "ffn1")
HALVED = ("conv1", "conv2", "ffn0gu", "ffn0d", "attn", "ffn1")
EMIT_ORDER = ("ffn1d", "ffn1gu", "attn", "ffn0d", "ffn0gu", "conv2", "conv1", "vec")
RETIRE_AT = {"attn": ("ffn1d", "ffn1gu"), "ffn0d": ("attn",), "conv1": ("ffn0d", "ffn0gu", "conv2")}
PACKED = (("a_norm_g", 0, 1), ("conv_b1", 8, 2), ("conv_dw", 16, CONV_WIDTH), ("conv_dw_b", 48, 1),
          ("conv_ln_g", 56, 1), ("conv_ln_b", 64, 1), ("conv_b2", 72, 1))
PACK_ROWS = 80
WHOLE = (("kv_norm_g", 0, 1), ("b_norm_g", 1, 1), ("ffn_norm_g", 2, 2), ("final_norm_g", 4, 1))
WHOLE_ROWS = 8


def _pack_rows(parts, total, width):
    out, at = [], 0
    for arr, first in parts:
        if first > at:
            out.append(jnp.zeros((first - at, width), F32))
        rows8 = -(-arr.shape[0] // 8) * 8
        out.append(jnp.pad(arr, ((0, rows8 - arr.shape[0]), (0, 0))))
        at = first + rows8
    if total > at:
        out.append(jnp.zeros((total - at, width), F32))
    return jnp.concatenate(out, axis=0)


def kernel(x, a_norm_g, conv_w1, conv_b1, conv_dw, conv_dw_b, conv_ln_g, conv_ln_b, conv_w2, conv_b2, kv_norm_g, w_k, w_v, b_norm_g, w_q, w_o, ffn_norm_g, ffn_w_gate, ffn_w_up, ffn_w_down, final_norm_g, loss_target, m_a_norm_g, m_conv_w1, m_conv_b1, m_conv_dw, m_conv_dw_b, m_conv_ln_g, m_conv_ln_b, m_conv_w2, m_conv_b2, m_kv_norm_g, m_w_k, m_w_v, m_b_norm_g, m_w_q, m_w_o, m_ffn_norm_g, m_ffn_w_gate, m_ffn_w_up, m_ffn_w_down, m_final_norm_g, v_a_norm_g, v_conv_w1, v_conv_b1, v_conv_dw, v_conv_dw_b, v_conv_ln_g, v_conv_ln_b, v_conv_w2, v_conv_b2, v_kv_norm_g, v_w_k, v_w_v, v_b_norm_g, v_w_q, v_w_o, v_ffn_norm_g, v_ffn_w_gate, v_ffn_w_up, v_ffn_w_down, v_final_norm_g):
    args = locals()
    wts = {n: args[n] for n in WEIGHT_NAMES}
    mom = {n: args["m_" + n] for n in WEIGHT_NAMES}
    vel = {n: args["v_" + n] for n in WEIGHT_NAMES}
    s, d = x.shape[-2:]
    dq = d // N_CHIPS
    x2 = x.reshape(s, d)
    tgt = loss_target.reshape(s, d)

    def pack_shard(src):
        return _pack_rows([(src[n].reshape(-1, dq), first) for n, first, _ in PACKED], PACK_ROWS, dq)

    def pack_whole(src):
        return _pack_rows([(jnp.concatenate([src[n].reshape(-1, d) for n, _, _ in WHOLE], axis=0), 0)], WHOLE_ROWS, d)

    gathers = {}
    pins = []
    ids = jnp.stack([2 * lax.axis_index("x") + lax.axis_index("y"),
                     4 * lax.axis_index("x") + 2 * lax.axis_index("y") + lax.axis_index("c")]).astype(jnp.int32)

    def start_gather(grp, pins):
        kinds = [kind for _, _, _, kind in GROUPS[grp]]
        lands = []
        for key, n, layer, kind in GROUPS[grp]:
            shard = wts[n] if layer is not None else wts[n].reshape(wts[n].shape[-2:])
            full = jax.ShapeDtypeStruct(_full_shape(shard, kind), BF16)
            lands.append(_place_own(f"gather_place_{grp}_{key}", shard, kind, full, ids, layer))
        copies = _gather_copies(kinds, grp in HALVED)
        send, recv, _, lands, pin = _copies_start("gather_start_" + grp, [], lands, copies, 3 * len(lands), pins)
        gathers[grp] = (send, recv, [], lands, copies, kinds)
        return [pin]

    (packed_full,) = _gather_weights([pack_shard(wts)], ["row"])
    w = {}
    pins = start_gather(FETCH_ORDER[0], [packed_full])
    started = {}
    swapping = {}

    def prefetch(grp, after):
        if grp == "rest":
            last = [after]
            for later in FETCH_ORDER[1:]:
                last = start_gather(later, last)
            started["pin"] = last
            return
        send, recv, srcs, lands, copies, kinds = gathers[grp]
        _, fulls = _copies_wait("gather_wait_" + grp, send, recv, srcs, lands, copies, [after])
        copies = _gather_pass_copies(kinds)
        send, recv, _, fulls, _ = _copies_start("gather_swap_" + grp, [], fulls, copies, 3 * len(fulls), ())
        swapping[grp] = (send, recv, fulls, copies)

    def fetch(grp, after):
        if grp in swapping:
            send, recv, fulls, copies = swapping[grp]
            _, fulls = _copies_wait("gather_swapped_" + grp, send, recv, [], fulls, copies, [after])
        else:
            send, recv, srcs, lands, copies, kinds = gathers[grp]
            behind = [after] + (started.get("pin", []) if grp == FETCH_ORDER[0] else [])
            _, fulls = _copies_wait("gather_wait_" + grp, send, recv, srcs, lands, copies, behind)
            if grp in HALVED:
                fulls = _gather_pass_on("gather_pass_" + grp, fulls, kinds)
        return {key: full for (key, _, _, _), full in zip(GROUPS[grp], fulls)}

    packed = packed_full.reshape(N_CHIPS, PACK_ROWS, dq)
    for n, first, rows in PACKED:
        part = packed[:, first:first + rows, :]
        if n == "conv_dw":
            w[n] = part.transpose(1, 0, 2).reshape(rows, d)
        else:
            w[n] = part.reshape(1, N_CHIPS * rows * dq)
    for n, _, rows in WHOLE:
        w[n] = wts[n].reshape(rows, d)

    exchanges = {}
    passing = {}
    own_grads = {}

    def retire(tag, after):
        send, recv, srcs, lands, copies, kinds = exchanges[tag]
        srcs, lands = _copies_wait("grads_wait_" + tag, send, recv, srcs, lands, copies, after)
        own_grads[tag] = list(zip(srcs, kinds))
        copies = _pass_copies(len(lands))
        send, recv, _, lands, pin = _copies_start("grads_pass_" + tag, [], lands, copies, 3 * len(lands), ())
        passing[tag] = (send, recv, lands, copies)
        return pin

    def emit_grads(tag, grads, kinds):
        pins = [retire(old, [grads[0]]) for old in RETIRE_AT.get(tag, ())]
        copies = _grad_copies(kinds)
        lands = [lax.empty(_land_shape(gr, kd), gr.dtype) for gr, kd in zip(grads, kinds)]
        send, recv, srcs, lands, pin = _copies_start("grads_start_" + tag, grads, lands, copies,
                                                     GRAD_PEERS * len(grads), pins)
        exchanges[tag] = (send, recv, srcs, lands, copies, kinds)
        return [pin]

    def emit(grp, grads):
        return emit_grads(grp, [grads[key] for key, _, _, _ in GROUPS[grp]], [kind for _, _, _, kind in GROUPS[grp]])

    loss_cols, dx, g = _local_step(x2, tgt, w, fetch, emit, prefetch, after=pins)
    loss = lax.psum(jnp.sum(loss_cols), ("x", "y", "c"))

    gp = []
    for n, first, rows in PACKED:
        if n == "conv_dw":
            part = g[n].reshape(rows, N_CHIPS, dq).transpose(1, 0, 2)
        else:
            part = g[n].reshape(N_CHIPS, rows, dq)
        gp.append((part, first))
    g_packed = jnp.concatenate(
        [_pack_rows([(p[ci], first) for p, first in gp], PACK_ROWS, dq) for ci in range(N_CHIPS)], axis=0)
    emit_grads("vec", [g_packed, pack_whole(g)], ["row", "all"])
    for tag in EMIT_ORDER:
        if tag not in passing:
            retire(tag, [dx])
    contribs = {}
    for tag in EMIT_ORDER:
        send, recv, lands, copies = passing[tag]
        _, arrived = _copies_wait("grads_passed_" + tag, send, recv, [], lands, copies, [dx])
        contribs[tag] = [(c, own, kind) for c, (own, kind) in zip(arrived, own_grads[tag])]

    res = {}

    def adamw(n, contrib, layer=None, prev=None):
        arrived, own, kind = contrib
        if layer is None:
            shape = wts[n].shape
            r2 = shape[-2:]
            outs = _adamw_reduce("adamw_" + n, arrived, own, kind, ids, wts[n].reshape(r2), mom[n].reshape(r2),
                                 vel[n].reshape(r2))
            return [o.reshape(shape) for o in outs]
        return _adamw_reduce(f"adamw_{n}_{layer}", arrived, own, kind, ids, wts[n], mom[n], vel[n], layer, prev)

    for grp in ("attn", "conv2", "conv1"):
        for (key, n, _, _), contrib in zip(GROUPS[grp], contribs[grp]):
            res[n] = adamw(n, contrib)
    for part in ("d", "gu"):
        for (key, n, _, _), c0, c1 in zip(GROUPS["ffn0" + part], contribs["ffn0" + part], contribs["ffn1" + part]):
            res[n] = adamw(n, c1, 1, adamw(n, c0, 0))
    outs = _adamw_reduce("adamw_packed", *contribs["vec"][0], ids, pack_shard(wts), pack_shard(mom), pack_shard(vel))
    for n, first, rows in PACKED:
        res[n] = [o[first:first + rows].reshape(wts[n].shape) for o in outs]
    outs = _adamw_reduce("adamw_whole", *contribs["vec"][1], ids, pack_whole(wts), pack_whole(mom), pack_whole(vel))
    for n, first, rows in WHOLE:
        res[n] = [o[first:first + rows].reshape(wts[n].shape) for o in outs]

    out = [loss, dx.reshape(x.shape)]
    for which in range(4):
        out += [res[n][which] for n in WEIGHT_NAMES]
    return tuple(out)
```

```python
import functools

import jax
import jax.numpy as jnp
from jax import lax
from jax.experimental import pallas as pl
from jax.experimental.pallas import tpu as pltpu

F32 = jnp.float32
BF16 = jnp.bfloat16

HEAD_DIM = 128
BRANCHES = ((128, 1), (512, 4), (2048, 16))
CONV_WIDTH = 31
CONV_HALO = 32
RMS_EPS = 1e-6
LN_EPS = 1e-5
ADAM_LR = 0.001
ADAM_B1 = 0.9
ADAM_B2 = 0.999
ADAM_EPS = 1e-08
ADAM_WD = 0.01
ADAM_STEP = 10
N_CHIPS = 4
N_DEV = 8
MESH = pl.DeviceIdType.MESH


def _sigmoid(x):
    return 0.5 * jnp.tanh(0.5 * x) + 0.5


def _row_tile(rows, want):
    t = min(rows, want)
    assert rows % t == 0, (rows, want)
    return t


_DOT_DIMS = {"nn": ((1,), (0,)), "nt": ((1,), (1,)), "tn": ((0,), (0,))}


ANY_SPEC = pl.BlockSpec(memory_space=pl.ANY)
WIDE_BN = 1024


def _mm(name, mode, a, bs, epilogue, outs, *, m, n, k, extras=(), bm=1024, bn=512, bk=None, after=()):
    bm, bn = min(bm, m), min(bn, n)
    bk = k if bk is None else min(bk, k)
    assert m % bm == 0 and n % bn == 0 and k % bk == 0, (name, m, n, k, bm, bn, bk)
    nk = k // bk
    a_list = list(a) if isinstance(a, (list, tuple)) else [a]
    na, nb, ne, no = len(a_list), len(bs), len(extras), len(outs)
    assert na in (1, nb)

    if mode == "tn":
        a_spec = pl.BlockSpec((bk, bm), lambda i, j, kk: (kk, i))
    else:
        a_spec = pl.BlockSpec((bm, bk), lambda i, j, kk: (i, kk))

    def b_spec(lead, off):
        if mode == "nt":
            blk, idx = (bn, bk), (lambda i, j, kk: (j + off, kk))
        else:
            blk, idx = (bk, bn), (lambda i, j, kk: (kk, j + off))
        if lead is None:
            return pl.BlockSpec(blk, idx)
        return pl.BlockSpec((None,) + blk, lambda i, j, kk: (lead,) + idx(i, j, kk))

    def e_spec(kind, off):
        if kind == "mn":
            return pl.BlockSpec((bm, bn), lambda i, j, kk: (i, j + off))
        return pl.BlockSpec((1, bn), lambda i, j, kk: (0, j + off))

    nf = len(after)
    in_specs = [a_spec] * na + [b_spec(l, o) for _, l, o in bs] + [e_spec(kd, o) for _, kd, o in extras]
    in_specs += [ANY_SPEC] * nf
    out_specs = [pl.BlockSpec((bm, bn), lambda i, j, kk: (i, j)) for _ in outs]
    out_shape = [jax.ShapeDtypeStruct((m, n), dt) for (dt,) in outs]
    dims = (_DOT_DIMS[mode], ((), ()))

    def body(*refs):
        a_refs = refs[:na]
        b_refs = refs[na:na + nb]
        e_refs = refs[na + nb:na + nb + ne]
        o_refs = refs[na + nb + ne + nf:na + nb + ne + nf + no]
        acc_refs = refs[na + nb + ne + nf + no:]
        avs = [a_ref[...].astype(BF16) for a_ref in a_refs]
        prods = [lax.dot_general(avs[bi % na], b_ref[...].astype(BF16), dims, preferred_element_type=F32)
                 for bi, b_ref in enumerate(b_refs)]

        def finish(accs):
            res = epilogue(accs, [e_ref[...] for e_ref in e_refs])
            for o_ref, r in zip(o_refs, res):
                o_ref[...] = r.astype(o_ref.dtype)

        if nk == 1:
            finish(prods)
        else:
            kk = pl.program_id(2)

            @pl.when(kk == 0)
            def _():
                for acc_ref, p in zip(acc_refs, prods):
                    acc_ref[...] = p

            @pl.when(kk > 0)
            def _():
                for acc_ref, p in zip(acc_refs, prods):
                    acc_ref[...] += p

            @pl.when(kk == nk - 1)
            def _():
                finish([acc_ref[...] for acc_ref in acc_refs])

    scratch = [] if nk == 1 else [pltpu.VMEM((bm, bn), F32) for _ in bs]
    res = pl.pallas_call(
        body,
        name=name,
        grid=(m // bm, n // bn, nk),
        in_specs=in_specs,
        out_specs=out_specs,
        out_shape=out_shape,
        scratch_shapes=scratch,
        compiler_params=pltpu.CompilerParams(dimension_semantics=("parallel", "parallel", "arbitrary")),
    )(*a_list, *[b for b, _, _ in bs], *[e for e, _, _ in extras], *after)
    return res


def _rms_fwd(name, x, gains, after=()):
    s, d = x.shape
    ng = gains.shape[0]
    t = _row_tile(s, 256)
    nf = len(after)

    def body(x_ref, g_ref, *refs):
        o_refs = refs[nf:]
        xv = x_ref[...]
        r = lax.rsqrt(jnp.mean(xv * xv, axis=-1, keepdims=True) + RMS_EPS)
        xh = xv * r
        for gi, o_ref in enumerate(o_refs):
            o_ref[...] = (xh * g_ref[gi:gi + 1, :]).astype(o_ref.dtype)

    return pl.pallas_call(
        body,
        name=name,
        grid=(s // t,),
        in_specs=[pl.BlockSpec((t, d), lambda i: (i, 0)), pl.BlockSpec((ng, d), lambda i: (0, 0))] + [ANY_SPEC] * nf,
        out_specs=[pl.BlockSpec((t, d), lambda i: (i, 0)) for _ in range(ng)],
        out_shape=[jax.ShapeDtypeStruct((s, d), BF16) for _ in range(ng)],
        compiler_params=pltpu.CompilerParams(dimension_semantics=("parallel",)),
    )(x, gains, *after)


def _rms_bwd(name, x, gains, dns, dres):
    s, d = x.shape
    ng = gains.shape[0]
    t = _row_tile(s, 256)

    def body(x_ref, g_ref, dres_ref, *refs):
        dn_refs = refs[:ng]
        dx_ref, dxb_ref, dg_ref, cs_ref = refs[ng:]
        i = pl.program_id(0)
        xv = x_ref[...]
        r = lax.rsqrt(jnp.mean(xv * xv, axis=-1, keepdims=True) + RMS_EPS)
        xh = xv * r
        dx = dres_ref[...]
        dgs = []
        for gi in range(ng):
            dn = dn_refs[gi][...].astype(F32)
            dxh = dn * g_ref[gi:gi + 1, :]
            dgs.append(jnp.sum(dn * xh, axis=0, keepdims=True))
            dx = dx + r * (dxh - xh * jnp.mean(dxh * xh, axis=-1, keepdims=True))
        dx_ref[...] = dx
        dxb_ref[...] = dx.astype(BF16)
        dg = jnp.concatenate(dgs, axis=0) if ng > 1 else dgs[0]
        cs = jnp.sum(dx, axis=0, keepdims=True)

        @pl.when(i == 0)
        def _():
            dg_ref[...] = dg
            cs_ref[...] = cs

        @pl.when(i > 0)
        def _():
            dg_ref[...] += dg
            cs_ref[...] += cs

    row = pl.BlockSpec((t, d), lambda i: (i, 0))
    return pl.pallas_call(
        body,
        name=name,
        grid=(s // t,),
        in_specs=[row, pl.BlockSpec((ng, d), lambda i: (0, 0)), row] + [row] * ng,
        out_specs=[row, row, pl.BlockSpec((ng, d), lambda i: (0, 0)), pl.BlockSpec((1, d), lambda i: (0, 0))],
        out_shape=[
            jax.ShapeDtypeStruct((s, d), F32),
            jax.ShapeDtypeStruct((s, d), BF16),
            jax.ShapeDtypeStruct((ng, d), F32),
            jax.ShapeDtypeStruct((1, d), F32),
        ],
        compiler_params=pltpu.CompilerParams(dimension_semantics=("arbitrary",)),
    )(x, gains, dres, *dns)


def _final_loss(name, h, gain, target):
    s, d = h.shape
    t = _row_tile(s, 256)

    def body(h_ref, g_ref, t_ref, dh_ref, dhb_ref, dg_ref, ls_ref):
        i = pl.program_id(0)
        xv = h_ref[...]
        g = g_ref[...]
        r = lax.rsqrt(jnp.mean(xv * xv, axis=-1, keepdims=True) + RMS_EPS)
        xh = xv * r
        err = xh * g - t_ref[...]
        ls = jnp.sum(err * err, axis=0, keepdims=True) * (0.5 / d)
        dy = err * (1.0 / d)
        dxh = dy * g
        dg = jnp.sum(dy * xh, axis=0, keepdims=True)
        dx = r * (dxh - xh * jnp.mean(dxh * xh, axis=-1, keepdims=True))
        dh_ref[...] = dx
        dhb_ref[...] = dx.astype(BF16)

        @pl.when(i == 0)
        def _():
            dg_ref[...] = dg
            ls_ref[...] = ls

        @pl.when(i > 0)
        def _():
            dg_ref[...] += dg
            ls_ref[...] += ls

    row = pl.BlockSpec((t, d), lambda i: (i, 0))
    vec = pl.BlockSpec((1, d), lambda i: (0, 0))
    return pl.pallas_call(
        body,
        name=name,
        grid=(s // t,),
        in_specs=[row, vec, row],
        out_specs=[row, row, vec, vec],
        out_shape=[
            jax.ShapeDtypeStruct((s, d), F32),
            jax.ShapeDtypeStruct((s, d), BF16),
            jax.ShapeDtypeStruct((1, d), F32),
            jax.ShapeDtypeStruct((1, d), F32),
        ],
        compiler_params=pltpu.CompilerParams(dimension_semantics=("arbitrary",)),
    )(h, gain, target)


SUBLANES = 8
CONV_LANES = 512
CONV_ROWS = 32
NORM_ROWS = 64


def _conv_tiles(s):
    t = _row_tile(s, 128)
    assert t % CONV_HALO == 0 and t % CONV_ROWS == 0
    return t, t // CONV_HALO


def _shifted_copies(dst, src, lanes, rows):
    for m in range(SUBLANES):
        n = rows if m == 0 else rows - SUBLANES
        dst[m, :n, :] = src[m:m + n, lanes]


def _shifted(copies, offset, n):
    m = offset % SUBLANES
    return copies[m, offset - m:offset - m + n, :]


def _dwconv_fwd(name, u, dw, dw_b, ln_g, ln_b):
    s, d = u.shape
    t, hb = _conv_tiles(s)
    w = dw.shape[0]
    lo = CONV_HALO - (w - 1)

    lw = min(CONV_LANES, d)

    def body(cur_ref, prev_ref, dw_ref, dwb_ref, lg_ref, lb_ref, c_ref, sw_ref, cat_ref, sh_ref):
        i = pl.program_id(0)
        cat_ref[CONV_HALO:, :] = cur_ref[...]

        @pl.when(i == 0)
        def _():
            cat_ref[:CONV_HALO, :] = jnp.zeros((CONV_HALO, d), F32)

        @pl.when(i > 0)
        def _():
            cat_ref[:CONV_HALO, :] = prev_ref[...]

        for lc in range(d // lw):
            lanes = slice(lc * lw, (lc + 1) * lw)
            _shifted_copies(sh_ref, cat_ref, lanes, t + CONV_HALO)
            for rc in range(t // CONV_ROWS):
                acc = jnp.broadcast_to(dwb_ref[:, lanes], (CONV_ROWS, lw))
                for kk in range(w):
                    acc = acc + dw_ref[kk:kk + 1, lanes] * _shifted(sh_ref, lo + kk + rc * CONV_ROWS, CONV_ROWS)
                c_ref[rc * CONV_ROWS:(rc + 1) * CONV_ROWS, lanes] = acc

        def norm_rows(ri, carry):
            rows = pl.ds(pl.multiple_of(ri * NORM_ROWS, NORM_ROWS), NORM_ROWS)
            cv = c_ref[rows, :]
            cc = cv - jnp.mean(cv, axis=-1, keepdims=True)
            var = jnp.mean(cc * cc, axis=-1, keepdims=True)
            ln = cc * lax.rsqrt(var + LN_EPS) * lg_ref[...] + lb_ref[...]
            sw_ref[rows, :] = (ln * _sigmoid(ln)).astype(BF16)
            return carry

        lax.fori_loop(0, t // NORM_ROWS, norm_rows, 0)

    row = pl.BlockSpec((t, d), lambda i: (i, 0))
    prev = pl.BlockSpec((CONV_HALO, d), lambda i: (jnp.maximum(i * hb - 1, 0), 0))
    vec = pl.BlockSpec((1, d), lambda i: (0, 0))
    return pl.pallas_call(
        body,
        name=name,
        grid=(s // t,),
        in_specs=[row, prev, pl.BlockSpec((w, d), lambda i: (0, 0)), vec, vec, vec],
        out_specs=[row, row],
        out_shape=[jax.ShapeDtypeStruct((s, d), F32), jax.ShapeDtypeStruct((s, d), BF16)],
        scratch_shapes=[pltpu.VMEM((CONV_HALO + t, d), F32), pltpu.VMEM((SUBLANES, CONV_HALO + t, lw), F32)],
        compiler_params=pltpu.CompilerParams(dimension_semantics=("parallel",)),
    )(u, u, dw, dw_b, ln_g, ln_b)


def _conv_ln_bwd(name, c, dsw, ln_g, ln_b, after=()):
    s, d = c.shape
    t = _row_tile(s, 256)
    nf = len(after)

    def body(c_ref, dsw_ref, lg_ref, lb_ref, *refs):
        dc_ref, sums_ref = refs[nf:]
        i = pl.program_id(0)
        cv = c_ref[...]
        g = lg_ref[...]
        mu = jnp.mean(cv, axis=-1, keepdims=True)
        cc = cv - mu
        rstd = lax.rsqrt(jnp.mean(cc * cc, axis=-1, keepdims=True) + LN_EPS)
        ch = cc * rstd
        ln = ch * g + lb_ref[...]
        sg = _sigmoid(ln)
        dln = dsw_ref[...] * (sg * (1.0 + ln * (1.0 - sg)))
        dch = dln * g
        dc = rstd * (dch - jnp.mean(dch, axis=-1, keepdims=True) - ch * jnp.mean(dch * ch, axis=-1, keepdims=True))
        dc_ref[...] = dc
        sums = jnp.concatenate(
            [
                jnp.sum(dln * ch, axis=0, keepdims=True),
                jnp.sum(dln, axis=0, keepdims=True),
                jnp.sum(dc, axis=0, keepdims=True),
                jnp.zeros((1, d), F32),
            ],
            axis=0,
        )

        @pl.when(i == 0)
        def _():
            sums_ref[...] = sums

        @pl.when(i > 0)
        def _():
            sums_ref[...] += sums

    row = pl.BlockSpec((t, d), lambda i: (i, 0))
    vec = pl.BlockSpec((1, d), lambda i: (0, 0))
    return pl.pallas_call(
        body,
        name=name,
        grid=(s // t,),
        in_specs=[row, row, vec, vec] + [ANY_SPEC] * nf,
        out_specs=[row, pl.BlockSpec((4, d), lambda i: (0, 0))],
        out_shape=[jax.ShapeDtypeStruct((s, d), F32), jax.ShapeDtypeStruct((4, d), F32)],
        compiler_params=pltpu.CompilerParams(dimension_semantics=("arbitrary",)),
    )(c, dsw, ln_g, ln_b, *after)


def _conv_dw_bwd(name, dc, u, a, gt, dw):
    s, d = dc.shape
    t, hb = _conv_tiles(s)
    w = dw.shape[0]
    lo = CONV_HALO - (w - 1)
    nt = s // t
    lw = min(CONV_LANES, d)

    def body(dc_ref, dcn_ref, u_ref, up_ref, a_ref, gt_ref, dw_ref, dpre_ref, ddw_ref, db_ref, dcat_ref, ucat_ref,
             dsh_ref, ush_ref, ddw_acc, db_acc):
        i = pl.program_id(0)
        dcat_ref[:t, :] = dc_ref[...]
        ucat_ref[CONV_HALO:, :] = u_ref[...]

        @pl.when(i == nt - 1)
        def _():
            dcat_ref[t:, :] = jnp.zeros((CONV_HALO, d), F32)

        @pl.when(i < nt - 1)
        def _():
            dcat_ref[t:, :] = dcn_ref[...]

        @pl.when(i == 0)
        def _():
            ucat_ref[:CONV_HALO, :] = jnp.zeros((CONV_HALO, d), F32)

        @pl.when(i > 0)
        def _():
            ucat_ref[:CONV_HALO, :] = up_ref[...]

        @pl.when(i == 0)
        def _():
            ddw_acc[...] = jnp.zeros(ddw_acc.shape, F32)
            db_acc[...] = jnp.zeros(db_acc.shape, F32)

        def fold(p):
            return functools.reduce(jnp.add, [p[j:j + SUBLANES] for j in range(0, CONV_ROWS, SUBLANES)])

        for lc in range(d // lw):
            lanes = slice(lc * lw, (lc + 1) * lw)
            gate_lanes = slice(d + lc * lw, d + (lc + 1) * lw)
            _shifted_copies(dsh_ref, dcat_ref, lanes, t + CONV_HALO)
            _shifted_copies(ush_ref, ucat_ref, lanes, t + CONV_HALO)
            for rc in range(t // CONV_ROWS):
                r0 = rc * CONV_ROWS
                rows = slice(r0, r0 + CONV_ROWS)
                dcv = dc_ref[rows, lanes]
                du = jnp.zeros((CONV_ROWS, lw), F32)
                for kk in range(w):
                    du = du + dw_ref[kk:kk + 1, lanes] * _shifted(dsh_ref, w - 1 - kk + r0, CONV_ROWS)
                    ddw_acc[kk, :, lanes] += fold(dcv * _shifted(ush_ref, lo + kk + r0, CONV_ROWS))
                av = a_ref[rows, lanes].astype(F32)
                sg = _sigmoid(gt_ref[rows, lanes].astype(F32))
                da = du * sg
                dgt = du * av * sg * (1.0 - sg)
                dpre_ref[rows, lanes] = da.astype(BF16)
                dpre_ref[rows, gate_lanes] = dgt.astype(BF16)
                db_acc[:, lanes] += fold(da)
                db_acc[:, gate_lanes] += fold(dgt)

        @pl.when(i == nt - 1)
        def _():
            ddw_ref[...] = jnp.sum(ddw_acc[...], axis=1)
            db_ref[...] = jnp.sum(db_acc[...], axis=0, keepdims=True)

    row = pl.BlockSpec((t, d), lambda i: (i, 0))
    nxt = pl.BlockSpec((CONV_HALO, d), lambda i: (jnp.minimum((i + 1) * hb, s // CONV_HALO - 1), 0))
    prev = pl.BlockSpec((CONV_HALO, d), lambda i: (jnp.maximum(i * hb - 1, 0), 0))
    return pl.pallas_call(
        body,
        name=name,
        grid=(nt,),
        in_specs=[row, nxt, row, prev, row, row, pl.BlockSpec((w, d), lambda i: (0, 0))],
        out_specs=[
            pl.BlockSpec((t, 2 * d), lambda i: (i, 0)),
            pl.BlockSpec((w, d), lambda i: (0, 0)),
            pl.BlockSpec((1, 2 * d), lambda i: (0, 0)),
        ],
        out_shape=[
            jax.ShapeDtypeStruct((s, 2 * d), BF16),
            jax.ShapeDtypeStruct((w, d), F32),
            jax.ShapeDtypeStruct((1, 2 * d), F32),
        ],
        scratch_shapes=[pltpu.VMEM((t + CONV_HALO, d), F32), pltpu.VMEM((CONV_HALO + t, d), F32)]
        + [pltpu.VMEM((SUBLANES, CONV_HALO + t, lw), F32)] * 2
        + [pltpu.VMEM((w, SUBLANES, d), F32), pltpu.VMEM((SUBLANES, 2 * d), F32)],
        compiler_params=pltpu.CompilerParams(dimension_semantics=("arbitrary",)),
    )(dc, dc, u, u, a, gt, dw)


def _alibi_slopes(n_heads):
    h = jnp.arange(1, n_heads + 1, dtype=F32)
    return jnp.exp2(-8.0 * h / n_heads)


def _band_masks(bq):
    qi = lax.broadcasted_iota(jnp.int32, (bq, bq), 0)
    kj = lax.broadcasted_iota(jnp.int32, (bq, bq), 1)
    return qi - kj, qi - kj + bq


ATTN_GROUP = 16


def _attn_fwd(name, q, k, v, slopes, bq):
    s, dm = q.shape
    nh = dm // HEAD_DIM
    nt = s // bq
    nbr = len(BRANCHES)
    scale = HEAD_DIM ** -0.5
    nt_dims = (((1,), (1,)), ((), ()))

    def body(sl_ref, q_ref, k_ref, v_ref, o_ref, ob_ref, l_ref, tmp, qr, kr, va, orm, lrm, onat, lnat, sbuf, mbuf):
        slope = sl_ref[pl.program_id(0)]
        jc, jp = _band_masks(bq)
        va[:, HEAD_DIM:] = jnp.ones((s, HEAD_DIM), BF16)
        for bi, (win, dil) in enumerate(BRANCHES):
            ll = s // dil
            nblk = ll // bq
            if dil == 1:
                sq, sk = q_ref, k_ref
                va[:, :HEAD_DIM] = v_ref[...]
                d_o, d_l = onat.at[bi], lnat.at[bi]
            else:
                for src, dst, wide in ((q_ref, qr, False), (k_ref, kr, False), (v_ref, va, True)):
                    tmp[...] = src[...].astype(F32)
                    for r in range(dil):
                        part = tmp[pl.ds(r, ll, stride=dil), :].astype(BF16)
                        if wide:
                            dst[r * ll:(r + 1) * ll, :HEAD_DIM] = part
                        else:
                            dst[r * ll:(r + 1) * ll, :] = part
                sq, sk = qr, kr
                d_o, d_l = orm, lrm
            bias_c = jnp.where(jc >= 0, jc.astype(F32) * (slope * dil), 1e30)
            bias_p = jnp.where(jp <= bq, jp.astype(F32) * (slope * dil), 1e30)

            def group(gi, carry):
                rows = []
                for g in range(ATTN_GROUP):
                    ti = gi * ATTN_GROUP + g
                    row = pl.ds(pl.multiple_of(ti * bq, bq), bq)
                    prow = pl.ds(pl.multiple_of(jnp.maximum(ti - 1, 0) * bq, bq), bq)
                    rows.append((row, prow))
                    qh = sq[row, :]
                    sc = lax.dot_general(qh, sk[row, :], nt_dims, preferred_element_type=F32) * scale - bias_c
                    sp = lax.dot_general(qh, sk[prow, :], nt_dims, preferred_element_type=F32) * scale - bias_p
                    sp = jnp.where(lax.rem(ti, nblk) > 0, sp, -1e30)
                    sbuf[g, :, :bq] = sc
                    sbuf[g, :, bq:] = sp
                    mbuf[g] = jnp.maximum(jnp.max(sc, axis=-1, keepdims=True), jnp.max(sp, axis=-1, keepdims=True))
                for g, (row, prow) in enumerate(rows):
                    mx = mbuf[g]
                    p = jnp.exp(sbuf[g] - mx).astype(BF16)
                    ov = jnp.dot(p[:, :bq], va[row, :], preferred_element_type=F32)
                    ov = ov + jnp.dot(p[:, bq:], va[prow, :], preferred_element_type=F32)
                    den = ov[:, HEAD_DIM:]
                    d_o[row, :] = ov[:, :HEAD_DIM] / den
                    d_l[row, :] = mx + jnp.log(den)
                return carry

            lax.fori_loop(0, nt // ATTN_GROUP, group, 0)
            if dil > 1:
                for r in range(dil):
                    onat[bi, pl.ds(r, ll, stride=dil), :] = orm[r * ll:(r + 1) * ll, :]
                    lnat[bi, pl.ds(r, ll, stride=dil), :] = lrm[r * ll:(r + 1) * ll, :]

        def merge(ti, carry):
            rows = pl.ds(pl.multiple_of(ti * bq, bq), bq)
            ls = [lnat[bi, rows, :] for bi in range(nbr)]
            mx = functools.reduce(jnp.maximum, ls)
            es = [jnp.exp(l - mx) for l in ls]
            tot = functools.reduce(jnp.add, es)
            inv = 1.0 / tot
            o = functools.reduce(jnp.add, [e * inv * onat[bi, rows, :] for bi, e in enumerate(es)])
            o_ref[rows, :] = o
            ob_ref[rows, :] = o.astype(BF16)
            l_ref[rows, :] = mx + jnp.log(tot)
            return carry

        lax.fori_loop(0, nt, merge, 0)

    head = pl.BlockSpec((s, HEAD_DIM), lambda h: (0, h))
    return pl.pallas_call(
        body,
        name=name,
        grid=(nh,),
        in_specs=[pl.BlockSpec(memory_space=pltpu.SMEM), head, head, head],
        out_specs=[head, head, head],
        out_shape=[jax.ShapeDtypeStruct((s, dm), F32), jax.ShapeDtypeStruct((s, dm), BF16),
                   jax.ShapeDtypeStruct((s, dm), F32)],
        scratch_shapes=[pltpu.VMEM((s, HEAD_DIM), F32)] + [pltpu.VMEM((s, HEAD_DIM), BF16)] * 2
        + [pltpu.VMEM((s, 2 * HEAD_DIM), BF16)] + [pltpu.VMEM((s, HEAD_DIM), F32)] * 2
        + [pltpu.VMEM((nbr, s, HEAD_DIM), F32)] * 2
        + [pltpu.VMEM((ATTN_GROUP, bq, 2 * bq), F32), pltpu.VMEM((ATTN_GROUP, bq, 1), F32)],
        compiler_params=pltpu.CompilerParams(dimension_semantics=("parallel",)),
    )(slopes, q, k, v)


def _attn_bwd(name, q, k, v, o, lse, do, slopes, bq):
    s, dm = q.shape
    nh = dm // HEAD_DIM
    nt = s // bq
    scale = HEAD_DIM ** -0.5
    nt_dims = (((1,), (1,)), ((), ()))
    tn_dims = (((0,), (0,)), ((), ()))

    def body(sl_ref, q_ref, k_ref, v_ref, o_ref, l_ref, do_ref, dq_ref, dk_ref, dv_ref,
             tmp, qr, kr, vr, dor, rown, rowr, dqr, dkr, dvr, aq, ak, av, pbuf, dsbuf):
        slope = sl_ref[pl.program_id(0)]
        jc, jp = _band_masks(bq)
        lane = lax.broadcasted_iota(jnp.int32, (bq, HEAD_DIM), 1)

        def row_terms(ti, carry):
            rows = pl.ds(pl.multiple_of(ti * bq, bq), bq)
            dl = jnp.sum(do_ref[rows, :].astype(F32) * o_ref[rows, :], axis=-1, keepdims=True)
            rown[rows, :] = jnp.where(lane == 0, l_ref[rows, :], dl)
            return carry

        lax.fori_loop(0, nt, row_terms, 0)

        for bi, (win, dil) in enumerate(BRANCHES):
            ll = s // dil
            nblk = ll // bq
            if dil == 1:
                sq, sk, sv, sdo, srow = q_ref, k_ref, v_ref, do_ref, rown
                gq, gk, gv = aq, ak, av
            else:
                for src, dst in ((q_ref, qr), (k_ref, kr), (v_ref, vr), (do_ref, dor)):
                    tmp[...] = src[...].astype(F32)
                    for r in range(dil):
                        dst[r * ll:(r + 1) * ll, :] = tmp[pl.ds(r, ll, stride=dil), :].astype(BF16)
                for r in range(dil):
                    rowr[r * ll:(r + 1) * ll, :] = rown[pl.ds(r, ll, stride=dil), :]
                sq, sk, sv, sdo, srow = qr, kr, vr, dor, rowr
                gq, gk, gv = dqr, dkr, dvr
            bias_c = jnp.where(jc >= 0, jc.astype(F32) * (slope * dil), 1e30)
            bias_p = jnp.where(jp <= bq, jp.astype(F32) * (slope * dil), 1e30)

            def group(gi, carry):
                rows = []
                for g in range(ATTN_GROUP):
                    ti = gi * ATTN_GROUP + g
                    row = pl.ds(pl.multiple_of(ti * bq, bq), bq)
                    prow = pl.ds(pl.multiple_of(jnp.maximum(ti - 1, 0) * bq, bq), bq)
                    rows.append((row, prow))
                    has_prev = lax.rem(ti, nblk) > 0
                    qh, doh = sq[row, :], sdo[row, :]
                    terms = srow[row, :]
                    lc = terms[:, 0:1]
                    dl = terms[:, 1:2]
                    for half, kv_rows, bias in ((0, row, bias_c), (1, prow, bias_p)):
                        sc = lax.dot_general(qh, sk[kv_rows, :], nt_dims, preferred_element_type=F32) * scale - bias
                        if half:
                            sc = jnp.where(has_prev, sc, -1e30)
                        p = jnp.exp(sc - lc)
                        dp = lax.dot_general(doh, sv[kv_rows, :], nt_dims, preferred_element_type=F32)
                        pbuf[g, :, half * bq:(half + 1) * bq] = p.astype(BF16)
                        dsbuf[g, :, half * bq:(half + 1) * bq] = (p * (dp - dl) * scale).astype(BF16)
                carry_k = carry_v = None
                for g, (row, prow) in enumerate(rows):
                    qh, doh = sq[row, :], sdo[row, :]
                    ds_c, ds_p = dsbuf[g, :, :bq], dsbuf[g, :, bq:]
                    p_c, p_p = pbuf[g, :, :bq], pbuf[g, :, bq:]
                    dq = jnp.dot(ds_c, sk[row, :], preferred_element_type=F32)
                    dq = dq + jnp.dot(ds_p, sk[prow, :], preferred_element_type=F32)
                    gq[row, :] = dq
                    dk_p = lax.dot_general(ds_p, qh, tn_dims, preferred_element_type=F32)
                    dv_p = lax.dot_general(p_p, doh, tn_dims, preferred_element_type=F32)
                    if g == 0:
                        @pl.when(gi > 0)
                        def _():
                            gk[prow, :] += dk_p
                            gv[prow, :] += dv_p
                    else:
                        gk[rows[g - 1][0], :] = carry_k + dk_p
                        gv[rows[g - 1][0], :] = carry_v + dv_p
                    carry_k = lax.dot_general(ds_c, qh, tn_dims, preferred_element_type=F32)
                    carry_v = lax.dot_general(p_c, doh, tn_dims, preferred_element_type=F32)
                gk[rows[-1][0], :] = carry_k
                gv[rows[-1][0], :] = carry_v
                return carry

            lax.fori_loop(0, nt // ATTN_GROUP, group, 0)
            if dil > 1:
                for acc, rm in ((aq, dqr), (ak, dkr), (av, dvr)):
                    for r in range(dil):
                        acc[pl.ds(r, ll, stride=dil), :] += rm[r * ll:(r + 1) * ll, :]

        dq_ref[...] = aq[...].astype(BF16)
        dk_ref[...] = ak[...].astype(BF16)
        dv_ref[...] = av[...].astype(BF16)

    head = pl.BlockSpec((s, HEAD_DIM), lambda h: (0, h))
    f32buf = pltpu.VMEM((s, HEAD_DIM), F32)
    b16buf = pltpu.VMEM((s, HEAD_DIM), BF16)
    return pl.pallas_call(
        body,
        name=name,
        grid=(nh,),
        in_specs=[pl.BlockSpec(memory_space=pltpu.SMEM)] + [head] * 6,
        out_specs=[head] * 3,
        out_shape=[jax.ShapeDtypeStruct((s, dm), BF16)] * 3,
        scratch_shapes=[f32buf] + [b16buf] * 4 + [f32buf] * 8 + [pltpu.VMEM((ATTN_GROUP, bq, 2 * bq), BF16)] * 2,
        compiler_params=pltpu.CompilerParams(dimension_semantics=("parallel",)),
    )(slopes, q, k, v, o, lse, do)


def _ep_id(accs, ex):
    return [accs[0]]


def _ep_all(accs, ex):
    return list(accs)


def _ep_sum(accs, ex):
    return [accs[0] + accs[1]]


def _ep_add(accs, ex):
    return [accs[0] + ex[0].astype(F32)]


def _ep_bias_res(accs, ex):
    return [accs[0] + ex[0] + ex[1]]


def _ep_glu(accs, ex):
    a = accs[0] + ex[0]
    gt = accs[1] + ex[1]
    return [a * _sigmoid(gt), a, gt]


def _ep_swiglu(accs, ex):
    g, u = accs
    return [g, u, g * _sigmoid(g) * u]


def _ep_swiglu_bwd(accs, ex):
    dact = accs[0]
    g = ex[0].astype(F32)
    u = ex[1].astype(F32)
    sg = _sigmoid(g)
    return [dact * u * (sg * (1.0 + g * (1.0 - sg))), dact * g * sg]


def _ffn_fwd(tag, h, gain, get_gate_up, get_down):
    s, d = h.shape
    (n,) = _rms_fwd(f"{tag}_norm", h, gain)
    wg, wu = get_gate_up(n)
    f = wg.shape[-1]
    g, u, act = _mm(f"{tag}_gate_up", "nn", n, [(wg, None, 0), (wu, None, 0)], _ep_swiglu,
                    [(BF16,), (BF16,), (BF16,)], m=s, n=f, k=d)
    wd = get_down(act)
    (out,) = _mm(f"{tag}_down", "nn", act, [(wd, None, 0)], _ep_add, [(F32,)], m=s, n=d, k=f,
                 extras=[(h, "mn", 0)])
    return out, (n, g, u, act), dict(gate=wg, up=wu, down=wd)


def _ffn_bwd(tag, h_in, gain, wg, wu, wd, saved, dh, dhb, after, emit):
    s, d = h_in.shape
    f = wg.shape[-1]
    n, g, u, act = saved
    dg, du = _mm(f"{tag}_bwd_dact", "nt", dhb, [(wd, None, 0)], _ep_swiglu_bwd, [(BF16,), (BF16,)],
                 m=s, n=f, k=d, extras=[(g, "mn", 0), (u, "mn", 0)], after=after)
    (dwd,) = _mm(f"{tag}_bwd_dwd", "tn", act, [(dhb, None, 0)], _ep_id, [(BF16,)], m=f, n=d, k=s,
                 bm=f // 4)
    pin = emit("d", dict(down=dwd))
    dwg, dwu = _mm(f"{tag}_bwd_dwgu", "tn", n, [(dg, None, 0), (du, None, 0)], _ep_all, [(BF16,), (BF16,)],
                   m=d, n=f, k=s, after=pin)
    pin = emit("gu", dict(gate=dwg, up=dwu))
    (dn,) = _mm(f"{tag}_bwd_dn", "nt", [dg, du], [(wg, None, 0), (wu, None, 0)], _ep_sum, [(BF16,)],
                m=s, n=d, k=f, bm=512, bn=256, after=pin)
    dx, dxb, dgain, cs = _rms_bwd(f"{tag}_bwd_norm", h_in, gain, [dn], dh)
    return dx, dxb, dgain, cs


def _local_step(x, target, w, fetch, emit, prefetch=lambda group, after: None, after=()):
    s, d = x.shape
    nh = d // HEAD_DIM
    bq = BRANCHES[0][0] // BRANCHES[0][1]
    assert all(win // dil == bq for win, dil in BRANCHES)
    assert BRANCHES[0][1] == 1 and all(dil > 1 for _, dil in BRANCHES[1:])
    slopes = _alibi_slopes(nh)
    nd = d // 512 if d >= 512 else 1
    bn = d // nd

    (n1,) = _rms_fwd("a_norm", x, w["a_norm_g"], after=after)
    prefetch("rest", n1)
    w_conv1 = fetch("conv1", n1)["conv_w1"]
    glu, a, gt = _mm("conv_pw1_glu", "nn", n1, [(w_conv1, None, 0), (w_conv1, None, nd)], _ep_glu,
                     [(F32,), (BF16,), (BF16,)], m=s, n=d, k=d, bn=bn,
                     extras=[(w["conv_b1"], "n", 0), (w["conv_b1"], "n", nd)])
    c, sw = _dwconv_fwd("conv_dw_ln", glu, w["conv_dw"], w["conv_dw_b"], w["conv_ln_g"], w["conv_ln_b"])
    w_conv2 = fetch("conv2", sw)["conv_w2"]
    (h1,) = _mm("conv_pw2", "nn", sw, [(w_conv2, None, 0)], _ep_bias_res, [(F32,)], m=s, n=d, k=d,
                extras=[(w["conv_b2"], "n", 0), (x, "mn", 0)], bn=WIDE_BN)
    def gate_up0(after):
        wts = fetch("ffn0gu", after)
        return wts["gate"], wts["up"]

    h2, ffn0, wf0 = _ffn_fwd("ffn0", h1, w["ffn_norm_g"][0:1], gate_up0, lambda after: fetch("ffn0d", after)["down"])
    wa = fetch("attn", h2)
    kvn, qn = _rms_fwd("kvq_norm", h2, jnp.concatenate([w["kv_norm_g"], w["b_norm_g"]], axis=0))
    k, v = _mm("kv_proj", "nn", kvn, [(wa["w_k"], None, 0), (wa["w_v"], None, 0)], _ep_all, [(BF16,), (BF16,)],
               m=s, n=d, k=d, bn=WIDE_BN)
    (q,) = _mm("q_proj", "nn", qn, [(wa["w_q"], None, 0)], _ep_id, [(BF16,)], m=s, n=d, k=d, bn=WIDE_BN)
    att, attb, lse = _attn_fwd("attn_fwd", q, k, v, slopes, bq)
    prefetch("ffn1", attb)
    (h3,) = _mm("o_proj", "nn", attb, [(wa["w_o"], None, 0)], _ep_add, [(F32,)], m=s, n=d, k=d,
                extras=[(h2, "mn", 0)], bn=WIDE_BN)
    got1 = {}

    def gate_up1(after):
        got1.update(fetch("ffn1", after))
        return got1["gate"], got1["up"]

    h4, ffn1, wf1 = _ffn_fwd("ffn1", h3, w["ffn_norm_g"][1:2], gate_up1, lambda after: got1["down"])
    dh4, dh4b, d_final_g, loss_cols = _final_loss("final_loss", h4, w["final_norm_g"], target)

    g = {}
    ga = {}
    dh3, dh3b, dgain1, _ = _ffn_bwd("ffn1", h3, w["ffn_norm_g"][1:2], wf1["gate"], wf1["up"], wf1["down"], ffn1,
                                    dh4, dh4b, (), lambda part, grads: emit("ffn1" + part, grads))
    (datt,) = _mm("o_proj_bwd_dx", "nt", dh3b, [(wa["w_o"], None, 0)], _ep_id, [(BF16,)], m=s, n=d, k=d, bn=WIDE_BN)
    (ga["w_o"],) = _mm("o_proj_bwd_dw", "tn", attb, [(dh3b, None, 0)], _ep_id, [(BF16,)], m=d, n=d, k=s, bn=WIDE_BN)
    dq, dk, dv = _attn_bwd("attn_bwd", q, k, v, att, lse, datt, slopes, bq)
    (ga["w_q"],) = _mm("q_proj_bwd_dw", "tn", qn, [(dq, None, 0)], _ep_id, [(BF16,)], m=d, n=d, k=s, bn=WIDE_BN)
    ga["w_k"], ga["w_v"] = _mm("kv_proj_bwd_dw", "tn", kvn, [(dk, None, 0), (dv, None, 0)], _ep_all,
                               [(BF16,), (BF16,)], m=d, n=d, k=s)
    pin = emit("attn", ga)
    (dqn,) = _mm("q_proj_bwd_dx", "nt", dq, [(wa["w_q"], None, 0)], _ep_id, [(BF16,)], m=s, n=d, k=d, after=pin,
                 bn=WIDE_BN)
    (dkvn,) = _mm("kv_proj_bwd_dx", "nt", [dk, dv], [(wa["w_k"], None, 0), (wa["w_v"], None, 0)], _ep_sum, [(BF16,)],
                  m=s, n=d, k=d, bn=WIDE_BN)
    dh2, dh2b, dg_kvq, _ = _rms_bwd("kvq_norm_bwd", h2, jnp.concatenate([w["kv_norm_g"], w["b_norm_g"]], axis=0),
                                    [dkvn, dqn], dh3)
    dh1, dh1b, dgain0, cs_h1 = _ffn_bwd("ffn0", h1, w["ffn_norm_g"][0:1], wf0["gate"], wf0["up"], wf0["down"], ffn0,
                                        dh2, dh2b, (), lambda part, grads: emit("ffn0" + part, grads))
    (dsw,) = _mm("conv_pw2_bwd_dx", "nt", dh1b, [(w_conv2, None, 0)], _ep_id, [(F32,)], m=s, n=d, k=d, bn=WIDE_BN)
    (dw2,) = _mm("conv_pw2_bwd_dw", "tn", sw, [(dh1b, None, 0)], _ep_id, [(BF16,)], m=d, n=d, k=s, bn=WIDE_BN)
    pin = emit("conv2", dict(conv_w2=dw2))
    dc, ln_sums = _conv_ln_bwd("conv_ln_bwd", c, dsw, w["conv_ln_g"], w["conv_ln_b"], after=pin)
    dpre, ddw, db1 = _conv_dw_bwd("conv_dw_bwd", dc, glu, a, gt, w["conv_dw"])
    (dw1,) = _mm("conv_pw1_bwd_dw", "tn", n1, [(dpre, None, 0)], _ep_id, [(BF16,)], m=d, n=2 * d, k=s)
    pin = emit("conv1", dict(conv_w1=dw1))
    (dn1,) = _mm("conv_pw1_bwd_dx", "nt", dpre, [(w_conv1, None, 0)], _ep_id, [(BF16,)], m=s, n=d, k=2 * d,
                 after=pin)
    dx, _, d_a_norm, _ = _rms_bwd("a_norm_bwd", x, w["a_norm_g"], [dn1], dh1)

    g.update(
        a_norm_g=d_a_norm, conv_b1=db1, conv_dw=ddw, conv_dw_b=ln_sums[2:3], conv_ln_g=ln_sums[0:1],
        conv_ln_b=ln_sums[1:2], conv_b2=cs_h1, kv_norm_g=dg_kvq[0:1], b_norm_g=dg_kvq[1:2],
        ffn_norm_g=jnp.concatenate([dgain0, dgain1], axis=0), final_norm_g=d_final_g,
    )
    return loss_cols, dx, g


HBM_SPEC = pl.BlockSpec(memory_space=pltpu.HBM)


def _mesh_place():
    x, y, c = lax.axis_index("x"), lax.axis_index("y"), lax.axis_index("c")
    chips = [(1 - x, y), (x, 1 - y), (1 - x, 1 - y)]
    return x, y, c, chips


def _shard_view(ref, kind, s, half=None):
    rows, cols = ref.shape
    if kind == "col":
        cw = cols // N_CHIPS
        if half is None:
            return ref.at[pl.ds(0, rows), pl.ds(s * cw, cw)]
        return ref.at[pl.ds(half * (rows // 2), rows // 2), pl.ds(s * cw, cw)]
    r = rows // N_CHIPS
    if half is None:
        return ref.at[pl.ds(s * r, r), pl.ds(0, cols)]
    return ref.at[pl.ds(s * r + half * (r // 2), r // 2), pl.ds(0, cols)]


def _full_shape(shard, kind):
    r, cw = shard.shape[-2:]
    return (r, cw * N_CHIPS) if kind == "col" else (r * N_CHIPS, cw)


def _gather_weights(shards, kinds, after=()):
    nt = len(shards)
    nf = len(after)
    fulls = [jax.ShapeDtypeStruct(_full_shape(sh, kind), sh.dtype) for sh, kind in zip(shards, kinds)]

    def body(*refs):
        src = refs[:nt]
        dst = refs[nt + nf:2 * nt + nf]
        send, recv, fsend, frecv, local = refs[2 * nt + nf:]
        x, y, c, chips = _mesh_place()
        s = 2 * x + y
        sib = (x, y, 1 - c)

        def half_of_shard(t):
            r, cw = src[t].shape
            return src[t].at[pl.ds(c * (r // 2), r // 2), pl.ds(0, cw)]

        locals_ = [pltpu.make_async_copy(src[t], _shard_view(dst[t], kinds[t], s), local.at[t]) for t in range(nt)]
        for cp in locals_:
            cp.start()
        sends = []
        for t in range(nt):
            for j, chip in enumerate(chips):
                cp = pltpu.make_async_remote_copy(
                    src_ref=half_of_shard(t), dst_ref=_shard_view(dst[t], kinds[t], s, c),
                    send_sem=send.at[t, j], recv_sem=recv.at[t, j], device_id=(*chip, c), device_id_type=MESH)
                cp.start()
                sends.append(cp)
        for t in range(nt):
            for j, (px, py) in enumerate(chips):
                landed = _shard_view(dst[t], kinds[t], 2 * px + py, c)
                pltpu.make_async_remote_copy(
                    src_ref=half_of_shard(t), dst_ref=landed, send_sem=send.at[t, j], recv_sem=recv.at[t, j],
                    device_id=(px, py, c), device_id_type=MESH).wait_recv()
                cp = pltpu.make_async_remote_copy(
                    src_ref=landed, dst_ref=landed, send_sem=fsend.at[t, j], recv_sem=frecv.at[t, j],
                    device_id=sib, device_id_type=MESH)
                cp.start()
                sends.append(cp)
        for t in range(nt):
            for j, (px, py) in enumerate(chips):
                other = _shard_view(dst[t], kinds[t], 2 * px + py, 1 - c)
                pltpu.make_async_remote_copy(
                    src_ref=other, dst_ref=other, send_sem=fsend.at[t, j], recv_sem=frecv.at[t, j],
                    device_id=sib, device_id_type=MESH).wait_recv()
        for cp in sends:
            cp.wait_send()
        for cp in locals_:
            cp.wait()

    return pl.pallas_call(
        body,
        name="gather_weights",
        in_specs=[HBM_SPEC] * nt + [ANY_SPEC] * nf,
        out_specs=[HBM_SPEC] * nt,
        out_shape=fulls,
        scratch_shapes=[pltpu.SemaphoreType.DMA((nt, 3))] * 4 + [pltpu.SemaphoreType.DMA((nt,))],
    )(*shards, *after)


def _row_blocks(rows, want=512):
    nb = 1
    while rows // nb > want or rows % nb or (rows // nb) % 16:
        nb += 1
        if nb > rows:
            return rows, 1
    return rows // nb, nb


def _place_own(name, src, kind, land, ids, lead=0):
    r, cw = src.shape[-2:]
    tr, nb = _row_blocks(r)
    if src.ndim == 3:
        src_spec = pl.BlockSpec((None, tr, cw), lambda i, ids_ref: (lead, i, 0))
    else:
        src_spec = pl.BlockSpec((tr, cw), lambda i, ids_ref: (i, 0))
    if kind == "col":
        dst_spec = pl.BlockSpec((tr, cw), lambda i, ids_ref: (i, ids_ref[0]))
    else:
        dst_spec = pl.BlockSpec((tr, cw), lambda i, ids_ref: (ids_ref[0] * nb + i, 0))

    def body(ids_ref, s_ref, o_ref):
        o_ref[...] = s_ref[...].astype(o_ref.dtype)

    return pl.pallas_call(
        body,
        name=name,
        grid_spec=pltpu.PrefetchScalarGridSpec(num_scalar_prefetch=1, grid=(nb,), in_specs=[src_spec],
                                               out_specs=dst_spec),
        out_shape=land,
        compiler_params=pltpu.CompilerParams(dimension_semantics=("parallel",)),
    )(ids, src)


SEM_SPEC = pl.BlockSpec(memory_space=pltpu.SEMAPHORE)
SIDE_EFFECT = pltpu.SideEffectType.DATAFLOW_SIDE_EFFECTING


def _copies_start(name, srcs, lands, copies, n_sems, after):
    ns, nl, nf = len(srcs), len(lands), len(after)

    def body(*refs):
        src, land = refs[:ns], refs[ns:ns + nl]
        send, recv = refs[ns + nl + nf], refs[ns + nl + nf + 1]
        pin = refs[-1]
        for cp in copies(src, land, send, recv, _mesh_place()):
            cp.start()
        pin[...] = jnp.zeros_like(pin)

    arrs = list(srcs) + list(lands)
    res = pl.pallas_call(
        body,
        name=name,
        in_specs=[HBM_SPEC] * (ns + nl) + [ANY_SPEC] * nf,
        out_specs=[SEM_SPEC, SEM_SPEC] + [HBM_SPEC] * (ns + nl) + [pl.BlockSpec(memory_space=pltpu.VMEM)],
        out_shape=[pltpu.SemaphoreType.DMA((n_sems,)), pltpu.SemaphoreType.DMA((n_sems,))]
        + [pltpu.HBM(a.shape, a.dtype) for a in arrs] + [jax.ShapeDtypeStruct((8, 128), F32)],
        input_output_aliases={i: 2 + i for i in range(ns + nl)},
        compiler_params=pltpu.CompilerParams(has_side_effects=SIDE_EFFECT),
    )(*[pltpu.with_memory_space_constraint(a, pltpu.HBM) for a in arrs], *after)
    return res[0], res[1], list(res[2:2 + ns]), list(res[2 + ns:2 + ns + nl]), res[-1]


def _copies_wait(name, send, recv, srcs, lands, copies, after):
    ns, nl, nf = len(srcs), len(lands), len(after)

    def body(*refs):
        src, land = refs[:ns], refs[ns:ns + nl]
        send_sems, recv_sems = refs[ns + nl], refs[ns + nl + 1]
        cps = copies(src, land, send_sems, recv_sems, _mesh_place())
        for cp in cps:
            cp.wait_send()
        for cp in cps:
            cp.wait_recv()

    arrs = list(srcs) + list(lands)
    res = pl.pallas_call(
        body,
        name=name,
        in_specs=[HBM_SPEC] * (ns + nl) + [SEM_SPEC, SEM_SPEC] + [ANY_SPEC] * nf,
        out_specs=[HBM_SPEC] * (ns + nl),
        out_shape=[pltpu.HBM(a.shape, a.dtype) for a in arrs],
        input_output_aliases={i: i for i in range(ns + nl)},
        compiler_params=pltpu.CompilerParams(has_side_effects=SIDE_EFFECT),
    )(*arrs, send, recv, *after)
    return list(res[:ns]), list(res[ns:])


def _gather_copies(kinds, halves):
    def copies(src, land, send, recv, place):
        x, y, c, chips = place
        s = 2 * x + y
        mine = [_shard_view(land[t], kinds[t], s, c if halves else None) for t in range(len(kinds))]
        return [
            pltpu.make_async_remote_copy(
                src_ref=mine[t], dst_ref=mine[t], send_sem=send.at[3 * t + j], recv_sem=recv.at[3 * t + j],
                device_id=(px, py, c), device_id_type=MESH)
            for t in range(len(kinds)) for j, (px, py) in enumerate(chips)
        ]

    return copies


def _gather_pass_copies(kinds):
    def copies(src, land, send, recv, place):
        x, y, c, chips = place
        views = [_shard_view(land[t], kinds[t], 2 * px + py, c) for t in range(len(kinds)) for px, py in chips]
        return [
            pltpu.make_async_remote_copy(src_ref=v, dst_ref=v, send_sem=send.at[i], recv_sem=recv.at[i],
                                         device_id=(x, y, 1 - c), device_id_type=MESH)
            for i, v in enumerate(views)
        ]

    return copies


def _gather_pass_on(name, lands, kinds):
    nt = len(lands)

    def body(*refs):
        buf = refs[nt:2 * nt]
        send, recv = refs[2 * nt:]
        x, y, c, chips = _mesh_place()
        sib = (x, y, 1 - c)
        sends = []
        for t in range(nt):
            for j, (px, py) in enumerate(chips):
                mine = _shard_view(buf[t], kinds[t], 2 * px + py, c)
                cp = pltpu.make_async_remote_copy(src_ref=mine, dst_ref=mine, send_sem=send.at[t, j],
                                                  recv_sem=recv.at[t, j], device_id=sib, device_id_type=MESH)
                cp.start()
                sends.append(cp)
        for t in range(nt):
            for j, (px, py) in enumerate(chips):
                theirs = _shard_view(buf[t], kinds[t], 2 * px + py, 1 - c)
                pltpu.make_async_remote_copy(src_ref=theirs, dst_ref=theirs, send_sem=send.at[t, j],
                                             recv_sem=recv.at[t, j], device_id=sib, device_id_type=MESH).wait_recv()
        for cp in sends:
            cp.wait_send()

    return pl.pallas_call(
        body,
        name=name,
        in_specs=[HBM_SPEC] * nt,
        out_specs=[HBM_SPEC] * nt,
        out_shape=[jax.ShapeDtypeStruct(a.shape, a.dtype) for a in lands],
        input_output_aliases={i: i for i in range(nt)},
        scratch_shapes=[pltpu.SemaphoreType.DMA((nt, 3))] * 2,
    )(*lands)


def _grad_part(ref, kind, s):
    return ref if kind == "all" else _shard_view(ref, kind, s)


def _grad_copies(kinds):
    def peers(place):
        x, y, c, chips = place
        return [(x, y, 1 - c)] + [(px, py, c) for px, py in chips]

    def copies(src, land, send, recv, place):
        x, y, c, chips = place
        me = 4 * x + 2 * y + c
        return [
            pltpu.make_async_remote_copy(
                src_ref=_grad_part(src[t], kinds[t], 2 * px + py), dst_ref=land[t].at[me],
                send_sem=send.at[GRAD_PEERS * t + k], recv_sem=recv.at[GRAD_PEERS * t + k], device_id=(px, py, pc),
                device_id_type=MESH)
            for t in range(len(kinds)) for k, (px, py, pc) in enumerate(peers(place))
        ]

    return copies


def _pass_copies(n):
    def copies(src, land, send, recv, place):
        x, y, c, chips = place
        return [
            pltpu.make_async_remote_copy(
                src_ref=land[t].at[4 * px + 2 * py + c], dst_ref=land[t].at[4 * px + 2 * py + c],
                send_sem=send.at[3 * t + j], recv_sem=recv.at[3 * t + j], device_id=(x, y, 1 - c),
                device_id_type=MESH)
            for t in range(n) for j, (px, py) in enumerate(chips)
        ]

    return copies


GRAD_PEERS = 4


def _land_shape(grad, kind):
    rows, cols = grad.shape
    if kind == "col":
        return (N_DEV, rows, cols // N_CHIPS)
    if kind == "row":
        return (N_DEV, rows // N_CHIPS, cols)
    return (N_DEV, rows, cols)


def _adamw_reduce(name, contrib, own, kind, ids, w, m, v, layer=None, prev=None):
    rows, cols = w.shape[-2:]
    t = rows
    for cand in (256, 128):
        if rows % cand == 0 and cand * cols <= 256 * 1408:
            t = cand
            break
    nb = rows // t
    c1 = 1.0 - ADAM_B1 ** ADAM_STEP
    c2 = 1.0 - ADAM_B2 ** ADAM_STEP

    n_prev = 0 if prev is None else 4

    def body(ids_ref, c_ref, own_ref, w_ref, m_ref, v_ref, *refs):
        g_ref, d_ref, nm_ref, nv_ref = refs[n_prev:]
        me = ids_ref[1]
        mine = own_ref[...].astype(F32)
        g = None
        for q in range(N_DEV):
            term = jnp.where(me == q, mine, c_ref[q].astype(F32))
            g = term if g is None else g + term
        nm = ADAM_B1 * m_ref[...] + (1.0 - ADAM_B1) * g
        nv = ADAM_B2 * v_ref[...] + (1.0 - ADAM_B2) * (g * g)
        g_ref[...] = g
        nm_ref[...] = nm
        nv_ref[...] = nv
        d_ref[...] = -ADAM_LR * ((nm / c1) / (jnp.sqrt(nv / c2) + ADAM_EPS) + ADAM_WD * w_ref[...])

    if layer is None:
        blk = pl.BlockSpec((t, cols), lambda i, ids_ref: (i, 0))
    else:
        blk = pl.BlockSpec((None, t, cols), lambda i, ids_ref: (layer, i, 0))
    if kind == "col":
        own_spec = pl.BlockSpec((t, cols), lambda i, ids_ref: (i, ids_ref[0]))
    elif kind == "row":
        own_spec = pl.BlockSpec((t, cols), lambda i, ids_ref: (ids_ref[0] * nb + i, 0))
    else:
        own_spec = pl.BlockSpec((t, cols), lambda i, ids_ref: (i, 0))
    return pl.pallas_call(
        body,
        name=name,
        grid_spec=pltpu.PrefetchScalarGridSpec(
            num_scalar_prefetch=1,
            grid=(nb,),
            in_specs=[pl.BlockSpec((N_DEV, t, cols), lambda i, ids_ref: (0, i, 0)), own_spec, blk, blk, blk]
            + [ANY_SPEC] * n_prev,
            out_specs=[blk] * 4,
        ),
        out_shape=[jax.ShapeDtypeStruct(w.shape, F32)] * 4,
        input_output_aliases={6 + i: i for i in range(n_prev)},
        compiler_params=pltpu.CompilerParams(dimension_semantics=("parallel",)),
    )(ids, contrib, own, w, m, v, *(prev or ()))


WEIGHT_NAMES = ("a_norm_g", "conv_w1", "conv_b1", "conv_dw", "conv_dw_b", "conv_ln_g", "conv_ln_b", "conv_w2",
                "conv_b2", "kv_norm_g", "w_k", "w_v", "b_norm_g", "w_q", "w_o", "ffn_norm_g", "ffn_w_gate",
                "ffn_w_up", "ffn_w_down", "final_norm_g")
GROUPS = {
    "conv2": (("conv_w2", "conv_w2", None, "row"),),
    "ffn0": (("gate", "ffn_w_gate", 0, "col"), ("up", "ffn_w_up", 0, "col"), ("down", "ffn_w_down", 0, "row")),
    "attn": (("w_k", "w_k", None, "row"), ("w_v", "w_v", None, "row"), ("w_q", "w_q", None, "row"),
             ("w_o", "w_o", None, "row")),
    "ffn1": (("gate", "ffn_w_gate", 1, "col"), ("up", "ffn_w_up", 1, "col"), ("down", "ffn_w_down", 1, "row")),
}
GROUPS["conv1"] = (("conv_w1", "conv_w1", None, "col"),)
for _layer in (0, 1):
    GROUPS[f"ffn{_layer}d"] = tuple(it for it in GROUPS[f"ffn{_layer}"] if it[0] == "down")
    GROUPS[f"ffn{_layer}gu"] = tuple(it for it in GROUPS[f"ffn{_layer}"] if it[0] != "down")
FETCH_ORDER = ("conv1", "conv2", "ffn0gu", "ffn0d", "attn", "ffn1")
HALVED = ("conv1", "conv2", "ffn0gu", "ffn0d", "attn", "ffn1")
EMIT_ORDER = ("ffn1d", "ffn1gu", "attn", "ffn0d", "ffn0gu", "conv2", "conv1", "vec")
RETIRE_AT = {"attn": ("ffn1d", "ffn1gu"), "ffn0d": ("attn",), "conv1": ("ffn0d", "ffn0gu", "conv2")}
PACKED = (("a_norm_g", 0, 1), ("conv_b1", 8, 2), ("conv_dw", 16, CONV_WIDTH), ("conv_dw_b", 48, 1),
          ("conv_ln_g", 56, 1), ("conv_ln_b", 64, 1), ("conv_b2", 72, 1))
PACK_ROWS = 80
WHOLE = (("kv_norm_g", 0, 1), ("b_norm_g", 1, 1), ("ffn_norm_g", 2, 2), ("final_norm_g", 4, 1))
WHOLE_ROWS = 8


def _pack_rows(parts, total, width):
    out, at = [], 0
    for arr, first in parts:
        if first > at:
            out.append(jnp.zeros((first - at, width), F32))
        rows8 = -(-arr.shape[0] // 8) * 8
        out.append(jnp.pad(arr, ((0, rows8 - arr.shape[0]), (0, 0))))
        at = first + rows8
    if total > at:
        out.append(jnp.zeros((total - at, width), F32))
    return jnp.concatenate(out, axis=0)


def kernel(x, a_norm_g, conv_w1, conv_b1, conv_dw, conv_dw_b, conv_ln_g, conv_ln_b, conv_w2, conv_b2, kv_norm_g, w_k, w_v, b_norm_g, w_q, w_o, ffn_norm_g, ffn_w_gate, ffn_w_up, ffn_w_down, final_norm_g, loss_target, m_a_norm_g, m_conv_w1, m_conv_b1, m_conv_dw, m_conv_dw_b, m_conv_ln_g, m_conv_ln_b, m_conv_w2, m_conv_b2, m_kv_norm_g, m_w_k, m_w_v, m_b_norm_g, m_w_q, m_w_o, m_ffn_norm_g, m_ffn_w_gate, m_ffn_w_up, m_ffn_w_down, m_final_norm_g, v_a_norm_g, v_conv_w1, v_conv_b1, v_conv_dw, v_conv_dw_b, v_conv_ln_g, v_conv_ln_b, v_conv_w2, v_conv_b2, v_kv_norm_g, v_w_k, v_w_v, v_b_norm_g, v_w_q, v_w_o, v_ffn_norm_g, v_ffn_w_gate, v_ffn_w_up, v_ffn_w_down, v_final_norm_g):
    args = locals()
    wts = {n: args[n] for n in WEIGHT_NAMES}
    mom = {n: args["m_" + n] for n in WEIGHT_NAMES}
    vel = {n: args["v_" + n] for n in WEIGHT_NAMES}
    s, d = x.shape[-2:]
    dq = d // N_CHIPS
    x2 = x.reshape(s, d)
    tgt = loss_target.reshape(s, d)

    def pack_shard(src):
        return _pack_rows([(src[n].reshape(-1, dq), first) for n, first, _ in PACKED], PACK_ROWS, dq)

    def pack_whole(src):
        return _pack_rows([(jnp.concatenate([src[n].reshape(-1, d) for n, _, _ in WHOLE], axis=0), 0)], WHOLE_ROWS, d)

    gathers = {}
    pins = []
    ids = jnp.stack([2 * lax.axis_index("x") + lax.axis_index("y"),
                     4 * lax.axis_index("x") + 2 * lax.axis_index("y") + lax.axis_index("c")]).astype(jnp.int32)

    def start_gather(grp, pins):
        kinds = [kind for _, _, _, kind in GROUPS[grp]]
        lands = []
        for key, n, layer, kind in GROUPS[grp]:
            shard = wts[n] if layer is not None else wts[n].reshape(wts[n].shape[-2:])
            full = jax.ShapeDtypeStruct(_full_shape(shard, kind), BF16)
            lands.append(_place_own(f"gather_place_{grp}_{key}", shard, kind, full, ids, layer))
        copies = _gather_copies(kinds, grp in HALVED)
        send, recv, _, lands, pin = _copies_start("gather_start_" + grp, [], lands, copies, 3 * len(lands), pins)
        gathers[grp] = (send, recv, [], lands, copies, kinds)
        return [pin]

    (packed_full,) = _gather_weights([pack_shard(wts)], ["row"])
    w = {}
    pins = start_gather(FETCH_ORDER[0], [packed_full])
    started = {}
    swapping = {}

    def prefetch(grp, after):
        if grp == "rest":
            last = [after]
            for later in FETCH_ORDER[1:]:
                last = start_gather(later, last)
            started["pin"] = last
            return
        send, recv, srcs, lands, copies, kinds = gathers[grp]
        _, fulls = _copies_wait("gather_wait_" + grp, send, recv, srcs, lands, copies, [after])
        copies = _gather_pass_copies(kinds)
        send, recv, _, fulls, _ = _copies_start("gather_swap_" + grp, [], fulls, copies, 3 * len(fulls), ())
        swapping[grp] = (send, recv, fulls, copies)

    def fetch(grp, after):
        if grp in swapping:
            send, recv, fulls, copies = swapping[grp]
            _, fulls = _copies_wait("gather_swapped_" + grp, send, recv, [], fulls, copies, [after])
        else:
            send, recv, srcs, lands, copies, kinds = gathers[grp]
            behind = [after] + (started.get("pin", []) if grp == FETCH_ORDER[0] else [])
            _, fulls = _copies_wait("gather_wait_" + grp, send, recv, srcs, lands, copies, behind)
            if grp in HALVED:
                fulls = _gather_pass_on("gather_pass_" + grp, fulls, kinds)
        return {key: full for (key, _, _, _), full in zip(GROUPS[grp], fulls)}

    packed = packed_full.reshape(N_CHIPS, PACK_ROWS, dq)
    for n, first, rows in PACKED:
        part = packed[:, first:first + rows, :]
        if n == "conv_dw":
            w[n] = part.transpose(1, 0, 2).reshape(rows, d)
        else:
            w[n] = part.reshape(1, N_CHIPS * rows * dq)
    for n, _, rows in WHOLE:
        w[n] = wts[n].reshape(rows, d)

    exchanges = {}
    passing = {}
    own_grads = {}

    def retire(tag, after):
        send, recv, srcs, lands, copies, kinds = exchanges[tag]
        srcs, lands = _copies_wait("grads_wait_" + tag, send, recv, srcs, lands, copies, after)
        own_grads[tag] = list(zip(srcs, kinds))
        copies = _pass_copies(len(lands))
        send, recv, _, lands, pin = _copies_start("grads_pass_" + tag, [], lands, copies, 3 * len(lands), ())
        passing[tag] = (send, recv, lands, copies)
        return pin

    def emit_grads(tag, grads, kinds):
        pins = [retire(old, [grads[0]]) for old in RETIRE_AT.get(tag, ())]
        copies = _grad_copies(kinds)
        lands = [lax.empty(_land_shape(gr, kd), gr.dtype) for gr, kd in zip(grads, kinds)]
        send, recv, srcs, lands, pin = _copies_start("grads_start_" + tag, grads, lands, copies,
                                                     GRAD_PEERS * len(grads), pins)
        exchanges[tag] = (send, recv, srcs, lands, copies, kinds)
        return [pin]

    def emit(grp, grads):
        return emit_grads(grp, [grads[key] for key, _, _, _ in GROUPS[grp]], [kind for _, _, _, kind in GROUPS[grp]])

    loss_cols, dx, g = _local_step(x2, tgt, w, fetch, emit, prefetch, after=pins)
    loss = lax.psum(jnp.sum(loss_cols), ("x", "y", "c"))

    gp = []
    for n, first, rows in PACKED:
        if n == "conv_dw":
            part = g[n].reshape(rows, N_CHIPS, dq).transpose(1, 0, 2)
        else:
            part = g[n].reshape(N_CHIPS, rows, dq)
        gp.append((part, first))
    g_packed = jnp.concatenate(
        [_pack_rows([(p[ci], first) for p, first in gp], PACK_ROWS, dq) for ci in range(N_CHIPS)], axis=0)
    emit_grads("vec", [g_packed, pack_whole(g)], ["row", "all"])
    late = [tag for tag in EMIT_ORDER if tag not in passing]
    contribs = {}

    def arrive(tag, after):
        send, recv, lands, copies = passing[tag]
        _, arrived = _copies_wait("grads_passed_" + tag, send, recv, [], lands, copies, after)
        contribs[tag] = [(c, own, kind) for c, (own, kind) in zip(arrived, own_grads[tag])]

    for tag in EMIT_ORDER:
        if tag not in late:
            arrive(tag, [dx])
    res = {}

    def adamw(n, contrib, layer=None, prev=None):
        arrived, own, kind = contrib
        if layer is None:
            shape = wts[n].shape
            r2 = shape[-2:]
            outs = _adamw_reduce("adamw_" + n, arrived, own, kind, ids, wts[n].reshape(r2), mom[n].reshape(r2),
                                 vel[n].reshape(r2))
            return [o.reshape(shape) for o in outs]
        return _adamw_reduce(f"adamw_{n}_{layer}", arrived, own, kind, ids, wts[n], mom[n], vel[n], layer, prev)

    for grp in ("attn", "conv2"):
        for (key, n, _, _), contrib in zip(GROUPS[grp], contribs[grp]):
            res[n] = adamw(n, contrib)
    for part in ("d", "gu"):
        for (key, n, _, _), c0, c1 in zip(GROUPS["ffn0" + part], contribs["ffn0" + part], contribs["ffn1" + part]):
            res[n] = adamw(n, c1, 1, adamw(n, c0, 0))
    behind = [res["ffn_w_up"][0]]
    for tag in late:
        retire(tag, behind)
    for tag in late:
        arrive(tag, behind)
    for (key, n, _, _), contrib in zip(GROUPS["conv1"], contribs["conv1"]):
        res[n] = adamw(n, contrib)
    outs = _adamw_reduce("adamw_packed", *contribs["vec"][0], ids, pack_shard(wts), pack_shard(mom), pack_shard(vel))
    for n, first, rows in PACKED:
        res[n] = [o[first:first + rows].reshape(wts[n].shape) for o in outs]
    outs = _adamw_reduce("adamw_whole", *contribs["vec"][1], ids, pack_whole(wts), pack_whole(mom), pack_whole(vel))
    for n, first, rows in WHOLE:
        res[n] = [o[first:first + rows].reshape(wts[n].shape) for o in outs]

    out = [loss, dx.reshape(x.shape)]
    for which in range(4):
        out += [res[n][which] for n in WEIGHT_NAMES]
    return tuple(out)
```

```python
import functools

import jax
import jax.numpy as jnp
from jax import lax
from jax.experimental import pallas as pl
from jax.experimental.pallas import tpu as pltpu

F32 = jnp.float32
BF16 = jnp.bfloat16

HEAD_DIM = 128
BRANCHES = ((128, 1), (512, 4), (2048, 16))
CONV_WIDTH = 31
CONV_HALO = 32
RMS_EPS = 1e-6
LN_EPS = 1e-5
ADAM_LR = 0.001
ADAM_B1 = 0.9
ADAM_B2 = 0.999
ADAM_EPS = 1e-08
ADAM_WD = 0.01
ADAM_STEP = 10
N_CHIPS = 4
N_DEV = 8
MESH = pl.DeviceIdType.MESH


def _sigmoid(x):
    return 0.5 * jnp.tanh(0.5 * x) + 0.5


def _row_tile(rows, want):
    t = min(rows, want)
    assert rows % t == 0, (rows, want)
    return t


_DOT_DIMS = {"nn": ((1,), (0,)), "nt": ((1,), (1,)), "tn": ((0,), (0,))}


ANY_SPEC = pl.BlockSpec(memory_space=pl.ANY)
WIDE_BN = 1024


def _mm(name, mode, a, bs, epilogue, outs, *, m, n, k, extras=(), bm=1024, bn=512, bk=None, after=()):
    bm, bn = min(bm, m), min(bn, n)
    bk = k if bk is None else min(bk, k)
    assert m % bm == 0 and n % bn == 0 and k % bk == 0, (name, m, n, k, bm, bn, bk)
    nk = k // bk
    a_list = list(a) if isinstance(a, (list, tuple)) else [a]
    na, nb, ne, no = len(a_list), len(bs), len(extras), len(outs)
    assert na in (1, nb)

    if mode == "tn":
        a_spec = pl.BlockSpec((bk, bm), lambda i, j, kk: (kk, i))
    else:
        a_spec = pl.BlockSpec((bm, bk), lambda i, j, kk: (i, kk))

    def b_spec(lead, off):
        if mode == "nt":
            blk, idx = (bn, bk), (lambda i, j, kk: (j + off, kk))
        else:
            blk, idx = (bk, bn), (lambda i, j, kk: (kk, j + off))
        if lead is None:
            return pl.BlockSpec(blk, idx)
        return pl.BlockSpec((None,) + blk, lambda i, j, kk: (lead,) + idx(i, j, kk))

    def e_spec(kind, off):
        if kind == "mn":
            return pl.BlockSpec((bm, bn), lambda i, j, kk: (i, j + off))
        return pl.BlockSpec((1, bn), lambda i, j, kk: (0, j + off))

    nf = len(after)
    in_specs = [a_spec] * na + [b_spec(l, o) for _, l, o in bs] + [e_spec(kd, o) for _, kd, o in extras]
    in_specs += [ANY_SPEC] * nf
    out_specs = [pl.BlockSpec((bm, bn), lambda i, j, kk: (i, j)) for _ in outs]
    out_shape = [jax.ShapeDtypeStruct((m, n), dt) for (dt,) in outs]
    dims = (_DOT_DIMS[mode], ((), ()))

    def body(*refs):
        a_refs = refs[:na]
        b_refs = refs[na:na + nb]
        e_refs = refs[na + nb:na + nb + ne]
        o_refs = refs[na + nb + ne + nf:na + nb + ne + nf + no]
        acc_refs = refs[na + nb + ne + nf + no:]
        avs = [a_ref[...].astype(BF16) for a_ref in a_refs]
        prods = [lax.dot_general(avs[bi % na], b_ref[...].astype(BF16), dims, preferred_element_type=F32)
                 for bi, b_ref in enumerate(b_refs)]

        def finish(accs):
            res = epilogue(accs, [e_ref[...] for e_ref in e_refs])
            for o_ref, r in zip(o_refs, res):
                o_ref[...] = r.astype(o_ref.dtype)

        if nk == 1:
            finish(prods)
        else:
            kk = pl.program_id(2)

            @pl.when(kk == 0)
            def _():
                for acc_ref, p in zip(acc_refs, prods):
                    acc_ref[...] = p

            @pl.when(kk > 0)
            def _():
                for acc_ref, p in zip(acc_refs, prods):
                    acc_ref[...] += p

            @pl.when(kk == nk - 1)
            def _():
                finish([acc_ref[...] for acc_ref in acc_refs])

    scratch = [] if nk == 1 else [pltpu.VMEM((bm, bn), F32) for _ in bs]
    res = pl.pallas_call(
        body,
        name=name,
        grid=(m // bm, n // bn, nk),
        in_specs=in_specs,
        out_specs=out_specs,
        out_shape=out_shape,
        scratch_shapes=scratch,
        compiler_params=pltpu.CompilerParams(dimension_semantics=("parallel", "parallel", "arbitrary")),
    )(*a_list, *[b for b, _, _ in bs], *[e for e, _, _ in extras], *after)
    return res


def _rms_fwd(name, x, gains, after=()):
    s, d = x.shape
    ng = gains.shape[0]
    t = _row_tile(s, 256)
    nf = len(after)

    def body(x_ref, g_ref, *refs):
        o_refs = refs[nf:]
        xv = x_ref[...]
        r = lax.rsqrt(jnp.mean(xv * xv, axis=-1, keepdims=True) + RMS_EPS)
        xh = xv * r
        for gi, o_ref in enumerate(o_refs):
            o_ref[...] = (xh * g_ref[gi:gi + 1, :]).astype(o_ref.dtype)

    return pl.pallas_call(
        body,
        name=name,
        grid=(s // t,),
        in_specs=[pl.BlockSpec((t, d), lambda i: (i, 0)), pl.BlockSpec((ng, d), lambda i: (0, 0))] + [ANY_SPEC] * nf,
        out_specs=[pl.BlockSpec((t, d), lambda i: (i, 0)) for _ in range(ng)],
        out_shape=[jax.ShapeDtypeStruct((s, d), BF16) for _ in range(ng)],
        compiler_params=pltpu.CompilerParams(dimension_semantics=("parallel",)),
    )(x, gains, *after)


def _rms_bwd(name, x, gains, dns, dres):
    s, d = x.shape
    ng = gains.shape[0]
    t = _row_tile(s, 256)

    def body(x_ref, g_ref, dres_ref, *refs):
        dn_refs = refs[:ng]
        dx_ref, dxb_ref, dg_ref, cs_ref = refs[ng:]
        i = pl.program_id(0)
        xv = x_ref[...]
        r = lax.rsqrt(jnp.mean(xv * xv, axis=-1, keepdims=True) + RMS_EPS)
        xh = xv * r
        dx = dres_ref[...]
        dgs = []
        for gi in range(ng):
            dn = dn_refs[gi][...].astype(F32)
            dxh = dn * g_ref[gi:gi + 1, :]
            dgs.append(jnp.sum(dn * xh, axis=0, keepdims=True))
            dx = dx + r * (dxh - xh * jnp.mean(dxh * xh, axis=-1, keepdims=True))
        dx_ref[...] = dx
        dxb_ref[...] = dx.astype(BF16)
        dg = jnp.concatenate(dgs, axis=0) if ng > 1 else dgs[0]
        cs = jnp.sum(dx, axis=0, keepdims=True)

        @pl.when(i == 0)
        def _():
            dg_ref[...] = dg
            cs_ref[...] = cs

        @pl.when(i > 0)
        def _():
            dg_ref[...] += dg
            cs_ref[...] += cs

    row = pl.BlockSpec((t, d), lambda i: (i, 0))
    return pl.pallas_call(
        body,
        name=name,
        grid=(s // t,),
        in_specs=[row, pl.BlockSpec((ng, d), lambda i: (0, 0)), row] + [row] * ng,
        out_specs=[row, row, pl.BlockSpec((ng, d), lambda i: (0, 0)), pl.BlockSpec((1, d), lambda i: (0, 0))],
        out_shape=[
            jax.ShapeDtypeStruct((s, d), F32),
            jax.ShapeDtypeStruct((s, d), BF16),
            jax.ShapeDtypeStruct((ng, d), F32),
            jax.ShapeDtypeStruct((1, d), F32),
        ],
        compiler_params=pltpu.CompilerParams(dimension_semantics=("arbitrary",)),
    )(x, gains, dres, *dns)


def _final_loss(name, h, gain, target):
    s, d = h.shape
    t = _row_tile(s, 256)

    def body(h_ref, g_ref, t_ref, dh_ref, dhb_ref, dg_ref, ls_ref):
        i = pl.program_id(0)
        xv = h_ref[...]
        g = g_ref[...]
        r = lax.rsqrt(jnp.mean(xv * xv, axis=-1, keepdims=True) + RMS_EPS)
        xh = xv * r
        err = xh * g - t_ref[...]
        ls = jnp.sum(err * err, axis=0, keepdims=True) * (0.5 / d)
        dy = err * (1.0 / d)
        dxh = dy * g
        dg = jnp.sum(dy * xh, axis=0, keepdims=True)
        dx = r * (dxh - xh * jnp.mean(dxh * xh, axis=-1, keepdims=True))
        dh_ref[...] = dx
        dhb_ref[...] = dx.astype(BF16)

        @pl.when(i == 0)
        def _():
            dg_ref[...] = dg
            ls_ref[...] = ls

        @pl.when(i > 0)
        def _():
            dg_ref[...] += dg
            ls_ref[...] += ls

    row = pl.BlockSpec((t, d), lambda i: (i, 0))
    vec = pl.BlockSpec((1, d), lambda i: (0, 0))
    return pl.pallas_call(
        body,
        name=name,
        grid=(s // t,),
        in_specs=[row, vec, row],
        out_specs=[row, row, vec, vec],
        out_shape=[
            jax.ShapeDtypeStruct((s, d), F32),
            jax.ShapeDtypeStruct((s, d), BF16),
            jax.ShapeDtypeStruct((1, d), F32),
            jax.ShapeDtypeStruct((1, d), F32),
        ],
        compiler_params=pltpu.CompilerParams(dimension_semantics=("arbitrary",)),
    )(h, gain, target)


SUBLANES = 8
CONV_LANES = 512
CONV_ROWS = 32
NORM_ROWS = 64


def _conv_tiles(s):
    t = _row_tile(s, 128)
    assert t % CONV_HALO == 0 and t % CONV_ROWS == 0
    return t, t // CONV_HALO


def _shifted_copies(dst, src, lanes, rows):
    for m in range(SUBLANES):
        n = rows if m == 0 else rows - SUBLANES
        dst[m, :n, :] = src[m:m + n, lanes]


def _shifted(copies, offset, n):
    m = offset % SUBLANES
    return copies[m, offset - m:offset - m + n, :]


def _dwconv_fwd(name, u, dw, dw_b, ln_g, ln_b):
    s, d = u.shape
    t, hb = _conv_tiles(s)
    w = dw.shape[0]
    lo = CONV_HALO - (w - 1)

    lw = min(CONV_LANES, d)

    def body(cur_ref, prev_ref, dw_ref, dwb_ref, lg_ref, lb_ref, c_ref, sw_ref, cat_ref, sh_ref):
        i = pl.program_id(0)
        cat_ref[CONV_HALO:, :] = cur_ref[...]

        @pl.when(i == 0)
        def _():
            cat_ref[:CONV_HALO, :] = jnp.zeros((CONV_HALO, d), F32)

        @pl.when(i > 0)
        def _():
            cat_ref[:CONV_HALO, :] = prev_ref[...]

        for lc in range(d // lw):
            lanes = slice(lc * lw, (lc + 1) * lw)
            _shifted_copies(sh_ref, cat_ref, lanes, t + CONV_HALO)
            for rc in range(t // CONV_ROWS):
                acc = jnp.broadcast_to(dwb_ref[:, lanes], (CONV_ROWS, lw))
                for kk in range(w):
                    acc = acc + dw_ref[kk:kk + 1, lanes] * _shifted(sh_ref, lo + kk + rc * CONV_ROWS, CONV_ROWS)
                c_ref[rc * CONV_ROWS:(rc + 1) * CONV_ROWS, lanes] = acc

        def norm_rows(ri, carry):
            rows = pl.ds(pl.multiple_of(ri * NORM_ROWS, NORM_ROWS), NORM_ROWS)
            cv = c_ref[rows, :]
            cc = cv - jnp.mean(cv, axis=-1, keepdims=True)
            var = jnp.mean(cc * cc, axis=-1, keepdims=True)
            ln = cc * lax.rsqrt(var + LN_EPS) * lg_ref[...] + lb_ref[...]
            sw_ref[rows, :] = (ln * _sigmoid(ln)).astype(BF16)
            return carry

        lax.fori_loop(0, t // NORM_ROWS, norm_rows, 0)

    row = pl.BlockSpec((t, d), lambda i: (i, 0))
    prev = pl.BlockSpec((CONV_HALO, d), lambda i: (jnp.maximum(i * hb - 1, 0), 0))
    vec = pl.BlockSpec((1, d), lambda i: (0, 0))
    return pl.pallas_call(
        body,
        name=name,
        grid=(s // t,),
        in_specs=[row, prev, pl.BlockSpec((w, d), lambda i: (0, 0)), vec, vec, vec],
        out_specs=[row, row],
        out_shape=[jax.ShapeDtypeStruct((s, d), F32), jax.ShapeDtypeStruct((s, d), BF16)],
        scratch_shapes=[pltpu.VMEM((CONV_HALO + t, d), F32), pltpu.VMEM((SUBLANES, CONV_HALO + t, lw), F32)],
        compiler_params=pltpu.CompilerParams(dimension_semantics=("parallel",)),
    )(u, u, dw, dw_b, ln_g, ln_b)


def _conv_ln_bwd(name, c, dsw, ln_g, ln_b, after=()):
    s, d = c.shape
    t = _row_tile(s, 256)
    nf = len(after)

    def body(c_ref, dsw_ref, lg_ref, lb_ref, *refs):
        dc_ref, sums_ref = refs[nf:]
        i = pl.program_id(0)
        cv = c_ref[...]
        g = lg_ref[...]
        mu = jnp.mean(cv, axis=-1, keepdims=True)
        cc = cv - mu
        rstd = lax.rsqrt(jnp.mean(cc * cc, axis=-1, keepdims=True) + LN_EPS)
        ch = cc * rstd
        ln = ch * g + lb_ref[...]
        sg = _sigmoid(ln)
        dln = dsw_ref[...] * (sg * (1.0 + ln * (1.0 - sg)))
        dch = dln * g
        dc = rstd * (dch - jnp.mean(dch, axis=-1, keepdims=True) - ch * jnp.mean(dch * ch, axis=-1, keepdims=True))
        dc_ref[...] = dc
        sums = jnp.concatenate(
            [
                jnp.sum(dln * ch, axis=0, keepdims=True),
                jnp.sum(dln, axis=0, keepdims=True),
                jnp.sum(dc, axis=0, keepdims=True),
                jnp.zeros((1, d), F32),
            ],
            axis=0,
        )

        @pl.when(i == 0)
        def _():
            sums_ref[...] = sums

        @pl.when(i > 0)
        def _():
            sums_ref[...] += sums

    row = pl.BlockSpec((t, d), lambda i: (i, 0))
    vec = pl.BlockSpec((1, d), lambda i: (0, 0))
    return pl.pallas_call(
        body,
        name=name,
        grid=(s // t,),
        in_specs=[row, row, vec, vec] + [ANY_SPEC] * nf,
        out_specs=[row, pl.BlockSpec((4, d), lambda i: (0, 0))],
        out_shape=[jax.ShapeDtypeStruct((s, d), F32), jax.ShapeDtypeStruct((4, d), F32)],
        compiler_params=pltpu.CompilerParams(dimension_semantics=("arbitrary",)),
    )(c, dsw, ln_g, ln_b, *after)


def _conv_dw_bwd(name, dc, u, a, gt, dw):
    s, d = dc.shape
    t, hb = _conv_tiles(s)
    w = dw.shape[0]
    lo = CONV_HALO - (w - 1)
    nt = s // t
    lw = min(CONV_LANES, d)

    def body(dc_ref, dcn_ref, u_ref, up_ref, a_ref, gt_ref, dw_ref, dpre_ref, ddw_ref, db_ref, dcat_ref, ucat_ref,
             dsh_ref, ush_ref, ddw_acc, db_acc):
        i = pl.program_id(0)
        dcat_ref[:t, :] = dc_ref[...]
        ucat_ref[CONV_HALO:, :] = u_ref[...]

        @pl.when(i == nt - 1)
        def _():
            dcat_ref[t:, :] = jnp.zeros((CONV_HALO, d), F32)

        @pl.when(i < nt - 1)
        def _():
            dcat_ref[t:, :] = dcn_ref[...]

        @pl.when(i == 0)
        def _():
            ucat_ref[:CONV_HALO, :] = jnp.zeros((CONV_HALO, d), F32)

        @pl.when(i > 0)
        def _():
            ucat_ref[:CONV_HALO, :] = up_ref[...]

        @pl.when(i == 0)
        def _():
            ddw_acc[...] = jnp.zeros(ddw_acc.shape, F32)
            db_acc[...] = jnp.zeros(db_acc.shape, F32)

        def fold(p):
            return functools.reduce(jnp.add, [p[j:j + SUBLANES] for j in range(0, CONV_ROWS, SUBLANES)])

        for lc in range(d // lw):
            lanes = slice(lc * lw, (lc + 1) * lw)
            gate_lanes = slice(d + lc * lw, d + (lc + 1) * lw)
            _shifted_copies(dsh_ref, dcat_ref, lanes, t + CONV_HALO)
            _shifted_copies(ush_ref, ucat_ref, lanes, t + CONV_HALO)
            for rc in range(t // CONV_ROWS):
                r0 = rc * CONV_ROWS
                rows = slice(r0, r0 + CONV_ROWS)
                dcv = dc_ref[rows, lanes]
                du = jnp.zeros((CONV_ROWS, lw), F32)
                for kk in range(w):
                    du = du + dw_ref[kk:kk + 1, lanes] * _shifted(dsh_ref, w - 1 - kk + r0, CONV_ROWS)
                    ddw_acc[kk, :, lanes] += fold(dcv * _shifted(ush_ref, lo + kk + r0, CONV_ROWS))
                av = a_ref[rows, lanes].astype(F32)
                sg = _sigmoid(gt_ref[rows, lanes].astype(F32))
                da = du * sg
                dgt = du * av * sg * (1.0 - sg)
                dpre_ref[rows, lanes] = da.astype(BF16)
                dpre_ref[rows, gate_lanes] = dgt.astype(BF16)
                db_acc[:, lanes] += fold(da)
                db_acc[:, gate_lanes] += fold(dgt)

        @pl.when(i == nt - 1)
        def _():
            ddw_ref[...] = jnp.sum(ddw_acc[...], axis=1)
            db_ref[...] = jnp.sum(db_acc[...], axis=0, keepdims=True)

    row = pl.BlockSpec((t, d), lambda i: (i, 0))
    nxt = pl.BlockSpec((CONV_HALO, d), lambda i: (jnp.minimum((i + 1) * hb, s // CONV_HALO - 1), 0))
    prev = pl.BlockSpec((CONV_HALO, d), lambda i: (jnp.maximum(i * hb - 1, 0), 0))
    return pl.pallas_call(
        body,
        name=name,
        grid=(nt,),
        in_specs=[row, nxt, row, prev, row, row, pl.BlockSpec((w, d), lambda i: (0, 0))],
        out_specs=[
            pl.BlockSpec((t, 2 * d), lambda i: (i, 0)),
            pl.BlockSpec((w, d), lambda i: (0, 0)),
            pl.BlockSpec((1, 2 * d), lambda i: (0, 0)),
        ],
        out_shape=[
            jax.ShapeDtypeStruct((s, 2 * d), BF16),
            jax.ShapeDtypeStruct((w, d), F32),
            jax.ShapeDtypeStruct((1, 2 * d), F32),
        ],
        scratch_shapes=[pltpu.VMEM((t + CONV_HALO, d), F32), pltpu.VMEM((CONV_HALO + t, d), F32)]
        + [pltpu.VMEM((SUBLANES, CONV_HALO + t, lw), F32)] * 2
        + [pltpu.VMEM((w, SUBLANES, d), F32), pltpu.VMEM((SUBLANES, 2 * d), F32)],
        compiler_params=pltpu.CompilerParams(dimension_semantics=("arbitrary",)),
    )(dc, dc, u, u, a, gt, dw)


def _alibi_slopes(n_heads):
    h = jnp.arange(1, n_heads + 1, dtype=F32)
    return jnp.exp2(-8.0 * h / n_heads)


def _band_masks(bq):
    qi = lax.broadcasted_iota(jnp.int32, (bq, bq), 0)
    kj = lax.broadcasted_iota(jnp.int32, (bq, bq), 1)
    return qi - kj, qi - kj + bq


ATTN_GROUP = 16


def _attn_fwd(name, q, k, v, slopes, bq):
    s, dm = q.shape
    nh = dm // HEAD_DIM
    nt = s // bq
    nbr = len(BRANCHES)
    scale = HEAD_DIM ** -0.5
    nt_dims = (((1,), (1,)), ((), ()))

    def body(sl_ref, q_ref, k_ref, v_ref, o_ref, ob_ref, l_ref, tmp, qr, kr, va, orm, lrm, onat, lnat, sbuf, mbuf):
        slope = sl_ref[pl.program_id(0)]
        jc, jp = _band_masks(bq)
        va[:, HEAD_DIM:] = jnp.ones((s, HEAD_DIM), BF16)
        for bi, (win, dil) in enumerate(BRANCHES):
            ll = s // dil
            nblk = ll // bq
            if dil == 1:
                sq, sk = q_ref, k_ref
                va[:, :HEAD_DIM] = v_ref[...]
                d_o, d_l = onat.at[bi], lnat.at[bi]
            else:
                for src, dst, wide in ((q_ref, qr, False), (k_ref, kr, False), (v_ref, va, True)):
                    tmp[...] = src[...].astype(F32)
                    for r in range(dil):
                        part = tmp[pl.ds(r, ll, stride=dil), :].astype(BF16)
                        if wide:
                            dst[r * ll:(r + 1) * ll, :HEAD_DIM] = part
                        else:
                            dst[r * ll:(r + 1) * ll, :] = part
                sq, sk = qr, kr
                d_o, d_l = orm, lrm
            bias_c = jnp.where(jc >= 0, jc.astype(F32) * (slope * dil), 1e30)
            bias_p = jnp.where(jp <= bq, jp.astype(F32) * (slope * dil), 1e30)

            def group(gi, carry):
                rows = []
                for g in range(ATTN_GROUP):
                    ti = gi * ATTN_GROUP + g
                    row = pl.ds(pl.multiple_of(ti * bq, bq), bq)
                    prow = pl.ds(pl.multiple_of(jnp.maximum(ti - 1, 0) * bq, bq), bq)
                    rows.append((row, prow))
                    qh = sq[row, :]
                    sc = lax.dot_general(qh, sk[row, :], nt_dims, preferred_element_type=F32) * scale - bias_c
                    sp = lax.dot_general(qh, sk[prow, :], nt_dims, preferred_element_type=F32) * scale - bias_p
                    sp = jnp.where(lax.rem(ti, nblk) > 0, sp, -1e30)
                    sbuf[g, :, :bq] = sc
                    sbuf[g, :, bq:] = sp
                    mbuf[g] = jnp.maximum(jnp.max(sc, axis=-1, keepdims=True), jnp.max(sp, axis=-1, keepdims=True))
                for g, (row, prow) in enumerate(rows):
                    mx = mbuf[g]
                    p = jnp.exp(sbuf[g] - mx).astype(BF16)
                    ov = jnp.dot(p[:, :bq], va[row, :], preferred_element_type=F32)
                    ov = ov + jnp.dot(p[:, bq:], va[prow, :], preferred_element_type=F32)
                    den = ov[:, HEAD_DIM:]
                    d_o[row, :] = ov[:, :HEAD_DIM] / den
                    d_l[row, :] = mx + jnp.log(den)
                return carry

            lax.fori_loop(0, nt // ATTN_GROUP, group, 0)
            if dil > 1:
                for r in range(dil):
                    onat[bi, pl.ds(r, ll, stride=dil), :] = orm[r * ll:(r + 1) * ll, :]
                    lnat[bi, pl.ds(r, ll, stride=dil), :] = lrm[r * ll:(r + 1) * ll, :]

        def merge(ti, carry):
            rows = pl.ds(pl.multiple_of(ti * bq, bq), bq)
            ls = [lnat[bi, rows, :] for bi in range(nbr)]
            mx = functools.reduce(jnp.maximum, ls)
            es = [jnp.exp(l - mx) for l in ls]
            tot = functools.reduce(jnp.add, es)
            inv = 1.0 / tot
            o = functools.reduce(jnp.add, [e * inv * onat[bi, rows, :] for bi, e in enumerate(es)])
            o_ref[rows, :] = o
            ob_ref[rows, :] = o.astype(BF16)
            l_ref[rows, :] = mx + jnp.log(tot)
            return carry

        lax.fori_loop(0, nt, merge, 0)

    head = pl.BlockSpec((s, HEAD_DIM), lambda h: (0, h))
    return pl.pallas_call(
        body,
        name=name,
        grid=(nh,),
        in_specs=[pl.BlockSpec(memory_space=pltpu.SMEM), head, head, head],
        out_specs=[head, head, head],
        out_shape=[jax.ShapeDtypeStruct((s, dm), F32), jax.ShapeDtypeStruct((s, dm), BF16),
                   jax.ShapeDtypeStruct((s, dm), F32)],
        scratch_shapes=[pltpu.VMEM((s, HEAD_DIM), F32)] + [pltpu.VMEM((s, HEAD_DIM), BF16)] * 2
        + [pltpu.VMEM((s, 2 * HEAD_DIM), BF16)] + [pltpu.VMEM((s, HEAD_DIM), F32)] * 2
        + [pltpu.VMEM((nbr, s, HEAD_DIM), F32)] * 2
        + [pltpu.VMEM((ATTN_GROUP, bq, 2 * bq), F32), pltpu.VMEM((ATTN_GROUP, bq, 1), F32)],
        compiler_params=pltpu.CompilerParams(dimension_semantics=("parallel",)),
    )(slopes, q, k, v)


def _attn_bwd(name, q, k, v, o, lse, do, slopes, bq):
    s, dm = q.shape
    nh = dm // HEAD_DIM
    nt = s // bq
    scale = HEAD_DIM ** -0.5
    nt_dims = (((1,), (1,)), ((), ()))
    tn_dims = (((0,), (0,)), ((), ()))

    def body(sl_ref, q_ref, k_ref, v_ref, o_ref, l_ref, do_ref, dq_ref, dk_ref, dv_ref,
             tmp, qr, kr, vr, dor, rown, rowr, dqr, dkr, dvr, aq, ak, av, pbuf, dsbuf):
        slope = sl_ref[pl.program_id(0)]
        jc, jp = _band_masks(bq)
        lane = lax.broadcasted_iota(jnp.int32, (bq, HEAD_DIM), 1)

        def row_terms(ti, carry):
            rows = pl.ds(pl.multiple_of(ti * bq, bq), bq)
            dl = jnp.sum(do_ref[rows, :].astype(F32) * o_ref[rows, :], axis=-1, keepdims=True)
            rown[rows, :] = jnp.where(lane == 0, l_ref[rows, :], dl)
            return carry

        lax.fori_loop(0, nt, row_terms, 0)

        for bi, (win, dil) in enumerate(BRANCHES):
            ll = s // dil
            nblk = ll // bq
            if dil == 1:
                sq, sk, sv, sdo, srow = q_ref, k_ref, v_ref, do_ref, rown
                gq, gk, gv = aq, ak, av
            else:
                for src, dst in ((q_ref, qr), (k_ref, kr), (v_ref, vr), (do_ref, dor)):
                    tmp[...] = src[...].astype(F32)
                    for r in range(dil):
                        dst[r * ll:(r + 1) * ll, :] = tmp[pl.ds(r, ll, stride=dil), :].astype(BF16)
                for r in range(dil):
                    rowr[r * ll:(r + 1) * ll, :] = rown[pl.ds(r, ll, stride=dil), :]
                sq, sk, sv, sdo, srow = qr, kr, vr, dor, rowr
                gq, gk, gv = dqr, dkr, dvr
            bias_c = jnp.where(jc >= 0, jc.astype(F32) * (slope * dil), 1e30)
            bias_p = jnp.where(jp <= bq, jp.astype(F32) * (slope * dil), 1e30)

            def group(gi, carry):
                rows = []
                for g in range(ATTN_GROUP):
                    ti = gi * ATTN_GROUP + g
                    row = pl.ds(pl.multiple_of(ti * bq, bq), bq)
                    prow = pl.ds(pl.multiple_of(jnp.maximum(ti - 1, 0) * bq, bq), bq)
                    rows.append((row, prow))
                    has_prev = lax.rem(ti, nblk) > 0
                    qh, doh = sq[row, :], sdo[row, :]
                    terms = srow[row, :]
                    lc = terms[:, 0:1]
                    dl = terms[:, 1:2]
                    for half, kv_rows, bias in ((0, row, bias_c), (1, prow, bias_p)):
                        sc = lax.dot_general(qh, sk[kv_rows, :], nt_dims, preferred_element_type=F32) * scale - bias
                        if half:
                            sc = jnp.where(has_prev, sc, -1e30)
                        p = jnp.exp(sc - lc)
                        dp = lax.dot_general(doh, sv[kv_rows, :], nt_dims, preferred_element_type=F32)
                        pbuf[g, :, half * bq:(half + 1) * bq] = p.astype(BF16)
                        dsbuf[g, :, half * bq:(half + 1) * bq] = (p * (dp - dl) * scale).astype(BF16)
                carry_k = carry_v = None
                for g, (row, prow) in enumerate(rows):
                    qh, doh = sq[row, :], sdo[row, :]
                    ds_c, ds_p = dsbuf[g, :, :bq], dsbuf[g, :, bq:]
                    p_c, p_p = pbuf[g, :, :bq], pbuf[g, :, bq:]
                    dq = jnp.dot(ds_c, sk[row, :], preferred_element_type=F32)
                    dq = dq + jnp.dot(ds_p, sk[prow, :], preferred_element_type=F32)
                    gq[row, :] = dq
                    dk_p = lax.dot_general(ds_p, qh, tn_dims, preferred_element_type=F32)
                    dv_p = lax.dot_general(p_p, doh, tn_dims, preferred_element_type=F32)
                    if g == 0:
                        @pl.when(gi > 0)
                        def _():
                            gk[prow, :] += dk_p
                            gv[prow, :] += dv_p
                    else:
                        gk[rows[g - 1][0], :] = carry_k + dk_p
                        gv[rows[g - 1][0], :] = carry_v + dv_p
                    carry_k = lax.dot_general(ds_c, qh, tn_dims, preferred_element_type=F32)
                    carry_v = lax.dot_general(p_c, doh, tn_dims, preferred_element_type=F32)
                gk[rows[-1][0], :] = carry_k
                gv[rows[-1][0], :] = carry_v
                return carry

            lax.fori_loop(0, nt // ATTN_GROUP, group, 0)
            if dil > 1:
                for acc, rm in ((aq, dqr), (ak, dkr), (av, dvr)):
                    for r in range(dil):
                        acc[pl.ds(r, ll, stride=dil), :] += rm[r * ll:(r + 1) * ll, :]

        dq_ref[...] = aq[...].astype(BF16)
        dk_ref[...] = ak[...].astype(BF16)
        dv_ref[...] = av[...].astype(BF16)

    head = pl.BlockSpec((s, HEAD_DIM), lambda h: (0, h))
    f32buf = pltpu.VMEM((s, HEAD_DIM), F32)
    b16buf = pltpu.VMEM((s, HEAD_DIM), BF16)
    return pl.pallas_call(
        body,
        name=name,
        grid=(nh,),
        in_specs=[pl.BlockSpec(memory_space=pltpu.SMEM)] + [head] * 6,
        out_specs=[head] * 3,
        out_shape=[jax.ShapeDtypeStruct((s, dm), BF16)] * 3,
        scratch_shapes=[f32buf] + [b16buf] * 4 + [f32buf] * 8 + [pltpu.VMEM((ATTN_GROUP, bq, 2 * bq), BF16)] * 2,
        compiler_params=pltpu.CompilerParams(dimension_semantics=("parallel",)),
    )(slopes, q, k, v, o, lse, do)


def _ep_id(accs, ex):
    return [accs[0]]


def _ep_all(accs, ex):
    return list(accs)


def _ep_sum(accs, ex):
    return [accs[0] + accs[1]]


def _ep_add(accs, ex):
    return [accs[0] + ex[0].astype(F32)]


def _ep_bias_res(accs, ex):
    return [accs[0] + ex[0] + ex[1]]


def _ep_glu(accs, ex):
    a = accs[0] + ex[0]
    gt = accs[1] + ex[1]
    return [a * _sigmoid(gt), a, gt]


def _ep_swiglu(accs, ex):
    g, u = accs
    return [g, u, g * _sigmoid(g) * u]


def _ep_swiglu_bwd(accs, ex):
    dact = accs[0]
    g = ex[0].astype(F32)
    u = ex[1].astype(F32)
    sg = _sigmoid(g)
    return [dact * u * (sg * (1.0 + g * (1.0 - sg))), dact * g * sg]


def _ffn_fwd(tag, h, gain, get_gate_up, get_down):
    s, d = h.shape
    (n,) = _rms_fwd(f"{tag}_norm", h, gain)
    wg, wu = get_gate_up(n)
    f = wg.shape[-1]
    g, u, act = _mm(f"{tag}_gate_up", "nn", n, [(wg, None, 0), (wu, None, 0)], _ep_swiglu,
                    [(BF16,), (BF16,), (BF16,)], m=s, n=f, k=d)
    wd = get_down(act)
    (out,) = _mm(f"{tag}_down", "nn", act, [(wd, None, 0)], _ep_add, [(F32,)], m=s, n=d, k=f,
                 extras=[(h, "mn", 0)])
    return out, (n, g, u, act), dict(gate=wg, up=wu, down=wd)


def _ffn_bwd(tag, h_in, gain, wg, wu, wd, saved, dh, dhb, after, emit):
    s, d = h_in.shape
    f = wg.shape[-1]
    n, g, u, act = saved
    dg, du = _mm(f"{tag}_bwd_dact", "nt", dhb, [(wd, None, 0)], _ep_swiglu_bwd, [(BF16,), (BF16,)],
                 m=s, n=f, k=d, extras=[(g, "mn", 0), (u, "mn", 0)], after=after)
    (dwd,) = _mm(f"{tag}_bwd_dwd", "tn", act, [(dhb, None, 0)], _ep_id, [(BF16,)], m=f, n=d, k=s,
                 bm=f // 4)
    pin = emit("d", dict(down=dwd))
    dwg, dwu = _mm(f"{tag}_bwd_dwgu", "tn", n, [(dg, None, 0), (du, None, 0)], _ep_all, [(BF16,), (BF16,)],
                   m=d, n=f, k=s, after=pin)
    pin = emit("gu", dict(gate=dwg, up=dwu))
    (dn,) = _mm(f"{tag}_bwd_dn", "nt", [dg, du], [(wg, None, 0), (wu, None, 0)], _ep_sum, [(BF16,)],
                m=s, n=d, k=f, bm=512, bn=256, after=pin)
    dx, dxb, dgain, cs = _rms_bwd(f"{tag}_bwd_norm", h_in, gain, [dn], dh)
    return dx, dxb, dgain, cs


def _local_step(x, target, w, fetch, emit, prefetch=lambda group, after: (), after=()):
    s, d = x.shape
    nh = d // HEAD_DIM
    bq = BRANCHES[0][0] // BRANCHES[0][1]
    assert all(win // dil == bq for win, dil in BRANCHES)
    assert BRANCHES[0][1] == 1 and all(dil > 1 for _, dil in BRANCHES[1:])
    slopes = _alibi_slopes(nh)
    nd = d // 512 if d >= 512 else 1
    bn = d // nd

    (n1,) = _rms_fwd("a_norm", x, w["a_norm_g"], after=after)
    prefetch("rest", n1)
    w_conv1 = fetch("conv1", n1)["conv_w1"]
    glu, a, gt = _mm("conv_pw1_glu", "nn", n1, [(w_conv1, None, 0), (w_conv1, None, nd)], _ep_glu,
                     [(F32,), (BF16,), (BF16,)], m=s, n=d, k=d, bn=bn,
                     extras=[(w["conv_b1"], "n", 0), (w["conv_b1"], "n", nd)])
    c, sw = _dwconv_fwd("conv_dw_ln", glu, w["conv_dw"], w["conv_dw_b"], w["conv_ln_g"], w["conv_ln_b"])
    w_conv2 = fetch("conv2", sw)["conv_w2"]
    (h1,) = _mm("conv_pw2", "nn", sw, [(w_conv2, None, 0)], _ep_bias_res, [(F32,)], m=s, n=d, k=d,
                extras=[(w["conv_b2"], "n", 0), (x, "mn", 0)], bn=WIDE_BN)
    def gate_up0(after):
        wts = fetch("ffn0gu", after)
        return wts["gate"], wts["up"]

    h2, ffn0, wf0 = _ffn_fwd("ffn0", h1, w["ffn_norm_g"][0:1], gate_up0, lambda after: fetch("ffn0d", after)["down"])
    wa = fetch("attn", h2)
    kvn, qn = _rms_fwd("kvq_norm", h2, jnp.concatenate([w["kv_norm_g"], w["b_norm_g"]], axis=0))
    k, v = _mm("kv_proj", "nn", kvn, [(wa["w_k"], None, 0), (wa["w_v"], None, 0)], _ep_all, [(BF16,), (BF16,)],
               m=s, n=d, k=d, bn=WIDE_BN)
    (q,) = _mm("q_proj", "nn", qn, [(wa["w_q"], None, 0)], _ep_id, [(BF16,)], m=s, n=d, k=d, bn=WIDE_BN)
    att, attb, lse = _attn_fwd("attn_fwd", q, k, v, slopes, bq)
    pin = prefetch("ffn1", attb)
    (h3,) = _mm("o_proj", "nn", attb, [(wa["w_o"], None, 0)], _ep_add, [(F32,)], m=s, n=d, k=d,
                extras=[(h2, "mn", 0)], bn=WIDE_BN, after=pin)
    got1 = {}

    def gate_up1(after):
        got1.update(fetch("ffn1", after))
        return got1["gate"], got1["up"]

    h4, ffn1, wf1 = _ffn_fwd("ffn1", h3, w["ffn_norm_g"][1:2], gate_up1, lambda after: got1["down"])
    dh4, dh4b, d_final_g, loss_cols = _final_loss("final_loss", h4, w["final_norm_g"], target)

    g = {}
    ga = {}
    dh3, dh3b, dgain1, _ = _ffn_bwd("ffn1", h3, w["ffn_norm_g"][1:2], wf1["gate"], wf1["up"], wf1["down"], ffn1,
                                    dh4, dh4b, (), lambda part, grads: emit("ffn1" + part, grads))
    (datt,) = _mm("o_proj_bwd_dx", "nt", dh3b, [(wa["w_o"], None, 0)], _ep_id, [(BF16,)], m=s, n=d, k=d, bn=WIDE_BN)
    (ga["w_o"],) = _mm("o_proj_bwd_dw", "tn", attb, [(dh3b, None, 0)], _ep_id, [(BF16,)], m=d, n=d, k=s, bn=WIDE_BN)
    dq, dk, dv = _attn_bwd("attn_bwd", q, k, v, att, lse, datt, slopes, bq)
    (ga["w_q"],) = _mm("q_proj_bwd_dw", "tn", qn, [(dq, None, 0)], _ep_id, [(BF16,)], m=d, n=d, k=s, bn=WIDE_BN)
    ga["w_k"], ga["w_v"] = _mm("kv_proj_bwd_dw", "tn", kvn, [(dk, None, 0), (dv, None, 0)], _ep_all,
                               [(BF16,), (BF16,)], m=d, n=d, k=s)
    pin = emit("attn", ga)
    (dqn,) = _mm("q_proj_bwd_dx", "nt", dq, [(wa["w_q"], None, 0)], _ep_id, [(BF16,)], m=s, n=d, k=d, after=pin,
                 bn=WIDE_BN)
    (dkvn,) = _mm("kv_proj_bwd_dx", "nt", [dk, dv], [(wa["w_k"], None, 0), (wa["w_v"], None, 0)], _ep_sum, [(BF16,)],
                  m=s, n=d, k=d, bn=WIDE_BN)
    dh2, dh2b, dg_kvq, _ = _rms_bwd("kvq_norm_bwd", h2, jnp.concatenate([w["kv_norm_g"], w["b_norm_g"]], axis=0),
                                    [dkvn, dqn], dh3)
    dh1, dh1b, dgain0, cs_h1 = _ffn_bwd("ffn0", h1, w["ffn_norm_g"][0:1], wf0["gate"], wf0["up"], wf0["down"], ffn0,
                                        dh2, dh2b, (), lambda part, grads: emit("ffn0" + part, grads))
    (dsw,) = _mm("conv_pw2_bwd_dx", "nt", dh1b, [(w_conv2, None, 0)], _ep_id, [(F32,)], m=s, n=d, k=d, bn=WIDE_BN)
    (dw2,) = _mm("conv_pw2_bwd_dw", "tn", sw, [(dh1b, None, 0)], _ep_id, [(BF16,)], m=d, n=d, k=s, bn=WIDE_BN)
    pin = emit("conv2", dict(conv_w2=dw2))
    dc, ln_sums = _conv_ln_bwd("conv_ln_bwd", c, dsw, w["conv_ln_g"], w["conv_ln_b"], after=pin)
    dpre, ddw, db1 = _conv_dw_bwd("conv_dw_bwd", dc, glu, a, gt, w["conv_dw"])
    (dw1,) = _mm("conv_pw1_bwd_dw", "tn", n1, [(dpre, None, 0)], _ep_id, [(BF16,)], m=d, n=2 * d, k=s)
    pin = emit("conv1", dict(conv_w1=dw1))
    (dn1,) = _mm("conv_pw1_bwd_dx", "nt", dpre, [(w_conv1, None, 0)], _ep_id, [(BF16,)], m=s, n=d, k=2 * d,
                 after=pin)
    dx, _, d_a_norm, _ = _rms_bwd("a_norm_bwd", x, w["a_norm_g"], [dn1], dh1)

    g.update(
        a_norm_g=d_a_norm, conv_b1=db1, conv_dw=ddw, conv_dw_b=ln_sums[2:3], conv_ln_g=ln_sums[0:1],
        conv_ln_b=ln_sums[1:2], conv_b2=cs_h1, kv_norm_g=dg_kvq[0:1], b_norm_g=dg_kvq[1:2],
        ffn_norm_g=jnp.concatenate([dgain0, dgain1], axis=0), final_norm_g=d_final_g,
    )
    return loss_cols, dx, g


HBM_SPEC = pl.BlockSpec(memory_space=pltpu.HBM)


def _mesh_place():
    x, y, c = lax.axis_index("x"), lax.axis_index("y"), lax.axis_index("c")
    chips = [(1 - x, y), (x, 1 - y), (1 - x, 1 - y)]
    return x, y, c, chips


def _shard_view(ref, kind, s, half=None):
    rows, cols = ref.shape
    if kind == "col":
        cw = cols // N_CHIPS
        if half is None:
            return ref.at[pl.ds(0, rows), pl.ds(s * cw, cw)]
        return ref.at[pl.ds(half * (rows // 2), rows // 2), pl.ds(s * cw, cw)]
    r = rows // N_CHIPS
    if half is None:
        return ref.at[pl.ds(s * r, r), pl.ds(0, cols)]
    return ref.at[pl.ds(s * r + half * (r // 2), r // 2), pl.ds(0, cols)]


def _full_shape(shard, kind):
    r, cw = shard.shape[-2:]
    return (r, cw * N_CHIPS) if kind == "col" else (r * N_CHIPS, cw)


def _gather_weights(shards, kinds, after=()):
    nt = len(shards)
    nf = len(after)
    fulls = [jax.ShapeDtypeStruct(_full_shape(sh, kind), sh.dtype) for sh, kind in zip(shards, kinds)]

    def body(*refs):
        src = refs[:nt]
        dst = refs[nt + nf:2 * nt + nf]
        send, recv, fsend, frecv, local = refs[2 * nt + nf:]
        x, y, c, chips = _mesh_place()
        s = 2 * x + y
        sib = (x, y, 1 - c)

        def half_of_shard(t):
            r, cw = src[t].shape
            return src[t].at[pl.ds(c * (r // 2), r // 2), pl.ds(0, cw)]

        locals_ = [pltpu.make_async_copy(src[t], _shard_view(dst[t], kinds[t], s), local.at[t]) for t in range(nt)]
        for cp in locals_:
            cp.start()
        sends = []
        for t in range(nt):
            for j, chip in enumerate(chips):
                cp = pltpu.make_async_remote_copy(
                    src_ref=half_of_shard(t), dst_ref=_shard_view(dst[t], kinds[t], s, c),
                    send_sem=send.at[t, j], recv_sem=recv.at[t, j], device_id=(*chip, c), device_id_type=MESH)
                cp.start()
                sends.append(cp)
        for t in range(nt):
            for j, (px, py) in enumerate(chips):
                landed = _shard_view(dst[t], kinds[t], 2 * px + py, c)
                pltpu.make_async_remote_copy(
                    src_ref=half_of_shard(t), dst_ref=landed, send_sem=send.at[t, j], recv_sem=recv.at[t, j],
                    device_id=(px, py, c), device_id_type=MESH).wait_recv()
                cp = pltpu.make_async_remote_copy(
                    src_ref=landed, dst_ref=landed, send_sem=fsend.at[t, j], recv_sem=frecv.at[t, j],
                    device_id=sib, device_id_type=MESH)
                cp.start()
                sends.append(cp)
        for t in range(nt):
            for j, (px, py) in enumerate(chips):
                other = _shard_view(dst[t], kinds[t], 2 * px + py, 1 - c)
                pltpu.make_async_remote_copy(
                    src_ref=other, dst_ref=other, send_sem=fsend.at[t, j], recv_sem=frecv.at[t, j],
                    device_id=sib, device_id_type=MESH).wait_recv()
        for cp in sends:
            cp.wait_send()
        for cp in locals_:
            cp.wait()

    return pl.pallas_call(
        body,
        name="gather_weights",
        in_specs=[HBM_SPEC] * nt + [ANY_SPEC] * nf,
        out_specs=[HBM_SPEC] * nt,
        out_shape=fulls,
        scratch_shapes=[pltpu.SemaphoreType.DMA((nt, 3))] * 4 + [pltpu.SemaphoreType.DMA((nt,))],
    )(*shards, *after)


def _row_blocks(rows, want=512):
    nb = 1
    while rows // nb > want or rows % nb or (rows // nb) % 16:
        nb += 1
        if nb > rows:
            return rows, 1
    return rows // nb, nb


def _place_own(name, src, kind, land, ids, lead=0):
    r, cw = src.shape[-2:]
    tr, nb = _row_blocks(r)
    if src.ndim == 3:
        src_spec = pl.BlockSpec((None, tr, cw), lambda i, ids_ref: (lead, i, 0))
    else:
        src_spec = pl.BlockSpec((tr, cw), lambda i, ids_ref: (i, 0))
    if kind == "col":
        dst_spec = pl.BlockSpec((tr, cw), lambda i, ids_ref: (i, ids_ref[0]))
    else:
        dst_spec = pl.BlockSpec((tr, cw), lambda i, ids_ref: (ids_ref[0] * nb + i, 0))

    def body(ids_ref, s_ref, o_ref):
        o_ref[...] = s_ref[...].astype(o_ref.dtype)

    return pl.pallas_call(
        body,
        name=name,
        grid_spec=pltpu.PrefetchScalarGridSpec(num_scalar_prefetch=1, grid=(nb,), in_specs=[src_spec],
                                               out_specs=dst_spec),
        out_shape=land,
        compiler_params=pltpu.CompilerParams(dimension_semantics=("parallel",)),
    )(ids, src)


SEM_SPEC = pl.BlockSpec(memory_space=pltpu.SEMAPHORE)
SIDE_EFFECT = pltpu.SideEffectType.DATAFLOW_SIDE_EFFECTING


def _copies_start(name, srcs, lands, copies, n_sems, after):
    ns, nl, nf = len(srcs), len(lands), len(after)

    def body(*refs):
        src, land = refs[:ns], refs[ns:ns + nl]
        send, recv = refs[ns + nl + nf], refs[ns + nl + nf + 1]
        pin = refs[-1]
        for cp in copies(src, land, send, recv, _mesh_place()):
            cp.start()
        pin[...] = jnp.zeros_like(pin)

    arrs = list(srcs) + list(lands)
    res = pl.pallas_call(
        body,
        name=name,
        in_specs=[HBM_SPEC] * (ns + nl) + [ANY_SPEC] * nf,
        out_specs=[SEM_SPEC, SEM_SPEC] + [HBM_SPEC] * (ns + nl) + [pl.BlockSpec(memory_space=pltpu.VMEM)],
        out_shape=[pltpu.SemaphoreType.DMA((n_sems,)), pltpu.SemaphoreType.DMA((n_sems,))]
        + [pltpu.HBM(a.shape, a.dtype) for a in arrs] + [jax.ShapeDtypeStruct((8, 128), F32)],
        input_output_aliases={i: 2 + i for i in range(ns + nl)},
        compiler_params=pltpu.CompilerParams(has_side_effects=SIDE_EFFECT),
    )(*[pltpu.with_memory_space_constraint(a, pltpu.HBM) for a in arrs], *after)
    return res[0], res[1], list(res[2:2 + ns]), list(res[2 + ns:2 + ns + nl]), res[-1]


def _copies_wait(name, send, recv, srcs, lands, copies, after):
    ns, nl, nf = len(srcs), len(lands), len(after)

    def body(*refs):
        src, land = refs[:ns], refs[ns:ns + nl]
        send_sems, recv_sems = refs[ns + nl], refs[ns + nl + 1]
        cps = copies(src, land, send_sems, recv_sems, _mesh_place())
        for cp in cps:
            cp.wait_send()
        for cp in cps:
            cp.wait_recv()

    arrs = list(srcs) + list(lands)
    res = pl.pallas_call(
        body,
        name=name,
        in_specs=[HBM_SPEC] * (ns + nl) + [SEM_SPEC, SEM_SPEC] + [ANY_SPEC] * nf,
        out_specs=[HBM_SPEC] * (ns + nl),
        out_shape=[pltpu.HBM(a.shape, a.dtype) for a in arrs],
        input_output_aliases={i: i for i in range(ns + nl)},
        compiler_params=pltpu.CompilerParams(has_side_effects=SIDE_EFFECT),
    )(*arrs, send, recv, *after)
    return list(res[:ns]), list(res[ns:])


def _gather_copies(kinds, halves):
    def copies(src, land, send, recv, place):
        x, y, c, chips = place
        s = 2 * x + y
        mine = [_shard_view(land[t], kinds[t], s, c if halves else None) for t in range(len(kinds))]
        return [
            pltpu.make_async_remote_copy(
                src_ref=mine[t], dst_ref=mine[t], send_sem=send.at[3 * t + j], recv_sem=recv.at[3 * t + j],
                device_id=(px, py, c), device_id_type=MESH)
            for t in range(len(kinds)) for j, (px, py) in enumerate(chips)
        ]

    return copies


def _gather_pass_copies(kinds):
    def copies(src, land, send, recv, place):
        x, y, c, chips = place
        views = [_shard_view(land[t], kinds[t], 2 * px + py, c) for t in range(len(kinds)) for px, py in chips]
        return [
            pltpu.make_async_remote_copy(src_ref=v, dst_ref=v, send_sem=send.at[i], recv_sem=recv.at[i],
                                         device_id=(x, y, 1 - c), device_id_type=MESH)
            for i, v in enumerate(views)
        ]

    return copies


def _gather_pass_on(name, lands, kinds):
    nt = len(lands)

    def body(*refs):
        buf = refs[nt:2 * nt]
        send, recv = refs[2 * nt:]
        x, y, c, chips = _mesh_place()
        sib = (x, y, 1 - c)
        sends = []
        for t in range(nt):
            for j, (px, py) in enumerate(chips):
                mine = _shard_view(buf[t], kinds[t], 2 * px + py, c)
                cp = pltpu.make_async_remote_copy(src_ref=mine, dst_ref=mine, send_sem=send.at[t, j],
                                                  recv_sem=recv.at[t, j], device_id=sib, device_id_type=MESH)
                cp.start()
                sends.append(cp)
        for t in range(nt):
            for j, (px, py) in enumerate(chips):
                theirs = _shard_view(buf[t], kinds[t], 2 * px + py, 1 - c)
                pltpu.make_async_remote_copy(src_ref=theirs, dst_ref=theirs, send_sem=send.at[t, j],
                                             recv_sem=recv.at[t, j], device_id=sib, device_id_type=MESH).wait_recv()
        for cp in sends:
            cp.wait_send()

    return pl.pallas_call(
        body,
        name=name,
        in_specs=[HBM_SPEC] * nt,
        out_specs=[HBM_SPEC] * nt,
        out_shape=[jax.ShapeDtypeStruct(a.shape, a.dtype) for a in lands],
        input_output_aliases={i: i for i in range(nt)},
        scratch_shapes=[pltpu.SemaphoreType.DMA((nt, 3))] * 2,
    )(*lands)


def _grad_part(ref, kind, s):
    return ref if kind == "all" else _shard_view(ref, kind, s)


def _grad_copies(kinds):
    def peers(place):
        x, y, c, chips = place
        return [(x, y, 1 - c)] + [(px, py, c) for px, py in chips]

    def copies(src, land, send, recv, place):
        x, y, c, chips = place
        me = 4 * x + 2 * y + c
        return [
            pltpu.make_async_remote_copy(
                src_ref=_grad_part(src[t], kinds[t], 2 * px + py), dst_ref=land[t].at[me],
                send_sem=send.at[GRAD_PEERS * t + k], recv_sem=recv.at[GRAD_PEERS * t + k], device_id=(px, py, pc),
                device_id_type=MESH)
            for t in range(len(kinds)) for k, (px, py, pc) in enumerate(peers(place))
        ]

    return copies


def _pass_copies(n):
    def copies(src, land, send, recv, place):
        x, y, c, chips = place
        return [
            pltpu.make_async_remote_copy(
                src_ref=land[t].at[4 * px + 2 * py + c], dst_ref=land[t].at[4 * px + 2 * py + c],
                send_sem=send.at[3 * t + j], recv_sem=recv.at[3 * t + j], device_id=(x, y, 1 - c),
                device_id_type=MESH)
            for t in range(n) for j, (px, py) in enumerate(chips)
        ]

    return copies


GRAD_PEERS = 4


def _land_shape(grad, kind):
    rows, cols = grad.shape
    if kind == "col":
        return (N_DEV, rows, cols // N_CHIPS)
    if kind == "row":
        return (N_DEV, rows // N_CHIPS, cols)
    return (N_DEV, rows, cols)


def _adamw_reduce(name, contrib, own, kind, ids, w, m, v, layer=None, prev=None):
    rows, cols = w.shape[-2:]
    t = rows
    for cand in (256, 128):
        if rows % cand == 0 and cand * cols <= 256 * 1408:
            t = cand
            break
    nb = rows // t
    c1 = 1.0 - ADAM_B1 ** ADAM_STEP
    c2 = 1.0 - ADAM_B2 ** ADAM_STEP

    n_prev = 0 if prev is None else 4

    def body(ids_ref, c_ref, own_ref, w_ref, m_ref, v_ref, *refs):
        g_ref, d_ref, nm_ref, nv_ref = refs[n_prev:]
        me = ids_ref[1]
        mine = own_ref[...].astype(F32)
        g = None
        for q in range(N_DEV):
            term = jnp.where(me == q, mine, c_ref[q].astype(F32))
            g = term if g is None else g + term
        nm = ADAM_B1 * m_ref[...] + (1.0 - ADAM_B1) * g
        nv = ADAM_B2 * v_ref[...] + (1.0 - ADAM_B2) * (g * g)
        g_ref[...] = g
        nm_ref[...] = nm
        nv_ref[...] = nv
        d_ref[...] = -ADAM_LR * ((nm / c1) / (jnp.sqrt(nv / c2) + ADAM_EPS) + ADAM_WD * w_ref[...])

    if layer is None:
        blk = pl.BlockSpec((t, cols), lambda i, ids_ref: (i, 0))
    else:
        blk = pl.BlockSpec((None, t, cols), lambda i, ids_ref: (layer, i, 0))
    if kind == "col":
        own_spec = pl.BlockSpec((t, cols), lambda i, ids_ref: (i, ids_ref[0]))
    elif kind == "row":
        own_spec = pl.BlockSpec((t, cols), lambda i, ids_ref: (ids_ref[0] * nb + i, 0))
    else:
        own_spec = pl.BlockSpec((t, cols), lambda i, ids_ref: (i, 0))
    return pl.pallas_call(
        body,
        name=name,
        grid_spec=pltpu.PrefetchScalarGridSpec(
            num_scalar_prefetch=1,
            grid=(nb,),
            in_specs=[pl.BlockSpec((N_DEV, t, cols), lambda i, ids_ref: (0, i, 0)), own_spec, blk, blk, blk]
            + [ANY_SPEC] * n_prev,
            out_specs=[blk] * 4,
        ),
        out_shape=[jax.ShapeDtypeStruct(w.shape, F32)] * 4,
        input_output_aliases={6 + i: i for i in range(n_prev)},
        compiler_params=pltpu.CompilerParams(dimension_semantics=("parallel",)),
    )(ids, contrib, own, w, m, v, *(prev or ()))


WEIGHT_NAMES = ("a_norm_g", "conv_w1", "conv_b1", "conv_dw", "conv_dw_b", "conv_ln_g", "conv_ln_b", "conv_w2",
                "conv_b2", "kv_norm_g", "w_k", "w_v", "b_norm_g", "w_q", "w_o", "ffn_norm_g", "ffn_w_gate",
                "ffn_w_up", "ffn_w_down", "final_norm_g")
GROUPS = {
    "conv2": (("conv_w2", "conv_w2", None, "row"),),
    "ffn0": (("gate", "ffn_w_gate", 0, "col"), ("up", "ffn_w_up", 0, "col"), ("down", "ffn_w_down", 0, "row")),
    "attn": (("w_k", "w_k", None, "row"), ("w_v", "w_v", None, "row"), ("w_q", "w_q", None, "row"),
             ("w_o", "w_o", None, "row")),
    "ffn1": (("gate", "ffn_w_gate", 1, "col"), ("up", "ffn_w_up", 1, "col"), ("down", "ffn_w_down", 1, "row")),
}
GROUPS["conv1"] = (("conv_w1", "conv_w1", None, "col"),)
for _layer in (0, 1):
    GROUPS[f"ffn{_layer}d"] = tuple(it for it in GROUPS[f"ffn{_layer}"] if it[0] == "down")
    GROUPS[f"ffn{_layer}gu"] = tuple(it for it in GROUPS[f"ffn{_layer}"] if it[0] != "down")
FETCH_ORDER = ("conv1", "conv2", "ffn0gu", "ffn0d", "attn", "ffn1")
HALVED = ("conv1", "conv2", "ffn0gu", "ffn0d", "attn", "ffn1")
EMIT_ORDER = ("ffn1d", "ffn1gu", "attn", "ffn0d", "ffn0gu", "conv2", "conv1", "vec")
RETIRE_AT = {"attn": ("ffn1d", "ffn1gu"), "ffn0d": ("attn",), "conv1": ("ffn0d", "ffn0gu", "conv2")}
PACKED = (("a_norm_g", 0, 1), ("conv_b1", 8, 2), ("conv_dw", 16, CONV_WIDTH), ("conv_dw_b", 48, 1),
          ("conv_ln_g", 56, 1), ("conv_ln_b", 64, 1), ("conv_b2", 72, 1))
PACK_ROWS = 80
WHOLE = (("kv_norm_g", 0, 1), ("b_norm_g", 1, 1), ("ffn_norm_g", 2, 2), ("final_norm_g", 4, 1))
WHOLE_ROWS = 8


def _pack_rows(parts, total, width):
    out, at = [], 0
    for arr, first in parts:
        if first > at:
            out.append(jnp.zeros((first - at, width), F32))
        rows8 = -(-arr.shape[0] // 8) * 8
        out.append(jnp.pad(arr, ((0, rows8 - arr.shape[0]), (0, 0))))
        at = first + rows8
    if total > at:
        out.append(jnp.zeros((total - at, width), F32))
    return jnp.concatenate(out, axis=0)


def kernel(x, a_norm_g, conv_w1, conv_b1, conv_dw, conv_dw_b, conv_ln_g, conv_ln_b, conv_w2, conv_b2, kv_norm_g, w_k, w_v, b_norm_g, w_q, w_o, ffn_norm_g, ffn_w_gate, ffn_w_up, ffn_w_down, final_norm_g, loss_target, m_a_norm_g, m_conv_w1, m_conv_b1, m_conv_dw, m_conv_dw_b, m_conv_ln_g, m_conv_ln_b, m_conv_w2, m_conv_b2, m_kv_norm_g, m_w_k, m_w_v, m_b_norm_g, m_w_q, m_w_o, m_ffn_norm_g, m_ffn_w_gate, m_ffn_w_up, m_ffn_w_down, m_final_norm_g, v_a_norm_g, v_conv_w1, v_conv_b1, v_conv_dw, v_conv_dw_b, v_conv_ln_g, v_conv_ln_b, v_conv_w2, v_conv_b2, v_kv_norm_g, v_w_k, v_w_v, v_b_norm_g, v_w_q, v_w_o, v_ffn_norm_g, v_ffn_w_gate, v_ffn_w_up, v_ffn_w_down, v_final_norm_g):
    args = locals()
    wts = {n: args[n] for n in WEIGHT_NAMES}
    mom = {n: args["m_" + n] for n in WEIGHT_NAMES}
    vel = {n: args["v_" + n] for n in WEIGHT_NAMES}
    s, d = x.shape[-2:]
    dq = d // N_CHIPS
    x2 = x.reshape(s, d)
    tgt = loss_target.reshape(s, d)

    def pack_shard(src):
        return _pack_rows([(src[n].reshape(-1, dq), first) for n, first, _ in PACKED], PACK_ROWS, dq)

    def pack_whole(src):
        return _pack_rows([(jnp.concatenate([src[n].reshape(-1, d) for n, _, _ in WHOLE], axis=0), 0)], WHOLE_ROWS, d)

    gathers = {}
    pins = []
    ids = jnp.stack([2 * lax.axis_index("x") + lax.axis_index("y"),
                     4 * lax.axis_index("x") + 2 * lax.axis_index("y") + lax.axis_index("c")]).astype(jnp.int32)

    def start_gather(grp, pins):
        kinds = [kind for _, _, _, kind in GROUPS[grp]]
        lands = []
        for key, n, layer, kind in GROUPS[grp]:
            shard = wts[n] if layer is not None else wts[n].reshape(wts[n].shape[-2:])
            full = jax.ShapeDtypeStruct(_full_shape(shard, kind), BF16)
            lands.append(_place_own(f"gather_place_{grp}_{key}", shard, kind, full, ids, layer))
        copies = _gather_copies(kinds, grp in HALVED)
        send, recv, _, lands, pin = _copies_start("gather_start_" + grp, [], lands, copies, 3 * len(lands), pins)
        gathers[grp] = (send, recv, [], lands, copies, kinds)
        return [pin]

    (packed_full,) = _gather_weights([pack_shard(wts)], ["row"])
    w = {}
    pins = start_gather(FETCH_ORDER[0], [packed_full])
    started = {}
    swapping = {}

    def prefetch(grp, after):
        if grp == "rest":
            last = [after]
            for later in FETCH_ORDER[1:]:
                last = start_gather(later, last)
            started["pin"] = last
            return ()
        send, recv, srcs, lands, copies, kinds = gathers[grp]
        _, fulls = _copies_wait("gather_wait_" + grp, send, recv, srcs, lands, copies, [after])
        copies = _gather_pass_copies(kinds)
        send, recv, _, fulls, pin = _copies_start("gather_swap_" + grp, [], fulls, copies, 3 * len(fulls), ())
        swapping[grp] = (send, recv, fulls, copies)
        return [pin]

    def fetch(grp, after):
        if grp in swapping:
            send, recv, fulls, copies = swapping[grp]
            _, fulls = _copies_wait("gather_swapped_" + grp, send, recv, [], fulls, copies, [after])
        else:
            send, recv, srcs, lands, copies, kinds = gathers[grp]
            behind = [after] + (started.get("pin", []) if grp == FETCH_ORDER[0] else [])
            _, fulls = _copies_wait("gather_wait_" + grp, send, recv, srcs, lands, copies, behind)
            if grp in HALVED:
                fulls = _gather_pass_on("gather_pass_" + grp, fulls, kinds)
        return {key: full for (key, _, _, _), full in zip(GROUPS[grp], fulls)}

    packed = packed_full.reshape(N_CHIPS, PACK_ROWS, dq)
    for n, first, rows in PACKED:
        part = packed[:, first:first + rows, :]
        if n == "conv_dw":
            w[n] = part.transpose(1, 0, 2).reshape(rows, d)
        else:
            w[n] = part.reshape(1, N_CHIPS * rows * dq)
    for n, _, rows in WHOLE:
        w[n] = wts[n].reshape(rows, d)

    exchanges = {}
    passing = {}
    own_grads = {}

    def retire(tag, after):
        send, recv, srcs, lands, copies, kinds = exchanges[tag]
        srcs, lands = _copies_wait("grads_wait_" + tag, send, recv, srcs, lands, copies, after)
        own_grads[tag] = list(zip(srcs, kinds))
        copies = _pass_copies(len(lands))
        send, recv, _, lands, pin = _copies_start("grads_pass_" + tag, [], lands, copies, 3 * len(lands), ())
        passing[tag] = (send, recv, lands, copies)
        return pin

    def emit_grads(tag, grads, kinds):
        pins = [retire(old, [grads[0]]) for old in RETIRE_AT.get(tag, ())]
        copies = _grad_copies(kinds)
        lands = [lax.empty(_land_shape(gr, kd), gr.dtype) for gr, kd in zip(grads, kinds)]
        send, recv, srcs, lands, pin = _copies_start("grads_start_" + tag, grads, lands, copies,
                                                     GRAD_PEERS * len(grads), pins)
        exchanges[tag] = (send, recv, srcs, lands, copies, kinds)
        return [pin]

    def emit(grp, grads):
        return emit_grads(grp, [grads[key] for key, _, _, _ in GROUPS[grp]], [kind for _, _, _, kind in GROUPS[grp]])

    loss_cols, dx, g = _local_step(x2, tgt, w, fetch, emit, prefetch, after=pins)
    loss = lax.psum(jnp.sum(loss_cols), ("x", "y", "c"))

    gp = []
    for n, first, rows in PACKED:
        if n == "conv_dw":
            part = g[n].reshape(rows, N_CHIPS, dq).transpose(1, 0, 2)
        else:
            part = g[n].reshape(N_CHIPS, rows, dq)
        gp.append((part, first))
    g_packed = jnp.concatenate(
        [_pack_rows([(p[ci], first) for p, first in gp], PACK_ROWS, dq) for ci in range(N_CHIPS)], axis=0)
    emit_grads("vec", [g_packed, pack_whole(g)], ["row", "all"])
    late = [tag for tag in EMIT_ORDER if tag not in passing]
    contribs = {}

    def arrive(tag, after):
        send, recv, lands, copies = passing[tag]
        _, arrived = _copies_wait("grads_passed_" + tag, send, recv, [], lands, copies, after)
        contribs[tag] = [(c, own, kind) for c, (own, kind) in zip(arrived, own_grads[tag])]

    for tag in EMIT_ORDER:
        if tag not in late:
            arrive(tag, [dx])
    res = {}

    def adamw(n, contrib, layer=None, prev=None):
        arrived, own, kind = contrib
        if layer is None:
            shape = wts[n].shape
            r2 = shape[-2:]
            outs = _adamw_reduce("adamw_" + n, arrived, own, kind, ids, wts[n].reshape(r2), mom[n].reshape(r2),
                                 vel[n].reshape(r2))
            return [o.reshape(shape) for o in outs]
        return _adamw_reduce(f"adamw_{n}_{layer}", arrived, own, kind, ids, wts[n], mom[n], vel[n], layer, prev)

    for grp in ("attn", "conv2"):
        for (key, n, _, _), contrib in zip(GROUPS[grp], contribs[grp]):
            res[n] = adamw(n, contrib)
    for tag in late:
        retire(tag, [res["conv_w2"][0]])
    for part in ("d", "gu"):
        for (key, n, _, _), c0, c1 in zip(GROUPS["ffn0" + part], contribs["ffn0" + part], contribs["ffn1" + part]):
            res[n] = adamw(n, c1, 1, adamw(n, c0, 0))
    for tag in late:
        arrive(tag, [res["ffn_w_up"][0]])
    for (key, n, _, _), contrib in zip(GROUPS["conv1"], contribs["conv1"]):
        res[n] = adamw(n, contrib)
    outs = _adamw_reduce("adamw_packed", *contribs["vec"][0], ids, pack_shard(wts), pack_shard(mom), pack_shard(vel))
    for n, first, rows in PACKED:
        res[n] = [o[first:first + rows].reshape(wts[n].shape) for o in outs]
    outs = _adamw_reduce("adamw_whole", *contribs["vec"][1], ids, pack_whole(wts), pack_whole(mom), pack_whole(vel))
    for n, first, rows in WHOLE:
        res[n] = [o[first:first + rows].reshape(wts[n].shape) for o in outs]

    out = [loss, dx.reshape(x.shape)]
    for which in range(4):
        out += [res[n][which] for n in WEIGHT_NAMES]
    return tuple(out)
```

```python
import functools

import jax
import jax.numpy as jnp
from jax import lax
from jax.experimental import pallas as pl
from jax.experimental.pallas import tpu as pltpu

F32 = jnp.float32
BF16 = jnp.bfloat16

HEAD_DIM = 128
BRANCHES = ((128, 1), (512, 4), (2048, 16))
CONV_WIDTH = 31
CONV_HALO = 32
RMS_EPS = 1e-6
LN_EPS = 1e-5
ADAM_LR = 0.001
ADAM_B1 = 0.9
ADAM_B2 = 0.999
ADAM_EPS = 1e-08
ADAM_WD = 0.01
ADAM_STEP = 10
N_CHIPS = 4
N_DEV = 8
MESH = pl.DeviceIdType.MESH


def _sigmoid(x):
    return 0.5 * jnp.tanh(0.5 * x) + 0.5


def _row_tile(rows, want):
    t = min(rows, want)
    assert rows % t == 0, (rows, want)
    return t


_DOT_DIMS = {"nn": ((1,), (0,)), "nt": ((1,), (1,)), "tn": ((0,), (0,))}


ANY_SPEC = pl.BlockSpec(memory_space=pl.ANY)
WIDE_BN = 1024


def _mm(name, mode, a, bs, epilogue, outs, *, m, n, k, extras=(), bm=1024, bn=512, bk=None, after=()):
    bm, bn = min(bm, m), min(bn, n)
    bk = k if bk is None else min(bk, k)
    assert m % bm == 0 and n % bn == 0 and k % bk == 0, (name, m, n, k, bm, bn, bk)
    nk = k // bk
    a_list = list(a) if isinstance(a, (list, tuple)) else [a]
    na, nb, ne, no = len(a_list), len(bs), len(extras), len(outs)
    assert na in (1, nb)

    if mode == "tn":
        a_spec = pl.BlockSpec((bk, bm), lambda i, j, kk: (kk, i))
    else:
        a_spec = pl.BlockSpec((bm, bk), lambda i, j, kk: (i, kk))

    def b_spec(lead, off):
        if mode == "nt":
            blk, idx = (bn, bk), (lambda i, j, kk: (j + off, kk))
        else:
            blk, idx = (bk, bn), (lambda i, j, kk: (kk, j + off))
        if lead is None:
            return pl.BlockSpec(blk, idx)
        return pl.BlockSpec((None,) + blk, lambda i, j, kk: (lead,) + idx(i, j, kk))

    def e_spec(kind, off):
        if kind == "mn":
            return pl.BlockSpec((bm, bn), lambda i, j, kk: (i, j + off))
        return pl.BlockSpec((1, bn), lambda i, j, kk: (0, j + off))

    nf = len(after)
    in_specs = [a_spec] * na + [b_spec(l, o) for _, l, o in bs] + [e_spec(kd, o) for _, kd, o in extras]
    in_specs += [ANY_SPEC] * nf
    out_specs = [pl.BlockSpec((bm, bn), lambda i, j, kk: (i, j)) for _ in outs]
    out_shape = [jax.ShapeDtypeStruct((m, n), dt) for (dt,) in outs]
    dims = (_DOT_DIMS[mode], ((), ()))

    def body(*refs):
        a_refs = refs[:na]
        b_refs = refs[na:na + nb]
        e_refs = refs[na + nb:na + nb + ne]
        o_refs = refs[na + nb + ne + nf:na + nb + ne + nf + no]
        acc_refs = refs[na + nb + ne + nf + no:]
        avs = [a_ref[...].astype(BF16) for a_ref in a_refs]
        prods = [lax.dot_general(avs[bi % na], b_ref[...].astype(BF16), dims, preferred_element_type=F32)
                 for bi, b_ref in enumerate(b_refs)]

        def finish(accs):
            res = epilogue(accs, [e_ref[...] for e_ref in e_refs])
            for o_ref, r in zip(o_refs, res):
                o_ref[...] = r.astype(o_ref.dtype)

        if nk == 1:
            finish(prods)
        else:
            kk = pl.program_id(2)

            @pl.when(kk == 0)
            def _():
                for acc_ref, p in zip(acc_refs, prods):
                    acc_ref[...] = p

            @pl.when(kk > 0)
            def _():
                for acc_ref, p in zip(acc_refs, prods):
                    acc_ref[...] += p

            @pl.when(kk == nk - 1)
            def _():
                finish([acc_ref[...] for acc_ref in acc_refs])

    scratch = [] if nk == 1 else [pltpu.VMEM((bm, bn), F32) for _ in bs]
    res = pl.pallas_call(
        body,
        name=name,
        grid=(m // bm, n // bn, nk),
        in_specs=in_specs,
        out_specs=out_specs,
        out_shape=out_shape,
        scratch_shapes=scratch,
        compiler_params=pltpu.CompilerParams(dimension_semantics=("parallel", "parallel", "arbitrary")),
    )(*a_list, *[b for b, _, _ in bs], *[e for e, _, _ in extras], *after)
    return res


def _rms_fwd(name, x, gains, after=()):
    s, d = x.shape
    ng = gains.shape[0]
    t = _row_tile(s, 256)
    nf = len(after)

    def body(x_ref, g_ref, *refs):
        o_refs = refs[nf:]
        xv = x_ref[...]
        r = lax.rsqrt(jnp.mean(xv * xv, axis=-1, keepdims=True) + RMS_EPS)
        xh = xv * r
        for gi, o_ref in enumerate(o_refs):
            o_ref[...] = (xh * g_ref[gi:gi + 1, :]).astype(o_ref.dtype)

    return pl.pallas_call(
        body,
        name=name,
        grid=(s // t,),
        in_specs=[pl.BlockSpec((t, d), lambda i: (i, 0)), pl.BlockSpec((ng, d), lambda i: (0, 0))] + [ANY_SPEC] * nf,
        out_specs=[pl.BlockSpec((t, d), lambda i: (i, 0)) for _ in range(ng)],
        out_shape=[jax.ShapeDtypeStruct((s, d), BF16) for _ in range(ng)],
        compiler_params=pltpu.CompilerParams(dimension_semantics=("parallel",)),
    )(x, gains, *after)


def _rms_bwd(name, x, gains, dns, dres):
    s, d = x.shape
    ng = gains.shape[0]
    t = _row_tile(s, 256)

    def body(x_ref, g_ref, dres_ref, *refs):
        dn_refs = refs[:ng]
        dx_ref, dxb_ref, dg_ref, cs_ref = refs[ng:]
        i = pl.program_id(0)
        xv = x_ref[...]
        r = lax.rsqrt(jnp.mean(xv * xv, axis=-1, keepdims=True) + RMS_EPS)
        xh = xv * r
        dx = dres_ref[...]
        dgs = []
        for gi in range(ng):
            dn = dn_refs[gi][...].astype(F32)
            dxh = dn * g_ref[gi:gi + 1, :]
            dgs.append(jnp.sum(dn * xh, axis=0, keepdims=True))
            dx = dx + r * (dxh - xh * jnp.mean(dxh * xh, axis=-1, keepdims=True))
        dx_ref[...] = dx
        dxb_ref[...] = dx.astype(BF16)
        dg = jnp.concatenate(dgs, axis=0) if ng > 1 else dgs[0]
        cs = jnp.sum(dx, axis=0, keepdims=True)

        @pl.when(i == 0)
        def _():
            dg_ref[...] = dg
            cs_ref[...] = cs

        @pl.when(i > 0)
        def _():
            dg_ref[...] += dg
            cs_ref[...] += cs

    row = pl.BlockSpec((t, d), lambda i: (i, 0))
    return pl.pallas_call(
        body,
        name=name,
        grid=(s // t,),
        in_specs=[row, pl.BlockSpec((ng, d), lambda i: (0, 0)), row] + [row] * ng,
        out_specs=[row, row, pl.BlockSpec((ng, d), lambda i: (0, 0)), pl.BlockSpec((1, d), lambda i: (0, 0))],
        out_shape=[
            jax.ShapeDtypeStruct((s, d), F32),
            jax.ShapeDtypeStruct((s, d), BF16),
            jax.ShapeDtypeStruct((ng, d), F32),
            jax.ShapeDtypeStruct((1, d), F32),
        ],
        compiler_params=pltpu.CompilerParams(dimension_semantics=("arbitrary",)),
    )(x, gains, dres, *dns)


def _final_loss(name, h, gain, target):
    s, d = h.shape
    t = _row_tile(s, 256)

    def body(h_ref, g_ref, t_ref, dh_ref, dhb_ref, dg_ref, ls_ref):
        i = pl.program_id(0)
        xv = h_ref[...]
        g = g_ref[...]
        r = lax.rsqrt(jnp.mean(xv * xv, axis=-1, keepdims=True) + RMS_EPS)
        xh = xv * r
        err = xh * g - t_ref[...]
        ls = jnp.sum(err * err, axis=0, keepdims=True) * (0.5 / d)
        dy = err * (1.0 / d)
        dxh = dy * g
        dg = jnp.sum(dy * xh, axis=0, keepdims=True)
        dx = r * (dxh - xh * jnp.mean(dxh * xh, axis=-1, keepdims=True))
        dh_ref[...] = dx
        dhb_ref[...] = dx.astype(BF16)

        @pl.when(i == 0)
        def _():
            dg_ref[...] = dg
            ls_ref[...] = ls

        @pl.when(i > 0)
        def _():
            dg_ref[...] += dg
            ls_ref[...] += ls

    row = pl.BlockSpec((t, d), lambda i: (i, 0))
    vec = pl.BlockSpec((1, d), lambda i: (0, 0))
    return pl.pallas_call(
        body,
        name=name,
        grid=(s // t,),
        in_specs=[row, vec, row],
        out_specs=[row, row, vec, vec],
        out_shape=[
            jax.ShapeDtypeStruct((s, d), F32),
            jax.ShapeDtypeStruct((s, d), BF16),
            jax.ShapeDtypeStruct((1, d), F32),
            jax.ShapeDtypeStruct((1, d), F32),
        ],
        compiler_params=pltpu.CompilerParams(dimension_semantics=("arbitrary",)),
    )(h, gain, target)


SUBLANES = 8
CONV_LANES = 512
CONV_ROWS = 32
NORM_ROWS = 64


def _conv_tiles(s):
    t = _row_tile(s, 128)
    assert t % CONV_HALO == 0 and t % CONV_ROWS == 0
    return t, t // CONV_HALO


def _shifted_copies(dst, src, lanes, rows):
    for m in range(SUBLANES):
        n = rows if m == 0 else rows - SUBLANES
        dst[m, :n, :] = src[m:m + n, lanes]


def _shifted(copies, offset, n):
    m = offset % SUBLANES
    return copies[m, offset - m:offset - m + n, :]


def _dwconv_fwd(name, u, dw, dw_b, ln_g, ln_b):
    s, d = u.shape
    t, hb = _conv_tiles(s)
    w = dw.shape[0]
    lo = CONV_HALO - (w - 1)

    lw = min(CONV_LANES, d)

    def body(cur_ref, prev_ref, dw_ref, dwb_ref, lg_ref, lb_ref, c_ref, sw_ref, cat_ref, sh_ref):
        i = pl.program_id(0)
        cat_ref[CONV_HALO:, :] = cur_ref[...]

        @pl.when(i == 0)
        def _():
            cat_ref[:CONV_HALO, :] = jnp.zeros((CONV_HALO, d), F32)

        @pl.when(i > 0)
        def _():
            cat_ref[:CONV_HALO, :] = prev_ref[...]

        for lc in range(d // lw):
            lanes = slice(lc * lw, (lc + 1) * lw)
            _shifted_copies(sh_ref, cat_ref, lanes, t + CONV_HALO)
            for rc in range(t // CONV_ROWS):
                acc = jnp.broadcast_to(dwb_ref[:, lanes], (CONV_ROWS, lw))
                for kk in range(w):
                    acc = acc + dw_ref[kk:kk + 1, lanes] * _shifted(sh_ref, lo + kk + rc * CONV_ROWS, CONV_ROWS)
                c_ref[rc * CONV_ROWS:(rc + 1) * CONV_ROWS, lanes] = acc

        def norm_rows(ri, carry):
            rows = pl.ds(pl.multiple_of(ri * NORM_ROWS, NORM_ROWS), NORM_ROWS)
            cv = c_ref[rows, :]
            cc = cv - jnp.mean(cv, axis=-1, keepdims=True)
            var = jnp.mean(cc * cc, axis=-1, keepdims=True)
            ln = cc * lax.rsqrt(var + LN_EPS) * lg_ref[...] + lb_ref[...]
            sw_ref[rows, :] = (ln * _sigmoid(ln)).astype(BF16)
            return carry

        lax.fori_loop(0, t // NORM_ROWS, norm_rows, 0)

    row = pl.BlockSpec((t, d), lambda i: (i, 0))
    prev = pl.BlockSpec((CONV_HALO, d), lambda i: (jnp.maximum(i * hb - 1, 0), 0))
    vec = pl.BlockSpec((1, d), lambda i: (0, 0))
    return pl.pallas_call(
        body,
        name=name,
        grid=(s // t,),
        in_specs=[row, prev, pl.BlockSpec((w, d), lambda i: (0, 0)), vec, vec, vec],
        out_specs=[row, row],
        out_shape=[jax.ShapeDtypeStruct((s, d), F32), jax.ShapeDtypeStruct((s, d), BF16)],
        scratch_shapes=[pltpu.VMEM((CONV_HALO + t, d), F32), pltpu.VMEM((SUBLANES, CONV_HALO + t, lw), F32)],
        compiler_params=pltpu.CompilerParams(dimension_semantics=("parallel",)),
    )(u, u, dw, dw_b, ln_g, ln_b)


def _conv_ln_bwd(name, c, dsw, ln_g, ln_b, after=()):
    s, d = c.shape
    t = _row_tile(s, 256)
    nf = len(after)

    def body(c_ref, dsw_ref, lg_ref, lb_ref, *refs):
        dc_ref, sums_ref = refs[nf:]
        i = pl.program_id(0)
        cv = c_ref[...]
        g = lg_ref[...]
        mu = jnp.mean(cv, axis=-1, keepdims=True)
        cc = cv - mu
        rstd = lax.rsqrt(jnp.mean(cc * cc, axis=-1, keepdims=True) + LN_EPS)
        ch = cc * rstd
        ln = ch * g + lb_ref[...]
        sg = _sigmoid(ln)
        dln = dsw_ref[...] * (sg * (1.0 + ln * (1.0 - sg)))
        dch = dln * g
        dc = rstd * (dch - jnp.mean(dch, axis=-1, keepdims=True) - ch * jnp.mean(dch * ch, axis=-1, keepdims=True))
        dc_ref[...] = dc
        sums = jnp.concatenate(
            [
                jnp.sum(dln * ch, axis=0, keepdims=True),
                jnp.sum(dln, axis=0, keepdims=True),
                jnp.sum(dc, axis=0, keepdims=True),
                jnp.zeros((1, d), F32),
            ],
            axis=0,
        )

        @pl.when(i == 0)
        def _():
            sums_ref[...] = sums

        @pl.when(i > 0)
        def _():
            sums_ref[...] += sums

    row = pl.BlockSpec((t, d), lambda i: (i, 0))
    vec = pl.BlockSpec((1, d), lambda i: (0, 0))
    return pl.pallas_call(
        body,
        name=name,
        grid=(s // t,),
        in_specs=[row, row, vec, vec] + [ANY_SPEC] * nf,
        out_specs=[row, pl.BlockSpec((4, d), lambda i: (0, 0))],
        out_shape=[jax.ShapeDtypeStruct((s, d), F32), jax.ShapeDtypeStruct((4, d), F32)],
        compiler_params=pltpu.CompilerParams(dimension_semantics=("arbitrary",)),
    )(c, dsw, ln_g, ln_b, *after)


def _conv_dw_bwd(name, dc, u, a, gt, dw):
    s, d = dc.shape
    t, hb = _conv_tiles(s)
    w = dw.shape[0]
    lo = CONV_HALO - (w - 1)
    nt = s // t
    lw = min(CONV_LANES, d)

    def body(dc_ref, dcn_ref, u_ref, up_ref, a_ref, gt_ref, dw_ref, dpre_ref, ddw_ref, db_ref, dcat_ref, ucat_ref,
             dsh_ref, ush_ref, ddw_acc, db_acc):
        i = pl.program_id(0)
        dcat_ref[:t, :] = dc_ref[...]
        ucat_ref[CONV_HALO:, :] = u_ref[...]

        @pl.when(i == nt - 1)
        def _():
            dcat_ref[t:, :] = jnp.zeros((CONV_HALO, d), F32)

        @pl.when(i < nt - 1)
        def _():
            dcat_ref[t:, :] = dcn_ref[...]

        @pl.when(i == 0)
        def _():
            ucat_ref[:CONV_HALO, :] = jnp.zeros((CONV_HALO, d), F32)

        @pl.when(i > 0)
        def _():
            ucat_ref[:CONV_HALO, :] = up_ref[...]

        @pl.when(i == 0)
        def _():
            ddw_acc[...] = jnp.zeros(ddw_acc.shape, F32)
            db_acc[...] = jnp.zeros(db_acc.shape, F32)

        def fold(p):
            return functools.reduce(jnp.add, [p[j:j + SUBLANES] for j in range(0, CONV_ROWS, SUBLANES)])

        for lc in range(d // lw):
            lanes = slice(lc * lw, (lc + 1) * lw)
            gate_lanes = slice(d + lc * lw, d + (lc + 1) * lw)
            _shifted_copies(dsh_ref, dcat_ref, lanes, t + CONV_HALO)
            _shifted_copies(ush_ref, ucat_ref, lanes, t + CONV_HALO)
            for rc in range(t // CONV_ROWS):
                r0 = rc * CONV_ROWS
                rows = slice(r0, r0 + CONV_ROWS)
                dcv = dc_ref[rows, lanes]
                du = jnp.zeros((CONV_ROWS, lw), F32)
                for kk in range(w):
                    du = du + dw_ref[kk:kk + 1, lanes] * _shifted(dsh_ref, w - 1 - kk + r0, CONV_ROWS)
                    ddw_acc[kk, :, lanes] += fold(dcv * _shifted(ush_ref, lo + kk + r0, CONV_ROWS))
                av = a_ref[rows, lanes].astype(F32)
                sg = _sigmoid(gt_ref[rows, lanes].astype(F32))
                da = du * sg
                dgt = du * av * sg * (1.0 - sg)
                dpre_ref[rows, lanes] = da.astype(BF16)
                dpre_ref[rows, gate_lanes] = dgt.astype(BF16)
                db_acc[:, lanes] += fold(da)
                db_acc[:, gate_lanes] += fold(dgt)

        @pl.when(i == nt - 1)
        def _():
            ddw_ref[...] = jnp.sum(ddw_acc[...], axis=1)
            db_ref[...] = jnp.sum(db_acc[...], axis=0, keepdims=True)

    row = pl.BlockSpec((t, d), lambda i: (i, 0))
    nxt = pl.BlockSpec((CONV_HALO, d), lambda i: (jnp.minimum((i + 1) * hb, s // CONV_HALO - 1), 0))
    prev = pl.BlockSpec((CONV_HALO, d), lambda i: (jnp.maximum(i * hb - 1, 0), 0))
    return pl.pallas_call(
        body,
        name=name,
        grid=(nt,),
        in_specs=[row, nxt, row, prev, row, row, pl.BlockSpec((w, d), lambda i: (0, 0))],
        out_specs=[
            pl.BlockSpec((t, 2 * d), lambda i: (i, 0)),
            pl.BlockSpec((w, d), lambda i: (0, 0)),
            pl.BlockSpec((1, 2 * d), lambda i: (0, 0)),
        ],
        out_shape=[
            jax.ShapeDtypeStruct((s, 2 * d), BF16),
            jax.ShapeDtypeStruct((w, d), F32),
            jax.ShapeDtypeStruct((1, 2 * d), F32),
        ],
        scratch_shapes=[pltpu.VMEM((t + CONV_HALO, d), F32), pltpu.VMEM((CONV_HALO + t, d), F32)]
        + [pltpu.VMEM((SUBLANES, CONV_HALO + t, lw), F32)] * 2
        + [pltpu.VMEM((w, SUBLANES, d), F32), pltpu.VMEM((SUBLANES, 2 * d), F32)],
        compiler_params=pltpu.CompilerParams(dimension_semantics=("arbitrary",)),
    )(dc, dc, u, u, a, gt, dw)


def _alibi_slopes(n_heads):
    h = jnp.arange(1, n_heads + 1, dtype=F32)
    return jnp.exp2(-8.0 * h / n_heads)


def _band_masks(bq):
    qi = lax.broadcasted_iota(jnp.int32, (bq, bq), 0)
    kj = lax.broadcasted_iota(jnp.int32, (bq, bq), 1)
    return qi - kj, qi - kj + bq


ATTN_GROUP = 32
ATTN_GROUP_BWD = 16


def _attn_fwd(name, q, k, v, slopes, bq):
    s, dm = q.shape
    nh = dm // HEAD_DIM
    nt = s // bq
    nbr = len(BRANCHES)
    scale = HEAD_DIM ** -0.5
    nt_dims = (((1,), (1,)), ((), ()))

    def body(sl_ref, q_ref, k_ref, v_ref, o_ref, ob_ref, l_ref, tmp, qr, kr, va, orm, lrm, onat, lnat, sbuf, mbuf):
        slope = sl_ref[pl.program_id(0)]
        jc, jp = _band_masks(bq)
        va[:, HEAD_DIM:] = jnp.ones((s, HEAD_DIM), BF16)
        for bi, (win, dil) in enumerate(BRANCHES):
            ll = s // dil
            nblk = ll // bq
            if dil == 1:
                sq, sk = q_ref, k_ref
                va[:, :HEAD_DIM] = v_ref[...]
                d_o, d_l = onat.at[bi], lnat.at[bi]
            else:
                for src, dst, wide in ((q_ref, qr, False), (k_ref, kr, False), (v_ref, va, True)):
                    tmp[...] = src[...].astype(F32)
                    for r in range(dil):
                        part = tmp[pl.ds(r, ll, stride=dil), :].astype(BF16)
                        if wide:
                            dst[r * ll:(r + 1) * ll, :HEAD_DIM] = part
                        else:
                            dst[r * ll:(r + 1) * ll, :] = part
                sq, sk = qr, kr
                d_o, d_l = orm, lrm
            bias_c = jnp.where(jc >= 0, jc.astype(F32) * (slope * dil), 1e30)
            bias_p = jnp.where(jp <= bq, jp.astype(F32) * (slope * dil), 1e30)

            def group(gi, carry):
                rows = []
                for g in range(ATTN_GROUP):
                    ti = gi * ATTN_GROUP + g
                    row = pl.ds(pl.multiple_of(ti * bq, bq), bq)
                    prow = pl.ds(pl.multiple_of(jnp.maximum(ti - 1, 0) * bq, bq), bq)
                    rows.append((row, prow))
                    qh = sq[row, :]
                    sc = lax.dot_general(qh, sk[row, :], nt_dims, preferred_element_type=F32) * scale - bias_c
                    sp = lax.dot_general(qh, sk[prow, :], nt_dims, preferred_element_type=F32) * scale - bias_p
                    sp = jnp.where(lax.rem(ti, nblk) > 0, sp, -1e30)
                    sbuf[g, :, :bq] = sc
                    sbuf[g, :, bq:] = sp
                    mbuf[g] = jnp.maximum(jnp.max(sc, axis=-1, keepdims=True), jnp.max(sp, axis=-1, keepdims=True))
                for g, (row, prow) in enumerate(rows):
                    mx = mbuf[g]
                    p = jnp.exp(sbuf[g] - mx).astype(BF16)
                    ov = jnp.dot(p[:, :bq], va[row, :], preferred_element_type=F32)
                    ov = ov + jnp.dot(p[:, bq:], va[prow, :], preferred_element_type=F32)
                    den = ov[:, HEAD_DIM:]
                    d_o[row, :] = ov[:, :HEAD_DIM] / den
                    d_l[row, :] = mx + jnp.log(den)
                return carry

            lax.fori_loop(0, nt // ATTN_GROUP, group, 0)
            if dil > 1:
                for r in range(dil):
                    onat[bi, pl.ds(r, ll, stride=dil), :] = orm[r * ll:(r + 1) * ll, :]
                    lnat[bi, pl.ds(r, ll, stride=dil), :] = lrm[r * ll:(r + 1) * ll, :]

        def merge(ti, carry):
            rows = pl.ds(pl.multiple_of(ti * bq, bq), bq)
            ls = [lnat[bi, rows, :] for bi in range(nbr)]
            mx = functools.reduce(jnp.maximum, ls)
            es = [jnp.exp(l - mx) for l in ls]
            tot = functools.reduce(jnp.add, es)
            inv = 1.0 / tot
            o = functools.reduce(jnp.add, [e * inv * onat[bi, rows, :] for bi, e in enumerate(es)])
            o_ref[rows, :] = o
            ob_ref[rows, :] = o.astype(BF16)
            l_ref[rows, :] = mx + jnp.log(tot)
            return carry

        lax.fori_loop(0, nt, merge, 0)

    head = pl.BlockSpec((s, HEAD_DIM), lambda h: (0, h))
    return pl.pallas_call(
        body,
        name=name,
        grid=(nh,),
        in_specs=[pl.BlockSpec(memory_space=pltpu.SMEM), head, head, head],
        out_specs=[head, head, head],
        out_shape=[jax.ShapeDtypeStruct((s, dm), F32), jax.ShapeDtypeStruct((s, dm), BF16),
                   jax.ShapeDtypeStruct((s, dm), F32)],
        scratch_shapes=[pltpu.VMEM((s, HEAD_DIM), F32)] + [pltpu.VMEM((s, HEAD_DIM), BF16)] * 2
        + [pltpu.VMEM((s, 2 * HEAD_DIM), BF16)] + [pltpu.VMEM((s, HEAD_DIM), F32)] * 2
        + [pltpu.VMEM((nbr, s, HEAD_DIM), F32)] * 2
        + [pltpu.VMEM((ATTN_GROUP, bq, 2 * bq), F32), pltpu.VMEM((ATTN_GROUP, bq, 1), F32)],
        compiler_params=pltpu.CompilerParams(dimension_semantics=("parallel",)),
    )(slopes, q, k, v)


def _attn_bwd(name, q, k, v, o, lse, do, slopes, bq):
    s, dm = q.shape
    nh = dm // HEAD_DIM
    nt = s // bq
    scale = HEAD_DIM ** -0.5
    nt_dims = (((1,), (1,)), ((), ()))
    tn_dims = (((0,), (0,)), ((), ()))

    def body(sl_ref, q_ref, k_ref, v_ref, o_ref, l_ref, do_ref, dq_ref, dk_ref, dv_ref,
             tmp, qr, kr, vr, dor, rown, rowr, dqr, dkr, dvr, aq, ak, av, pbuf, dsbuf):
        slope = sl_ref[pl.program_id(0)]
        jc, jp = _band_masks(bq)
        lane = lax.broadcasted_iota(jnp.int32, (bq, HEAD_DIM), 1)

        def row_terms(ti, carry):
            rows = pl.ds(pl.multiple_of(ti * bq, bq), bq)
            dl = jnp.sum(do_ref[rows, :].astype(F32) * o_ref[rows, :], axis=-1, keepdims=True)
            rown[rows, :] = jnp.where(lane == 0, l_ref[rows, :], dl)
            return carry

        lax.fori_loop(0, nt, row_terms, 0)

        for bi, (win, dil) in enumerate(BRANCHES):
            ll = s // dil
            nblk = ll // bq
            if dil == 1:
                sq, sk, sv, sdo, srow = q_ref, k_ref, v_ref, do_ref, rown
                gq, gk, gv = aq, ak, av
            else:
                for src, dst in ((q_ref, qr), (k_ref, kr), (v_ref, vr), (do_ref, dor)):
                    tmp[...] = src[...].astype(F32)
                    for r in range(dil):
                        dst[r * ll:(r + 1) * ll, :] = tmp[pl.ds(r, ll, stride=dil), :].astype(BF16)
                for r in range(dil):
                    rowr[r * ll:(r + 1) * ll, :] = rown[pl.ds(r, ll, stride=dil), :]
                sq, sk, sv, sdo, srow = qr, kr, vr, dor, rowr
                gq, gk, gv = dqr, dkr, dvr
            bias_c = jnp.where(jc >= 0, jc.astype(F32) * (slope * dil), 1e30)
            bias_p = jnp.where(jp <= bq, jp.astype(F32) * (slope * dil), 1e30)

            def group(gi, carry):
                rows = []
                for g in range(ATTN_GROUP_BWD):
                    ti = gi * ATTN_GROUP_BWD + g
                    row = pl.ds(pl.multiple_of(ti * bq, bq), bq)
                    prow = pl.ds(pl.multiple_of(jnp.maximum(ti - 1, 0) * bq, bq), bq)
                    rows.append((row, prow))
                    has_prev = lax.rem(ti, nblk) > 0
                    qh, doh = sq[row, :], sdo[row, :]
                    terms = srow[row, :]
                    lc = terms[:, 0:1]
                    dl = terms[:, 1:2]
                    for half, kv_rows, bias in ((0, row, bias_c), (1, prow, bias_p)):
                        sc = lax.dot_general(qh, sk[kv_rows, :], nt_dims, preferred_element_type=F32) * scale - bias
                        if half:
                            sc = jnp.where(has_prev, sc, -1e30)
                        p = jnp.exp(sc - lc)
                        dp = lax.dot_general(doh, sv[kv_rows, :], nt_dims, preferred_element_type=F32)
                        pbuf[g, :, half * bq:(half + 1) * bq] = p.astype(BF16)
                        dsbuf[g, :, half * bq:(half + 1) * bq] = (p * (dp - dl) * scale).astype(BF16)
                carry_k = carry_v = None
                for g, (row, prow) in enumerate(rows):
                    qh, doh = sq[row, :], sdo[row, :]
                    ds_c, ds_p = dsbuf[g, :, :bq], dsbuf[g, :, bq:]
                    p_c, p_p = pbuf[g, :, :bq], pbuf[g, :, bq:]
                    dq = jnp.dot(ds_c, sk[row, :], preferred_element_type=F32)
                    dq = dq + jnp.dot(ds_p, sk[prow, :], preferred_element_type=F32)
                    gq[row, :] = dq
                    dk_p = lax.dot_general(ds_p, qh, tn_dims, preferred_element_type=F32)
                    dv_p = lax.dot_general(p_p, doh, tn_dims, preferred_element_type=F32)
                    if g == 0:
                        @pl.when(gi > 0)
                        def _():
                            gk[prow, :] += dk_p
                            gv[prow, :] += dv_p
                    else:
                        gk[rows[g - 1][0], :] = carry_k + dk_p
                        gv[rows[g - 1][0], :] = carry_v + dv_p
                    carry_k = lax.dot_general(ds_c, qh, tn_dims, preferred_element_type=F32)
                    carry_v = lax.dot_general(p_c, doh, tn_dims, preferred_element_type=F32)
                gk[rows[-1][0], :] = carry_k
                gv[rows[-1][0], :] = carry_v
                return carry

            lax.fori_loop(0, nt // ATTN_GROUP_BWD, group, 0)
            if dil > 1:
                for acc, rm in ((aq, dqr), (ak, dkr), (av, dvr)):
                    for r in range(dil):
                        acc[pl.ds(r, ll, stride=dil), :] += rm[r * ll:(r + 1) * ll, :]

        dq_ref[...] = aq[...].astype(BF16)
        dk_ref[...] = ak[...].astype(BF16)
        dv_ref[...] = av[...].astype(BF16)

    head = pl.BlockSpec((s, HEAD_DIM), lambda h: (0, h))
    f32buf = pltpu.VMEM((s, HEAD_DIM), F32)
    b16buf = pltpu.VMEM((s, HEAD_DIM), BF16)
    return pl.pallas_call(
        body,
        name=name,
        grid=(nh,),
        in_specs=[pl.BlockSpec(memory_space=pltpu.SMEM)] + [head] * 6,
        out_specs=[head] * 3,
        out_shape=[jax.ShapeDtypeStruct((s, dm), BF16)] * 3,
        scratch_shapes=[f32buf] + [b16buf] * 4 + [f32buf] * 8
        + [pltpu.VMEM((ATTN_GROUP_BWD, bq, 2 * bq), BF16)] * 2,
        compiler_params=pltpu.CompilerParams(dimension_semantics=("parallel",)),
    )(slopes, q, k, v, o, lse, do)


def _ep_id(accs, ex):
    return [accs[0]]


def _ep_all(accs, ex):
    return list(accs)


def _ep_sum(accs, ex):
    return [accs[0] + accs[1]]


def _ep_add(accs, ex):
    return [accs[0] + ex[0].astype(F32)]


def _ep_bias_res(accs, ex):
    return [accs[0] + ex[0] + ex[1]]


def _ep_glu(accs, ex):
    a = accs[0] + ex[0]
    gt = accs[1] + ex[1]
    return [a * _sigmoid(gt), a, gt]


def _ep_swiglu(accs, ex):
    g, u = accs
    return [g, u, g * _sigmoid(g) * u]


def _ep_swiglu_bwd(accs, ex):
    dact = accs[0]
    g = ex[0].astype(F32)
    u = ex[1].astype(F32)
    sg = _sigmoid(g)
    return [dact * u * (sg * (1.0 + g * (1.0 - sg))), dact * g * sg]


def _ffn_fwd(tag, h, gain, get_gate_up, get_down):
    s, d = h.shape
    (n,) = _rms_fwd(f"{tag}_norm", h, gain)
    wg, wu = get_gate_up(n)
    f = wg.shape[-1]
    g, u, act = _mm(f"{tag}_gate_up", "nn", n, [(wg, None, 0), (wu, None, 0)], _ep_swiglu,
                    [(BF16,), (BF16,), (BF16,)], m=s, n=f, k=d)
    wd = get_down(act)
    (out,) = _mm(f"{tag}_down", "nn", act, [(wd, None, 0)], _ep_add, [(F32,)], m=s, n=d, k=f,
                 extras=[(h, "mn", 0)])
    return out, (n, g, u, act), dict(gate=wg, up=wu, down=wd)


def _ffn_bwd(tag, h_in, gain, wg, wu, wd, saved, dh, dhb, after, emit):
    s, d = h_in.shape
    f = wg.shape[-1]
    n, g, u, act = saved
    dg, du = _mm(f"{tag}_bwd_dact", "nt", dhb, [(wd, None, 0)], _ep_swiglu_bwd, [(BF16,), (BF16,)],
                 m=s, n=f, k=d, extras=[(g, "mn", 0), (u, "mn", 0)], after=after)
    (dwd,) = _mm(f"{tag}_bwd_dwd", "tn", act, [(dhb, None, 0)], _ep_id, [(BF16,)], m=f, n=d, k=s,
                 bm=f // 4)
    pin = emit("d", dict(down=dwd))
    dwg, dwu = _mm(f"{tag}_bwd_dwgu", "tn", n, [(dg, None, 0), (du, None, 0)], _ep_all, [(BF16,), (BF16,)],
                   m=d, n=f, k=s, after=pin)
    pin = emit("gu", dict(gate=dwg, up=dwu))
    (dn,) = _mm(f"{tag}_bwd_dn", "nt", [dg, du], [(wg, None, 0), (wu, None, 0)], _ep_sum, [(BF16,)],
                m=s, n=d, k=f, bm=512, bn=256, after=pin)
    dx, dxb, dgain, cs = _rms_bwd(f"{tag}_bwd_norm", h_in, gain, [dn], dh)
    return dx, dxb, dgain, cs


def _local_step(x, target, w, fetch, emit, prefetch=lambda group, after: (), after=()):
    s, d = x.shape
    nh = d // HEAD_DIM
    bq = BRANCHES[0][0] // BRANCHES[0][1]
    assert all(win // dil == bq for win, dil in BRANCHES)
    assert BRANCHES[0][1] == 1 and all(dil > 1 for _, dil in BRANCHES[1:])
    slopes = _alibi_slopes(nh)
    nd = d // 512 if d >= 512 else 1
    bn = d // nd

    (n1,) = _rms_fwd("a_norm", x, w["a_norm_g"], after=after)
    prefetch("rest", n1)
    w_conv1 = fetch("conv1", n1)["conv_w1"]
    glu, a, gt = _mm("conv_pw1_glu", "nn", n1, [(w_conv1, None, 0), (w_conv1, None, nd)], _ep_glu,
                     [(F32,), (BF16,), (BF16,)], m=s, n=d, k=d, bn=bn,
                     extras=[(w["conv_b1"], "n", 0), (w["conv_b1"], "n", nd)])
    c, sw = _dwconv_fwd("conv_dw_ln", glu, w["conv_dw"], w["conv_dw_b"], w["conv_ln_g"], w["conv_ln_b"])
    w_conv2 = fetch("conv2", sw)["conv_w2"]
    (h1,) = _mm("conv_pw2", "nn", sw, [(w_conv2, None, 0)], _ep_bias_res, [(F32,)], m=s, n=d, k=d,
                extras=[(w["conv_b2"], "n", 0), (x, "mn", 0)], bn=WIDE_BN)
    def gate_up0(after):
        wts = fetch("ffn0gu", after)
        return wts["gate"], wts["up"]

    h2, ffn0, wf0 = _ffn_fwd("ffn0", h1, w["ffn_norm_g"][0:1], gate_up0, lambda after: fetch("ffn0d", after)["down"])
    wa = fetch("attn", h2)
    kvn, qn = _rms_fwd("kvq_norm", h2, jnp.concatenate([w["kv_norm_g"], w["b_norm_g"]], axis=0))
    k, v = _mm("kv_proj", "nn", kvn, [(wa["w_k"], None, 0), (wa["w_v"], None, 0)], _ep_all, [(BF16,), (BF16,)],
               m=s, n=d, k=d, bn=WIDE_BN)
    (q,) = _mm("q_proj", "nn", qn, [(wa["w_q"], None, 0)], _ep_id, [(BF16,)], m=s, n=d, k=d, bn=WIDE_BN)
    att, attb, lse = _attn_fwd("attn_fwd", q, k, v, slopes, bq)
    pin = prefetch("ffn1", attb)
    (h3,) = _mm("o_proj", "nn", attb, [(wa["w_o"], None, 0)], _ep_add, [(F32,)], m=s, n=d, k=d,
                extras=[(h2, "mn", 0)], bn=WIDE_BN, after=pin)
    got1 = {}

    def gate_up1(after):
        got1.update(fetch("ffn1", after))
        return got1["gate"], got1["up"]

    h4, ffn1, wf1 = _ffn_fwd("ffn1", h3, w["ffn_norm_g"][1:2], gate_up1, lambda after: got1["down"])
    dh4, dh4b, d_final_g, loss_cols = _final_loss("final_loss", h4, w["final_norm_g"], target)

    g = {}
    ga = {}
    dh3, dh3b, dgain1, _ = _ffn_bwd("ffn1", h3, w["ffn_norm_g"][1:2], wf1["gate"], wf1["up"], wf1["down"], ffn1,
                                    dh4, dh4b, (), lambda part, grads: emit("ffn1" + part, grads))
    (datt,) = _mm("o_proj_bwd_dx", "nt", dh3b, [(wa["w_o"], None, 0)], _ep_id, [(BF16,)], m=s, n=d, k=d, bn=WIDE_BN)
    (ga["w_o"],) = _mm("o_proj_bwd_dw", "tn", attb, [(dh3b, None, 0)], _ep_id, [(BF16,)], m=d, n=d, k=s, bn=WIDE_BN)
    dq, dk, dv = _attn_bwd("attn_bwd", q, k, v, att, lse, datt, slopes, bq)
    (ga["w_q"],) = _mm("q_proj_bwd_dw", "tn", qn, [(dq, None, 0)], _ep_id, [(BF16,)], m=d, n=d, k=s, bn=WIDE_BN)
    ga["w_k"], ga["w_v"] = _mm("kv_proj_bwd_dw", "tn", kvn, [(dk, None, 0), (dv, None, 0)], _ep_all,
                               [(BF16,), (BF16,)], m=d, n=d, k=s)
    pin = emit("attn", ga)
    (dqn,) = _mm("q_proj_bwd_dx", "nt", dq, [(wa["w_q"], None, 0)], _ep_id, [(BF16,)], m=s, n=d, k=d, after=pin,
                 bn=WIDE_BN)
    (dkvn,) = _mm("kv_proj_bwd_dx", "nt", [dk, dv], [(wa["w_k"], None, 0), (wa["w_v"], None, 0)], _ep_sum, [(BF16,)],
                  m=s, n=d, k=d, bn=WIDE_BN)
    dh2, dh2b, dg_kvq, _ = _rms_bwd("kvq_norm_bwd", h2, jnp.concatenate([w["kv_norm_g"], w["b_norm_g"]], axis=0),
                                    [dkvn, dqn], dh3)
    dh1, dh1b, dgain0, cs_h1 = _ffn_bwd("ffn0", h1, w["ffn_norm_g"][0:1], wf0["gate"], wf0["up"], wf0["down"], ffn0,
                                        dh2, dh2b, (), lambda part, grads: emit("ffn0" + part, grads))
    (dsw,) = _mm("conv_pw2_bwd_dx", "nt", dh1b, [(w_conv2, None, 0)], _ep_id, [(F32,)], m=s, n=d, k=d, bn=WIDE_BN)
    (dw2,) = _mm("conv_pw2_bwd_dw", "tn", sw, [(dh1b, None, 0)], _ep_id, [(BF16,)], m=d, n=d, k=s, bn=WIDE_BN)
    pin = emit("conv2", dict(conv_w2=dw2))
    dc, ln_sums = _conv_ln_bwd("conv_ln_bwd", c, dsw, w["conv_ln_g"], w["conv_ln_b"], after=pin)
    dpre, ddw, db1 = _conv_dw_bwd("conv_dw_bwd", dc, glu, a, gt, w["conv_dw"])
    (dw1,) = _mm("conv_pw1_bwd_dw", "tn", n1, [(dpre, None, 0)], _ep_id, [(BF16,)], m=d, n=2 * d, k=s)
    pin = emit("conv1", dict(conv_w1=dw1))
    (dn1,) = _mm("conv_pw1_bwd_dx", "nt", dpre, [(w_conv1, None, 0)], _ep_id, [(BF16,)], m=s, n=d, k=2 * d,
                 after=pin)
    dx, _, d_a_norm, _ = _rms_bwd("a_norm_bwd", x, w["a_norm_g"], [dn1], dh1)

    g.update(
        a_norm_g=d_a_norm, conv_b1=db1, conv_dw=ddw, conv_dw_b=ln_sums[2:3], conv_ln_g=ln_sums[0:1],
        conv_ln_b=ln_sums[1:2], conv_b2=cs_h1, kv_norm_g=dg_kvq[0:1], b_norm_g=dg_kvq[1:2],
        ffn_norm_g=jnp.concatenate([dgain0, dgain1], axis=0), final_norm_g=d_final_g,
    )
    return loss_cols, dx, g


HBM_SPEC = pl.BlockSpec(memory_space=pltpu.HBM)


def _mesh_place():
    x, y, c = lax.axis_index("x"), lax.axis_index("y"), lax.axis_index("c")
    chips = [(1 - x, y), (x, 1 - y), (1 - x, 1 - y)]
    return x, y, c, chips


def _shard_view(ref, kind, s, half=None):
    rows, cols = ref.shape
    if kind == "col":
        cw = cols // N_CHIPS
        if half is None:
            return ref.at[pl.ds(0, rows), pl.ds(s * cw, cw)]
        return ref.at[pl.ds(half * (rows // 2), rows // 2), pl.ds(s * cw, cw)]
    r = rows // N_CHIPS
    if half is None:
        return ref.at[pl.ds(s * r, r), pl.ds(0, cols)]
    return ref.at[pl.ds(s * r + half * (r // 2), r // 2), pl.ds(0, cols)]


def _full_shape(shard, kind):
    r, cw = shard.shape[-2:]
    return (r, cw * N_CHIPS) if kind == "col" else (r * N_CHIPS, cw)


def _gather_weights(shards, kinds, after=()):
    nt = len(shards)
    nf = len(after)
    fulls = [jax.ShapeDtypeStruct(_full_shape(sh, kind), sh.dtype) for sh, kind in zip(shards, kinds)]

    def body(*refs):
        src = refs[:nt]
        dst = refs[nt + nf:2 * nt + nf]
        send, recv, fsend, frecv, local = refs[2 * nt + nf:]
        x, y, c, chips = _mesh_place()
        s = 2 * x + y
        sib = (x, y, 1 - c)

        def half_of_shard(t):
            r, cw = src[t].shape
            return src[t].at[pl.ds(c * (r // 2), r // 2), pl.ds(0, cw)]

        locals_ = [pltpu.make_async_copy(src[t], _shard_view(dst[t], kinds[t], s), local.at[t]) for t in range(nt)]
        for cp in locals_:
            cp.start()
        sends = []
        for t in range(nt):
            for j, chip in enumerate(chips):
                cp = pltpu.make_async_remote_copy(
                    src_ref=half_of_shard(t), dst_ref=_shard_view(dst[t], kinds[t], s, c),
                    send_sem=send.at[t, j], recv_sem=recv.at[t, j], device_id=(*chip, c), device_id_type=MESH)
                cp.start()
                sends.append(cp)
        for t in range(nt):
            for j, (px, py) in enumerate(chips):
                landed = _shard_view(dst[t], kinds[t], 2 * px + py, c)
                pltpu.make_async_remote_copy(
                    src_ref=half_of_shard(t), dst_ref=landed, send_sem=send.at[t, j], recv_sem=recv.at[t, j],
                    device_id=(px, py, c), device_id_type=MESH).wait_recv()
                cp = pltpu.make_async_remote_copy(
                    src_ref=landed, dst_ref=landed, send_sem=fsend.at[t, j], recv_sem=frecv.at[t, j],
                    device_id=sib, device_id_type=MESH)
                cp.start()
                sends.append(cp)
        for t in range(nt):
            for j, (px, py) in enumerate(chips):
                other = _shard_view(dst[t], kinds[t], 2 * px + py, 1 - c)
                pltpu.make_async_remote_copy(
                    src_ref=other, dst_ref=other, send_sem=fsend.at[t, j], recv_sem=frecv.at[t, j],
                    device_id=sib, device_id_type=MESH).wait_recv()
        for cp in sends:
            cp.wait_send()
        for cp in locals_:
            cp.wait()

    return pl.pallas_call(
        body,
        name="gather_weights",
        in_specs=[HBM_SPEC] * nt + [ANY_SPEC] * nf,
        out_specs=[HBM_SPEC] * nt,
        out_shape=fulls,
        scratch_shapes=[pltpu.SemaphoreType.DMA((nt, 3))] * 4 + [pltpu.SemaphoreType.DMA((nt,))],
    )(*shards, *after)


def _row_blocks(rows, want=512):
    nb = 1
    while rows // nb > want or rows % nb or (rows // nb) % 16:
        nb += 1
        if nb > rows:
            return rows, 1
    return rows // nb, nb


def _place_own(name, src, kind, land, ids, lead=0):
    r, cw = src.shape[-2:]
    tr, nb = _row_blocks(r)
    if src.ndim == 3:
        src_spec = pl.BlockSpec((None, tr, cw), lambda i, ids_ref: (lead, i, 0))
    else:
        src_spec = pl.BlockSpec((tr, cw), lambda i, ids_ref: (i, 0))
    if kind == "col":
        dst_spec = pl.BlockSpec((tr, cw), lambda i, ids_ref: (i, ids_ref[0]))
    else:
        dst_spec = pl.BlockSpec((tr, cw), lambda i, ids_ref: (ids_ref[0] * nb + i, 0))

    def body(ids_ref, s_ref, o_ref):
        o_ref[...] = s_ref[...].astype(o_ref.dtype)

    return pl.pallas_call(
        body,
        name=name,
        grid_spec=pltpu.PrefetchScalarGridSpec(num_scalar_prefetch=1, grid=(nb,), in_specs=[src_spec],
                                               out_specs=dst_spec),
        out_shape=land,
        compiler_params=pltpu.CompilerParams(dimension_semantics=("parallel",)),
    )(ids, src)


SEM_SPEC = pl.BlockSpec(memory_space=pltpu.SEMAPHORE)
SIDE_EFFECT = pltpu.SideEffectType.DATAFLOW_SIDE_EFFECTING


def _copies_start(name, srcs, lands, copies, n_sems, after):
    ns, nl, nf = len(srcs), len(lands), len(after)

    def body(*refs):
        src, land = refs[:ns], refs[ns:ns + nl]
        send, recv = refs[ns + nl + nf], refs[ns + nl + nf + 1]
        pin = refs[-1]
        for cp in copies(src, land, send, recv, _mesh_place()):
            cp.start()
        pin[...] = jnp.zeros_like(pin)

    arrs = list(srcs) + list(lands)
    res = pl.pallas_call(
        body,
        name=name,
        in_specs=[HBM_SPEC] * (ns + nl) + [ANY_SPEC] * nf,
        out_specs=[SEM_SPEC, SEM_SPEC] + [HBM_SPEC] * (ns + nl) + [pl.BlockSpec(memory_space=pltpu.VMEM)],
        out_shape=[pltpu.SemaphoreType.DMA((n_sems,)), pltpu.SemaphoreType.DMA((n_sems,))]
        + [pltpu.HBM(a.shape, a.dtype) for a in arrs] + [jax.ShapeDtypeStruct((8, 128), F32)],
        input_output_aliases={i: 2 + i for i in range(ns + nl)},
        compiler_params=pltpu.CompilerParams(has_side_effects=SIDE_EFFECT),
    )(*[pltpu.with_memory_space_constraint(a, pltpu.HBM) for a in arrs], *after)
    return res[0], res[1], list(res[2:2 + ns]), list(res[2 + ns:2 + ns + nl]), res[-1]


def _copies_wait(name, send, recv, srcs, lands, copies, after):
    ns, nl, nf = len(srcs), len(lands), len(after)

    def body(*refs):
        src, land = refs[:ns], refs[ns:ns + nl]
        send_sems, recv_sems = refs[ns + nl], refs[ns + nl + 1]
        cps = copies(src, land, send_sems, recv_sems, _mesh_place())
        for cp in cps:
            cp.wait_send()
        for cp in cps:
            cp.wait_recv()

    arrs = list(srcs) + list(lands)
    res = pl.pallas_call(
        body,
        name=name,
        in_specs=[HBM_SPEC] * (ns + nl) + [SEM_SPEC, SEM_SPEC] + [ANY_SPEC] * nf,
        out_specs=[HBM_SPEC] * (ns + nl),
        out_shape=[pltpu.HBM(a.shape, a.dtype) for a in arrs],
        input_output_aliases={i: i for i in range(ns + nl)},
        compiler_params=pltpu.CompilerParams(has_side_effects=SIDE_EFFECT),
    )(*arrs, send, recv, *after)
    return list(res[:ns]), list(res[ns:])


def _gather_copies(kinds, halves):
    def copies(src, land, send, recv, place):
        x, y, c, chips = place
        s = 2 * x + y
        mine = [_shard_view(land[t], kinds[t], s, c if halves else None) for t in range(len(kinds))]
        return [
            pltpu.make_async_remote_copy(
                src_ref=mine[t], dst_ref=mine[t], send_sem=send.at[3 * t + j], recv_sem=recv.at[3 * t + j],
                device_id=(px, py, c), device_id_type=MESH)
            for t in range(len(kinds)) for j, (px, py) in enumerate(chips)
        ]

    return copies


def _gather_pass_copies(kinds):
    def copies(src, land, send, recv, place):
        x, y, c, chips = place
        views = [_shard_view(land[t], kinds[t], 2 * px + py, c) for t in range(len(kinds)) for px, py in chips]
        return [
            pltpu.make_async_remote_copy(src_ref=v, dst_ref=v, send_sem=send.at[i], recv_sem=recv.at[i],
                                         device_id=(x, y, 1 - c), device_id_type=MESH)
            for i, v in enumerate(views)
        ]

    return copies


def _gather_pass_on(name, lands, kinds):
    nt = len(lands)

    def body(*refs):
        buf = refs[nt:2 * nt]
        send, recv = refs[2 * nt:]
        x, y, c, chips = _mesh_place()
        sib = (x, y, 1 - c)
        sends = []
        for t in range(nt):
            for j, (px, py) in enumerate(chips):
                mine = _shard_view(buf[t], kinds[t], 2 * px + py, c)
                cp = pltpu.make_async_remote_copy(src_ref=mine, dst_ref=mine, send_sem=send.at[t, j],
                                                  recv_sem=recv.at[t, j], device_id=sib, device_id_type=MESH)
                cp.start()
                sends.append(cp)
        for t in range(nt):
            for j, (px, py) in enumerate(chips):
                theirs = _shard_view(buf[t], kinds[t], 2 * px + py, 1 - c)
                pltpu.make_async_remote_copy(src_ref=theirs, dst_ref=theirs, send_sem=send.at[t, j],
                                             recv_sem=recv.at[t, j], device_id=sib, device_id_type=MESH).wait_recv()
        for cp in sends:
            cp.wait_send()

    return pl.pallas_call(
        body,
        name=name,
        in_specs=[HBM_SPEC] * nt,
        out_specs=[HBM_SPEC] * nt,
        out_shape=[jax.ShapeDtypeStruct(a.shape, a.dtype) for a in lands],
        input_output_aliases={i: i for i in range(nt)},
        scratch_shapes=[pltpu.SemaphoreType.DMA((nt, 3))] * 2,
    )(*lands)


def _grad_part(ref, kind, s):
    return ref if kind == "all" else _shard_view(ref, kind, s)


def _grad_copies(kinds):
    def peers(place):
        x, y, c, chips = place
        return [(x, y, 1 - c)] + [(px, py, c) for px, py in chips]

    def copies(src, land, send, recv, place):
        x, y, c, chips = place
        me = 4 * x + 2 * y + c
        return [
            pltpu.make_async_remote_copy(
                src_ref=_grad_part(src[t], kinds[t], 2 * px + py), dst_ref=land[t].at[me],
                send_sem=send.at[GRAD_PEERS * t + k], recv_sem=recv.at[GRAD_PEERS * t + k], device_id=(px, py, pc),
                device_id_type=MESH)
            for t in range(len(kinds)) for k, (px, py, pc) in enumerate(peers(place))
        ]

    return copies


def _pass_copies(n):
    def copies(src, land, send, recv, place):
        x, y, c, chips = place
        return [
            pltpu.make_async_remote_copy(
                src_ref=land[t].at[4 * px + 2 * py + c], dst_ref=land[t].at[4 * px + 2 * py + c],
                send_sem=send.at[3 * t + j], recv_sem=recv.at[3 * t + j], device_id=(x, y, 1 - c),
                device_id_type=MESH)
            for t in range(n) for j, (px, py) in enumerate(chips)
        ]

    return copies


GRAD_PEERS = 4


def _land_shape(grad, kind):
    rows, cols = grad.shape
    if kind == "col":
        return (N_DEV, rows, cols // N_CHIPS)
    if kind == "row":
        return (N_DEV, rows // N_CHIPS, cols)
    return (N_DEV, rows, cols)


def _adamw_reduce(name, contrib, own, kind, ids, w, m, v, layer=None, prev=None):
    rows, cols = w.shape[-2:]
    t = rows
    for cand in (256, 128):
        if rows % cand == 0 and cand * cols <= 256 * 1408:
            t = cand
            break
    nb = rows // t
    c1 = 1.0 - ADAM_B1 ** ADAM_STEP
    c2 = 1.0 - ADAM_B2 ** ADAM_STEP

    n_prev = 0 if prev is None else 4

    def body(ids_ref, c_ref, own_ref, w_ref, m_ref, v_ref, *refs):
        g_ref, d_ref, nm_ref, nv_ref = refs[n_prev:]
        me = ids_ref[1]
        mine = own_ref[...].astype(F32)
        g = None
        for q in range(N_DEV):
            term = jnp.where(me == q, mine, c_ref[q].astype(F32))
            g = term if g is None else g + term
        nm = ADAM_B1 * m_ref[...] + (1.0 - ADAM_B1) * g
        nv = ADAM_B2 * v_ref[...] + (1.0 - ADAM_B2) * (g * g)
        g_ref[...] = g
        nm_ref[...] = nm
        nv_ref[...] = nv
        d_ref[...] = -ADAM_LR * ((nm / c1) / (jnp.sqrt(nv / c2) + ADAM_EPS) + ADAM_WD * w_ref[...])

    if layer is None:
        blk = pl.BlockSpec((t, cols), lambda i, ids_ref: (i, 0))
    else:
        blk = pl.BlockSpec((None, t, cols), lambda i, ids_ref: (layer, i, 0))
    if kind == "col":
        own_spec = pl.BlockSpec((t, cols), lambda i, ids_ref: (i, ids_ref[0]))
    elif kind == "row":
        own_spec = pl.BlockSpec((t, cols), lambda i, ids_ref: (ids_ref[0] * nb + i, 0))
    else:
        own_spec = pl.BlockSpec((t, cols), lambda i, ids_ref: (i, 0))
    return pl.pallas_call(
        body,
        name=name,
        grid_spec=pltpu.PrefetchScalarGridSpec(
            num_scalar_prefetch=1,
            grid=(nb,),
            in_specs=[pl.BlockSpec((N_DEV, t, cols), lambda i, ids_ref: (0, i, 0)), own_spec, blk, blk, blk]
            + [ANY_SPEC] * n_prev,
            out_specs=[blk] * 4,
        ),
        out_shape=[jax.ShapeDtypeStruct(w.shape, F32)] * 4,
        input_output_aliases={6 + i: i for i in range(n_prev)},
        compiler_params=pltpu.CompilerParams(dimension_semantics=("parallel",)),
    )(ids, contrib, own, w, m, v, *(prev or ()))


WEIGHT_NAMES = ("a_norm_g", "conv_w1", "conv_b1", "conv_dw", "conv_dw_b", "conv_ln_g", "conv_ln_b", "conv_w2",
                "conv_b2", "kv_norm_g", "w_k", "w_v", "b_norm_g", "w_q", "w_o", "ffn_norm_g", "ffn_w_gate",
                "ffn_w_up", "ffn_w_down", "final_norm_g")
GROUPS = {
    "conv2": (("conv_w2", "conv_w2", None, "row"),),
    "ffn0": (("gate", "ffn_w_gate", 0, "col"), ("up", "ffn_w_up", 0, "col"), ("down", "ffn_w_down", 0, "row")),
    "attn": (("w_k", "w_k", None, "row"), ("w_v", "w_v", None, "row"), ("w_q", "w_q", None, "row"),
             ("w_o", "w_o", None, "row")),
    "ffn1": (("gate", "ffn_w_gate", 1, "col"), ("up", "ffn_w_up", 1, "col"), ("down", "ffn_w_down", 1, "row")),
}
GROUPS["conv1"] = (("conv_w1", "conv_w1", None, "col"),)
for _layer in (0, 1):
    GROUPS[f"ffn{_layer}d"] = tuple(it for it in GROUPS[f"ffn{_layer}"] if it[0] == "down")
    GROUPS[f"ffn{_layer}gu"] = tuple(it for it in GROUPS[f"ffn{_layer}"] if it[0] != "down")
FETCH_ORDER = ("conv1", "conv2", "ffn0gu", "ffn0d", "attn", "ffn1")
HALVED = ("conv1", "conv2", "ffn0gu", "ffn0d", "attn", "ffn1")
EMIT_ORDER = ("ffn1d", "ffn1gu", "attn", "ffn0d", "ffn0gu", "conv2", "conv1", "vec")
RETIRE_AT = {"attn": ("ffn1d", "ffn1gu"), "ffn0d": ("attn",), "conv1": ("ffn0d", "ffn0gu", "conv2")}
PACKED = (("a_norm_g", 0, 1), ("conv_b1", 8, 2), ("conv_dw", 16, CONV_WIDTH), ("conv_dw_b", 48, 1),
          ("conv_ln_g", 56, 1), ("conv_ln_b", 64, 1), ("conv_b2", 72, 1))
PACK_ROWS = 80
WHOLE = (("kv_norm_g", 0, 1), ("b_norm_g", 1, 1), ("ffn_norm_g", 2, 2), ("final_norm_g", 4, 1))
WHOLE_ROWS = 8


def _pack_rows(parts, total, width):
    out, at = [], 0
    for arr, first in parts:
        if first > at:
            out.append(jnp.zeros((first - at, width), F32))
        rows8 = -(-arr.shape[0] // 8) * 8
        out.append(jnp.pad(arr, ((0, rows8 - arr.shape[0]), (0, 0))))
        at = first + rows8
    if total > at:
        out.append(jnp.zeros((total - at, width), F32))
    return jnp.concatenate(out, axis=0)


def kernel(x, a_norm_g, conv_w1, conv_b1, conv_dw, conv_dw_b, conv_ln_g, conv_ln_b, conv_w2, conv_b2, kv_norm_g, w_k, w_v, b_norm_g, w_q, w_o, ffn_norm_g, ffn_w_gate, ffn_w_up, ffn_w_down, final_norm_g, loss_target, m_a_norm_g, m_conv_w1, m_conv_b1, m_conv_dw, m_conv_dw_b, m_conv_ln_g, m_conv_ln_b, m_conv_w2, m_conv_b2, m_kv_norm_g, m_w_k, m_w_v, m_b_norm_g, m_w_q, m_w_o, m_ffn_norm_g, m_ffn_w_gate, m_ffn_w_up, m_ffn_w_down, m_final_norm_g, v_a_norm_g, v_conv_w1, v_conv_b1, v_conv_dw, v_conv_dw_b, v_conv_ln_g, v_conv_ln_b, v_conv_w2, v_conv_b2, v_kv_norm_g, v_w_k, v_w_v, v_b_norm_g, v_w_q, v_w_o, v_ffn_norm_g, v_ffn_w_gate, v_ffn_w_up, v_ffn_w_down, v_final_norm_g):
    args = locals()
    wts = {n: args[n] for n in WEIGHT_NAMES}
    mom = {n: args["m_" + n] for n in WEIGHT_NAMES}
    vel = {n: args["v_" + n] for n in WEIGHT_NAMES}
    s, d = x.shape[-2:]
    dq = d // N_CHIPS
    x2 = x.reshape(s, d)
    tgt = loss_target.reshape(s, d)

    def pack_shard(src):
        return _pack_rows([(src[n].reshape(-1, dq), first) for n, first, _ in PACKED], PACK_ROWS, dq)

    def pack_whole(src):
        return _pack_rows([(jnp.concatenate([src[n].reshape(-1, d) for n, _, _ in WHOLE], axis=0), 0)], WHOLE_ROWS, d)

    gathers = {}
    pins = []
    ids = jnp.stack([2 * lax.axis_index("x") + lax.axis_index("y"),
                     4 * lax.axis_index("x") + 2 * lax.axis_index("y") + lax.axis_index("c")]).astype(jnp.int32)

    def start_gather(grp, pins):
        kinds = [kind for _, _, _, kind in GROUPS[grp]]
        lands = []
        for key, n, layer, kind in GROUPS[grp]:
            shard = wts[n] if layer is not None else wts[n].reshape(wts[n].shape[-2:])
            full = jax.ShapeDtypeStruct(_full_shape(shard, kind), BF16)
            lands.append(_place_own(f"gather_place_{grp}_{key}", shard, kind, full, ids, layer))
        copies = _gather_copies(kinds, grp in HALVED)
        send, recv, _, lands, pin = _copies_start("gather_start_" + grp, [], lands, copies, 3 * len(lands), pins)
        gathers[grp] = (send, recv, [], lands, copies, kinds)
        return [pin]

    (packed_full,) = _gather_weights([pack_shard(wts)], ["row"])
    w = {}
    pins = start_gather(FETCH_ORDER[0], [packed_full])
    started = {}
    swapping = {}

    def prefetch(grp, after):
        if grp == "rest":
            last = [after]
            for later in FETCH_ORDER[1:]:
                last = start_gather(later, last)
            started["pin"] = last
            return ()
        send, recv, srcs, lands, copies, kinds = gathers[grp]
        _, fulls = _copies_wait("gather_wait_" + grp, send, recv, srcs, lands, copies, [after])
        copies = _gather_pass_copies(kinds)
        send, recv, _, fulls, pin = _copies_start("gather_swap_" + grp, [], fulls, copies, 3 * len(fulls), ())
        swapping[grp] = (send, recv, fulls, copies)
        return [pin]

    def fetch(grp, after):
        if grp in swapping:
            send, recv, fulls, copies = swapping[grp]
            _, fulls = _copies_wait("gather_swapped_" + grp, send, recv, [], fulls, copies, [after])
        else:
            send, recv, srcs, lands, copies, kinds = gathers[grp]
            behind = [after] + (started.get("pin", []) if grp == FETCH_ORDER[0] else [])
            _, fulls = _copies_wait("gather_wait_" + grp, send, recv, srcs, lands, copies, behind)
            if grp in HALVED:
                fulls = _gather_pass_on("gather_pass_" + grp, fulls, kinds)
        return {key: full for (key, _, _, _), full in zip(GROUPS[grp], fulls)}

    packed = packed_full.reshape(N_CHIPS, PACK_ROWS, dq)
    for n, first, rows in PACKED:
        part = packed[:, first:first + rows, :]
        if n == "conv_dw":
            w[n] = part.transpose(1, 0, 2).reshape(rows, d)
        else:
            w[n] = part.reshape(1, N_CHIPS * rows * dq)
    for n, _, rows in WHOLE:
        w[n] = wts[n].reshape(rows, d)

    exchanges = {}
    passing = {}
    own_grads = {}

    def retire(tag, after):
        send, recv, srcs, lands, copies, kinds = exchanges[tag]
        srcs, lands = _copies_wait("grads_wait_" + tag, send, recv, srcs, lands, copies, after)
        own_grads[tag] = list(zip(srcs, kinds))
        copies = _pass_copies(len(lands))
        send, recv, _, lands, pin = _copies_start("grads_pass_" + tag, [], lands, copies, 3 * len(lands), ())
        passing[tag] = (send, recv, lands, copies)
        return pin

    def emit_grads(tag, grads, kinds):
        pins = [retire(old, [grads[0]]) for old in RETIRE_AT.get(tag, ())]
        copies = _grad_copies(kinds)
        lands = [lax.empty(_land_shape(gr, kd), gr.dtype) for gr, kd in zip(grads, kinds)]
        send, recv, srcs, lands, pin = _copies_start("grads_start_" + tag, grads, lands, copies,
                                                     GRAD_PEERS * len(grads), pins)
        exchanges[tag] = (send, recv, srcs, lands, copies, kinds)
        return [pin]

    def emit(grp, grads):
        return emit_grads(grp, [grads[key] for key, _, _, _ in GROUPS[grp]], [kind for _, _, _, kind in GROUPS[grp]])

    loss_cols, dx, g = _local_step(x2, tgt, w, fetch, emit, prefetch, after=pins)
    loss = lax.psum(jnp.sum(loss_cols), ("x", "y", "c"))

    gp = []
    for n, first, rows in PACKED:
        if n == "conv_dw":
            part = g[n].reshape(rows, N_CHIPS, dq).transpose(1, 0, 2)
        else:
            part = g[n].reshape(N_CHIPS, rows, dq)
        gp.append((part, first))
    g_packed = jnp.concatenate(
        [_pack_rows([(p[ci], first) for p, first in gp], PACK_ROWS, dq) for ci in range(N_CHIPS)], axis=0)
    emit_grads("vec", [g_packed, pack_whole(g)], ["row", "all"])
    late = [tag for tag in EMIT_ORDER if tag not in passing]
    contribs = {}

    def arrive(tag, after):
        send, recv, lands, copies = passing[tag]
        _, arrived = _copies_wait("grads_passed_" + tag, send, recv, [], lands, copies, after)
        contribs[tag] = [(c, own, kind) for c, (own, kind) in zip(arrived, own_grads[tag])]

    for tag in EMIT_ORDER:
        if tag not in late:
            arrive(tag, [dx])
    res = {}

    def adamw(n, contrib, layer=None, prev=None):
        arrived, own, kind = contrib
        if layer is None:
            shape = wts[n].shape
            r2 = shape[-2:]
            outs = _adamw_reduce("adamw_" + n, arrived, own, kind, ids, wts[n].reshape(r2), mom[n].reshape(r2),
                                 vel[n].reshape(r2))
            return [o.reshape(shape) for o in outs]
        return _adamw_reduce(f"adamw_{n}_{layer}", arrived, own, kind, ids, wts[n], mom[n], vel[n], layer, prev)

    for grp in ("attn", "conv2"):
        for (key, n, _, _), contrib in zip(GROUPS[grp], contribs[grp]):
            res[n] = adamw(n, contrib)
    for tag in late:
        retire(tag, [res["conv_w2"][0]])
    for part in ("d", "gu"):
        for (key, n, _, _), c0, c1 in zip(GROUPS["ffn0" + part], contribs["ffn0" + part], contribs["ffn1" + part]):
            res[n] = adamw(n, c1, 1, adamw(n, c0, 0))
    for tag in late:
        arrive(tag, [res["ffn_w_up"][0]])
    for (key, n, _, _), contrib in zip(GROUPS["conv1"], contribs["conv1"]):
        res[n] = adamw(n, contrib)
    outs = _adamw_reduce("adamw_packed", *contribs["vec"][0], ids, pack_shard(wts), pack_shard(mom), pack_shard(vel))
    for n, first, rows in PACKED:
        res[n] = [o[first:first + rows].reshape(wts[n].shape) for o in outs]
    outs = _adamw_reduce("adamw_whole", *contribs["vec"][1], ids, pack_whole(wts), pack_whole(mom), pack_whole(vel))
    for n, first, rows in WHOLE:
        res[n] = [o[first:first + rows].reshape(wts[n].shape) for o in outs]

    out = [loss, dx.reshape(x.shape)]
    for which in range(4):
        out += [res[n][which] for n in WEIGHT_NAMES]
    return tuple(out)
```

```python
import functools

import jax
import jax.numpy as jnp
from jax import lax
from jax.experimental import pallas as pl
from jax.experimental.pallas import tpu as pltpu

F32 = jnp.float32
BF16 = jnp.bfloat16

HEAD_DIM = 128
BRANCHES = ((128, 1), (512, 4), (2048, 16))
CONV_WIDTH = 31
CONV_HALO = 32
RMS_EPS = 1e-6
LN_EPS = 1e-5
ADAM_LR = 0.001
ADAM_B1 = 0.9
ADAM_B2 = 0.999
ADAM_EPS = 1e-08
ADAM_WD = 0.01
ADAM_STEP = 10
N_CHIPS = 4
N_DEV = 8
MESH = pl.DeviceIdType.MESH


def _sigmoid(x):
    return 0.5 * jnp.tanh(0.5 * x) + 0.5


def _row_tile(rows, want):
    t = min(rows, want)
    assert rows % t == 0, (rows, want)
    return t


_DOT_DIMS = {"nn": ((1,), (0,)), "nt": ((1,), (1,)), "tn": ((0,), (0,))}


ANY_SPEC = pl.BlockSpec(memory_space=pl.ANY)
WIDE_BN = 1024


def _mm(name, mode, a, bs, epilogue, outs, *, m, n, k, extras=(), bm=1024, bn=512, bk=None, after=()):
    bm, bn = min(bm, m), min(bn, n)
    bk = k if bk is None else min(bk, k)
    assert m % bm == 0 and n % bn == 0 and k % bk == 0, (name, m, n, k, bm, bn, bk)
    nk = k // bk
    a_list = list(a) if isinstance(a, (list, tuple)) else [a]
    na, nb, ne, no = len(a_list), len(bs), len(extras), len(outs)
    assert na in (1, nb)

    if mode == "tn":
        a_spec = pl.BlockSpec((bk, bm), lambda i, j, kk: (kk, i))
    else:
        a_spec = pl.BlockSpec((bm, bk), lambda i, j, kk: (i, kk))

    def b_spec(lead, off):
        if mode == "nt":
            blk, idx = (bn, bk), (lambda i, j, kk: (j + off, kk))
        else:
            blk, idx = (bk, bn), (lambda i, j, kk: (kk, j + off))
        if lead is None:
            return pl.BlockSpec(blk, idx)
        return pl.BlockSpec((None,) + blk, lambda i, j, kk: (lead,) + idx(i, j, kk))

    def e_spec(kind, off):
        if kind == "mn":
            return pl.BlockSpec((bm, bn), lambda i, j, kk: (i, j + off))
        return pl.BlockSpec((1, bn), lambda i, j, kk: (0, j + off))

    nf = len(after)
    in_specs = [a_spec] * na + [b_spec(l, o) for _, l, o in bs] + [e_spec(kd, o) for _, kd, o in extras]
    in_specs += [ANY_SPEC] * nf
    out_specs = [pl.BlockSpec((bm, bn), lambda i, j, kk: (i, j)) for _ in outs]
    out_shape = [jax.ShapeDtypeStruct((m, n), dt) for (dt,) in outs]
    dims = (_DOT_DIMS[mode], ((), ()))

    def body(*refs):
        a_refs = refs[:na]
        b_refs = refs[na:na + nb]
        e_refs = refs[na + nb:na + nb + ne]
        o_refs = refs[na + nb + ne + nf:na + nb + ne + nf + no]
        acc_refs = refs[na + nb + ne + nf + no:]
        avs = [a_ref[...].astype(BF16) for a_ref in a_refs]
        prods = [lax.dot_general(avs[bi % na], b_ref[...].astype(BF16), dims, preferred_element_type=F32)
                 for bi, b_ref in enumerate(b_refs)]

        def finish(accs):
            res = epilogue(accs, [e_ref[...] for e_ref in e_refs])
            for o_ref, r in zip(o_refs, res):
                o_ref[...] = r.astype(o_ref.dtype)

        if nk == 1:
            finish(prods)
        else:
            kk = pl.program_id(2)

            @pl.when(kk == 0)
            def _():
                for acc_ref, p in zip(acc_refs, prods):
                    acc_ref[...] = p

            @pl.when(kk > 0)
            def _():
                for acc_ref, p in zip(acc_refs, prods):
                    acc_ref[...] += p

            @pl.when(kk == nk - 1)
            def _():
                finish([acc_ref[...] for acc_ref in acc_refs])

    scratch = [] if nk == 1 else [pltpu.VMEM((bm, bn), F32) for _ in bs]
    res = pl.pallas_call(
        body,
        name=name,
        grid=(m // bm, n // bn, nk),
        in_specs=in_specs,
        out_specs=out_specs,
        out_shape=out_shape,
        scratch_shapes=scratch,
        compiler_params=pltpu.CompilerParams(dimension_semantics=("parallel", "parallel", "arbitrary")),
    )(*a_list, *[b for b, _, _ in bs], *[e for e, _, _ in extras], *after)
    return res


def _rms_fwd(name, x, gains, after=()):
    s, d = x.shape
    ng = gains.shape[0]
    t = _row_tile(s, 256)
    nf = len(after)

    def body(x_ref, g_ref, *refs):
        o_refs = refs[nf:]
        xv = x_ref[...]
        r = lax.rsqrt(jnp.mean(xv * xv, axis=-1, keepdims=True) + RMS_EPS)
        xh = xv * r
        for gi, o_ref in enumerate(o_refs):
            o_ref[...] = (xh * g_ref[gi:gi + 1, :]).astype(o_ref.dtype)

    return pl.pallas_call(
        body,
        name=name,
        grid=(s // t,),
        in_specs=[pl.BlockSpec((t, d), lambda i: (i, 0)), pl.BlockSpec((ng, d), lambda i: (0, 0))] + [ANY_SPEC] * nf,
        out_specs=[pl.BlockSpec((t, d), lambda i: (i, 0)) for _ in range(ng)],
        out_shape=[jax.ShapeDtypeStruct((s, d), BF16) for _ in range(ng)],
        compiler_params=pltpu.CompilerParams(dimension_semantics=("parallel",)),
    )(x, gains, *after)


def _rms_bwd(name, x, gains, dns, dres):
    s, d = x.shape
    ng = gains.shape[0]
    t = _row_tile(s, 256)

    def body(x_ref, g_ref, dres_ref, *refs):
        dn_refs = refs[:ng]
        dx_ref, dxb_ref, dg_ref, cs_ref = refs[ng:]
        i = pl.program_id(0)
        xv = x_ref[...]
        r = lax.rsqrt(jnp.mean(xv * xv, axis=-1, keepdims=True) + RMS_EPS)
        xh = xv * r
        dx = dres_ref[...]
        dgs = []
        for gi in range(ng):
            dn = dn_refs[gi][...].astype(F32)
            dxh = dn * g_ref[gi:gi + 1, :]
            dgs.append(jnp.sum(dn * xh, axis=0, keepdims=True))
            dx = dx + r * (dxh - xh * jnp.mean(dxh * xh, axis=-1, keepdims=True))
        dx_ref[...] = dx
        dxb_ref[...] = dx.astype(BF16)
        dg = jnp.concatenate(dgs, axis=0) if ng > 1 else dgs[0]
        cs = jnp.sum(dx, axis=0, keepdims=True)

        @pl.when(i == 0)
        def _():
            dg_ref[...] = dg
            cs_ref[...] = cs

        @pl.when(i > 0)
        def _():
            dg_ref[...] += dg
            cs_ref[...] += cs

    row = pl.BlockSpec((t, d), lambda i: (i, 0))
    return pl.pallas_call(
        body,
        name=name,
        grid=(s // t,),
        in_specs=[row, pl.BlockSpec((ng, d), lambda i: (0, 0)), row] + [row] * ng,
        out_specs=[row, row, pl.BlockSpec((ng, d), lambda i: (0, 0)), pl.BlockSpec((1, d), lambda i: (0, 0))],
        out_shape=[
            jax.ShapeDtypeStruct((s, d), F32),
            jax.ShapeDtypeStruct((s, d), BF16),
            jax.ShapeDtypeStruct((ng, d), F32),
            jax.ShapeDtypeStruct((1, d), F32),
        ],
        compiler_params=pltpu.CompilerParams(dimension_semantics=("arbitrary",)),
    )(x, gains, dres, *dns)


def _final_loss(name, h, gain, target):
    s, d = h.shape
    t = _row_tile(s, 256)

    def body(h_ref, g_ref, t_ref, dh_ref, dhb_ref, dg_ref, ls_ref):
        i = pl.program_id(0)
        xv = h_ref[...]
        g = g_ref[...]
        r = lax.rsqrt(jnp.mean(xv * xv, axis=-1, keepdims=True) + RMS_EPS)
        xh = xv * r
        err = xh * g - t_ref[...]
        ls = jnp.sum(err * err, axis=0, keepdims=True) * (0.5 / d)
        dy = err * (1.0 / d)
        dxh = dy * g
        dg = jnp.sum(dy * xh, axis=0, keepdims=True)
        dx = r * (dxh - xh * jnp.mean(dxh * xh, axis=-1, keepdims=True))
        dh_ref[...] = dx
        dhb_ref[...] = dx.astype(BF16)

        @pl.when(i == 0)
        def _():
            dg_ref[...] = dg
            ls_ref[...] = ls

        @pl.when(i > 0)
        def _():
            dg_ref[...] += dg
            ls_ref[...] += ls

    row = pl.BlockSpec((t, d), lambda i: (i, 0))
    vec = pl.BlockSpec((1, d), lambda i: (0, 0))
    return pl.pallas_call(
        body,
        name=name,
        grid=(s // t,),
        in_specs=[row, vec, row],
        out_specs=[row, row, vec, vec],
        out_shape=[
            jax.ShapeDtypeStruct((s, d), F32),
            jax.ShapeDtypeStruct((s, d), BF16),
            jax.ShapeDtypeStruct((1, d), F32),
            jax.ShapeDtypeStruct((1, d), F32),
        ],
        compiler_params=pltpu.CompilerParams(dimension_semantics=("arbitrary",)),
    )(h, gain, target)


SUBLANES = 8
CONV_LANES = 512
CONV_ROWS = 32
NORM_ROWS = 64


def _conv_tiles(s):
    t = _row_tile(s, 128)
    assert t % CONV_HALO == 0 and t % CONV_ROWS == 0
    return t, t // CONV_HALO


def _shifted_copies(dst, src, lanes, rows):
    for m in range(SUBLANES):
        n = rows if m == 0 else rows - SUBLANES
        dst[m, :n, :] = src[m:m + n, lanes]


def _shifted(copies, offset, n):
    m = offset % SUBLANES
    return copies[m, offset - m:offset - m + n, :]


def _dwconv_fwd(name, u, dw, dw_b, ln_g, ln_b):
    s, d = u.shape
    t, hb = _conv_tiles(s)
    w = dw.shape[0]
    lo = CONV_HALO - (w - 1)

    lw = min(CONV_LANES, d)

    def body(cur_ref, prev_ref, dw_ref, dwb_ref, lg_ref, lb_ref, c_ref, sw_ref, cat_ref, sh_ref):
        i = pl.program_id(0)
        cat_ref[CONV_HALO:, :] = cur_ref[...]

        @pl.when(i == 0)
        def _():
            cat_ref[:CONV_HALO, :] = jnp.zeros((CONV_HALO, d), F32)

        @pl.when(i > 0)
        def _():
            cat_ref[:CONV_HALO, :] = prev_ref[...]

        for lc in range(d // lw):
            lanes = slice(lc * lw, (lc + 1) * lw)
            _shifted_copies(sh_ref, cat_ref, lanes, t + CONV_HALO)
            for rc in range(t // CONV_ROWS):
                acc = jnp.broadcast_to(dwb_ref[:, lanes], (CONV_ROWS, lw))
                for kk in range(w):
                    acc = acc + dw_ref[kk:kk + 1, lanes] * _shifted(sh_ref, lo + kk + rc * CONV_ROWS, CONV_ROWS)
                c_ref[rc * CONV_ROWS:(rc + 1) * CONV_ROWS, lanes] = acc

        def norm_rows(ri, carry):
            rows = pl.ds(pl.multiple_of(ri * NORM_ROWS, NORM_ROWS), NORM_ROWS)
            cv = c_ref[rows, :]
            cc = cv - jnp.mean(cv, axis=-1, keepdims=True)
            var = jnp.mean(cc * cc, axis=-1, keepdims=True)
            ln = cc * lax.rsqrt(var + LN_EPS) * lg_ref[...] + lb_ref[...]
            sw_ref[rows, :] = (ln * _sigmoid(ln)).astype(BF16)
            return carry

        lax.fori_loop(0, t // NORM_ROWS, norm_rows, 0)

    row = pl.BlockSpec((t, d), lambda i: (i, 0))
    prev = pl.BlockSpec((CONV_HALO, d), lambda i: (jnp.maximum(i * hb - 1, 0), 0))
    vec = pl.BlockSpec((1, d), lambda i: (0, 0))
    return pl.pallas_call(
        body,
        name=name,
        grid=(s // t,),
        in_specs=[row, prev, pl.BlockSpec((w, d), lambda i: (0, 0)), vec, vec, vec],
        out_specs=[row, row],
        out_shape=[jax.ShapeDtypeStruct((s, d), F32), jax.ShapeDtypeStruct((s, d), BF16)],
        scratch_shapes=[pltpu.VMEM((CONV_HALO + t, d), F32), pltpu.VMEM((SUBLANES, CONV_HALO + t, lw), F32)],
        compiler_params=pltpu.CompilerParams(dimension_semantics=("parallel",)),
    )(u, u, dw, dw_b, ln_g, ln_b)


def _conv_ln_bwd(name, c, dsw, ln_g, ln_b, after=()):
    s, d = c.shape
    t = _row_tile(s, 256)
    nf = len(after)

    def body(c_ref, dsw_ref, lg_ref, lb_ref, *refs):
        dc_ref, sums_ref = refs[nf:]
        i = pl.program_id(0)
        cv = c_ref[...]
        g = lg_ref[...]
        mu = jnp.mean(cv, axis=-1, keepdims=True)
        cc = cv - mu
        rstd = lax.rsqrt(jnp.mean(cc * cc, axis=-1, keepdims=True) + LN_EPS)
        ch = cc * rstd
        ln = ch * g + lb_ref[...]
        sg = _sigmoid(ln)
        dln = dsw_ref[...] * (sg * (1.0 + ln * (1.0 - sg)))
        dch = dln * g
        dc = rstd * (dch - jnp.mean(dch, axis=-1, keepdims=True) - ch * jnp.mean(dch * ch, axis=-1, keepdims=True))
        dc_ref[...] = dc
        sums = jnp.concatenate(
            [
                jnp.sum(dln * ch, axis=0, keepdims=True),
                jnp.sum(dln, axis=0, keepdims=True),
                jnp.sum(dc, axis=0, keepdims=True),
                jnp.zeros((1, d), F32),
            ],
            axis=0,
        )

        @pl.when(i == 0)
        def _():
            sums_ref[...] = sums

        @pl.when(i > 0)
        def _():
            sums_ref[...] += sums

    row = pl.BlockSpec((t, d), lambda i: (i, 0))
    vec = pl.BlockSpec((1, d), lambda i: (0, 0))
    return pl.pallas_call(
        body,
        name=name,
        grid=(s // t,),
        in_specs=[row, row, vec, vec] + [ANY_SPEC] * nf,
        out_specs=[row, pl.BlockSpec((4, d), lambda i: (0, 0))],
        out_shape=[jax.ShapeDtypeStruct((s, d), F32), jax.ShapeDtypeStruct((4, d), F32)],
        compiler_params=pltpu.CompilerParams(dimension_semantics=("arbitrary",)),
    )(c, dsw, ln_g, ln_b, *after)


def _conv_dw_bwd(name, dc, u, a, gt, dw):
    s, d = dc.shape
    t, hb = _conv_tiles(s)
    w = dw.shape[0]
    lo = CONV_HALO - (w - 1)
    nt = s // t
    lw = min(CONV_LANES, d)

    def body(dc_ref, dcn_ref, u_ref, up_ref, a_ref, gt_ref, dw_ref, dpre_ref, ddw_ref, db_ref, dcat_ref, ucat_ref,
             dsh_ref, ush_ref, ddw_acc, db_acc):
        i = pl.program_id(0)
        dcat_ref[:t, :] = dc_ref[...]
        ucat_ref[CONV_HALO:, :] = u_ref[...]

        @pl.when(i == nt - 1)
        def _():
            dcat_ref[t:, :] = jnp.zeros((CONV_HALO, d), F32)

        @pl.when(i < nt - 1)
        def _():
            dcat_ref[t:, :] = dcn_ref[...]

        @pl.when(i == 0)
        def _():
            ucat_ref[:CONV_HALO, :] = jnp.zeros((CONV_HALO, d), F32)

        @pl.when(i > 0)
        def _():
            ucat_ref[:CONV_HALO, :] = up_ref[...]

        @pl.when(i == 0)
        def _():
            ddw_acc[...] = jnp.zeros(ddw_acc.shape, F32)
            db_acc[...] = jnp.zeros(db_acc.shape, F32)

        def fold(p):
            return functools.reduce(jnp.add, [p[j:j + SUBLANES] for j in range(0, CONV_ROWS, SUBLANES)])

        for lc in range(d // lw):
            lanes = slice(lc * lw, (lc + 1) * lw)
            gate_lanes = slice(d + lc * lw, d + (lc + 1) * lw)
            _shifted_copies(dsh_ref, dcat_ref, lanes, t + CONV_HALO)
            _shifted_copies(ush_ref, ucat_ref, lanes, t + CONV_HALO)
            for rc in range(t // CONV_ROWS):
                r0 = rc * CONV_ROWS
                rows = slice(r0, r0 + CONV_ROWS)
                dcv = dc_ref[rows, lanes]
                du = jnp.zeros((CONV_ROWS, lw), F32)
                for kk in range(w):
                    du = du + dw_ref[kk:kk + 1, lanes] * _shifted(dsh_ref, w - 1 - kk + r0, CONV_ROWS)
                    ddw_acc[kk, :, lanes] += fold(dcv * _shifted(ush_ref, lo + kk + r0, CONV_ROWS))
                av = a_ref[rows, lanes].astype(F32)
                sg = _sigmoid(gt_ref[rows, lanes].astype(F32))
                da = du * sg
                dgt = du * av * sg * (1.0 - sg)
                dpre_ref[rows, lanes] = da.astype(BF16)
                dpre_ref[rows, gate_lanes] = dgt.astype(BF16)
                db_acc[:, lanes] += fold(da)
                db_acc[:, gate_lanes] += fold(dgt)

        @pl.when(i == nt - 1)
        def _():
            ddw_ref[...] = jnp.sum(ddw_acc[...], axis=1)
            db_ref[...] = jnp.sum(db_acc[...], axis=0, keepdims=True)

    row = pl.BlockSpec((t, d), lambda i: (i, 0))
    nxt = pl.BlockSpec((CONV_HALO, d), lambda i: (jnp.minimum((i + 1) * hb, s // CONV_HALO - 1), 0))
    prev = pl.BlockSpec((CONV_HALO, d), lambda i: (jnp.maximum(i * hb - 1, 0), 0))
    return pl.pallas_call(
        body,
        name=name,
        grid=(nt,),
        in_specs=[row, nxt, row, prev, row, row, pl.BlockSpec((w, d), lambda i: (0, 0))],
        out_specs=[
            pl.BlockSpec((t, 2 * d), lambda i: (i, 0)),
            pl.BlockSpec((w, d), lambda i: (0, 0)),
            pl.BlockSpec((1, 2 * d), lambda i: (0, 0)),
        ],
        out_shape=[
            jax.ShapeDtypeStruct((s, 2 * d), BF16),
            jax.ShapeDtypeStruct((w, d), F32),
            jax.ShapeDtypeStruct((1, 2 * d), F32),
        ],
        scratch_shapes=[pltpu.VMEM((t + CONV_HALO, d), F32), pltpu.VMEM((CONV_HALO + t, d), F32)]
        + [pltpu.VMEM((SUBLANES, CONV_HALO + t, lw), F32)] * 2
        + [pltpu.VMEM((w, SUBLANES, d), F32), pltpu.VMEM((SUBLANES, 2 * d), F32)],
        compiler_params=pltpu.CompilerParams(dimension_semantics=("arbitrary",)),
    )(dc, dc, u, u, a, gt, dw)


def _alibi_slopes(n_heads):
    h = jnp.arange(1, n_heads + 1, dtype=F32)
    return jnp.exp2(-8.0 * h / n_heads)


def _band_masks(bq):
    qi = lax.broadcasted_iota(jnp.int32, (bq, bq), 0)
    kj = lax.broadcasted_iota(jnp.int32, (bq, bq), 1)
    return qi - kj, qi - kj + bq


ATTN_GROUP = 32
ATTN_GROUP_BWD = 32


def _attn_fwd(name, q, k, v, slopes, bq):
    s, dm = q.shape
    nh = dm // HEAD_DIM
    nt = s // bq
    nbr = len(BRANCHES)
    scale = HEAD_DIM ** -0.5
    nt_dims = (((1,), (1,)), ((), ()))

    def body(sl_ref, q_ref, k_ref, v_ref, o_ref, ob_ref, l_ref, tmp, qr, kr, va, orm, lrm, onat, lnat, sbuf, mbuf):
        slope = sl_ref[pl.program_id(0)]
        jc, jp = _band_masks(bq)
        va[:, HEAD_DIM:] = jnp.ones((s, HEAD_DIM), BF16)
        for bi, (win, dil) in enumerate(BRANCHES):
            ll = s // dil
            nblk = ll // bq
            if dil == 1:
                sq, sk = q_ref, k_ref
                va[:, :HEAD_DIM] = v_ref[...]
                d_o, d_l = onat.at[bi], lnat.at[bi]
            else:
                for src, dst, wide in ((q_ref, qr, False), (k_ref, kr, False), (v_ref, va, True)):
                    tmp[...] = src[...].astype(F32)
                    for r in range(dil):
                        part = tmp[pl.ds(r, ll, stride=dil), :].astype(BF16)
                        if wide:
                            dst[r * ll:(r + 1) * ll, :HEAD_DIM] = part
                        else:
                            dst[r * ll:(r + 1) * ll, :] = part
                sq, sk = qr, kr
                d_o, d_l = orm, lrm
            bias_c = jnp.where(jc >= 0, jc.astype(F32) * (slope * dil), 1e30)
            bias_p = jnp.where(jp <= bq, jp.astype(F32) * (slope * dil), 1e30)

            def group(gi, carry):
                rows = []
                for g in range(ATTN_GROUP):
                    ti = gi * ATTN_GROUP + g
                    row = pl.ds(pl.multiple_of(ti * bq, bq), bq)
                    prow = pl.ds(pl.multiple_of(jnp.maximum(ti - 1, 0) * bq, bq), bq)
                    rows.append((row, prow))
                    qh = sq[row, :]
                    sc = lax.dot_general(qh, sk[row, :], nt_dims, preferred_element_type=F32) * scale - bias_c
                    sp = lax.dot_general(qh, sk[prow, :], nt_dims, preferred_element_type=F32) * scale - bias_p
                    sp = jnp.where(lax.rem(ti, nblk) > 0, sp, -1e30)
                    sbuf[g, :, :bq] = sc
                    sbuf[g, :, bq:] = sp
                    mbuf[g] = jnp.maximum(jnp.max(sc, axis=-1, keepdims=True), jnp.max(sp, axis=-1, keepdims=True))
                for g, (row, prow) in enumerate(rows):
                    mx = mbuf[g]
                    p = jnp.exp(sbuf[g] - mx).astype(BF16)
                    ov = jnp.dot(p[:, :bq], va[row, :], preferred_element_type=F32)
                    ov = ov + jnp.dot(p[:, bq:], va[prow, :], preferred_element_type=F32)
                    den = ov[:, HEAD_DIM:]
                    d_o[row, :] = ov[:, :HEAD_DIM] / den
                    d_l[row, :] = mx + jnp.log(den)
                return carry

            lax.fori_loop(0, nt // ATTN_GROUP, group, 0)
            if dil > 1:
                for r in range(dil):
                    onat[bi, pl.ds(r, ll, stride=dil), :] = orm[r * ll:(r + 1) * ll, :]
                    lnat[bi, pl.ds(r, ll, stride=dil), :] = lrm[r * ll:(r + 1) * ll, :]

        def merge(ti, carry):
            rows = pl.ds(pl.multiple_of(ti * bq, bq), bq)
            ls = [lnat[bi, rows, :] for bi in range(nbr)]
            mx = functools.reduce(jnp.maximum, ls)
            es = [jnp.exp(l - mx) for l in ls]
            tot = functools.reduce(jnp.add, es)
            inv = 1.0 / tot
            o = functools.reduce(jnp.add, [e * inv * onat[bi, rows, :] for bi, e in enumerate(es)])
            o_ref[rows, :] = o
            ob_ref[rows, :] = o.astype(BF16)
            l_ref[rows, :] = mx + jnp.log(tot)
            return carry

        lax.fori_loop(0, nt, merge, 0)

    head = pl.BlockSpec((s, HEAD_DIM), lambda h: (0, h))
    return pl.pallas_call(
        body,
        name=name,
        grid=(nh,),
        in_specs=[pl.BlockSpec(memory_space=pltpu.SMEM), head, head, head],
        out_specs=[head, head, head],
        out_shape=[jax.ShapeDtypeStruct((s, dm), F32), jax.ShapeDtypeStruct((s, dm), BF16),
                   jax.ShapeDtypeStruct((s, dm), F32)],
        scratch_shapes=[pltpu.VMEM((s, HEAD_DIM), F32)] + [pltpu.VMEM((s, HEAD_DIM), BF16)] * 2
        + [pltpu.VMEM((s, 2 * HEAD_DIM), BF16)] + [pltpu.VMEM((s, HEAD_DIM), F32)] * 2
        + [pltpu.VMEM((nbr, s, HEAD_DIM), F32)] * 2
        + [pltpu.VMEM((ATTN_GROUP, bq, 2 * bq), F32), pltpu.VMEM((ATTN_GROUP, bq, 1), F32)],
        compiler_params=pltpu.CompilerParams(dimension_semantics=("parallel",)),
    )(slopes, q, k, v)


def _attn_bwd(name, q, k, v, o, lse, do, slopes, bq):
    s, dm = q.shape
    nh = dm // HEAD_DIM
    nt = s // bq
    scale = HEAD_DIM ** -0.5
    nt_dims = (((1,), (1,)), ((), ()))
    tn_dims = (((0,), (0,)), ((), ()))

    def body(sl_ref, q_ref, k_ref, v_ref, o_ref, l_ref, do_ref, dq_ref, dk_ref, dv_ref,
             tmp, qr, kr, vr, dor, rown, rowr, dqr, dkr, dvr, aq, ak, av, pbuf, dsbuf):
        slope = sl_ref[pl.program_id(0)]
        jc, jp = _band_masks(bq)
        lane = lax.broadcasted_iota(jnp.int32, (bq, HEAD_DIM), 1)

        def row_terms(ti, carry):
            rows = pl.ds(pl.multiple_of(ti * bq, bq), bq)
            dl = jnp.sum(do_ref[rows, :].astype(F32) * o_ref[rows, :], axis=-1, keepdims=True)
            rown[rows, :] = jnp.where(lane == 0, l_ref[rows, :], dl)
            return carry

        lax.fori_loop(0, nt, row_terms, 0)

        for bi, (win, dil) in enumerate(BRANCHES):
            ll = s // dil
            nblk = ll // bq
            if dil == 1:
                sq, sk, sv, sdo, srow = q_ref, k_ref, v_ref, do_ref, rown
                gq, gk, gv = aq, ak, av
            else:
                for src, dst in ((q_ref, qr), (k_ref, kr), (v_ref, vr), (do_ref, dor)):
                    tmp[...] = src[...].astype(F32)
                    for r in range(dil):
                        dst[r * ll:(r + 1) * ll, :] = tmp[pl.ds(r, ll, stride=dil), :].astype(BF16)
                for r in range(dil):
                    rowr[r * ll:(r + 1) * ll, :] = rown[pl.ds(r, ll, stride=dil), :]
                sq, sk, sv, sdo, srow = qr, kr, vr, dor, rowr
                gq, gk, gv = dqr, dkr, dvr
            bias_c = jnp.where(jc >= 0, jc.astype(F32) * (slope * dil), 1e30)
            bias_p = jnp.where(jp <= bq, jp.astype(F32) * (slope * dil), 1e30)

            def group(gi, carry):
                rows = []
                for g in range(ATTN_GROUP_BWD):
                    ti = gi * ATTN_GROUP_BWD + g
                    row = pl.ds(pl.multiple_of(ti * bq, bq), bq)
                    prow = pl.ds(pl.multiple_of(jnp.maximum(ti - 1, 0) * bq, bq), bq)
                    rows.append((row, prow))
                    has_prev = lax.rem(ti, nblk) > 0
                    qh, doh = sq[row, :], sdo[row, :]
                    terms = srow[row, :]
                    lc = terms[:, 0:1]
                    dl = terms[:, 1:2]
                    for half, kv_rows, bias in ((0, row, bias_c), (1, prow, bias_p)):
                        sc = lax.dot_general(qh, sk[kv_rows, :], nt_dims, preferred_element_type=F32) * scale - bias
                        if half:
                            sc = jnp.where(has_prev, sc, -1e30)
                        p = jnp.exp(sc - lc)
                        dp = lax.dot_general(doh, sv[kv_rows, :], nt_dims, preferred_element_type=F32)
                        pbuf[g, :, half * bq:(half + 1) * bq] = p.astype(BF16)
                        dsbuf[g, :, half * bq:(half + 1) * bq] = (p * (dp - dl) * scale).astype(BF16)
                carry_k = carry_v = None
                for g, (row, prow) in enumerate(rows):
                    qh, doh = sq[row, :], sdo[row, :]
                    ds_c, ds_p = dsbuf[g, :, :bq], dsbuf[g, :, bq:]
                    p_c, p_p = pbuf[g, :, :bq], pbuf[g, :, bq:]
                    dq = jnp.dot(ds_c, sk[row, :], preferred_element_type=F32)
                    dq = dq + jnp.dot(ds_p, sk[prow, :], preferred_element_type=F32)
                    gq[row, :] = dq
                    dk_p = lax.dot_general(ds_p, qh, tn_dims, preferred_element_type=F32)
                    dv_p = lax.dot_general(p_p, doh, tn_dims, preferred_element_type=F32)
                    if g == 0:
                        @pl.when(gi > 0)
                        def _():
                            gk[prow, :] += dk_p
                            gv[prow, :] += dv_p
                    else:
                        gk[rows[g - 1][0], :] = carry_k + dk_p
                        gv[rows[g - 1][0], :] = carry_v + dv_p
                    carry_k = lax.dot_general(ds_c, qh, tn_dims, preferred_element_type=F32)
                    carry_v = lax.dot_general(p_c, doh, tn_dims, preferred_element_type=F32)
                gk[rows[-1][0], :] = carry_k
                gv[rows[-1][0], :] = carry_v
                return carry

            lax.fori_loop(0, nt // ATTN_GROUP_BWD, group, 0)
            if dil > 1:
                for acc, rm in ((aq, dqr), (ak, dkr), (av, dvr)):
                    for r in range(dil):
                        acc[pl.ds(r, ll, stride=dil), :] += rm[r * ll:(r + 1) * ll, :]

        dq_ref[...] = aq[...].astype(BF16)
        dk_ref[...] = ak[...].astype(BF16)
        dv_ref[...] = av[...].astype(BF16)

    head = pl.BlockSpec((s, HEAD_DIM), lambda h: (0, h))
    f32buf = pltpu.VMEM((s, HEAD_DIM), F32)
    b16buf = pltpu.VMEM((s, HEAD_DIM), BF16)
    return pl.pallas_call(
        body,
        name=name,
        grid=(nh,),
        in_specs=[pl.BlockSpec(memory_space=pltpu.SMEM)]
        + [pl.BlockSpec((s, HEAD_DIM), lambda h: (0, h), pipeline_mode=pl.Buffered(1))] * 6,
        out_specs=[head] * 3,
        out_shape=[jax.ShapeDtypeStruct((s, dm), BF16)] * 3,
        scratch_shapes=[f32buf] + [b16buf] * 4 + [f32buf] * 8
        + [pltpu.VMEM((ATTN_GROUP_BWD, bq, 2 * bq), BF16)] * 2,
        compiler_params=pltpu.CompilerParams(dimension_semantics=("parallel",)),
    )(slopes, q, k, v, o, lse, do)


def _ep_id(accs, ex):
    return [accs[0]]


def _ep_all(accs, ex):
    return list(accs)


def _ep_sum(accs, ex):
    return [accs[0] + accs[1]]


def _ep_add(accs, ex):
    return [accs[0] + ex[0].astype(F32)]


def _ep_bias_res(accs, ex):
    return [accs[0] + ex[0] + ex[1]]


def _ep_glu(accs, ex):
    a = accs[0] + ex[0]
    gt = accs[1] + ex[1]
    return [a * _sigmoid(gt), a, gt]


def _ep_swiglu(accs, ex):
    g, u = accs
    return [g, u, g * _sigmoid(g) * u]


def _ep_swiglu_bwd(accs, ex):
    dact = accs[0]
    g = ex[0].astype(F32)
    u = ex[1].astype(F32)
    sg = _sigmoid(g)
    return [dact * u * (sg * (1.0 + g * (1.0 - sg))), dact * g * sg]


def _ffn_fwd(tag, h, gain, get_gate_up, get_down):
    s, d = h.shape
    (n,) = _rms_fwd(f"{tag}_norm", h, gain)
    wg, wu = get_gate_up(n)
    f = wg.shape[-1]
    g, u, act = _mm(f"{tag}_gate_up", "nn", n, [(wg, None, 0), (wu, None, 0)], _ep_swiglu,
                    [(BF16,), (BF16,), (BF16,)], m=s, n=f, k=d)
    wd = get_down(act)
    (out,) = _mm(f"{tag}_down", "nn", act, [(wd, None, 0)], _ep_add, [(F32,)], m=s, n=d, k=f,
                 extras=[(h, "mn", 0)])
    return out, (n, g, u, act), dict(gate=wg, up=wu, down=wd)


def _ffn_bwd(tag, h_in, gain, wg, wu, wd, saved, dh, dhb, after, emit):
    s, d = h_in.shape
    f = wg.shape[-1]
    n, g, u, act = saved
    dg, du = _mm(f"{tag}_bwd_dact", "nt", dhb, [(wd, None, 0)], _ep_swiglu_bwd, [(BF16,), (BF16,)],
                 m=s, n=f, k=d, extras=[(g, "mn", 0), (u, "mn", 0)], after=after)
    (dwd,) = _mm(f"{tag}_bwd_dwd", "tn", act, [(dhb, None, 0)], _ep_id, [(BF16,)], m=f, n=d, k=s,
                 bm=f // 4)
    pin = emit("d", dict(down=dwd))
    dwg, dwu = _mm(f"{tag}_bwd_dwgu", "tn", n, [(dg, None, 0), (du, None, 0)], _ep_all, [(BF16,), (BF16,)],
                   m=d, n=f, k=s, after=pin)
    pin = emit("gu", dict(gate=dwg, up=dwu))
    (dn,) = _mm(f"{tag}_bwd_dn", "nt", [dg, du], [(wg, None, 0), (wu, None, 0)], _ep_sum, [(BF16,)],
                m=s, n=d, k=f, bm=512, bn=256, after=pin)
    dx, dxb, dgain, cs = _rms_bwd(f"{tag}_bwd_norm", h_in, gain, [dn], dh)
    return dx, dxb, dgain, cs


def _local_step(x, target, w, fetch, emit, prefetch=lambda group, after: (), after=()):
    s, d = x.shape
    nh = d // HEAD_DIM
    bq = BRANCHES[0][0] // BRANCHES[0][1]
    assert all(win // dil == bq for win, dil in BRANCHES)
    assert BRANCHES[0][1] == 1 and all(dil > 1 for _, dil in BRANCHES[1:])
    slopes = _alibi_slopes(nh)
    nd = d // 512 if d >= 512 else 1
    bn = d // nd

    (n1,) = _rms_fwd("a_norm", x, w["a_norm_g"], after=after)
    prefetch("rest", n1)
    w_conv1 = fetch("conv1", n1)["conv_w1"]
    glu, a, gt = _mm("conv_pw1_glu", "nn", n1, [(w_conv1, None, 0), (w_conv1, None, nd)], _ep_glu,
                     [(F32,), (BF16,), (BF16,)], m=s, n=d, k=d, bn=bn,
                     extras=[(w["conv_b1"], "n", 0), (w["conv_b1"], "n", nd)])
    c, sw = _dwconv_fwd("conv_dw_ln", glu, w["conv_dw"], w["conv_dw_b"], w["conv_ln_g"], w["conv_ln_b"])
    w_conv2 = fetch("conv2", sw)["conv_w2"]
    (h1,) = _mm("conv_pw2", "nn", sw, [(w_conv2, None, 0)], _ep_bias_res, [(F32,)], m=s, n=d, k=d,
                extras=[(w["conv_b2"], "n", 0), (x, "mn", 0)], bn=WIDE_BN)
    def gate_up0(after):
        wts = fetch("ffn0gu", after)
        return wts["gate"], wts["up"]

    h2, ffn0, wf0 = _ffn_fwd("ffn0", h1, w["ffn_norm_g"][0:1], gate_up0, lambda after: fetch("ffn0d", after)["down"])
    wa = fetch("attn", h2)
    kvn, qn = _rms_fwd("kvq_norm", h2, jnp.concatenate([w["kv_norm_g"], w["b_norm_g"]], axis=0))
    k, v = _mm("kv_proj", "nn", kvn, [(wa["w_k"], None, 0), (wa["w_v"], None, 0)], _ep_all, [(BF16,), (BF16,)],
               m=s, n=d, k=d, bn=WIDE_BN)
    (q,) = _mm("q_proj", "nn", qn, [(wa["w_q"], None, 0)], _ep_id, [(BF16,)], m=s, n=d, k=d, bn=WIDE_BN)
    att, attb, lse = _attn_fwd("attn_fwd", q, k, v, slopes, bq)
    pin = prefetch("ffn1", attb)
    (h3,) = _mm("o_proj", "nn", attb, [(wa["w_o"], None, 0)], _ep_add, [(F32,)], m=s, n=d, k=d,
                extras=[(h2, "mn", 0)], bn=WIDE_BN, after=pin)
    got1 = {}

    def gate_up1(after):
        got1.update(fetch("ffn1", after))
        return got1["gate"], got1["up"]

    h4, ffn1, wf1 = _ffn_fwd("ffn1", h3, w["ffn_norm_g"][1:2], gate_up1, lambda after: got1["down"])
    dh4, dh4b, d_final_g, loss_cols = _final_loss("final_loss", h4, w["final_norm_g"], target)

    g = {}
    ga = {}
    dh3, dh3b, dgain1, _ = _ffn_bwd("ffn1", h3, w["ffn_norm_g"][1:2], wf1["gate"], wf1["up"], wf1["down"], ffn1,
                                    dh4, dh4b, (), lambda part, grads: emit("ffn1" + part, grads))
    (datt,) = _mm("o_proj_bwd_dx", "nt", dh3b, [(wa["w_o"], None, 0)], _ep_id, [(BF16,)], m=s, n=d, k=d, bn=WIDE_BN)
    (ga["w_o"],) = _mm("o_proj_bwd_dw", "tn", attb, [(dh3b, None, 0)], _ep_id, [(BF16,)], m=d, n=d, k=s, bn=WIDE_BN)
    dq, dk, dv = _attn_bwd("attn_bwd", q, k, v, att, lse, datt, slopes, bq)
    (ga["w_q"],) = _mm("q_proj_bwd_dw", "tn", qn, [(dq, None, 0)], _ep_id, [(BF16,)], m=d, n=d, k=s, bn=WIDE_BN)
    ga["w_k"], ga["w_v"] = _mm("kv_proj_bwd_dw", "tn", kvn, [(dk, None, 0), (dv, None, 0)], _ep_all,
                               [(BF16,), (BF16,)], m=d, n=d, k=s)
    pin = emit("attn", ga)
    (dqn,) = _mm("q_proj_bwd_dx", "nt", dq, [(wa["w_q"], None, 0)], _ep_id, [(BF16,)], m=s, n=d, k=d, after=pin,
                 bn=WIDE_BN)
    (dkvn,) = _mm("kv_proj_bwd_dx", "nt", [dk, dv], [(wa["w_k"], None, 0), (wa["w_v"], None, 0)], _ep_sum, [(BF16,)],
                  m=s, n=d, k=d, bn=WIDE_BN)
    dh2, dh2b, dg_kvq, _ = _rms_bwd("kvq_norm_bwd", h2, jnp.concatenate([w["kv_norm_g"], w["b_norm_g"]], axis=0),
                                    [dkvn, dqn], dh3)
    dh1, dh1b, dgain0, cs_h1 = _ffn_bwd("ffn0", h1, w["ffn_norm_g"][0:1], wf0["gate"], wf0["up"], wf0["down"], ffn0,
                                        dh2, dh2b, (), lambda part, grads: emit("ffn0" + part, grads))
    (dsw,) = _mm("conv_pw2_bwd_dx", "nt", dh1b, [(w_conv2, None, 0)], _ep_id, [(F32,)], m=s, n=d, k=d, bn=WIDE_BN)
    (dw2,) = _mm("conv_pw2_bwd_dw", "tn", sw, [(dh1b, None, 0)], _ep_id, [(BF16,)], m=d, n=d, k=s, bn=WIDE_BN)
    pin = emit("conv2", dict(conv_w2=dw2))
    dc, ln_sums = _conv_ln_bwd("conv_ln_bwd", c, dsw, w["conv_ln_g"], w["conv_ln_b"], after=pin)
    dpre, ddw, db1 = _conv_dw_bwd("conv_dw_bwd", dc, glu, a, gt, w["conv_dw"])
    (dw1,) = _mm("conv_pw1_bwd_dw", "tn", n1, [(dpre, None, 0)], _ep_id, [(BF16,)], m=d, n=2 * d, k=s)
    pin = emit("conv1", dict(conv_w1=dw1))
    (dn1,) = _mm("conv_pw1_bwd_dx", "nt", dpre, [(w_conv1, None, 0)], _ep_id, [(BF16,)], m=s, n=d, k=2 * d,
                 after=pin)
    dx, _, d_a_norm, _ = _rms_bwd("a_norm_bwd", x, w["a_norm_g"], [dn1], dh1)

    g.update(
        a_norm_g=d_a_norm, conv_b1=db1, conv_dw=ddw, conv_dw_b=ln_sums[2:3], conv_ln_g=ln_sums[0:1],
        conv_ln_b=ln_sums[1:2], conv_b2=cs_h1, kv_norm_g=dg_kvq[0:1], b_norm_g=dg_kvq[1:2],
        ffn_norm_g=jnp.concatenate([dgain0, dgain1], axis=0), final_norm_g=d_final_g,
    )
    return loss_cols, dx, g


HBM_SPEC = pl.BlockSpec(memory_space=pltpu.HBM)


def _mesh_place():
    x, y, c = lax.axis_index("x"), lax.axis_index("y"), lax.axis_index("c")
    chips = [(1 - x, y), (x, 1 - y), (1 - x, 1 - y)]
    return x, y, c, chips


def _shard_view(ref, kind, s, half=None):
    rows, cols = ref.shape
    if kind == "col":
        cw = cols // N_CHIPS
        if half is None:
            return ref.at[pl.ds(0, rows), pl.ds(s * cw, cw)]
        return ref.at[pl.ds(half * (rows // 2), rows // 2), pl.ds(s * cw, cw)]
    r = rows // N_CHIPS
    if half is None:
        return ref.at[pl.ds(s * r, r), pl.ds(0, cols)]
    return ref.at[pl.ds(s * r + half * (r // 2), r // 2), pl.ds(0, cols)]


def _full_shape(shard, kind):
    r, cw = shard.shape[-2:]
    return (r, cw * N_CHIPS) if kind == "col" else (r * N_CHIPS, cw)


def _gather_weights(shards, kinds, after=()):
    nt = len(shards)
    nf = len(after)
    fulls = [jax.ShapeDtypeStruct(_full_shape(sh, kind), sh.dtype) for sh, kind in zip(shards, kinds)]

    def body(*refs):
        src = refs[:nt]
        dst = refs[nt + nf:2 * nt + nf]
        send, recv, fsend, frecv, local = refs[2 * nt + nf:]
        x, y, c, chips = _mesh_place()
        s = 2 * x + y
        sib = (x, y, 1 - c)

        def half_of_shard(t):
            r, cw = src[t].shape
            return src[t].at[pl.ds(c * (r // 2), r // 2), pl.ds(0, cw)]

        locals_ = [pltpu.make_async_copy(src[t], _shard_view(dst[t], kinds[t], s), local.at[t]) for t in range(nt)]
        for cp in locals_:
            cp.start()
        sends = []
        for t in range(nt):
            for j, chip in enumerate(chips):
                cp = pltpu.make_async_remote_copy(
                    src_ref=half_of_shard(t), dst_ref=_shard_view(dst[t], kinds[t], s, c),
                    send_sem=send.at[t, j], recv_sem=recv.at[t, j], device_id=(*chip, c), device_id_type=MESH)
                cp.start()
                sends.append(cp)
        for t in range(nt):
            for j, (px, py) in enumerate(chips):
                landed = _shard_view(dst[t], kinds[t], 2 * px + py, c)
                pltpu.make_async_remote_copy(
                    src_ref=half_of_shard(t), dst_ref=landed, send_sem=send.at[t, j], recv_sem=recv.at[t, j],
                    device_id=(px, py, c), device_id_type=MESH).wait_recv()
                cp = pltpu.make_async_remote_copy(
                    src_ref=landed, dst_ref=landed, send_sem=fsend.at[t, j], recv_sem=frecv.at[t, j],
                    device_id=sib, device_id_type=MESH)
                cp.start()
                sends.append(cp)
        for t in range(nt):
            for j, (px, py) in enumerate(chips):
                other = _shard_view(dst[t], kinds[t], 2 * px + py, 1 - c)
                pltpu.make_async_remote_copy(
                    src_ref=other, dst_ref=other, send_sem=fsend.at[t, j], recv_sem=frecv.at[t, j],
                    device_id=sib, device_id_type=MESH).wait_recv()
        for cp in sends:
            cp.wait_send()
        for cp in locals_:
            cp.wait()

    return pl.pallas_call(
        body,
        name="gather_weights",
        in_specs=[HBM_SPEC] * nt + [ANY_SPEC] * nf,
        out_specs=[HBM_SPEC] * nt,
        out_shape=fulls,
        scratch_shapes=[pltpu.SemaphoreType.DMA((nt, 3))] * 4 + [pltpu.SemaphoreType.DMA((nt,))],
    )(*shards, *after)


def _row_blocks(rows, want=512):
    nb = 1
    while rows // nb > want or rows % nb or (rows // nb) % 16:
        nb += 1
        if nb > rows:
            return rows, 1
    return rows // nb, nb


def _place_own(name, src, kind, land, ids, lead=0):
    r, cw = src.shape[-2:]
    tr, nb = _row_blocks(r)
    if src.ndim == 3:
        src_spec = pl.BlockSpec((None, tr, cw), lambda i, ids_ref: (lead, i, 0))
    else:
        src_spec = pl.BlockSpec((tr, cw), lambda i, ids_ref: (i, 0))
    if kind == "col":
        dst_spec = pl.BlockSpec((tr, cw), lambda i, ids_ref: (i, ids_ref[0]))
    else:
        dst_spec = pl.BlockSpec((tr, cw), lambda i, ids_ref: (ids_ref[0] * nb + i, 0))

    def body(ids_ref, s_ref, o_ref):
        o_ref[...] = s_ref[...].astype(o_ref.dtype)

    return pl.pallas_call(
        body,
        name=name,
        grid_spec=pltpu.PrefetchScalarGridSpec(num_scalar_prefetch=1, grid=(nb,), in_specs=[src_spec],
                                               out_specs=dst_spec),
        out_shape=land,
        compiler_params=pltpu.CompilerParams(dimension_semantics=("parallel",)),
    )(ids, src)


SEM_SPEC = pl.BlockSpec(memory_space=pltpu.SEMAPHORE)
SIDE_EFFECT = pltpu.SideEffectType.DATAFLOW_SIDE_EFFECTING


def _copies_start(name, srcs, lands, copies, n_sems, after):
    ns, nl, nf = len(srcs), len(lands), len(after)

    def body(*refs):
        src, land = refs[:ns], refs[ns:ns + nl]
        send, recv = refs[ns + nl + nf], refs[ns + nl + nf + 1]
        pin = refs[-1]
        for cp in copies(src, land, send, recv, _mesh_place()):
            cp.start()
        pin[...] = jnp.zeros_like(pin)

    arrs = list(srcs) + list(lands)
    res = pl.pallas_call(
        body,
        name=name,
        in_specs=[HBM_SPEC] * (ns + nl) + [ANY_SPEC] * nf,
        out_specs=[SEM_SPEC, SEM_SPEC] + [HBM_SPEC] * (ns + nl) + [pl.BlockSpec(memory_space=pltpu.VMEM)],
        out_shape=[pltpu.SemaphoreType.DMA((n_sems,)), pltpu.SemaphoreType.DMA((n_sems,))]
        + [pltpu.HBM(a.shape, a.dtype) for a in arrs] + [jax.ShapeDtypeStruct((8, 128), F32)],
        input_output_aliases={i: 2 + i for i in range(ns + nl)},
        compiler_params=pltpu.CompilerParams(has_side_effects=SIDE_EFFECT),
    )(*[pltpu.with_memory_space_constraint(a, pltpu.HBM) for a in arrs], *after)
    return res[0], res[1], list(res[2:2 + ns]), list(res[2 + ns:2 + ns + nl]), res[-1]


def _copies_wait(name, send, recv, srcs, lands, copies, after):
    ns, nl, nf = len(srcs), len(lands), len(after)

    def body(*refs):
        src, land = refs[:ns], refs[ns:ns + nl]
        send_sems, recv_sems = refs[ns + nl], refs[ns + nl + 1]
        cps = copies(src, land, send_sems, recv_sems, _mesh_place())
        for cp in cps:
            cp.wait_send()
        for cp in cps:
            cp.wait_recv()

    arrs = list(srcs) + list(lands)
    res = pl.pallas_call(
        body,
        name=name,
        in_specs=[HBM_SPEC] * (ns + nl) + [SEM_SPEC, SEM_SPEC] + [ANY_SPEC] * nf,
        out_specs=[HBM_SPEC] * (ns + nl),
        out_shape=[pltpu.HBM(a.shape, a.dtype) for a in arrs],
        input_output_aliases={i: i for i in range(ns + nl)},
        compiler_params=pltpu.CompilerParams(has_side_effects=SIDE_EFFECT),
    )(*arrs, send, recv, *after)
    return list(res[:ns]), list(res[ns:])


def _gather_copies(kinds, halves):
    def copies(src, land, send, recv, place):
        x, y, c, chips = place
        s = 2 * x + y
        mine = [_shard_view(land[t], kinds[t], s, c if halves else None) for t in range(len(kinds))]
        return [
            pltpu.make_async_remote_copy(
                src_ref=mine[t], dst_ref=mine[t], send_sem=send.at[3 * t + j], recv_sem=recv.at[3 * t + j],
                device_id=(px, py, c), device_id_type=MESH)
            for t in range(len(kinds)) for j, (px, py) in enumerate(chips)
        ]

    return copies


def _gather_pass_copies(kinds):
    def copies(src, land, send, recv, place):
        x, y, c, chips = place
        views = [_shard_view(land[t], kinds[t], 2 * px + py, c) for t in range(len(kinds)) for px, py in chips]
        return [
            pltpu.make_async_remote_copy(src_ref=v, dst_ref=v, send_sem=send.at[i], recv_sem=recv.at[i],
                                         device_id=(x, y, 1 - c), device_id_type=MESH)
            for i, v in enumerate(views)
        ]

    return copies


def _gather_pass_on(name, lands, kinds):
    nt = len(lands)

    def body(*refs):
        buf = refs[nt:2 * nt]
        send, recv = refs[2 * nt:]
        x, y, c, chips = _mesh_place()
        sib = (x, y, 1 - c)
        sends = []
        for t in range(nt):
            for j, (px, py) in enumerate(chips):
                mine = _shard_view(buf[t], kinds[t], 2 * px + py, c)
                cp = pltpu.make_async_remote_copy(src_ref=mine, dst_ref=mine, send_sem=send.at[t, j],
                                                  recv_sem=recv.at[t, j], device_id=sib, device_id_type=MESH)
                cp.start()
                sends.append(cp)
        for t in range(nt):
            for j, (px, py) in enumerate(chips):
                theirs = _shard_view(buf[t], kinds[t], 2 * px + py, 1 - c)
                pltpu.make_async_remote_copy(src_ref=theirs, dst_ref=theirs, send_sem=send.at[t, j],
                                             recv_sem=recv.at[t, j], device_id=sib, device_id_type=MESH).wait_recv()
        for cp in sends:
            cp.wait_send()

    return pl.pallas_call(
        body,
        name=name,
        in_specs=[HBM_SPEC] * nt,
        out_specs=[HBM_SPEC] * nt,
        out_shape=[jax.ShapeDtypeStruct(a.shape, a.dtype) for a in lands],
        input_output_aliases={i: i for i in range(nt)},
        scratch_shapes=[pltpu.SemaphoreType.DMA((nt, 3))] * 2,
    )(*lands)


def _grad_part(ref, kind, s):
    return ref if kind == "all" else _shard_view(ref, kind, s)


def _grad_copies(kinds):
    def peers(place):
        x, y, c, chips = place
        return [(x, y, 1 - c)] + [(px, py, c) for px, py in chips]

    def copies(src, land, send, recv, place):
        x, y, c, chips = place
        me = 4 * x + 2 * y + c
        return [
            pltpu.make_async_remote_copy(
                src_ref=_grad_part(src[t], kinds[t], 2 * px + py), dst_ref=land[t].at[me],
                send_sem=send.at[GRAD_PEERS * t + k], recv_sem=recv.at[GRAD_PEERS * t + k], device_id=(px, py, pc),
                device_id_type=MESH)
            for t in range(len(kinds)) for k, (px, py, pc) in enumerate(peers(place))
        ]

    return copies


def _pass_copies(n):
    def copies(src, land, send, recv, place):
        x, y, c, chips = place
        return [
            pltpu.make_async_remote_copy(
                src_ref=land[t].at[4 * px + 2 * py + c], dst_ref=land[t].at[4 * px + 2 * py + c],
                send_sem=send.at[3 * t + j], recv_sem=recv.at[3 * t + j], device_id=(x, y, 1 - c),
                device_id_type=MESH)
            for t in range(n) for j, (px, py) in enumerate(chips)
        ]

    return copies


GRAD_PEERS = 4


def _land_shape(grad, kind):
    rows, cols = grad.shape
    if kind == "col":
        return (N_DEV, rows, cols // N_CHIPS)
    if kind == "row":
        return (N_DEV, rows // N_CHIPS, cols)
    return (N_DEV, rows, cols)


def _adamw_reduce(name, contrib, own, kind, ids, w, m, v, layer=None, prev=None):
    rows, cols = w.shape[-2:]
    t = rows
    for cand in (256, 128):
        if rows % cand == 0 and cand * cols <= 256 * 1408:
            t = cand
            break
    nb = rows // t
    c1 = 1.0 - ADAM_B1 ** ADAM_STEP
    c2 = 1.0 - ADAM_B2 ** ADAM_STEP

    n_prev = 0 if prev is None else 4

    def body(ids_ref, c_ref, own_ref, w_ref, m_ref, v_ref, *refs):
        g_ref, d_ref, nm_ref, nv_ref = refs[n_prev:]
        me = ids_ref[1]
        mine = own_ref[...].astype(F32)
        g = None
        for q in range(N_DEV):
            term = jnp.where(me == q, mine, c_ref[q].astype(F32))
            g = term if g is None else g + term
        nm = ADAM_B1 * m_ref[...] + (1.0 - ADAM_B1) * g
        nv = ADAM_B2 * v_ref[...] + (1.0 - ADAM_B2) * (g * g)
        g_ref[...] = g
        nm_ref[...] = nm
        nv_ref[...] = nv
        d_ref[...] = -ADAM_LR * ((nm / c1) / (jnp.sqrt(nv / c2) + ADAM_EPS) + ADAM_WD * w_ref[...])

    if layer is None:
        blk = pl.BlockSpec((t, cols), lambda i, ids_ref: (i, 0))
    else:
        blk = pl.BlockSpec((None, t, cols), lambda i, ids_ref: (layer, i, 0))
    if kind == "col":
        own_spec = pl.BlockSpec((t, cols), lambda i, ids_ref: (i, ids_ref[0]))
    elif kind == "row":
        own_spec = pl.BlockSpec((t, cols), lambda i, ids_ref: (ids_ref[0] * nb + i, 0))
    else:
        own_spec = pl.BlockSpec((t, cols), lambda i, ids_ref: (i, 0))
    return pl.pallas_call(
        body,
        name=name,
        grid_spec=pltpu.PrefetchScalarGridSpec(
            num_scalar_prefetch=1,
            grid=(nb,),
            in_specs=[pl.BlockSpec((N_DEV, t, cols), lambda i, ids_ref: (0, i, 0)), own_spec, blk, blk, blk]
            + [ANY_SPEC] * n_prev,
            out_specs=[blk] * 4,
        ),
        out_shape=[jax.ShapeDtypeStruct(w.shape, F32)] * 4,
        input_output_aliases={6 + i: i for i in range(n_prev)},
        compiler_params=pltpu.CompilerParams(dimension_semantics=("parallel",)),
    )(ids, contrib, own, w, m, v, *(prev or ()))


WEIGHT_NAMES = ("a_norm_g", "conv_w1", "conv_b1", "conv_dw", "conv_dw_b", "conv_ln_g", "conv_ln_b", "conv_w2",
                "conv_b2", "kv_norm_g", "w_k", "w_v", "b_norm_g", "w_q", "w_o", "ffn_norm_g", "ffn_w_gate",
                "ffn_w_up", "ffn_w_down", "final_norm_g")
GROUPS = {
    "conv2": (("conv_w2", "conv_w2", None, "row"),),
    "ffn0": (("gate", "ffn_w_gate", 0, "col"), ("up", "ffn_w_up", 0, "col"), ("down", "ffn_w_down", 0, "row")),
    "attn": (("w_k", "w_k", None, "row"), ("w_v", "w_v", None, "row"), ("w_q", "w_q", None, "row"),
             ("w_o", "w_o", None, "row")),
    "ffn1": (("gate", "ffn_w_gate", 1, "col"), ("up", "ffn_w_up", 1, "col"), ("down", "ffn_w_down", 1, "row")),
}
GROUPS["conv1"] = (("conv_w1", "conv_w1", None, "col"),)
for _layer in (0, 1):
    GROUPS[f"ffn{_layer}d"] = tuple(it for it in GROUPS[f"ffn{_layer}"] if it[0] == "down")
    GROUPS[f"ffn{_layer}gu"] = tuple(it for it in GROUPS[f"ffn{_layer}"] if it[0] != "down")
FETCH_ORDER = ("conv1", "conv2", "ffn0gu", "ffn0d", "attn", "ffn1")
HALVED = ("conv1", "conv2", "ffn0gu", "ffn0d", "attn", "ffn1")
EMIT_ORDER = ("ffn1d", "ffn1gu", "attn", "ffn0d", "ffn0gu", "conv2", "conv1", "vec")
RETIRE_AT = {"attn": ("ffn1d", "ffn1gu"), "ffn0d": ("attn",), "conv1": ("ffn0d", "ffn0gu", "conv2")}
PACKED = (("a_norm_g", 0, 1), ("conv_b1", 8, 2), ("conv_dw", 16, CONV_WIDTH), ("conv_dw_b", 48, 1),
          ("conv_ln_g", 56, 1), ("conv_ln_b", 64, 1), ("conv_b2", 72, 1))
PACK_ROWS = 80
WHOLE = (("kv_norm_g", 0, 1), ("b_norm_g", 1, 1), ("ffn_norm_g", 2, 2), ("final_norm_g", 4, 1))
WHOLE_ROWS = 8


def _pack_rows(parts, total, width):
    out, at = [], 0
    for arr, first in parts:
        if first > at:
            out.append(jnp.zeros((first - at, width), F32))
        rows8 = -(-arr.shape[0] // 8) * 8
        out.append(jnp.pad(arr, ((0, rows8 - arr.shape[0]), (0, 0))))
        at = first + rows8
    if total > at:
        out.append(jnp.zeros((total - at, width), F32))
    return jnp.concatenate(out, axis=0)


def kernel(x, a_norm_g, conv_w1, conv_b1, conv_dw, conv_dw_b, conv_ln_g, conv_ln_b, conv_w2, conv_b2, kv_norm_g, w_k, w_v, b_norm_g, w_q, w_o, ffn_norm_g, ffn_w_gate, ffn_w_up, ffn_w_down, final_norm_g, loss_target, m_a_norm_g, m_conv_w1, m_conv_b1, m_conv_dw, m_conv_dw_b, m_conv_ln_g, m_conv_ln_b, m_conv_w2, m_conv_b2, m_kv_norm_g, m_w_k, m_w_v, m_b_norm_g, m_w_q, m_w_o, m_ffn_norm_g, m_ffn_w_gate, m_ffn_w_up, m_ffn_w_down, m_final_norm_g, v_a_norm_g, v_conv_w1, v_conv_b1, v_conv_dw, v_conv_dw_b, v_conv_ln_g, v_conv_ln_b, v_conv_w2, v_conv_b2, v_kv_norm_g, v_w_k, v_w_v, v_b_norm_g, v_w_q, v_w_o, v_ffn_norm_g, v_ffn_w_gate, v_ffn_w_up, v_ffn_w_down, v_final_norm_g):
    args = locals()
    wts = {n: args[n] for n in WEIGHT_NAMES}
    mom = {n: args["m_" + n] for n in WEIGHT_NAMES}
    vel = {n: args["v_" + n] for n in WEIGHT_NAMES}
    s, d = x.shape[-2:]
    dq = d // N_CHIPS
    x2 = x.reshape(s, d)
    tgt = loss_target.reshape(s, d)

    def pack_shard(src):
        return _pack_rows([(src[n].reshape(-1, dq), first) for n, first, _ in PACKED], PACK_ROWS, dq)

    def pack_whole(src):
        return _pack_rows([(jnp.concatenate([src[n].reshape(-1, d) for n, _, _ in WHOLE], axis=0), 0)], WHOLE_ROWS, d)

    gathers = {}
    pins = []
    ids = jnp.stack([2 * lax.axis_index("x") + lax.axis_index("y"),
                     4 * lax.axis_index("x") + 2 * lax.axis_index("y") + lax.axis_index("c")]).astype(jnp.int32)

    def start_gather(grp, pins):
        kinds = [kind for _, _, _, kind in GROUPS[grp]]
        lands = []
        for key, n, layer, kind in GROUPS[grp]:
            shard = wts[n] if layer is not None else wts[n].reshape(wts[n].shape[-2:])
            full = jax.ShapeDtypeStruct(_full_shape(shard, kind), BF16)
            lands.append(_place_own(f"gather_place_{grp}_{key}", shard, kind, full, ids, layer))
        copies = _gather_copies(kinds, grp in HALVED)
        send, recv, _, lands, pin = _copies_start("gather_start_" + grp, [], lands, copies, 3 * len(lands), pins)
        gathers[grp] = (send, recv, [], lands, copies, kinds)
        return [pin]

    (packed_full,) = _gather_weights([pack_shard(wts)], ["row"])
    w = {}
    pins = start_gather(FETCH_ORDER[0], [packed_full])
    started = {}
    swapping = {}

    def prefetch(grp, after):
        if grp == "rest":
            last = [after]
            for later in FETCH_ORDER[1:]:
                last = start_gather(later, last)
            started["pin"] = last
            return ()
        send, recv, srcs, lands, copies, kinds = gathers[grp]
        _, fulls = _copies_wait("gather_wait_" + grp, send, recv, srcs, lands, copies, [after])
        copies = _gather_pass_copies(kinds)
        send, recv, _, fulls, pin = _copies_start("gather_swap_" + grp, [], fulls, copies, 3 * len(fulls), ())
        swapping[grp] = (send, recv, fulls, copies)
        return [pin]

    def fetch(grp, after):
        if grp in swapping:
            send, recv, fulls, copies = swapping[grp]
            _, fulls = _copies_wait("gather_swapped_" + grp, send, recv, [], fulls, copies, [after])
        else:
            send, recv, srcs, lands, copies, kinds = gathers[grp]
            behind = [after] + (started.get("pin", []) if grp == FETCH_ORDER[0] else [])
            _, fulls = _copies_wait("gather_wait_" + grp, send, recv, srcs, lands, copies, behind)
            if grp in HALVED:
                fulls = _gather_pass_on("gather_pass_" + grp, fulls, kinds)
        return {key: full for (key, _, _, _), full in zip(GROUPS[grp], fulls)}

    packed = packed_full.reshape(N_CHIPS, PACK_ROWS, dq)
    for n, first, rows in PACKED:
        part = packed[:, first:first + rows, :]
        if n == "conv_dw":
            w[n] = part.transpose(1, 0, 2).reshape(rows, d)
        else:
            w[n] = part.reshape(1, N_CHIPS * rows * dq)
    for n, _, rows in WHOLE:
        w[n] = wts[n].reshape(rows, d)

    exchanges = {}
    passing = {}
    own_grads = {}

    def retire(tag, after):
        send, recv, srcs, lands, copies, kinds = exchanges[tag]
        srcs, lands = _copies_wait("grads_wait_" + tag, send, recv, srcs, lands, copies, after)
        own_grads[tag] = list(zip(srcs, kinds))
        copies = _pass_copies(len(lands))
        send, recv, _, lands, pin = _copies_start("grads_pass_" + tag, [], lands, copies, 3 * len(lands), ())
        passing[tag] = (send, recv, lands, copies)
        return pin

    def emit_grads(tag, grads, kinds):
        pins = [retire(old, [grads[0]]) for old in RETIRE_AT.get(tag, ())]
        copies = _grad_copies(kinds)
        lands = [lax.empty(_land_shape(gr, kd), gr.dtype) for gr, kd in zip(grads, kinds)]
        send, recv, srcs, lands, pin = _copies_start("grads_start_" + tag, grads, lands, copies,
                                                     GRAD_PEERS * len(grads), pins)
        exchanges[tag] = (send, recv, srcs, lands, copies, kinds)
        return [pin]

    def emit(grp, grads):
        return emit_grads(grp, [grads[key] for key, _, _, _ in GROUPS[grp]], [kind for _, _, _, kind in GROUPS[grp]])

    loss_cols, dx, g = _local_step(x2, tgt, w, fetch, emit, prefetch, after=pins)
    loss = lax.psum(jnp.sum(loss_cols), ("x", "y", "c"))

    gp = []
    for n, first, rows in PACKED:
        if n == "conv_dw":
            part = g[n].reshape(rows, N_CHIPS, dq).transpose(1, 0, 2)
        else:
            part = g[n].reshape(N_CHIPS, rows, dq)
        gp.append((part, first))
    g_packed = jnp.concatenate(
        [_pack_rows([(p[ci], first) for p, first in gp], PACK_ROWS, dq) for ci in range(N_CHIPS)], axis=0)
    emit_grads("vec", [g_packed, pack_whole(g)], ["row", "all"])
    late = [tag for tag in EMIT_ORDER if tag not in passing]
    contribs = {}

    def arrive(tag, after):
        send, recv, lands, copies = passing[tag]
        _, arrived = _copies_wait("grads_passed_" + tag, send, recv, [], lands, copies, after)
        contribs[tag] = [(c, own, kind) for c, (own, kind) in zip(arrived, own_grads[tag])]

    for tag in EMIT_ORDER:
        if tag not in late:
            arrive(tag, [dx])
    res = {}

    def adamw(n, contrib, layer=None, prev=None):
        arrived, own, kind = contrib
        if layer is None:
            shape = wts[n].shape
            r2 = shape[-2:]
            outs = _adamw_reduce("adamw_" + n, arrived, own, kind, ids, wts[n].reshape(r2), mom[n].reshape(r2),
                                 vel[n].reshape(r2))
            return [o.reshape(shape) for o in outs]
        return _adamw_reduce(f"adamw_{n}_{layer}", arrived, own, kind, ids, wts[n], mom[n], vel[n], layer, prev)

    for grp in ("attn", "conv2"):
        for (key, n, _, _), contrib in zip(GROUPS[grp], contribs[grp]):
            res[n] = adamw(n, contrib)
    for tag in late:
        retire(tag, [res["conv_w2"][0]])
    for part in ("d", "gu"):
        for (key, n, _, _), c0, c1 in zip(GROUPS["ffn0" + part], contribs["ffn0" + part], contribs["ffn1" + part]):
            res[n] = adamw(n, c1, 1, adamw(n, c0, 0))
    for tag in late:
        arrive(tag, [res["ffn_w_up"][0]])
    for (key, n, _, _), contrib in zip(GROUPS["conv1"], contribs["conv1"]):
        res[n] = adamw(n, contrib)
    outs = _adamw_reduce("adamw_packed", *contribs["vec"][0], ids, pack_shard(wts), pack_shard(mom), pack_shard(vel))
    for n, first, rows in PACKED:
        res[n] = [o[first:first + rows].reshape(wts[n].shape) for o in outs]
    outs = _adamw_reduce("adamw_whole", *contribs["vec"][1], ids, pack_whole(wts), pack_whole(mom), pack_whole(vel))
    for n, first, rows in WHOLE:
        res[n] = [o[first:first + rows].reshape(wts[n].shape) for o in outs]

    out = [loss, dx.reshape(x.shape)]
    for which in range(4):
        out += [res[n][which] for n in WEIGHT_NAMES]
    return tuple(out)
```
